```python
import jax, jax.numpy as jnp
from jax import lax
import numpy as np

D_MODEL = 1024
BATCH = 16
SEQ = 256
DEPTH = 2
DEC_BATCH = 4
DEC_SEQ = 1024
PAST_LEN = 256

GRID_W = 64
HEAD_DIM = 64
N_HEADS = D_MODEL // 128
N_KV_HEADS = N_HEADS // 4
KV_GROUP = N_HEADS // N_KV_HEADS
ATTN_W = N_HEADS * HEAD_DIM
KV_W = N_KV_HEADS * HEAD_DIM
Q_BLOCK = 128
ROPE_BASE = 10000.0
D_RNN = D_MODEL // 2
RG_BLOCKS = 8
RG_BW = D_RNN // RG_BLOCKS
RG_C = 8.0
CONV_W = 4
CONV_PAD_LO = 1
HG_HEADS = 8
HG_DK = 64
HG_DV = 64
HG_W = HG_HEADS * HG_DK
HG_CHUNK = 32
N_BRANCH = 3
N_EXPERTS = 32
TOP_K = 4
D_FF = D_MODEL
SWIGLU_LIMIT = 7.0
SWIGLU_ALPHA = 1.702
N_MOD = 6
EPS = 1e-6
IN_SPLITS = (ATTN_W, KV_W, KV_W, D_RNN, D_RNN, HG_W, HG_W, HG_W, HG_W, HG_W, N_BRANCH * D_MODEL)
IN_W = ATTN_W + 2 * KV_W + 2 * D_RNN + 5 * HG_W + N_BRANCH * D_MODEL

kernel_name = 'hybrid_prefix_diffusion_step'


def rmsnorm(x, g):
    xf = x.astype(jnp.float32)
    y = xf * lax.rsqrt(jnp.mean(xf * xf, axis=-1, keepdims=True) + EPS)
    return y.astype(x.dtype) * g


def axial_rope_tables(rows):
    row = jnp.repeat(jnp.arange(rows, dtype=jnp.float32), GRID_W)
    col = jnp.tile(jnp.arange(GRID_W, dtype=jnp.float32), rows)
    quarter = HEAD_DIM // 4
    inv_freq = ROPE_BASE ** (-jnp.arange(quarter, dtype=jnp.float32) / quarter)
    ang_r = row[:, None] * inv_freq
    ang_c = col[:, None] * inv_freq
    ang = jnp.concatenate([ang_r, ang_r, ang_c, ang_c], axis=-1)
    return jnp.cos(ang), jnp.sin(ang)


def apply_axial_rope(x, cos, sin):
    x1, x2, x3, x4 = jnp.split(x, 4, axis=-1)
    rot = jnp.concatenate([-x2, x1, -x4, x3], axis=-1)
    return x * cos[None, :, None, :] + rot * sin[None, :, None, :]


def gqa_block(q, k, v):
    B, Tq = q.shape[0], q.shape[1]
    qg = q.reshape(B, Tq, N_KV_HEADS, KV_GROUP, HEAD_DIM)
    s = jnp.einsum('bqkgd,bskd->bkgqs', qg, k).astype(jnp.float32) * (HEAD_DIM ** -0.5)
    pr = jax.nn.softmax(s, axis=-1).astype(v.dtype)
    o = jnp.einsum('bkgqs,bskd->bqkgd', pr, v)
    return o.reshape(B, Tq, ATTN_W)


def attention(q, k, v):
    B, T = q.shape[0], q.shape[1]
    nb = T // Q_BLOCK
    qb = jnp.moveaxis(q.reshape(B, nb, Q_BLOCK, N_HEADS, HEAD_DIM), 1, 0)
    ob = lax.map(lambda qi: gqa_block(qi, k, v), qb)
    return jnp.moveaxis(ob, 0, 1).reshape(B, T, ATTN_W)


def centred_dwconv(x, w, b):
    T = x.shape[1]
    xp = jnp.pad(x, ((0, 0), (CONV_PAD_LO, CONV_W - 1 - CONV_PAD_LO), (0, 0)))
    y = b
    for j in range(CONV_W):
        y = y + xp[:, j:j + T] * w[j]
    return y


def linear_scan(a, u, h0):
    def combine(lhs, rhs):
        return lhs[0] * rhs[0], rhs[0] * lhs[1] + rhs[1]
    a_cum, u_cum = lax.associative_scan(combine, (a, u), axis=1)
    return a_cum * h0[:, None] + u_cum


def rglru_direction(x, wa, ba, wx, bx, lam, h0, reverse):
    B, T = x.shape[0], x.shape[1]
    xb = x.reshape(B, T, RG_BLOCKS, RG_BW)
    r = jax.nn.sigmoid(jnp.einsum('btnd,nde->btne', xb, wa).reshape(B, T, D_RNN) + ba)
    i = jax.nn.sigmoid(jnp.einsum('btnd,nde->btne', xb, wx).reshape(B, T, D_RNN) + bx)
    log_a = -RG_C * r * jax.nn.softplus(-lam)
    a = jnp.exp(log_a)
    u = jnp.sqrt(-jnp.expm1(2.0 * log_a)) * (i * x)
    if reverse:
        a, u = a[:, ::-1], u[:, ::-1]
    h = linear_scan(a, u, h0)
    h_last = h[:, -1]
    if reverse:
        h = h[:, ::-1]
    return h, h_last


def hgrn2_chunked(q, k, v, g, s0):
    B, T, H, DK = q.shape
    DV = v.shape[-1]
    N = T // HG_CHUNK
    q, k, g = [t.reshape(B, N, HG_CHUNK, H, DK) for t in (q, k, g)]
    v = v.reshape(B, N, HG_CHUNK, H, DV)
    G = jnp.cumsum(g, axis=2)
    G_last = G[:, :, -1]
    lower = jnp.tril(jnp.ones((HG_CHUNK, HG_CHUNK), dtype=bool))
    diff = G[:, :, :, None] - G[:, :, None, :]
    decay = jnp.exp(jnp.where(lower[:, :, None, None], diff, -jnp.inf))
    scores = jnp.einsum('bnthd,bnshd,bntshd->bnhts', q, k, decay)
    o_intra = jnp.einsum('bnhts,bnshv->bnthv', scores, v)
    k_to_end = k * jnp.exp(G_last[:, :, None] - G)
    dS = jnp.einsum('bnshd,bnshv->bnhdv', k_to_end, v)

    def step(S, inp):
        dS_c, gl_c = inp
        return jnp.exp(gl_c)[..., None] * S + dS_c, S

    s_last, s_start = lax.scan(step, s0, (jnp.moveaxis(dS, 1, 0), jnp.moveaxis(G_last, 1, 0)))
    s_start = jnp.moveaxis(s_start, 0, 1)
    o_inter = jnp.einsum('bnthd,bnhdv->bnthv', q * jnp.exp(G), s_start)
    return (o_intra + o_inter).reshape(B, T, H, DV), s_last


def hgrn2_direction(q, f_logit, i, lb, s0, reverse):
    B, T = q.shape[0], q.shape[1]
    f_logit = f_logit.astype(jnp.float32)
    log_f = jnp.logaddexp(jnp.log(lb), jnp.log1p(-lb) + jax.nn.log_sigmoid(f_logit))
    k = (1.0 - lb) * jax.nn.sigmoid(-f_logit)
    heads = lambda t: t.reshape(B, T, HG_HEADS, HG_DK)
    qh, kh, vh, gh = heads(q), heads(k), heads(i.astype(jnp.float32)), heads(log_f)
    if reverse:
        qh, kh, vh, gh = qh[:, ::-1], kh[:, ::-1], vh[:, ::-1], gh[:, ::-1]
    o, s_last = hgrn2_chunked(qh, kh, vh, gh, s0.astype(jnp.float32))
    if reverse:
        o = o[:, ::-1]
    return o, s_last


def token_mixer(h, p, lb, rope, ctx):
    B, T = h.shape[0], h.shape[1]
    points = np.cumsum(np.array(IN_SPLITS))[:-1].tolist()
    z = h @ p['w_in']
    q, k, v, rx, ry, hq, hf_fwd, hf_bwd, hi, hog, gl = jnp.split(z, points, axis=-1)
    q = rmsnorm(q.reshape(B, T, N_HEADS, HEAD_DIM), p['q_norm_g'])
    k = rmsnorm(k.reshape(B, T, N_KV_HEADS, HEAD_DIM), p['k_norm_g'])
    v = v.reshape(B, T, N_KV_HEADS, HEAD_DIM)
    if rope is not None:
        cos, sin = rope
        q = apply_axial_rope(q, cos, sin)
        k = apply_axial_rope(k, cos, sin)
    if ctx is None:
        k_all, v_all = k, v
        hr0 = jnp.zeros((B, 2, D_RNN), jnp.float32)
        s0 = jnp.zeros((B, 2, HG_HEADS, HG_DK, HG_DV), jnp.float32)
    else:
        k_ctx, v_ctx, hr0, s0 = ctx
        k_all = jnp.concatenate([k, k_ctx.astype(k.dtype)], axis=1)
        v_all = jnp.concatenate([v, v_ctx.astype(v.dtype)], axis=1)
        hr0 = hr0.astype(jnp.float32)
    attn = attention(q, k_all, v_all)
    xr = centred_dwconv(rx, p['conv_w'], p['conv_b']).astype(jnp.float32)
    h_f, hf_last = rglru_direction(xr, p['rg_wa'][0], p['rg_ba'][0], p['rg_wx'][0], p['rg_bx'][0],
                                   p['rg_lambda'][0], hr0[:, 0], False)
    h_b, hb_last = rglru_direction(xr, p['rg_wa'][1], p['rg_ba'][1], p['rg_wx'][1], p['rg_bx'][1],
                                   p['rg_lambda'][1], hr0[:, 1], True)
    rnn = (jax.nn.gelu(ry.astype(jnp.float32)) * (h_f + h_b)).astype(h.dtype)
    q_h = jax.nn.silu(hq.astype(jnp.float32))
    o_f, s_f = hgrn2_direction(q_h, hf_fwd, hi, lb[0], s0[:, 0], False)
    o_b, s_b = hgrn2_direction(q_h, hf_bwd, hi, lb[1], s0[:, 1], True)
    o_h = rmsnorm(o_f + o_b, p['hgrn_norm_g']).reshape(B, T, HG_W) * jax.nn.silu(hog.astype(jnp.float32))
    gates = jax.nn.sigmoid(gl).reshape(B, T, N_BRANCH, D_MODEL)
    merged = (gates[:, :, 0] * (attn @ p['w_attn_o'])
              + gates[:, :, 1] * (rnn @ p['w_rnn_o'])
              + gates[:, :, 2] * (o_h.astype(h.dtype) @ p['w_hgrn_o']))
    out = merged @ p['w_out']
    if ctx is None:
        return out, (k, v, jnp.stack([hf_last, hb_last], axis=1), jnp.stack([s_f, s_b], axis=1))
    return out, None


def moe(h, p):
    logits = (h @ p['w_router'] + p['b_router']).astype(jnp.float32)
    top_v, top_i = lax.top_k(logits, TOP_K)
    top_w = jax.nn.softmax(top_v, axis=-1)
    comb = jnp.einsum('btk,btke->bte', top_w, jax.nn.one_hot(top_i, N_EXPERTS, dtype=jnp.float32))
    gu = jnp.einsum('btd,edf->btef', h, p['w_gu']) + p['b_gu']
    gate, up = jnp.split(gu, 2, axis=-1)
    gate = jnp.minimum(gate, SWIGLU_LIMIT)
    up = jnp.clip(up, -SWIGLU_LIMIT, SWIGLU_LIMIT)
    act = gate * jax.nn.sigmoid(SWIGLU_ALPHA * gate) * (up + 1.0)
    act = act * comb[..., None].astype(act.dtype)
    return jnp.einsum('btef,efd->btd', act, p['w_down']) + comb.astype(h.dtype) @ p['b_down']


def trunk_layer(x, cond, p, lb, rope, ctx):
    mod = jax.nn.silu(cond) @ p['w_mod'] + p['b_mod']
    shift1, scale1, gate1, shift2, scale2, gate2 = jnp.split(mod, N_MOD, axis=-1)
    h = rmsnorm(x, p['norm1_g']) * (1.0 + scale1) + shift1
    mix, new_ctx = token_mixer(h, p, lb, rope, ctx)
    x = x + gate1 * mix
    h = rmsnorm(x, p['norm2_g']) * (1.0 + scale2) + shift2
    x = x + gate2 * moe(h, p)
    return x, new_ctx


def setup_inputs(seed: int = 0) -> dict:
    key = jax.random.key(seed)
    ks = jax.random.split(key, 34)
    f32 = jnp.float32

    def nrm(i, shape, scale):
        return scale * jax.random.normal(ks[i], shape, f32)

    a0 = jax.random.uniform(ks[21], (DEPTH, 2, D_RNN), f32, 0.9, 0.999)
    return {
        'x_prompt': nrm(0, (BATCH, SEQ, D_MODEL), 1.0),
        'x_sample': nrm(1, (DEC_BATCH, DEC_SEQ, D_MODEL), 1.0),
        'c': nrm(2, (DEC_BATCH, D_MODEL), 1.0),
        'cache_k': nrm(3, (DEC_BATCH, DEPTH, PAST_LEN, N_KV_HEADS, HEAD_DIM), 1.0),
        'cache_v': nrm(4, (DEC_BATCH, DEPTH, PAST_LEN, N_KV_HEADS, HEAD_DIM), 1.0),
        'state_rglru': nrm(5, (DEC_BATCH, DEPTH, 2, D_RNN), 0.5),
        'state_hgrn': nrm(6, (DEC_BATCH, DEPTH, 2, HG_HEADS, HG_DK, HG_DV), 0.5),
        'c_ctx': nrm(7, (D_MODEL,), 1.0),
        'w_mod': nrm(8, (DEPTH, D_MODEL, N_MOD * D_MODEL), D_MODEL ** -0.5),
        'b_mod': nrm(9, (DEPTH, N_MOD * D_MODEL), 0.02),
        'norm1_g': 1.0 + nrm(10, (DEPTH, D_MODEL), 0.05),
        'norm2_g': 1.0 + nrm(11, (DEPTH, D_MODEL), 0.05),
        'w_in': nrm(12, (DEPTH, D_MODEL, IN_W), D_MODEL ** -0.5),
        'q_norm_g': 1.0 + nrm(13, (DEPTH, HEAD_DIM), 0.05),
        'k_norm_g': 1.0 + nrm(14, (DEPTH, HEAD_DIM), 0.05),
        'conv_w': nrm(15, (DEPTH, CONV_W, D_RNN), CONV_W ** -0.5),
        'conv_b': nrm(16, (DEPTH, D_RNN), 0.02),
        'rg_wa': nrm(17, (DEPTH, 2, RG_BLOCKS, RG_BW, RG_BW), RG_BW ** -0.5),
        'rg_ba': nrm(18, (DEPTH, 2, D_RNN), 0.02),
        'rg_wx': nrm(19, (DEPTH, 2, RG_BLOCKS, RG_BW, RG_BW), RG_BW ** -0.5),
        'rg_bx': nrm(20, (DEPTH, 2, D_RNN), 0.02),
        'rg_lambda': jnp.log(a0) - jnp.log1p(-a0),
        'hgrn_lb_logits': nrm(22, (DEPTH, 2, HG_W), 1.0),
        'hgrn_norm_g': 1.0 + nrm(23, (DEPTH, HG_DV), 0.05),
        'w_attn_o': nrm(24, (DEPTH, ATTN_W, D_MODEL), ATTN_W ** -0.5),
        'w_rnn_o': nrm(25, (DEPTH, D_RNN, D_MODEL), D_RNN ** -0.5),
        'w_hgrn_o': nrm(26, (DEPTH, HG_W, D_MODEL), HG_W ** -0.5),
        'w_out': nrm(27, (DEPTH, D_MODEL, D_MODEL), D_MODEL ** -0.5),
        'w_router': nrm(28, (DEPTH, D_MODEL, N_EXPERTS), D_MODEL ** -0.5),
        'b_router': nrm(29, (DEPTH, N_EXPERTS), 0.01),
        'w_gu': nrm(30, (DEPTH, N_EXPERTS, D_MODEL, 2 * D_FF), D_MODEL ** -0.5),
        'b_gu': nrm(31, (DEPTH, N_EXPERTS, 2 * D_FF), 0.02),
        'w_down': nrm(32, (DEPTH, N_EXPERTS, D_FF, D_MODEL), D_FF ** -0.5),
        'b_down': nrm(33, (DEPTH, N_EXPERTS, D_MODEL), 0.02),
    }


def reference(x_prompt, x_sample, c, cache_k, cache_v, state_rglru, state_hgrn, c_ctx,
              w_mod, b_mod, norm1_g, norm2_g, w_in, q_norm_g, k_norm_g, conv_w, conv_b,
              rg_wa, rg_ba, rg_wx, rg_bx, rg_lambda, hgrn_lb_logits, hgrn_norm_g,
              w_attn_o, w_rnn_o, w_hgrn_o, w_out, w_router, b_router, w_gu, b_gu, w_down, b_down):
    lb_all = jnp.cumsum(jax.nn.softmax(hgrn_lb_logits.astype(jnp.float32), axis=0), axis=0)
    lb_all = lb_all - lb_all[:1]
    rows = x_sample.shape[1] // GRID_W
    rope = axial_rope_tables(rows)
    cond_ctx = c_ctx[None, None, :]
    cond_lat = c[:, None, :]
    y_prompt, y_sample = x_prompt, x_sample
    new_k, new_v, new_hr, new_s = [], [], [], []
    for l in range(DEPTH):
        p = dict(w_mod=w_mod[l], b_mod=b_mod[l], norm1_g=norm1_g[l], norm2_g=norm2_g[l], w_in=w_in[l],
                 q_norm_g=q_norm_g[l], k_norm_g=k_norm_g[l], conv_w=conv_w[l], conv_b=conv_b[l],
                 rg_wa=rg_wa[l], rg_ba=rg_ba[l], rg_wx=rg_wx[l], rg_bx=rg_bx[l], rg_lambda=rg_lambda[l],
                 hgrn_norm_g=hgrn_norm_g[l], w_attn_o=w_attn_o[l], w_rnn_o=w_rnn_o[l], w_hgrn_o=w_hgrn_o[l],
                 w_out=w_out[l], w_router=w_router[l], b_router=b_router[l], w_gu=w_gu[l], b_gu=b_gu[l],
                 w_down=w_down[l], b_down=b_down[l])
        y_prompt, (k_l, v_l, hr_l, s_l) = trunk_layer(y_prompt, cond_ctx, p, lb_all[l], None, None)
        new_k.append(k_l)
        new_v.append(v_l)
        new_hr.append(hr_l)
        new_s.append(s_l)
        y_sample, _ = trunk_layer(y_sample, cond_lat, p, lb_all[l], rope,
                                  (cache_k[:, l], cache_v[:, l], state_rglru[:, l], state_hgrn[:, l]))
    new_cache_k = jnp.stack(new_k, axis=1)
    new_cache_v = jnp.stack(new_v, axis=1)
    new_state_rglru = jnp.stack(new_hr, axis=1)
    new_state_hgrn = jnp.stack(new_s, axis=1)
    return (y_prompt, y_sample, new_cache_k, new_cache_v, new_state_rglru, new_state_hgrn)
```

```python
import functools

import jax
import jax.numpy as jnp
from jax import lax
from jax.experimental import pallas as pl
from jax.experimental.pallas import tpu as pltpu

F32 = jnp.float32
BF16 = jnp.bfloat16

D_MODEL = 1024
GRID_W = 64
HEAD_DIM = 64
N_HEADS = 8
N_KV_HEADS = 2
KV_GROUP = N_HEADS // N_KV_HEADS
ATTN_W = N_HEADS * HEAD_DIM
KV_W = N_KV_HEADS * HEAD_DIM
ROPE_BASE = 10000.0
D_RNN = 512
RG_BLOCKS = 8
RG_BW = D_RNN // RG_BLOCKS
RG_C = 8.0
HG_HEADS = 8
HG_DK = 64
HG_DV = 64
HG_W = HG_HEADS * HG_DK
N_EXPERTS = 32
TOP_K = 4
D_FF = D_MODEL
SWIGLU_LIMIT = 7.0
SWIGLU_ALPHA = 1.702
N_MOD = 6
EPS = 1e-6
IN_W = ATTN_W + 2 * KV_W + 2 * D_RNN + 5 * HG_W + 3 * D_MODEL

LANES = 128
SEG_ROWS = 1024
COL_TILE = 256
HG_CHUNK = 256
Q_TILE = 256
EXPERT_TILE = 256
TOKEN_TILE = 256
NEG_BIG = -1e30

_SRC_TILES = tuple(range(17, 29)) + tuple(range(7, 17)) + tuple(range(3, 7)) + (0, 1, 2)
Z_GL = 0
Z_HG = 3 * D_MODEL
Z_RG = Z_HG + 5 * HG_W
Z_Q = Z_RG + 2 * D_RNN
Z_K = Z_Q + ATTN_W
Z_V = Z_K + KV_W


def _cparams(sem, vmem_mb):
    return pltpu.CompilerParams(dimension_semantics=sem, vmem_limit_bytes=vmem_mb * 1024 * 1024)


def _mm(a, b):
    return jnp.dot(a.astype(BF16), b.astype(BF16), preferred_element_type=F32)


def _mm_nt(a, b):
    return lax.dot_general(a.astype(BF16), b.astype(BF16), (((1,), (1,)), ((), ())),
                           preferred_element_type=F32)


def _mm_tn(a, b):
    return lax.dot_general(a.astype(BF16), b.astype(BF16), (((0,), (0,)), ((), ())),
                           preferred_element_type=F32)


def _sigmoid(x):
    return 1.0 / (1.0 + jnp.exp(-x))


def _softplus(x):
    return jnp.maximum(x, 0.0) + jnp.log1p(jnp.exp(-jnp.abs(x)))


def _modulated_norm(x, g, scale, shift):
    y = x * lax.rsqrt(jnp.mean(x * x, axis=-1, keepdims=True) + EPS)
    return (y * g) * (1.0 + scale) + shift


def _mod_kernel(c_ref, w_ref, b_ref, o_ref):
    c = c_ref[...]
    s = c * _sigmoid(c)
    o_ref[0] = _mm(s, w_ref[0]) + b_ref[0]


def _modulation(cond8, w_mod, b_mod):
    depth = w_mod.shape[0]
    tn = 1024
    return pl.pallas_call(
        _mod_kernel,
        grid=(depth, N_MOD * D_MODEL // tn),
        in_specs=[pl.BlockSpec((8, D_MODEL), lambda l, j: (0, 0)),
                  pl.BlockSpec((1, D_MODEL, tn), lambda l, j: (l, 0, j)),
                  pl.BlockSpec((1, 1, tn), lambda l, j: (l, 0, j))],
        out_specs=pl.BlockSpec((1, 8, tn), lambda l, j: (l, 0, j)),
        out_shape=jax.ShapeDtypeStruct((depth, 8, N_MOD * D_MODEL), F32),
        compiler_params=_cparams(("arbitrary", "arbitrary"), 32),
        name="modulation",
    )(cond8, w_mod, b_mod.reshape(depth, 1, N_MOD * D_MODEL))


def _inproj_kernel(src_ref, x_ref, g_ref, mod_ref, w_ref, o_ref, h_ref):
    @pl.when(pl.program_id(1) == 0)
    def _():
        h = _modulated_norm(x_ref[...], g_ref[0], mod_ref[0, 1:2, :], mod_ref[0, 0:1, :])
        h_ref[...] = h.astype(BF16)

    o_ref[...] = jnp.dot(h_ref[...], w_ref[0].astype(BF16), preferred_element_type=F32)


def _input_projection(x, norm_g, modseg, w_in, layer):
    n_tok = x.shape[0]
    tm = SEG_ROWS
    src = jnp.asarray(_SRC_TILES, jnp.int32)
    grid_spec = pltpu.PrefetchScalarGridSpec(
        num_scalar_prefetch=1,
        grid=(n_tok // tm, IN_W // COL_TILE),
        in_specs=[pl.BlockSpec((tm, D_MODEL), lambda i, j, s: (i, 0)),
                  pl.BlockSpec((1, 1, D_MODEL), lambda i, j, s: (layer, 0, 0)),
                  pl.BlockSpec((1, N_MOD, D_MODEL), lambda i, j, s: (i, 0, 0)),
                  pl.BlockSpec((1, D_MODEL, COL_TILE), lambda i, j, s: (layer, 0, s[j]))],
        out_specs=pl.BlockSpec((tm, COL_TILE), lambda i, j, s: (i, j)),
        scratch_shapes=[pltpu.VMEM((tm, D_MODEL), BF16)],
    )
    return pl.pallas_call(
        _inproj_kernel,
        grid_spec=grid_spec,
        out_shape=jax.ShapeDtypeStruct((n_tok, IN_W), F32),
        compiler_params=_cparams(("arbitrary", "arbitrary"), 40),
        name="input_projection",
    )(src, x, norm_g.reshape(-1, 1, D_MODEL), modseg, w_in)


def _head_rms(x, g):
    lane = lax.broadcasted_iota(jnp.int32, x.shape, 1)
    lo = lane < HEAD_DIM
    xx = x * x
    s_lo = jnp.sum(jnp.where(lo, xx, 0.0), axis=-1, keepdims=True)
    s_hi = jnp.sum(jnp.where(lo, 0.0, xx), axis=-1, keepdims=True)
    inv = jnp.where(lo, lax.rsqrt(s_lo * (1.0 / HEAD_DIM) + EPS),
                    lax.rsqrt(s_hi * (1.0 / HEAD_DIM) + EPS))
    return x * inv * g


def _rope(x, cos, sin_a, sin_b):
    q = HEAD_DIM // 4
    return x * cos + pltpu.roll(x, LANES - q, 1) * sin_a + pltpu.roll(x, q, 1) * sin_b


def _attn_kernel(*refs, t_len, use_ctx):
    if use_ctx:
        (q_ref, k_ref, v_ref, qg_ref, kg_ref, cos_q_ref, sa_q_ref, sb_q_ref,
         cos_k_ref, sa_k_ref, sb_k_ref, ck_ref, cv_ref, o_ref, kall_ref, vall_ref) = refs
    else:
        (q_ref, k_ref, v_ref, qg_ref, kg_ref, o_ref, ko_ref, vo_ref, kall_ref, vall_ref) = refs

    @pl.when(pl.program_id(1) == 0)
    def _():
        k = _head_rms(k_ref[...], kg_ref[...])
        if use_ctx:
            k = _rope(k, cos_k_ref[...], sa_k_ref[...], sb_k_ref[...])
            kall_ref[0:t_len, :] = k.astype(BF16)
            vall_ref[0:t_len, :] = v_ref[...].astype(BF16)
            kall_ref[t_len:, :] = ck_ref[0, 0].astype(BF16)
            vall_ref[t_len:, :] = cv_ref[0, 0].astype(BF16)
        else:
            ko_ref[0] = k
            vo_ref[0] = v_ref[...]
            kall_ref[...] = k.astype(BF16)
            vall_ref[...] = v_ref[...].astype(BF16)

    heads = []
    for c in range(ATTN_W // LANES):
        qc = _head_rms(q_ref[:, c * LANES:(c + 1) * LANES], qg_ref[...])
        if use_ctx:
            qc = _rope(qc, cos_q_ref[...], sa_q_ref[...], sb_q_ref[...])
        qc = (qc * (HEAD_DIM ** -0.5)).astype(BF16)
        heads.append(qc[:, :HEAD_DIM])
        heads.append(qc[:, HEAD_DIM:])
    tq = q_ref.shape[0]
    for g in range(N_KV_HEADS):
        qs = jnp.concatenate(heads[g * KV_GROUP:(g + 1) * KV_GROUP], axis=0)
        kh = kall_ref[:, g * HEAD_DIM:(g + 1) * HEAD_DIM]
        vh = vall_ref[:, g * HEAD_DIM:(g + 1) * HEAD_DIM]
        s = _mm_nt(qs, kh)
        m = jnp.max(s, axis=-1, keepdims=True)
        p = jnp.exp(s - m)
        den = jnp.sum(p, axis=-1, keepdims=True)
        o = _mm(p, vh) / den
        for j in range(KV_GROUP):
            hh = g * KV_GROUP + j
            o_ref[:, hh * HEAD_DIM:(hh + 1) * HEAD_DIM] = o[j * tq:(j + 1) * tq].astype(BF16)


def _attention(z, q_g, k_g, layer, n_batch, t_len, row0, rope=None, cache=None):
    use_ctx = cache is not None
    nq = t_len // Q_TILE
    rb0 = row0 // t_len
    qb0 = row0 // Q_TILE
    qg = jnp.tile(q_g[layer], 2).reshape(1, LANES)
    kg = jnp.tile(k_g[layer], 2).reshape(1, LANES)
    n_out_tok = n_batch * t_len
    vec = pl.BlockSpec((1, LANES), lambda b, i: (0, 0))
    in_specs = [pl.BlockSpec((Q_TILE, ATTN_W), lambda b, i: (qb0 + b * nq + i, Z_Q // ATTN_W)),
                pl.BlockSpec((t_len, KV_W), lambda b, i: (rb0 + b, Z_K // KV_W)),
                pl.BlockSpec((t_len, KV_W), lambda b, i: (rb0 + b, Z_V // KV_W)),
                vec, vec]
    args = [z, z, z, qg, kg]
    out_specs = [pl.BlockSpec((Q_TILE, ATTN_W), lambda b, i: (b * nq + i, 0))]
    out_shape = [jax.ShapeDtypeStruct((n_out_tok, ATTN_W), BF16)]
    t_keys = t_len
    if use_ctx:
        cos, sin_a, sin_b = rope
        cache_k, cache_v = cache
        past = cache_k.shape[2]
        t_keys = t_len + past
        tq_spec = pl.BlockSpec((Q_TILE, LANES), lambda b, i: (i, 0))
        tk_spec = pl.BlockSpec((t_len, LANES), lambda b, i: (0, 0))
        c_spec = pl.BlockSpec((1, 1, past, KV_W), lambda b, i: (b, layer, 0, 0))
        in_specs += [tq_spec, tq_spec, tq_spec, tk_spec, tk_spec, tk_spec, c_spec, c_spec]
        args += [cos, sin_a, sin_b, cos, sin_a, sin_b,
                 cache_k.reshape(cache_k.shape[0], cache_k.shape[1], past, KV_W),
                 cache_v.reshape(cache_v.shape[0], cache_v.shape[1], past, KV_W)]
    else:
        kv_spec = pl.BlockSpec((1, t_len, KV_W), lambda b, i: (b, 0, 0))
        out_specs += [kv_spec, kv_spec]
        out_shape += [jax.ShapeDtypeStruct((n_batch, t_len, KV_W), F32)] * 2
    return pl.pallas_call(
        functools.partial(_attn_kernel, t_len=t_len, use_ctx=use_ctx),
        grid=(n_batch, nq),
        in_specs=in_specs,
        out_specs=out_specs,
        out_shape=out_shape,
        scratch_shapes=[pltpu.VMEM((t_keys, KV_W), BF16), pltpu.VMEM((t_keys, KV_W), BF16)],
        compiler_params=_cparams(("arbitrary", "arbitrary"), 48),
        name="attention_ctx" if use_ctx else "attention",
    )(*args)


def _shift_rows(x, d):
    n = x.shape[0]
    row = lax.broadcasted_iota(jnp.int32, x.shape, 0)
    y = pltpu.roll(x, d % n, 0)
    if d > 0:
        return jnp.where(row >= d, y, 0.0)
    return jnp.where(row < n + d, y, 0.0)


def _linear_scan(a, u, reverse):
    n = a.shape[0]
    row = lax.broadcasted_iota(jnp.int32, a.shape, 0)
    d = 1
    while d < n:
        if reverse:
            keep = row < n - d
            a_s = jnp.where(keep, pltpu.roll(a, n - d, 0), 1.0)
            u_s = jnp.where(keep, pltpu.roll(u, n - d, 0), 0.0)
        else:
            keep = row >= d
            a_s = jnp.where(keep, pltpu.roll(a, d, 0), 1.0)
            u_s = jnp.where(keep, pltpu.roll(u, d, 0), 0.0)
        u = a * u_s + u
        a = a * a_s
        d *= 2
    return a, u


def _rglru_kernel(*refs, use_ctx):
    if use_ctx:
        (x_ref, y_ref, cw_ref, cb_ref, wg_ref, bg_ref, lam_ref, h0_ref, o_ref) = refs
    else:
        (x_ref, y_ref, cw_ref, cb_ref, wg_ref, bg_ref, lam_ref, o_ref, last_ref) = refs
    x = x_ref[...]
    cw = cw_ref[0]
    xr = cb_ref[0] + _shift_rows(x, 1) * cw[0:1] + x * cw[1:2] \
        + _shift_rows(x, -1) * cw[2:3] + _shift_rows(x, -2) * cw[3:4]
    gates = _sigmoid(_mm(xr, wg_ref[0]) + bg_ref[0])
    n = x.shape[0]
    total = None
    lasts = []
    for d in range(2):
        r = gates[:, (2 * d) * D_RNN:(2 * d + 1) * D_RNN]
        i = gates[:, (2 * d + 1) * D_RNN:(2 * d + 2) * D_RNN]
        log_a = (-RG_C * _softplus(-lam_ref[0, d:d + 1, :])) * r
        a = jnp.exp(log_a)
        u = jnp.sqrt(1.0 - jnp.exp(2.0 * log_a)) * (i * xr)
        a_cum, h = _linear_scan(a, u, reverse=(d == 1))
        if use_ctx:
            h = h + a_cum * h0_ref[0, 0, d:d + 1, :]
        else:
            lasts.append(h[0:1] if d == 1 else h[n - 1:n])
        total = h if total is None else total + h
    o_ref[...] = (jax.nn.gelu(y_ref[...]) * total).astype(BF16)
    if not use_ctx:
        last_ref[0] = jnp.concatenate(lasts, axis=0)


def _rglru(z, conv_w, conv_b, w_gate, b_gate, lam, layer, n_batch, t_len, row0, state=None):
    use_ctx = state is not None
    rb0 = row0 // t_len
    in_specs = [pl.BlockSpec((t_len, D_RNN), lambda b: (rb0 + b, Z_RG // D_RNN)),
                pl.BlockSpec((t_len, D_RNN), lambda b: (rb0 + b, Z_RG // D_RNN + 1)),
                pl.BlockSpec((1, 4, D_RNN), lambda b: (layer, 0, 0)),
                pl.BlockSpec((1, 1, D_RNN), lambda b: (layer, 0, 0)),
                pl.BlockSpec((1, D_RNN, 4 * D_RNN), lambda b: (layer, 0, 0)),
                pl.BlockSpec((1, 1, 4 * D_RNN), lambda b: (layer, 0, 0)),
                pl.BlockSpec((1, 2, D_RNN), lambda b: (layer, 0, 0))]
    args = [z, z, conv_w, conv_b.reshape(-1, 1, D_RNN), w_gate, b_gate, lam]
    out_specs = [pl.BlockSpec((t_len, D_RNN), lambda b: (b, 0))]
    out_shape = [jax.ShapeDtypeStruct((n_batch * t_len, D_RNN), BF16)]
    if use_ctx:
        in_specs.append(pl.BlockSpec((1, 1, 2, D_RNN), lambda b: (b, layer, 0, 0)))
        args.append(state)
    else:
        out_specs.append(pl.BlockSpec((1, 2, D_RNN), lambda b: (b, 0, 0)))
        out_shape.append(jax.ShapeDtypeStruct((n_batch, 2, D_RNN), F32))
    return pl.pallas_call(
        functools.partial(_rglru_kernel, use_ctx=use_ctx),
        grid=(n_batch,),
        in_specs=in_specs,
        out_specs=out_specs,
        out_shape=out_shape,
        compiler_params=_cparams(("arbitrary",), 56),
        name="rglru_ctx" if use_ctx else "rglru",
    )(*args)


def _cumsum_rows(g, reverse):
    n = g.shape[0]
    row = lax.broadcasted_iota(jnp.int32, g.shape, 0)
    d = 1
    while d < n:
        if reverse:
            g = g + jnp.where(row < n - d, pltpu.roll(g, n - d, 0), 0.0)
        else:
            g = g + jnp.where(row >= d, pltpu.roll(g, d, 0), 0.0)
        d *= 2
    return g


def _block_row(gc, blk, r):
    n, w = gc.shape
    if blk >= 8:
        g3 = gc.reshape(n // blk, blk, w)
        return jnp.broadcast_to(g3[:, r:r + 1, :], (n // blk, blk, w)).reshape(n, w)
    row = lax.broadcasted_iota(jnp.int32, gc.shape, 0)
    res = row & (blk - 1)
    out = gc
    for m in range(blk):
        if m != r:
            out = jnp.where(res == m, pltpu.roll(gc, (m - r) % n, 0), out)
    return out


def _hgrn_gates(f_logit, lb):
    e = jnp.log1p(jnp.exp(-jnp.abs(f_logit)))
    log_sig = jnp.minimum(f_logit, 0.0) - e
    a = jnp.log(lb)
    b = jnp.log1p(-lb) + log_sig
    log_f = jnp.maximum(a, b) + jnp.log1p(jnp.exp(-jnp.abs(a - b)))
    k = (1.0 - lb) * jnp.exp(-jnp.maximum(f_logit, 0.0) - e)
    return log_f, k


def _hgrn_chunk_state(gc, k, v, reverse):
    n = gc.shape[0]
    g_tot = gc[0:1] if reverse else gc[n - 1:n]
    k_end = k * jnp.exp(jnp.minimum(g_tot - gc, 0.0))
    ds = [_mm_tn(k_end[:, h * HG_DK:(h + 1) * HG_DK], v[:, h * HG_DV:(h + 1) * HG_DV])
          for h in range(HG_HEADS)]
    return ds, jnp.exp(g_tot)


def _decay_state(s, e_row):
    eye = (lax.broadcasted_iota(jnp.int32, (HG_DK, HG_DK), 0)
           == lax.broadcasted_iota(jnp.int32, (HG_DK, HG_DK), 1))
    col = jnp.sum(jnp.where(eye, jnp.broadcast_to(e_row, (HG_DK, HG_DK)), 0.0), axis=1, keepdims=True)
    return s * col


def _hgrn_intra_scores(q, k, gc, reverse, txs, t_gt_s):
    n = q.shape[0]
    scores = [None] * HG_HEADS
    w = 1
    while w < n:
        blk = 2 * w
        m = _block_row(gc, blk, w if reverse else w - 1)
        qt = q * jnp.exp(jnp.minimum(gc - m, 0.0))
        kt = k * jnp.exp(jnp.minimum(m - gc, 0.0))
        sel = (txs >= w) & (txs < blk) & (t_gt_s != reverse)
        for h in range(HG_HEADS):
            s = _mm_nt(qt[:, h * HG_DK:(h + 1) * HG_DK], kt[:, h * HG_DK:(h + 1) * HG_DK])
            s = jnp.where(sel, s, 0.0)
            scores[h] = s if scores[h] is None else scores[h] + s
        w = blk
    return scores


def _hgrn_kernel(*refs, t_len, use_ctx):
    if use_ctx:
        (q_ref, ff_ref, fb_ref, i_ref, og_ref, lb_ref, ng_ref, s0_ref, o_ref, sin_ref) = refs
    else:
        (q_ref, ff_ref, fb_ref, i_ref, og_ref, lb_ref, ng_ref, o_ref, sl_ref) = refs
    c = HG_CHUNK
    n_chunks = t_len // c
    f_refs = (ff_ref, fb_ref)

    def load_dir(d, r0):
        log_f, k = _hgrn_gates(f_refs[d][pl.ds(r0, c), :], lb_ref[0, d:d + 1, :])
        return _cumsum_rows(log_f, reverse=(d == 1)), k

    if use_ctx:
        for d in range(2):
            states = [s0_ref[0, 0, d, h] for h in range(HG_HEADS)]
            order = range(n_chunks) if d == 0 else range(n_chunks - 1, -1, -1)
            for ci in order:
                for h in range(HG_HEADS):
                    sin_ref[d, ci, h] = states[h]
                gc, k = load_dir(d, ci * c)
                ds, e_tot = _hgrn_chunk_state(gc, k, i_ref[pl.ds(ci * c, c), :], reverse=(d == 1))
                states = [_decay_state(states[h], e_tot[:, h * HG_DK:(h + 1) * HG_DK]) + ds[h]
                          for h in range(HG_HEADS)]

    ti = lax.broadcasted_iota(jnp.int32, (c, c), 0)
    si = lax.broadcasted_iota(jnp.int32, (c, c), 1)
    txs = ti ^ si
    t_gt_s = ti > si
    diag = ti == si

    def chunk_body(ci, carry):
        r0 = pl.multiple_of(ci * c, c)
        hq = q_ref[pl.ds(r0, c), :]
        q = hq * _sigmoid(hq)
        v = i_ref[pl.ds(r0, c), :]
        scores = None
        inter = [None] * HG_HEADS
        k_sum = None
        for d in range(2):
            gc, k = load_dir(d, r0)
            sc = _hgrn_intra_scores(q, k, gc, d == 1, txs, t_gt_s)
            scores = sc if scores is None else [a + b for a, b in zip(scores, sc)]
            k_sum = k if k_sum is None else k_sum + k
            if use_ctx:
                qe = q * jnp.exp(gc)
                for h in range(HG_HEADS):
                    t = _mm(qe[:, h * HG_DK:(h + 1) * HG_DK], sin_ref[d, ci, h])
                    inter[h] = t if inter[h] is None else inter[h] + t
            else:
                ds, _ = _hgrn_chunk_state(gc, k, v, reverse=(d == 1))
                for h in range(HG_HEADS):
                    sl_ref[0, d, h] = ds[h]
        outs = []
        for h in range(HG_HEADS):
            hs = slice(h * HG_DK, (h + 1) * HG_DK)
            s_diag = jnp.where(diag, _mm_nt(q[:, hs], k_sum[:, hs]), 0.0)
            o = _mm(scores[h] + s_diag, v[:, h * HG_DV:(h + 1) * HG_DV])
            if use_ctx:
                o = o + inter[h]
            o = o * lax.rsqrt(jnp.mean(o * o, axis=-1, keepdims=True) + EPS) * ng_ref[0]
            outs.append(o)
        og = og_ref[pl.ds(r0, c), :]
        o_all = jnp.concatenate(outs, axis=1) * (og * _sigmoid(og))
        o_ref[pl.ds(r0, c), :] = o_all.astype(BF16)
        return carry

    if n_chunks == 1:
        chunk_body(0, 0)
    else:
        lax.fori_loop(0, n_chunks, chunk_body, 0)


def _hgrn(z, lb, norm_g, layer, n_batch, t_len, row0, state=None):
    use_ctx = state is not None
    rb0 = row0 // t_len
    cb = Z_HG // HG_W

    def zspec(k):
        return pl.BlockSpec((t_len, HG_W), lambda b: (rb0 + b, cb + k))

    in_specs = [zspec(0), zspec(1), zspec(2), zspec(3), zspec(4),
                pl.BlockSpec((1, 2, HG_W), lambda b: (layer, 0, 0)),
                pl.BlockSpec((1, 1, HG_DV), lambda b: (layer, 0, 0))]
    args = [z, z, z, z, z, lb, norm_g.reshape(-1, 1, HG_DV)]
    out_specs = [pl.BlockSpec((t_len, HG_W), lambda b: (b, 0))]
    out_shape = [jax.ShapeDtypeStruct((n_batch * t_len, HG_W), BF16)]
    scratch = []
    if use_ctx:
        in_specs.append(pl.BlockSpec((1, 1, 2, HG_HEADS, HG_DK, HG_DV), lambda b: (b, layer, 0, 0, 0, 0)))
        args.append(state)
        scratch.append(pltpu.VMEM((2, t_len // HG_CHUNK, HG_HEADS, HG_DK, HG_DV), F32))
    else:
        out_specs.append(pl.BlockSpec((1, 2, HG_HEADS, HG_DK, HG_DV), lambda b: (b, 0, 0, 0, 0)))
        out_shape.append(jax.ShapeDtypeStruct((n_batch, 2, HG_HEADS, HG_DK, HG_DV), F32))
    return pl.pallas_call(
        functools.partial(_hgrn_kernel, t_len=t_len, use_ctx=use_ctx),
        grid=(n_batch,),
        in_specs=in_specs,
        out_specs=out_specs,
        out_shape=out_shape,
        scratch_shapes=scratch,
        compiler_params=_cparams(("arbitrary",), 56),
        name="hgrn_ctx" if use_ctx else "hgrn",
    )(*args)


def _merge_kernel(a_ref, r_ref, h_ref, g0_ref, g1_ref, g2_ref, x_ref, mod_ref,
                  wa_ref, wr_ref, wh_ref, wo_ref, o_ref, wa_s, wr_s, wh_s, wo_s):
    @pl.when(pl.program_id(0) == 0)
    def _():
        wa_s[...] = wa_ref[0].astype(BF16)
        wr_s[...] = wr_ref[0].astype(BF16)
        wh_s[...] = wh_ref[0].astype(BF16)
        wo_s[...] = wo_ref[0].astype(BF16)

    def proj(x_r, w_s):
        return jnp.dot(x_r[...], w_s[...], preferred_element_type=F32)

    merged = (_sigmoid(g0_ref[...]) * proj(a_ref, wa_s)
              + _sigmoid(g1_ref[...]) * proj(r_ref, wr_s)
              + _sigmoid(g2_ref[...]) * proj(h_ref, wh_s))
    out = jnp.dot(merged.astype(BF16), wo_s[...], preferred_element_type=F32)
    o_ref[...] = x_ref[...] + mod_ref[0, 2:3, :] * out


def _merge_project(attn, rnn, hg, z, x, modseg, w_attn_o, w_rnn_o, w_hgrn_o, w_out, layer):
    n_tok = x.shape[0]
    tm = 512
    per_seg = SEG_ROWS // tm
    half = pl.BlockSpec((tm, ATTN_W), lambda i: (i, 0))
    full = pl.BlockSpec((tm, D_MODEL), lambda i: (i, 0))

    def gspec(k):
        return pl.BlockSpec((tm, D_MODEL), lambda i: (i, Z_GL // D_MODEL + k))

    def wspec(rows):
        return pl.BlockSpec((1, rows, D_MODEL), lambda i: (layer, 0, 0))

    return pl.pallas_call(
        _merge_kernel,
        grid=(n_tok // tm,),
        in_specs=[half, half, half, gspec(0), gspec(1), gspec(2), full,
                  pl.BlockSpec((1, N_MOD, D_MODEL), lambda i: (i // per_seg, 0, 0)),
                  wspec(ATTN_W), wspec(D_RNN), wspec(HG_W), wspec(D_MODEL)],
        out_specs=full,
        out_shape=jax.ShapeDtypeStruct((n_tok, D_MODEL), F32),
        scratch_shapes=[pltpu.VMEM((ATTN_W, D_MODEL), BF16), pltpu.VMEM((D_RNN, D_MODEL), BF16),
                        pltpu.VMEM((HG_W, D_MODEL), BF16), pltpu.VMEM((D_MODEL, D_MODEL), BF16)],
        compiler_params=_cparams(("arbitrary",), 56),
        name="merge_project",
    )(attn, rnn, hg, z, z, z, x, modseg, w_attn_o, w_rnn_o, w_hgrn_o, w_out)


def _router_kernel(x_ref, g_ref, mod_ref, wr_ref, br_ref, h_ref, id_ref, wt_ref, pr_ref, cnt_ref, run_ref):
    @pl.when(pl.program_id(0) == 0)
    def _():
        run_ref[...] = jnp.zeros_like(run_ref)

    h = _modulated_norm(x_ref[...], g_ref[0], mod_ref[0, 4:5, :], mod_ref[0, 3:4, :])
    h_ref[...] = h
    logits = jnp.dot(h, wr_ref[0], preferred_element_type=F32,
                     precision=lax.Precision.HIGHEST) + br_ref[0]
    tm = logits.shape[0]
    lane = lax.broadcasted_iota(jnp.int32, logits.shape, 1)
    ids, vals = [], []
    chosen = jnp.zeros(logits.shape, jnp.bool_)
    work = logits
    for _ in range(TOP_K):
        m = jnp.max(work, axis=-1, keepdims=True)
        idx = jnp.min(jnp.where(work == m, lane, LANES), axis=-1, keepdims=True)
        hit = lane == idx
        chosen = chosen | hit
        work = jnp.where(hit, -jnp.inf, work)
        ids.append(idx)
        vals.append(m)
    exps = [jnp.exp(v - vals[0]) for v in vals]
    den = exps[0] + exps[1] + exps[2] + exps[3]
    ind = jnp.where(chosen, 1.0, 0.0)
    r_i = lax.broadcasted_iota(jnp.int32, (tm, tm), 0)
    c_i = lax.broadcasted_iota(jnp.int32, (tm, tm), 1)
    before = _mm(jnp.where(c_i < r_i, 1.0, 0.0), ind) + run_ref[0:1, :]
    id_out = jnp.zeros(logits.shape, jnp.int32)
    wt_out = jnp.zeros(logits.shape, F32)
    pr_out = jnp.zeros(logits.shape, jnp.int32)
    for k in range(TOP_K):
        prior = jnp.sum(jnp.where(lane == ids[k], before, 0.0), axis=-1, keepdims=True)
        id_out = jnp.where(lane == k, ids[k], id_out)
        wt_out = jnp.where(lane == k, exps[k] / den, wt_out)
        pr_out = jnp.where(lane == k, prior.astype(jnp.int32), pr_out)
    id_ref[...] = id_out
    wt_ref[...] = wt_out
    pr_ref[...] = pr_out
    total = run_ref[...] + jnp.sum(ind, axis=0, keepdims=True)
    run_ref[...] = total
    cnt_ref[...] = total


def _router(x, norm_g, modseg, w_router, b_router, layer):
    n_tok = x.shape[0]
    tm = 512
    per_seg = SEG_ROWS // tm
    depth = w_router.shape[0]
    w_pad = jnp.zeros((depth, D_MODEL, LANES), F32).at[:, :, :N_EXPERTS].set(w_router)
    b_pad = jnp.full((depth, 1, LANES), NEG_BIG, F32).at[:, 0, :N_EXPERTS].set(b_router)
    full = pl.BlockSpec((tm, D_MODEL), lambda i: (i, 0))
    small = pl.BlockSpec((tm, LANES), lambda i: (i, 0))
    return pl.pallas_call(
        _router_kernel,
        grid=(n_tok // tm,),
        in_specs=[full,
                  pl.BlockSpec((1, 1, D_MODEL), lambda i: (layer, 0, 0)),
                  pl.BlockSpec((1, N_MOD, D_MODEL), lambda i: (i // per_seg, 0, 0)),
                  pl.BlockSpec((1, D_MODEL, LANES), lambda i: (layer, 0, 0)),
                  pl.BlockSpec((1, 1, LANES), lambda i: (layer, 0, 0))],
        out_specs=[full, small, small, small, pl.BlockSpec((8, LANES), lambda i: (0, 0))],
        out_shape=[jax.ShapeDtypeStruct((n_tok, D_MODEL), F32),
                   jax.ShapeDtypeStruct((n_tok, LANES), jnp.int32),
                   jax.ShapeDtypeStruct((n_tok, LANES), F32),
                   jax.ShapeDtypeStruct((n_tok, LANES), jnp.int32),
                   jax.ShapeDtypeStruct((8, LANES), F32)],
        scratch_shapes=[pltpu.VMEM((8, LANES), F32)],
        compiler_params=_cparams(("arbitrary",), 40),
        name="router",
    )(x, norm_g.reshape(-1, 1, D_MODEL), modseg, w_pad, b_pad)


def _dispatch_kernel(pos_ref, h_ref, init_ref, xs_ref, sem):
    del init_ref
    base = pl.program_id(0) * TOKEN_TILE

    def row_copy(t, k):
        return pltpu.make_async_copy(h_ref.at[pl.ds(base + t, 1)],
                                     xs_ref.at[pl.ds(pos_ref[t * TOP_K + k], 1)], sem)

    def start(t, carry):
        for k in range(TOP_K):
            row_copy(t, k).start()
        return carry

    def wait(t, carry):
        for k in range(TOP_K):
            row_copy(t, k).wait()
        return carry

    lax.fori_loop(0, TOKEN_TILE, start, 0)
    lax.fori_loop(0, TOKEN_TILE, wait, 0)


def _dispatch(h, pos_flat, n_rows):
    n_tok = h.shape[0]
    init = jnp.zeros((n_rows, D_MODEL), F32)
    return pl.pallas_call(
        _dispatch_kernel,
        grid=(n_tok // TOKEN_TILE,),
        in_specs=[pl.BlockSpec((TOKEN_TILE * TOP_K,), lambda i: (i,), memory_space=pltpu.SMEM),
                  pl.BlockSpec(memory_space=pl.ANY),
                  pl.BlockSpec(memory_space=pl.ANY)],
        out_specs=pl.BlockSpec(memory_space=pl.ANY),
        out_shape=jax.ShapeDtypeStruct((n_rows, D_MODEL), F32),
        scratch_shapes=[pltpu.SemaphoreType.DMA(())],
        input_output_aliases={2: 0},
        compiler_params=_cparams(("arbitrary",), 32),
        name="dispatch",
    )(pos_flat, h, init)


def _combine_kernel(pos_ref, ys_ref, wt_ref, x_ref, mod_ref, o_ref, buf, sem):
    def row_copy(t, k):
        return pltpu.make_async_copy(ys_ref.at[pl.ds(pos_ref[t * TOP_K + k], 1)],
                                     buf.at[k, pl.ds(t, 1)], sem)

    def start(t, carry):
        for k in range(TOP_K):
            row_copy(t, k).start()
        return carry

    def wait(t, carry):
        for k in range(TOP_K):
            row_copy(t, k).wait()
        return carry

    lax.fori_loop(0, TOKEN_TILE, start, 0)
    lax.fori_loop(0, TOKEN_TILE, wait, 0)
    acc = wt_ref[:, 0:1] * buf[0]
    for k in range(1, TOP_K):
        acc = acc + wt_ref[:, k:k + 1] * buf[k]
    o_ref[...] = x_ref[...] + mod_ref[0, 5:6, :] * acc


def _combine(ys, pos_flat, wts, x, modseg):
    n_tok = x.shape[0]
    per_seg = SEG_ROWS // TOKEN_TILE
    full = pl.BlockSpec((TOKEN_TILE, D_MODEL), lambda i: (i, 0))
    return pl.pallas_call(
        _combine_kernel,
        grid=(n_tok // TOKEN_TILE,),
        in_specs=[pl.BlockSpec((TOKEN_TILE * TOP_K,), lambda i: (i,), memory_space=pltpu.SMEM),
                  pl.BlockSpec(memory_space=pl.ANY),
                  pl.BlockSpec((TOKEN_TILE, LANES), lambda i: (i, 0)),
                  full,
                  pl.BlockSpec((1, N_MOD, D_MODEL), lambda i: (i // per_seg, 0, 0))],
        out_specs=full,
        out_shape=jax.ShapeDtypeStruct((n_tok, D_MODEL), F32),
        scratch_shapes=[pltpu.VMEM((TOP_K, TOKEN_TILE, D_MODEL), F32), pltpu.SemaphoreType.DMA(())],
        compiler_params=_cparams(("arbitrary",), 40),
        name="combine",
    )(pos_flat, ys, wts, x, modseg)


def _expert_kernel(te_ref, na_ref, x_ref, wgu_ref, bgu_ref, wd_ref, bd_ref, o_ref, wgu_s, wd_s):
    i = pl.program_id(0)
    active = i < na_ref[0]
    fresh = jnp.logical_or(i == 0, te_ref[i] != te_ref[jnp.maximum(i - 1, 0)])

    @pl.when(jnp.logical_and(active, fresh))
    def _():
        wgu_s[...] = wgu_ref[0, 0].astype(BF16)
        wd_s[...] = wd_ref[0, 0].astype(BF16)

    @pl.when(active)
    def _():
        gu = jnp.dot(x_ref[...].astype(BF16), wgu_s[...], preferred_element_type=F32) + bgu_ref[0, 0]
        gate = jnp.minimum(gu[:, :D_FF], SWIGLU_LIMIT)
        up = jnp.clip(gu[:, D_FF:], -SWIGLU_LIMIT, SWIGLU_LIMIT)
        act = gate * _sigmoid(SWIGLU_ALPHA * gate) * (up + 1.0)
        o_ref[...] = jnp.dot(act.astype(BF16), wd_s[...], preferred_element_type=F32) + bd_ref[0, 0]

    @pl.when(jnp.logical_not(active))
    def _():
        o_ref[...] = jnp.zeros_like(o_ref)


def _experts(xs, tile_expert, n_active, w_gu, b_gu, w_down, b_down, layer):
    n_rows = xs.shape[0]
    depth = w_gu.shape[0]
    tm = EXPERT_TILE
    grid_spec = pltpu.PrefetchScalarGridSpec(
        num_scalar_prefetch=2,
        grid=(n_rows // tm,),
        in_specs=[pl.BlockSpec((tm, D_MODEL), lambda i, te, na: (i, 0)),
                  pl.BlockSpec((1, 1, D_MODEL, 2 * D_FF), lambda i, te, na: (layer, te[i], 0, 0)),
                  pl.BlockSpec((1, 1, 1, 2 * D_FF), lambda i, te, na: (layer, te[i], 0, 0)),
                  pl.BlockSpec((1, 1, D_FF, D_MODEL), lambda i, te, na: (layer, te[i], 0, 0)),
                  pl.BlockSpec((1, 1, 1, D_MODEL), lambda i, te, na: (layer, te[i], 0, 0))],
        out_specs=pl.BlockSpec((tm, D_MODEL), lambda i, te, na: (i, 0)),
        scratch_shapes=[pltpu.VMEM((D_MODEL, 2 * D_FF), BF16), pltpu.VMEM((D_FF, D_MODEL), BF16)],
    )
    return pl.pallas_call(
        _expert_kernel,
        grid_spec=grid_spec,
        out_shape=jax.ShapeDtypeStruct((n_rows, D_MODEL), F32),
        compiler_params=_cparams(("arbitrary",), 56),
        name="experts",
    )(tile_expert, n_active, xs, w_gu, b_gu.reshape(depth, N_EXPERTS, 1, 2 * D_FF),
      w_down, b_down.reshape(depth, N_EXPERTS, 1, D_MODEL))


def _moe(x, norm_g, modseg, w_router, b_router, w_gu, b_gu, w_down, b_down, layer):
    n_tok = x.shape[0]
    h, ids, wts, prior, cnt = _router(x, norm_g, modseg, w_router, b_router, layer)
    tm = EXPERT_TILE
    n_rows = n_tok * TOP_K + N_EXPERTS * tm
    counts = cnt[0, :N_EXPERTS].astype(jnp.int32)
    padded = ((counts + tm - 1) // tm) * tm
    ends = jnp.cumsum(padded)
    starts = ends - padded
    pos = (starts[ids[:, :TOP_K]] + prior[:, :TOP_K]).reshape(-1)
    tile_row0 = jnp.arange(n_rows // tm, dtype=jnp.int32) * tm
    tile_expert = jnp.minimum(jnp.searchsorted(ends, tile_row0, side="right"),
                              N_EXPERTS - 1).astype(jnp.int32)
    n_active = (ends[-1:] // tm).astype(jnp.int32)
    xs = _dispatch(h, pos, n_rows)
    ys = _experts(xs, tile_expert, n_active, w_gu, b_gu, w_down, b_down, layer)
    return _combine(ys, pos, wts, x, modseg)


def _rope_tables(n_tok):
    rows = n_tok // GRID_W
    row = jnp.repeat(jnp.arange(rows, dtype=F32), GRID_W)
    col = jnp.tile(jnp.arange(GRID_W, dtype=F32), rows)
    quarter = HEAD_DIM // 4
    inv_freq = ROPE_BASE ** (-jnp.arange(quarter, dtype=F32) / quarter)
    ang_r = row[:, None] * inv_freq
    ang_c = col[:, None] * inv_freq
    ang = jnp.concatenate([ang_r, ang_r, ang_c, ang_c], axis=-1)
    cos, sin = jnp.cos(ang), jnp.sin(ang)
    first = (jnp.arange(HEAD_DIM) % (2 * quarter)) < quarter
    sin_a = jnp.where(first, -sin, 0.0)
    sin_b = jnp.where(first, 0.0, sin)
    return tuple(jnp.tile(t, (1, LANES // HEAD_DIM)) for t in (cos, sin_a, sin_b))


def _block_diag(w):
    eye = jnp.eye(RG_BLOCKS, dtype=w.dtype)
    return jnp.einsum("nde,nm->ndme", w, eye).reshape(D_RNN, D_RNN)


def kernel(x_prompt, x_sample, c, cache_k, cache_v, state_rglru, state_hgrn, c_ctx, w_mod, b_mod, norm1_g, norm2_g, w_in, q_norm_g, k_norm_g, conv_w, conv_b, rg_wa, rg_ba, rg_wx, rg_bx, rg_lambda, hgrn_lb_logits, hgrn_norm_g, w_attn_o, w_rnn_o, w_hgrn_o, w_out, w_router, b_router, w_gu, b_gu, w_down, b_down):
    depth = w_mod.shape[0]
    n_p, t_p = x_prompt.shape[0], x_prompt.shape[1]
    n_s, t_s = x_sample.shape[0], x_sample.shape[1]
    tok_p = n_p * t_p
    tok_s = n_s * t_s
    assert tok_p % SEG_ROWS == 0 and t_s == SEG_ROWS and SEG_ROWS % t_p == 0

    x = jnp.concatenate([x_prompt.reshape(tok_p, D_MODEL), x_sample.reshape(tok_s, D_MODEL)], axis=0)

    lb_all = jnp.cumsum(jax.nn.softmax(hgrn_lb_logits.astype(F32), axis=0), axis=0)
    lb_all = lb_all - lb_all[:1]
    rope = _rope_tables(t_s)

    cond8 = jnp.zeros((8, D_MODEL), F32).at[0].set(c_ctx).at[1:1 + n_s].set(c)
    mod = _modulation(cond8, w_mod, b_mod)
    seg_ids = jnp.asarray([0] * (tok_p // SEG_ROWS) + [1 + b for b in range(n_s)], jnp.int32)
    modsegs = mod[:, seg_ids].reshape(depth, seg_ids.shape[0], N_MOD, D_MODEL)

    w_gate = jnp.stack([jnp.concatenate([_block_diag(rg_wa[l, 0]), _block_diag(rg_wx[l, 0]),
                                         _block_diag(rg_wa[l, 1]), _block_diag(rg_wx[l, 1])], axis=1)
                        for l in range(depth)])
    b_gate = jnp.stack([jnp.concatenate([rg_ba[l, 0], rg_bx[l, 0], rg_ba[l, 1], rg_bx[l, 1]])[None]
                        for l in range(depth)])

    new_k, new_v, new_hr, new_s = [], [], [], []
    for l in range(depth):
        modseg = modsegs[l]
        z = _input_projection(x, norm1_g, modseg, w_in, l)
        attn_p, k_l, v_l = _attention(z, q_norm_g, k_norm_g, l, n_p, t_p, 0)
        (attn_s,) = _attention(z, q_norm_g, k_norm_g, l, n_s, t_s, tok_p, rope=rope,
                               cache=(cache_k, cache_v))
        rnn_p, hr_l = _rglru(z, conv_w, conv_b, w_gate, b_gate, rg_lambda, l, n_p, t_p, 0)
        (rnn_s,) = _rglru(z, conv_w, conv_b, w_gate, b_gate, rg_lambda, l, n_s, t_s, tok_p,
                          state=state_rglru)
        hg_p, s_l = _hgrn(z, lb_all, hgrn_norm_g, l, n_p, t_p, 0)
        (hg_s,) = _hgrn(z, lb_all, hgrn_norm_g, l, n_s, t_s, tok_p, state=state_hgrn)
        attn = jnp.concatenate([attn_p, attn_s], axis=0)
        rnn = jnp.concatenate([rnn_p, rnn_s], axis=0)
        hg = jnp.concatenate([hg_p, hg_s], axis=0)
        x = _merge_project(attn, rnn, hg, z, x, modseg, w_attn_o, w_rnn_o, w_hgrn_o, w_out, l)
        x = _moe(x, norm2_g, modseg, w_router, b_router, w_gu, b_gu, w_down, b_down, l)
        new_k.append(k_l.reshape(n_p, t_p, N_KV_HEADS, HEAD_DIM))
        new_v.append(v_l.reshape(n_p, t_p, N_KV_HEADS, HEAD_DIM))
        new_hr.append(hr_l)
        new_s.append(s_l)

    y_prompt = x[:tok_p].reshape(n_p, t_p, D_MODEL)
    y_sample = x[tok_p:].reshape(n_s, t_s, D_MODEL)
    return (y_prompt, y_sample, jnp.stack(new_k, axis=1), jnp.stack(new_v, axis=1),
            jnp.stack(new_hr, axis=1), jnp.stack(new_s, axis=1))
```

```python
import functools

import jax
import jax.numpy as jnp
from jax import lax
from jax.experimental import pallas as pl
from jax.experimental.pallas import tpu as pltpu

F32 = jnp.float32
BF16 = jnp.bfloat16

D_MODEL = 1024
GRID_W = 64
HEAD_DIM = 64
N_HEADS = 8
N_KV_HEADS = 2
KV_GROUP = N_HEADS // N_KV_HEADS
ATTN_W = N_HEADS * HEAD_DIM
KV_W = N_KV_HEADS * HEAD_DIM
ROPE_BASE = 10000.0
D_RNN = 512
RG_BLOCKS = 8
RG_BW = D_RNN // RG_BLOCKS
RG_C = 8.0
HG_HEADS = 8
HG_DK = 64
HG_DV = 64
HG_W = HG_HEADS * HG_DK
N_EXPERTS = 32
TOP_K = 4
D_FF = D_MODEL
SWIGLU_LIMIT = 7.0
SWIGLU_ALPHA = 1.702
N_MOD = 6
EPS = 1e-6
IN_W = ATTN_W + 2 * KV_W + 2 * D_RNN + 5 * HG_W + 3 * D_MODEL

LANES = 128
SEG_ROWS = 1024
COL_TILE = 256
HG_CHUNK = 256
Q_TILE = 256
EXPERT_TILE = 256
TOKEN_TILE = 256
NEG_BIG = -1e30

_SRC_TILES = tuple(range(17, 29)) + tuple(range(7, 17)) + tuple(range(3, 7)) + (0, 1, 2)
Z_GL = 0
Z_HG = 3 * D_MODEL
Z_RG = Z_HG + 5 * HG_W
Z_Q = Z_RG + 2 * D_RNN
Z_K = Z_Q + ATTN_W
Z_V = Z_K + KV_W


def _cparams(sem, vmem_mb):
    return pltpu.CompilerParams(dimension_semantics=sem, vmem_limit_bytes=vmem_mb * 1024 * 1024)


def _mm(a, b):
    return jnp.dot(a.astype(BF16), b.astype(BF16), preferred_element_type=F32)


def _mm_nt(a, b):
    return lax.dot_general(a.astype(BF16), b.astype(BF16), (((1,), (1,)), ((), ())),
                           preferred_element_type=F32)


def _mm_tn(a, b):
    return lax.dot_general(a.astype(BF16), b.astype(BF16), (((0,), (0,)), ((), ())),
                           preferred_element_type=F32)


def _sigmoid(x):
    return 1.0 / (1.0 + jnp.exp(-x))


def _softplus(x):
    return jnp.maximum(x, 0.0) + jnp.log1p(jnp.exp(-jnp.abs(x)))


def _modulated_norm(x, g, scale, shift):
    y = x * lax.rsqrt(jnp.mean(x * x, axis=-1, keepdims=True) + EPS)
    return (y * g) * (1.0 + scale) + shift


def _mod_kernel(c_ref, w_ref, b_ref, o_ref):
    c = c_ref[...]
    s = c * _sigmoid(c)
    o_ref[0] = _mm(s, w_ref[0]) + b_ref[0]


def _modulation(cond8, w_mod, b_mod):
    depth = w_mod.shape[0]
    tn = 1024
    return pl.pallas_call(
        _mod_kernel,
        grid=(depth, N_MOD * D_MODEL // tn),
        in_specs=[pl.BlockSpec((8, D_MODEL), lambda l, j: (0, 0)),
                  pl.BlockSpec((1, D_MODEL, tn), lambda l, j: (l, 0, j)),
                  pl.BlockSpec((1, 1, tn), lambda l, j: (l, 0, j))],
        out_specs=pl.BlockSpec((1, 8, tn), lambda l, j: (l, 0, j)),
        out_shape=jax.ShapeDtypeStruct((depth, 8, N_MOD * D_MODEL), F32),
        compiler_params=_cparams(("arbitrary", "arbitrary"), 32),
        name="modulation",
    )(cond8, w_mod, b_mod.reshape(depth, 1, N_MOD * D_MODEL))


def _inproj_kernel(src_ref, x_ref, g_ref, mod_ref, w_ref, o_ref, h_ref, w_s):
    j = pl.program_id(1)

    @pl.when(j == 0)
    def _():
        h = _modulated_norm(x_ref[...], g_ref[0], mod_ref[0, 1:2, :], mod_ref[0, 0:1, :])
        h_ref[...] = h.astype(BF16)

    @pl.when(pl.program_id(0) == 0)
    def _():
        w_s[j] = w_ref[0].astype(BF16)

    o_ref[...] = jnp.dot(h_ref[...], w_s[j], preferred_element_type=F32)


def _input_projection(x, norm_g, modseg, w_in, layer):
    n_tok = x.shape[0]
    tm = SEG_ROWS
    n_col = IN_W // COL_TILE
    src = jnp.asarray(_SRC_TILES, jnp.int32)
    grid_spec = pltpu.PrefetchScalarGridSpec(
        num_scalar_prefetch=1,
        grid=(n_tok // tm, n_col),
        in_specs=[pl.BlockSpec((tm, D_MODEL), lambda i, j, s: (i, 0)),
                  pl.BlockSpec((1, 1, D_MODEL), lambda i, j, s: (layer, 0, 0)),
                  pl.BlockSpec((1, N_MOD, D_MODEL), lambda i, j, s: (i, 0, 0)),
                  pl.BlockSpec((1, D_MODEL, COL_TILE),
                               lambda i, j, s: (layer, 0, s[jnp.where(i == 0, j, n_col - 1)]))],
        out_specs=pl.BlockSpec((tm, COL_TILE), lambda i, j, s: (i, j)),
        scratch_shapes=[pltpu.VMEM((tm, D_MODEL), BF16), pltpu.VMEM((n_col, D_MODEL, COL_TILE), BF16)],
    )
    return pl.pallas_call(
        _inproj_kernel,
        grid_spec=grid_spec,
        out_shape=jax.ShapeDtypeStruct((n_tok, IN_W), F32),
        compiler_params=_cparams(("arbitrary", "arbitrary"), 40),
        name="input_projection",
    )(src, x, norm_g.reshape(-1, 1, D_MODEL), modseg, w_in)


def _head_rms(x, g):
    lane = lax.broadcasted_iota(jnp.int32, x.shape, 1)
    lo = lane < HEAD_DIM
    xx = x * x
    s_lo = jnp.sum(jnp.where(lo, xx, 0.0), axis=-1, keepdims=True)
    s_hi = jnp.sum(jnp.where(lo, 0.0, xx), axis=-1, keepdims=True)
    inv = jnp.where(lo, lax.rsqrt(s_lo * (1.0 / HEAD_DIM) + EPS),
                    lax.rsqrt(s_hi * (1.0 / HEAD_DIM) + EPS))
    return x * inv * g


def _rope(x, cos, sin_a, sin_b):
    q = HEAD_DIM // 4
    return x * cos + pltpu.roll(x, LANES - q, 1) * sin_a + pltpu.roll(x, q, 1) * sin_b


def _attn_kernel(*refs, t_len, use_ctx):
    if use_ctx:
        (q_ref, k_ref, v_ref, qg_ref, kg_ref, cos_q_ref, sa_q_ref, sb_q_ref,
         cos_k_ref, sa_k_ref, sb_k_ref, ck_ref, cv_ref, o_ref, kall_ref, vall_ref) = refs
    else:
        (q_ref, k_ref, v_ref, qg_ref, kg_ref, o_ref, ko_ref, vo_ref, kall_ref, vall_ref) = refs

    @pl.when(pl.program_id(1) == 0)
    def _():
        k = _head_rms(k_ref[...], kg_ref[...])
        if use_ctx:
            k = _rope(k, cos_k_ref[...], sa_k_ref[...], sb_k_ref[...])
            kall_ref[0:t_len, :] = k.astype(BF16)
            vall_ref[0:t_len, :] = v_ref[...].astype(BF16)
            kall_ref[t_len:, :] = ck_ref[0, 0].astype(BF16)
            vall_ref[t_len:, :] = cv_ref[0, 0].astype(BF16)
        else:
            ko_ref[0] = k
            vo_ref[0] = v_ref[...]
            kall_ref[...] = k.astype(BF16)
            vall_ref[...] = v_ref[...].astype(BF16)

    heads = []
    for c in range(ATTN_W // LANES):
        qc = _head_rms(q_ref[:, c * LANES:(c + 1) * LANES], qg_ref[...])
        if use_ctx:
            qc = _rope(qc, cos_q_ref[...], sa_q_ref[...], sb_q_ref[...])
        qc = (qc * (HEAD_DIM ** -0.5)).astype(BF16)
        heads.append(qc[:, :HEAD_DIM])
        heads.append(qc[:, HEAD_DIM:])
    tq = q_ref.shape[0]
    for g in range(N_KV_HEADS):
        qs = jnp.concatenate(heads[g * KV_GROUP:(g + 1) * KV_GROUP], axis=0)
        kh = kall_ref[:, g * HEAD_DIM:(g + 1) * HEAD_DIM]
        vh = vall_ref[:, g * HEAD_DIM:(g + 1) * HEAD_DIM]
        s = _mm_nt(qs, kh)
        m = jnp.max(s, axis=-1, keepdims=True)
        p = jnp.exp(s - m)
        den = jnp.sum(p, axis=-1, keepdims=True)
        o = _mm(p, vh) / den
        for j in range(KV_GROUP):
            hh = g * KV_GROUP + j
            o_ref[:, hh * HEAD_DIM:(hh + 1) * HEAD_DIM] = o[j * tq:(j + 1) * tq].astype(BF16)


def _attention(z, q_g, k_g, layer, n_batch, t_len, row0, rope=None, cache=None):
    use_ctx = cache is not None
    nq = t_len // Q_TILE
    rb0 = row0 // t_len
    qb0 = row0 // Q_TILE
    qg = jnp.tile(q_g[layer], 2).reshape(1, LANES)
    kg = jnp.tile(k_g[layer], 2).reshape(1, LANES)
    n_out_tok = n_batch * t_len
    vec = pl.BlockSpec((1, LANES), lambda b, i: (0, 0))
    in_specs = [pl.BlockSpec((Q_TILE, ATTN_W), lambda b, i: (qb0 + b * nq + i, Z_Q // ATTN_W)),
                pl.BlockSpec((t_len, KV_W), lambda b, i: (rb0 + b, Z_K // KV_W)),
                pl.BlockSpec((t_len, KV_W), lambda b, i: (rb0 + b, Z_V // KV_W)),
                vec, vec]
    args = [z, z, z, qg, kg]
    out_specs = [pl.BlockSpec((Q_TILE, ATTN_W), lambda b, i: (b * nq + i, 0))]
    out_shape = [jax.ShapeDtypeStruct((n_out_tok, ATTN_W), BF16)]
    t_keys = t_len
    if use_ctx:
        cos, sin_a, sin_b = rope
        cache_k, cache_v = cache
        past = cache_k.shape[2]
        t_keys = t_len + past
        tq_spec = pl.BlockSpec((Q_TILE, LANES), lambda b, i: (i, 0))
        tk_spec = pl.BlockSpec((t_len, LANES), lambda b, i: (0, 0))
        c_spec = pl.BlockSpec((1, 1, past, KV_W), lambda b, i: (b, layer, 0, 0))
        in_specs += [tq_spec, tq_spec, tq_spec, tk_spec, tk_spec, tk_spec, c_spec, c_spec]
        args += [cos, sin_a, sin_b, cos, sin_a, sin_b,
                 cache_k.reshape(cache_k.shape[0], cache_k.shape[1], past, KV_W),
                 cache_v.reshape(cache_v.shape[0], cache_v.shape[1], past, KV_W)]
    else:
        kv_spec = pl.BlockSpec((1, t_len, KV_W), lambda b, i: (b, 0, 0))
        out_specs += [kv_spec, kv_spec]
        out_shape += [jax.ShapeDtypeStruct((n_batch, t_len, KV_W), F32)] * 2
    return pl.pallas_call(
        functools.partial(_attn_kernel, t_len=t_len, use_ctx=use_ctx),
        grid=(n_batch, nq),
        in_specs=in_specs,
        out_specs=out_specs,
        out_shape=out_shape,
        scratch_shapes=[pltpu.VMEM((t_keys, KV_W), BF16), pltpu.VMEM((t_keys, KV_W), BF16)],
        compiler_params=_cparams(("arbitrary", "arbitrary"), 48),
        name="attention_ctx" if use_ctx else "attention",
    )(*args)


def _shift_rows(x, d):
    n = x.shape[0]
    row = lax.broadcasted_iota(jnp.int32, x.shape, 0)
    y = pltpu.roll(x, d % n, 0)
    if d > 0:
        return jnp.where(row >= d, y, 0.0)
    return jnp.where(row < n + d, y, 0.0)


def _linear_scan(a, u, reverse):
    n = a.shape[0]
    row = lax.broadcasted_iota(jnp.int32, a.shape, 0)
    d = 1
    while d < n:
        if reverse:
            keep = row < n - d
            a_s = jnp.where(keep, pltpu.roll(a, n - d, 0), 1.0)
            u_s = jnp.where(keep, pltpu.roll(u, n - d, 0), 0.0)
        else:
            keep = row >= d
            a_s = jnp.where(keep, pltpu.roll(a, d, 0), 1.0)
            u_s = jnp.where(keep, pltpu.roll(u, d, 0), 0.0)
        u = a * u_s + u
        a = a * a_s
        d *= 2
    return a, u


def _rglru_kernel(*refs, use_ctx):
    if use_ctx:
        (x_ref, y_ref, cw_ref, cb_ref, wg_ref, bg_ref, lam_ref, h0_ref, o_ref) = refs
    else:
        (x_ref, y_ref, cw_ref, cb_ref, wg_ref, bg_ref, lam_ref, o_ref, last_ref) = refs
    x = x_ref[...]
    cw = cw_ref[0]
    xr = cb_ref[0] + _shift_rows(x, 1) * cw[0:1] + x * cw[1:2] \
        + _shift_rows(x, -1) * cw[2:3] + _shift_rows(x, -2) * cw[3:4]
    gates = _sigmoid(_mm(xr, wg_ref[0]) + bg_ref[0])
    n = x.shape[0]
    total = None
    lasts = []
    for d in range(2):
        r = gates[:, (2 * d) * D_RNN:(2 * d + 1) * D_RNN]
        i = gates[:, (2 * d + 1) * D_RNN:(2 * d + 2) * D_RNN]
        log_a = (-RG_C * _softplus(-lam_ref[0, d:d + 1, :])) * r
        a = jnp.exp(log_a)
        u = jnp.sqrt(1.0 - jnp.exp(2.0 * log_a)) * (i * xr)
        a_cum, h = _linear_scan(a, u, reverse=(d == 1))
        if use_ctx:
            h = h + a_cum * h0_ref[0, 0, d:d + 1, :]
        else:
            lasts.append(h[0:1] if d == 1 else h[n - 1:n])
        total = h if total is None else total + h
    o_ref[...] = (jax.nn.gelu(y_ref[...]) * total).astype(BF16)
    if not use_ctx:
        last_ref[0] = jnp.concatenate(lasts, axis=0)


def _rglru(z, conv_w, conv_b, w_gate, b_gate, lam, layer, n_batch, t_len, row0, state=None):
    use_ctx = state is not None
    rb0 = row0 // t_len
    in_specs = [pl.BlockSpec((t_len, D_RNN), lambda b: (rb0 + b, Z_RG // D_RNN)),
                pl.BlockSpec((t_len, D_RNN), lambda b: (rb0 + b, Z_RG // D_RNN + 1)),
                pl.BlockSpec((1, 4, D_RNN), lambda b: (layer, 0, 0)),
                pl.BlockSpec((1, 1, D_RNN), lambda b: (layer, 0, 0)),
                pl.BlockSpec((1, D_RNN, 4 * D_RNN), lambda b: (layer, 0, 0)),
                pl.BlockSpec((1, 1, 4 * D_RNN), lambda b: (layer, 0, 0)),
                pl.BlockSpec((1, 2, D_RNN), lambda b: (layer, 0, 0))]
    args = [z, z, conv_w, conv_b.reshape(-1, 1, D_RNN), w_gate, b_gate, lam]
    out_specs = [pl.BlockSpec((t_len, D_RNN), lambda b: (b, 0))]
    out_shape = [jax.ShapeDtypeStruct((n_batch * t_len, D_RNN), BF16)]
    if use_ctx:
        in_specs.append(pl.BlockSpec((1, 1, 2, D_RNN), lambda b: (b, layer, 0, 0)))
        args.append(state)
    else:
        out_specs.append(pl.BlockSpec((1, 2, D_RNN), lambda b: (b, 0, 0)))
        out_shape.append(jax.ShapeDtypeStruct((n_batch, 2, D_RNN), F32))
    return pl.pallas_call(
        functools.partial(_rglru_kernel, use_ctx=use_ctx),
        grid=(n_batch,),
        in_specs=in_specs,
        out_specs=out_specs,
        out_shape=out_shape,
        compiler_params=_cparams(("arbitrary",), 56),
        name="rglru_ctx" if use_ctx else "rglru",
    )(*args)


def _cumsum_rows(g, reverse):
    n = g.shape[0]
    row = lax.broadcasted_iota(jnp.int32, g.shape, 0)
    d = 1
    while d < n:
        if reverse:
            g = g + jnp.where(row < n - d, pltpu.roll(g, n - d, 0), 0.0)
        else:
            g = g + jnp.where(row >= d, pltpu.roll(g, d, 0), 0.0)
        d *= 2
    return g


def _block_row(gc, blk, r):
    n, w = gc.shape
    if blk >= 8:
        g3 = gc.reshape(n // blk, blk, w)
        return jnp.broadcast_to(g3[:, r:r + 1, :], (n // blk, blk, w)).reshape(n, w)
    row = lax.broadcasted_iota(jnp.int32, gc.shape, 0)
    res = row & (blk - 1)
    out = gc
    for m in range(blk):
        if m != r:
            out = jnp.where(res == m, pltpu.roll(gc, (m - r) % n, 0), out)
    return out


def _hgrn_gates(f_logit, lb):
    e = jnp.log1p(jnp.exp(-jnp.abs(f_logit)))
    log_sig = jnp.minimum(f_logit, 0.0) - e
    a = jnp.log(lb)
    b = jnp.log1p(-lb) + log_sig
    log_f = jnp.maximum(a, b) + jnp.log1p(jnp.exp(-jnp.abs(a - b)))
    k = (1.0 - lb) * jnp.exp(-jnp.maximum(f_logit, 0.0) - e)
    return log_f, k


def _hgrn_chunk_state(gc, k, v, reverse):
    n = gc.shape[0]
    g_tot = gc[0:1] if reverse else gc[n - 1:n]
    k_end = k * jnp.exp(jnp.minimum(g_tot - gc, 0.0))
    ds = [_mm_tn(k_end[:, h * HG_DK:(h + 1) * HG_DK], v[:, h * HG_DV:(h + 1) * HG_DV])
          for h in range(HG_HEADS)]
    return ds, jnp.exp(g_tot)


def _decay_state(s, e_row):
    eye = (lax.broadcasted_iota(jnp.int32, (HG_DK, HG_DK), 0)
           == lax.broadcasted_iota(jnp.int32, (HG_DK, HG_DK), 1))
    col = jnp.sum(jnp.where(eye, jnp.broadcast_to(e_row, (HG_DK, HG_DK)), 0.0), axis=1, keepdims=True)
    return s * col


def _hgrn_intra_scores(q, k, gc, reverse, txs, t_gt_s):
    n = q.shape[0]
    scores = [None] * HG_HEADS
    w = 1
    while w < n:
        blk = 2 * w
        m = _block_row(gc, blk, w if reverse else w - 1)
        qt = q * jnp.exp(jnp.minimum(gc - m, 0.0))
        kt = k * jnp.exp(jnp.minimum(m - gc, 0.0))
        sel = (txs >= w) & (txs < blk) & (t_gt_s != reverse)
        for h in range(HG_HEADS):
            s = _mm_nt(qt[:, h * HG_DK:(h + 1) * HG_DK], kt[:, h * HG_DK:(h + 1) * HG_DK])
            s = jnp.where(sel, s, 0.0)
            scores[h] = s if scores[h] is None else scores[h] + s
        w = blk
    return scores


def _hgrn_kernel(*refs, t_len, use_ctx):
    if use_ctx:
        (q_ref, ff_ref, fb_ref, i_ref, og_ref, lb_ref, ng_ref, s0_ref, o_ref, sin_ref) = refs
    else:
        (q_ref, ff_ref, fb_ref, i_ref, og_ref, lb_ref, ng_ref, o_ref, sl_ref) = refs
    c = HG_CHUNK
    n_chunks = t_len // c
    f_refs = (ff_ref, fb_ref)

    def load_dir(d, r0):
        log_f, k = _hgrn_gates(f_refs[d][pl.ds(r0, c), :], lb_ref[0, d:d + 1, :])
        return _cumsum_rows(log_f, reverse=(d == 1)), k

    if use_ctx:
        for d in range(2):
            states = [s0_ref[0, 0, d, h] for h in range(HG_HEADS)]
            order = range(n_chunks) if d == 0 else range(n_chunks - 1, -1, -1)
            for ci in order:
                for h in range(HG_HEADS):
                    sin_ref[d, ci, h] = states[h]
                gc, k = load_dir(d, ci * c)
                ds, e_tot = _hgrn_chunk_state(gc, k, i_ref[pl.ds(ci * c, c), :], reverse=(d == 1))
                states = [_decay_state(states[h], e_tot[:, h * HG_DK:(h + 1) * HG_DK]) + ds[h]
                          for h in range(HG_HEADS)]

    ti = lax.broadcasted_iota(jnp.int32, (c, c), 0)
    si = lax.broadcasted_iota(jnp.int32, (c, c), 1)
    txs = ti ^ si
    t_gt_s = ti > si
    diag = ti == si

    def chunk_body(ci, carry):
        r0 = pl.multiple_of(ci * c, c)
        hq = q_ref[pl.ds(r0, c), :]
        q = hq * _sigmoid(hq)
        v = i_ref[pl.ds(r0, c), :]
        scores = None
        inter = [None] * HG_HEADS
        k_sum = None
        for d in range(2):
            gc, k = load_dir(d, r0)
            sc = _hgrn_intra_scores(q, k, gc, d == 1, txs, t_gt_s)
            scores = sc if scores is None else [a + b for a, b in zip(scores, sc)]
            k_sum = k if k_sum is None else k_sum + k
            if use_ctx:
                qe = q * jnp.exp(gc)
                for h in range(HG_HEADS):
                    t = _mm(qe[:, h * HG_DK:(h + 1) * HG_DK], sin_ref[d, ci, h])
                    inter[h] = t if inter[h] is None else inter[h] + t
            else:
                ds, _ = _hgrn_chunk_state(gc, k, v, reverse=(d == 1))
                for h in range(HG_HEADS):
                    sl_ref[0, d, h] = ds[h]
        outs = []
        for h in range(HG_HEADS):
            hs = slice(h * HG_DK, (h + 1) * HG_DK)
            s_diag = jnp.where(diag, _mm_nt(q[:, hs], k_sum[:, hs]), 0.0)
            o = _mm(scores[h] + s_diag, v[:, h * HG_DV:(h + 1) * HG_DV])
            if use_ctx:
                o = o + inter[h]
            o = o * lax.rsqrt(jnp.mean(o * o, axis=-1, keepdims=True) + EPS) * ng_ref[0]
            outs.append(o)
        og = og_ref[pl.ds(r0, c), :]
        o_all = jnp.concatenate(outs, axis=1) * (og * _sigmoid(og))
        o_ref[pl.ds(r0, c), :] = o_all.astype(BF16)
        return carry

    if n_chunks == 1:
        chunk_body(0, 0)
    else:
        lax.fori_loop(0, n_chunks, chunk_body, 0)


def _hgrn(z, lb, norm_g, layer, n_batch, t_len, row0, state=None):
    use_ctx = state is not None
    rb0 = row0 // t_len
    cb = Z_HG // HG_W

    def zspec(k):
        return pl.BlockSpec((t_len, HG_W), lambda b: (rb0 + b, cb + k))

    in_specs = [zspec(0), zspec(1), zspec(2), zspec(3), zspec(4),
                pl.BlockSpec((1, 2, HG_W), lambda b: (layer, 0, 0)),
                pl.BlockSpec((1, 1, HG_DV), lambda b: (layer, 0, 0))]
    args = [z, z, z, z, z, lb, norm_g.reshape(-1, 1, HG_DV)]
    out_specs = [pl.BlockSpec((t_len, HG_W), lambda b: (b, 0))]
    out_shape = [jax.ShapeDtypeStruct((n_batch * t_len, HG_W), BF16)]
    scratch = []
    if use_ctx:
        in_specs.append(pl.BlockSpec((1, 1, 2, HG_HEADS, HG_DK, HG_DV), lambda b: (b, layer, 0, 0, 0, 0)))
        args.append(state)
        scratch.append(pltpu.VMEM((2, t_len // HG_CHUNK, HG_HEADS, HG_DK, HG_DV), F32))
    else:
        out_specs.append(pl.BlockSpec((1, 2, HG_HEADS, HG_DK, HG_DV), lambda b: (b, 0, 0, 0, 0)))
        out_shape.append(jax.ShapeDtypeStruct((n_batch, 2, HG_HEADS, HG_DK, HG_DV), F32))
    return pl.pallas_call(
        functools.partial(_hgrn_kernel, t_len=t_len, use_ctx=use_ctx),
        grid=(n_batch,),
        in_specs=in_specs,
        out_specs=out_specs,
        out_shape=out_shape,
        scratch_shapes=scratch,
        compiler_params=_cparams(("arbitrary",), 56),
        name="hgrn_ctx" if use_ctx else "hgrn",
    )(*args)


def _merge_kernel(a_ref, r_ref, h_ref, g0_ref, g1_ref, g2_ref, x_ref, mod_ref,
                  wa_ref, wr_ref, wh_ref, wo_ref, o_ref, wa_s, wr_s, wh_s, wo_s):
    @pl.when(pl.program_id(0) == 0)
    def _():
        wa_s[...] = wa_ref[0].astype(BF16)
        wr_s[...] = wr_ref[0].astype(BF16)
        wh_s[...] = wh_ref[0].astype(BF16)
        wo_s[...] = wo_ref[0].astype(BF16)

    def proj(x_r, w_s):
        return jnp.dot(x_r[...], w_s[...], preferred_element_type=F32)

    merged = (_sigmoid(g0_ref[...]) * proj(a_ref, wa_s)
              + _sigmoid(g1_ref[...]) * proj(r_ref, wr_s)
              + _sigmoid(g2_ref[...]) * proj(h_ref, wh_s))
    out = jnp.dot(merged.astype(BF16), wo_s[...], preferred_element_type=F32)
    o_ref[...] = x_ref[...] + mod_ref[0, 2:3, :] * out


def _merge_project(attn, rnn, hg, z, x, modseg, w_attn_o, w_rnn_o, w_hgrn_o, w_out, layer):
    n_tok = x.shape[0]
    tm = 512
    per_seg = SEG_ROWS // tm
    half = pl.BlockSpec((tm, ATTN_W), lambda i: (i, 0))
    full = pl.BlockSpec((tm, D_MODEL), lambda i: (i, 0))

    def gspec(k):
        return pl.BlockSpec((tm, D_MODEL), lambda i: (i, Z_GL // D_MODEL + k))

    def wspec(rows):
        return pl.BlockSpec((1, rows, D_MODEL), lambda i: (layer, 0, 0))

    return pl.pallas_call(
        _merge_kernel,
        grid=(n_tok // tm,),
        in_specs=[half, half, half, gspec(0), gspec(1), gspec(2), full,
                  pl.BlockSpec((1, N_MOD, D_MODEL), lambda i: (i // per_seg, 0, 0)),
                  wspec(ATTN_W), wspec(D_RNN), wspec(HG_W), wspec(D_MODEL)],
        out_specs=full,
        out_shape=jax.ShapeDtypeStruct((n_tok, D_MODEL), F32),
        scratch_shapes=[pltpu.VMEM((ATTN_W, D_MODEL), BF16), pltpu.VMEM((D_RNN, D_MODEL), BF16),
                        pltpu.VMEM((HG_W, D_MODEL), BF16), pltpu.VMEM((D_MODEL, D_MODEL), BF16)],
        compiler_params=_cparams(("arbitrary",), 56),
        name="merge_project",
    )(attn, rnn, hg, z, z, z, x, modseg, w_attn_o, w_rnn_o, w_hgrn_o, w_out)


def _router_kernel(x_ref, g_ref, mod_ref, wr_ref, br_ref, h_ref, id_ref, wt_ref, pr_ref, cnt_ref, run_ref):
    @pl.when(pl.program_id(0) == 0)
    def _():
        run_ref[...] = jnp.zeros_like(run_ref)

    h = _modulated_norm(x_ref[...], g_ref[0], mod_ref[0, 4:5, :], mod_ref[0, 3:4, :])
    h_ref[...] = h
    logits = jnp.dot(h, wr_ref[0], preferred_element_type=F32,
                     precision=lax.Precision.HIGHEST) + br_ref[0]
    tm = logits.shape[0]
    lane = lax.broadcasted_iota(jnp.int32, logits.shape, 1)
    ids, vals = [], []
    chosen = jnp.zeros(logits.shape, jnp.bool_)
    work = logits
    for _ in range(TOP_K):
        m = jnp.max(work, axis=-1, keepdims=True)
        idx = jnp.min(jnp.where(work == m, lane, LANES), axis=-1, keepdims=True)
        hit = lane == idx
        chosen = chosen | hit
        work = jnp.where(hit, -jnp.inf, work)
        ids.append(idx)
        vals.append(m)
    exps = [jnp.exp(v - vals[0]) for v in vals]
    den = exps[0] + exps[1] + exps[2] + exps[3]
    ind = jnp.where(chosen, 1.0, 0.0)
    r_i = lax.broadcasted_iota(jnp.int32, (tm, tm), 0)
    c_i = lax.broadcasted_iota(jnp.int32, (tm, tm), 1)
    before = _mm(jnp.where(c_i < r_i, 1.0, 0.0), ind) + run_ref[0:1, :]
    id_out = jnp.zeros(logits.shape, jnp.int32)
    wt_out = jnp.zeros(logits.shape, F32)
    pr_out = jnp.zeros(logits.shape, jnp.int32)
    for k in range(TOP_K):
        prior = jnp.sum(jnp.where(lane == ids[k], before, 0.0), axis=-1, keepdims=True)
        id_out = jnp.where(lane == k, ids[k], id_out)
        wt_out = jnp.where(lane == k, exps[k] / den, wt_out)
        pr_out = jnp.where(lane == k, prior.astype(jnp.int32), pr_out)
    id_ref[...] = id_out
    wt_ref[...] = wt_out
    pr_ref[...] = pr_out
    total = run_ref[...] + jnp.sum(ind, axis=0, keepdims=True)
    run_ref[...] = total
    cnt_ref[...] = total


def _router(x, norm_g, modseg, w_router, b_router, layer):
    n_tok = x.shape[0]
    tm = 512
    per_seg = SEG_ROWS // tm
    depth = w_router.shape[0]
    w_pad = jnp.zeros((depth, D_MODEL, LANES), F32).at[:, :, :N_EXPERTS].set(w_router)
    b_pad = jnp.full((depth, 1, LANES), NEG_BIG, F32).at[:, 0, :N_EXPERTS].set(b_router)
    full = pl.BlockSpec((tm, D_MODEL), lambda i: (i, 0))
    small = pl.BlockSpec((tm, LANES), lambda i: (i, 0))
    return pl.pallas_call(
        _router_kernel,
        grid=(n_tok // tm,),
        in_specs=[full,
                  pl.BlockSpec((1, 1, D_MODEL), lambda i: (layer, 0, 0)),
                  pl.BlockSpec((1, N_MOD, D_MODEL), lambda i: (i // per_seg, 0, 0)),
                  pl.BlockSpec((1, D_MODEL, LANES), lambda i: (layer, 0, 0)),
                  pl.BlockSpec((1, 1, LANES), lambda i: (layer, 0, 0))],
        out_specs=[full, small, small, small, pl.BlockSpec((8, LANES), lambda i: (0, 0))],
        out_shape=[jax.ShapeDtypeStruct((n_tok, D_MODEL), F32),
                   jax.ShapeDtypeStruct((n_tok, LANES), jnp.int32),
                   jax.ShapeDtypeStruct((n_tok, LANES), F32),
                   jax.ShapeDtypeStruct((n_tok, LANES), jnp.int32),
                   jax.ShapeDtypeStruct((8, LANES), F32)],
        scratch_shapes=[pltpu.VMEM((8, LANES), F32)],
        compiler_params=_cparams(("arbitrary",), 40),
        name="router",
    )(x, norm_g.reshape(-1, 1, D_MODEL), modseg, w_pad, b_pad)


def _dispatch_kernel(pos_ref, h_ref, init_ref, xs_ref, sem):
    del init_ref

    def row_copy(t, k):
        return pltpu.make_async_copy(h_ref.at[pl.ds(t, 1)],
                                     xs_ref.at[pl.ds(pos_ref[t * TOP_K + k], 1)], sem)

    def start(t, carry):
        for k in range(TOP_K):
            row_copy(t, k).start()
        return carry

    def wait(t, carry):
        for k in range(TOP_K):
            row_copy(t, k).wait()
        return carry

    lax.fori_loop(0, TOKEN_TILE, start, 0)
    lax.fori_loop(0, TOKEN_TILE, wait, 0)


def _dispatch(h, pos_flat, n_rows):
    n_tok = h.shape[0]
    init = jnp.zeros((n_rows, D_MODEL), F32)
    return pl.pallas_call(
        _dispatch_kernel,
        grid=(n_tok // TOKEN_TILE,),
        in_specs=[pl.BlockSpec((TOKEN_TILE * TOP_K,), lambda i: (i,), memory_space=pltpu.SMEM),
                  pl.BlockSpec((TOKEN_TILE, D_MODEL), lambda i: (i, 0)),
                  pl.BlockSpec(memory_space=pl.ANY)],
        out_specs=pl.BlockSpec(memory_space=pl.ANY),
        out_shape=jax.ShapeDtypeStruct((n_rows, D_MODEL), F32),
        scratch_shapes=[pltpu.SemaphoreType.DMA(())],
        input_output_aliases={2: 0},
        compiler_params=_cparams(("arbitrary",), 32),
        name="dispatch",
    )(pos_flat, h, init)


def _combine_kernel(pos_ref, ys_ref, wt_ref, x_ref, mod_ref, o_ref, buf, sem):
    def row_copy(t, k):
        return pltpu.make_async_copy(ys_ref.at[pl.ds(pos_ref[t * TOP_K + k], 1)],
                                     buf.at[k, pl.ds(t, 1)], sem)

    def start(t, carry):
        for k in range(TOP_K):
            row_copy(t, k).start()
        return carry

    def wait(t, carry):
        for k in range(TOP_K):
            row_copy(t, k).wait()
        return carry

    lax.fori_loop(0, TOKEN_TILE, start, 0)
    lax.fori_loop(0, TOKEN_TILE, wait, 0)
    acc = wt_ref[:, 0:1] * buf[0]
    for k in range(1, TOP_K):
        acc = acc + wt_ref[:, k:k + 1] * buf[k]
    o_ref[...] = x_ref[...] + mod_ref[0, 5:6, :] * acc


def _combine(ys, pos_flat, wts, x, modseg):
    n_tok = x.shape[0]
    per_seg = SEG_ROWS // TOKEN_TILE
    full = pl.BlockSpec((TOKEN_TILE, D_MODEL), lambda i: (i, 0))
    return pl.pallas_call(
        _combine_kernel,
        grid=(n_tok // TOKEN_TILE,),
        in_specs=[pl.BlockSpec((TOKEN_TILE * TOP_K,), lambda i: (i,), memory_space=pltpu.SMEM),
                  pl.BlockSpec(memory_space=pl.ANY),
                  pl.BlockSpec((TOKEN_TILE, LANES), lambda i: (i, 0)),
                  full,
                  pl.BlockSpec((1, N_MOD, D_MODEL), lambda i: (i // per_seg, 0, 0))],
        out_specs=full,
        out_shape=jax.ShapeDtypeStruct((n_tok, D_MODEL), F32),
        scratch_shapes=[pltpu.VMEM((TOP_K, TOKEN_TILE, D_MODEL), F32), pltpu.SemaphoreType.DMA(())],
        compiler_params=_cparams(("arbitrary",), 40),
        name="combine",
    )(pos_flat, ys, wts, x, modseg)


def _expert_kernel(te_ref, na_ref, x_ref, wgu_ref, bgu_ref, wd_ref, bd_ref, o_ref, wgu_s, wd_s):
    i = pl.program_id(0)
    active = i < na_ref[0]
    fresh = jnp.logical_or(i == 0, te_ref[i] != te_ref[jnp.maximum(i - 1, 0)])

    @pl.when(jnp.logical_and(active, fresh))
    def _():
        wgu_s[...] = wgu_ref[0, 0].astype(BF16)
        wd_s[...] = wd_ref[0, 0].astype(BF16)

    @pl.when(active)
    def _():
        gu = jnp.dot(x_ref[...].astype(BF16), wgu_s[...], preferred_element_type=F32) + bgu_ref[0, 0]
        gate = jnp.minimum(gu[:, :D_FF], SWIGLU_LIMIT)
        up = jnp.clip(gu[:, D_FF:], -SWIGLU_LIMIT, SWIGLU_LIMIT)
        act = gate * _sigmoid(SWIGLU_ALPHA * gate) * (up + 1.0)
        o_ref[...] = jnp.dot(act.astype(BF16), wd_s[...], preferred_element_type=F32) + bd_ref[0, 0]

    @pl.when(jnp.logical_not(active))
    def _():
        o_ref[...] = jnp.zeros_like(o_ref)


def _experts(xs, tile_expert, n_active, w_gu, b_gu, w_down, b_down, layer):
    n_rows = xs.shape[0]
    depth = w_gu.shape[0]
    tm = EXPERT_TILE
    grid_spec = pltpu.PrefetchScalarGridSpec(
        num_scalar_prefetch=2,
        grid=(n_rows // tm,),
        in_specs=[pl.BlockSpec((tm, D_MODEL), lambda i, te, na: (i, 0)),
                  pl.BlockSpec((1, 1, D_MODEL, 2 * D_FF), lambda i, te, na: (layer, te[i], 0, 0)),
                  pl.BlockSpec((1, 1, 1, 2 * D_FF), lambda i, te, na: (layer, te[i], 0, 0)),
                  pl.BlockSpec((1, 1, D_FF, D_MODEL), lambda i, te, na: (layer, te[i], 0, 0)),
                  pl.BlockSpec((1, 1, 1, D_MODEL), lambda i, te, na: (layer, te[i], 0, 0))],
        out_specs=pl.BlockSpec((tm, D_MODEL), lambda i, te, na: (i, 0)),
        scratch_shapes=[pltpu.VMEM((D_MODEL, 2 * D_FF), BF16), pltpu.VMEM((D_FF, D_MODEL), BF16)],
    )
    return pl.pallas_call(
        _expert_kernel,
        grid_spec=grid_spec,
        out_shape=jax.ShapeDtypeStruct((n_rows, D_MODEL), F32),
        compiler_params=_cparams(("arbitrary",), 56),
        name="experts",
    )(tile_expert, n_active, xs, w_gu, b_gu.reshape(depth, N_EXPERTS, 1, 2 * D_FF),
      w_down, b_down.reshape(depth, N_EXPERTS, 1, D_MODEL))


def _moe(x, norm_g, modseg, w_router, b_router, w_gu, b_gu, w_down, b_down, layer):
    n_tok = x.shape[0]
    h, ids, wts, prior, cnt = _router(x, norm_g, modseg, w_router, b_router, layer)
    tm = EXPERT_TILE
    n_rows = n_tok * TOP_K + N_EXPERTS * tm
    counts = cnt[0, :N_EXPERTS].astype(jnp.int32)
    padded = ((counts + tm - 1) // tm) * tm
    ends = jnp.cumsum(padded)
    starts = ends - padded
    pos = (starts[ids[:, :TOP_K]] + prior[:, :TOP_K]).reshape(-1)
    tile_row0 = jnp.arange(n_rows // tm, dtype=jnp.int32) * tm
    tile_expert = jnp.minimum(jnp.sum((ends[None, :] <= tile_row0[:, None]).astype(jnp.int32), axis=1),
                              N_EXPERTS - 1)
    n_active = (ends[-1:] // tm).astype(jnp.int32)
    xs = _dispatch(h, pos, n_rows)
    ys = _experts(xs, tile_expert, n_active, w_gu, b_gu, w_down, b_down, layer)
    return _combine(ys, pos, wts, x, modseg)


def _rope_tables(n_tok):
    rows = n_tok // GRID_W
    row = jnp.repeat(jnp.arange(rows, dtype=F32), GRID_W)
    col = jnp.tile(jnp.arange(GRID_W, dtype=F32), rows)
    quarter = HEAD_DIM // 4
    inv_freq = ROPE_BASE ** (-jnp.arange(quarter, dtype=F32) / quarter)
    ang_r = row[:, None] * inv_freq
    ang_c = col[:, None] * inv_freq
    ang = jnp.concatenate([ang_r, ang_r, ang_c, ang_c], axis=-1)
    cos, sin = jnp.cos(ang), jnp.sin(ang)
    first = (jnp.arange(HEAD_DIM) % (2 * quarter)) < quarter
    sin_a = jnp.where(first, -sin, 0.0)
    sin_b = jnp.where(first, 0.0, sin)
    return tuple(jnp.tile(t, (1, LANES // HEAD_DIM)) for t in (cos, sin_a, sin_b))


def _block_diag(w):
    eye = jnp.eye(RG_BLOCKS, dtype=w.dtype)
    return jnp.einsum("nde,nm->ndme", w, eye).reshape(D_RNN, D_RNN)


def kernel(x_prompt, x_sample, c, cache_k, cache_v, state_rglru, state_hgrn, c_ctx, w_mod, b_mod, norm1_g, norm2_g, w_in, q_norm_g, k_norm_g, conv_w, conv_b, rg_wa, rg_ba, rg_wx, rg_bx, rg_lambda, hgrn_lb_logits, hgrn_norm_g, w_attn_o, w_rnn_o, w_hgrn_o, w_out, w_router, b_router, w_gu, b_gu, w_down, b_down):
    depth = w_mod.shape[0]
    n_p, t_p = x_prompt.shape[0], x_prompt.shape[1]
    n_s, t_s = x_sample.shape[0], x_sample.shape[1]
    tok_p = n_p * t_p
    tok_s = n_s * t_s
    assert tok_p % SEG_ROWS == 0 and t_s == SEG_ROWS and SEG_ROWS % t_p == 0

    x = jnp.concatenate([x_prompt.reshape(tok_p, D_MODEL), x_sample.reshape(tok_s, D_MODEL)], axis=0)

    lb_all = jnp.cumsum(jax.nn.softmax(hgrn_lb_logits.astype(F32), axis=0), axis=0)
    lb_all = lb_all - lb_all[:1]
    rope = _rope_tables(t_s)

    cond8 = jnp.zeros((8, D_MODEL), F32).at[0].set(c_ctx).at[1:1 + n_s].set(c)
    mod = _modulation(cond8, w_mod, b_mod)
    seg_ids = jnp.asarray([0] * (tok_p // SEG_ROWS) + [1 + b for b in range(n_s)], jnp.int32)
    modsegs = mod[:, seg_ids].reshape(depth, seg_ids.shape[0], N_MOD, D_MODEL)

    w_gate = jnp.stack([jnp.concatenate([_block_diag(rg_wa[l, 0]), _block_diag(rg_wx[l, 0]),
                                         _block_diag(rg_wa[l, 1]), _block_diag(rg_wx[l, 1])], axis=1)
                        for l in range(depth)])
    b_gate = jnp.stack([jnp.concatenate([rg_ba[l, 0], rg_bx[l, 0], rg_ba[l, 1], rg_bx[l, 1]])[None]
                        for l in range(depth)])

    new_k, new_v, new_hr, new_s = [], [], [], []
    for l in range(depth):
        modseg = modsegs[l]
        z = _input_projection(x, norm1_g, modseg, w_in, l)
        attn_p, k_l, v_l = _attention(z, q_norm_g, k_norm_g, l, n_p, t_p, 0)
        (attn_s,) = _attention(z, q_norm_g, k_norm_g, l, n_s, t_s, tok_p, rope=rope,
                               cache=(cache_k, cache_v))
        rnn_p, hr_l = _rglru(z, conv_w, conv_b, w_gate, b_gate, rg_lambda, l, n_p, t_p, 0)
        (rnn_s,) = _rglru(z, conv_w, conv_b, w_gate, b_gate, rg_lambda, l, n_s, t_s, tok_p,
                          state=state_rglru)
        hg_p, s_l = _hgrn(z, lb_all, hgrn_norm_g, l, n_p, t_p, 0)
        (hg_s,) = _hgrn(z, lb_all, hgrn_norm_g, l, n_s, t_s, tok_p, state=state_hgrn)
        attn = jnp.concatenate([attn_p, attn_s], axis=0)
        rnn = jnp.concatenate([rnn_p, rnn_s], axis=0)
        hg = jnp.concatenate([hg_p, hg_s], axis=0)
        x = _merge_project(attn, rnn, hg, z, x, modseg, w_attn_o, w_rnn_o, w_hgrn_o, w_out, l)
        x = _moe(x, norm2_g, modseg, w_router, b_router, w_gu, b_gu, w_down, b_down, l)
        new_k.append(k_l.reshape(n_p, t_p, N_KV_HEADS, HEAD_DIM))
        new_v.append(v_l.reshape(n_p, t_p, N_KV_HEADS, HEAD_DIM))
        new_hr.append(hr_l)
        new_s.append(s_l)

    y_prompt = x[:tok_p].reshape(n_p, t_p, D_MODEL)
    y_sample = x[tok_p:].reshape(n_s, t_s, D_MODEL)
    return (y_prompt, y_sample, jnp.stack(new_k, axis=1), jnp.stack(new_v, axis=1),
            jnp.stack(new_hr, axis=1), jnp.stack(new_s, axis=1))
```

```python
import functools

import jax
import jax.numpy as jnp
from jax import lax
from jax.experimental import pallas as pl
from jax.experimental.pallas import tpu as pltpu

F32 = jnp.float32
BF16 = jnp.bfloat16

D_MODEL = 1024
GRID_W = 64
HEAD_DIM = 64
N_HEADS = 8
N_KV_HEADS = 2
KV_GROUP = N_HEADS // N_KV_HEADS
ATTN_W = N_HEADS * HEAD_DIM
KV_W = N_KV_HEADS * HEAD_DIM
ROPE_BASE = 10000.0
D_RNN = 512
RG_BLOCKS = 8
RG_BW = D_RNN // RG_BLOCKS
RG_C = 8.0
HG_HEADS = 8
HG_DK = 64
HG_DV = 64
HG_W = HG_HEADS * HG_DK
N_PAIRS = HG_W // 128
N_EXPERTS = 32
TOP_K = 4
D_FF = D_MODEL
SWIGLU_LIMIT = 7.0
SWIGLU_ALPHA = 1.702
N_MOD = 6
EPS = 1e-6
IN_W = ATTN_W + 2 * KV_W + 2 * D_RNN + 5 * HG_W + 3 * D_MODEL

LANES = 128
SEG_ROWS = 1024
COL_TILE = 256
HG_CHUNK = 256
Q_TILE = 256
EXPERT_TILE = 256
TOKEN_TILE = 256
NEG_BIG = -1e30

_SRC_TILES = tuple(range(17, 29)) + tuple(range(7, 17)) + tuple(range(3, 7)) + (0, 1, 2)
Z_GL = 0
Z_HG = 3 * D_MODEL
Z_RG = Z_HG + 5 * HG_W
Z_Q = Z_RG + 2 * D_RNN
Z_K = Z_Q + ATTN_W
Z_V = Z_K + KV_W
Z_W = -(-IN_W // (2 * COL_TILE)) * (2 * COL_TILE)


def _cparams(sem, vmem_mb):
    return pltpu.CompilerParams(dimension_semantics=sem, vmem_limit_bytes=vmem_mb * 1024 * 1024)


def _mm(a, b):
    return jnp.dot(a.astype(BF16), b.astype(BF16), preferred_element_type=F32)


def _mm_nt(a, b):
    return lax.dot_general(a.astype(BF16), b.astype(BF16), (((1,), (1,)), ((), ())),
                           preferred_element_type=F32)


def _mm_tn(a, b):
    return lax.dot_general(a.astype(BF16), b.astype(BF16), (((0,), (0,)), ((), ())),
                           preferred_element_type=F32)


def _sigmoid(x):
    return 1.0 / (1.0 + jnp.exp(-x))


def _softplus(x):
    return jnp.maximum(x, 0.0) + jnp.log1p(jnp.exp(-jnp.abs(x)))


def _modulated_norm(x, g, scale, shift):
    y = x * lax.rsqrt(jnp.mean(x * x, axis=-1, keepdims=True) + EPS)
    return (y * g) * (1.0 + scale) + shift


def _mod_kernel(c_ref, w_ref, b_ref, o_ref):
    c = c_ref[...]
    s = c * _sigmoid(c)
    o_ref[0] = _mm(s, w_ref[0]) + b_ref[0]


def _modulation(cond8, w_mod, b_mod):
    depth = w_mod.shape[0]
    tn = 1024
    return pl.pallas_call(
        _mod_kernel,
        grid=(depth, N_MOD * D_MODEL // tn),
        in_specs=[pl.BlockSpec((8, D_MODEL), lambda l, j: (0, 0)),
                  pl.BlockSpec((1, D_MODEL, tn), lambda l, j: (l, 0, j)),
                  pl.BlockSpec((1, 1, tn), lambda l, j: (l, 0, j))],
        out_specs=pl.BlockSpec((1, 8, tn), lambda l, j: (l, 0, j)),
        out_shape=jax.ShapeDtypeStruct((depth, 8, N_MOD * D_MODEL), F32),
        compiler_params=_cparams(("arbitrary", "arbitrary"), 32),
        name="modulation",
    )(cond8, w_mod, b_mod.reshape(depth, 1, N_MOD * D_MODEL))


def _inproj_kernel(src_ref, x_ref, g_ref, mod_ref, wa_ref, wb_ref, o_ref, h_ref, w_s):
    j = pl.program_id(1)

    @pl.when(j == 0)
    def _():
        h = _modulated_norm(x_ref[...], g_ref[0], mod_ref[0, 1:2, :], mod_ref[0, 0:1, :])
        h_ref[...] = h.astype(BF16)

    @pl.when(pl.program_id(0) == 0)
    def _():
        w_s[j, :, 0:COL_TILE] = wa_ref[0].astype(BF16)
        w_s[j, :, COL_TILE:2 * COL_TILE] = wb_ref[0].astype(BF16)

    o_ref[...] = jnp.dot(h_ref[...], w_s[j], preferred_element_type=F32)


def _input_projection(x, norm_g, modseg, w_in, layer):
    n_tok = x.shape[0]
    tm = SEG_ROWS
    n_col = Z_W // (2 * COL_TILE)
    src = jnp.asarray(_SRC_TILES + (_SRC_TILES[-1],) * (Z_W // COL_TILE - len(_SRC_TILES)), jnp.int32)

    def wspec(k):
        return pl.BlockSpec((1, D_MODEL, COL_TILE),
                            lambda i, j, s: (layer, 0, s[2 * jnp.where(i == 0, j, n_col - 1) + k]))

    grid_spec = pltpu.PrefetchScalarGridSpec(
        num_scalar_prefetch=1,
        grid=(n_tok // tm, n_col),
        in_specs=[pl.BlockSpec((tm, D_MODEL), lambda i, j, s: (i, 0)),
                  pl.BlockSpec((1, 1, D_MODEL), lambda i, j, s: (layer, 0, 0)),
                  pl.BlockSpec((1, N_MOD, D_MODEL), lambda i, j, s: (i, 0, 0)),
                  wspec(0), wspec(1)],
        out_specs=pl.BlockSpec((tm, 2 * COL_TILE), lambda i, j, s: (i, j)),
        scratch_shapes=[pltpu.VMEM((tm, D_MODEL), BF16),
                        pltpu.VMEM((n_col, D_MODEL, 2 * COL_TILE), BF16)],
    )
    return pl.pallas_call(
        _inproj_kernel,
        grid_spec=grid_spec,
        out_shape=jax.ShapeDtypeStruct((n_tok, Z_W), F32),
        compiler_params=_cparams(("arbitrary", "arbitrary"), 40),
        name="input_projection",
    )(src, x, norm_g.reshape(-1, 1, D_MODEL), modseg, w_in, w_in)


def _head_rms(x, g):
    lane = lax.broadcasted_iota(jnp.int32, x.shape, 1)
    lo = lane < HEAD_DIM
    xx = x * x
    s_lo = jnp.sum(jnp.where(lo, xx, 0.0), axis=-1, keepdims=True)
    s_hi = jnp.sum(jnp.where(lo, 0.0, xx), axis=-1, keepdims=True)
    inv = jnp.where(lo, lax.rsqrt(s_lo * (1.0 / HEAD_DIM) + EPS),
                    lax.rsqrt(s_hi * (1.0 / HEAD_DIM) + EPS))
    return x * inv * g


def _rope(x, cos, sin_a, sin_b):
    q = HEAD_DIM // 4
    return x * cos + pltpu.roll(x, LANES - q, 1) * sin_a + pltpu.roll(x, q, 1) * sin_b


def _attn_kernel(*refs, t_len, use_ctx):
    if use_ctx:
        (q_ref, k_ref, v_ref, qg_ref, kg_ref, cos_q_ref, sa_q_ref, sb_q_ref,
         cos_k_ref, sa_k_ref, sb_k_ref, ck_ref, cv_ref, o_ref, kall_ref, vall_ref) = refs
    else:
        (q_ref, k_ref, v_ref, qg_ref, kg_ref, o_ref, ko_ref, vo_ref, kall_ref, vall_ref) = refs

    @pl.when(pl.program_id(1) == 0)
    def _():
        k = _head_rms(k_ref[...], kg_ref[...])
        if use_ctx:
            k = _rope(k, cos_k_ref[...], sa_k_ref[...], sb_k_ref[...])
            kall_ref[0:t_len, :] = k.astype(BF16)
            vall_ref[0:t_len, :] = v_ref[...].astype(BF16)
            kall_ref[t_len:, :] = ck_ref[0, 0].astype(BF16)
            vall_ref[t_len:, :] = cv_ref[0, 0].astype(BF16)
        else:
            ko_ref[0] = k
            vo_ref[0] = v_ref[...]
            kall_ref[...] = k.astype(BF16)
            vall_ref[...] = v_ref[...].astype(BF16)

    heads = []
    for c in range(ATTN_W // LANES):
        qc = _head_rms(q_ref[:, c * LANES:(c + 1) * LANES], qg_ref[...])
        if use_ctx:
            qc = _rope(qc, cos_q_ref[...], sa_q_ref[...], sb_q_ref[...])
        qc = (qc * (HEAD_DIM ** -0.5)).astype(BF16)
        heads.append(qc[:, :HEAD_DIM])
        heads.append(qc[:, HEAD_DIM:])
    tq = q_ref.shape[0]
    for g in range(N_KV_HEADS):
        qs = jnp.concatenate(heads[g * KV_GROUP:(g + 1) * KV_GROUP], axis=0)
        kh = kall_ref[:, g * HEAD_DIM:(g + 1) * HEAD_DIM]
        vh = vall_ref[:, g * HEAD_DIM:(g + 1) * HEAD_DIM]
        s = _mm_nt(qs, kh)
        m = jnp.max(s, axis=-1, keepdims=True)
        p = jnp.exp(s - m)
        den = jnp.sum(p, axis=-1, keepdims=True)
        o = _mm(p, vh) / den
        for j in range(KV_GROUP):
            hh = g * KV_GROUP + j
            o_ref[:, hh * HEAD_DIM:(hh + 1) * HEAD_DIM] = o[j * tq:(j + 1) * tq].astype(BF16)


def _attention(z, q_g, k_g, layer, n_batch, t_len, row0, rope=None, cache=None):
    use_ctx = cache is not None
    nq = t_len // Q_TILE
    rb0 = row0 // t_len
    qb0 = row0 // Q_TILE
    qg = jnp.tile(q_g[layer], 2).reshape(1, LANES)
    kg = jnp.tile(k_g[layer], 2).reshape(1, LANES)
    n_out_tok = n_batch * t_len
    vec = pl.BlockSpec((1, LANES), lambda b, i: (0, 0))
    in_specs = [pl.BlockSpec((Q_TILE, ATTN_W), lambda b, i: (qb0 + b * nq + i, Z_Q // ATTN_W)),
                pl.BlockSpec((t_len, KV_W), lambda b, i: (rb0 + b, Z_K // KV_W)),
                pl.BlockSpec((t_len, KV_W), lambda b, i: (rb0 + b, Z_V // KV_W)),
                vec, vec]
    args = [z, z, z, qg, kg]
    out_specs = [pl.BlockSpec((Q_TILE, ATTN_W), lambda b, i: (b * nq + i, 0))]
    out_shape = [jax.ShapeDtypeStruct((n_out_tok, ATTN_W), BF16)]
    t_keys = t_len
    if use_ctx:
        cos, sin_a, sin_b = rope
        cache_k, cache_v = cache
        past = cache_k.shape[2]
        t_keys = t_len + past
        tq_spec = pl.BlockSpec((Q_TILE, LANES), lambda b, i: (i, 0))
        tk_spec = pl.BlockSpec((t_len, LANES), lambda b, i: (0, 0))
        c_spec = pl.BlockSpec((1, 1, past, KV_W), lambda b, i: (b, layer, 0, 0))
        in_specs += [tq_spec, tq_spec, tq_spec, tk_spec, tk_spec, tk_spec, c_spec, c_spec]
        args += [cos, sin_a, sin_b, cos, sin_a, sin_b,
                 cache_k.reshape(cache_k.shape[0], cache_k.shape[1], past, KV_W),
                 cache_v.reshape(cache_v.shape[0], cache_v.shape[1], past, KV_W)]
    else:
        kv_spec = pl.BlockSpec((1, t_len, KV_W), lambda b, i: (b, 0, 0))
        out_specs += [kv_spec, kv_spec]
        out_shape += [jax.ShapeDtypeStruct((n_batch, t_len, KV_W), F32)] * 2
    return pl.pallas_call(
        functools.partial(_attn_kernel, t_len=t_len, use_ctx=use_ctx),
        grid=(n_batch, nq),
        in_specs=in_specs,
        out_specs=out_specs,
        out_shape=out_shape,
        scratch_shapes=[pltpu.VMEM((t_keys, KV_W), BF16), pltpu.VMEM((t_keys, KV_W), BF16)],
        compiler_params=_cparams(("arbitrary", "arbitrary"), 48),
        name="attention_ctx" if use_ctx else "attention",
    )(*args)


def _shift_rows(x, d):
    n = x.shape[0]
    row = lax.broadcasted_iota(jnp.int32, x.shape, 0)
    y = pltpu.roll(x, d % n, 0)
    if d > 0:
        return jnp.where(row >= d, y, 0.0)
    return jnp.where(row < n + d, y, 0.0)


def _linear_scan(a, u, reverse):
    n = a.shape[0]
    row = lax.broadcasted_iota(jnp.int32, a.shape, 0)
    d = 1
    while d < n:
        if reverse:
            keep = row < n - d
            a_s = jnp.where(keep, pltpu.roll(a, n - d, 0), 1.0)
            u_s = jnp.where(keep, pltpu.roll(u, n - d, 0), 0.0)
        else:
            keep = row >= d
            a_s = jnp.where(keep, pltpu.roll(a, d, 0), 1.0)
            u_s = jnp.where(keep, pltpu.roll(u, d, 0), 0.0)
        u = a * u_s + u
        a = a * a_s
        d *= 2
    return a, u


def _rglru_kernel(*refs, use_ctx):
    if use_ctx:
        (x_ref, y_ref, cw_ref, cb_ref, wg_ref, bg_ref, lam_ref, h0_ref, o_ref) = refs
    else:
        (x_ref, y_ref, cw_ref, cb_ref, wg_ref, bg_ref, lam_ref, o_ref, last_ref) = refs
    x = x_ref[...]
    cw = cw_ref[0]
    xr = cb_ref[0] + _shift_rows(x, 1) * cw[0:1] + x * cw[1:2] \
        + _shift_rows(x, -1) * cw[2:3] + _shift_rows(x, -2) * cw[3:4]
    gates = _sigmoid(_mm(xr, wg_ref[0]) + bg_ref[0])
    n = x.shape[0]
    total = None
    lasts = []
    for d in range(2):
        r = gates[:, (2 * d) * D_RNN:(2 * d + 1) * D_RNN]
        i = gates[:, (2 * d + 1) * D_RNN:(2 * d + 2) * D_RNN]
        log_a = (-RG_C * _softplus(-lam_ref[0, d:d + 1, :])) * r
        a = jnp.exp(log_a)
        u = jnp.sqrt(1.0 - jnp.exp(2.0 * log_a)) * (i * xr)
        a_cum, h = _linear_scan(a, u, reverse=(d == 1))
        if use_ctx:
            h = h + a_cum * h0_ref[0, 0, d:d + 1, :]
        else:
            lasts.append(h[0:1] if d == 1 else h[n - 1:n])
        total = h if total is None else total + h
    o_ref[...] = (jax.nn.gelu(y_ref[...]) * total).astype(BF16)
    if not use_ctx:
        last_ref[0] = jnp.concatenate(lasts, axis=0)


def _rglru(z, conv_w, conv_b, w_gate, b_gate, lam, layer, n_batch, t_len, row0, state=None):
    use_ctx = state is not None
    rb0 = row0 // t_len
    in_specs = [pl.BlockSpec((t_len, D_RNN), lambda b: (rb0 + b, Z_RG // D_RNN)),
                pl.BlockSpec((t_len, D_RNN), lambda b: (rb0 + b, Z_RG // D_RNN + 1)),
                pl.BlockSpec((1, 4, D_RNN), lambda b: (layer, 0, 0)),
                pl.BlockSpec((1, 1, D_RNN), lambda b: (layer, 0, 0)),
                pl.BlockSpec((1, D_RNN, 4 * D_RNN), lambda b: (layer, 0, 0)),
                pl.BlockSpec((1, 1, 4 * D_RNN), lambda b: (layer, 0, 0)),
                pl.BlockSpec((1, 2, D_RNN), lambda b: (layer, 0, 0))]
    args = [z, z, conv_w, conv_b.reshape(-1, 1, D_RNN), w_gate, b_gate, lam]
    out_specs = [pl.BlockSpec((t_len, D_RNN), lambda b: (b, 0))]
    out_shape = [jax.ShapeDtypeStruct((n_batch * t_len, D_RNN), BF16)]
    if use_ctx:
        in_specs.append(pl.BlockSpec((1, 1, 2, D_RNN), lambda b: (b, layer, 0, 0)))
        args.append(state)
    else:
        out_specs.append(pl.BlockSpec((1, 2, D_RNN), lambda b: (b, 0, 0)))
        out_shape.append(jax.ShapeDtypeStruct((n_batch, 2, D_RNN), F32))
    return pl.pallas_call(
        functools.partial(_rglru_kernel, use_ctx=use_ctx),
        grid=(n_batch,),
        in_specs=in_specs,
        out_specs=out_specs,
        out_shape=out_shape,
        compiler_params=_cparams(("arbitrary",), 56),
        name="rglru_ctx" if use_ctx else "rglru",
    )(*args)


def _cumsum_rows(g, reverse):
    n = g.shape[0]
    row = lax.broadcasted_iota(jnp.int32, g.shape, 0)
    d = 1
    while d < n:
        if reverse:
            g = g + jnp.where(row < n - d, pltpu.roll(g, n - d, 0), 0.0)
        else:
            g = g + jnp.where(row >= d, pltpu.roll(g, d, 0), 0.0)
        d *= 2
    return g


def _block_row(gc, blk, r):
    n, w = gc.shape
    if blk >= 8:
        g3 = gc.reshape(n // blk, blk, w)
        return jnp.broadcast_to(g3[:, r:r + 1, :], g3.shape).reshape(n, w)
    g3 = gc.reshape(n // 8, 8, w)
    sub = lax.broadcasted_iota(jnp.int32, g3.shape, 1)
    out = jnp.broadcast_to(g3[:, r:r + 1, :], g3.shape)
    for b0 in range(blk, 8, blk):
        out = jnp.where(sub >= b0, jnp.broadcast_to(g3[:, b0 + r:b0 + r + 1, :], g3.shape), out)
    return out.reshape(n, w)


def _first_head(shape):
    lane = lax.broadcasted_iota(jnp.int32, shape, len(shape) - 1)
    return (lane & (LANES - 1)) < HG_DK


def _pair(x, p):
    return x[:, p * LANES:(p + 1) * LANES]


def _hgrn_gates(f_logit, lb):
    e = jnp.log1p(jnp.exp(-jnp.abs(f_logit)))
    log_sig = jnp.minimum(f_logit, 0.0) - e
    a = jnp.log(lb)
    b = jnp.log1p(-lb) + log_sig
    log_f = jnp.maximum(a, b) + jnp.log1p(jnp.exp(-jnp.abs(a - b)))
    k = (1.0 - lb) * jnp.exp(-jnp.maximum(f_logit, 0.0) - e)
    return log_f, k


def _hgrn_chunk_state(gc, k, v, reverse):
    n = gc.shape[0]
    g_tot = gc[0:1] if reverse else gc[n - 1:n]
    k_end = (k * jnp.exp(jnp.minimum(g_tot - gc, 0.0))).astype(BF16)
    vb = v.astype(BF16)
    same_head = (_first_head((LANES, LANES))
                 == (lax.broadcasted_iota(jnp.int32, (LANES, LANES), 0) < HG_DK))
    ds = [jnp.where(same_head, _mm_tn(_pair(k_end, p), _pair(vb, p)), 0.0) for p in range(N_PAIRS)]
    return ds, jnp.exp(g_tot)


def _decay_state(s, e_row):
    n = s.shape[0]
    eye = (lax.broadcasted_iota(jnp.int32, (n, n), 0) == lax.broadcasted_iota(jnp.int32, (n, n), 1))
    col = jnp.sum(jnp.where(eye, jnp.broadcast_to(e_row, (n, n)), 0.0), axis=1, keepdims=True)
    return s * col


def _hgrn_intra_scores(q, ks, gcs, txs):
    n = q.shape[0]
    row = lax.broadcasted_iota(jnp.int32, q.shape, 0)
    first = _first_head(q.shape)
    scores = [None] * HG_HEADS
    w = 1
    while w < n:
        blk = 2 * w
        second = (row & w) != 0
        e_f = jnp.exp(-jnp.abs(gcs[0] - _block_row(gcs[0], blk, w - 1)))
        e_b = jnp.exp(-jnp.abs(gcs[1] - _block_row(gcs[1], blk, w)))
        q_f = jnp.where(second, q * e_f, 0.0).astype(BF16)
        q_b = jnp.where(second, 0.0, q * e_b).astype(BF16)
        k_f = jnp.where(second, 0.0, ks[0] * e_f)
        k_b = jnp.where(second, ks[1] * e_b, 0.0)
        k_heads = [(jnp.where(first, k_f, 0.0).astype(BF16), jnp.where(first, k_b, 0.0).astype(BF16)),
                   (jnp.where(first, 0.0, k_f).astype(BF16), jnp.where(first, 0.0, k_b).astype(BF16))]
        sel = (txs >= w) & (txs < blk)
        for p in range(N_PAIRS):
            qc = jnp.concatenate([_pair(q_f, p), _pair(q_b, p)], axis=1)
            for j in range(2):
                kc = jnp.concatenate([_pair(k_heads[j][0], p), _pair(k_heads[j][1], p)], axis=1)
                s = jnp.where(sel, _mm_nt(qc, kc), 0.0)
                h = 2 * p + j
                scores[h] = s if scores[h] is None else scores[h] + s
        w = blk
    return scores


def _hgrn_kernel(*refs, t_len, use_ctx):
    if use_ctx:
        (q_ref, ff_ref, fb_ref, i_ref, og_ref, lb_ref, ng_ref, s0_ref, o_ref, sin_ref) = refs
    else:
        (q_ref, ff_ref, fb_ref, i_ref, og_ref, lb_ref, ng_ref, o_ref, sl_ref) = refs
    c = HG_CHUNK
    n_chunks = t_len // c
    f_refs = (ff_ref, fb_ref)

    def load_dir(d, r0):
        log_f, k = _hgrn_gates(f_refs[d][pl.ds(r0, c), :], lb_ref[0, d:d + 1, :])
        return _cumsum_rows(log_f, reverse=(d == 1)), k

    if use_ctx:
        zero = jnp.zeros((HG_DK, HG_DV), F32)
        for d in range(2):
            states = []
            for p in range(N_PAIRS):
                top = jnp.concatenate([s0_ref[0, 0, d, 2 * p], zero], axis=1)
                bot = jnp.concatenate([zero, s0_ref[0, 0, d, 2 * p + 1]], axis=1)
                states.append(jnp.concatenate([top, bot], axis=0))
            order = range(n_chunks) if d == 0 else range(n_chunks - 1, -1, -1)
            for ci in order:
                for p in range(N_PAIRS):
                    sin_ref[ci, p, d * LANES:(d + 1) * LANES, :] = states[p]
                gc, k = load_dir(d, ci * c)
                ds, e_tot = _hgrn_chunk_state(gc, k, i_ref[pl.ds(ci * c, c), :], reverse=(d == 1))
                states = [_decay_state(states[p], _pair(e_tot, p)) + ds[p] for p in range(N_PAIRS)]

    ti = lax.broadcasted_iota(jnp.int32, (c, c), 0)
    si = lax.broadcasted_iota(jnp.int32, (c, c), 1)
    txs = ti ^ si
    diag = ti == si

    def chunk_body(ci, carry):
        r0 = pl.multiple_of(ci * c, c)
        hq = q_ref[pl.ds(r0, c), :]
        q = hq * _sigmoid(hq)
        v = i_ref[pl.ds(r0, c), :]
        gcs, ks = [], []
        for d in range(2):
            gc, k = load_dir(d, r0)
            gcs.append(gc)
            ks.append(k)
        scores = _hgrn_intra_scores(q, ks, gcs, txs)
        first = _first_head(q.shape)
        k_sum = ks[0] + ks[1]
        v_heads = (jnp.where(first, v, 0.0).astype(BF16), jnp.where(first, 0.0, v).astype(BF16))
        k_heads = (jnp.where(first, k_sum, 0.0).astype(BF16), jnp.where(first, 0.0, k_sum).astype(BF16))
        q_bf = q.astype(BF16)
        if use_ctx:
            qe = [(q * jnp.exp(gcs[d])).astype(BF16) for d in range(2)]
        else:
            for d in range(2):
                ds, _ = _hgrn_chunk_state(gcs[d], ks[d], v, reverse=(d == 1))
                for p in range(N_PAIRS):
                    sl_ref[0, d, 2 * p] = ds[p][:HG_DK, :HG_DV]
                    sl_ref[0, d, 2 * p + 1] = ds[p][HG_DK:, HG_DV:]
        outs = []
        for p in range(N_PAIRS):
            o = None
            for j in range(2):
                s_diag = jnp.where(diag, _mm_nt(_pair(q_bf, p), _pair(k_heads[j], p)), 0.0)
                t = _mm(scores[2 * p + j] + s_diag, _pair(v_heads[j], p))
                o = t if o is None else o + t
            if use_ctx:
                o = o + _mm(jnp.concatenate([_pair(qe[0], p), _pair(qe[1], p)], axis=1), sin_ref[ci, p])
            outs.append(_head_rms(o, ng_ref[0]))
        og = og_ref[pl.ds(r0, c), :]
        o_all = jnp.concatenate(outs, axis=1) * (og * _sigmoid(og))
        o_ref[pl.ds(r0, c), :] = o_all.astype(BF16)
        return carry

    if n_chunks == 1:
        chunk_body(0, 0)
    else:
        lax.fori_loop(0, n_chunks, chunk_body, 0)


def _hgrn(z, lb, norm_g, layer, n_batch, t_len, row0, state=None):
    use_ctx = state is not None
    rb0 = row0 // t_len
    cb = Z_HG // HG_W

    def zspec(k):
        return pl.BlockSpec((t_len, HG_W), lambda b: (rb0 + b, cb + k))

    in_specs = [zspec(0), zspec(1), zspec(2), zspec(3), zspec(4),
                pl.BlockSpec((1, 2, HG_W), lambda b: (layer, 0, 0)),
                pl.BlockSpec((1, 1, LANES), lambda b: (layer, 0, 0))]
    args = [z, z, z, z, z, lb, jnp.tile(norm_g, (1, LANES // HG_DV)).reshape(-1, 1, LANES)]
    out_specs = [pl.BlockSpec((t_len, HG_W), lambda b: (b, 0))]
    out_shape = [jax.ShapeDtypeStruct((n_batch * t_len, HG_W), BF16)]
    scratch = []
    if use_ctx:
        in_specs.append(pl.BlockSpec((1, 1, 2, HG_HEADS, HG_DK, HG_DV), lambda b: (b, layer, 0, 0, 0, 0)))
        args.append(state)
        scratch.append(pltpu.VMEM((t_len // HG_CHUNK, N_PAIRS, 2 * LANES, LANES), F32))
    else:
        out_specs.append(pl.BlockSpec((1, 2, HG_HEADS, HG_DK, HG_DV), lambda b: (b, 0, 0, 0, 0)))
        out_shape.append(jax.ShapeDtypeStruct((n_batch, 2, HG_HEADS, HG_DK, HG_DV), F32))
    return pl.pallas_call(
        functools.partial(_hgrn_kernel, t_len=t_len, use_ctx=use_ctx),
        grid=(n_batch,),
        in_specs=in_specs,
        out_specs=out_specs,
        out_shape=out_shape,
        scratch_shapes=scratch,
        compiler_params=_cparams(("arbitrary",), 56),
        name="hgrn_ctx" if use_ctx else "hgrn",
    )(*args)


def _merge_kernel(a_ref, r_ref, h_ref, g0_ref, g1_ref, g2_ref, x_ref, mod_ref,
                  wa_ref, wr_ref, wh_ref, wo_ref, o_ref, wa_s, wr_s, wh_s, wo_s):
    @pl.when(pl.program_id(0) == 0)
    def _():
        wa_s[...] = wa_ref[0].astype(BF16)
        wr_s[...] = wr_ref[0].astype(BF16)
        wh_s[...] = wh_ref[0].astype(BF16)
        wo_s[...] = wo_ref[0].astype(BF16)

    def proj(x_r, w_s):
        return jnp.dot(x_r[...], w_s[...], preferred_element_type=F32)

    merged = (_sigmoid(g0_ref[...]) * proj(a_ref, wa_s)
              + _sigmoid(g1_ref[...]) * proj(r_ref, wr_s)
              + _sigmoid(g2_ref[...]) * proj(h_ref, wh_s))
    out = jnp.dot(merged.astype(BF16), wo_s[...], preferred_element_type=F32)
    o_ref[...] = x_ref[...] + mod_ref[0, 2:3, :] * out


def _merge_project(attn, rnn, hg, z, x, modseg, w_attn_o, w_rnn_o, w_hgrn_o, w_out, layer):
    n_tok = x.shape[0]
    tm = 512
    per_seg = SEG_ROWS // tm
    half = pl.BlockSpec((tm, ATTN_W), lambda i: (i, 0))
    full = pl.BlockSpec((tm, D_MODEL), lambda i: (i, 0))

    def gspec(k):
        return pl.BlockSpec((tm, D_MODEL), lambda i: (i, Z_GL // D_MODEL + k))

    def wspec(rows):
        return pl.BlockSpec((1, rows, D_MODEL), lambda i: (layer, 0, 0))

    return pl.pallas_call(
        _merge_kernel,
        grid=(n_tok // tm,),
        in_specs=[half, half, half, gspec(0), gspec(1), gspec(2), full,
                  pl.BlockSpec((1, N_MOD, D_MODEL), lambda i: (i // per_seg, 0, 0)),
                  wspec(ATTN_W), wspec(D_RNN), wspec(HG_W), wspec(D_MODEL)],
        out_specs=full,
        out_shape=jax.ShapeDtypeStruct((n_tok, D_MODEL), F32),
        scratch_shapes=[pltpu.VMEM((ATTN_W, D_MODEL), BF16), pltpu.VMEM((D_RNN, D_MODEL), BF16),
                        pltpu.VMEM((HG_W, D_MODEL), BF16), pltpu.VMEM((D_MODEL, D_MODEL), BF16)],
        compiler_params=_cparams(("arbitrary",), 56),
        name="merge_project",
    )(attn, rnn, hg, z, z, z, x, modseg, w_attn_o, w_rnn_o, w_hgrn_o, w_out)


def _router_kernel(x_ref, g_ref, mod_ref, wr_ref, br_ref, h_ref, id_ref, wt_ref, pr_ref, cnt_ref, run_ref):
    @pl.when(pl.program_id(0) == 0)
    def _():
        run_ref[...] = jnp.zeros_like(run_ref)

    h = _modulated_norm(x_ref[...], g_ref[0], mod_ref[0, 4:5, :], mod_ref[0, 3:4, :])
    h_ref[...] = h
    logits = jnp.dot(h, wr_ref[0], preferred_element_type=F32,
                     precision=lax.Precision.HIGHEST) + br_ref[0]
    tm = logits.shape[0]
    lane = lax.broadcasted_iota(jnp.int32, logits.shape, 1)
    ids, vals = [], []
    chosen = jnp.zeros(logits.shape, jnp.bool_)
    work = logits
    for _ in range(TOP_K):
        m = jnp.max(work, axis=-1, keepdims=True)
        idx = jnp.min(jnp.where(work == m, lane, LANES), axis=-1, keepdims=True)
        hit = lane == idx
        chosen = chosen | hit
        work = jnp.where(hit, -jnp.inf, work)
        ids.append(idx)
        vals.append(m)
    exps = [jnp.exp(v - vals[0]) for v in vals]
    den = exps[0] + exps[1] + exps[2] + exps[3]
    ind = jnp.where(chosen, 1.0, 0.0)
    r_i = lax.broadcasted_iota(jnp.int32, (tm, tm), 0)
    c_i = lax.broadcasted_iota(jnp.int32, (tm, tm), 1)
    before = _mm(jnp.where(c_i < r_i, 1.0, 0.0), ind) + run_ref[0:1, :]
    id_out = jnp.zeros(logits.shape, jnp.int32)
    wt_out = jnp.zeros(logits.shape, F32)
    pr_out = jnp.zeros(logits.shape, jnp.int32)
    for k in range(TOP_K):
        prior = jnp.sum(jnp.where(lane == ids[k], before, 0.0), axis=-1, keepdims=True)
        id_out = jnp.where(lane == k, ids[k], id_out)
        wt_out = jnp.where(lane == k, exps[k] / den, wt_out)
        pr_out = jnp.where(lane == k, prior.astype(jnp.int32), pr_out)
    id_ref[...] = id_out
    wt_ref[...] = wt_out
    pr_ref[...] = pr_out
    total = run_ref[...] + jnp.sum(ind, axis=0, keepdims=True)
    run_ref[...] = total
    cnt_ref[...] = total


def _router(x, norm_g, modseg, w_router, b_router, layer):
    n_tok = x.shape[0]
    tm = 512
    per_seg = SEG_ROWS // tm
    depth = w_router.shape[0]
    w_pad = jnp.zeros((depth, D_MODEL, LANES), F32).at[:, :, :N_EXPERTS].set(w_router)
    b_pad = jnp.full((depth, 1, LANES), NEG_BIG, F32).at[:, 0, :N_EXPERTS].set(b_router)
    full = pl.BlockSpec((tm, D_MODEL), lambda i: (i, 0))
    small = pl.BlockSpec((tm, LANES), lambda i: (i, 0))
    return pl.pallas_call(
        _router_kernel,
        grid=(n_tok // tm,),
        in_specs=[full,
                  pl.BlockSpec((1, 1, D_MODEL), lambda i: (layer, 0, 0)),
                  pl.BlockSpec((1, N_MOD, D_MODEL), lambda i: (i // per_seg, 0, 0)),
                  pl.BlockSpec((1, D_MODEL, LANES), lambda i: (layer, 0, 0)),
                  pl.BlockSpec((1, 1, LANES), lambda i: (layer, 0, 0))],
        out_specs=[full, small, small, small, pl.BlockSpec((8, LANES), lambda i: (0, 0))],
        out_shape=[jax.ShapeDtypeStruct((n_tok, D_MODEL), F32),
                   jax.ShapeDtypeStruct((n_tok, LANES), jnp.int32),
                   jax.ShapeDtypeStruct((n_tok, LANES), F32),
                   jax.ShapeDtypeStruct((n_tok, LANES), jnp.int32),
                   jax.ShapeDtypeStruct((8, LANES), F32)],
        scratch_shapes=[pltpu.VMEM((8, LANES), F32)],
        compiler_params=_cparams(("arbitrary",), 40),
        name="router",
    )(x, norm_g.reshape(-1, 1, D_MODEL), modseg, w_pad, b_pad)


def _dispatch_kernel(pos_ref, h_ref, init_ref, xs_ref, sem):
    del init_ref

    def row_copy(t, k):
        return pltpu.make_async_copy(h_ref.at[pl.ds(t, 1)],
                                     xs_ref.at[pl.ds(pos_ref[t * TOP_K + k], 1)], sem)

    def start(t, carry):
        for k in range(TOP_K):
            row_copy(t, k).start()
        return carry

    def wait(t, carry):
        for k in range(TOP_K):
            row_copy(t, k).wait()
        return carry

    lax.fori_loop(0, TOKEN_TILE, start, 0)
    lax.fori_loop(0, TOKEN_TILE, wait, 0)


def _dispatch(h, pos_flat, n_rows):
    n_tok = h.shape[0]
    init = jnp.zeros((n_rows, D_MODEL), F32)
    return pl.pallas_call(
        _dispatch_kernel,
        grid=(n_tok // TOKEN_TILE,),
        in_specs=[pl.BlockSpec((TOKEN_TILE * TOP_K,), lambda i: (i,), memory_space=pltpu.SMEM),
                  pl.BlockSpec((TOKEN_TILE, D_MODEL), lambda i: (i, 0)),
                  pl.BlockSpec(memory_space=pl.ANY)],
        out_specs=pl.BlockSpec(memory_space=pl.ANY),
        out_shape=jax.ShapeDtypeStruct((n_rows, D_MODEL), F32),
        scratch_shapes=[pltpu.SemaphoreType.DMA(())],
        input_output_aliases={2: 0},
        compiler_params=_cparams(("arbitrary",), 32),
        name="dispatch",
    )(pos_flat, h, init)


def _combine_kernel(pos_ref, ys_ref, wt_ref, x_ref, mod_ref, o_ref, buf, sem):
    def row_copy(t, k):
        return pltpu.make_async_copy(ys_ref.at[pl.ds(pos_ref[t * TOP_K + k], 1)],
                                     buf.at[k, pl.ds(t, 1)], sem)

    def start(t, carry):
        for k in range(TOP_K):
            row_copy(t, k).start()
        return carry

    def wait(t, carry):
        for k in range(TOP_K):
            row_copy(t, k).wait()
        return carry

    lax.fori_loop(0, TOKEN_TILE, start, 0)
    lax.fori_loop(0, TOKEN_TILE, wait, 0)
    acc = wt_ref[:, 0:1] * buf[0]
    for k in range(1, TOP_K):
        acc = acc + wt_ref[:, k:k + 1] * buf[k]
    o_ref[...] = x_ref[...] + mod_ref[0, 5:6, :] * acc


def _combine(ys, pos_flat, wts, x, modseg):
    n_tok = x.shape[0]
    per_seg = SEG_ROWS // TOKEN_TILE
    full = pl.BlockSpec((TOKEN_TILE, D_MODEL), lambda i: (i, 0))
    return pl.pallas_call(
        _combine_kernel,
        grid=(n_tok // TOKEN_TILE,),
        in_specs=[pl.BlockSpec((TOKEN_TILE * TOP_K,), lambda i: (i,), memory_space=pltpu.SMEM),
                  pl.BlockSpec(memory_space=pl.ANY),
                  pl.BlockSpec((TOKEN_TILE, LANES), lambda i: (i, 0)),
                  full,
                  pl.BlockSpec((1, N_MOD, D_MODEL), lambda i: (i // per_seg, 0, 0))],
        out_specs=full,
        out_shape=jax.ShapeDtypeStruct((n_tok, D_MODEL), F32),
        scratch_shapes=[pltpu.VMEM((TOP_K, TOKEN_TILE, D_MODEL), F32), pltpu.SemaphoreType.DMA(())],
        compiler_params=_cparams(("arbitrary",), 40),
        name="combine",
    )(pos_flat, ys, wts, x, modseg)


def _expert_kernel(te_ref, na_ref, x_ref, wgu_ref, bgu_ref, wd_ref, bd_ref, o_ref, wgu_s, wd_s):
    i = pl.program_id(0)
    active = i < na_ref[0]
    fresh = jnp.logical_or(i == 0, te_ref[i] != te_ref[jnp.maximum(i - 1, 0)])

    @pl.when(jnp.logical_and(active, fresh))
    def _():
        wgu_s[...] = wgu_ref[0, 0].astype(BF16)
        wd_s[...] = wd_ref[0, 0].astype(BF16)

    @pl.when(active)
    def _():
        gu = jnp.dot(x_ref[...].astype(BF16), wgu_s[...], preferred_element_type=F32) + bgu_ref[0, 0]
        gate = jnp.minimum(gu[:, :D_FF], SWIGLU_LIMIT)
        up = jnp.clip(gu[:, D_FF:], -SWIGLU_LIMIT, SWIGLU_LIMIT)
        act = gate * _sigmoid(SWIGLU_ALPHA * gate) * (up + 1.0)
        o_ref[...] = jnp.dot(act.astype(BF16), wd_s[...], preferred_element_type=F32) + bd_ref[0, 0]

    @pl.when(jnp.logical_not(active))
    def _():
        o_ref[...] = jnp.zeros_like(o_ref)


def _experts(xs, tile_expert, n_active, w_gu, b_gu, w_down, b_down, layer):
    n_rows = xs.shape[0]
    depth = w_gu.shape[0]
    tm = EXPERT_TILE
    grid_spec = pltpu.PrefetchScalarGridSpec(
        num_scalar_prefetch=2,
        grid=(n_rows // tm,),
        in_specs=[pl.BlockSpec((tm, D_MODEL), lambda i, te, na: (i, 0)),
                  pl.BlockSpec((1, 1, D_MODEL, 2 * D_FF), lambda i, te, na: (layer, te[i], 0, 0)),
                  pl.BlockSpec((1, 1, 1, 2 * D_FF), lambda i, te, na: (layer, te[i], 0, 0)),
                  pl.BlockSpec((1, 1, D_FF, D_MODEL), lambda i, te, na: (layer, te[i], 0, 0)),
                  pl.BlockSpec((1, 1, 1, D_MODEL), lambda i, te, na: (layer, te[i], 0, 0))],
        out_specs=pl.BlockSpec((tm, D_MODEL), lambda i, te, na: (i, 0)),
        scratch_shapes=[pltpu.VMEM((D_MODEL, 2 * D_FF), BF16), pltpu.VMEM((D_FF, D_MODEL), BF16)],
    )
    return pl.pallas_call(
        _expert_kernel,
        grid_spec=grid_spec,
        out_shape=jax.ShapeDtypeStruct((n_rows, D_MODEL), F32),
        compiler_params=_cparams(("arbitrary",), 56),
        name="experts",
    )(tile_expert, n_active, xs, w_gu, b_gu.reshape(depth, N_EXPERTS, 1, 2 * D_FF),
      w_down, b_down.reshape(depth, N_EXPERTS, 1, D_MODEL))


def _moe(x, norm_g, modseg, w_router, b_router, w_gu, b_gu, w_down, b_down, layer):
    n_tok = x.shape[0]
    h, ids, wts, prior, cnt = _router(x, norm_g, modseg, w_router, b_router, layer)
    tm = EXPERT_TILE
    n_rows = n_tok * TOP_K + N_EXPERTS * tm
    counts = cnt[0, :N_EXPERTS].astype(jnp.int32)
    padded = ((counts + tm - 1) // tm) * tm
    ends = jnp.cumsum(padded)
    starts = ends - padded
    pos = (starts[ids[:, :TOP_K]] + prior[:, :TOP_K]).reshape(-1)
    tile_row0 = jnp.arange(n_rows // tm, dtype=jnp.int32) * tm
    tile_expert = jnp.minimum(jnp.sum((ends[None, :] <= tile_row0[:, None]).astype(jnp.int32), axis=1),
                              N_EXPERTS - 1)
    n_active = (ends[-1:] // tm).astype(jnp.int32)
    xs = _dispatch(h, pos, n_rows)
    ys = _experts(xs, tile_expert, n_active, w_gu, b_gu, w_down, b_down, layer)
    return _combine(ys, pos, wts, x, modseg)


def _rope_tables(n_tok):
    rows = n_tok // GRID_W
    row = jnp.repeat(jnp.arange(rows, dtype=F32), GRID_W)
    col = jnp.tile(jnp.arange(GRID_W, dtype=F32), rows)
    quarter = HEAD_DIM // 4
    inv_freq = ROPE_BASE ** (-jnp.arange(quarter, dtype=F32) / quarter)
    ang_r = row[:, None] * inv_freq
    ang_c = col[:, None] * inv_freq
    ang = jnp.concatenate([ang_r, ang_r, ang_c, ang_c], axis=-1)
    cos, sin = jnp.cos(ang), jnp.sin(ang)
    first = (jnp.arange(HEAD_DIM) % (2 * quarter)) < quarter
    sin_a = jnp.where(first, -sin, 0.0)
    sin_b = jnp.where(first, 0.0, sin)
    return tuple(jnp.tile(t, (1, LANES // HEAD_DIM)) for t in (cos, sin_a, sin_b))


def _block_diag(w):
    eye = jnp.eye(RG_BLOCKS, dtype=w.dtype)
    return jnp.einsum("nde,nm->ndme", w, eye).reshape(D_RNN, D_RNN)


def kernel(x_prompt, x_sample, c, cache_k, cache_v, state_rglru, state_hgrn, c_ctx, w_mod, b_mod, norm1_g, norm2_g, w_in, q_norm_g, k_norm_g, conv_w, conv_b, rg_wa, rg_ba, rg_wx, rg_bx, rg_lambda, hgrn_lb_logits, hgrn_norm_g, w_attn_o, w_rnn_o, w_hgrn_o, w_out, w_router, b_router, w_gu, b_gu, w_down, b_down):
    depth = w_mod.shape[0]
    n_p, t_p = x_prompt.shape[0], x_prompt.shape[1]
    n_s, t_s = x_sample.shape[0], x_sample.shape[1]
    tok_p = n_p * t_p
    tok_s = n_s * t_s
    assert tok_p % SEG_ROWS == 0 and t_s == SEG_ROWS and SEG_ROWS % t_p == 0

    x = jnp.concatenate([x_prompt.reshape(tok_p, D_MODEL), x_sample.reshape(tok_s, D_MODEL)], axis=0)

    lb_all = jnp.cumsum(jax.nn.softmax(hgrn_lb_logits.astype(F32), axis=0), axis=0)
    lb_all = lb_all - lb_all[:1]
    rope = _rope_tables(t_s)

    cond8 = jnp.zeros((8, D_MODEL), F32).at[0].set(c_ctx).at[1:1 + n_s].set(c)
    mod = _modulation(cond8, w_mod, b_mod)
    seg_ids = jnp.asarray([0] * (tok_p // SEG_ROWS) + [1 + b for b in range(n_s)], jnp.int32)
    modsegs = mod[:, seg_ids].reshape(depth, seg_ids.shape[0], N_MOD, D_MODEL)

    w_gate = jnp.stack([jnp.concatenate([_block_diag(rg_wa[l, 0]), _block_diag(rg_wx[l, 0]),
                                         _block_diag(rg_wa[l, 1]), _block_diag(rg_wx[l, 1])], axis=1)
                        for l in range(depth)])
    b_gate = jnp.stack([jnp.concatenate([rg_ba[l, 0], rg_bx[l, 0], rg_ba[l, 1], rg_bx[l, 1]])[None]
                        for l in range(depth)])

    new_k, new_v, new_hr, new_s = [], [], [], []
    for l in range(depth):
        modseg = modsegs[l]
        z = _input_projection(x, norm1_g, modseg, w_in, l)
        attn_p, k_l, v_l = _attention(z, q_norm_g, k_norm_g, l, n_p, t_p, 0)
        (attn_s,) = _attention(z, q_norm_g, k_norm_g, l, n_s, t_s, tok_p, rope=rope,
                               cache=(cache_k, cache_v))
        rnn_p, hr_l = _rglru(z, conv_w, conv_b, w_gate, b_gate, rg_lambda, l, n_p, t_p, 0)
        (rnn_s,) = _rglru(z, conv_w, conv_b, w_gate, b_gate, rg_lambda, l, n_s, t_s, tok_p,
                          state=state_rglru)
        hg_p, s_l = _hgrn(z, lb_all, hgrn_norm_g, l, n_p, t_p, 0)
        (hg_s,) = _hgrn(z, lb_all, hgrn_norm_g, l, n_s, t_s, tok_p, state=state_hgrn)
        attn = jnp.concatenate([attn_p, attn_s], axis=0)
        rnn = jnp.concatenate([rnn_p, rnn_s], axis=0)
        hg = jnp.concatenate([hg_p, hg_s], axis=0)
        x = _merge_project(attn, rnn, hg, z, x, modseg, w_attn_o, w_rnn_o, w_hgrn_o, w_out, l)
        x = _moe(x, norm2_g, modseg, w_router, b_router, w_gu, b_gu, w_down, b_down, l)
        new_k.append(k_l.reshape(n_p, t_p, N_KV_HEADS, HEAD_DIM))
        new_v.append(v_l.reshape(n_p, t_p, N_KV_HEADS, HEAD_DIM))
        new_hr.append(hr_l)
        new_s.append(s_l)

    y_prompt = x[:tok_p].reshape(n_p, t_p, D_MODEL)
    y_sample = x[tok_p:].reshape(n_s, t_s, D_MODEL)
    return (y_prompt, y_sample, jnp.stack(new_k, axis=1), jnp.stack(new_v, axis=1),
            jnp.stack(new_hr, axis=1), jnp.stack(new_s, axis=1))
```

```python
import functools

import jax
import jax.numpy as jnp
from jax import lax
from jax.experimental import pallas as pl
from jax.experimental.pallas import tpu as pltpu

F32 = jnp.float32
BF16 = jnp.bfloat16

D_MODEL = 1024
GRID_W = 64
HEAD_DIM = 64
N_HEADS = 8
N_KV_HEADS = 2
KV_GROUP = N_HEADS // N_KV_HEADS
ATTN_W = N_HEADS * HEAD_DIM
KV_W = N_KV_HEADS * HEAD_DIM
ROPE_BASE = 10000.0
D_RNN = 512
RG_BLOCKS = 8
RG_BW = D_RNN // RG_BLOCKS
RG_C = 8.0
HG_HEADS = 8
HG_DK = 64
HG_DV = 64
HG_W = HG_HEADS * HG_DK
N_PAIRS = HG_W // 128
N_EXPERTS = 32
TOP_K = 4
D_FF = D_MODEL
SWIGLU_LIMIT = 7.0
SWIGLU_ALPHA = 1.702
N_MOD = 6
EPS = 1e-6
IN_W = ATTN_W + 2 * KV_W + 2 * D_RNN + 5 * HG_W + 3 * D_MODEL

LANES = 128
SEG_ROWS = 1024
COL_TILE = 256
HG_CHUNK = 256
Q_TILE = 256
EXPERT_TILE = 256
TOKEN_TILE = 512
ROW_BLOCK = 8
LOCAL_ROWS = TOKEN_TILE * TOP_K + N_EXPERTS * ROW_BLOCK
NEG_BIG = -1e30

_SRC_TILES = tuple(range(17, 29)) + tuple(range(7, 17)) + tuple(range(3, 7)) + (0, 1, 2)
Z_GL = 0
Z_HG = 3 * D_MODEL
Z_RG = Z_HG + 5 * HG_W
Z_Q = Z_RG + 2 * D_RNN
Z_K = Z_Q + ATTN_W
Z_V = Z_K + KV_W
Z_W = -(-IN_W // (2 * COL_TILE)) * (2 * COL_TILE)


def _cparams(sem, vmem_mb):
    return pltpu.CompilerParams(dimension_semantics=sem, vmem_limit_bytes=vmem_mb * 1024 * 1024)


def _mm(a, b):
    return jnp.dot(a.astype(BF16), b.astype(BF16), preferred_element_type=F32)


def _mm_nt(a, b):
    return lax.dot_general(a.astype(BF16), b.astype(BF16), (((1,), (1,)), ((), ())),
                           preferred_element_type=F32)


def _mm_tn(a, b):
    return lax.dot_general(a.astype(BF16), b.astype(BF16), (((0,), (0,)), ((), ())),
                           preferred_element_type=F32)


def _sigmoid(x):
    return 1.0 / (1.0 + jnp.exp(-x))


def _softplus(x):
    return jnp.maximum(x, 0.0) + jnp.log1p(jnp.exp(-jnp.abs(x)))


def _modulated_norm(x, g, scale, shift):
    y = x * lax.rsqrt(jnp.mean(x * x, axis=-1, keepdims=True) + EPS)
    return (y * g) * (1.0 + scale) + shift


def _mod_kernel(c_ref, w_ref, b_ref, o_ref):
    c = c_ref[...]
    s = c * _sigmoid(c)
    o_ref[0] = _mm(s, w_ref[0]) + b_ref[0]


def _modulation(cond8, w_mod, b_mod):
    depth = w_mod.shape[0]
    tn = 1024
    return pl.pallas_call(
        _mod_kernel,
        grid=(depth, N_MOD * D_MODEL // tn),
        in_specs=[pl.BlockSpec((8, D_MODEL), lambda l, j: (0, 0)),
                  pl.BlockSpec((1, D_MODEL, tn), lambda l, j: (l, 0, j)),
                  pl.BlockSpec((1, 1, tn), lambda l, j: (l, 0, j))],
        out_specs=pl.BlockSpec((1, 8, tn), lambda l, j: (l, 0, j)),
        out_shape=jax.ShapeDtypeStruct((depth, 8, N_MOD * D_MODEL), F32),
        compiler_params=_cparams(("arbitrary", "arbitrary"), 32),
        name="modulation",
    )(cond8, w_mod, b_mod.reshape(depth, 1, N_MOD * D_MODEL))


def _inproj_kernel(src_ref, x_ref, g_ref, mod_ref, wa_ref, wb_ref, o_ref, h_ref, w_s):
    j = pl.program_id(1)

    @pl.when(j == 0)
    def _():
        h = _modulated_norm(x_ref[...], g_ref[0], mod_ref[0, 1:2, :], mod_ref[0, 0:1, :])
        h_ref[...] = h.astype(BF16)

    @pl.when(pl.program_id(0) == 0)
    def _():
        w_s[j, :, 0:COL_TILE] = wa_ref[0].astype(BF16)
        w_s[j, :, COL_TILE:2 * COL_TILE] = wb_ref[0].astype(BF16)

    o_ref[...] = jnp.dot(h_ref[...], w_s[j], preferred_element_type=F32)


def _input_projection(x, norm_g, modseg, w_in, layer):
    n_tok = x.shape[0]
    tm = SEG_ROWS
    n_col = Z_W // (2 * COL_TILE)
    src = jnp.asarray(_SRC_TILES + (_SRC_TILES[-1],) * (Z_W // COL_TILE - len(_SRC_TILES)), jnp.int32)

    def wspec(k):
        return pl.BlockSpec((1, D_MODEL, COL_TILE),
                            lambda i, j, s: (layer, 0, s[2 * jnp.where(i == 0, j, n_col - 1) + k]))

    grid_spec = pltpu.PrefetchScalarGridSpec(
        num_scalar_prefetch=1,
        grid=(n_tok // tm, n_col),
        in_specs=[pl.BlockSpec((tm, D_MODEL), lambda i, j, s: (i, 0)),
                  pl.BlockSpec((1, 1, D_MODEL), lambda i, j, s: (layer, 0, 0)),
                  pl.BlockSpec((1, N_MOD, D_MODEL), lambda i, j, s: (i, 0, 0)),
                  wspec(0), wspec(1)],
        out_specs=pl.BlockSpec((tm, 2 * COL_TILE), lambda i, j, s: (i, j)),
        scratch_shapes=[pltpu.VMEM((tm, D_MODEL), BF16),
                        pltpu.VMEM((n_col, D_MODEL, 2 * COL_TILE), BF16)],
    )
    return pl.pallas_call(
        _inproj_kernel,
        grid_spec=grid_spec,
        out_shape=jax.ShapeDtypeStruct((n_tok, Z_W), F32),
        compiler_params=_cparams(("arbitrary", "arbitrary"), 40),
        name="input_projection",
    )(src, x, norm_g.reshape(-1, 1, D_MODEL), modseg, w_in, w_in)


def _head_rms(x, g):
    lane = lax.broadcasted_iota(jnp.int32, x.shape, 1)
    lo = lane < HEAD_DIM
    xx = x * x
    s_lo = jnp.sum(jnp.where(lo, xx, 0.0), axis=-1, keepdims=True)
    s_hi = jnp.sum(jnp.where(lo, 0.0, xx), axis=-1, keepdims=True)
    inv = jnp.where(lo, lax.rsqrt(s_lo * (1.0 / HEAD_DIM) + EPS),
                    lax.rsqrt(s_hi * (1.0 / HEAD_DIM) + EPS))
    return x * inv * g


def _rope(x, cos, sin_a, sin_b):
    q = HEAD_DIM // 4
    return x * cos + pltpu.roll(x, LANES - q, 1) * sin_a + pltpu.roll(x, q, 1) * sin_b


def _attn_kernel(*refs, t_len, use_ctx):
    if use_ctx:
        (q_ref, k_ref, v_ref, qg_ref, kg_ref, cos_q_ref, sa_q_ref, sb_q_ref,
         cos_k_ref, sa_k_ref, sb_k_ref, ck_ref, cv_ref, o_ref, kall_ref, vall_ref) = refs
    else:
        (q_ref, k_ref, v_ref, qg_ref, kg_ref, o_ref, ko_ref, vo_ref, kall_ref, vall_ref) = refs

    @pl.when(pl.program_id(1) == 0)
    def _():
        k = _head_rms(k_ref[...], kg_ref[...])
        if use_ctx:
            k = _rope(k, cos_k_ref[...], sa_k_ref[...], sb_k_ref[...])
            kall_ref[0:t_len, :] = k.astype(BF16)
            vall_ref[0:t_len, :] = v_ref[...].astype(BF16)
            kall_ref[t_len:, :] = ck_ref[0, 0].astype(BF16)
            vall_ref[t_len:, :] = cv_ref[0, 0].astype(BF16)
        else:
            ko_ref[0] = k
            vo_ref[0] = v_ref[...]
            kall_ref[...] = k.astype(BF16)
            vall_ref[...] = v_ref[...].astype(BF16)

    heads = []
    for c in range(ATTN_W // LANES):
        qc = _head_rms(q_ref[:, c * LANES:(c + 1) * LANES], qg_ref[...])
        if use_ctx:
            qc = _rope(qc, cos_q_ref[...], sa_q_ref[...], sb_q_ref[...])
        qc = (qc * (HEAD_DIM ** -0.5)).astype(BF16)
        heads.append(qc[:, :HEAD_DIM])
        heads.append(qc[:, HEAD_DIM:])
    tq = q_ref.shape[0]
    for g in range(N_KV_HEADS):
        qs = jnp.concatenate(heads[g * KV_GROUP:(g + 1) * KV_GROUP], axis=0)
        kh = kall_ref[:, g * HEAD_DIM:(g + 1) * HEAD_DIM]
        vh = vall_ref[:, g * HEAD_DIM:(g + 1) * HEAD_DIM]
        s = _mm_nt(qs, kh)
        m = jnp.max(s, axis=-1, keepdims=True)
        p = jnp.exp(s - m)
        den = jnp.sum(p, axis=-1, keepdims=True)
        o = _mm(p, vh) / den
        for j in range(KV_GROUP):
            hh = g * KV_GROUP + j
            o_ref[:, hh * HEAD_DIM:(hh + 1) * HEAD_DIM] = o[j * tq:(j + 1) * tq].astype(BF16)


def _attention(z, q_g, k_g, layer, n_batch, t_len, row0, rope=None, cache=None):
    use_ctx = cache is not None
    nq = t_len // Q_TILE
    rb0 = row0 // t_len
    qb0 = row0 // Q_TILE
    qg = jnp.tile(q_g[layer], 2).reshape(1, LANES)
    kg = jnp.tile(k_g[layer], 2).reshape(1, LANES)
    n_out_tok = n_batch * t_len
    vec = pl.BlockSpec((1, LANES), lambda b, i: (0, 0))
    in_specs = [pl.BlockSpec((Q_TILE, ATTN_W), lambda b, i: (qb0 + b * nq + i, Z_Q // ATTN_W)),
                pl.BlockSpec((t_len, KV_W), lambda b, i: (rb0 + b, Z_K // KV_W)),
                pl.BlockSpec((t_len, KV_W), lambda b, i: (rb0 + b, Z_V // KV_W)),
                vec, vec]
    args = [z, z, z, qg, kg]
    out_specs = [pl.BlockSpec((Q_TILE, ATTN_W), lambda b, i: (b * nq + i, 0))]
    out_shape = [jax.ShapeDtypeStruct((n_out_tok, ATTN_W), BF16)]
    t_keys = t_len
    if use_ctx:
        cos, sin_a, sin_b = rope
        cache_k, cache_v = cache
        past = cache_k.shape[2]
        t_keys = t_len + past
        tq_spec = pl.BlockSpec((Q_TILE, LANES), lambda b, i: (i, 0))
        tk_spec = pl.BlockSpec((t_len, LANES), lambda b, i: (0, 0))
        c_spec = pl.BlockSpec((1, 1, past, KV_W), lambda b, i: (b, layer, 0, 0))
        in_specs += [tq_spec, tq_spec, tq_spec, tk_spec, tk_spec, tk_spec, c_spec, c_spec]
        args += [cos, sin_a, sin_b, cos, sin_a, sin_b,
                 cache_k.reshape(cache_k.shape[0], cache_k.shape[1], past, KV_W),
                 cache_v.reshape(cache_v.shape[0], cache_v.shape[1], past, KV_W)]
    else:
        kv_spec = pl.BlockSpec((1, t_len, KV_W), lambda b, i: (b, 0, 0))
        out_specs += [kv_spec, kv_spec]
        out_shape += [jax.ShapeDtypeStruct((n_batch, t_len, KV_W), F32)] * 2
    return pl.pallas_call(
        functools.partial(_attn_kernel, t_len=t_len, use_ctx=use_ctx),
        grid=(n_batch, nq),
        in_specs=in_specs,
        out_specs=out_specs,
        out_shape=out_shape,
        scratch_shapes=[pltpu.VMEM((t_keys, KV_W), BF16), pltpu.VMEM((t_keys, KV_W), BF16)],
        compiler_params=_cparams(("arbitrary", "arbitrary"), 48),
        name="attention_ctx" if use_ctx else "attention",
    )(*args)


def _shift_rows(x, d):
    n = x.shape[0]
    row = lax.broadcasted_iota(jnp.int32, x.shape, 0)
    y = pltpu.roll(x, d % n, 0)
    if d > 0:
        return jnp.where(row >= d, y, 0.0)
    return jnp.where(row < n + d, y, 0.0)


def _linear_scan(a, u, reverse):
    n = a.shape[0]
    row = lax.broadcasted_iota(jnp.int32, a.shape, 0)
    d = 1
    while d < n:
        if reverse:
            keep = row < n - d
            a_s = jnp.where(keep, pltpu.roll(a, n - d, 0), 1.0)
            u_s = jnp.where(keep, pltpu.roll(u, n - d, 0), 0.0)
        else:
            keep = row >= d
            a_s = jnp.where(keep, pltpu.roll(a, d, 0), 1.0)
            u_s = jnp.where(keep, pltpu.roll(u, d, 0), 0.0)
        u = a * u_s + u
        a = a * a_s
        d *= 2
    return a, u


def _rglru_kernel(*refs, use_ctx):
    if use_ctx:
        (x_ref, y_ref, cw_ref, cb_ref, wg_ref, bg_ref, lam_ref, h0_ref, o_ref) = refs
    else:
        (x_ref, y_ref, cw_ref, cb_ref, wg_ref, bg_ref, lam_ref, o_ref, last_ref) = refs
    x = x_ref[...]
    cw = cw_ref[0]
    xr = cb_ref[0] + _shift_rows(x, 1) * cw[0:1] + x * cw[1:2] \
        + _shift_rows(x, -1) * cw[2:3] + _shift_rows(x, -2) * cw[3:4]
    gates = _sigmoid(_mm(xr, wg_ref[0]) + bg_ref[0])
    n = x.shape[0]
    total = None
    lasts = []
    for d in range(2):
        r = gates[:, (2 * d) * D_RNN:(2 * d + 1) * D_RNN]
        i = gates[:, (2 * d + 1) * D_RNN:(2 * d + 2) * D_RNN]
        log_a = (-RG_C * _softplus(-lam_ref[0, d:d + 1, :])) * r
        a = jnp.exp(log_a)
        u = jnp.sqrt(1.0 - jnp.exp(2.0 * log_a)) * (i * xr)
        a_cum, h = _linear_scan(a, u, reverse=(d == 1))
        if use_ctx:
            h = h + a_cum * h0_ref[0, 0, d:d + 1, :]
        else:
            lasts.append(h[0:1] if d == 1 else h[n - 1:n])
        total = h if total is None else total + h
    o_ref[...] = (jax.nn.gelu(y_ref[...]) * total).astype(BF16)
    if not use_ctx:
        last_ref[0] = jnp.concatenate(lasts, axis=0)


def _rglru(z, conv_w, conv_b, w_gate, b_gate, lam, layer, n_batch, t_len, row0, state=None):
    use_ctx = state is not None
    rb0 = row0 // t_len
    in_specs = [pl.BlockSpec((t_len, D_RNN), lambda b: (rb0 + b, Z_RG // D_RNN)),
                pl.BlockSpec((t_len, D_RNN), lambda b: (rb0 + b, Z_RG // D_RNN + 1)),
                pl.BlockSpec((1, 4, D_RNN), lambda b: (layer, 0, 0)),
                pl.BlockSpec((1, 1, D_RNN), lambda b: (layer, 0, 0)),
                pl.BlockSpec((1, D_RNN, 4 * D_RNN), lambda b: (layer, 0, 0)),
                pl.BlockSpec((1, 1, 4 * D_RNN), lambda b: (layer, 0, 0)),
                pl.BlockSpec((1, 2, D_RNN), lambda b: (layer, 0, 0))]
    args = [z, z, conv_w, conv_b.reshape(-1, 1, D_RNN), w_gate, b_gate, lam]
    out_specs = [pl.BlockSpec((t_len, D_RNN), lambda b: (b, 0))]
    out_shape = [jax.ShapeDtypeStruct((n_batch * t_len, D_RNN), BF16)]
    if use_ctx:
        in_specs.append(pl.BlockSpec((1, 1, 2, D_RNN), lambda b: (b, layer, 0, 0)))
        args.append(state)
    else:
        out_specs.append(pl.BlockSpec((1, 2, D_RNN), lambda b: (b, 0, 0)))
        out_shape.append(jax.ShapeDtypeStruct((n_batch, 2, D_RNN), F32))
    return pl.pallas_call(
        functools.partial(_rglru_kernel, use_ctx=use_ctx),
        grid=(n_batch,),
        in_specs=in_specs,
        out_specs=out_specs,
        out_shape=out_shape,
        compiler_params=_cparams(("arbitrary",), 56),
        name="rglru_ctx" if use_ctx else "rglru",
    )(*args)


def _cumsum_rows(g, reverse):
    n = g.shape[0]
    row = lax.broadcasted_iota(jnp.int32, g.shape, 0)
    d = 1
    while d < n:
        if reverse:
            g = g + jnp.where(row < n - d, pltpu.roll(g, n - d, 0), 0.0)
        else:
            g = g + jnp.where(row >= d, pltpu.roll(g, d, 0), 0.0)
        d *= 2
    return g


def _block_row(gc, blk, r):
    n, w = gc.shape
    if blk >= 8:
        g3 = gc.reshape(n // blk, blk, w)
        return jnp.broadcast_to(g3[:, r:r + 1, :], g3.shape).reshape(n, w)
    g3 = gc.reshape(n // 8, 8, w)
    sub = lax.broadcasted_iota(jnp.int32, g3.shape, 1)
    out = jnp.broadcast_to(g3[:, r:r + 1, :], g3.shape)
    for b0 in range(blk, 8, blk):
        out = jnp.where(sub >= b0, jnp.broadcast_to(g3[:, b0 + r:b0 + r + 1, :], g3.shape), out)
    return out.reshape(n, w)


def _first_head(shape):
    lane = lax.broadcasted_iota(jnp.int32, shape, len(shape) - 1)
    return (lane & (LANES - 1)) < HG_DK


def _pair(x, p):
    return x[:, p * LANES:(p + 1) * LANES]


def _hgrn_gates(f_logit, lb):
    e = jnp.log1p(jnp.exp(-jnp.abs(f_logit)))
    log_sig = jnp.minimum(f_logit, 0.0) - e
    a = jnp.log(lb)
    b = jnp.log1p(-lb) + log_sig
    log_f = jnp.maximum(a, b) + jnp.log1p(jnp.exp(-jnp.abs(a - b)))
    k = (1.0 - lb) * jnp.exp(-jnp.maximum(f_logit, 0.0) - e)
    return log_f, k


def _hgrn_chunk_state(gc, k, v, reverse):
    n = gc.shape[0]
    g_tot = gc[0:1] if reverse else gc[n - 1:n]
    k_end = (k * jnp.exp(jnp.minimum(g_tot - gc, 0.0))).astype(BF16)
    vb = v.astype(BF16)
    same_head = (_first_head((LANES, LANES))
                 == (lax.broadcasted_iota(jnp.int32, (LANES, LANES), 0) < HG_DK))
    ds = [jnp.where(same_head, _mm_tn(_pair(k_end, p), _pair(vb, p)), 0.0) for p in range(N_PAIRS)]
    return ds, jnp.exp(g_tot)


def _decay_state(s, e_row):
    n = s.shape[0]
    eye = (lax.broadcasted_iota(jnp.int32, (n, n), 0) == lax.broadcasted_iota(jnp.int32, (n, n), 1))
    col = jnp.sum(jnp.where(eye, jnp.broadcast_to(e_row, (n, n)), 0.0), axis=1, keepdims=True)
    return s * col


def _hgrn_intra_scores(q, ks, gcs, txs):
    n = q.shape[0]
    row = lax.broadcasted_iota(jnp.int32, q.shape, 0)
    first = _first_head(q.shape)
    scores = [None] * HG_HEADS
    w = 1
    while w < n:
        blk = 2 * w
        second = (row & w) != 0
        e_f = jnp.exp(-jnp.abs(gcs[0] - _block_row(gcs[0], blk, w - 1)))
        e_b = jnp.exp(-jnp.abs(gcs[1] - _block_row(gcs[1], blk, w)))
        q_f = jnp.where(second, q * e_f, 0.0).astype(BF16)
        q_b = jnp.where(second, 0.0, q * e_b).astype(BF16)
        k_f = jnp.where(second, 0.0, ks[0] * e_f)
        k_b = jnp.where(second, ks[1] * e_b, 0.0)
        k_heads = [(jnp.where(first, k_f, 0.0).astype(BF16), jnp.where(first, k_b, 0.0).astype(BF16)),
                   (jnp.where(first, 0.0, k_f).astype(BF16), jnp.where(first, 0.0, k_b).astype(BF16))]
        sel = (txs >= w) & (txs < blk)
        for p in range(N_PAIRS):
            qc = jnp.concatenate([_pair(q_f, p), _pair(q_b, p)], axis=1)
            for j in range(2):
                kc = jnp.concatenate([_pair(k_heads[j][0], p), _pair(k_heads[j][1], p)], axis=1)
                s = jnp.where(sel, _mm_nt(qc, kc), 0.0)
                h = 2 * p + j
                scores[h] = s if scores[h] is None else scores[h] + s
        w = blk
    return scores


def _hgrn_kernel(*refs, t_len, use_ctx):
    if use_ctx:
        (q_ref, ff_ref, fb_ref, i_ref, og_ref, lb_ref, ng_ref, s0_ref, o_ref, sin_ref) = refs
    else:
        (q_ref, ff_ref, fb_ref, i_ref, og_ref, lb_ref, ng_ref, o_ref, sl_ref) = refs
    c = HG_CHUNK
    n_chunks = t_len // c
    f_refs = (ff_ref, fb_ref)

    def load_dir(d, r0):
        log_f, k = _hgrn_gates(f_refs[d][pl.ds(r0, c), :], lb_ref[0, d:d + 1, :])
        return _cumsum_rows(log_f, reverse=(d == 1)), k

    if use_ctx:
        zero = jnp.zeros((HG_DK, HG_DV), F32)
        for d in range(2):
            states = []
            for p in range(N_PAIRS):
                top = jnp.concatenate([s0_ref[0, 0, d, 2 * p], zero], axis=1)
                bot = jnp.concatenate([zero, s0_ref[0, 0, d, 2 * p + 1]], axis=1)
                states.append(jnp.concatenate([top, bot], axis=0))
            order = range(n_chunks) if d == 0 else range(n_chunks - 1, -1, -1)
            for ci in order:
                for p in range(N_PAIRS):
                    sin_ref[ci, p, d * LANES:(d + 1) * LANES, :] = states[p]
                gc, k = load_dir(d, ci * c)
                ds, e_tot = _hgrn_chunk_state(gc, k, i_ref[pl.ds(ci * c, c), :], reverse=(d == 1))
                states = [_decay_state(states[p], _pair(e_tot, p)) + ds[p] for p in range(N_PAIRS)]

    ti = lax.broadcasted_iota(jnp.int32, (c, c), 0)
    si = lax.broadcasted_iota(jnp.int32, (c, c), 1)
    txs = ti ^ si
    diag = ti == si

    def chunk_body(ci, carry):
        r0 = pl.multiple_of(ci * c, c)
        hq = q_ref[pl.ds(r0, c), :]
        q = hq * _sigmoid(hq)
        v = i_ref[pl.ds(r0, c), :]
        gcs, ks = [], []
        for d in range(2):
            gc, k = load_dir(d, r0)
            gcs.append(gc)
            ks.append(k)
        scores = _hgrn_intra_scores(q, ks, gcs, txs)
        first = _first_head(q.shape)
        k_sum = ks[0] + ks[1]
        v_heads = (jnp.where(first, v, 0.0).astype(BF16), jnp.where(first, 0.0, v).astype(BF16))
        k_heads = (jnp.where(first, k_sum, 0.0).astype(BF16), jnp.where(first, 0.0, k_sum).astype(BF16))
        q_bf = q.astype(BF16)
        if use_ctx:
            qe = [(q * jnp.exp(gcs[d])).astype(BF16) for d in range(2)]
        else:
            for d in range(2):
                ds, _ = _hgrn_chunk_state(gcs[d], ks[d], v, reverse=(d == 1))
                for p in range(N_PAIRS):
                    sl_ref[0, d, 2 * p] = ds[p][:HG_DK, :HG_DV]
                    sl_ref[0, d, 2 * p + 1] = ds[p][HG_DK:, HG_DV:]
        outs = []
        for p in range(N_PAIRS):
            o = None
            for j in range(2):
                s_diag = jnp.where(diag, _mm_nt(_pair(q_bf, p), _pair(k_heads[j], p)), 0.0)
                t = _mm(scores[2 * p + j] + s_diag, _pair(v_heads[j], p))
                o = t if o is None else o + t
            if use_ctx:
                o = o + _mm(jnp.concatenate([_pair(qe[0], p), _pair(qe[1], p)], axis=1), sin_ref[ci, p])
            outs.append(_head_rms(o, ng_ref[0]))
        og = og_ref[pl.ds(r0, c), :]
        o_all = jnp.concatenate(outs, axis=1) * (og * _sigmoid(og))
        o_ref[pl.ds(r0, c), :] = o_all.astype(BF16)
        return carry

    if n_chunks == 1:
        chunk_body(0, 0)
    else:
        lax.fori_loop(0, n_chunks, chunk_body, 0)


def _hgrn(z, lb, norm_g, layer, n_batch, t_len, row0, state=None):
    use_ctx = state is not None
    rb0 = row0 // t_len
    cb = Z_HG // HG_W

    def zspec(k):
        return pl.BlockSpec((t_len, HG_W), lambda b: (rb0 + b, cb + k))

    in_specs = [zspec(0), zspec(1), zspec(2), zspec(3), zspec(4),
                pl.BlockSpec((1, 2, HG_W), lambda b: (layer, 0, 0)),
                pl.BlockSpec((1, 1, LANES), lambda b: (layer, 0, 0))]
    args = [z, z, z, z, z, lb, jnp.tile(norm_g, (1, LANES // HG_DV)).reshape(-1, 1, LANES)]
    out_specs = [pl.BlockSpec((t_len, HG_W), lambda b: (b, 0))]
    out_shape = [jax.ShapeDtypeStruct((n_batch * t_len, HG_W), BF16)]
    scratch = []
    if use_ctx:
        in_specs.append(pl.BlockSpec((1, 1, 2, HG_HEADS, HG_DK, HG_DV), lambda b: (b, layer, 0, 0, 0, 0)))
        args.append(state)
        scratch.append(pltpu.VMEM((t_len // HG_CHUNK, N_PAIRS, 2 * LANES, LANES), F32))
    else:
        out_specs.append(pl.BlockSpec((1, 2, HG_HEADS, HG_DK, HG_DV), lambda b: (b, 0, 0, 0, 0)))
        out_shape.append(jax.ShapeDtypeStruct((n_batch, 2, HG_HEADS, HG_DK, HG_DV), F32))
    return pl.pallas_call(
        functools.partial(_hgrn_kernel, t_len=t_len, use_ctx=use_ctx),
        grid=(n_batch,),
        in_specs=in_specs,
        out_specs=out_specs,
        out_shape=out_shape,
        scratch_shapes=scratch,
        compiler_params=_cparams(("arbitrary",), 56),
        name="hgrn_ctx" if use_ctx else "hgrn",
    )(*args)


def _merge_kernel(a_ref, r_ref, h_ref, g0_ref, g1_ref, g2_ref, x_ref, mod_ref,
                  wa_ref, wr_ref, wh_ref, wo_ref, o_ref, wa_s, wr_s, wh_s, wo_s):
    @pl.when(pl.program_id(0) == 0)
    def _():
        wa_s[...] = wa_ref[0].astype(BF16)
        wr_s[...] = wr_ref[0].astype(BF16)
        wh_s[...] = wh_ref[0].astype(BF16)
        wo_s[...] = wo_ref[0].astype(BF16)

    def proj(x_r, w_s):
        return jnp.dot(x_r[...], w_s[...], preferred_element_type=F32)

    merged = (_sigmoid(g0_ref[...]) * proj(a_ref, wa_s)
              + _sigmoid(g1_ref[...]) * proj(r_ref, wr_s)
              + _sigmoid(g2_ref[...]) * proj(h_ref, wh_s))
    out = jnp.dot(merged.astype(BF16), wo_s[...], preferred_element_type=F32)
    o_ref[...] = x_ref[...] + mod_ref[0, 2:3, :] * out


def _merge_project(attn, rnn, hg, z, x, modseg, w_attn_o, w_rnn_o, w_hgrn_o, w_out, layer):
    n_tok = x.shape[0]
    tm = 512
    per_seg = SEG_ROWS // tm
    half = pl.BlockSpec((tm, ATTN_W), lambda i: (i, 0))
    full = pl.BlockSpec((tm, D_MODEL), lambda i: (i, 0))

    def gspec(k):
        return pl.BlockSpec((tm, D_MODEL), lambda i: (i, Z_GL // D_MODEL + k))

    def wspec(rows):
        return pl.BlockSpec((1, rows, D_MODEL), lambda i: (layer, 0, 0))

    return pl.pallas_call(
        _merge_kernel,
        grid=(n_tok // tm,),
        in_specs=[half, half, half, gspec(0), gspec(1), gspec(2), full,
                  pl.BlockSpec((1, N_MOD, D_MODEL), lambda i: (i // per_seg, 0, 0)),
                  wspec(ATTN_W), wspec(D_RNN), wspec(HG_W), wspec(D_MODEL)],
        out_specs=full,
        out_shape=jax.ShapeDtypeStruct((n_tok, D_MODEL), F32),
        scratch_shapes=[pltpu.VMEM((ATTN_W, D_MODEL), BF16), pltpu.VMEM((D_RNN, D_MODEL), BF16),
                        pltpu.VMEM((HG_W, D_MODEL), BF16), pltpu.VMEM((D_MODEL, D_MODEL), BF16)],
        compiler_params=_cparams(("arbitrary",), 56),
        name="merge_project",
    )(attn, rnn, hg, z, z, z, x, modseg, w_attn_o, w_rnn_o, w_hgrn_o, w_out)


def _router_kernel(x_ref, g_ref, mod_ref, wr_ref, br_ref, h_ref, slot_ref, wt_ref, meta_ref, run_ref):
    @pl.when(pl.program_id(0) == 0)
    def _():
        run_ref[...] = jnp.zeros_like(run_ref)

    h = _modulated_norm(x_ref[...], g_ref[0], mod_ref[0, 4:5, :], mod_ref[0, 3:4, :])
    h_ref[...] = h
    logits = jnp.dot(h, wr_ref[0], preferred_element_type=F32,
                     precision=lax.Precision.HIGHEST) + br_ref[0]
    tm = logits.shape[0]
    lane = lax.broadcasted_iota(jnp.int32, logits.shape, 1)
    ids, vals = [], []
    chosen = jnp.zeros(logits.shape, jnp.bool_)
    work = logits
    for _ in range(TOP_K):
        m = jnp.max(work, axis=-1, keepdims=True)
        idx = jnp.min(jnp.where(work == m, lane, LANES), axis=-1, keepdims=True)
        hit = lane == idx
        chosen = chosen | hit
        work = jnp.where(hit, -jnp.inf, work)
        ids.append(idx)
        vals.append(m)
    exps = [jnp.exp(v - vals[0]) for v in vals]
    den = exps[0] + exps[1] + exps[2] + exps[3]
    ind = jnp.where(chosen, 1.0, 0.0)
    r_i = lax.broadcasted_iota(jnp.int32, (tm, tm), 0)
    c_i = lax.broadcasted_iota(jnp.int32, (tm, tm), 1)
    count = jnp.sum(ind, axis=0, keepdims=True)
    padded = jnp.floor((count + (ROW_BLOCK - 1)) * (1.0 / ROW_BLOCK)) * ROW_BLOCK
    incl = jnp.broadcast_to(padded, (8, LANES))
    lane8 = lax.broadcasted_iota(jnp.int32, (8, LANES), 1)
    d = 1
    while d < LANES:
        incl = incl + jnp.where(lane8 >= d, pltpu.roll(incl, d, 1), 0.0)
        d *= 2
    local_start = incl[0:1] - padded
    place = _mm(jnp.where(c_i < r_i, 1.0, 0.0), ind) + local_start
    slot_out = jnp.zeros(logits.shape, jnp.int32)
    wt_out = jnp.zeros(logits.shape, F32)
    for k in range(TOP_K):
        slot = jnp.sum(jnp.where(lane == ids[k], place, 0.0), axis=-1, keepdims=True)
        slot_out = jnp.where(lane == k, slot.astype(jnp.int32), slot_out)
        wt_out = jnp.where(lane == k, exps[k] / den, wt_out)
    slot_ref[...] = slot_out
    wt_ref[...] = wt_out
    row8 = lax.broadcasted_iota(jnp.int32, (8, LANES), 0)
    table = jnp.where(row8 == 0, local_start, jnp.where(row8 == 1, padded, run_ref[0:1, :]))
    meta_ref[0] = jnp.where(row8 < 3, table, 0.0).astype(jnp.int32)
    run_ref[...] = run_ref[...] + padded


def _router(x, norm_g, modseg, w_router, b_router, layer):
    n_tok = x.shape[0]
    tm = TOKEN_TILE
    per_seg = SEG_ROWS // tm
    depth = w_router.shape[0]
    w_pad = jnp.zeros((depth, D_MODEL, LANES), F32).at[:, :, :N_EXPERTS].set(w_router)
    b_pad = jnp.full((depth, 1, LANES), NEG_BIG, F32).at[:, 0, :N_EXPERTS].set(b_router)
    full = pl.BlockSpec((tm, D_MODEL), lambda i: (i, 0))
    small = pl.BlockSpec((tm, LANES), lambda i: (i, 0))
    return pl.pallas_call(
        _router_kernel,
        grid=(n_tok // tm,),
        in_specs=[full,
                  pl.BlockSpec((1, 1, D_MODEL), lambda i: (layer, 0, 0)),
                  pl.BlockSpec((1, N_MOD, D_MODEL), lambda i: (i // per_seg, 0, 0)),
                  pl.BlockSpec((1, D_MODEL, LANES), lambda i: (layer, 0, 0)),
                  pl.BlockSpec((1, 1, LANES), lambda i: (layer, 0, 0))],
        out_specs=[full, small, small, pl.BlockSpec((1, 8, LANES), lambda i: (i, 0, 0))],
        out_shape=[jax.ShapeDtypeStruct((n_tok, D_MODEL), F32),
                   jax.ShapeDtypeStruct((n_tok, LANES), jnp.int32),
                   jax.ShapeDtypeStruct((n_tok, LANES), F32),
                   jax.ShapeDtypeStruct((n_tok // tm, 8, LANES), jnp.int32)],
        scratch_shapes=[pltpu.VMEM((8, LANES), F32)],
        compiler_params=_cparams(("arbitrary",), 40),
        name="router",
    )(x, norm_g.reshape(-1, 1, D_MODEL), modseg, w_pad, b_pad)


def _for_each_block(meta_ref, fn):
    def group(e, carry):
        local0 = meta_ref[e]
        sorted0 = meta_ref[2 * N_EXPERTS + e]

        def block(j, c):
            fn(pl.multiple_of(local0 + j * ROW_BLOCK, ROW_BLOCK),
               pl.multiple_of(sorted0 + j * ROW_BLOCK, ROW_BLOCK))
            return c

        return lax.fori_loop(0, meta_ref[N_EXPERTS + e] // ROW_BLOCK, block, carry)

    lax.fori_loop(0, N_EXPERTS, group, 0)


def _slot_matrix(slot_ref, values):
    col = lax.broadcasted_iota(jnp.int32, (TOKEN_TILE, LOCAL_ROWS), 1)
    out = jnp.zeros((TOKEN_TILE, LOCAL_ROWS), F32)
    for k in range(TOP_K):
        out = jnp.where(col == slot_ref[:, k:k + 1], values[k], out)
    return out.astype(BF16)


def _dispatch_kernel(meta_ref, tail_ref, slot_ref, h_ref, xs_ref, local_ref, zero_ref, sem):
    @pl.when(pl.program_id(0) == 0)
    def _():
        zero_ref[...] = jnp.zeros_like(zero_ref)

        def tails(fn):
            def group(e, carry):
                def block(j, c):
                    fn(pl.multiple_of(tail_ref[e] + j * ROW_BLOCK, ROW_BLOCK))
                    return c
                return lax.fori_loop(0, tail_ref[N_EXPERTS + e], block, carry)
            lax.fori_loop(0, N_EXPERTS, group, 0)

        def zero_copy(row):
            return pltpu.make_async_copy(zero_ref.at[pl.ds(0, ROW_BLOCK)],
                                         xs_ref.at[pl.ds(row, ROW_BLOCK)], sem)

        tails(lambda row: zero_copy(row).start())
        tails(lambda row: zero_copy(row).wait())

        def unused(fn):
            def tile(j, c):
                fn(pl.multiple_of(tail_ref[2 * N_EXPERTS] + j * EXPERT_TILE, EXPERT_TILE))
                return c
            lax.fori_loop(0, tail_ref[2 * N_EXPERTS + 1], tile, 0)

        def zero_tile(row):
            return pltpu.make_async_copy(zero_ref, xs_ref.at[pl.ds(row, EXPERT_TILE)], sem)

        unused(lambda row: zero_tile(row).start())
        unused(lambda row: zero_tile(row).wait())

    onehot = _slot_matrix(slot_ref, [1.0] * TOP_K)
    local_ref[...] = _mm_tn(onehot, h_ref[...])

    def block_copy(local_row, sorted_row):
        return pltpu.make_async_copy(local_ref.at[pl.ds(local_row, ROW_BLOCK)],
                                     xs_ref.at[pl.ds(sorted_row, ROW_BLOCK)], sem)

    _for_each_block(meta_ref, lambda a, b: block_copy(a, b).start())
    _for_each_block(meta_ref, lambda a, b: block_copy(a, b).wait())


def _dispatch(h, slots, meta, tails, n_rows):
    n_tok = h.shape[0]
    return pl.pallas_call(
        _dispatch_kernel,
        grid=(n_tok // TOKEN_TILE,),
        in_specs=[pl.BlockSpec((LANES,), lambda i: (i,), memory_space=pltpu.SMEM),
                  pl.BlockSpec((LANES,), lambda i: (0,), memory_space=pltpu.SMEM),
                  pl.BlockSpec((TOKEN_TILE, LANES), lambda i: (i, 0)),
                  pl.BlockSpec((TOKEN_TILE, D_MODEL), lambda i: (i, 0))],
        out_specs=pl.BlockSpec(memory_space=pl.ANY),
        out_shape=jax.ShapeDtypeStruct((n_rows, D_MODEL), F32),
        scratch_shapes=[pltpu.VMEM((LOCAL_ROWS, D_MODEL), F32), pltpu.VMEM((EXPERT_TILE, D_MODEL), F32),
                        pltpu.SemaphoreType.DMA(())],
        compiler_params=_cparams(("arbitrary",), 40),
        name="dispatch",
    )(meta, tails, slots, h)


def _combine_kernel(meta_ref, slot_ref, wt_ref, ys_ref, x_ref, mod_ref, o_ref, local_ref, sem):
    local_ref[TOKEN_TILE * TOP_K:, :] = jnp.zeros((LOCAL_ROWS - TOKEN_TILE * TOP_K, D_MODEL), F32)

    def block_copy(local_row, sorted_row):
        return pltpu.make_async_copy(ys_ref.at[pl.ds(sorted_row, ROW_BLOCK)],
                                     local_ref.at[pl.ds(local_row, ROW_BLOCK)], sem)

    _for_each_block(meta_ref, lambda a, b: block_copy(a, b).start())
    _for_each_block(meta_ref, lambda a, b: block_copy(a, b).wait())
    weights = _slot_matrix(slot_ref, [wt_ref[:, k:k + 1] for k in range(TOP_K)])
    acc = jnp.dot(weights, local_ref[...].astype(BF16), preferred_element_type=F32)
    o_ref[...] = x_ref[...] + mod_ref[0, 5:6, :] * acc


def _combine(ys, slots, meta, wts, x, modseg):
    n_tok = x.shape[0]
    per_seg = SEG_ROWS // TOKEN_TILE
    full = pl.BlockSpec((TOKEN_TILE, D_MODEL), lambda i: (i, 0))
    small = pl.BlockSpec((TOKEN_TILE, LANES), lambda i: (i, 0))
    return pl.pallas_call(
        _combine_kernel,
        grid=(n_tok // TOKEN_TILE,),
        in_specs=[pl.BlockSpec((LANES,), lambda i: (i,), memory_space=pltpu.SMEM),
                  small, small,
                  pl.BlockSpec(memory_space=pl.ANY),
                  full,
                  pl.BlockSpec((1, N_MOD, D_MODEL), lambda i: (i // per_seg, 0, 0))],
        out_specs=full,
        out_shape=jax.ShapeDtypeStruct((n_tok, D_MODEL), F32),
        scratch_shapes=[pltpu.VMEM((LOCAL_ROWS, D_MODEL), F32), pltpu.SemaphoreType.DMA(())],
        compiler_params=_cparams(("arbitrary",), 48),
        name="combine",
    )(meta, slots, wts, ys, x, modseg)


def _expert_kernel(te_ref, na_ref, x_ref, wgu_ref, bgu_ref, wd_ref, bd_ref, o_ref, wgu_s, wd_s):
    i = pl.program_id(0)
    active = i < na_ref[0]
    fresh = jnp.logical_or(i == 0, te_ref[i] != te_ref[jnp.maximum(i - 1, 0)])

    @pl.when(jnp.logical_and(active, fresh))
    def _():
        wgu_s[...] = wgu_ref[0, 0].astype(BF16)
        wd_s[...] = wd_ref[0, 0].astype(BF16)

    @pl.when(active)
    def _():
        gu = jnp.dot(x_ref[...].astype(BF16), wgu_s[...], preferred_element_type=F32) + bgu_ref[0, 0]
        gate = jnp.minimum(gu[:, :D_FF], SWIGLU_LIMIT)
        up = jnp.clip(gu[:, D_FF:], -SWIGLU_LIMIT, SWIGLU_LIMIT)
        act = gate * _sigmoid(SWIGLU_ALPHA * gate) * (up + 1.0)
        o_ref[...] = jnp.dot(act.astype(BF16), wd_s[...], preferred_element_type=F32) + bd_ref[0, 0]

    @pl.when(jnp.logical_not(active))
    def _():
        o_ref[...] = jnp.zeros_like(o_ref)


def _experts(xs, tile_expert, n_active, w_gu, b_gu, w_down, b_down, layer):
    n_rows = xs.shape[0]
    depth = w_gu.shape[0]
    tm = EXPERT_TILE
    grid_spec = pltpu.PrefetchScalarGridSpec(
        num_scalar_prefetch=2,
        grid=(n_rows // tm,),
        in_specs=[pl.BlockSpec((tm, D_MODEL), lambda i, te, na: (jnp.where(i < na[0], i, 0), 0)),
                  pl.BlockSpec((1, 1, D_MODEL, 2 * D_FF), lambda i, te, na: (layer, te[i], 0, 0)),
                  pl.BlockSpec((1, 1, 1, 2 * D_FF), lambda i, te, na: (layer, te[i], 0, 0)),
                  pl.BlockSpec((1, 1, D_FF, D_MODEL), lambda i, te, na: (layer, te[i], 0, 0)),
                  pl.BlockSpec((1, 1, 1, D_MODEL), lambda i, te, na: (layer, te[i], 0, 0))],
        out_specs=pl.BlockSpec((tm, D_MODEL), lambda i, te, na: (i, 0)),
        scratch_shapes=[pltpu.VMEM((D_MODEL, 2 * D_FF), BF16), pltpu.VMEM((D_FF, D_MODEL), BF16)],
    )
    return pl.pallas_call(
        _expert_kernel,
        grid_spec=grid_spec,
        out_shape=jax.ShapeDtypeStruct((n_rows, D_MODEL), F32),
        compiler_params=_cparams(("arbitrary",), 56),
        name="experts",
    )(tile_expert, n_active, xs, w_gu, b_gu.reshape(depth, N_EXPERTS, 1, 2 * D_FF),
      w_down, b_down.reshape(depth, N_EXPERTS, 1, D_MODEL))


def _moe(x, norm_g, modseg, w_router, b_router, w_gu, b_gu, w_down, b_down, layer):
    n_tok = x.shape[0]
    h, slots, wts, table = _router(x, norm_g, modseg, w_router, b_router, layer)
    tm = EXPERT_TILE
    n_tiles = n_tok // TOKEN_TILE
    n_rows = n_tok * TOP_K + n_tiles * N_EXPERTS * ROW_BLOCK + N_EXPERTS * tm
    n_rows = -(-n_rows // tm) * tm
    local_start = table[:, 0, :N_EXPERTS]
    padded = table[:, 1, :N_EXPERTS]
    earlier = table[:, 2, :N_EXPERTS]
    used = earlier[-1] + padded[-1]
    size = ((used + tm - 1) // tm) * tm
    ends = jnp.cumsum(size)
    starts = ends - size
    sorted_start = starts[None, :] + earlier
    meta = jnp.concatenate([local_start, padded, sorted_start, jnp.zeros_like(padded)], axis=1).reshape(-1)
    tails = jnp.concatenate([starts + used, (size - used) // ROW_BLOCK,
                             ends[-1:], (n_rows - ends[-1:]) // tm,
                             jnp.zeros((LANES - 2 * N_EXPERTS - 2,), jnp.int32)])
    tile_row0 = jnp.arange(n_rows // tm, dtype=jnp.int32) * tm
    tile_expert = jnp.minimum(jnp.sum((ends[None, :] <= tile_row0[:, None]).astype(jnp.int32), axis=1),
                              N_EXPERTS - 1)
    n_active = (ends[-1:] // tm).astype(jnp.int32)
    xs = _dispatch(h, slots, meta, tails, n_rows)
    ys = _experts(xs, tile_expert, n_active, w_gu, b_gu, w_down, b_down, layer)
    return _combine(ys, slots, meta, wts, x, modseg)


def _rope_tables(n_tok):
    rows = n_tok // GRID_W
    row = jnp.repeat(jnp.arange(rows, dtype=F32), GRID_W)
    col = jnp.tile(jnp.arange(GRID_W, dtype=F32), rows)
    quarter = HEAD_DIM // 4
    inv_freq = ROPE_BASE ** (-jnp.arange(quarter, dtype=F32) / quarter)
    ang_r = row[:, None] * inv_freq
    ang_c = col[:, None] * inv_freq
    ang = jnp.concatenate([ang_r, ang_r, ang_c, ang_c], axis=-1)
    cos, sin = jnp.cos(ang), jnp.sin(ang)
    first = (jnp.arange(HEAD_DIM) % (2 * quarter)) < quarter
    sin_a = jnp.where(first, -sin, 0.0)
    sin_b = jnp.where(first, 0.0, sin)
    return tuple(jnp.tile(t, (1, LANES // HEAD_DIM)) for t in (cos, sin_a, sin_b))


def _block_diag(w):
    eye = jnp.eye(RG_BLOCKS, dtype=w.dtype)
    return jnp.einsum("nde,nm->ndme", w, eye).reshape(D_RNN, D_RNN)


def kernel(x_prompt, x_sample, c, cache_k, cache_v, state_rglru, state_hgrn, c_ctx, w_mod, b_mod, norm1_g, norm2_g, w_in, q_norm_g, k_norm_g, conv_w, conv_b, rg_wa, rg_ba, rg_wx, rg_bx, rg_lambda, hgrn_lb_logits, hgrn_norm_g, w_attn_o, w_rnn_o, w_hgrn_o, w_out, w_router, b_router, w_gu, b_gu, w_down, b_down):
    depth = w_mod.shape[0]
    n_p, t_p = x_prompt.shape[0], x_prompt.shape[1]
    n_s, t_s = x_sample.shape[0], x_sample.shape[1]
    tok_p = n_p * t_p
    tok_s = n_s * t_s
    assert tok_p % SEG_ROWS == 0 and t_s == SEG_ROWS and SEG_ROWS % t_p == 0

    x = jnp.concatenate([x_prompt.reshape(tok_p, D_MODEL), x_sample.reshape(tok_s, D_MODEL)], axis=0)

    lb_all = jnp.cumsum(jax.nn.softmax(hgrn_lb_logits.astype(F32), axis=0), axis=0)
    lb_all = lb_all - lb_all[:1]
    rope = _rope_tables(t_s)

    cond8 = jnp.zeros((8, D_MODEL), F32).at[0].set(c_ctx).at[1:1 + n_s].set(c)
    mod = _modulation(cond8, w_mod, b_mod)
    seg_ids = jnp.asarray([0] * (tok_p // SEG_ROWS) + [1 + b for b in range(n_s)], jnp.int32)
    modsegs = mod[:, seg_ids].reshape(depth, seg_ids.shape[0], N_MOD, D_MODEL)

    w_gate = jnp.stack([jnp.concatenate([_block_diag(rg_wa[l, 0]), _block_diag(rg_wx[l, 0]),
                                         _block_diag(rg_wa[l, 1]), _block_diag(rg_wx[l, 1])], axis=1)
                        for l in range(depth)])
    b_gate = jnp.stack([jnp.concatenate([rg_ba[l, 0], rg_bx[l, 0], rg_ba[l, 1], rg_bx[l, 1]])[None]
                        for l in range(depth)])

    new_k, new_v, new_hr, new_s = [], [], [], []
    for l in range(depth):
        modseg = modsegs[l]
        z = _input_projection(x, norm1_g, modseg, w_in, l)
        attn_p, k_l, v_l = _attention(z, q_norm_g, k_norm_g, l, n_p, t_p, 0)
        (attn_s,) = _attention(z, q_norm_g, k_norm_g, l, n_s, t_s, tok_p, rope=rope,
                               cache=(cache_k, cache_v))
        rnn_p, hr_l = _rglru(z, conv_w, conv_b, w_gate, b_gate, rg_lambda, l, n_p, t_p, 0)
        (rnn_s,) = _rglru(z, conv_w, conv_b, w_gate, b_gate, rg_lambda, l, n_s, t_s, tok_p,
                          state=state_rglru)
        hg_p, s_l = _hgrn(z, lb_all, hgrn_norm_g, l, n_p, t_p, 0)
        (hg_s,) = _hgrn(z, lb_all, hgrn_norm_g, l, n_s, t_s, tok_p, state=state_hgrn)
        attn = jnp.concatenate([attn_p, attn_s], axis=0)
        rnn = jnp.concatenate([rnn_p, rnn_s], axis=0)
        hg = jnp.concatenate([hg_p, hg_s], axis=0)
        x = _merge_project(attn, rnn, hg, z, x, modseg, w_attn_o, w_rnn_o, w_hgrn_o, w_out, l)
        x = _moe(x, norm2_g, modseg, w_router, b_router, w_gu, b_gu, w_down, b_down, l)
        new_k.append(k_l.reshape(n_p, t_p, N_KV_HEADS, HEAD_DIM))
        new_v.append(v_l.reshape(n_p, t_p, N_KV_HEADS, HEAD_DIM))
        new_hr.append(hr_l)
        new_s.append(s_l)

    y_prompt = x[:tok_p].reshape(n_p, t_p, D_MODEL)
    y_sample = x[tok_p:].reshape(n_s, t_s, D_MODEL)
    return (y_prompt, y_sample, jnp.stack(new_k, axis=1), jnp.stack(new_v, axis=1),
            jnp.stack(new_hr, axis=1), jnp.stack(new_s, axis=1))
```

```python
import functools

import jax
import jax.numpy as jnp
from jax import lax
from jax.experimental import pallas as pl
from jax.experimental.pallas import tpu as pltpu

F32 = jnp.float32
BF16 = jnp.bfloat16

D_MODEL = 1024
GRID_W = 64
HEAD_DIM = 64
N_HEADS = 8
N_KV_HEADS = 2
KV_GROUP = N_HEADS // N_KV_HEADS
ATTN_W = N_HEADS * HEAD_DIM
KV_W = N_KV_HEADS * HEAD_DIM
ROPE_BASE = 10000.0
D_RNN = 512
RG_BLOCKS = 8
RG_BW = D_RNN // RG_BLOCKS
RG_C = 8.0
HG_HEADS = 8
HG_DK = 64
HG_DV = 64
HG_W = HG_HEADS * HG_DK
N_PAIRS = HG_W // 128
N_EXPERTS = 32
TOP_K = 4
D_FF = D_MODEL
SWIGLU_LIMIT = 7.0
SWIGLU_ALPHA = 1.702
N_MOD = 6
EPS = 1e-6
IN_W = ATTN_W + 2 * KV_W + 2 * D_RNN + 5 * HG_W + 3 * D_MODEL

LANES = 128
SEG_ROWS = 1024
COL_TILE = 256
HG_CHUNK = 256
Q_TILE = 256
EXPERT_TILE = 384
BIG_BLOCK = 64
TOKEN_TILE = 512
ROW_BLOCK = 8
LOCAL_ROWS = TOKEN_TILE * TOP_K + N_EXPERTS * ROW_BLOCK
NEG_BIG = -1e30
LOG2_E = 1.4426950408889634

_SRC_TILES = tuple(range(17, 29)) + tuple(range(7, 17)) + tuple(range(3, 7)) + (0, 1, 2)
Z_GL = 0
Z_HG = 3 * D_MODEL
Z_RG = Z_HG + 5 * HG_W
Z_Q = Z_RG + 2 * D_RNN
Z_K = Z_Q + ATTN_W
Z_V = Z_K + KV_W
Z_W = -(-IN_W // (2 * COL_TILE)) * (2 * COL_TILE)


def _cparams(sem, vmem_mb):
    return pltpu.CompilerParams(dimension_semantics=sem, vmem_limit_bytes=vmem_mb * 1024 * 1024)


def _mm(a, b):
    return jnp.dot(a.astype(BF16), b.astype(BF16), preferred_element_type=F32)


def _mm_nt(a, b):
    return lax.dot_general(a.astype(BF16), b.astype(BF16), (((1,), (1,)), ((), ())),
                           preferred_element_type=F32)


def _mm_tn(a, b):
    return lax.dot_general(a.astype(BF16), b.astype(BF16), (((0,), (0,)), ((), ())),
                           preferred_element_type=F32)


def _sigmoid(x):
    return 1.0 / (1.0 + jnp.exp(-x))


def _softplus(x):
    return jnp.maximum(x, 0.0) + jnp.log1p(jnp.exp(-jnp.abs(x)))


def _modulated_norm(x, g, scale, shift):
    y = x * lax.rsqrt(jnp.mean(x * x, axis=-1, keepdims=True) + EPS)
    return (y * g) * (1.0 + scale) + shift


def _mod_kernel(c_ref, w_ref, b_ref, o_ref):
    c = c_ref[...]
    s = c * _sigmoid(c)
    o_ref[0] = _mm(s, w_ref[0]) + b_ref[0]


def _modulation(cond8, w_mod, b_mod):
    depth = w_mod.shape[0]
    tn = 1024
    return pl.pallas_call(
        _mod_kernel,
        grid=(depth, N_MOD * D_MODEL // tn),
        in_specs=[pl.BlockSpec((8, D_MODEL), lambda l, j: (0, 0)),
                  pl.BlockSpec((1, D_MODEL, tn), lambda l, j: (l, 0, j)),
                  pl.BlockSpec((1, 1, tn), lambda l, j: (l, 0, j))],
        out_specs=pl.BlockSpec((1, 8, tn), lambda l, j: (l, 0, j)),
        out_shape=jax.ShapeDtypeStruct((depth, 8, N_MOD * D_MODEL), F32),
        compiler_params=_cparams(("arbitrary", "arbitrary"), 32),
        name="modulation",
    )(cond8, w_mod, b_mod.reshape(depth, 1, N_MOD * D_MODEL))


def _inproj_kernel(src_ref, x_ref, g_ref, mod_ref, wa_ref, wb_ref, o_ref, h_ref, w_s):
    j = pl.program_id(1)

    @pl.when(j == 0)
    def _():
        h = _modulated_norm(x_ref[...], g_ref[0], mod_ref[0, 1:2, :], mod_ref[0, 0:1, :])
        h_ref[...] = h.astype(BF16)

    @pl.when(pl.program_id(0) == 0)
    def _():
        w_s[j, :, 0:COL_TILE] = wa_ref[0].astype(BF16)
        w_s[j, :, COL_TILE:2 * COL_TILE] = wb_ref[0].astype(BF16)

    o_ref[...] = jnp.dot(h_ref[...], w_s[j], preferred_element_type=F32)


def _input_projection(x, norm_g, modseg, w_in, layer):
    n_tok = x.shape[0]
    tm = SEG_ROWS
    n_col = Z_W // (2 * COL_TILE)
    src = jnp.asarray(_SRC_TILES + (_SRC_TILES[-1],) * (Z_W // COL_TILE - len(_SRC_TILES)), jnp.int32)

    def wspec(k):
        return pl.BlockSpec((1, D_MODEL, COL_TILE),
                            lambda i, j, s: (layer, 0, s[2 * jnp.where(i == 0, j, n_col - 1) + k]))

    grid_spec = pltpu.PrefetchScalarGridSpec(
        num_scalar_prefetch=1,
        grid=(n_tok // tm, n_col),
        in_specs=[pl.BlockSpec((tm, D_MODEL), lambda i, j, s: (i, 0)),
                  pl.BlockSpec((1, 1, D_MODEL), lambda i, j, s: (layer, 0, 0)),
                  pl.BlockSpec((1, N_MOD, D_MODEL), lambda i, j, s: (i, 0, 0)),
                  wspec(0), wspec(1)],
        out_specs=pl.BlockSpec((tm, 2 * COL_TILE), lambda i, j, s: (i, j)),
        scratch_shapes=[pltpu.VMEM((tm, D_MODEL), BF16),
                        pltpu.VMEM((n_col, D_MODEL, 2 * COL_TILE), BF16)],
    )
    return pl.pallas_call(
        _inproj_kernel,
        grid_spec=grid_spec,
        out_shape=jax.ShapeDtypeStruct((n_tok, Z_W), F32),
        compiler_params=_cparams(("arbitrary", "arbitrary"), 40),
        name="input_projection",
    )(src, x, norm_g.reshape(-1, 1, D_MODEL), modseg, w_in, w_in)


def _head_rms(x, g):
    lane = lax.broadcasted_iota(jnp.int32, x.shape, 1)
    lo = lane < HEAD_DIM
    xx = x * x
    s_lo = jnp.sum(jnp.where(lo, xx, 0.0), axis=-1, keepdims=True)
    s_hi = jnp.sum(jnp.where(lo, 0.0, xx), axis=-1, keepdims=True)
    inv = jnp.where(lo, lax.rsqrt(s_lo * (1.0 / HEAD_DIM) + EPS),
                    lax.rsqrt(s_hi * (1.0 / HEAD_DIM) + EPS))
    return x * inv * g


def _rope(x, cos, sin_a, sin_b):
    q = HEAD_DIM // 4
    return x * cos + pltpu.roll(x, LANES - q, 1) * sin_a + pltpu.roll(x, q, 1) * sin_b


def _attn_kernel(*refs, t_len, use_ctx):
    if use_ctx:
        (q_ref, k_ref, v_ref, qg_ref, kg_ref, cos_q_ref, sa_q_ref, sb_q_ref,
         cos_k_ref, sa_k_ref, sb_k_ref, ck_ref, cv_ref, o_ref, kall_ref, vall_ref) = refs
    else:
        (q_ref, k_ref, v_ref, qg_ref, kg_ref, o_ref, ko_ref, vo_ref, kall_ref, vall_ref) = refs

    @pl.when(pl.program_id(1) == 0)
    def _():
        k = _head_rms(k_ref[...], kg_ref[...])
        if use_ctx:
            k = _rope(k, cos_k_ref[...], sa_k_ref[...], sb_k_ref[...])
            kall_ref[0:t_len, :] = k.astype(BF16)
            vall_ref[0:t_len, :] = v_ref[...].astype(BF16)
            kall_ref[t_len:, :] = ck_ref[0, 0].astype(BF16)
            vall_ref[t_len:, :] = cv_ref[0, 0].astype(BF16)
        else:
            ko_ref[0] = k
            vo_ref[0] = v_ref[...]
            kall_ref[...] = k.astype(BF16)
            vall_ref[...] = v_ref[...].astype(BF16)

    heads = []
    for c in range(ATTN_W // LANES):
        qc = _head_rms(q_ref[:, c * LANES:(c + 1) * LANES], qg_ref[...])
        if use_ctx:
            qc = _rope(qc, cos_q_ref[...], sa_q_ref[...], sb_q_ref[...])
        qc = (qc * (HEAD_DIM ** -0.5)).astype(BF16)
        heads.append(qc[:, :HEAD_DIM])
        heads.append(qc[:, HEAD_DIM:])
    tq = q_ref.shape[0]
    for g in range(N_KV_HEADS):
        qs = jnp.concatenate(heads[g * KV_GROUP:(g + 1) * KV_GROUP], axis=0)
        kh = kall_ref[:, g * HEAD_DIM:(g + 1) * HEAD_DIM]
        vh = vall_ref[:, g * HEAD_DIM:(g + 1) * HEAD_DIM]
        s = _mm_nt(qs, kh)
        m = jnp.max(s, axis=-1, keepdims=True)
        p = jnp.exp(s - m)
        den = jnp.sum(p, axis=-1, keepdims=True)
        o = _mm(p, vh) / den
        for j in range(KV_GROUP):
            hh = g * KV_GROUP + j
            o_ref[:, hh * HEAD_DIM:(hh + 1) * HEAD_DIM] = o[j * tq:(j + 1) * tq].astype(BF16)


def _attention(z, q_g, k_g, layer, n_batch, t_len, row0, rope=None, cache=None):
    use_ctx = cache is not None
    nq = t_len // Q_TILE
    rb0 = row0 // t_len
    qb0 = row0 // Q_TILE
    qg = jnp.tile(q_g[layer], 2).reshape(1, LANES)
    kg = jnp.tile(k_g[layer], 2).reshape(1, LANES)
    n_out_tok = n_batch * t_len
    vec = pl.BlockSpec((1, LANES), lambda b, i: (0, 0))
    in_specs = [pl.BlockSpec((Q_TILE, ATTN_W), lambda b, i: (qb0 + b * nq + i, Z_Q // ATTN_W)),
                pl.BlockSpec((t_len, KV_W), lambda b, i: (rb0 + b, Z_K // KV_W)),
                pl.BlockSpec((t_len, KV_W), lambda b, i: (rb0 + b, Z_V // KV_W)),
                vec, vec]
    args = [z, z, z, qg, kg]
    out_specs = [pl.BlockSpec((Q_TILE, ATTN_W), lambda b, i: (b * nq + i, 0))]
    out_shape = [jax.ShapeDtypeStruct((n_out_tok, ATTN_W), BF16)]
    t_keys = t_len
    if use_ctx:
        cos, sin_a, sin_b = rope
        cache_k, cache_v = cache
        past = cache_k.shape[2]
        t_keys = t_len + past
        tq_spec = pl.BlockSpec((Q_TILE, LANES), lambda b, i: (i, 0))
        tk_spec = pl.BlockSpec((t_len, LANES), lambda b, i: (0, 0))
        c_spec = pl.BlockSpec((1, 1, past, KV_W), lambda b, i: (b, layer, 0, 0))
        in_specs += [tq_spec, tq_spec, tq_spec, tk_spec, tk_spec, tk_spec, c_spec, c_spec]
        args += [cos, sin_a, sin_b, cos, sin_a, sin_b,
                 cache_k.reshape(cache_k.shape[0], cache_k.shape[1], past, KV_W),
                 cache_v.reshape(cache_v.shape[0], cache_v.shape[1], past, KV_W)]
    else:
        kv_spec = pl.BlockSpec((1, t_len, KV_W), lambda b, i: (b, 0, 0))
        out_specs += [kv_spec, kv_spec]
        out_shape += [jax.ShapeDtypeStruct((n_batch, t_len, KV_W), F32)] * 2
    return pl.pallas_call(
        functools.partial(_attn_kernel, t_len=t_len, use_ctx=use_ctx),
        grid=(n_batch, nq),
        in_specs=in_specs,
        out_specs=out_specs,
        out_shape=out_shape,
        scratch_shapes=[pltpu.VMEM((t_keys, KV_W), BF16), pltpu.VMEM((t_keys, KV_W), BF16)],
        compiler_params=_cparams(("arbitrary", "arbitrary"), 48),
        name="attention_ctx" if use_ctx else "attention",
    )(*args)


def _shift_rows(x, d):
    n = x.shape[0]
    row = lax.broadcasted_iota(jnp.int32, x.shape, 0)
    y = pltpu.roll(x, d % n, 0)
    if d > 0:
        return jnp.where(row >= d, y, 0.0)
    return jnp.where(row < n + d, y, 0.0)


def _linear_scan(a, u, reverse):
    n = a.shape[0]
    row = lax.broadcasted_iota(jnp.int32, a.shape, 0)
    d = 1
    while d < n:
        if reverse:
            keep = row < n - d
            a_s = jnp.where(keep, pltpu.roll(a, n - d, 0), 1.0)
            u_s = jnp.where(keep, pltpu.roll(u, n - d, 0), 0.0)
        else:
            keep = row >= d
            a_s = jnp.where(keep, pltpu.roll(a, d, 0), 1.0)
            u_s = jnp.where(keep, pltpu.roll(u, d, 0), 0.0)
        u = a * u_s + u
        a = a * a_s
        d *= 2
    return a, u


def _rglru_kernel(*refs, use_ctx):
    if use_ctx:
        (x_ref, y_ref, cw_ref, cb_ref, wg_ref, bg_ref, lam_ref, h0_ref, o_ref) = refs
    else:
        (x_ref, y_ref, cw_ref, cb_ref, wg_ref, bg_ref, lam_ref, o_ref, last_ref) = refs
    x = x_ref[...]
    cw = cw_ref[0]
    xr = cb_ref[0] + _shift_rows(x, 1) * cw[0:1] + x * cw[1:2] \
        + _shift_rows(x, -1) * cw[2:3] + _shift_rows(x, -2) * cw[3:4]
    gates = _sigmoid(_mm(xr, wg_ref[0]) + bg_ref[0])
    n = x.shape[0]
    total = None
    lasts = []
    for d in range(2):
        r = gates[:, (2 * d) * D_RNN:(2 * d + 1) * D_RNN]
        i = gates[:, (2 * d + 1) * D_RNN:(2 * d + 2) * D_RNN]
        log_a = (-RG_C * _softplus(-lam_ref[0, d:d + 1, :])) * r
        a = jnp.exp(log_a)
        u = jnp.sqrt(1.0 - a * a) * (i * xr)
        a_cum, h = _linear_scan(a, u, reverse=(d == 1))
        if use_ctx:
            h = h + a_cum * h0_ref[0, 0, d:d + 1, :]
        else:
            lasts.append(h[0:1] if d == 1 else h[n - 1:n])
        total = h if total is None else total + h
    o_ref[...] = (jax.nn.gelu(y_ref[...]) * total).astype(BF16)
    if not use_ctx:
        last_ref[0] = jnp.concatenate(lasts, axis=0)


def _rglru(z, conv_w, conv_b, w_gate, b_gate, lam, layer, n_batch, t_len, row0, state=None):
    use_ctx = state is not None
    rb0 = row0 // t_len
    in_specs = [pl.BlockSpec((t_len, D_RNN), lambda b: (rb0 + b, Z_RG // D_RNN)),
                pl.BlockSpec((t_len, D_RNN), lambda b: (rb0 + b, Z_RG // D_RNN + 1)),
                pl.BlockSpec((1, 4, D_RNN), lambda b: (layer, 0, 0)),
                pl.BlockSpec((1, 1, D_RNN), lambda b: (layer, 0, 0)),
                pl.BlockSpec((1, D_RNN, 4 * D_RNN), lambda b: (layer, 0, 0)),
                pl.BlockSpec((1, 1, 4 * D_RNN), lambda b: (layer, 0, 0)),
                pl.BlockSpec((1, 2, D_RNN), lambda b: (layer, 0, 0))]
    args = [z, z, conv_w, conv_b.reshape(-1, 1, D_RNN), w_gate, b_gate, lam]
    out_specs = [pl.BlockSpec((t_len, D_RNN), lambda b: (b, 0))]
    out_shape = [jax.ShapeDtypeStruct((n_batch * t_len, D_RNN), BF16)]
    if use_ctx:
        in_specs.append(pl.BlockSpec((1, 1, 2, D_RNN), lambda b: (b, layer, 0, 0)))
        args.append(state)
    else:
        out_specs.append(pl.BlockSpec((1, 2, D_RNN), lambda b: (b, 0, 0)))
        out_shape.append(jax.ShapeDtypeStruct((n_batch, 2, D_RNN), F32))
    return pl.pallas_call(
        functools.partial(_rglru_kernel, use_ctx=use_ctx),
        grid=(n_batch,),
        in_specs=in_specs,
        out_specs=out_specs,
        out_shape=out_shape,
        compiler_params=_cparams(("arbitrary",), 56),
        name="rglru_ctx" if use_ctx else "rglru",
    )(*args)


def _cumsum_rows(g, reverse):
    n, w = g.shape
    m = n // ROW_BLOCK

    def scan(x, size, pos):
        d = 1
        while d < size:
            if reverse:
                x = x + jnp.where(pos < size - d, pltpu.roll(x, x.shape[0] - d, 0), 0.0)
            else:
                x = x + jnp.where(pos >= d, pltpu.roll(x, d, 0), 0.0)
            d *= 2
        return x

    g = scan(g, ROW_BLOCK, lax.broadcasted_iota(jnp.int32, g.shape, 0) & (ROW_BLOCK - 1))
    g3 = g.reshape(m, ROW_BLOCK, w)
    total = g3[:, 0, :] if reverse else g3[:, ROW_BLOCK - 1, :]
    before = scan(total, m, lax.broadcasted_iota(jnp.int32, total.shape, 0)) - total
    return (g3 + before[:, None, :]).reshape(n, w)


def _block_row(gc, blk, r):
    n, w = gc.shape
    if blk >= 8:
        g3 = gc.reshape(n // blk, blk, w)
        return jnp.broadcast_to(g3[:, r:r + 1, :], g3.shape).reshape(n, w)
    g3 = gc.reshape(n // 8, 8, w)
    sub = lax.broadcasted_iota(jnp.int32, g3.shape, 1)
    out = jnp.broadcast_to(g3[:, r:r + 1, :], g3.shape)
    for b0 in range(blk, 8, blk):
        out = jnp.where(sub >= b0, jnp.broadcast_to(g3[:, b0 + r:b0 + r + 1, :], g3.shape), out)
    return out.reshape(n, w)


def _first_head(shape):
    lane = lax.broadcasted_iota(jnp.int32, shape, len(shape) - 1)
    return (lane & (LANES - 1)) < HG_DK


def _pair(x, p):
    return x[:, p * LANES:(p + 1) * LANES]


def _hgrn_gates(f_logit, lb):
    y = jnp.exp(-jnp.abs(f_logit))
    log_sig = jnp.minimum(f_logit, 0.0) - jnp.log(1.0 + y)
    a = jnp.log(lb)
    b = jnp.log1p(-lb) + log_sig
    log_f = jnp.maximum(a, b) + jnp.log(1.0 + jnp.exp(-jnp.abs(a - b)))
    k = (1.0 - lb) * (jnp.where(f_logit >= 0.0, y, 1.0) / (1.0 + y))
    return log_f * LOG2_E, k


def _hgrn_chunk_state(gc, k, v, reverse):
    n = gc.shape[0]
    g_tot = gc[0:1] if reverse else gc[n - 1:n]
    k_end = (k * jnp.exp2(jnp.minimum(g_tot - gc, 0.0))).astype(BF16)
    vb = v.astype(BF16)
    same_head = (_first_head((LANES, LANES))
                 == (lax.broadcasted_iota(jnp.int32, (LANES, LANES), 0) < HG_DK))
    ds = [jnp.where(same_head, _mm_tn(_pair(k_end, p), _pair(vb, p)), 0.0) for p in range(N_PAIRS)]
    return ds, jnp.exp2(g_tot)


def _decay_state(s, e_row):
    n = s.shape[0]
    eye = (lax.broadcasted_iota(jnp.int32, (n, n), 0) == lax.broadcasted_iota(jnp.int32, (n, n), 1))
    col = jnp.sum(jnp.where(eye, jnp.broadcast_to(e_row, (n, n)), 0.0), axis=1, keepdims=True)
    return s * col


def _hgrn_intra_scores(q, ks, gcs, txs):
    n = q.shape[0]
    row = lax.broadcasted_iota(jnp.int32, q.shape, 0)
    first = _first_head(q.shape)
    scores = [None] * HG_HEADS
    w = 1
    while w < n:
        blk = 2 * w
        second = (row & w) != 0
        e_f = jnp.exp2(-jnp.abs(gcs[0] - _block_row(gcs[0], blk, w - 1)))
        e_b = jnp.exp2(-jnp.abs(gcs[1] - _block_row(gcs[1], blk, w)))
        q_f = jnp.where(second, q * e_f, 0.0).astype(BF16)
        q_b = jnp.where(second, 0.0, q * e_b).astype(BF16)
        k_f = jnp.where(second, 0.0, ks[0] * e_f)
        k_b = jnp.where(second, ks[1] * e_b, 0.0)
        k_heads = [(jnp.where(first, k_f, 0.0).astype(BF16), jnp.where(first, k_b, 0.0).astype(BF16)),
                   (jnp.where(first, 0.0, k_f).astype(BF16), jnp.where(first, 0.0, k_b).astype(BF16))]
        sel = (txs >= w) & (txs < blk)
        for p in range(N_PAIRS):
            qc = jnp.concatenate([_pair(q_f, p), _pair(q_b, p)], axis=1)
            for j in range(2):
                kc = jnp.concatenate([_pair(k_heads[j][0], p), _pair(k_heads[j][1], p)], axis=1)
                s = jnp.where(sel, _mm_nt(qc, kc), 0.0)
                h = 2 * p + j
                scores[h] = s if scores[h] is None else scores[h] + s
        w = blk
    return scores


def _hgrn_kernel(*refs, t_len, use_ctx):
    if use_ctx:
        (q_ref, ff_ref, fb_ref, i_ref, og_ref, lb_ref, ng_ref, s0_ref, o_ref, sin_ref, gk_ref) = refs
    else:
        (q_ref, ff_ref, fb_ref, i_ref, og_ref, lb_ref, ng_ref, o_ref, sl_ref) = refs
    c = HG_CHUNK
    n_chunks = t_len // c
    f_refs = (ff_ref, fb_ref)

    def load_dir(d, r0):
        log_f, k = _hgrn_gates(f_refs[d][pl.ds(r0, c), :], lb_ref[0, d:d + 1, :])
        return _cumsum_rows(log_f, reverse=(d == 1)), k

    if use_ctx:
        zero = jnp.zeros((HG_DK, HG_DV), F32)
        for d in range(2):
            states = []
            for p in range(N_PAIRS):
                top = jnp.concatenate([s0_ref[0, 0, d, 2 * p], zero], axis=1)
                bot = jnp.concatenate([zero, s0_ref[0, 0, d, 2 * p + 1]], axis=1)
                states.append(jnp.concatenate([top, bot], axis=0))
            order = range(n_chunks) if d == 0 else range(n_chunks - 1, -1, -1)
            for ci in order:
                for p in range(N_PAIRS):
                    sin_ref[ci, p, d * LANES:(d + 1) * LANES, :] = states[p]
                gc, k = load_dir(d, ci * c)
                gk_ref[d, 0, pl.ds(ci * c, c), :] = gc
                gk_ref[d, 1, pl.ds(ci * c, c), :] = k
                ds, e_tot = _hgrn_chunk_state(gc, k, i_ref[pl.ds(ci * c, c), :], reverse=(d == 1))
                states = [_decay_state(states[p], _pair(e_tot, p)) + ds[p] for p in range(N_PAIRS)]

    ti = lax.broadcasted_iota(jnp.int32, (c, c), 0)
    si = lax.broadcasted_iota(jnp.int32, (c, c), 1)
    txs = ti ^ si
    diag = ti == si

    def chunk_body(ci, carry):
        r0 = pl.multiple_of(ci * c, c)
        hq = q_ref[pl.ds(r0, c), :]
        q = hq * _sigmoid(hq)
        v = i_ref[pl.ds(r0, c), :]
        gcs, ks = [], []
        for d in range(2):
            if use_ctx:
                gc, k = gk_ref[d, 0, pl.ds(r0, c), :], gk_ref[d, 1, pl.ds(r0, c), :]
            else:
                gc, k = load_dir(d, r0)
            gcs.append(gc)
            ks.append(k)
        scores = _hgrn_intra_scores(q, ks, gcs, txs)
        first = _first_head(q.shape)
        k_sum = ks[0] + ks[1]
        v_heads = (jnp.where(first, v, 0.0).astype(BF16), jnp.where(first, 0.0, v).astype(BF16))
        k_heads = (jnp.where(first, k_sum, 0.0).astype(BF16), jnp.where(first, 0.0, k_sum).astype(BF16))
        q_bf = q.astype(BF16)
        if use_ctx:
            qe = [(q * jnp.exp2(gcs[d])).astype(BF16) for d in range(2)]
        else:
            for d in range(2):
                ds, _ = _hgrn_chunk_state(gcs[d], ks[d], v, reverse=(d == 1))
                for p in range(N_PAIRS):
                    sl_ref[0, d, 2 * p] = ds[p][:HG_DK, :HG_DV]
                    sl_ref[0, d, 2 * p + 1] = ds[p][HG_DK:, HG_DV:]
        outs = []
        for p in range(N_PAIRS):
            o = None
            for j in range(2):
                s_diag = jnp.where(diag, _mm_nt(_pair(q_bf, p), _pair(k_heads[j], p)), 0.0)
                t = _mm(scores[2 * p + j] + s_diag, _pair(v_heads[j], p))
                o = t if o is None else o + t
            if use_ctx:
                o = o + _mm(jnp.concatenate([_pair(qe[0], p), _pair(qe[1], p)], axis=1), sin_ref[ci, p])
            outs.append(_head_rms(o, ng_ref[0]))
        og = og_ref[pl.ds(r0, c), :]
        o_all = jnp.concatenate(outs, axis=1) * (og * _sigmoid(og))
        o_ref[pl.ds(r0, c), :] = o_all.astype(BF16)
        return carry

    if n_chunks == 1:
        chunk_body(0, 0)
    else:
        lax.fori_loop(0, n_chunks, chunk_body, 0)


def _hgrn(z, lb, norm_g, layer, n_batch, t_len, row0, state=None):
    use_ctx = state is not None
    rb0 = row0 // t_len
    cb = Z_HG // HG_W

    def zspec(k):
        return pl.BlockSpec((t_len, HG_W), lambda b: (rb0 + b, cb + k))

    in_specs = [zspec(0), zspec(1), zspec(2), zspec(3), zspec(4),
                pl.BlockSpec((1, 2, HG_W), lambda b: (layer, 0, 0)),
                pl.BlockSpec((1, 1, LANES), lambda b: (layer, 0, 0))]
    args = [z, z, z, z, z, lb, jnp.tile(norm_g, (1, LANES // HG_DV)).reshape(-1, 1, LANES)]
    out_specs = [pl.BlockSpec((t_len, HG_W), lambda b: (b, 0))]
    out_shape = [jax.ShapeDtypeStruct((n_batch * t_len, HG_W), BF16)]
    scratch = []
    if use_ctx:
        in_specs.append(pl.BlockSpec((1, 1, 2, HG_HEADS, HG_DK, HG_DV), lambda b: (b, layer, 0, 0, 0, 0)))
        args.append(state)
        scratch.append(pltpu.VMEM((t_len // HG_CHUNK, N_PAIRS, 2 * LANES, LANES), F32))
        scratch.append(pltpu.VMEM((2, 2, t_len, HG_W), F32))
    else:
        out_specs.append(pl.BlockSpec((1, 2, HG_HEADS, HG_DK, HG_DV), lambda b: (b, 0, 0, 0, 0)))
        out_shape.append(jax.ShapeDtypeStruct((n_batch, 2, HG_HEADS, HG_DK, HG_DV), F32))
    return pl.pallas_call(
        functools.partial(_hgrn_kernel, t_len=t_len, use_ctx=use_ctx),
        grid=(n_batch,),
        in_specs=in_specs,
        out_specs=out_specs,
        out_shape=out_shape,
        scratch_shapes=scratch,
        compiler_params=_cparams(("arbitrary",), 56),
        name="hgrn_ctx" if use_ctx else "hgrn",
    )(*args)


def _merge_kernel(a_ref, r_ref, h_ref, g0_ref, g1_ref, g2_ref, x_ref, mod_ref,
                  wa_ref, wr_ref, wh_ref, wo_ref, o_ref, wa_s, wr_s, wh_s, wo_s):
    @pl.when(pl.program_id(0) == 0)
    def _():
        wa_s[...] = wa_ref[0].astype(BF16)
        wr_s[...] = wr_ref[0].astype(BF16)
        wh_s[...] = wh_ref[0].astype(BF16)
        wo_s[...] = wo_ref[0].astype(BF16)

    def proj(x_r, w_s):
        return jnp.dot(x_r[...], w_s[...], preferred_element_type=F32)

    merged = (_sigmoid(g0_ref[...]) * proj(a_ref, wa_s)
              + _sigmoid(g1_ref[...]) * proj(r_ref, wr_s)
              + _sigmoid(g2_ref[...]) * proj(h_ref, wh_s))
    out = jnp.dot(merged.astype(BF16), wo_s[...], preferred_element_type=F32)
    o_ref[...] = x_ref[...] + mod_ref[0, 2:3, :] * out


def _merge_project(attn, rnn, hg, z, x, modseg, w_attn_o, w_rnn_o, w_hgrn_o, w_out, layer):
    n_tok = x.shape[0]
    tm = 512
    per_seg = SEG_ROWS // tm
    half = pl.BlockSpec((tm, ATTN_W), lambda i: (i, 0))
    full = pl.BlockSpec((tm, D_MODEL), lambda i: (i, 0))

    def gspec(k):
        return pl.BlockSpec((tm, D_MODEL), lambda i: (i, Z_GL // D_MODEL + k))

    def wspec(rows):
        return pl.BlockSpec((1, rows, D_MODEL), lambda i: (layer, 0, 0))

    return pl.pallas_call(
        _merge_kernel,
        grid=(n_tok // tm,),
        in_specs=[half, half, half, gspec(0), gspec(1), gspec(2), full,
                  pl.BlockSpec((1, N_MOD, D_MODEL), lambda i: (i // per_seg, 0, 0)),
                  wspec(ATTN_W), wspec(D_RNN), wspec(HG_W), wspec(D_MODEL)],
        out_specs=full,
        out_shape=jax.ShapeDtypeStruct((n_tok, D_MODEL), F32),
        scratch_shapes=[pltpu.VMEM((ATTN_W, D_MODEL), BF16), pltpu.VMEM((D_RNN, D_MODEL), BF16),
                        pltpu.VMEM((HG_W, D_MODEL), BF16), pltpu.VMEM((D_MODEL, D_MODEL), BF16)],
        compiler_params=_cparams(("arbitrary",), 56),
        name="merge_project",
    )(attn, rnn, hg, z, z, z, x, modseg, w_attn_o, w_rnn_o, w_hgrn_o, w_out)


def _router_kernel(x_ref, g_ref, mod_ref, wr_ref, br_ref, h_ref, slot_ref, wt_ref, meta_ref, run_ref):
    @pl.when(pl.program_id(0) == 0)
    def _():
        run_ref[...] = jnp.zeros_like(run_ref)

    h = _modulated_norm(x_ref[...], g_ref[0], mod_ref[0, 4:5, :], mod_ref[0, 3:4, :])
    h_ref[...] = h
    logits = jnp.dot(h, wr_ref[0], preferred_element_type=F32,
                     precision=lax.Precision.HIGHEST) + br_ref[0]
    tm = logits.shape[0]
    lane = lax.broadcasted_iota(jnp.int32, logits.shape, 1)
    ids, vals = [], []
    chosen = jnp.zeros(logits.shape, jnp.bool_)
    work = logits
    for _ in range(TOP_K):
        m = jnp.max(work, axis=-1, keepdims=True)
        idx = jnp.min(jnp.where(work == m, lane, LANES), axis=-1, keepdims=True)
        hit = lane == idx
        chosen = chosen | hit
        work = jnp.where(hit, -jnp.inf, work)
        ids.append(idx)
        vals.append(m)
    exps = [jnp.exp(v - vals[0]) for v in vals]
    den = exps[0] + exps[1] + exps[2] + exps[3]
    ind = jnp.where(chosen, 1.0, 0.0)
    r_i = lax.broadcasted_iota(jnp.int32, (tm, tm), 0)
    c_i = lax.broadcasted_iota(jnp.int32, (tm, tm), 1)
    count = jnp.sum(ind, axis=0, keepdims=True)
    padded = jnp.floor((count + (ROW_BLOCK - 1)) * (1.0 / ROW_BLOCK)) * ROW_BLOCK
    incl = jnp.broadcast_to(padded, (8, LANES))
    lane8 = lax.broadcasted_iota(jnp.int32, (8, LANES), 1)
    d = 1
    while d < LANES:
        incl = incl + jnp.where(lane8 >= d, pltpu.roll(incl, d, 1), 0.0)
        d *= 2
    local_start = incl[0:1] - padded
    place = _mm(jnp.where(c_i < r_i, 1.0, 0.0), ind) + local_start
    slot_out = jnp.zeros(logits.shape, jnp.int32)
    wt_out = jnp.zeros(logits.shape, F32)
    for k in range(TOP_K):
        slot = jnp.sum(jnp.where(lane == ids[k], place, 0.0), axis=-1, keepdims=True)
        slot_out = jnp.where(lane == k, slot.astype(jnp.int32), slot_out)
        wt_out = jnp.where(lane == k, exps[k] / den, wt_out)
    slot_ref[...] = slot_out
    wt_ref[...] = wt_out
    row8 = lax.broadcasted_iota(jnp.int32, (8, LANES), 0)
    table = jnp.where(row8 == 0, local_start, jnp.where(row8 == 1, padded, run_ref[0:1, :]))
    meta_ref[0] = jnp.where(row8 < 3, table, 0.0).astype(jnp.int32)
    run_ref[...] = run_ref[...] + padded


def _router(x, norm_g, modseg, w_router, b_router, layer):
    n_tok = x.shape[0]
    tm = TOKEN_TILE
    per_seg = SEG_ROWS // tm
    depth = w_router.shape[0]
    w_pad = jnp.zeros((depth, D_MODEL, LANES), F32).at[:, :, :N_EXPERTS].set(w_router)
    b_pad = jnp.full((depth, 1, LANES), NEG_BIG, F32).at[:, 0, :N_EXPERTS].set(b_router)
    full = pl.BlockSpec((tm, D_MODEL), lambda i: (i, 0))
    small = pl.BlockSpec((tm, LANES), lambda i: (i, 0))
    return pl.pallas_call(
        _router_kernel,
        grid=(n_tok // tm,),
        in_specs=[full,
                  pl.BlockSpec((1, 1, D_MODEL), lambda i: (layer, 0, 0)),
                  pl.BlockSpec((1, N_MOD, D_MODEL), lambda i: (i // per_seg, 0, 0)),
                  pl.BlockSpec((1, D_MODEL, LANES), lambda i: (layer, 0, 0)),
                  pl.BlockSpec((1, 1, LANES), lambda i: (layer, 0, 0))],
        out_specs=[full, small, small, pl.BlockSpec((1, 8, LANES), lambda i: (i, 0, 0))],
        out_shape=[jax.ShapeDtypeStruct((n_tok, D_MODEL), F32),
                   jax.ShapeDtypeStruct((n_tok, LANES), jnp.int32),
                   jax.ShapeDtypeStruct((n_tok, LANES), F32),
                   jax.ShapeDtypeStruct((n_tok // tm, 8, LANES), jnp.int32)],
        scratch_shapes=[pltpu.VMEM((8, LANES), F32)],
        compiler_params=_cparams(("arbitrary",), 40),
        name="router",
    )(x, norm_g.reshape(-1, 1, D_MODEL), modseg, w_pad, b_pad)


def _for_each_block(meta_ref, fn):
    def group(e, carry):
        local0 = meta_ref[e]
        sorted0 = meta_ref[2 * N_EXPERTS + e]
        size = meta_ref[N_EXPERTS + e]
        n_big = size // BIG_BLOCK

        def big(j, c):
            fn(pl.multiple_of(local0 + j * BIG_BLOCK, ROW_BLOCK),
               pl.multiple_of(sorted0 + j * BIG_BLOCK, ROW_BLOCK), BIG_BLOCK)
            return c

        def small(j, c):
            off = n_big * BIG_BLOCK + j * ROW_BLOCK
            fn(pl.multiple_of(local0 + off, ROW_BLOCK), pl.multiple_of(sorted0 + off, ROW_BLOCK), ROW_BLOCK)
            return c

        carry = lax.fori_loop(0, n_big, big, carry)
        return lax.fori_loop(0, (size - n_big * BIG_BLOCK) // ROW_BLOCK, small, carry)

    lax.fori_loop(0, N_EXPERTS, group, 0)


def _slot_matrix(slot_ref, values):
    col = lax.broadcasted_iota(jnp.int32, (TOKEN_TILE, LOCAL_ROWS), 1)
    out = jnp.zeros((TOKEN_TILE, LOCAL_ROWS), F32)
    for k in range(TOP_K):
        out = jnp.where(col == slot_ref[:, k:k + 1], values[k], out)
    return out.astype(BF16)


def _dispatch_kernel(meta_ref, tail_ref, slot_ref, h_ref, xs_ref, local_ref, zero_ref, sem):
    @pl.when(pl.program_id(0) == 0)
    def _():
        zero_ref[...] = jnp.zeros_like(zero_ref)

        def tails(fn):
            def group(e, carry):
                def block(j, c):
                    fn(pl.multiple_of(tail_ref[e] + j * ROW_BLOCK, ROW_BLOCK))
                    return c
                return lax.fori_loop(0, tail_ref[N_EXPERTS + e], block, carry)
            lax.fori_loop(0, N_EXPERTS, group, 0)

        def zero_copy(row):
            return pltpu.make_async_copy(zero_ref.at[pl.ds(0, ROW_BLOCK)],
                                         xs_ref.at[pl.ds(row, ROW_BLOCK)], sem)

        tails(lambda row: zero_copy(row).start())
        tails(lambda row: zero_copy(row).wait())

        def unused(fn):
            def tile(j, c):
                fn(pl.multiple_of(tail_ref[2 * N_EXPERTS] + j * EXPERT_TILE, EXPERT_TILE))
                return c
            lax.fori_loop(0, tail_ref[2 * N_EXPERTS + 1], tile, 0)

        def zero_tile(row):
            return pltpu.make_async_copy(zero_ref, xs_ref.at[pl.ds(row, EXPERT_TILE)], sem)

        unused(lambda row: zero_tile(row).start())
        unused(lambda row: zero_tile(row).wait())

    onehot = _slot_matrix(slot_ref, [1.0] * TOP_K)
    local_ref[...] = _mm_tn(onehot, h_ref[...])

    def block_copy(local_row, sorted_row, rows):
        return pltpu.make_async_copy(local_ref.at[pl.ds(local_row, rows)],
                                     xs_ref.at[pl.ds(sorted_row, rows)], sem)

    _for_each_block(meta_ref, lambda a, b, n: block_copy(a, b, n).start())
    _for_each_block(meta_ref, lambda a, b, n: block_copy(a, b, n).wait())


def _dispatch(h, slots, meta, tails, n_rows):
    n_tok = h.shape[0]
    return pl.pallas_call(
        _dispatch_kernel,
        grid=(n_tok // TOKEN_TILE,),
        in_specs=[pl.BlockSpec((LANES,), lambda i: (i,), memory_space=pltpu.SMEM),
                  pl.BlockSpec((LANES,), lambda i: (0,), memory_space=pltpu.SMEM),
                  pl.BlockSpec((TOKEN_TILE, LANES), lambda i: (i, 0)),
                  pl.BlockSpec((TOKEN_TILE, D_MODEL), lambda i: (i, 0))],
        out_specs=pl.BlockSpec(memory_space=pl.ANY),
        out_shape=jax.ShapeDtypeStruct((n_rows, D_MODEL), F32),
        scratch_shapes=[pltpu.VMEM((LOCAL_ROWS, D_MODEL), F32), pltpu.VMEM((EXPERT_TILE, D_MODEL), F32),
                        pltpu.SemaphoreType.DMA(())],
        compiler_params=_cparams(("arbitrary",), 40),
        name="dispatch",
    )(meta, tails, slots, h)


def _combine_kernel(meta_ref, slot_ref, wt_ref, ys_ref, x_ref, mod_ref, o_ref, local_ref, sem):
    local_ref[TOKEN_TILE * TOP_K:, :] = jnp.zeros((LOCAL_ROWS - TOKEN_TILE * TOP_K, D_MODEL), F32)

    def block_copy(local_row, sorted_row, rows):
        return pltpu.make_async_copy(ys_ref.at[pl.ds(sorted_row, rows)],
                                     local_ref.at[pl.ds(local_row, rows)], sem)

    _for_each_block(meta_ref, lambda a, b, n: block_copy(a, b, n).start())
    _for_each_block(meta_ref, lambda a, b, n: block_copy(a, b, n).wait())
    weights = _slot_matrix(slot_ref, [wt_ref[:, k:k + 1] for k in range(TOP_K)])
    acc = jnp.dot(weights, local_ref[...].astype(BF16), preferred_element_type=F32)
    o_ref[...] = x_ref[...] + mod_ref[0, 5:6, :] * acc


def _combine(ys, slots, meta, wts, x, modseg):
    n_tok = x.shape[0]
    per_seg = SEG_ROWS // TOKEN_TILE
    full = pl.BlockSpec((TOKEN_TILE, D_MODEL), lambda i: (i, 0))
    small = pl.BlockSpec((TOKEN_TILE, LANES), lambda i: (i, 0))
    return pl.pallas_call(
        _combine_kernel,
        grid=(n_tok // TOKEN_TILE,),
        in_specs=[pl.BlockSpec((LANES,), lambda i: (i,), memory_space=pltpu.SMEM),
                  small, small,
                  pl.BlockSpec(memory_space=pl.ANY),
                  full,
                  pl.BlockSpec((1, N_MOD, D_MODEL), lambda i: (i // per_seg, 0, 0))],
        out_specs=full,
        out_shape=jax.ShapeDtypeStruct((n_tok, D_MODEL), F32),
        scratch_shapes=[pltpu.VMEM((LOCAL_ROWS, D_MODEL), F32), pltpu.SemaphoreType.DMA(())],
        compiler_params=_cparams(("arbitrary",), 48),
        name="combine",
    )(meta, slots, wts, ys, x, modseg)


def _expert_kernel(te_ref, na_ref, x_ref, wgu_ref, bgu_ref, wd_ref, bd_ref, o_ref, wgu_s, wd_s):
    i = pl.program_id(0)
    active = i < na_ref[0]
    fresh = jnp.logical_or(i == 0, te_ref[i] != te_ref[jnp.maximum(i - 1, 0)])

    @pl.when(jnp.logical_and(active, fresh))
    def _():
        wgu_s[...] = wgu_ref[0, 0].astype(BF16)
        wd_s[...] = wd_ref[0, 0].astype(BF16)

    @pl.when(active)
    def _():
        gu = jnp.dot(x_ref[...].astype(BF16), wgu_s[...], preferred_element_type=F32) + bgu_ref[0, 0]
        gate = jnp.minimum(gu[:, :D_FF], SWIGLU_LIMIT)
        up = jnp.clip(gu[:, D_FF:], -SWIGLU_LIMIT, SWIGLU_LIMIT)
        act = gate * _sigmoid(SWIGLU_ALPHA * gate) * (up + 1.0)
        o_ref[...] = jnp.dot(act.astype(BF16), wd_s[...], preferred_element_type=F32) + bd_ref[0, 0]

    @pl.when(jnp.logical_not(active))
    def _():
        o_ref[...] = jnp.zeros_like(o_ref)


def _experts(xs, tile_expert, n_active, w_gu, b_gu, w_down, b_down, layer):
    n_rows = xs.shape[0]
    depth = w_gu.shape[0]
    tm = EXPERT_TILE
    grid_spec = pltpu.PrefetchScalarGridSpec(
        num_scalar_prefetch=2,
        grid=(n_rows // tm,),
        in_specs=[pl.BlockSpec((tm, D_MODEL), lambda i, te, na: (jnp.where(i < na[0], i, 0), 0)),
                  pl.BlockSpec((1, 1, D_MODEL, 2 * D_FF), lambda i, te, na: (layer, te[i], 0, 0)),
                  pl.BlockSpec((1, 1, 1, 2 * D_FF), lambda i, te, na: (layer, te[i], 0, 0)),
                  pl.BlockSpec((1, 1, D_FF, D_MODEL), lambda i, te, na: (layer, te[i], 0, 0)),
                  pl.BlockSpec((1, 1, 1, D_MODEL), lambda i, te, na: (layer, te[i], 0, 0))],
        out_specs=pl.BlockSpec((tm, D_MODEL), lambda i, te, na: (i, 0)),
        scratch_shapes=[pltpu.VMEM((D_MODEL, 2 * D_FF), BF16), pltpu.VMEM((D_FF, D_MODEL), BF16)],
    )
    return pl.pallas_call(
        _expert_kernel,
        grid_spec=grid_spec,
        out_shape=jax.ShapeDtypeStruct((n_rows, D_MODEL), F32),
        compiler_params=_cparams(("arbitrary",), 56),
        name="experts",
    )(tile_expert, n_active, xs, w_gu, b_gu.reshape(depth, N_EXPERTS, 1, 2 * D_FF),
      w_down, b_down.reshape(depth, N_EXPERTS, 1, D_MODEL))


def _moe(x, norm_g, modseg, w_router, b_router, w_gu, b_gu, w_down, b_down, layer):
    n_tok = x.shape[0]
    h, slots, wts, table = _router(x, norm_g, modseg, w_router, b_router, layer)
    tm = EXPERT_TILE
    n_tiles = n_tok // TOKEN_TILE
    n_rows = n_tok * TOP_K + n_tiles * N_EXPERTS * ROW_BLOCK + N_EXPERTS * tm
    n_rows = -(-n_rows // tm) * tm
    local_start = table[:, 0, :N_EXPERTS]
    padded = table[:, 1, :N_EXPERTS]
    earlier = table[:, 2, :N_EXPERTS]
    used = earlier[-1] + padded[-1]
    size = ((used + tm - 1) // tm) * tm
    ends = jnp.cumsum(size)
    starts = ends - size
    sorted_start = starts[None, :] + earlier
    meta = jnp.concatenate([local_start, padded, sorted_start, jnp.zeros_like(padded)], axis=1).reshape(-1)
    tails = jnp.concatenate([starts + used, (size - used) // ROW_BLOCK,
                             ends[-1:], (n_rows - ends[-1:]) // tm,
                             jnp.zeros((LANES - 2 * N_EXPERTS - 2,), jnp.int32)])
    tile_row0 = jnp.arange(n_rows // tm, dtype=jnp.int32) * tm
    tile_expert = jnp.minimum(jnp.sum((ends[None, :] <= tile_row0[:, None]).astype(jnp.int32), axis=1),
                              N_EXPERTS - 1)
    n_active = (ends[-1:] // tm).astype(jnp.int32)
    xs = _dispatch(h, slots, meta, tails, n_rows)
    ys = _experts(xs, tile_expert, n_active, w_gu, b_gu, w_down, b_down, layer)
    return _combine(ys, slots, meta, wts, x, modseg)


def _rope_tables(n_tok):
    rows = n_tok // GRID_W
    row = jnp.repeat(jnp.arange(rows, dtype=F32), GRID_W)
    col = jnp.tile(jnp.arange(GRID_W, dtype=F32), rows)
    quarter = HEAD_DIM // 4
    inv_freq = ROPE_BASE ** (-jnp.arange(quarter, dtype=F32) / quarter)
    ang_r = row[:, None] * inv_freq
    ang_c = col[:, None] * inv_freq
    ang = jnp.concatenate([ang_r, ang_r, ang_c, ang_c], axis=-1)
    cos, sin = jnp.cos(ang), jnp.sin(ang)
    first = (jnp.arange(HEAD_DIM) % (2 * quarter)) < quarter
    sin_a = jnp.where(first, -sin, 0.0)
    sin_b = jnp.where(first, 0.0, sin)
    return tuple(jnp.tile(t, (1, LANES // HEAD_DIM)) for t in (cos, sin_a, sin_b))


def _block_diag(w):
    eye = jnp.eye(RG_BLOCKS, dtype=w.dtype)
    return jnp.einsum("nde,nm->ndme", w, eye).reshape(D_RNN, D_RNN)


def kernel(x_prompt, x_sample, c, cache_k, cache_v, state_rglru, state_hgrn, c_ctx, w_mod, b_mod, norm1_g, norm2_g, w_in, q_norm_g, k_norm_g, conv_w, conv_b, rg_wa, rg_ba, rg_wx, rg_bx, rg_lambda, hgrn_lb_logits, hgrn_norm_g, w_attn_o, w_rnn_o, w_hgrn_o, w_out, w_router, b_router, w_gu, b_gu, w_down, b_down):
    depth = w_mod.shape[0]
    n_p, t_p = x_prompt.shape[0], x_prompt.shape[1]
    n_s, t_s = x_sample.shape[0], x_sample.shape[1]
    tok_p = n_p * t_p
    tok_s = n_s * t_s
    assert tok_p % SEG_ROWS == 0 and t_s == SEG_ROWS and SEG_ROWS % t_p == 0

    x = jnp.concatenate([x_prompt.reshape(tok_p, D_MODEL), x_sample.reshape(tok_s, D_MODEL)], axis=0)

    lb_all = jnp.cumsum(jax.nn.softmax(hgrn_lb_logits.astype(F32), axis=0), axis=0)
    lb_all = lb_all - lb_all[:1]
    rope = _rope_tables(t_s)

    cond8 = jnp.zeros((8, D_MODEL), F32).at[0].set(c_ctx).at[1:1 + n_s].set(c)
    mod = _modulation(cond8, w_mod, b_mod)
    seg_ids = jnp.asarray([0] * (tok_p // SEG_ROWS) + [1 + b for b in range(n_s)], jnp.int32)
    modsegs = mod[:, seg_ids].reshape(depth, seg_ids.shape[0], N_MOD, D_MODEL)

    w_gate = jnp.stack([jnp.concatenate([_block_diag(rg_wa[l, 0]), _block_diag(rg_wx[l, 0]),
                                         _block_diag(rg_wa[l, 1]), _block_diag(rg_wx[l, 1])], axis=1)
                        for l in range(depth)])
    b_gate = jnp.stack([jnp.concatenate([rg_ba[l, 0], rg_bx[l, 0], rg_ba[l, 1], rg_bx[l, 1]])[None]
                        for l in range(depth)])

    new_k, new_v, new_hr, new_s = [], [], [], []
    for l in range(depth):
        modseg = modsegs[l]
        z = _input_projection(x, norm1_g, modseg, w_in, l)
        attn_p, k_l, v_l = _attention(z, q_norm_g, k_norm_g, l, n_p, t_p, 0)
        (attn_s,) = _attention(z, q_norm_g, k_norm_g, l, n_s, t_s, tok_p, rope=rope,
                               cache=(cache_k, cache_v))
        rnn_p, hr_l = _rglru(z, conv_w, conv_b, w_gate, b_gate, rg_lambda, l, n_p, t_p, 0)
        (rnn_s,) = _rglru(z, conv_w, conv_b, w_gate, b_gate, rg_lambda, l, n_s, t_s, tok_p,
                          state=state_rglru)
        hg_p, s_l = _hgrn(z, lb_all, hgrn_norm_g, l, n_p, t_p, 0)
        (hg_s,) = _hgrn(z, lb_all, hgrn_norm_g, l, n_s, t_s, tok_p, state=state_hgrn)
        attn = jnp.concatenate([attn_p, attn_s], axis=0)
        rnn = jnp.concatenate([rnn_p, rnn_s], axis=0)
        hg = jnp.concatenate([hg_p, hg_s], axis=0)
        x = _merge_project(attn, rnn, hg, z, x, modseg, w_attn_o, w_rnn_o, w_hgrn_o, w_out, l)
        x = _moe(x, norm2_g, modseg, w_router, b_router, w_gu, b_gu, w_down, b_down, l)
        new_k.append(k_l.reshape(n_p, t_p, N_KV_HEADS, HEAD_DIM))
        new_v.append(v_l.reshape(n_p, t_p, N_KV_HEADS, HEAD_DIM))
        new_hr.append(hr_l)
        new_s.append(s_l)

    y_prompt = x[:tok_p].reshape(n_p, t_p, D_MODEL)
    y_sample = x[tok_p:].reshape(n_s, t_s, D_MODEL)
    return (y_prompt, y_sample, jnp.stack(new_k, axis=1), jnp.stack(new_v, axis=1),
            jnp.stack(new_hr, axis=1), jnp.stack(new_s, axis=1))
```

```python
import functools

import jax
import jax.numpy as jnp
from jax import lax
from jax.experimental import pallas as pl
from jax.experimental.pallas import tpu as pltpu

F32 = jnp.float32
BF16 = jnp.bfloat16

D_MODEL = 1024
GRID_W = 64
HEAD_DIM = 64
N_HEADS = 8
N_KV_HEADS = 2
KV_GROUP = N_HEADS // N_KV_HEADS
ATTN_W = N_HEADS * HEAD_DIM
KV_W = N_KV_HEADS * HEAD_DIM
ROPE_BASE = 10000.0
D_RNN = 512
RG_BLOCKS = 8
RG_BW = D_RNN // RG_BLOCKS
RG_C = 8.0
HG_HEADS = 8
HG_DK = 64
HG_DV = 64
HG_W = HG_HEADS * HG_DK
N_PAIRS = HG_W // 128
N_EXPERTS = 32
TOP_K = 4
D_FF = D_MODEL
SWIGLU_LIMIT = 7.0
SWIGLU_ALPHA = 1.702
N_MOD = 6
EPS = 1e-6
IN_W = ATTN_W + 2 * KV_W + 2 * D_RNN + 5 * HG_W + 3 * D_MODEL

LANES = 128
SEG_ROWS = 1024
COL_TILE = 256
HG_CHUNK = 256
Q_TILE = 256
EXPERT_TILE = 384
BIG_BLOCK = 64
TOKEN_TILE = 512
ROW_BLOCK = 8
LOCAL_ROWS = TOKEN_TILE * TOP_K + N_EXPERTS * ROW_BLOCK
NEG_BIG = -1e30
LOG2_E = 1.4426950408889634

_SRC_TILES = tuple(range(17, 29)) + tuple(range(7, 17)) + tuple(range(3, 7)) + (0, 1, 2)
Z_GL = 0
Z_HG = 3 * D_MODEL
Z_RG = Z_HG + 5 * HG_W
Z_Q = Z_RG + 2 * D_RNN
Z_K = Z_Q + ATTN_W
Z_V = Z_K + KV_W
Z_W = -(-IN_W // (2 * COL_TILE)) * (2 * COL_TILE)


def _cparams(sem, vmem_mb):
    return pltpu.CompilerParams(dimension_semantics=sem, vmem_limit_bytes=vmem_mb * 1024 * 1024)


def _mm(a, b):
    return jnp.dot(a.astype(BF16), b.astype(BF16), preferred_element_type=F32)


def _mm_nt(a, b):
    return lax.dot_general(a.astype(BF16), b.astype(BF16), (((1,), (1,)), ((), ())),
                           preferred_element_type=F32)


def _mm_tn(a, b):
    return lax.dot_general(a.astype(BF16), b.astype(BF16), (((0,), (0,)), ((), ())),
                           preferred_element_type=F32)


def _sigmoid(x):
    return 1.0 / (1.0 + jnp.exp(-x))


def _softplus(x):
    return jnp.maximum(x, 0.0) + jnp.log1p(jnp.exp(-jnp.abs(x)))


def _modulated_norm(x, g, scale, shift):
    y = x * lax.rsqrt(jnp.mean(x * x, axis=-1, keepdims=True) + EPS)
    return (y * g) * (1.0 + scale) + shift


def _mod_kernel(c_ref, w_ref, b_ref, o_ref):
    c = c_ref[...]
    s = c * _sigmoid(c)
    o_ref[0] = _mm(s, w_ref[0]) + b_ref[0]


def _modulation(cond8, w_mod, b_mod):
    depth = w_mod.shape[0]
    tn = 1024
    return pl.pallas_call(
        _mod_kernel,
        grid=(depth, N_MOD * D_MODEL // tn),
        in_specs=[pl.BlockSpec((8, D_MODEL), lambda l, j: (0, 0)),
                  pl.BlockSpec((1, D_MODEL, tn), lambda l, j: (l, 0, j)),
                  pl.BlockSpec((1, 1, tn), lambda l, j: (l, 0, j))],
        out_specs=pl.BlockSpec((1, 8, tn), lambda l, j: (l, 0, j)),
        out_shape=jax.ShapeDtypeStruct((depth, 8, N_MOD * D_MODEL), F32),
        compiler_params=_cparams(("arbitrary", "arbitrary"), 32),
        name="modulation",
    )(cond8, w_mod, b_mod.reshape(depth, 1, N_MOD * D_MODEL))


def _inproj_kernel(src_ref, x_ref, g_ref, mod_ref, wa_ref, wb_ref, o_ref, h_ref, w_s):
    j = pl.program_id(1)

    @pl.when(j == 0)
    def _():
        h = _modulated_norm(x_ref[...], g_ref[0], mod_ref[0, 1:2, :], mod_ref[0, 0:1, :])
        h_ref[...] = h.astype(BF16)

    @pl.when(pl.program_id(0) == 0)
    def _():
        w_s[j, :, 0:COL_TILE] = wa_ref[0].astype(BF16)
        w_s[j, :, COL_TILE:2 * COL_TILE] = wb_ref[0].astype(BF16)

    o_ref[...] = jnp.dot(h_ref[...], w_s[j], preferred_element_type=F32)


def _input_projection(x, norm_g, modseg, w_in, layer):
    n_tok = x.shape[0]
    tm = SEG_ROWS
    n_col = Z_W // (2 * COL_TILE)
    src = jnp.asarray(_SRC_TILES + (_SRC_TILES[-1],) * (Z_W // COL_TILE - len(_SRC_TILES)), jnp.int32)

    def wspec(k):
        return pl.BlockSpec((1, D_MODEL, COL_TILE),
                            lambda i, j, s: (layer, 0, s[2 * jnp.where(i == 0, j, n_col - 1) + k]))

    grid_spec = pltpu.PrefetchScalarGridSpec(
        num_scalar_prefetch=1,
        grid=(n_tok // tm, n_col),
        in_specs=[pl.BlockSpec((tm, D_MODEL), lambda i, j, s: (i, 0)),
                  pl.BlockSpec((1, 1, D_MODEL), lambda i, j, s: (layer, 0, 0)),
                  pl.BlockSpec((1, N_MOD, D_MODEL), lambda i, j, s: (i, 0, 0)),
                  wspec(0), wspec(1)],
        out_specs=pl.BlockSpec((tm, 2 * COL_TILE), lambda i, j, s: (i, j)),
        scratch_shapes=[pltpu.VMEM((tm, D_MODEL), BF16),
                        pltpu.VMEM((n_col, D_MODEL, 2 * COL_TILE), BF16)],
    )
    return pl.pallas_call(
        _inproj_kernel,
        grid_spec=grid_spec,
        out_shape=jax.ShapeDtypeStruct((n_tok, Z_W), F32),
        compiler_params=_cparams(("arbitrary", "arbitrary"), 40),
        name="input_projection",
    )(src, x, norm_g.reshape(-1, 1, D_MODEL), modseg, w_in, w_in)


def _head_rms(x, g):
    lane = lax.broadcasted_iota(jnp.int32, x.shape, 1)
    lo = lane < HEAD_DIM
    xx = x * x
    s_lo = jnp.sum(jnp.where(lo, xx, 0.0), axis=-1, keepdims=True)
    s_hi = jnp.sum(jnp.where(lo, 0.0, xx), axis=-1, keepdims=True)
    inv = jnp.where(lo, lax.rsqrt(s_lo * (1.0 / HEAD_DIM) + EPS),
                    lax.rsqrt(s_hi * (1.0 / HEAD_DIM) + EPS))
    return x * inv * g


def _rope(x, cos, sin_a, sin_b):
    q = HEAD_DIM // 4
    return x * cos + pltpu.roll(x, LANES - q, 1) * sin_a + pltpu.roll(x, q, 1) * sin_b


def _attn_kernel(*refs, t_len, use_ctx):
    if use_ctx:
        (q_ref, k_ref, v_ref, qg_ref, kg_ref, cos_q_ref, sa_q_ref, sb_q_ref,
         cos_k_ref, sa_k_ref, sb_k_ref, ck_ref, cv_ref, o_ref, kall_ref, vall_ref) = refs
    else:
        (q_ref, k_ref, v_ref, qg_ref, kg_ref, o_ref, ko_ref, vo_ref, kall_ref, vall_ref) = refs

    @pl.when(pl.program_id(1) == 0)
    def _():
        k = _head_rms(k_ref[...], kg_ref[...])
        if use_ctx:
            k = _rope(k, cos_k_ref[...], sa_k_ref[...], sb_k_ref[...])
            kall_ref[0:t_len, :] = k.astype(BF16)
            vall_ref[0:t_len, :] = v_ref[...].astype(BF16)
            kall_ref[t_len:, :] = ck_ref[0, 0].astype(BF16)
            vall_ref[t_len:, :] = cv_ref[0, 0].astype(BF16)
        else:
            ko_ref[0] = k
            vo_ref[0] = v_ref[...]
            kall_ref[...] = k.astype(BF16)
            vall_ref[...] = v_ref[...].astype(BF16)

    heads = []
    for c in range(ATTN_W // LANES):
        qc = _head_rms(q_ref[:, c * LANES:(c + 1) * LANES], qg_ref[...])
        if use_ctx:
            qc = _rope(qc, cos_q_ref[...], sa_q_ref[...], sb_q_ref[...])
        qc = (qc * (HEAD_DIM ** -0.5)).astype(BF16)
        heads.append(qc[:, :HEAD_DIM])
        heads.append(qc[:, HEAD_DIM:])
    tq = q_ref.shape[0]
    for g in range(N_KV_HEADS):
        qs = jnp.concatenate(heads[g * KV_GROUP:(g + 1) * KV_GROUP], axis=0)
        kh = kall_ref[:, g * HEAD_DIM:(g + 1) * HEAD_DIM]
        vh = vall_ref[:, g * HEAD_DIM:(g + 1) * HEAD_DIM]
        s = _mm_nt(qs, kh)
        m = jnp.max(s, axis=-1, keepdims=True)
        p = jnp.exp(s - m)
        den = jnp.sum(p, axis=-1, keepdims=True)
        o = _mm(p, vh) / den
        for j in range(KV_GROUP):
            hh = g * KV_GROUP + j
            o_ref[:, hh * HEAD_DIM:(hh + 1) * HEAD_DIM] = o[j * tq:(j + 1) * tq].astype(BF16)


def _attention(z, q_g, k_g, layer, n_batch, t_len, row0, rope=None, cache=None):
    use_ctx = cache is not None
    nq = t_len // Q_TILE
    rb0 = row0 // t_len
    qb0 = row0 // Q_TILE
    qg = jnp.tile(q_g[layer], 2).reshape(1, LANES)
    kg = jnp.tile(k_g[layer], 2).reshape(1, LANES)
    n_out_tok = n_batch * t_len
    vec = pl.BlockSpec((1, LANES), lambda b, i: (0, 0))
    in_specs = [pl.BlockSpec((Q_TILE, ATTN_W), lambda b, i: (qb0 + b * nq + i, Z_Q // ATTN_W)),
                pl.BlockSpec((t_len, KV_W), lambda b, i: (rb0 + b, Z_K // KV_W)),
                pl.BlockSpec((t_len, KV_W), lambda b, i: (rb0 + b, Z_V // KV_W)),
                vec, vec]
    args = [z, z, z, qg, kg]
    out_specs = [pl.BlockSpec((Q_TILE, ATTN_W), lambda b, i: (b * nq + i, 0))]
    out_shape = [jax.ShapeDtypeStruct((n_out_tok, ATTN_W), BF16)]
    t_keys = t_len
    if use_ctx:
        cos, sin_a, sin_b = rope
        cache_k, cache_v = cache
        past = cache_k.shape[2]
        t_keys = t_len + past
        tq_spec = pl.BlockSpec((Q_TILE, LANES), lambda b, i: (i, 0))
        tk_spec = pl.BlockSpec((t_len, LANES), lambda b, i: (0, 0))
        c_spec = pl.BlockSpec((1, 1, past, KV_W), lambda b, i: (b, layer, 0, 0))
        in_specs += [tq_spec, tq_spec, tq_spec, tk_spec, tk_spec, tk_spec, c_spec, c_spec]
        args += [cos, sin_a, sin_b, cos, sin_a, sin_b,
                 cache_k.reshape(cache_k.shape[0], cache_k.shape[1], past, KV_W),
                 cache_v.reshape(cache_v.shape[0], cache_v.shape[1], past, KV_W)]
    else:
        kv_spec = pl.BlockSpec((1, t_len, KV_W), lambda b, i: (b, 0, 0))
        out_specs += [kv_spec, kv_spec]
        out_shape += [jax.ShapeDtypeStruct((n_batch, t_len, KV_W), F32)] * 2
    return pl.pallas_call(
        functools.partial(_attn_kernel, t_len=t_len, use_ctx=use_ctx),
        grid=(n_batch, nq),
        in_specs=in_specs,
        out_specs=out_specs,
        out_shape=out_shape,
        scratch_shapes=[pltpu.VMEM((t_keys, KV_W), BF16), pltpu.VMEM((t_keys, KV_W), BF16)],
        compiler_params=_cparams(("arbitrary", "arbitrary"), 48),
        name="attention_ctx" if use_ctx else "attention",
    )(*args)


def _shift_rows(x, d):
    n = x.shape[0]
    row = lax.broadcasted_iota(jnp.int32, x.shape, 0)
    y = pltpu.roll(x, d % n, 0)
    if d > 0:
        return jnp.where(row >= d, y, 0.0)
    return jnp.where(row < n + d, y, 0.0)


def _linear_scan(a, u, reverse):
    n = a.shape[0]
    row = lax.broadcasted_iota(jnp.int32, a.shape, 0)
    d = 1
    while d < n:
        if reverse:
            keep = row < n - d
            a_s = jnp.where(keep, pltpu.roll(a, n - d, 0), 1.0)
            u_s = jnp.where(keep, pltpu.roll(u, n - d, 0), 0.0)
        else:
            keep = row >= d
            a_s = jnp.where(keep, pltpu.roll(a, d, 0), 1.0)
            u_s = jnp.where(keep, pltpu.roll(u, d, 0), 0.0)
        u = a * u_s + u
        a = a * a_s
        d *= 2
    return a, u


def _rglru_kernel(*refs, use_ctx):
    if use_ctx:
        (x_ref, y_ref, cw_ref, cb_ref, wg_ref, bg_ref, lam_ref, h0_ref, o_ref) = refs
    else:
        (x_ref, y_ref, cw_ref, cb_ref, wg_ref, bg_ref, lam_ref, o_ref, last_ref) = refs
    x = x_ref[...]
    cw = cw_ref[0]
    xr = cb_ref[0] + _shift_rows(x, 1) * cw[0:1] + x * cw[1:2] \
        + _shift_rows(x, -1) * cw[2:3] + _shift_rows(x, -2) * cw[3:4]
    gates = _sigmoid(_mm(xr, wg_ref[0]) + bg_ref[0])
    n = x.shape[0]
    total = None
    lasts = []
    for d in range(2):
        r = gates[:, (2 * d) * D_RNN:(2 * d + 1) * D_RNN]
        i = gates[:, (2 * d + 1) * D_RNN:(2 * d + 2) * D_RNN]
        log_a = (-RG_C * _softplus(-lam_ref[0, d:d + 1, :])) * r
        a = jnp.exp(log_a)
        u = jnp.sqrt(1.0 - a * a) * (i * xr)
        a_cum, h = _linear_scan(a, u, reverse=(d == 1))
        if use_ctx:
            h = h + a_cum * h0_ref[0, 0, d:d + 1, :]
        else:
            lasts.append(h[0:1] if d == 1 else h[n - 1:n])
        total = h if total is None else total + h
    o_ref[...] = (jax.nn.gelu(y_ref[...]) * total).astype(BF16)
    if not use_ctx:
        last_ref[0] = jnp.concatenate(lasts, axis=0)


def _rglru(z, conv_w, conv_b, w_gate, b_gate, lam, layer, n_batch, t_len, row0, state=None):
    use_ctx = state is not None
    rb0 = row0 // t_len
    in_specs = [pl.BlockSpec((t_len, D_RNN), lambda b: (rb0 + b, Z_RG // D_RNN)),
                pl.BlockSpec((t_len, D_RNN), lambda b: (rb0 + b, Z_RG // D_RNN + 1)),
                pl.BlockSpec((1, 4, D_RNN), lambda b: (layer, 0, 0)),
                pl.BlockSpec((1, 1, D_RNN), lambda b: (layer, 0, 0)),
                pl.BlockSpec((1, D_RNN, 4 * D_RNN), lambda b: (layer, 0, 0)),
                pl.BlockSpec((1, 1, 4 * D_RNN), lambda b: (layer, 0, 0)),
                pl.BlockSpec((1, 2, D_RNN), lambda b: (layer, 0, 0))]
    args = [z, z, conv_w, conv_b.reshape(-1, 1, D_RNN), w_gate, b_gate, lam]
    out_specs = [pl.BlockSpec((t_len, D_RNN), lambda b: (b, 0))]
    out_shape = [jax.ShapeDtypeStruct((n_batch * t_len, D_RNN), BF16)]
    if use_ctx:
        in_specs.append(pl.BlockSpec((1, 1, 2, D_RNN), lambda b: (b, layer, 0, 0)))
        args.append(state)
    else:
        out_specs.append(pl.BlockSpec((1, 2, D_RNN), lambda b: (b, 0, 0)))
        out_shape.append(jax.ShapeDtypeStruct((n_batch, 2, D_RNN), F32))
    return pl.pallas_call(
        functools.partial(_rglru_kernel, use_ctx=use_ctx),
        grid=(n_batch,),
        in_specs=in_specs,
        out_specs=out_specs,
        out_shape=out_shape,
        compiler_params=_cparams(("arbitrary",), 56),
        name="rglru_ctx" if use_ctx else "rglru",
    )(*args)


def _cumsum_rows(g, reverse):
    n, w = g.shape
    m = n // ROW_BLOCK

    def scan(x, size, pos):
        d = 1
        while d < size:
            if reverse:
                x = x + jnp.where(pos < size - d, pltpu.roll(x, x.shape[0] - d, 0), 0.0)
            else:
                x = x + jnp.where(pos >= d, pltpu.roll(x, d, 0), 0.0)
            d *= 2
        return x

    g = scan(g, ROW_BLOCK, lax.broadcasted_iota(jnp.int32, g.shape, 0) & (ROW_BLOCK - 1))
    g3 = g.reshape(m, ROW_BLOCK, w)
    total = g3[:, 0, :] if reverse else g3[:, ROW_BLOCK - 1, :]
    before = scan(total, m, lax.broadcasted_iota(jnp.int32, total.shape, 0)) - total
    return (g3 + before[:, None, :]).reshape(n, w)


def _block_row_fn(gc, reverse):
    n, w = gc.shape
    g3 = gc.reshape(n // ROW_BLOCK, ROW_BLOCK, w)
    edge_row = 0 if reverse else ROW_BLOCK - 1
    edge = jnp.broadcast_to(g3[:, edge_row:edge_row + 1, :], g3.shape)
    sub = lax.broadcasted_iota(jnp.int32, g3.shape, 1)

    def at(blk):
        r = blk // 2 if reverse else blk // 2 - 1
        if blk > ROW_BLOCK:
            groups = blk // ROW_BLOCK
            e4 = edge.reshape(n // blk, groups, ROW_BLOCK, w)
            pick = r // ROW_BLOCK
            return jnp.broadcast_to(e4[:, pick:pick + 1], e4.shape).reshape(n, w)
        out = jnp.broadcast_to(g3[:, r:r + 1, :], g3.shape)
        for b0 in range(blk, ROW_BLOCK, blk):
            out = jnp.where(sub >= b0, jnp.broadcast_to(g3[:, b0 + r:b0 + r + 1, :], g3.shape), out)
        return out.reshape(n, w)

    return at


def _first_head(shape):
    lane = lax.broadcasted_iota(jnp.int32, shape, len(shape) - 1)
    return (lane & (LANES - 1)) < HG_DK


def _pair(x, p):
    return x[:, p * LANES:(p + 1) * LANES]


def _hgrn_gates(f_logit, lb):
    y = jnp.exp(-jnp.abs(f_logit))
    log_sig = jnp.minimum(f_logit, 0.0) - jnp.log(1.0 + y)
    a = jnp.log(lb)
    b = jnp.log1p(-lb) + log_sig
    log_f = jnp.maximum(a, b) + jnp.log(1.0 + jnp.exp(-jnp.abs(a - b)))
    k = (1.0 - lb) * (jnp.where(f_logit >= 0.0, y, 1.0) / (1.0 + y))
    return log_f * LOG2_E, k


def _hgrn_chunk_state(gc, k, v, reverse):
    n = gc.shape[0]
    g_tot = gc[0:1] if reverse else gc[n - 1:n]
    k_end = (k * jnp.exp2(jnp.minimum(g_tot - gc, 0.0))).astype(BF16)
    vb = v.astype(BF16)
    same_head = (_first_head((LANES, LANES))
                 == (lax.broadcasted_iota(jnp.int32, (LANES, LANES), 0) < HG_DK))
    ds = [jnp.where(same_head, _mm_tn(_pair(k_end, p), _pair(vb, p)), 0.0) for p in range(N_PAIRS)]
    return ds, jnp.exp2(g_tot)


def _decay_state(s, e_row):
    n = s.shape[0]
    eye = (lax.broadcasted_iota(jnp.int32, (n, n), 0) == lax.broadcasted_iota(jnp.int32, (n, n), 1))
    col = jnp.sum(jnp.where(eye, jnp.broadcast_to(e_row, (n, n)), 0.0), axis=1, keepdims=True)
    return s * col


def _hgrn_intra_scores(q, ks, gcs, txs):
    n = q.shape[0]
    row = lax.broadcasted_iota(jnp.int32, q.shape, 0)
    head_a = jnp.where(_first_head(q.shape), 1.0, 0.0).astype(BF16)
    head_b = 1.0 - head_a
    q_bf = q.astype(BF16)
    k_bf = [(ks[d].astype(BF16) * head_a, ks[d].astype(BF16) * head_b) for d in range(2)]
    boundary = [_block_row_fn(gcs[0], False), _block_row_fn(gcs[1], True)]
    scores = [None] * HG_HEADS
    w = 1
    while w < n:
        blk = 2 * w
        second = jnp.where((row & w) != 0, 1.0, 0.0).astype(BF16)
        e_f = jnp.exp2(-jnp.abs(gcs[0] - boundary[0](blk))).astype(BF16)
        e_b = jnp.exp2(-jnp.abs(gcs[1] - boundary[1](blk))).astype(BF16)
        e_f2, e_b2 = e_f * second, e_b * second
        e_f1, e_b1 = e_f - e_f2, e_b - e_b2
        q_f = q_bf * e_f2
        q_b = q_bf * e_b1
        k_heads = [(k_bf[0][j] * e_f1, k_bf[1][j] * e_b2) for j in range(2)]
        sel = (txs >= w) & (txs < blk)
        for p in range(N_PAIRS):
            qc = jnp.concatenate([_pair(q_f, p), _pair(q_b, p)], axis=1)
            for j in range(2):
                kc = jnp.concatenate([_pair(k_heads[j][0], p), _pair(k_heads[j][1], p)], axis=1)
                s = jnp.where(sel, _mm_nt(qc, kc), 0.0)
                h = 2 * p + j
                scores[h] = s if scores[h] is None else scores[h] + s
        w = blk
    return scores


def _hgrn_kernel(*refs, t_len, use_ctx):
    if use_ctx:
        (q_ref, ff_ref, fb_ref, i_ref, og_ref, lb_ref, ng_ref, s0_ref, o_ref, sin_ref, gk_ref) = refs
    else:
        (q_ref, ff_ref, fb_ref, i_ref, og_ref, lb_ref, ng_ref, o_ref, sl_ref) = refs
    c = HG_CHUNK
    n_chunks = t_len // c
    f_refs = (ff_ref, fb_ref)

    def load_dir(d, r0):
        log_f, k = _hgrn_gates(f_refs[d][pl.ds(r0, c), :], lb_ref[0, d:d + 1, :])
        return _cumsum_rows(log_f, reverse=(d == 1)), k

    if use_ctx:
        zero = jnp.zeros((HG_DK, HG_DV), F32)
        for d in range(2):
            states = []
            for p in range(N_PAIRS):
                top = jnp.concatenate([s0_ref[0, 0, d, 2 * p], zero], axis=1)
                bot = jnp.concatenate([zero, s0_ref[0, 0, d, 2 * p + 1]], axis=1)
                states.append(jnp.concatenate([top, bot], axis=0))
            order = range(n_chunks) if d == 0 else range(n_chunks - 1, -1, -1)
            for ci in order:
                for p in range(N_PAIRS):
                    sin_ref[ci, p, d * LANES:(d + 1) * LANES, :] = states[p]
                gc, k = load_dir(d, ci * c)
                gk_ref[d, 0, pl.ds(ci * c, c), :] = gc
                gk_ref[d, 1, pl.ds(ci * c, c), :] = k
                ds, e_tot = _hgrn_chunk_state(gc, k, i_ref[pl.ds(ci * c, c), :], reverse=(d == 1))
                states = [_decay_state(states[p], _pair(e_tot, p)) + ds[p] for p in range(N_PAIRS)]

    ti = lax.broadcasted_iota(jnp.int32, (c, c), 0)
    si = lax.broadcasted_iota(jnp.int32, (c, c), 1)
    txs = ti ^ si
    diag = ti == si

    def chunk_body(ci, carry):
        r0 = pl.multiple_of(ci * c, c)
        hq = q_ref[pl.ds(r0, c), :]
        q = hq * _sigmoid(hq)
        v = i_ref[pl.ds(r0, c), :]
        gcs, ks = [], []
        for d in range(2):
            if use_ctx:
                gc, k = gk_ref[d, 0, pl.ds(r0, c), :], gk_ref[d, 1, pl.ds(r0, c), :]
            else:
                gc, k = load_dir(d, r0)
            gcs.append(gc)
            ks.append(k)
        scores = _hgrn_intra_scores(q, ks, gcs, txs)
        first = _first_head(q.shape)
        k_sum = ks[0] + ks[1]
        v_heads = (jnp.where(first, v, 0.0).astype(BF16), jnp.where(first, 0.0, v).astype(BF16))
        k_heads = (jnp.where(first, k_sum, 0.0).astype(BF16), jnp.where(first, 0.0, k_sum).astype(BF16))
        q_bf = q.astype(BF16)
        if use_ctx:
            qe = [(q * jnp.exp2(gcs[d])).astype(BF16) for d in range(2)]
        else:
            for d in range(2):
                ds, _ = _hgrn_chunk_state(gcs[d], ks[d], v, reverse=(d == 1))
                for p in range(N_PAIRS):
                    sl_ref[0, d, 2 * p] = ds[p][:HG_DK, :HG_DV]
                    sl_ref[0, d, 2 * p + 1] = ds[p][HG_DK:, HG_DV:]
        outs = []
        for p in range(N_PAIRS):
            o = None
            for j in range(2):
                s_diag = jnp.where(diag, _mm_nt(_pair(q_bf, p), _pair(k_heads[j], p)), 0.0)
                t = _mm(scores[2 * p + j] + s_diag, _pair(v_heads[j], p))
                o = t if o is None else o + t
            if use_ctx:
                o = o + _mm(jnp.concatenate([_pair(qe[0], p), _pair(qe[1], p)], axis=1), sin_ref[ci, p])
            outs.append(_head_rms(o, ng_ref[0]))
        og = og_ref[pl.ds(r0, c), :]
        o_all = jnp.concatenate(outs, axis=1) * (og * _sigmoid(og))
        o_ref[pl.ds(r0, c), :] = o_all.astype(BF16)
        return carry

    if n_chunks == 1:
        chunk_body(0, 0)
    else:
        lax.fori_loop(0, n_chunks, chunk_body, 0)


def _hgrn(z, lb, norm_g, layer, n_batch, t_len, row0, state=None):
    use_ctx = state is not None
    rb0 = row0 // t_len
    cb = Z_HG // HG_W

    def zspec(k):
        return pl.BlockSpec((t_len, HG_W), lambda b: (rb0 + b, cb + k))

    in_specs = [zspec(0), zspec(1), zspec(2), zspec(3), zspec(4),
                pl.BlockSpec((1, 2, HG_W), lambda b: (layer, 0, 0)),
                pl.BlockSpec((1, 1, LANES), lambda b: (layer, 0, 0))]
    args = [z, z, z, z, z, lb, jnp.tile(norm_g, (1, LANES // HG_DV)).reshape(-1, 1, LANES)]
    out_specs = [pl.BlockSpec((t_len, HG_W), lambda b: (b, 0))]
    out_shape = [jax.ShapeDtypeStruct((n_batch * t_len, HG_W), BF16)]
    scratch = []
    if use_ctx:
        in_specs.append(pl.BlockSpec((1, 1, 2, HG_HEADS, HG_DK, HG_DV), lambda b: (b, layer, 0, 0, 0, 0)))
        args.append(state)
        scratch.append(pltpu.VMEM((t_len // HG_CHUNK, N_PAIRS, 2 * LANES, LANES), F32))
        scratch.append(pltpu.VMEM((2, 2, t_len, HG_W), F32))
    else:
        out_specs.append(pl.BlockSpec((1, 2, HG_HEADS, HG_DK, HG_DV), lambda b: (b, 0, 0, 0, 0)))
        out_shape.append(jax.ShapeDtypeStruct((n_batch, 2, HG_HEADS, HG_DK, HG_DV), F32))
    return pl.pallas_call(
        functools.partial(_hgrn_kernel, t_len=t_len, use_ctx=use_ctx),
        grid=(n_batch,),
        in_specs=in_specs,
        out_specs=out_specs,
        out_shape=out_shape,
        scratch_shapes=scratch,
        compiler_params=_cparams(("arbitrary",), 56),
        name="hgrn_ctx" if use_ctx else "hgrn",
    )(*args)


def _merge_kernel(a_ref, r_ref, h_ref, g0_ref, g1_ref, g2_ref, x_ref, mod_ref,
                  wa_ref, wr_ref, wh_ref, wo_ref, o_ref, wa_s, wr_s, wh_s, wo_s):
    @pl.when(pl.program_id(0) == 0)
    def _():
        wa_s[...] = wa_ref[0].astype(BF16)
        wr_s[...] = wr_ref[0].astype(BF16)
        wh_s[...] = wh_ref[0].astype(BF16)
        wo_s[...] = wo_ref[0].astype(BF16)

    def proj(x_r, w_s):
        return jnp.dot(x_r[...], w_s[...], preferred_element_type=F32)

    merged = (_sigmoid(g0_ref[...]) * proj(a_ref, wa_s)
              + _sigmoid(g1_ref[...]) * proj(r_ref, wr_s)
              + _sigmoid(g2_ref[...]) * proj(h_ref, wh_s))
    out = jnp.dot(merged.astype(BF16), wo_s[...], preferred_element_type=F32)
    o_ref[...] = x_ref[...] + mod_ref[0, 2:3, :] * out


def _merge_project(attn, rnn, hg, z, x, modseg, w_attn_o, w_rnn_o, w_hgrn_o, w_out, layer):
    n_tok = x.shape[0]
    tm = 512
    per_seg = SEG_ROWS // tm
    half = pl.BlockSpec((tm, ATTN_W), lambda i: (i, 0))
    full = pl.BlockSpec((tm, D_MODEL), lambda i: (i, 0))

    def gspec(k):
        return pl.BlockSpec((tm, D_MODEL), lambda i: (i, Z_GL // D_MODEL + k))

    def wspec(rows):
        return pl.BlockSpec((1, rows, D_MODEL), lambda i: (layer, 0, 0))

    return pl.pallas_call(
        _merge_kernel,
        grid=(n_tok // tm,),
        in_specs=[half, half, half, gspec(0), gspec(1), gspec(2), full,
                  pl.BlockSpec((1, N_MOD, D_MODEL), lambda i: (i // per_seg, 0, 0)),
                  wspec(ATTN_W), wspec(D_RNN), wspec(HG_W), wspec(D_MODEL)],
        out_specs=full,
        out_shape=jax.ShapeDtypeStruct((n_tok, D_MODEL), F32),
        scratch_shapes=[pltpu.VMEM((ATTN_W, D_MODEL), BF16), pltpu.VMEM((D_RNN, D_MODEL), BF16),
                        pltpu.VMEM((HG_W, D_MODEL), BF16), pltpu.VMEM((D_MODEL, D_MODEL), BF16)],
        compiler_params=_cparams(("arbitrary",), 56),
        name="merge_project",
    )(attn, rnn, hg, z, z, z, x, modseg, w_attn_o, w_rnn_o, w_hgrn_o, w_out)


def _router_kernel(x_ref, g_ref, mod_ref, wr_ref, br_ref, h_ref, slot_ref, wt_ref, meta_ref, run_ref):
    @pl.when(pl.program_id(0) == 0)
    def _():
        run_ref[...] = jnp.zeros_like(run_ref)

    h = _modulated_norm(x_ref[...], g_ref[0], mod_ref[0, 4:5, :], mod_ref[0, 3:4, :])
    h_ref[...] = h
    logits = jnp.dot(h, wr_ref[0], preferred_element_type=F32,
                     precision=lax.Precision.HIGHEST) + br_ref[0]
    tm = logits.shape[0]
    lane = lax.broadcasted_iota(jnp.int32, logits.shape, 1)
    ids, vals = [], []
    chosen = jnp.zeros(logits.shape, jnp.bool_)
    work = logits
    for _ in range(TOP_K):
        m = jnp.max(work, axis=-1, keepdims=True)
        idx = jnp.min(jnp.where(work == m, lane, LANES), axis=-1, keepdims=True)
        hit = lane == idx
        chosen = chosen | hit
        work = jnp.where(hit, -jnp.inf, work)
        ids.append(idx)
        vals.append(m)
    exps = [jnp.exp(v - vals[0]) for v in vals]
    den = exps[0] + exps[1] + exps[2] + exps[3]
    ind = jnp.where(chosen, 1.0, 0.0)
    r_i = lax.broadcasted_iota(jnp.int32, (tm, tm), 0)
    c_i = lax.broadcasted_iota(jnp.int32, (tm, tm), 1)
    count = jnp.sum(ind, axis=0, keepdims=True)
    padded = jnp.floor((count + (ROW_BLOCK - 1)) * (1.0 / ROW_BLOCK)) * ROW_BLOCK
    incl = jnp.broadcast_to(padded, (8, LANES))
    lane8 = lax.broadcasted_iota(jnp.int32, (8, LANES), 1)
    d = 1
    while d < LANES:
        incl = incl + jnp.where(lane8 >= d, pltpu.roll(incl, d, 1), 0.0)
        d *= 2
    local_start = incl[0:1] - padded
    place = _mm(jnp.where(c_i < r_i, 1.0, 0.0), ind) + local_start
    slot_out = jnp.zeros(logits.shape, jnp.int32)
    wt_out = jnp.zeros(logits.shape, F32)
    for k in range(TOP_K):
        slot = jnp.sum(jnp.where(lane == ids[k], place, 0.0), axis=-1, keepdims=True)
        slot_out = jnp.where(lane == k, slot.astype(jnp.int32), slot_out)
        wt_out = jnp.where(lane == k, exps[k] / den, wt_out)
    slot_ref[...] = slot_out
    wt_ref[...] = wt_out
    row8 = lax.broadcasted_iota(jnp.int32, (8, LANES), 0)
    table = jnp.where(row8 == 0, local_start, jnp.where(row8 == 1, padded, run_ref[0:1, :]))
    meta_ref[0] = jnp.where(row8 < 3, table, 0.0).astype(jnp.int32)
    run_ref[...] = run_ref[...] + padded


def _router(x, norm_g, modseg, w_router, b_router, layer):
    n_tok = x.shape[0]
    tm = TOKEN_TILE
    per_seg = SEG_ROWS // tm
    depth = w_router.shape[0]
    w_pad = jnp.zeros((depth, D_MODEL, LANES), F32).at[:, :, :N_EXPERTS].set(w_router)
    b_pad = jnp.full((depth, 1, LANES), NEG_BIG, F32).at[:, 0, :N_EXPERTS].set(b_router)
    full = pl.BlockSpec((tm, D_MODEL), lambda i: (i, 0))
    small = pl.BlockSpec((tm, LANES), lambda i: (i, 0))
    return pl.pallas_call(
        _router_kernel,
        grid=(n_tok // tm,),
        in_specs=[full,
                  pl.BlockSpec((1, 1, D_MODEL), lambda i: (layer, 0, 0)),
                  pl.BlockSpec((1, N_MOD, D_MODEL), lambda i: (i // per_seg, 0, 0)),
                  pl.BlockSpec((1, D_MODEL, LANES), lambda i: (layer, 0, 0)),
                  pl.BlockSpec((1, 1, LANES), lambda i: (layer, 0, 0))],
        out_specs=[full, small, small, pl.BlockSpec((1, 8, LANES), lambda i: (i, 0, 0))],
        out_shape=[jax.ShapeDtypeStruct((n_tok, D_MODEL), F32),
                   jax.ShapeDtypeStruct((n_tok, LANES), jnp.int32),
                   jax.ShapeDtypeStruct((n_tok, LANES), F32),
                   jax.ShapeDtypeStruct((n_tok // tm, 8, LANES), jnp.int32)],
        scratch_shapes=[pltpu.VMEM((8, LANES), F32)],
        compiler_params=_cparams(("arbitrary",), 40),
        name="router",
    )(x, norm_g.reshape(-1, 1, D_MODEL), modseg, w_pad, b_pad)


def _for_each_block(meta_ref, fn):
    def group(e, carry):
        local0 = meta_ref[e]
        sorted0 = meta_ref[2 * N_EXPERTS + e]
        size = meta_ref[N_EXPERTS + e]
        n_big = size // BIG_BLOCK

        def big(j, c):
            fn(pl.multiple_of(local0 + j * BIG_BLOCK, ROW_BLOCK),
               pl.multiple_of(sorted0 + j * BIG_BLOCK, ROW_BLOCK), BIG_BLOCK)
            return c

        def small(j, c):
            off = n_big * BIG_BLOCK + j * ROW_BLOCK
            fn(pl.multiple_of(local0 + off, ROW_BLOCK), pl.multiple_of(sorted0 + off, ROW_BLOCK), ROW_BLOCK)
            return c

        carry = lax.fori_loop(0, n_big, big, carry)
        return lax.fori_loop(0, (size - n_big * BIG_BLOCK) // ROW_BLOCK, small, carry)

    lax.fori_loop(0, N_EXPERTS, group, 0)


def _slot_matrix(slot_ref, values):
    col = lax.broadcasted_iota(jnp.int32, (TOKEN_TILE, LOCAL_ROWS), 1)
    out = jnp.zeros((TOKEN_TILE, LOCAL_ROWS), F32)
    for k in range(TOP_K):
        out = jnp.where(col == slot_ref[:, k:k + 1], values[k], out)
    return out.astype(BF16)


def _dispatch_kernel(meta_ref, prev_meta_ref, tail_ref, slot_ref, h_ref, xs_ref, local_ref, zero_ref,
                     sem, block_sem):
    i = pl.program_id(0)
    cur = i % 2

    @pl.when(i == 0)
    def _():
        zero_ref[...] = jnp.zeros_like(zero_ref)

        def tails(fn):
            def group(e, carry):
                def block(j, c):
                    fn(pl.multiple_of(tail_ref[e] + j * ROW_BLOCK, ROW_BLOCK))
                    return c
                return lax.fori_loop(0, tail_ref[N_EXPERTS + e], block, carry)
            lax.fori_loop(0, N_EXPERTS, group, 0)

        def zero_copy(row):
            return pltpu.make_async_copy(zero_ref.at[pl.ds(0, ROW_BLOCK)],
                                         xs_ref.at[pl.ds(row, ROW_BLOCK)], sem)

        tails(lambda row: zero_copy(row).start())
        tails(lambda row: zero_copy(row).wait())

        def unused(fn):
            def tile(j, c):
                fn(pl.multiple_of(tail_ref[2 * N_EXPERTS] + j * EXPERT_TILE, EXPERT_TILE))
                return c
            lax.fori_loop(0, tail_ref[2 * N_EXPERTS + 1], tile, 0)

        def zero_tile(row):
            return pltpu.make_async_copy(zero_ref, xs_ref.at[pl.ds(row, EXPERT_TILE)], sem)

        unused(lambda row: zero_tile(row).start())
        unused(lambda row: zero_tile(row).wait())

    onehot = _slot_matrix(slot_ref, [1.0] * TOP_K)
    local_ref[cur] = _mm_tn(onehot, h_ref[...])

    def block_copy(buf):
        return lambda local_row, sorted_row, rows: pltpu.make_async_copy(
            local_ref.at[buf, pl.ds(local_row, rows)], xs_ref.at[pl.ds(sorted_row, rows)],
            block_sem.at[buf])

    _for_each_block(meta_ref, lambda a, b, n: block_copy(cur)(a, b, n).start())

    @pl.when(i > 0)
    def _():
        _for_each_block(prev_meta_ref, lambda a, b, n: block_copy(1 - cur)(a, b, n).wait())

    @pl.when(i == pl.num_programs(0) - 1)
    def _():
        _for_each_block(meta_ref, lambda a, b, n: block_copy(cur)(a, b, n).wait())


def _dispatch(h, slots, meta, tails, n_rows):
    n_tok = h.shape[0]
    return pl.pallas_call(
        _dispatch_kernel,
        grid=(n_tok // TOKEN_TILE,),
        in_specs=[pl.BlockSpec((LANES,), lambda i: (i,), memory_space=pltpu.SMEM),
                  pl.BlockSpec((LANES,), lambda i: (jnp.maximum(i - 1, 0),), memory_space=pltpu.SMEM),
                  pl.BlockSpec((LANES,), lambda i: (0,), memory_space=pltpu.SMEM),
                  pl.BlockSpec((TOKEN_TILE, LANES), lambda i: (i, 0)),
                  pl.BlockSpec((TOKEN_TILE, D_MODEL), lambda i: (i, 0))],
        out_specs=pl.BlockSpec(memory_space=pl.ANY),
        out_shape=jax.ShapeDtypeStruct((n_rows, D_MODEL), F32),
        scratch_shapes=[pltpu.VMEM((2, LOCAL_ROWS, D_MODEL), F32), pltpu.VMEM((EXPERT_TILE, D_MODEL), F32),
                        pltpu.SemaphoreType.DMA(()), pltpu.SemaphoreType.DMA((2,))],
        compiler_params=_cparams(("arbitrary",), 48),
        name="dispatch",
    )(meta, meta, tails, slots, h)


def _combine_kernel(meta_ref, next_meta_ref, slot_ref, wt_ref, ys_ref, x_ref, mod_ref, o_ref, local_ref, sem):
    i = pl.program_id(0)
    cur = i % 2

    def block_copy(buf):
        return lambda local_row, sorted_row, rows: pltpu.make_async_copy(
            ys_ref.at[pl.ds(sorted_row, rows)], local_ref.at[buf, pl.ds(local_row, rows)], sem.at[buf])

    def start_gather(table_ref, buf):
        local_ref[buf, TOKEN_TILE * TOP_K:, :] = jnp.zeros((LOCAL_ROWS - TOKEN_TILE * TOP_K, D_MODEL), F32)
        _for_each_block(table_ref, lambda a, b, n: block_copy(buf)(a, b, n).start())

    @pl.when(i == 0)
    def _():
        start_gather(meta_ref, 0)

    @pl.when(i + 1 < pl.num_programs(0))
    def _():
        start_gather(next_meta_ref, 1 - cur)

    _for_each_block(meta_ref, lambda a, b, n: block_copy(cur)(a, b, n).wait())
    weights = _slot_matrix(slot_ref, [wt_ref[:, k:k + 1] for k in range(TOP_K)])
    acc = jnp.dot(weights, local_ref[cur].astype(BF16), preferred_element_type=F32)
    o_ref[...] = x_ref[...] + mod_ref[0, 5:6, :] * acc


def _combine(ys, slots, meta, wts, x, modseg):
    n_tok = x.shape[0]
    per_seg = SEG_ROWS // TOKEN_TILE
    full = pl.BlockSpec((TOKEN_TILE, D_MODEL), lambda i: (i, 0))
    small = pl.BlockSpec((TOKEN_TILE, LANES), lambda i: (i, 0))
    return pl.pallas_call(
        _combine_kernel,
        grid=(n_tok // TOKEN_TILE,),
        in_specs=[pl.BlockSpec((LANES,), lambda i: (i,), memory_space=pltpu.SMEM),
                  pl.BlockSpec((LANES,), lambda i: (jnp.minimum(i + 1, n_tok // TOKEN_TILE - 1),),
                               memory_space=pltpu.SMEM),
                  small, small,
                  pl.BlockSpec(memory_space=pl.ANY),
                  full,
                  pl.BlockSpec((1, N_MOD, D_MODEL), lambda i: (i // per_seg, 0, 0))],
        out_specs=full,
        out_shape=jax.ShapeDtypeStruct((n_tok, D_MODEL), F32),
        scratch_shapes=[pltpu.VMEM((2, LOCAL_ROWS, D_MODEL), F32), pltpu.SemaphoreType.DMA((2,))],
        compiler_params=_cparams(("arbitrary",), 56),
        name="combine",
    )(meta, meta, slots, wts, ys, x, modseg)


def _expert_kernel(te_ref, first_ref, slot_ref, next_ref, na_ref, x_ref, wgu_hbm, bgu_ref, wd_hbm, bd_ref,
                   o_ref, wgu_f, wd_f, wgu_s, wd_s, sem, *, layer):
    i = pl.program_id(0)
    active = i < na_ref[0]

    def fetch(e, s):
        return (pltpu.make_async_copy(wgu_hbm.at[layer, e], wgu_f.at[s], sem.at[0, s]),
                pltpu.make_async_copy(wd_hbm.at[layer, e], wd_f.at[s], sem.at[1, s]))

    @pl.when(i == 0)
    def _():
        for cp in fetch(te_ref[0], 0):
            cp.start()

    @pl.when(first_ref[i] == 1)
    def _():
        s = slot_ref[i]
        for cp in fetch(te_ref[i], s):
            cp.wait()
        wgu_s[...] = wgu_f[s].astype(BF16)
        wd_s[...] = wd_f[s].astype(BF16)

        @pl.when(next_ref[i] >= 0)
        def _():
            for cp in fetch(next_ref[i], 1 - s):
                cp.start()

    @pl.when(active)
    def _():
        gu = jnp.dot(x_ref[...].astype(BF16), wgu_s[...], preferred_element_type=F32) + bgu_ref[0, 0]
        gate = jnp.minimum(gu[:, :D_FF], SWIGLU_LIMIT)
        up = jnp.clip(gu[:, D_FF:], -SWIGLU_LIMIT, SWIGLU_LIMIT)
        act = gate * _sigmoid(SWIGLU_ALPHA * gate) * (up + 1.0)
        o_ref[...] = jnp.dot(act.astype(BF16), wd_s[...], preferred_element_type=F32) + bd_ref[0, 0]

    @pl.when(jnp.logical_not(active))
    def _():
        o_ref[...] = jnp.zeros_like(o_ref)


def _experts(xs, tile_expert, ends, n_active, w_gu, b_gu, w_down, b_down, layer):
    n_rows = xs.shape[0]
    depth = w_gu.shape[0]
    tm = EXPERT_TILE
    n_tiles = n_rows // tm
    tile = jnp.arange(n_tiles, dtype=jnp.int32)
    active = tile < n_active[0]
    first = (active & ((tile == 0) | (tile_expert != jnp.roll(tile_expert, 1)))).astype(jnp.int32)
    slot = (jnp.cumsum(first) - 1) % 2
    next_tile = ends[tile_expert] // tm
    next_expert = jnp.where(next_tile < n_active[0],
                            tile_expert[jnp.minimum(next_tile, n_tiles - 1)], -1).astype(jnp.int32)

    def tiled(shape, index):
        return pl.BlockSpec(shape, lambda i, te, fi, sl, nx, na: index(i, te, na))

    grid_spec = pltpu.PrefetchScalarGridSpec(
        num_scalar_prefetch=5,
        grid=(n_tiles,),
        in_specs=[tiled((tm, D_MODEL), lambda i, te, na: (jnp.where(i < na[0], i, 0), 0)),
                  pl.BlockSpec(memory_space=pl.ANY),
                  tiled((1, 1, 1, 2 * D_FF), lambda i, te, na: (layer, te[i], 0, 0)),
                  pl.BlockSpec(memory_space=pl.ANY),
                  tiled((1, 1, 1, D_MODEL), lambda i, te, na: (layer, te[i], 0, 0))],
        out_specs=tiled((tm, D_MODEL), lambda i, te, na: (i, 0)),
        scratch_shapes=[pltpu.VMEM((2, D_MODEL, 2 * D_FF), F32), pltpu.VMEM((2, D_FF, D_MODEL), F32),
                        pltpu.VMEM((D_MODEL, 2 * D_FF), BF16), pltpu.VMEM((D_FF, D_MODEL), BF16),
                        pltpu.SemaphoreType.DMA((2, 2))],
    )
    return pl.pallas_call(
        functools.partial(_expert_kernel, layer=layer),
        grid_spec=grid_spec,
        out_shape=jax.ShapeDtypeStruct((n_rows, D_MODEL), F32),
        compiler_params=_cparams(("arbitrary",), 56),
        name="experts",
    )(tile_expert, first, slot.astype(jnp.int32), next_expert, n_active, xs, w_gu,
      b_gu.reshape(depth, N_EXPERTS, 1, 2 * D_FF), w_down, b_down.reshape(depth, N_EXPERTS, 1, D_MODEL))


def _moe(x, norm_g, modseg, w_router, b_router, w_gu, b_gu, w_down, b_down, layer):
    n_tok = x.shape[0]
    h, slots, wts, table = _router(x, norm_g, modseg, w_router, b_router, layer)
    tm = EXPERT_TILE
    n_tiles = n_tok // TOKEN_TILE
    n_rows = n_tok * TOP_K + n_tiles * N_EXPERTS * ROW_BLOCK + N_EXPERTS * tm
    n_rows = -(-n_rows // tm) * tm
    local_start = table[:, 0, :N_EXPERTS]
    padded = table[:, 1, :N_EXPERTS]
    earlier = table[:, 2, :N_EXPERTS]
    used = earlier[-1] + padded[-1]
    size = ((used + tm - 1) // tm) * tm
    ends = jnp.cumsum(size)
    starts = ends - size
    sorted_start = starts[None, :] + earlier
    meta = jnp.concatenate([local_start, padded, sorted_start, jnp.zeros_like(padded)], axis=1).reshape(-1)
    tails = jnp.concatenate([starts + used, (size - used) // ROW_BLOCK,
                             ends[-1:], (n_rows - ends[-1:]) // tm,
                             jnp.zeros((LANES - 2 * N_EXPERTS - 2,), jnp.int32)])
    tile_row0 = jnp.arange(n_rows // tm, dtype=jnp.int32) * tm
    tile_expert = jnp.minimum(jnp.sum((ends[None, :] <= tile_row0[:, None]).astype(jnp.int32), axis=1),
                              N_EXPERTS - 1)
    n_active = (ends[-1:] // tm).astype(jnp.int32)
    xs = _dispatch(h, slots, meta, tails, n_rows)
    ys = _experts(xs, tile_expert, ends, n_active, w_gu, b_gu, w_down, b_down, layer)
    return _combine(ys, slots, meta, wts, x, modseg)


def _rope_tables(n_tok):
    rows = n_tok // GRID_W
    row = jnp.repeat(jnp.arange(rows, dtype=F32), GRID_W)
    col = jnp.tile(jnp.arange(GRID_W, dtype=F32), rows)
    quarter = HEAD_DIM // 4
    inv_freq = ROPE_BASE ** (-jnp.arange(quarter, dtype=F32) / quarter)
    ang_r = row[:, None] * inv_freq
    ang_c = col[:, None] * inv_freq
    ang = jnp.concatenate([ang_r, ang_r, ang_c, ang_c], axis=-1)
    cos, sin = jnp.cos(ang), jnp.sin(ang)
    first = (jnp.arange(HEAD_DIM) % (2 * quarter)) < quarter
    sin_a = jnp.where(first, -sin, 0.0)
    sin_b = jnp.where(first, 0.0, sin)
    return tuple(jnp.tile(t, (1, LANES // HEAD_DIM)) for t in (cos, sin_a, sin_b))


def _block_diag(w):
    eye = jnp.eye(RG_BLOCKS, dtype=w.dtype)
    return jnp.einsum("nde,nm->ndme", w, eye).reshape(D_RNN, D_RNN)


def kernel(x_prompt, x_sample, c, cache_k, cache_v, state_rglru, state_hgrn, c_ctx, w_mod, b_mod, norm1_g, norm2_g, w_in, q_norm_g, k_norm_g, conv_w, conv_b, rg_wa, rg_ba, rg_wx, rg_bx, rg_lambda, hgrn_lb_logits, hgrn_norm_g, w_attn_o, w_rnn_o, w_hgrn_o, w_out, w_router, b_router, w_gu, b_gu, w_down, b_down):
    depth = w_mod.shape[0]
    n_p, t_p = x_prompt.shape[0], x_prompt.shape[1]
    n_s, t_s = x_sample.shape[0], x_sample.shape[1]
    tok_p = n_p * t_p
    tok_s = n_s * t_s
    assert tok_p % SEG_ROWS == 0 and t_s == SEG_ROWS and SEG_ROWS % t_p == 0

    x = jnp.concatenate([x_prompt.reshape(tok_p, D_MODEL), x_sample.reshape(tok_s, D_MODEL)], axis=0)

    lb_all = jnp.cumsum(jax.nn.softmax(hgrn_lb_logits.astype(F32), axis=0), axis=0)
    lb_all = lb_all - lb_all[:1]
    rope = _rope_tables(t_s)

    cond8 = jnp.zeros((8, D_MODEL), F32).at[0].set(c_ctx).at[1:1 + n_s].set(c)
    mod = _modulation(cond8, w_mod, b_mod)
    seg_ids = jnp.asarray([0] * (tok_p // SEG_ROWS) + [1 + b for b in range(n_s)], jnp.int32)
    modsegs = mod[:, seg_ids].reshape(depth, seg_ids.shape[0], N_MOD, D_MODEL)

    w_gate = jnp.stack([jnp.concatenate([_block_diag(rg_wa[l, 0]), _block_diag(rg_wx[l, 0]),
                                         _block_diag(rg_wa[l, 1]), _block_diag(rg_wx[l, 1])], axis=1)
                        for l in range(depth)])
    b_gate = jnp.stack([jnp.concatenate([rg_ba[l, 0], rg_bx[l, 0], rg_ba[l, 1], rg_bx[l, 1]])[None]
                        for l in range(depth)])

    new_k, new_v, new_hr, new_s = [], [], [], []
    for l in range(depth):
        modseg = modsegs[l]
        z = _input_projection(x, norm1_g, modseg, w_in, l)
        attn_p, k_l, v_l = _attention(z, q_norm_g, k_norm_g, l, n_p, t_p, 0)
        (attn_s,) = _attention(z, q_norm_g, k_norm_g, l, n_s, t_s, tok_p, rope=rope,
                               cache=(cache_k, cache_v))
        rnn_p, hr_l = _rglru(z, conv_w, conv_b, w_gate, b_gate, rg_lambda, l, n_p, t_p, 0)
        (rnn_s,) = _rglru(z, conv_w, conv_b, w_gate, b_gate, rg_lambda, l, n_s, t_s, tok_p,
                          state=state_rglru)
        hg_p, s_l = _hgrn(z, lb_all, hgrn_norm_g, l, n_p, t_p, 0)
        (hg_s,) = _hgrn(z, lb_all, hgrn_norm_g, l, n_s, t_s, tok_p, state=state_hgrn)
        attn = jnp.concatenate([attn_p, attn_s], axis=0)
        rnn = jnp.concatenate([rnn_p, rnn_s], axis=0)
        hg = jnp.concatenate([hg_p, hg_s], axis=0)
        x = _merge_project(attn, rnn, hg, z, x, modseg, w_attn_o, w_rnn_o, w_hgrn_o, w_out, l)
        x = _moe(x, norm2_g, modseg, w_router, b_router, w_gu, b_gu, w_down, b_down, l)
        new_k.append(k_l.reshape(n_p, t_p, N_KV_HEADS, HEAD_DIM))
        new_v.append(v_l.reshape(n_p, t_p, N_KV_HEADS, HEAD_DIM))
        new_hr.append(hr_l)
        new_s.append(s_l)

    y_prompt = x[:tok_p].reshape(n_p, t_p, D_MODEL)
    y_sample = x[tok_p:].reshape(n_s, t_s, D_MODEL)
    return (y_prompt, y_sample, jnp.stack(new_k, axis=1), jnp.stack(new_v, axis=1),
            jnp.stack(new_hr, axis=1), jnp.stack(new_s, axis=1))
```

```python
import functools

import jax
import jax.numpy as jnp
from jax import lax
from jax.experimental import pallas as pl
from jax.experimental.pallas import tpu as pltpu

F32 = jnp.float32
BF16 = jnp.bfloat16

D_MODEL = 1024
GRID_W = 64
HEAD_DIM = 64
N_HEADS = 8
N_KV_HEADS = 2
KV_GROUP = N_HEADS // N_KV_HEADS
ATTN_W = N_HEADS * HEAD_DIM
KV_W = N_KV_HEADS * HEAD_DIM
ROPE_BASE = 10000.0
D_RNN = 512
RG_BLOCKS = 8
RG_BW = D_RNN // RG_BLOCKS
RG_C = 8.0
HG_HEADS = 8
HG_DK = 64
HG_DV = 64
HG_W = HG_HEADS * HG_DK
N_PAIRS = HG_W // 128
N_EXPERTS = 32
TOP_K = 4
D_FF = D_MODEL
SWIGLU_LIMIT = 7.0
SWIGLU_ALPHA = 1.702
N_MOD = 6
EPS = 1e-6
IN_W = ATTN_W + 2 * KV_W + 2 * D_RNN + 5 * HG_W + 3 * D_MODEL

LANES = 128
SEG_ROWS = 1024
COL_TILE = 256
HG_CHUNK = 256
Q_TILE = 256
EXPERT_TILE = 384
BIG_BLOCK = 64
TOKEN_TILE = 512
ROW_BLOCK = 8
LOCAL_ROWS = TOKEN_TILE * TOP_K + N_EXPERTS * ROW_BLOCK
NEG_BIG = -1e30
LOG2_E = 1.4426950408889634

_SRC_TILES = tuple(range(17, 29)) + tuple(range(7, 17)) + tuple(range(3, 7)) + (0, 1, 2)
Z_GL = 0
Z_HG = 3 * D_MODEL
Z_RG = Z_HG + 5 * HG_W
Z_Q = Z_RG + 2 * D_RNN
Z_K = Z_Q + ATTN_W
Z_V = Z_K + KV_W
Z_W = -(-IN_W // (2 * COL_TILE)) * (2 * COL_TILE)


def _cparams(sem, vmem_mb):
    return pltpu.CompilerParams(dimension_semantics=sem, vmem_limit_bytes=vmem_mb * 1024 * 1024)


def _mm(a, b):
    return jnp.dot(a.astype(BF16), b.astype(BF16), preferred_element_type=F32)


def _mm_nt(a, b):
    return lax.dot_general(a.astype(BF16), b.astype(BF16), (((1,), (1,)), ((), ())),
                           preferred_element_type=F32)


def _mm_tn(a, b):
    return lax.dot_general(a.astype(BF16), b.astype(BF16), (((0,), (0,)), ((), ())),
                           preferred_element_type=F32)


def _sigmoid(x):
    return 1.0 / (1.0 + jnp.exp(-x))


def _softplus(x):
    return jnp.maximum(x, 0.0) + jnp.log1p(jnp.exp(-jnp.abs(x)))


def _modulated_norm(x, g, scale, shift):
    y = x * lax.rsqrt(jnp.mean(x * x, axis=-1, keepdims=True) + EPS)
    return (y * g) * (1.0 + scale) + shift


def _mod_kernel(c_ref, w_ref, b_ref, o_ref):
    c = c_ref[...]
    s = c * _sigmoid(c)
    o_ref[0] = _mm(s, w_ref[0]) + b_ref[0]


def _modulation(cond8, w_mod, b_mod):
    depth = w_mod.shape[0]
    tn = 1024
    return pl.pallas_call(
        _mod_kernel,
        grid=(depth, N_MOD * D_MODEL // tn),
        in_specs=[pl.BlockSpec((8, D_MODEL), lambda l, j: (0, 0)),
                  pl.BlockSpec((1, D_MODEL, tn), lambda l, j: (l, 0, j)),
                  pl.BlockSpec((1, 1, tn), lambda l, j: (l, 0, j))],
        out_specs=pl.BlockSpec((1, 8, tn), lambda l, j: (l, 0, j)),
        out_shape=jax.ShapeDtypeStruct((depth, 8, N_MOD * D_MODEL), F32),
        compiler_params=_cparams(("arbitrary", "arbitrary"), 32),
        name="modulation",
    )(cond8, w_mod, b_mod.reshape(depth, 1, N_MOD * D_MODEL))


def _inproj_kernel(src_ref, x_ref, g_ref, mod_ref, wa_ref, wb_ref, o_ref, h_ref, w_s):
    j = pl.program_id(1)

    @pl.when(j == 0)
    def _():
        for s in range(x_ref.shape[0] // SEG_ROWS):
            rows = slice(s * SEG_ROWS, (s + 1) * SEG_ROWS)
            h = _modulated_norm(x_ref[rows, :], g_ref[0], mod_ref[s, 1:2, :], mod_ref[s, 0:1, :])
            h_ref[rows, :] = h.astype(BF16)

    @pl.when(pl.program_id(0) == 0)
    def _():
        w_s[j, :, 0:COL_TILE] = wa_ref[0].astype(BF16)
        w_s[j, :, COL_TILE:2 * COL_TILE] = wb_ref[0].astype(BF16)

    o_ref[...] = jnp.dot(h_ref[...], w_s[j], preferred_element_type=F32)


def _input_projection(x, norm_g, modseg, w_in, layer):
    n_tok = x.shape[0]
    segs = 2
    tm = segs * SEG_ROWS
    n_col = Z_W // (2 * COL_TILE)
    src = jnp.asarray(_SRC_TILES + (_SRC_TILES[-1],) * (Z_W // COL_TILE - len(_SRC_TILES)), jnp.int32)

    def wspec(k):
        return pl.BlockSpec((1, D_MODEL, COL_TILE),
                            lambda i, j, s: (layer, 0, s[2 * jnp.where(i == 0, j, n_col - 1) + k]))

    grid_spec = pltpu.PrefetchScalarGridSpec(
        num_scalar_prefetch=1,
        grid=(n_tok // tm, n_col),
        in_specs=[pl.BlockSpec((tm, D_MODEL), lambda i, j, s: (i, 0)),
                  pl.BlockSpec((1, 1, D_MODEL), lambda i, j, s: (layer, 0, 0)),
                  pl.BlockSpec((segs, N_MOD, D_MODEL), lambda i, j, s: (i, 0, 0)),
                  wspec(0), wspec(1)],
        out_specs=pl.BlockSpec((tm, 2 * COL_TILE), lambda i, j, s: (i, j)),
        scratch_shapes=[pltpu.VMEM((tm, D_MODEL), BF16),
                        pltpu.VMEM((n_col, D_MODEL, 2 * COL_TILE), BF16)],
    )
    return pl.pallas_call(
        _inproj_kernel,
        grid_spec=grid_spec,
        out_shape=jax.ShapeDtypeStruct((n_tok, Z_W), F32),
        compiler_params=_cparams(("arbitrary", "arbitrary"), 56),
        name="input_projection",
    )(src, x, norm_g.reshape(-1, 1, D_MODEL), modseg, w_in, w_in)


def _head_rms(x, g):
    lane = lax.broadcasted_iota(jnp.int32, x.shape, 1)
    lo = lane < HEAD_DIM
    xx = x * x
    s_lo = jnp.sum(jnp.where(lo, xx, 0.0), axis=-1, keepdims=True)
    s_hi = jnp.sum(jnp.where(lo, 0.0, xx), axis=-1, keepdims=True)
    inv = jnp.where(lo, lax.rsqrt(s_lo * (1.0 / HEAD_DIM) + EPS),
                    lax.rsqrt(s_hi * (1.0 / HEAD_DIM) + EPS))
    return x * inv * g


def _rope(x, cos, sin_a, sin_b):
    q = HEAD_DIM // 4
    return x * cos + pltpu.roll(x, LANES - q, 1) * sin_a + pltpu.roll(x, q, 1) * sin_b


def _attn_kernel(*refs, t_len, use_ctx):
    if use_ctx:
        (q_ref, k_ref, v_ref, qg_ref, kg_ref, cos_q_ref, sa_q_ref, sb_q_ref,
         cos_k_ref, sa_k_ref, sb_k_ref, ck_ref, cv_ref, o_ref, kall_ref, vall_ref) = refs
    else:
        (q_ref, k_ref, v_ref, qg_ref, kg_ref, o_ref, ko_ref, vo_ref, kall_ref, vall_ref) = refs

    @pl.when(pl.program_id(1) == 0)
    def _():
        k = _head_rms(k_ref[...], kg_ref[...])
        if use_ctx:
            k = _rope(k, cos_k_ref[...], sa_k_ref[...], sb_k_ref[...])
            kall_ref[0:t_len, :] = k.astype(BF16)
            vall_ref[0:t_len, :] = v_ref[...].astype(BF16)
            kall_ref[t_len:, :] = ck_ref[0, 0].astype(BF16)
            vall_ref[t_len:, :] = cv_ref[0, 0].astype(BF16)
        else:
            ko_ref[0] = k
            vo_ref[0] = v_ref[...]
            kall_ref[...] = k.astype(BF16)
            vall_ref[...] = v_ref[...].astype(BF16)

    heads = []
    for c in range(ATTN_W // LANES):
        qc = _head_rms(q_ref[:, c * LANES:(c + 1) * LANES], qg_ref[...])
        if use_ctx:
            qc = _rope(qc, cos_q_ref[...], sa_q_ref[...], sb_q_ref[...])
        qc = (qc * (HEAD_DIM ** -0.5)).astype(BF16)
        heads.append(qc[:, :HEAD_DIM])
        heads.append(qc[:, HEAD_DIM:])
    tq = q_ref.shape[0]
    for g in range(N_KV_HEADS):
        qs = jnp.concatenate(heads[g * KV_GROUP:(g + 1) * KV_GROUP], axis=0)
        kh = kall_ref[:, g * HEAD_DIM:(g + 1) * HEAD_DIM]
        vh = vall_ref[:, g * HEAD_DIM:(g + 1) * HEAD_DIM]
        s = _mm_nt(qs, kh)
        m = jnp.max(s, axis=-1, keepdims=True)
        p = jnp.exp(s - m)
        den = jnp.sum(p, axis=-1, keepdims=True)
        o = _mm(p, vh) / den
        for j in range(KV_GROUP):
            hh = g * KV_GROUP + j
            o_ref[:, hh * HEAD_DIM:(hh + 1) * HEAD_DIM] = o[j * tq:(j + 1) * tq].astype(BF16)


def _attention(z, q_g, k_g, layer, n_batch, t_len, row0, rope=None, cache=None):
    use_ctx = cache is not None
    nq = t_len // Q_TILE
    rb0 = row0 // t_len
    qb0 = row0 // Q_TILE
    qg = jnp.tile(q_g[layer], 2).reshape(1, LANES)
    kg = jnp.tile(k_g[layer], 2).reshape(1, LANES)
    n_out_tok = n_batch * t_len
    vec = pl.BlockSpec((1, LANES), lambda b, i: (0, 0))
    in_specs = [pl.BlockSpec((Q_TILE, ATTN_W), lambda b, i: (qb0 + b * nq + i, Z_Q // ATTN_W)),
                pl.BlockSpec((t_len, KV_W), lambda b, i: (rb0 + b, Z_K // KV_W)),
                pl.BlockSpec((t_len, KV_W), lambda b, i: (rb0 + b, Z_V // KV_W)),
                vec, vec]
    args = [z, z, z, qg, kg]
    out_specs = [pl.BlockSpec((Q_TILE, ATTN_W), lambda b, i: (b * nq + i, 0))]
    out_shape = [jax.ShapeDtypeStruct((n_out_tok, ATTN_W), BF16)]
    t_keys = t_len
    if use_ctx:
        cos, sin_a, sin_b = rope
        cache_k, cache_v = cache
        past = cache_k.shape[2]
        t_keys = t_len + past
        tq_spec = pl.BlockSpec((Q_TILE, LANES), lambda b, i: (i, 0))
        tk_spec = pl.BlockSpec((t_len, LANES), lambda b, i: (0, 0))
        c_spec = pl.BlockSpec((1, 1, past, KV_W), lambda b, i: (b, layer, 0, 0))
        in_specs += [tq_spec, tq_spec, tq_spec, tk_spec, tk_spec, tk_spec, c_spec, c_spec]
        args += [cos, sin_a, sin_b, cos, sin_a, sin_b,
                 cache_k.reshape(cache_k.shape[0], cache_k.shape[1], past, KV_W),
                 cache_v.reshape(cache_v.shape[0], cache_v.shape[1], past, KV_W)]
    else:
        kv_spec = pl.BlockSpec((1, t_len, KV_W), lambda b, i: (b, 0, 0))
        out_specs += [kv_spec, kv_spec]
        out_shape += [jax.ShapeDtypeStruct((n_batch, t_len, KV_W), F32)] * 2
    return pl.pallas_call(
        functools.partial(_attn_kernel, t_len=t_len, use_ctx=use_ctx),
        grid=(n_batch, nq),
        in_specs=in_specs,
        out_specs=out_specs,
        out_shape=out_shape,
        scratch_shapes=[pltpu.VMEM((t_keys, KV_W), BF16), pltpu.VMEM((t_keys, KV_W), BF16)],
        compiler_params=_cparams(("arbitrary", "arbitrary"), 48),
        name="attention_ctx" if use_ctx else "attention",
    )(*args)


def _shift_rows(x, d):
    n = x.shape[0]
    row = lax.broadcasted_iota(jnp.int32, x.shape, 0)
    y = pltpu.roll(x, d % n, 0)
    if d > 0:
        return jnp.where(row >= d, y, 0.0)
    return jnp.where(row < n + d, y, 0.0)


def _linear_scan(a, u, reverse):
    n, w = a.shape
    m = n // ROW_BLOCK

    def shifted(x, d, pos, size, fill):
        if reverse:
            return jnp.where(pos < size - d, pltpu.roll(x, x.shape[0] - d, 0), fill)
        return jnp.where(pos >= d, pltpu.roll(x, d, 0), fill)

    def scan(a, u, size, pos):
        d = 1
        while d < size:
            u = a * shifted(u, d, pos, size, 0.0) + u
            a = a * shifted(a, d, pos, size, 1.0)
            d *= 2
        return a, u

    a, u = scan(a, u, ROW_BLOCK, lax.broadcasted_iota(jnp.int32, a.shape, 0) & (ROW_BLOCK - 1))
    a3 = a.reshape(m, ROW_BLOCK, w)
    u3 = u.reshape(m, ROW_BLOCK, w)
    edge = 0 if reverse else ROW_BLOCK - 1
    pos = lax.broadcasted_iota(jnp.int32, (m, w), 0)
    a_grp, u_grp = scan(a3[:, edge, :], u3[:, edge, :], m, pos)
    a_in = shifted(a_grp, 1, pos, m, 1.0)
    u_in = shifted(u_grp, 1, pos, m, 0.0)
    u_all = a3 * u_in[:, None, :] + u3
    a_all = a3 * a_in[:, None, :]
    return a_all.reshape(n, w), u_all.reshape(n, w)


def _rglru_kernel(*refs, use_ctx):
    if use_ctx:
        (x_ref, y_ref, cw_ref, cb_ref, wg_ref, bg_ref, lam_ref, h0_ref, o_ref) = refs
    else:
        (x_ref, y_ref, cw_ref, cb_ref, wg_ref, bg_ref, lam_ref, o_ref, last_ref) = refs
    x = x_ref[...]
    cw = cw_ref[0]
    xr = cb_ref[0] + _shift_rows(x, 1) * cw[0:1] + x * cw[1:2] \
        + _shift_rows(x, -1) * cw[2:3] + _shift_rows(x, -2) * cw[3:4]
    gates = _sigmoid(_mm(xr, wg_ref[0]) + bg_ref[0])
    n = x.shape[0]
    total = None
    lasts = []
    for d in range(2):
        r = gates[:, (2 * d) * D_RNN:(2 * d + 1) * D_RNN]
        i = gates[:, (2 * d + 1) * D_RNN:(2 * d + 2) * D_RNN]
        log_a = (-RG_C * _softplus(-lam_ref[0, d:d + 1, :])) * r
        a = jnp.exp(log_a)
        u = jnp.sqrt(1.0 - a * a) * (i * xr)
        a_cum, h = _linear_scan(a, u, reverse=(d == 1))
        if use_ctx:
            h = h + a_cum * h0_ref[0, 0, d:d + 1, :]
        else:
            lasts.append(h[0:1] if d == 1 else h[n - 1:n])
        total = h if total is None else total + h
    o_ref[...] = (jax.nn.gelu(y_ref[...]) * total).astype(BF16)
    if not use_ctx:
        last_ref[0] = jnp.concatenate(lasts, axis=0)


def _rglru(z, conv_w, conv_b, w_gate, b_gate, lam, layer, n_batch, t_len, row0, state=None):
    use_ctx = state is not None
    rb0 = row0 // t_len
    in_specs = [pl.BlockSpec((t_len, D_RNN), lambda b: (rb0 + b, Z_RG // D_RNN)),
                pl.BlockSpec((t_len, D_RNN), lambda b: (rb0 + b, Z_RG // D_RNN + 1)),
                pl.BlockSpec((1, 4, D_RNN), lambda b: (layer, 0, 0)),
                pl.BlockSpec((1, 1, D_RNN), lambda b: (layer, 0, 0)),
                pl.BlockSpec((1, D_RNN, 4 * D_RNN), lambda b: (layer, 0, 0)),
                pl.BlockSpec((1, 1, 4 * D_RNN), lambda b: (layer, 0, 0)),
                pl.BlockSpec((1, 2, D_RNN), lambda b: (layer, 0, 0))]
    args = [z, z, conv_w, conv_b.reshape(-1, 1, D_RNN), w_gate, b_gate, lam]
    out_specs = [pl.BlockSpec((t_len, D_RNN), lambda b: (b, 0))]
    out_shape = [jax.ShapeDtypeStruct((n_batch * t_len, D_RNN), BF16)]
    if use_ctx:
        in_specs.append(pl.BlockSpec((1, 1, 2, D_RNN), lambda b: (b, layer, 0, 0)))
        args.append(state)
    else:
        out_specs.append(pl.BlockSpec((1, 2, D_RNN), lambda b: (b, 0, 0)))
        out_shape.append(jax.ShapeDtypeStruct((n_batch, 2, D_RNN), F32))
    return pl.pallas_call(
        functools.partial(_rglru_kernel, use_ctx=use_ctx),
        grid=(n_batch,),
        in_specs=in_specs,
        out_specs=out_specs,
        out_shape=out_shape,
        compiler_params=_cparams(("arbitrary",), 56),
        name="rglru_ctx" if use_ctx else "rglru",
    )(*args)


def _cumsum_rows(g, reverse):
    n, w = g.shape
    m = n // ROW_BLOCK

    def scan(x, size, pos):
        d = 1
        while d < size:
            if reverse:
                x = x + jnp.where(pos < size - d, pltpu.roll(x, x.shape[0] - d, 0), 0.0)
            else:
                x = x + jnp.where(pos >= d, pltpu.roll(x, d, 0), 0.0)
            d *= 2
        return x

    g = scan(g, ROW_BLOCK, lax.broadcasted_iota(jnp.int32, g.shape, 0) & (ROW_BLOCK - 1))
    g3 = g.reshape(m, ROW_BLOCK, w)
    total = g3[:, 0, :] if reverse else g3[:, ROW_BLOCK - 1, :]
    before = scan(total, m, lax.broadcasted_iota(jnp.int32, total.shape, 0)) - total
    return (g3 + before[:, None, :]).reshape(n, w)


def _block_row_fn(gc, reverse):
    n, w = gc.shape
    g3 = gc.reshape(n // ROW_BLOCK, ROW_BLOCK, w)
    edge_row = 0 if reverse else ROW_BLOCK - 1
    edge = jnp.broadcast_to(g3[:, edge_row:edge_row + 1, :], g3.shape)
    sub = lax.broadcasted_iota(jnp.int32, g3.shape, 1)

    def at(blk):
        r = blk // 2 if reverse else blk // 2 - 1
        if blk > ROW_BLOCK:
            groups = blk // ROW_BLOCK
            e4 = edge.reshape(n // blk, groups, ROW_BLOCK, w)
            pick = r // ROW_BLOCK
            return jnp.broadcast_to(e4[:, pick:pick + 1], e4.shape).reshape(n, w)
        out = jnp.broadcast_to(g3[:, r:r + 1, :], g3.shape)
        for b0 in range(blk, ROW_BLOCK, blk):
            out = jnp.where(sub >= b0, jnp.broadcast_to(g3[:, b0 + r:b0 + r + 1, :], g3.shape), out)
        return out.reshape(n, w)

    return at


def _first_head(shape):
    lane = lax.broadcasted_iota(jnp.int32, shape, len(shape) - 1)
    return (lane & (LANES - 1)) < HG_DK


def _pair(x, p):
    return x[:, p * LANES:(p + 1) * LANES]


def _hgrn_gates(f_logit, lb):
    y = jnp.exp(-jnp.abs(f_logit))
    log_sig = jnp.minimum(f_logit, 0.0) - jnp.log(1.0 + y)
    a = jnp.log(lb)
    b = jnp.log1p(-lb) + log_sig
    log_f = jnp.maximum(a, b) + jnp.log(1.0 + jnp.exp(-jnp.abs(a - b)))
    k = (1.0 - lb) * (jnp.where(f_logit >= 0.0, y, 1.0) / (1.0 + y))
    return log_f * LOG2_E, k


def _hgrn_chunk_state(gc, k, v, reverse):
    n = gc.shape[0]
    g_tot = gc[0:1] if reverse else gc[n - 1:n]
    k_end = (k * jnp.exp2(jnp.minimum(g_tot - gc, 0.0))).astype(BF16)
    vb = v.astype(BF16)
    same_head = (_first_head((LANES, LANES))
                 == (lax.broadcasted_iota(jnp.int32, (LANES, LANES), 0) < HG_DK))
    ds = [jnp.where(same_head, _mm_tn(_pair(k_end, p), _pair(vb, p)), 0.0) for p in range(N_PAIRS)]
    return ds, jnp.exp2(g_tot)


def _decay_state(s, e_row):
    n = s.shape[0]
    eye = (lax.broadcasted_iota(jnp.int32, (n, n), 0) == lax.broadcasted_iota(jnp.int32, (n, n), 1))
    col = jnp.sum(jnp.where(eye, jnp.broadcast_to(e_row, (n, n)), 0.0), axis=1, keepdims=True)
    return s * col


def _hgrn_intra_scores(q, ks, gcs, txs):
    n = q.shape[0]
    row = lax.broadcasted_iota(jnp.int32, q.shape, 0)
    first = _first_head(q.shape)
    boundary = [_block_row_fn(gcs[0], False), _block_row_fn(gcs[1], True)]
    scores = [None] * HG_HEADS
    w = 1
    while w < n:
        blk = 2 * w
        second = (row & w) != 0
        e_f = jnp.exp2(-jnp.abs(gcs[0] - boundary[0](blk)))
        e_b = jnp.exp2(-jnp.abs(gcs[1] - boundary[1](blk)))
        q_f = jnp.where(second, q * e_f, 0.0).astype(BF16)
        q_b = jnp.where(second, 0.0, q * e_b).astype(BF16)
        k_f = jnp.where(second, 0.0, ks[0] * e_f)
        k_b = jnp.where(second, ks[1] * e_b, 0.0)
        k_heads = [(jnp.where(first, k_f, 0.0).astype(BF16), jnp.where(first, k_b, 0.0).astype(BF16)),
                   (jnp.where(first, 0.0, k_f).astype(BF16), jnp.where(first, 0.0, k_b).astype(BF16))]
        sel = (txs >= w) & (txs < blk)
        for p in range(N_PAIRS):
            qc = jnp.concatenate([_pair(q_f, p), _pair(q_b, p)], axis=1)
            for j in range(2):
                kc = jnp.concatenate([_pair(k_heads[j][0], p), _pair(k_heads[j][1], p)], axis=1)
                s = jnp.where(sel, _mm_nt(qc, kc), 0.0)
                h = 2 * p + j
                scores[h] = s if scores[h] is None else scores[h] + s
        w = blk
    return scores


def _hgrn_kernel(*refs, t_len, use_ctx):
    if use_ctx:
        (q_ref, ff_ref, fb_ref, i_ref, og_ref, lb_ref, ng_ref, s0_ref, o_ref, sin_ref, gk_ref) = refs
    else:
        (q_ref, ff_ref, fb_ref, i_ref, og_ref, lb_ref, ng_ref, o_ref, sl_ref) = refs
    c = HG_CHUNK
    n_chunks = t_len // c
    f_refs = (ff_ref, fb_ref)

    def load_dir(d, r0):
        log_f, k = _hgrn_gates(f_refs[d][pl.ds(r0, c), :], lb_ref[0, d:d + 1, :])
        return _cumsum_rows(log_f, reverse=(d == 1)), k

    if use_ctx:
        zero = jnp.zeros((HG_DK, HG_DV), F32)
        for d in range(2):
            states = []
            for p in range(N_PAIRS):
                top = jnp.concatenate([s0_ref[0, 0, d, 2 * p], zero], axis=1)
                bot = jnp.concatenate([zero, s0_ref[0, 0, d, 2 * p + 1]], axis=1)
                states.append(jnp.concatenate([top, bot], axis=0))
            order = range(n_chunks) if d == 0 else range(n_chunks - 1, -1, -1)
            for ci in order:
                for p in range(N_PAIRS):
                    sin_ref[ci, p, d * LANES:(d + 1) * LANES, :] = states[p]
                gc, k = load_dir(d, ci * c)
                gk_ref[d, 0, pl.ds(ci * c, c), :] = gc
                gk_ref[d, 1, pl.ds(ci * c, c), :] = k
                ds, e_tot = _hgrn_chunk_state(gc, k, i_ref[pl.ds(ci * c, c), :], reverse=(d == 1))
                states = [_decay_state(states[p], _pair(e_tot, p)) + ds[p] for p in range(N_PAIRS)]

    ti = lax.broadcasted_iota(jnp.int32, (c, c), 0)
    si = lax.broadcasted_iota(jnp.int32, (c, c), 1)
    txs = ti ^ si
    diag = ti == si

    def chunk_body(ci, carry):
        r0 = pl.multiple_of(ci * c, c)
        hq = q_ref[pl.ds(r0, c), :]
        q = hq * _sigmoid(hq)
        v = i_ref[pl.ds(r0, c), :]
        gcs, ks = [], []
        for d in range(2):
            if use_ctx:
                gc, k = gk_ref[d, 0, pl.ds(r0, c), :], gk_ref[d, 1, pl.ds(r0, c), :]
            else:
                gc, k = load_dir(d, r0)
            gcs.append(gc)
            ks.append(k)
        scores = _hgrn_intra_scores(q, ks, gcs, txs)
        first = _first_head(q.shape)
        k_sum = ks[0] + ks[1]
        v_heads = (jnp.where(first, v, 0.0).astype(BF16), jnp.where(first, 0.0, v).astype(BF16))
        k_heads = (jnp.where(first, k_sum, 0.0).astype(BF16), jnp.where(first, 0.0, k_sum).astype(BF16))
        q_bf = q.astype(BF16)
        if use_ctx:
            qe = [(q * jnp.exp2(gcs[d])).astype(BF16) for d in range(2)]
        else:
            for d in range(2):
                ds, _ = _hgrn_chunk_state(gcs[d], ks[d], v, reverse=(d == 1))
                for p in range(N_PAIRS):
                    sl_ref[0, d, 2 * p] = ds[p][:HG_DK, :HG_DV]
                    sl_ref[0, d, 2 * p + 1] = ds[p][HG_DK:, HG_DV:]
        outs = []
        for p in range(N_PAIRS):
            o = None
            for j in range(2):
                s_diag = jnp.where(diag, _mm_nt(_pair(q_bf, p), _pair(k_heads[j], p)), 0.0)
                t = _mm(scores[2 * p + j] + s_diag, _pair(v_heads[j], p))
                o = t if o is None else o + t
            if use_ctx:
                o = o + _mm(jnp.concatenate([_pair(qe[0], p), _pair(qe[1], p)], axis=1), sin_ref[ci, p])
            outs.append(_head_rms(o, ng_ref[0]))
        og = og_ref[pl.ds(r0, c), :]
        o_all = jnp.concatenate(outs, axis=1) * (og * _sigmoid(og))
        o_ref[pl.ds(r0, c), :] = o_all.astype(BF16)
        return carry

    if n_chunks == 1:
        chunk_body(0, 0)
    else:
        lax.fori_loop(0, n_chunks, chunk_body, 0)


def _hgrn(z, lb, norm_g, layer, n_batch, t_len, row0, state=None):
    use_ctx = state is not None
    rb0 = row0 // t_len
    cb = Z_HG // HG_W

    def zspec(k):
        return pl.BlockSpec((t_len, HG_W), lambda b: (rb0 + b, cb + k))

    in_specs = [zspec(0), zspec(1), zspec(2), zspec(3), zspec(4),
                pl.BlockSpec((1, 2, HG_W), lambda b: (layer, 0, 0)),
                pl.BlockSpec((1, 1, LANES), lambda b: (layer, 0, 0))]
    args = [z, z, z, z, z, lb, jnp.tile(norm_g, (1, LANES // HG_DV)).reshape(-1, 1, LANES)]
    out_specs = [pl.BlockSpec((t_len, HG_W), lambda b: (b, 0))]
    out_shape = [jax.ShapeDtypeStruct((n_batch * t_len, HG_W), BF16)]
    scratch = []
    if use_ctx:
        in_specs.append(pl.BlockSpec((1, 1, 2, HG_HEADS, HG_DK, HG_DV), lambda b: (b, layer, 0, 0, 0, 0)))
        args.append(state)
        scratch.append(pltpu.VMEM((t_len // HG_CHUNK, N_PAIRS, 2 * LANES, LANES), F32))
        scratch.append(pltpu.VMEM((2, 2, t_len, HG_W), F32))
    else:
        out_specs.append(pl.BlockSpec((1, 2, HG_HEADS, HG_DK, HG_DV), lambda b: (b, 0, 0, 0, 0)))
        out_shape.append(jax.ShapeDtypeStruct((n_batch, 2, HG_HEADS, HG_DK, HG_DV), F32))
    return pl.pallas_call(
        functools.partial(_hgrn_kernel, t_len=t_len, use_ctx=use_ctx),
        grid=(n_batch,),
        in_specs=in_specs,
        out_specs=out_specs,
        out_shape=out_shape,
        scratch_shapes=scratch,
        compiler_params=_cparams(("arbitrary",), 56),
        name="hgrn_ctx" if use_ctx else "hgrn",
    )(*args)


def _merge_kernel(a_ref, r_ref, h_ref, g0_ref, g1_ref, g2_ref, x_ref, mod_ref,
                  wa_ref, wr_ref, wh_ref, wo_ref, o_ref, wa_s, wr_s, wh_s, wo_s):
    @pl.when(pl.program_id(0) == 0)
    def _():
        wa_s[...] = wa_ref[0].astype(BF16)
        wr_s[...] = wr_ref[0].astype(BF16)
        wh_s[...] = wh_ref[0].astype(BF16)
        wo_s[...] = wo_ref[0].astype(BF16)

    def proj(x_r, w_s):
        return jnp.dot(x_r[...], w_s[...], preferred_element_type=F32)

    merged = (_sigmoid(g0_ref[...]) * proj(a_ref, wa_s)
              + _sigmoid(g1_ref[...]) * proj(r_ref, wr_s)
              + _sigmoid(g2_ref[...]) * proj(h_ref, wh_s))
    out = jnp.dot(merged.astype(BF16), wo_s[...], preferred_element_type=F32)
    o_ref[...] = x_ref[...] + mod_ref[0, 2:3, :] * out


def _merge_project(attn, rnn, hg, z, x, modseg, w_attn_o, w_rnn_o, w_hgrn_o, w_out, layer):
    n_tok = x.shape[0]
    tm = 512
    per_seg = SEG_ROWS // tm
    half = pl.BlockSpec((tm, ATTN_W), lambda i: (i, 0))
    full = pl.BlockSpec((tm, D_MODEL), lambda i: (i, 0))

    def gspec(k):
        return pl.BlockSpec((tm, D_MODEL), lambda i: (i, Z_GL // D_MODEL + k))

    def wspec(rows):
        return pl.BlockSpec((1, rows, D_MODEL), lambda i: (layer, 0, 0))

    return pl.pallas_call(
        _merge_kernel,
        grid=(n_tok // tm,),
        in_specs=[half, half, half, gspec(0), gspec(1), gspec(2), full,
                  pl.BlockSpec((1, N_MOD, D_MODEL), lambda i: (i // per_seg, 0, 0)),
                  wspec(ATTN_W), wspec(D_RNN), wspec(HG_W), wspec(D_MODEL)],
        out_specs=full,
        out_shape=jax.ShapeDtypeStruct((n_tok, D_MODEL), F32),
        scratch_shapes=[pltpu.VMEM((ATTN_W, D_MODEL), BF16), pltpu.VMEM((D_RNN, D_MODEL), BF16),
                        pltpu.VMEM((HG_W, D_MODEL), BF16), pltpu.VMEM((D_MODEL, D_MODEL), BF16)],
        compiler_params=_cparams(("arbitrary",), 56),
        name="merge_project",
    )(attn, rnn, hg, z, z, z, x, modseg, w_attn_o, w_rnn_o, w_hgrn_o, w_out)


def _router_kernel(x_ref, g_ref, mod_ref, wr_ref, br_ref, h_ref, slot_ref, wt_ref, meta_ref, run_ref):
    @pl.when(pl.program_id(0) == 0)
    def _():
        run_ref[...] = jnp.zeros_like(run_ref)

    h = _modulated_norm(x_ref[...], g_ref[0], mod_ref[0, 4:5, :], mod_ref[0, 3:4, :])
    h_ref[...] = h
    logits = jnp.dot(h, wr_ref[0], preferred_element_type=F32,
                     precision=lax.Precision.HIGHEST) + br_ref[0]
    tm = logits.shape[0]
    lane = lax.broadcasted_iota(jnp.int32, logits.shape, 1)
    ids, vals = [], []
    chosen = jnp.zeros(logits.shape, jnp.bool_)
    work = logits
    for _ in range(TOP_K):
        m = jnp.max(work, axis=-1, keepdims=True)
        idx = jnp.min(jnp.where(work == m, lane, LANES), axis=-1, keepdims=True)
        hit = lane == idx
        chosen = chosen | hit
        work = jnp.where(hit, -jnp.inf, work)
        ids.append(idx)
        vals.append(m)
    exps = [jnp.exp(v - vals[0]) for v in vals]
    den = exps[0] + exps[1] + exps[2] + exps[3]
    ind = jnp.where(chosen, 1.0, 0.0)
    r_i = lax.broadcasted_iota(jnp.int32, (tm, tm), 0)
    c_i = lax.broadcasted_iota(jnp.int32, (tm, tm), 1)
    count = jnp.sum(ind, axis=0, keepdims=True)
    padded = jnp.floor((count + (ROW_BLOCK - 1)) * (1.0 / ROW_BLOCK)) * ROW_BLOCK
    incl = jnp.broadcast_to(padded, (8, LANES))
    lane8 = lax.broadcasted_iota(jnp.int32, (8, LANES), 1)
    d = 1
    while d < LANES:
        incl = incl + jnp.where(lane8 >= d, pltpu.roll(incl, d, 1), 0.0)
        d *= 2
    local_start = incl[0:1] - padded
    place = _mm(jnp.where(c_i < r_i, 1.0, 0.0), ind) + local_start
    slot_out = jnp.zeros(logits.shape, jnp.int32)
    wt_out = jnp.zeros(logits.shape, F32)
    for k in range(TOP_K):
        slot = jnp.sum(jnp.where(lane == ids[k], place, 0.0), axis=-1, keepdims=True)
        slot_out = jnp.where(lane == k, slot.astype(jnp.int32), slot_out)
        wt_out = jnp.where(lane == k, exps[k] / den, wt_out)
    slot_ref[...] = slot_out
    wt_ref[...] = wt_out
    row8 = lax.broadcasted_iota(jnp.int32, (8, LANES), 0)
    table = jnp.where(row8 == 0, local_start, jnp.where(row8 == 1, padded, run_ref[0:1, :]))
    meta_ref[0] = jnp.where(row8 < 3, table, 0.0).astype(jnp.int32)
    run_ref[...] = run_ref[...] + padded


def _router(x, norm_g, modseg, w_router, b_router, layer):
    n_tok = x.shape[0]
    tm = TOKEN_TILE
    per_seg = SEG_ROWS // tm
    depth = w_router.shape[0]
    w_pad = jnp.zeros((depth, D_MODEL, LANES), F32).at[:, :, :N_EXPERTS].set(w_router)
    b_pad = jnp.full((depth, 1, LANES), NEG_BIG, F32).at[:, 0, :N_EXPERTS].set(b_router)
    full = pl.BlockSpec((tm, D_MODEL), lambda i: (i, 0))
    small = pl.BlockSpec((tm, LANES), lambda i: (i, 0))
    return pl.pallas_call(
        _router_kernel,
        grid=(n_tok // tm,),
        in_specs=[full,
                  pl.BlockSpec((1, 1, D_MODEL), lambda i: (layer, 0, 0)),
                  pl.BlockSpec((1, N_MOD, D_MODEL), lambda i: (i // per_seg, 0, 0)),
                  pl.BlockSpec((1, D_MODEL, LANES), lambda i: (layer, 0, 0)),
                  pl.BlockSpec((1, 1, LANES), lambda i: (layer, 0, 0))],
        out_specs=[full, small, small, pl.BlockSpec((1, 8, LANES), lambda i: (i, 0, 0))],
        out_shape=[jax.ShapeDtypeStruct((n_tok, D_MODEL), F32),
                   jax.ShapeDtypeStruct((n_tok, LANES), jnp.int32),
                   jax.ShapeDtypeStruct((n_tok, LANES), F32),
                   jax.ShapeDtypeStruct((n_tok // tm, 8, LANES), jnp.int32)],
        scratch_shapes=[pltpu.VMEM((8, LANES), F32)],
        compiler_params=_cparams(("arbitrary",), 40),
        name="router",
    )(x, norm_g.reshape(-1, 1, D_MODEL), modseg, w_pad, b_pad)


MAX_BIG = LOCAL_ROWS // BIG_BLOCK
MAX_SMALL = N_EXPERTS * (BIG_BLOCK // ROW_BLOCK - 1)
COPY_LIST_LEN = 1024
_BIG_AT = 2
_SMALL_AT = 2 + 2 * MAX_BIG
assert _SMALL_AT + 2 * MAX_SMALL <= COPY_LIST_LEN


def _copy_lists(local_start, padded, sorted_start):
    n_big = padded // BIG_BLOCK
    n_small = (padded - n_big * BIG_BLOCK) // ROW_BLOCK

    def expand(count, local0, sorted0, step, cap):
        cum = jnp.cumsum(count, axis=1)
        f = jnp.arange(cap, dtype=jnp.int32)
        group = jnp.minimum(jnp.sum((cum[:, None, :] <= f[None, :, None]).astype(jnp.int32), axis=2),
                            N_EXPERTS - 1)
        j = f[None, :] - jnp.take_along_axis(cum - count, group, axis=1)
        return (cum[:, -1:], jnp.take_along_axis(local0, group, axis=1) + step * j,
                jnp.take_along_axis(sorted0, group, axis=1) + step * j)

    cnt_b, loc_b, srt_b = expand(n_big, local_start, sorted_start, BIG_BLOCK, MAX_BIG)
    done = n_big * BIG_BLOCK
    cnt_s, loc_s, srt_s = expand(n_small, local_start + done, sorted_start + done, ROW_BLOCK, MAX_SMALL)
    parts = [cnt_b, cnt_s, loc_b, srt_b, loc_s, srt_s]
    used = sum(p.shape[1] for p in parts)
    parts.append(jnp.zeros((padded.shape[0], COPY_LIST_LEN - used), jnp.int32))
    return jnp.concatenate(parts, axis=1).astype(jnp.int32).reshape(-1)


def _for_each_block(list_ref, fn):
    def big(j, c):
        fn(pl.multiple_of(list_ref[_BIG_AT + j], ROW_BLOCK),
           pl.multiple_of(list_ref[_BIG_AT + MAX_BIG + j], ROW_BLOCK), BIG_BLOCK)
        return c

    def small(j, c):
        fn(pl.multiple_of(list_ref[_SMALL_AT + j], ROW_BLOCK),
           pl.multiple_of(list_ref[_SMALL_AT + MAX_SMALL + j], ROW_BLOCK), ROW_BLOCK)
        return c

    lax.fori_loop(0, list_ref[0], big, 0)
    lax.fori_loop(0, list_ref[1], small, 0)


def _slot_matrix(slot_ref, values):
    col = lax.broadcasted_iota(jnp.int32, (TOKEN_TILE, LOCAL_ROWS), 1)
    out = jnp.zeros((TOKEN_TILE, LOCAL_ROWS), F32)
    for k in range(TOP_K):
        out = jnp.where(col == slot_ref[:, k:k + 1], values[k], out)
    return out.astype(BF16)


def _dispatch_kernel(meta_ref, prev_meta_ref, tail_ref, slot_ref, h_ref, xs_ref, local_ref, zero_ref,
                     sem, block_sem):
    i = pl.program_id(0)
    cur = i % 2

    @pl.when(i == 0)
    def _():
        zero_ref[...] = jnp.zeros_like(zero_ref)

        def tails(fn):
            def group(e, carry):
                def block(j, c):
                    fn(pl.multiple_of(tail_ref[e] + j * ROW_BLOCK, ROW_BLOCK))
                    return c
                return lax.fori_loop(0, tail_ref[N_EXPERTS + e], block, carry)
            lax.fori_loop(0, N_EXPERTS, group, 0)

        def zero_copy(row):
            return pltpu.make_async_copy(zero_ref.at[pl.ds(0, ROW_BLOCK)],
                                         xs_ref.at[pl.ds(row, ROW_BLOCK)], sem)

        tails(lambda row: zero_copy(row).start())
        tails(lambda row: zero_copy(row).wait())

        def unused(fn):
            def tile(j, c):
                fn(pl.multiple_of(tail_ref[2 * N_EXPERTS] + j * EXPERT_TILE, EXPERT_TILE))
                return c
            lax.fori_loop(0, tail_ref[2 * N_EXPERTS + 1], tile, 0)

        def zero_tile(row):
            return pltpu.make_async_copy(zero_ref, xs_ref.at[pl.ds(row, EXPERT_TILE)], sem)

        unused(lambda row: zero_tile(row).start())
        unused(lambda row: zero_tile(row).wait())

    onehot = _slot_matrix(slot_ref, [1.0] * TOP_K)
    local_ref[cur] = _mm_tn(onehot, h_ref[...])

    def block_copy(buf):
        return lambda local_row, sorted_row, rows: pltpu.make_async_copy(
            local_ref.at[buf, pl.ds(local_row, rows)], xs_ref.at[pl.ds(sorted_row, rows)],
            block_sem.at[buf])

    _for_each_block(meta_ref, lambda a, b, n: block_copy(cur)(a, b, n).start())

    @pl.when(i > 0)
    def _():
        _for_each_block(prev_meta_ref, lambda a, b, n: block_copy(1 - cur)(a, b, n).wait())

    @pl.when(i == pl.num_programs(0) - 1)
    def _():
        _for_each_block(meta_ref, lambda a, b, n: block_copy(cur)(a, b, n).wait())


def _dispatch(h, slots, meta, tails, n_rows):
    n_tok = h.shape[0]
    return pl.pallas_call(
        _dispatch_kernel,
        grid=(n_tok // TOKEN_TILE,),
        in_specs=[pl.BlockSpec((COPY_LIST_LEN,), lambda i: (i,), memory_space=pltpu.SMEM),
                  pl.BlockSpec((COPY_LIST_LEN,), lambda i: (jnp.maximum(i - 1, 0),), memory_space=pltpu.SMEM),
                  pl.BlockSpec((LANES,), lambda i: (0,), memory_space=pltpu.SMEM),
                  pl.BlockSpec((TOKEN_TILE, LANES), lambda i: (i, 0)),
                  pl.BlockSpec((TOKEN_TILE, D_MODEL), lambda i: (i, 0))],
        out_specs=pl.BlockSpec(memory_space=pl.ANY),
        out_shape=jax.ShapeDtypeStruct((n_rows, D_MODEL), F32),
        scratch_shapes=[pltpu.VMEM((2, LOCAL_ROWS, D_MODEL), F32), pltpu.VMEM((EXPERT_TILE, D_MODEL), F32),
                        pltpu.SemaphoreType.DMA(()), pltpu.SemaphoreType.DMA((2,))],
        compiler_params=_cparams(("arbitrary",), 48),
        name="dispatch",
    )(meta, meta, tails, slots, h)


def _combine_kernel(meta_ref, next_meta_ref, slot_ref, wt_ref, ys_ref, x_ref, mod_ref, o_ref, local_ref, sem):
    i = pl.program_id(0)
    cur = i % 2

    def block_copy(buf):
        return lambda local_row, sorted_row, rows: pltpu.make_async_copy(
            ys_ref.at[pl.ds(sorted_row, rows)], local_ref.at[buf, pl.ds(local_row, rows)], sem.at[buf])

    def start_gather(table_ref, buf):
        local_ref[buf, TOKEN_TILE * TOP_K:, :] = jnp.zeros((LOCAL_ROWS - TOKEN_TILE * TOP_K, D_MODEL), F32)
        _for_each_block(table_ref, lambda a, b, n: block_copy(buf)(a, b, n).start())

    @pl.when(i == 0)
    def _():
        start_gather(meta_ref, 0)

    @pl.when(i + 1 < pl.num_programs(0))
    def _():
        start_gather(next_meta_ref, 1 - cur)

    _for_each_block(meta_ref, lambda a, b, n: block_copy(cur)(a, b, n).wait())
    weights = _slot_matrix(slot_ref, [wt_ref[:, k:k + 1] for k in range(TOP_K)])
    acc = jnp.dot(weights, local_ref[cur].astype(BF16), preferred_element_type=F32)
    o_ref[...] = x_ref[...] + mod_ref[0, 5:6, :] * acc


def _combine(ys, slots, meta, wts, x, modseg):
    n_tok = x.shape[0]
    per_seg = SEG_ROWS // TOKEN_TILE
    full = pl.BlockSpec((TOKEN_TILE, D_MODEL), lambda i: (i, 0))
    small = pl.BlockSpec((TOKEN_TILE, LANES), lambda i: (i, 0))
    return pl.pallas_call(
        _combine_kernel,
        grid=(n_tok // TOKEN_TILE,),
        in_specs=[pl.BlockSpec((COPY_LIST_LEN,), lambda i: (i,), memory_space=pltpu.SMEM),
                  pl.BlockSpec((COPY_LIST_LEN,), lambda i: (jnp.minimum(i + 1, n_tok // TOKEN_TILE - 1),),
                               memory_space=pltpu.SMEM),
                  small, small,
                  pl.BlockSpec(memory_space=pl.ANY),
                  full,
                  pl.BlockSpec((1, N_MOD, D_MODEL), lambda i: (i // per_seg, 0, 0))],
        out_specs=full,
        out_shape=jax.ShapeDtypeStruct((n_tok, D_MODEL), F32),
        scratch_shapes=[pltpu.VMEM((2, LOCAL_ROWS, D_MODEL), F32), pltpu.SemaphoreType.DMA((2,))],
        compiler_params=_cparams(("arbitrary",), 56),
        name="combine",
    )(meta, meta, slots, wts, ys, x, modseg)


def _expert_kernel(te_ref, first_ref, slot_ref, next_ref, na_ref, x_ref, wgu_hbm, bgu_ref, wd_hbm, bd_ref,
                   o_ref, wgu_f, wd_f, wgu_s, wd_s, sem, *, layer):
    i = pl.program_id(0)
    active = i < na_ref[0]

    def fetch(e, s):
        return (pltpu.make_async_copy(wgu_hbm.at[layer, e], wgu_f.at[s], sem.at[0, s]),
                pltpu.make_async_copy(wd_hbm.at[layer, e], wd_f.at[s], sem.at[1, s]))

    @pl.when(i == 0)
    def _():
        for cp in fetch(te_ref[0], 0):
            cp.start()

    @pl.when(first_ref[i] == 1)
    def _():
        s = slot_ref[i]
        for cp in fetch(te_ref[i], s):
            cp.wait()
        wgu_s[...] = wgu_f[s].astype(BF16)
        wd_s[...] = wd_f[s].astype(BF16)

        @pl.when(next_ref[i] >= 0)
        def _():
            for cp in fetch(next_ref[i], 1 - s):
                cp.start()

    @pl.when(active)
    def _():
        gu = jnp.dot(x_ref[...].astype(BF16), wgu_s[...], preferred_element_type=F32) + bgu_ref[0, 0]
        gate = jnp.minimum(gu[:, :D_FF], SWIGLU_LIMIT)
        up = jnp.clip(gu[:, D_FF:], -SWIGLU_LIMIT, SWIGLU_LIMIT)
        act = gate * _sigmoid(SWIGLU_ALPHA * gate) * (up + 1.0)
        o_ref[...] = jnp.dot(act.astype(BF16), wd_s[...], preferred_element_type=F32) + bd_ref[0, 0]

    @pl.when(jnp.logical_not(active))
    def _():
        o_ref[...] = jnp.zeros_like(o_ref)


def _experts(xs, tile_expert, ends, n_active, w_gu, b_gu, w_down, b_down, layer):
    n_rows = xs.shape[0]
    depth = w_gu.shape[0]
    tm = EXPERT_TILE
    n_tiles = n_rows // tm
    tile = jnp.arange(n_tiles, dtype=jnp.int32)
    active = tile < n_active[0]
    first = (active & ((tile == 0) | (tile_expert != jnp.roll(tile_expert, 1)))).astype(jnp.int32)
    slot = (jnp.cumsum(first) - 1) % 2
    next_tile = ends[tile_expert] // tm
    next_expert = jnp.where(next_tile < n_active[0],
                            tile_expert[jnp.minimum(next_tile, n_tiles - 1)], -1).astype(jnp.int32)

    def tiled(shape, index):
        return pl.BlockSpec(shape, lambda i, te, fi, sl, nx, na: index(i, te, na))

    grid_spec = pltpu.PrefetchScalarGridSpec(
        num_scalar_prefetch=5,
        grid=(n_tiles,),
        in_specs=[tiled((tm, D_MODEL), lambda i, te, na: (jnp.where(i < na[0], i, 0), 0)),
                  pl.BlockSpec(memory_space=pl.ANY),
                  tiled((1, 1, 1, 2 * D_FF), lambda i, te, na: (layer, te[i], 0, 0)),
                  pl.BlockSpec(memory_space=pl.ANY),
                  tiled((1, 1, 1, D_MODEL), lambda i, te, na: (layer, te[i], 0, 0))],
        out_specs=tiled((tm, D_MODEL), lambda i, te, na: (i, 0)),
        scratch_shapes=[pltpu.VMEM((2, D_MODEL, 2 * D_FF), F32), pltpu.VMEM((2, D_FF, D_MODEL), F32),
                        pltpu.VMEM((D_MODEL, 2 * D_FF), BF16), pltpu.VMEM((D_FF, D_MODEL), BF16),
                        pltpu.SemaphoreType.DMA((2, 2))],
    )
    return pl.pallas_call(
        functools.partial(_expert_kernel, layer=layer),
        grid_spec=grid_spec,
        out_shape=jax.ShapeDtypeStruct((n_rows, D_MODEL), F32),
        compiler_params=_cparams(("arbitrary",), 56),
        name="experts",
    )(tile_expert, first, slot.astype(jnp.int32), next_expert, n_active, xs, w_gu,
      b_gu.reshape(depth, N_EXPERTS, 1, 2 * D_FF), w_down, b_down.reshape(depth, N_EXPERTS, 1, D_MODEL))


def _moe(x, norm_g, modseg, w_router, b_router, w_gu, b_gu, w_down, b_down, layer):
    n_tok = x.shape[0]
    h, slots, wts, table = _router(x, norm_g, modseg, w_router, b_router, layer)
    tm = EXPERT_TILE
    n_tiles = n_tok // TOKEN_TILE
    n_rows = n_tok * TOP_K + n_tiles * N_EXPERTS * ROW_BLOCK + N_EXPERTS * tm
    n_rows = -(-n_rows // tm) * tm
    local_start = table[:, 0, :N_EXPERTS]
    padded = table[:, 1, :N_EXPERTS]
    earlier = table[:, 2, :N_EXPERTS]
    used = earlier[-1] + padded[-1]
    size = ((used + tm - 1) // tm) * tm
    ends = jnp.cumsum(size)
    starts = ends - size
    sorted_start = starts[None, :] + earlier
    meta = _copy_lists(local_start, padded, sorted_start)
    tails = jnp.concatenate([starts + used, (size - used) // ROW_BLOCK,
                             ends[-1:], (n_rows - ends[-1:]) // tm,
                             jnp.zeros((LANES - 2 * N_EXPERTS - 2,), jnp.int32)])
    tile_row0 = jnp.arange(n_rows // tm, dtype=jnp.int32) * tm
    tile_expert = jnp.minimum(jnp.sum((ends[None, :] <= tile_row0[:, None]).astype(jnp.int32), axis=1),
                              N_EXPERTS - 1)
    n_active = (ends[-1:] // tm).astype(jnp.int32)
    xs = _dispatch(h, slots, meta, tails, n_rows)
    ys = _experts(xs, tile_expert, ends, n_active, w_gu, b_gu, w_down, b_down, layer)
    return _combine(ys, slots, meta, wts, x, modseg)


def _rope_tables(n_tok):
    rows = n_tok // GRID_W
    row = jnp.repeat(jnp.arange(rows, dtype=F32), GRID_W)
    col = jnp.tile(jnp.arange(GRID_W, dtype=F32), rows)
    quarter = HEAD_DIM // 4
    inv_freq = ROPE_BASE ** (-jnp.arange(quarter, dtype=F32) / quarter)
    ang_r = row[:, None] * inv_freq
    ang_c = col[:, None] * inv_freq
    ang = jnp.concatenate([ang_r, ang_r, ang_c, ang_c], axis=-1)
    cos, sin = jnp.cos(ang), jnp.sin(ang)
    first = (jnp.arange(HEAD_DIM) % (2 * quarter)) < quarter
    sin_a = jnp.where(first, -sin, 0.0)
    sin_b = jnp.where(first, 0.0, sin)
    return tuple(jnp.tile(t, (1, LANES // HEAD_DIM)) for t in (cos, sin_a, sin_b))


def _block_diag(w):
    eye = jnp.eye(RG_BLOCKS, dtype=w.dtype)
    return jnp.einsum("nde,nm->ndme", w, eye).reshape(D_RNN, D_RNN)


def kernel(x_prompt, x_sample, c, cache_k, cache_v, state_rglru, state_hgrn, c_ctx, w_mod, b_mod, norm1_g, norm2_g, w_in, q_norm_g, k_norm_g, conv_w, conv_b, rg_wa, rg_ba, rg_wx, rg_bx, rg_lambda, hgrn_lb_logits, hgrn_norm_g, w_attn_o, w_rnn_o, w_hgrn_o, w_out, w_router, b_router, w_gu, b_gu, w_down, b_down):
    depth = w_mod.shape[0]
    n_p, t_p = x_prompt.shape[0], x_prompt.shape[1]
    n_s, t_s = x_sample.shape[0], x_sample.shape[1]
    tok_p = n_p * t_p
    tok_s = n_s * t_s
    assert tok_p % SEG_ROWS == 0 and t_s == SEG_ROWS and SEG_ROWS % t_p == 0

    x = jnp.concatenate([x_prompt.reshape(tok_p, D_MODEL), x_sample.reshape(tok_s, D_MODEL)], axis=0)

    lb_all = jnp.cumsum(jax.nn.softmax(hgrn_lb_logits.astype(F32), axis=0), axis=0)
    lb_all = lb_all - lb_all[:1]
    rope = _rope_tables(t_s)

    cond8 = jnp.zeros((8, D_MODEL), F32).at[0].set(c_ctx).at[1:1 + n_s].set(c)
    mod = _modulation(cond8, w_mod, b_mod)
    seg_ids = jnp.asarray([0] * (tok_p // SEG_ROWS) + [1 + b for b in range(n_s)], jnp.int32)
    modsegs = mod[:, seg_ids].reshape(depth, seg_ids.shape[0], N_MOD, D_MODEL)

    w_gate = jnp.stack([jnp.concatenate([_block_diag(rg_wa[l, 0]), _block_diag(rg_wx[l, 0]),
                                         _block_diag(rg_wa[l, 1]), _block_diag(rg_wx[l, 1])], axis=1)
                        for l in range(depth)])
    b_gate = jnp.stack([jnp.concatenate([rg_ba[l, 0], rg_bx[l, 0], rg_ba[l, 1], rg_bx[l, 1]])[None]
                        for l in range(depth)])

    new_k, new_v, new_hr, new_s = [], [], [], []
    for l in range(depth):
        modseg = modsegs[l]
        z = _input_projection(x, norm1_g, modseg, w_in, l)
        attn_p, k_l, v_l = _attention(z, q_norm_g, k_norm_g, l, n_p, t_p, 0)
        (attn_s,) = _attention(z, q_norm_g, k_norm_g, l, n_s, t_s, tok_p, rope=rope,
                               cache=(cache_k, cache_v))
        rnn_p, hr_l = _rglru(z, conv_w, conv_b, w_gate, b_gate, rg_lambda, l, n_p, t_p, 0)
        (rnn_s,) = _rglru(z, conv_w, conv_b, w_gate, b_gate, rg_lambda, l, n_s, t_s, tok_p,
                          state=state_rglru)
        hg_p, s_l = _hgrn(z, lb_all, hgrn_norm_g, l, n_p, t_p, 0)
        (hg_s,) = _hgrn(z, lb_all, hgrn_norm_g, l, n_s, t_s, tok_p, state=state_hgrn)
        attn = jnp.concatenate([attn_p, attn_s], axis=0)
        rnn = jnp.concatenate([rnn_p, rnn_s], axis=0)
        hg = jnp.concatenate([hg_p, hg_s], axis=0)
        x = _merge_project(attn, rnn, hg, z, x, modseg, w_attn_o, w_rnn_o, w_hgrn_o, w_out, l)
        x = _moe(x, norm2_g, modseg, w_router, b_router, w_gu, b_gu, w_down, b_down, l)
        new_k.append(k_l.reshape(n_p, t_p, N_KV_HEADS, HEAD_DIM))
        new_v.append(v_l.reshape(n_p, t_p, N_KV_HEADS, HEAD_DIM))
        new_hr.append(hr_l)
        new_s.append(s_l)

    y_prompt = x[:tok_p].reshape(n_p, t_p, D_MODEL)
    y_sample = x[tok_p:].reshape(n_s, t_s, D_MODEL)
    return (y_prompt, y_sample, jnp.stack(new_k, axis=1), jnp.stack(new_v, axis=1),
            jnp.stack(new_hr, axis=1), jnp.stack(new_s, axis=1))
```

```python
import functools

import jax
import jax.numpy as jnp
from jax import lax
from jax.experimental import pallas as pl
from jax.experimental.pallas import tpu as pltpu

F32 = jnp.float32
BF16 = jnp.bfloat16

D_MODEL = 1024
GRID_W = 64
HEAD_DIM = 64
N_HEADS = 8
N_KV_HEADS = 2
KV_GROUP = N_HEADS // N_KV_HEADS
ATTN_W = N_HEADS * HEAD_DIM
KV_W = N_KV_HEADS * HEAD_DIM
ROPE_BASE = 10000.0
D_RNN = 512
RG_BLOCKS = 8
RG_BW = D_RNN // RG_BLOCKS
RG_C = 8.0
HG_HEADS = 8
HG_DK = 64
HG_DV = 64
HG_W = HG_HEADS * HG_DK
N_PAIRS = HG_W // 128
N_EXPERTS = 32
TOP_K = 4
D_FF = D_MODEL
SWIGLU_LIMIT = 7.0
SWIGLU_ALPHA = 1.702
N_MOD = 6
EPS = 1e-6
IN_W = ATTN_W + 2 * KV_W + 2 * D_RNN + 5 * HG_W + 3 * D_MODEL

LANES = 128
SEG_ROWS = 1024
COL_TILE = 256
HG_CHUNK = 256
Q_TILE = 256
EXPERT_TILE = 384
BIG_BLOCK = 64
TOKEN_TILE = 512
ROW_BLOCK = 8
LOCAL_ROWS = TOKEN_TILE * TOP_K + N_EXPERTS * ROW_BLOCK
NEG_BIG = -1e30
LOG2_E = 1.4426950408889634

_SRC_TILES = tuple(range(17, 29)) + tuple(range(7, 17)) + tuple(range(3, 7)) + (0, 1, 2)
Z_GL = 0
Z_HG = 3 * D_MODEL
Z_RG = Z_HG + 5 * HG_W
Z_Q = Z_RG + 2 * D_RNN
Z_K = Z_Q + ATTN_W
Z_V = Z_K + KV_W
Z_W = -(-IN_W // (2 * COL_TILE)) * (2 * COL_TILE)


def _cparams(sem, vmem_mb):
    return pltpu.CompilerParams(dimension_semantics=sem, vmem_limit_bytes=vmem_mb * 1024 * 1024)


def _prefix_sum(x):
    n = x.shape[-1]
    upto = (jnp.arange(n)[None, :] <= jnp.arange(n)[:, None]).astype(x.dtype)
    return jnp.sum(x[..., None, :] * upto, axis=-1)


def _mm(a, b):
    return jnp.dot(a.astype(BF16), b.astype(BF16), preferred_element_type=F32)


def _mm_nt(a, b):
    return lax.dot_general(a.astype(BF16), b.astype(BF16), (((1,), (1,)), ((), ())),
                           preferred_element_type=F32)


def _mm_tn(a, b):
    return lax.dot_general(a.astype(BF16), b.astype(BF16), (((0,), (0,)), ((), ())),
                           preferred_element_type=F32)


def _sigmoid(x):
    return 1.0 / (1.0 + jnp.exp(-x))


def _softplus(x):
    return jnp.maximum(x, 0.0) + jnp.log1p(jnp.exp(-jnp.abs(x)))


def _modulated_norm(x, g, scale, shift):
    y = x * lax.rsqrt(jnp.mean(x * x, axis=-1, keepdims=True) + EPS)
    return (y * g) * (1.0 + scale) + shift


def _mod_kernel(c_ref, w_ref, b_ref, o_ref):
    c = c_ref[...]
    s = c * _sigmoid(c)
    o_ref[0] = _mm(s, w_ref[0]) + b_ref[0]


def _modulation(cond8, w_mod, b_mod):
    depth = w_mod.shape[0]
    tn = 1024
    return pl.pallas_call(
        _mod_kernel,
        grid=(depth, N_MOD * D_MODEL // tn),
        in_specs=[pl.BlockSpec((8, D_MODEL), lambda l, j: (0, 0)),
                  pl.BlockSpec((1, D_MODEL, tn), lambda l, j: (l, 0, j)),
                  pl.BlockSpec((1, 1, tn), lambda l, j: (l, 0, j))],
        out_specs=pl.BlockSpec((1, 8, tn), lambda l, j: (l, 0, j)),
        out_shape=jax.ShapeDtypeStruct((depth, 8, N_MOD * D_MODEL), F32),
        compiler_params=_cparams(("arbitrary", "arbitrary"), 32),
        name="modulation",
    )(cond8, w_mod, b_mod.reshape(depth, 1, N_MOD * D_MODEL))


def _inproj_kernel(src_ref, x_ref, g_ref, mod_ref, wa_ref, wb_ref, o_ref, h_ref, w_s):
    j = pl.program_id(1)

    @pl.when(j == 0)
    def _():
        for s in range(x_ref.shape[0] // SEG_ROWS):
            rows = slice(s * SEG_ROWS, (s + 1) * SEG_ROWS)
            h = _modulated_norm(x_ref[rows, :], g_ref[0], mod_ref[s, 1:2, :], mod_ref[s, 0:1, :])
            h_ref[rows, :] = h.astype(BF16)

    @pl.when(pl.program_id(0) == 0)
    def _():
        w_s[j, :, 0:COL_TILE] = wa_ref[0].astype(BF16)
        w_s[j, :, COL_TILE:2 * COL_TILE] = wb_ref[0].astype(BF16)

    o_ref[...] = jnp.dot(h_ref[...], w_s[j], preferred_element_type=F32)


def _input_projection(x, norm_g, modseg, w_in, layer):
    n_tok = x.shape[0]
    segs = 2
    tm = segs * SEG_ROWS
    n_col = Z_W // (2 * COL_TILE)
    src = jnp.asarray(_SRC_TILES + (_SRC_TILES[-1],) * (Z_W // COL_TILE - len(_SRC_TILES)), jnp.int32)

    def wspec(k):
        return pl.BlockSpec((1, D_MODEL, COL_TILE),
                            lambda i, j, s: (layer, 0, s[2 * jnp.where(i == 0, j, n_col - 1) + k]))

    grid_spec = pltpu.PrefetchScalarGridSpec(
        num_scalar_prefetch=1,
        grid=(n_tok // tm, n_col),
        in_specs=[pl.BlockSpec((tm, D_MODEL), lambda i, j, s: (i, 0)),
                  pl.BlockSpec((1, 1, D_MODEL), lambda i, j, s: (layer, 0, 0)),
                  pl.BlockSpec((segs, N_MOD, D_MODEL), lambda i, j, s: (i, 0, 0)),
                  wspec(0), wspec(1)],
        out_specs=pl.BlockSpec((tm, 2 * COL_TILE), lambda i, j, s: (i, j)),
        scratch_shapes=[pltpu.VMEM((tm, D_MODEL), BF16),
                        pltpu.VMEM((n_col, D_MODEL, 2 * COL_TILE), BF16)],
    )
    return pl.pallas_call(
        _inproj_kernel,
        grid_spec=grid_spec,
        out_shape=jax.ShapeDtypeStruct((n_tok, Z_W), F32),
        compiler_params=_cparams(("arbitrary", "arbitrary"), 56),
        name="input_projection",
    )(src, x, norm_g.reshape(-1, 1, D_MODEL), modseg, w_in, w_in)


def _head_rms(x, g):
    lane = lax.broadcasted_iota(jnp.int32, x.shape, 1)
    lo = lane < HEAD_DIM
    xx = x * x
    s_lo = jnp.sum(jnp.where(lo, xx, 0.0), axis=-1, keepdims=True)
    s_hi = jnp.sum(jnp.where(lo, 0.0, xx), axis=-1, keepdims=True)
    inv = jnp.where(lo, lax.rsqrt(s_lo * (1.0 / HEAD_DIM) + EPS),
                    lax.rsqrt(s_hi * (1.0 / HEAD_DIM) + EPS))
    return x * inv * g


def _rope(x, cos, sin_a, sin_b):
    q = HEAD_DIM // 4
    return x * cos + pltpu.roll(x, LANES - q, 1) * sin_a + pltpu.roll(x, q, 1) * sin_b


def _attn_kernel(*refs, t_len, use_ctx):
    if use_ctx:
        (q_ref, k_ref, v_ref, qg_ref, kg_ref, cos_q_ref, sa_q_ref, sb_q_ref,
         cos_k_ref, sa_k_ref, sb_k_ref, ck_ref, cv_ref, o_ref, kall_ref, vall_ref) = refs
    else:
        (q_ref, k_ref, v_ref, qg_ref, kg_ref, o_ref, ko_ref, vo_ref, kall_ref, vall_ref) = refs

    @pl.when(pl.program_id(1) == 0)
    def _():
        k = _head_rms(k_ref[...], kg_ref[...])
        if use_ctx:
            k = _rope(k, cos_k_ref[...], sa_k_ref[...], sb_k_ref[...])
            kall_ref[0:t_len, :] = k.astype(BF16)
            vall_ref[0:t_len, :] = v_ref[...].astype(BF16)
            kall_ref[t_len:, :] = ck_ref[0, 0].astype(BF16)
            vall_ref[t_len:, :] = cv_ref[0, 0].astype(BF16)
        else:
            ko_ref[0] = k
            vo_ref[0] = v_ref[...]
            kall_ref[...] = k.astype(BF16)
            vall_ref[...] = v_ref[...].astype(BF16)

    heads = []
    for c in range(ATTN_W // LANES):
        qc = _head_rms(q_ref[:, c * LANES:(c + 1) * LANES], qg_ref[...])
        if use_ctx:
            qc = _rope(qc, cos_q_ref[...], sa_q_ref[...], sb_q_ref[...])
        qc = (qc * (HEAD_DIM ** -0.5)).astype(BF16)
        heads.append(qc[:, :HEAD_DIM])
        heads.append(qc[:, HEAD_DIM:])
    tq = q_ref.shape[0]
    for g in range(N_KV_HEADS):
        qs = jnp.concatenate(heads[g * KV_GROUP:(g + 1) * KV_GROUP], axis=0)
        kh = kall_ref[:, g * HEAD_DIM:(g + 1) * HEAD_DIM]
        vh = vall_ref[:, g * HEAD_DIM:(g + 1) * HEAD_DIM]
        s = _mm_nt(qs, kh)
        m = jnp.max(s, axis=-1, keepdims=True)
        p = jnp.exp(s - m)
        den = jnp.sum(p, axis=-1, keepdims=True)
        o = _mm(p, vh) / den
        for j in range(KV_GROUP):
            hh = g * KV_GROUP + j
            o_ref[:, hh * HEAD_DIM:(hh + 1) * HEAD_DIM] = o[j * tq:(j + 1) * tq].astype(BF16)


def _attention(z, q_g, k_g, layer, n_batch, t_len, row0, rope=None, cache=None):
    use_ctx = cache is not None
    nq = t_len // Q_TILE
    rb0 = row0 // t_len
    qb0 = row0 // Q_TILE
    qg = jnp.tile(q_g[layer], 2).reshape(1, LANES)
    kg = jnp.tile(k_g[layer], 2).reshape(1, LANES)
    n_out_tok = n_batch * t_len
    vec = pl.BlockSpec((1, LANES), lambda b, i: (0, 0))
    in_specs = [pl.BlockSpec((Q_TILE, ATTN_W), lambda b, i: (qb0 + b * nq + i, Z_Q // ATTN_W)),
                pl.BlockSpec((t_len, KV_W), lambda b, i: (rb0 + b, Z_K // KV_W)),
                pl.BlockSpec((t_len, KV_W), lambda b, i: (rb0 + b, Z_V // KV_W)),
                vec, vec]
    args = [z, z, z, qg, kg]
    out_specs = [pl.BlockSpec((Q_TILE, ATTN_W), lambda b, i: (b * nq + i, 0))]
    out_shape = [jax.ShapeDtypeStruct((n_out_tok, ATTN_W), BF16)]
    t_keys = t_len
    if use_ctx:
        cos, sin_a, sin_b = rope
        cache_k, cache_v = cache
        past = cache_k.shape[2]
        t_keys = t_len + past
        tq_spec = pl.BlockSpec((Q_TILE, LANES), lambda b, i: (i, 0))
        tk_spec = pl.BlockSpec((t_len, LANES), lambda b, i: (0, 0))
        c_spec = pl.BlockSpec((1, 1, past, KV_W), lambda b, i: (b, layer, 0, 0))
        in_specs += [tq_spec, tq_spec, tq_spec, tk_spec, tk_spec, tk_spec, c_spec, c_spec]
        args += [cos, sin_a, sin_b, cos, sin_a, sin_b,
                 cache_k.reshape(cache_k.shape[0], cache_k.shape[1], past, KV_W),
                 cache_v.reshape(cache_v.shape[0], cache_v.shape[1], past, KV_W)]
    else:
        kv_spec = pl.BlockSpec((1, t_len, KV_W), lambda b, i: (b, 0, 0))
        out_specs += [kv_spec, kv_spec]
        out_shape += [jax.ShapeDtypeStruct((n_batch, t_len, KV_W), F32)] * 2
    return pl.pallas_call(
        functools.partial(_attn_kernel, t_len=t_len, use_ctx=use_ctx),
        grid=(n_batch, nq),
        in_specs=in_specs,
        out_specs=out_specs,
        out_shape=out_shape,
        scratch_shapes=[pltpu.VMEM((t_keys, KV_W), BF16), pltpu.VMEM((t_keys, KV_W), BF16)],
        compiler_params=_cparams(("arbitrary", "arbitrary"), 48),
        name="attention_ctx" if use_ctx else "attention",
    )(*args)


def _shift_rows(x, d):
    n = x.shape[0]
    row = lax.broadcasted_iota(jnp.int32, x.shape, 0)
    y = pltpu.roll(x, d % n, 0)
    if d > 0:
        return jnp.where(row >= d, y, 0.0)
    return jnp.where(row < n + d, y, 0.0)


def _linear_scan(a, u, reverse):
    n = a.shape[0]
    row = lax.broadcasted_iota(jnp.int32, a.shape, 0)
    d = 1
    while d < n:
        if reverse:
            keep = row < n - d
            a_s = jnp.where(keep, pltpu.roll(a, n - d, 0), 1.0)
            u_s = jnp.where(keep, pltpu.roll(u, n - d, 0), 0.0)
        else:
            keep = row >= d
            a_s = jnp.where(keep, pltpu.roll(a, d, 0), 1.0)
            u_s = jnp.where(keep, pltpu.roll(u, d, 0), 0.0)
        u = a * u_s + u
        a = a * a_s
        d *= 2
    return a, u


def _rglru_kernel(*refs, use_ctx):
    if use_ctx:
        (x_ref, y_ref, cw_ref, cb_ref, wg_ref, bg_ref, lam_ref, h0_ref, o_ref) = refs
    else:
        (x_ref, y_ref, cw_ref, cb_ref, wg_ref, bg_ref, lam_ref, o_ref, last_ref) = refs
    x = x_ref[...]
    cw = cw_ref[0]
    xr = cb_ref[0] + _shift_rows(x, 1) * cw[0:1] + x * cw[1:2] \
        + _shift_rows(x, -1) * cw[2:3] + _shift_rows(x, -2) * cw[3:4]
    gates = _sigmoid(_mm(xr, wg_ref[0]) + bg_ref[0])
    n = x.shape[0]
    total = None
    lasts = []
    for d in range(2):
        r = gates[:, (2 * d) * D_RNN:(2 * d + 1) * D_RNN]
        i = gates[:, (2 * d + 1) * D_RNN:(2 * d + 2) * D_RNN]
        log_a = (-RG_C * _softplus(-lam_ref[0, d:d + 1, :])) * r
        a = jnp.exp(log_a)
        u = jnp.sqrt(1.0 - a * a) * (i * xr)
        a_cum, h = _linear_scan(a, u, reverse=(d == 1))
        if use_ctx:
            h = h + a_cum * h0_ref[0, 0, d:d + 1, :]
        else:
            lasts.append(h[0:1] if d == 1 else h[n - 1:n])
        total = h if total is None else total + h
    o_ref[...] = (jax.nn.gelu(y_ref[...]) * total).astype(BF16)
    if not use_ctx:
        last_ref[0] = jnp.concatenate(lasts, axis=0)


def _rglru(z, conv_w, conv_b, w_gate, b_gate, lam, layer, n_batch, t_len, row0, state=None):
    use_ctx = state is not None
    rb0 = row0 // t_len
    in_specs = [pl.BlockSpec((t_len, D_RNN), lambda b: (rb0 + b, Z_RG // D_RNN)),
                pl.BlockSpec((t_len, D_RNN), lambda b: (rb0 + b, Z_RG // D_RNN + 1)),
                pl.BlockSpec((1, 4, D_RNN), lambda b: (layer, 0, 0)),
                pl.BlockSpec((1, 1, D_RNN), lambda b: (layer, 0, 0)),
                pl.BlockSpec((1, D_RNN, 4 * D_RNN), lambda b: (layer, 0, 0)),
                pl.BlockSpec((1, 1, 4 * D_RNN), lambda b: (layer, 0, 0)),
                pl.BlockSpec((1, 2, D_RNN), lambda b: (layer, 0, 0))]
    args = [z, z, conv_w, conv_b.reshape(-1, 1, D_RNN), w_gate, b_gate, lam]
    out_specs = [pl.BlockSpec((t_len, D_RNN), lambda b: (b, 0))]
    out_shape = [jax.ShapeDtypeStruct((n_batch * t_len, D_RNN), BF16)]
    if use_ctx:
        in_specs.append(pl.BlockSpec((1, 1, 2, D_RNN), lambda b: (b, layer, 0, 0)))
        args.append(state)
    else:
        out_specs.append(pl.BlockSpec((1, 2, D_RNN), lambda b: (b, 0, 0)))
        out_shape.append(jax.ShapeDtypeStruct((n_batch, 2, D_RNN), F32))
    return pl.pallas_call(
        functools.partial(_rglru_kernel, use_ctx=use_ctx),
        grid=(n_batch,),
        in_specs=in_specs,
        out_specs=out_specs,
        out_shape=out_shape,
        compiler_params=_cparams(("arbitrary",), 56),
        name="rglru_ctx" if use_ctx else "rglru",
    )(*args)


def _cumsum_rows(g, reverse):
    n, w = g.shape
    m = n // ROW_BLOCK

    def scan(x, size, pos):
        d = 1
        while d < size:
            if reverse:
                x = x + jnp.where(pos < size - d, pltpu.roll(x, x.shape[0] - d, 0), 0.0)
            else:
                x = x + jnp.where(pos >= d, pltpu.roll(x, d, 0), 0.0)
            d *= 2
        return x

    g = scan(g, ROW_BLOCK, lax.broadcasted_iota(jnp.int32, g.shape, 0) & (ROW_BLOCK - 1))
    g3 = g.reshape(m, ROW_BLOCK, w)
    total = g3[:, 0, :] if reverse else g3[:, ROW_BLOCK - 1, :]
    before = scan(total, m, lax.broadcasted_iota(jnp.int32, total.shape, 0)) - total
    return (g3 + before[:, None, :]).reshape(n, w)


def _block_row_fn(gc, reverse):
    n, w = gc.shape
    g3 = gc.reshape(n // ROW_BLOCK, ROW_BLOCK, w)
    edge_row = 0 if reverse else ROW_BLOCK - 1
    edge = jnp.broadcast_to(g3[:, edge_row:edge_row + 1, :], g3.shape)
    sub = lax.broadcasted_iota(jnp.int32, g3.shape, 1)

    def at(blk):
        r = blk // 2 if reverse else blk // 2 - 1
        if blk > ROW_BLOCK:
            groups = blk // ROW_BLOCK
            e4 = edge.reshape(n // blk, groups, ROW_BLOCK, w)
            pick = r // ROW_BLOCK
            return jnp.broadcast_to(e4[:, pick:pick + 1], e4.shape).reshape(n, w)
        out = jnp.broadcast_to(g3[:, r:r + 1, :], g3.shape)
        for b0 in range(blk, ROW_BLOCK, blk):
            out = jnp.where(sub >= b0, jnp.broadcast_to(g3[:, b0 + r:b0 + r + 1, :], g3.shape), out)
        return out.reshape(n, w)

    return at


def _first_head(shape):
    lane = lax.broadcasted_iota(jnp.int32, shape, len(shape) - 1)
    return (lane & (LANES - 1)) < HG_DK


def _pair(x, p):
    return x[:, p * LANES:(p + 1) * LANES]


def _hgrn_gates(f_logit, lb):
    y = jnp.exp(-jnp.abs(f_logit))
    log_sig = jnp.minimum(f_logit, 0.0) - jnp.log(1.0 + y)
    a = jnp.log(lb)
    b = jnp.log1p(-lb) + log_sig
    log_f = jnp.maximum(a, b) + jnp.log(1.0 + jnp.exp(-jnp.abs(a - b)))
    k = (1.0 - lb) * (jnp.where(f_logit >= 0.0, y, 1.0) / (1.0 + y))
    return log_f * LOG2_E, k


def _hgrn_chunk_state(gc, k, v, reverse):
    n = gc.shape[0]
    g_tot = gc[0:1] if reverse else gc[n - 1:n]
    k_end = (k * jnp.exp2(jnp.minimum(g_tot - gc, 0.0))).astype(BF16)
    vb = v.astype(BF16)
    same_head = (_first_head((LANES, LANES))
                 == (lax.broadcasted_iota(jnp.int32, (LANES, LANES), 0) < HG_DK))
    ds = [jnp.where(same_head, _mm_tn(_pair(k_end, p), _pair(vb, p)), 0.0) for p in range(N_PAIRS)]
    return ds, jnp.exp2(g_tot)


def _decay_state(s, e_row):
    n = s.shape[0]
    eye = (lax.broadcasted_iota(jnp.int32, (n, n), 0) == lax.broadcasted_iota(jnp.int32, (n, n), 1))
    col = jnp.sum(jnp.where(eye, jnp.broadcast_to(e_row, (n, n)), 0.0), axis=1, keepdims=True)
    return s * col


def _hgrn_intra_scores(q, ks, gcs, txs):
    n = q.shape[0]
    row = lax.broadcasted_iota(jnp.int32, q.shape, 0)
    first = _first_head(q.shape)
    boundary = [_block_row_fn(gcs[0], False), _block_row_fn(gcs[1], True)]
    scores = [None] * HG_HEADS
    w = 1
    while w < n:
        blk = 2 * w
        second = (row & w) != 0
        e_f = jnp.exp2(-jnp.abs(gcs[0] - boundary[0](blk)))
        e_b = jnp.exp2(-jnp.abs(gcs[1] - boundary[1](blk)))
        q_f = jnp.where(second, q * e_f, 0.0).astype(BF16)
        q_b = jnp.where(second, 0.0, q * e_b).astype(BF16)
        k_f = jnp.where(second, 0.0, ks[0] * e_f)
        k_b = jnp.where(second, ks[1] * e_b, 0.0)
        k_heads = [(jnp.where(first, k_f, 0.0).astype(BF16), jnp.where(first, k_b, 0.0).astype(BF16)),
                   (jnp.where(first, 0.0, k_f).astype(BF16), jnp.where(first, 0.0, k_b).astype(BF16))]
        sel = (txs >= w) & (txs < blk)
        for p in range(N_PAIRS):
            qc = jnp.concatenate([_pair(q_f, p), _pair(q_b, p)], axis=1)
            for j in range(2):
                kc = jnp.concatenate([_pair(k_heads[j][0], p), _pair(k_heads[j][1], p)], axis=1)
                s = jnp.where(sel, _mm_nt(qc, kc), 0.0)
                h = 2 * p + j
                scores[h] = s if scores[h] is None else scores[h] + s
        w = blk
    return scores


def _hgrn_kernel(*refs, t_len, use_ctx):
    if use_ctx:
        (q_ref, ff_ref, fb_ref, i_ref, og_ref, lb_ref, ng_ref, s0_ref, o_ref, sin_ref, gk_ref) = refs
    else:
        (q_ref, ff_ref, fb_ref, i_ref, og_ref, lb_ref, ng_ref, o_ref, sl_ref) = refs
    c = HG_CHUNK
    n_chunks = t_len // c
    f_refs = (ff_ref, fb_ref)

    def load_dir(d, r0):
        log_f, k = _hgrn_gates(f_refs[d][pl.ds(r0, c), :], lb_ref[0, d:d + 1, :])
        return _cumsum_rows(log_f, reverse=(d == 1)), k

    if use_ctx:
        zero = jnp.zeros((HG_DK, HG_DV), F32)
        for d in range(2):
            states = []
            for p in range(N_PAIRS):
                top = jnp.concatenate([s0_ref[0, 0, d, 2 * p], zero], axis=1)
                bot = jnp.concatenate([zero, s0_ref[0, 0, d, 2 * p + 1]], axis=1)
                states.append(jnp.concatenate([top, bot], axis=0))
            order = range(n_chunks) if d == 0 else range(n_chunks - 1, -1, -1)
            for ci in order:
                for p in range(N_PAIRS):
                    sin_ref[ci, p, d * LANES:(d + 1) * LANES, :] = states[p]
                gc, k = load_dir(d, ci * c)
                gk_ref[d, 0, pl.ds(ci * c, c), :] = gc
                gk_ref[d, 1, pl.ds(ci * c, c), :] = k
                ds, e_tot = _hgrn_chunk_state(gc, k, i_ref[pl.ds(ci * c, c), :], reverse=(d == 1))
                states = [_decay_state(states[p], _pair(e_tot, p)) + ds[p] for p in range(N_PAIRS)]

    ti = lax.broadcasted_iota(jnp.int32, (c, c), 0)
    si = lax.broadcasted_iota(jnp.int32, (c, c), 1)
    txs = ti ^ si
    diag = ti == si

    def chunk_body(ci, carry):
        r0 = pl.multiple_of(ci * c, c)
        hq = q_ref[pl.ds(r0, c), :]
        q = hq * _sigmoid(hq)
        v = i_ref[pl.ds(r0, c), :]
        gcs, ks = [], []
        for d in range(2):
            if use_ctx:
                gc, k = gk_ref[d, 0, pl.ds(r0, c), :], gk_ref[d, 1, pl.ds(r0, c), :]
            else:
                gc, k = load_dir(d, r0)
            gcs.append(gc)
            ks.append(k)
        scores = _hgrn_intra_scores(q, ks, gcs, txs)
        first = _first_head(q.shape)
        k_sum = ks[0] + ks[1]
        v_heads = (jnp.where(first, v, 0.0).astype(BF16), jnp.where(first, 0.0, v).astype(BF16))
        k_heads = (jnp.where(first, k_sum, 0.0).astype(BF16), jnp.where(first, 0.0, k_sum).astype(BF16))
        q_bf = q.astype(BF16)
        if use_ctx:
            qe = [(q * jnp.exp2(gcs[d])).astype(BF16) for d in range(2)]
        else:
            for d in range(2):
                ds, _ = _hgrn_chunk_state(gcs[d], ks[d], v, reverse=(d == 1))
                for p in range(N_PAIRS):
                    sl_ref[0, d, 2 * p] = ds[p][:HG_DK, :HG_DV]
                    sl_ref[0, d, 2 * p + 1] = ds[p][HG_DK:, HG_DV:]
        outs = []
        for p in range(N_PAIRS):
            o = None
            for j in range(2):
                s_diag = jnp.where(diag, _mm_nt(_pair(q_bf, p), _pair(k_heads[j], p)), 0.0)
                t = _mm(scores[2 * p + j] + s_diag, _pair(v_heads[j], p))
                o = t if o is None else o + t
            if use_ctx:
                o = o + _mm(jnp.concatenate([_pair(qe[0], p), _pair(qe[1], p)], axis=1), sin_ref[ci, p])
            outs.append(_head_rms(o, ng_ref[0]))
        og = og_ref[pl.ds(r0, c), :]
        o_all = jnp.concatenate(outs, axis=1) * (og * _sigmoid(og))
        o_ref[pl.ds(r0, c), :] = o_all.astype(BF16)
        return carry

    if n_chunks == 1:
        chunk_body(0, 0)
    else:
        lax.fori_loop(0, n_chunks, chunk_body, 0)


def _hgrn(z, lb, norm_g, layer, n_batch, t_len, row0, state=None):
    use_ctx = state is not None
    rb0 = row0 // t_len
    cb = Z_HG // HG_W

    def zspec(k):
        return pl.BlockSpec((t_len, HG_W), lambda b: (rb0 + b, cb + k))

    in_specs = [zspec(0), zspec(1), zspec(2), zspec(3), zspec(4),
                pl.BlockSpec((1, 2, HG_W), lambda b: (layer, 0, 0)),
                pl.BlockSpec((1, 1, LANES), lambda b: (layer, 0, 0))]
    args = [z, z, z, z, z, lb, jnp.tile(norm_g, (1, LANES // HG_DV)).reshape(-1, 1, LANES)]
    out_specs = [pl.BlockSpec((t_len, HG_W), lambda b: (b, 0))]
    out_shape = [jax.ShapeDtypeStruct((n_batch * t_len, HG_W), BF16)]
    scratch = []
    if use_ctx:
        in_specs.append(pl.BlockSpec((1, 1, 2, HG_HEADS, HG_DK, HG_DV), lambda b: (b, layer, 0, 0, 0, 0)))
        args.append(state)
        scratch.append(pltpu.VMEM((t_len // HG_CHUNK, N_PAIRS, 2 * LANES, LANES), F32))
        scratch.append(pltpu.VMEM((2, 2, t_len, HG_W), F32))
    else:
        out_specs.append(pl.BlockSpec((1, 2, HG_HEADS, HG_DK, HG_DV), lambda b: (b, 0, 0, 0, 0)))
        out_shape.append(jax.ShapeDtypeStruct((n_batch, 2, HG_HEADS, HG_DK, HG_DV), F32))
    return pl.pallas_call(
        functools.partial(_hgrn_kernel, t_len=t_len, use_ctx=use_ctx),
        grid=(n_batch,),
        in_specs=in_specs,
        out_specs=out_specs,
        out_shape=out_shape,
        scratch_shapes=scratch,
        compiler_params=_cparams(("arbitrary",), 56),
        name="hgrn_ctx" if use_ctx else "hgrn",
    )(*args)


def _merge_kernel(ap_ref, as_ref, rp_ref, rs_ref, hp_ref, hs_ref, g0_ref, g1_ref, g2_ref, x_ref, mod_ref,
                  wa_ref, wr_ref, wh_ref, wo_ref, o_ref, wa_s, wr_s, wh_s, wo_s, *, prompt_tiles):
    @pl.when(pl.program_id(0) == 0)
    def _():
        wa_s[...] = wa_ref[0].astype(BF16)
        wr_s[...] = wr_ref[0].astype(BF16)
        wh_s[...] = wh_ref[0].astype(BF16)
        wo_s[...] = wo_ref[0].astype(BF16)

    is_prompt = pl.program_id(0) < prompt_tiles

    def proj(p_ref, s_ref, w_s):
        return jnp.dot(jnp.where(is_prompt, p_ref[...], s_ref[...]), w_s[...], preferred_element_type=F32)

    merged = (_sigmoid(g0_ref[...]) * proj(ap_ref, as_ref, wa_s)
              + _sigmoid(g1_ref[...]) * proj(rp_ref, rs_ref, wr_s)
              + _sigmoid(g2_ref[...]) * proj(hp_ref, hs_ref, wh_s))
    out = jnp.dot(merged.astype(BF16), wo_s[...], preferred_element_type=F32)
    o_ref[...] = x_ref[...] + mod_ref[0, 2:3, :] * out


def _merge_project(mixed, z, x, modseg, w_attn_o, w_rnn_o, w_hgrn_o, w_out, layer):
    n_tok = x.shape[0]
    tm = 512
    per_seg = SEG_ROWS // tm
    prompt_tiles = mixed[0][0].shape[0] // tm
    half_p = pl.BlockSpec((tm, ATTN_W), lambda i: (jnp.minimum(i, prompt_tiles - 1), 0))
    half_s = pl.BlockSpec((tm, ATTN_W), lambda i: (jnp.maximum(i - prompt_tiles, 0), 0))
    full = pl.BlockSpec((tm, D_MODEL), lambda i: (i, 0))

    def gspec(k):
        return pl.BlockSpec((tm, D_MODEL), lambda i: (i, Z_GL // D_MODEL + k))

    def wspec(rows):
        return pl.BlockSpec((1, rows, D_MODEL), lambda i: (layer, 0, 0))

    return pl.pallas_call(
        functools.partial(_merge_kernel, prompt_tiles=prompt_tiles),
        grid=(n_tok // tm,),
        in_specs=[half_p, half_s, half_p, half_s, half_p, half_s, gspec(0), gspec(1), gspec(2), full,
                  pl.BlockSpec((1, N_MOD, D_MODEL), lambda i: (i // per_seg, 0, 0)),
                  wspec(ATTN_W), wspec(D_RNN), wspec(HG_W), wspec(D_MODEL)],
        out_specs=full,
        out_shape=jax.ShapeDtypeStruct((n_tok, D_MODEL), F32),
        scratch_shapes=[pltpu.VMEM((ATTN_W, D_MODEL), BF16), pltpu.VMEM((D_RNN, D_MODEL), BF16),
                        pltpu.VMEM((HG_W, D_MODEL), BF16), pltpu.VMEM((D_MODEL, D_MODEL), BF16)],
        compiler_params=_cparams(("arbitrary",), 56),
        name="merge_project",
    )(*[part for pair in mixed for part in pair], z, z, z, x, modseg, w_attn_o, w_rnn_o, w_hgrn_o, w_out)


def _router_kernel(x_ref, g_ref, mod_ref, wr_ref, br_ref, h_ref, slot_ref, wt_ref, meta_ref, run_ref):
    @pl.when(pl.program_id(0) == 0)
    def _():
        run_ref[...] = jnp.zeros_like(run_ref)

    h = _modulated_norm(x_ref[...], g_ref[0], mod_ref[0, 4:5, :], mod_ref[0, 3:4, :])
    h_ref[...] = h
    logits = jnp.dot(h, wr_ref[0], preferred_element_type=F32,
                     precision=lax.Precision.HIGHEST) + br_ref[0]
    tm = logits.shape[0]
    lane = lax.broadcasted_iota(jnp.int32, logits.shape, 1)
    ids, vals = [], []
    chosen = jnp.zeros(logits.shape, jnp.bool_)
    work = logits
    for _ in range(TOP_K):
        m = jnp.max(work, axis=-1, keepdims=True)
        idx = jnp.min(jnp.where(work == m, lane, LANES), axis=-1, keepdims=True)
        hit = lane == idx
        chosen = chosen | hit
        work = jnp.where(hit, -jnp.inf, work)
        ids.append(idx)
        vals.append(m)
    exps = [jnp.exp(v - vals[0]) for v in vals]
    den = exps[0] + exps[1] + exps[2] + exps[3]
    ind = jnp.where(chosen, 1.0, 0.0)
    r_i = lax.broadcasted_iota(jnp.int32, (tm, tm), 0)
    c_i = lax.broadcasted_iota(jnp.int32, (tm, tm), 1)
    count = jnp.sum(ind, axis=0, keepdims=True)
    padded = jnp.floor((count + (ROW_BLOCK - 1)) * (1.0 / ROW_BLOCK)) * ROW_BLOCK
    incl = jnp.broadcast_to(padded, (8, LANES))
    lane8 = lax.broadcasted_iota(jnp.int32, (8, LANES), 1)
    d = 1
    while d < LANES:
        incl = incl + jnp.where(lane8 >= d, pltpu.roll(incl, d, 1), 0.0)
        d *= 2
    local_start = incl[0:1] - padded
    place = _mm(jnp.where(c_i < r_i, 1.0, 0.0), ind) + local_start
    slot_out = jnp.zeros(logits.shape, jnp.int32)
    wt_out = jnp.zeros(logits.shape, F32)
    for k in range(TOP_K):
        slot = jnp.sum(jnp.where(lane == ids[k], place, 0.0), axis=-1, keepdims=True)
        slot_out = jnp.where(lane == k, slot.astype(jnp.int32), slot_out)
        wt_out = jnp.where(lane == k, exps[k] / den, wt_out)
    slot_ref[...] = slot_out
    wt_ref[...] = wt_out
    row8 = lax.broadcasted_iota(jnp.int32, (8, LANES), 0)
    table = jnp.where(row8 == 0, local_start, jnp.where(row8 == 1, padded, run_ref[0:1, :]))
    meta_ref[0] = jnp.where(row8 < 3, table, 0.0).astype(jnp.int32)
    run_ref[...] = run_ref[...] + padded


def _router(x, norm_g, modseg, w_router, b_router, layer):
    n_tok = x.shape[0]
    tm = TOKEN_TILE
    per_seg = SEG_ROWS // tm
    depth = w_router.shape[0]
    w_pad = jnp.zeros((depth, D_MODEL, LANES), F32).at[:, :, :N_EXPERTS].set(w_router)
    b_pad = jnp.full((depth, 1, LANES), NEG_BIG, F32).at[:, 0, :N_EXPERTS].set(b_router)
    full = pl.BlockSpec((tm, D_MODEL), lambda i: (i, 0))
    small = pl.BlockSpec((tm, LANES), lambda i: (i, 0))
    return pl.pallas_call(
        _router_kernel,
        grid=(n_tok // tm,),
        in_specs=[full,
                  pl.BlockSpec((1, 1, D_MODEL), lambda i: (layer, 0, 0)),
                  pl.BlockSpec((1, N_MOD, D_MODEL), lambda i: (i // per_seg, 0, 0)),
                  pl.BlockSpec((1, D_MODEL, LANES), lambda i: (layer, 0, 0)),
                  pl.BlockSpec((1, 1, LANES), lambda i: (layer, 0, 0))],
        out_specs=[full, small, small, pl.BlockSpec((1, 8, LANES), lambda i: (i, 0, 0))],
        out_shape=[jax.ShapeDtypeStruct((n_tok, D_MODEL), F32),
                   jax.ShapeDtypeStruct((n_tok, LANES), jnp.int32),
                   jax.ShapeDtypeStruct((n_tok, LANES), F32),
                   jax.ShapeDtypeStruct((n_tok // tm, 8, LANES), jnp.int32)],
        scratch_shapes=[pltpu.VMEM((8, LANES), F32)],
        compiler_params=_cparams(("arbitrary",), 40),
        name="router",
    )(x, norm_g.reshape(-1, 1, D_MODEL), modseg, w_pad, b_pad)


MAX_BIG = LOCAL_ROWS // BIG_BLOCK
MAX_SMALL = N_EXPERTS * (BIG_BLOCK // ROW_BLOCK - 1)
COPY_LIST_LEN = 1024
_BIG_AT = 2
_SMALL_AT = 2 + 2 * MAX_BIG
assert _SMALL_AT + 2 * MAX_SMALL <= COPY_LIST_LEN


def _copy_lists(local_start, padded, sorted_start):
    n_big = padded // BIG_BLOCK
    n_small = (padded - n_big * BIG_BLOCK) // ROW_BLOCK

    def expand(count, local0, sorted0, step, cap):
        last = _prefix_sum(count)[:, None, :]
        first = last - count[:, None, :]
        f = jnp.arange(cap, dtype=jnp.int32)[None, :, None]
        mine = ((first <= f) & (f < last)).astype(jnp.int32)
        offset = step * (f - first)
        return (last[:, 0, -1:], jnp.sum(mine * (local0[:, None, :] + offset), axis=2),
                jnp.sum(mine * (sorted0[:, None, :] + offset), axis=2))

    cnt_b, loc_b, srt_b = expand(n_big, local_start, sorted_start, BIG_BLOCK, MAX_BIG)
    done = n_big * BIG_BLOCK
    cnt_s, loc_s, srt_s = expand(n_small, local_start + done, sorted_start + done, ROW_BLOCK, MAX_SMALL)
    parts = [cnt_b, cnt_s, loc_b, srt_b, loc_s, srt_s]
    used = sum(p.shape[1] for p in parts)
    parts.append(jnp.zeros((padded.shape[0], COPY_LIST_LEN - used), jnp.int32))
    return jnp.concatenate(parts, axis=1).astype(jnp.int32).reshape(-1)


def _for_each_block(list_ref, fn):
    def big(j, c):
        fn(pl.multiple_of(list_ref[_BIG_AT + j], ROW_BLOCK),
           pl.multiple_of(list_ref[_BIG_AT + MAX_BIG + j], ROW_BLOCK), BIG_BLOCK)
        return c

    def small(j, c):
        fn(pl.multiple_of(list_ref[_SMALL_AT + j], ROW_BLOCK),
           pl.multiple_of(list_ref[_SMALL_AT + MAX_SMALL + j], ROW_BLOCK), ROW_BLOCK)
        return c

    lax.fori_loop(0, list_ref[0], big, 0)
    lax.fori_loop(0, list_ref[1], small, 0)


def _slot_matrix(slot_ref, values):
    col = lax.broadcasted_iota(jnp.int32, (TOKEN_TILE, LOCAL_ROWS), 1)
    out = jnp.zeros((TOKEN_TILE, LOCAL_ROWS), F32)
    for k in range(TOP_K):
        out = jnp.where(col == slot_ref[:, k:k + 1], values[k], out)
    return out.astype(BF16)


def _dispatch_kernel(meta_ref, prev_meta_ref, tail_ref, slot_ref, h_ref, xs_ref, local_ref, zero_ref,
                     sem, block_sem):
    i = pl.program_id(0)
    cur = i % 2

    @pl.when(i == 0)
    def _():
        zero_ref[...] = jnp.zeros_like(zero_ref)

        def tails(fn):
            def group(e, carry):
                def block(j, c):
                    fn(pl.multiple_of(tail_ref[e] + j * ROW_BLOCK, ROW_BLOCK))
                    return c
                return lax.fori_loop(0, tail_ref[N_EXPERTS + e], block, carry)
            lax.fori_loop(0, N_EXPERTS, group, 0)

        def zero_copy(row):
            return pltpu.make_async_copy(zero_ref.at[pl.ds(0, ROW_BLOCK)],
                                         xs_ref.at[pl.ds(row, ROW_BLOCK)], sem)

        tails(lambda row: zero_copy(row).start())
        tails(lambda row: zero_copy(row).wait())

        def unused(fn):
            def tile(j, c):
                fn(pl.multiple_of(tail_ref[2 * N_EXPERTS] + j * EXPERT_TILE, EXPERT_TILE))
                return c
            lax.fori_loop(0, tail_ref[2 * N_EXPERTS + 1], tile, 0)

        def zero_tile(row):
            return pltpu.make_async_copy(zero_ref, xs_ref.at[pl.ds(row, EXPERT_TILE)], sem)

        unused(lambda row: zero_tile(row).start())
        unused(lambda row: zero_tile(row).wait())

    onehot = _slot_matrix(slot_ref, [1.0] * TOP_K)
    local_ref[cur] = _mm_tn(onehot, h_ref[...])

    def block_copy(buf):
        return lambda local_row, sorted_row, rows: pltpu.make_async_copy(
            local_ref.at[buf, pl.ds(local_row, rows)], xs_ref.at[pl.ds(sorted_row, rows)],
            block_sem.at[buf])

    _for_each_block(meta_ref, lambda a, b, n: block_copy(cur)(a, b, n).start())

    @pl.when(i > 0)
    def _():
        _for_each_block(prev_meta_ref, lambda a, b, n: block_copy(1 - cur)(a, b, n).wait())

    @pl.when(i == pl.num_programs(0) - 1)
    def _():
        _for_each_block(meta_ref, lambda a, b, n: block_copy(cur)(a, b, n).wait())


def _dispatch(h, slots, meta, tails, n_rows):
    n_tok = h.shape[0]
    return pl.pallas_call(
        _dispatch_kernel,
        grid=(n_tok // TOKEN_TILE,),
        in_specs=[pl.BlockSpec((COPY_LIST_LEN,), lambda i: (i,), memory_space=pltpu.SMEM),
                  pl.BlockSpec((COPY_LIST_LEN,), lambda i: (jnp.maximum(i - 1, 0),), memory_space=pltpu.SMEM),
                  pl.BlockSpec((LANES,), lambda i: (0,), memory_space=pltpu.SMEM),
                  pl.BlockSpec((TOKEN_TILE, LANES), lambda i: (i, 0)),
                  pl.BlockSpec((TOKEN_TILE, D_MODEL), lambda i: (i, 0))],
        out_specs=pl.BlockSpec(memory_space=pl.ANY),
        out_shape=jax.ShapeDtypeStruct((n_rows, D_MODEL), F32),
        scratch_shapes=[pltpu.VMEM((2, LOCAL_ROWS, D_MODEL), F32), pltpu.VMEM((EXPERT_TILE, D_MODEL), F32),
                        pltpu.SemaphoreType.DMA(()), pltpu.SemaphoreType.DMA((2,))],
        compiler_params=_cparams(("arbitrary",), 48),
        name="dispatch",
    )(meta, meta, tails, slots, h)


def _combine_kernel(meta_ref, next_meta_ref, slot_ref, wt_ref, ys_ref, x_ref, mod_ref, o_ref, local_ref, sem):
    i = pl.program_id(0)
    cur = i % 2

    def block_copy(buf):
        return lambda local_row, sorted_row, rows: pltpu.make_async_copy(
            ys_ref.at[pl.ds(sorted_row, rows)], local_ref.at[buf, pl.ds(local_row, rows)], sem.at[buf])

    def start_gather(table_ref, buf):
        local_ref[buf, TOKEN_TILE * TOP_K:, :] = jnp.zeros((LOCAL_ROWS - TOKEN_TILE * TOP_K, D_MODEL), F32)
        _for_each_block(table_ref, lambda a, b, n: block_copy(buf)(a, b, n).start())

    @pl.when(i == 0)
    def _():
        start_gather(meta_ref, 0)

    @pl.when(i + 1 < pl.num_programs(0))
    def _():
        start_gather(next_meta_ref, 1 - cur)

    _for_each_block(meta_ref, lambda a, b, n: block_copy(cur)(a, b, n).wait())
    weights = _slot_matrix(slot_ref, [wt_ref[:, k:k + 1] for k in range(TOP_K)])
    acc = jnp.dot(weights, local_ref[cur].astype(BF16), preferred_element_type=F32)
    o_ref[...] = x_ref[...] + mod_ref[0, 5:6, :] * acc


def _combine(ys, slots, meta, wts, x, modseg):
    n_tok = x.shape[0]
    per_seg = SEG_ROWS // TOKEN_TILE
    full = pl.BlockSpec((TOKEN_TILE, D_MODEL), lambda i: (i, 0))
    small = pl.BlockSpec((TOKEN_TILE, LANES), lambda i: (i, 0))
    return pl.pallas_call(
        _combine_kernel,
        grid=(n_tok // TOKEN_TILE,),
        in_specs=[pl.BlockSpec((COPY_LIST_LEN,), lambda i: (i,), memory_space=pltpu.SMEM),
                  pl.BlockSpec((COPY_LIST_LEN,), lambda i: (jnp.minimum(i + 1, n_tok // TOKEN_TILE - 1),),
                               memory_space=pltpu.SMEM),
                  small, small,
                  pl.BlockSpec(memory_space=pl.ANY),
                  full,
                  pl.BlockSpec((1, N_MOD, D_MODEL), lambda i: (i // per_seg, 0, 0))],
        out_specs=full,
        out_shape=jax.ShapeDtypeStruct((n_tok, D_MODEL), F32),
        scratch_shapes=[pltpu.VMEM((2, LOCAL_ROWS, D_MODEL), F32), pltpu.SemaphoreType.DMA((2,))],
        compiler_params=_cparams(("arbitrary",), 56),
        name="combine",
    )(meta, meta, slots, wts, ys, x, modseg)


def _expert_kernel(te_ref, first_ref, slot_ref, next_ref, na_ref, x_ref, wgu_hbm, bgu_ref, wd_hbm, bd_ref,
                   o_ref, wgu_f, wd_f, wgu_s, wd_s, sem, *, layer):
    i = pl.program_id(0)
    active = i < na_ref[0]

    def fetch(e, s):
        return (pltpu.make_async_copy(wgu_hbm.at[layer, e], wgu_f.at[s], sem.at[0, s]),
                pltpu.make_async_copy(wd_hbm.at[layer, e], wd_f.at[s], sem.at[1, s]))

    @pl.when(i == 0)
    def _():
        for cp in fetch(te_ref[0], 0):
            cp.start()

    @pl.when(first_ref[i] == 1)
    def _():
        s = slot_ref[i]
        for cp in fetch(te_ref[i], s):
            cp.wait()
        wgu_s[...] = wgu_f[s].astype(BF16)
        wd_s[...] = wd_f[s].astype(BF16)

        @pl.when(next_ref[i] >= 0)
        def _():
            for cp in fetch(next_ref[i], 1 - s):
                cp.start()

    @pl.when(active)
    def _():
        gu = jnp.dot(x_ref[...].astype(BF16), wgu_s[...], preferred_element_type=F32) + bgu_ref[0, 0]
        gate = jnp.minimum(gu[:, :D_FF], SWIGLU_LIMIT)
        up = jnp.clip(gu[:, D_FF:], -SWIGLU_LIMIT, SWIGLU_LIMIT)
        act = gate * _sigmoid(SWIGLU_ALPHA * gate) * (up + 1.0)
        o_ref[...] = jnp.dot(act.astype(BF16), wd_s[...], preferred_element_type=F32) + bd_ref[0, 0]

    @pl.when(jnp.logical_not(active))
    def _():
        o_ref[...] = jnp.zeros_like(o_ref)


def _experts(xs, tile_expert, ends, n_active, w_gu, b_gu, w_down, b_down, layer):
    n_rows = xs.shape[0]
    depth = w_gu.shape[0]
    tm = EXPERT_TILE
    n_tiles = n_rows // tm
    tile = jnp.arange(n_tiles, dtype=jnp.int32)
    active = tile < n_active[0]
    first = (active & ((tile == 0) | (tile_expert != jnp.roll(tile_expert, 1)))).astype(jnp.int32)
    slot = (_prefix_sum(first) - 1) % 2
    next_tile = ends[tile_expert] // tm
    next_expert = jnp.where(next_tile < n_active[0],
                            tile_expert[jnp.minimum(next_tile, n_tiles - 1)], -1).astype(jnp.int32)

    def tiled(shape, index):
        return pl.BlockSpec(shape, lambda i, te, fi, sl, nx, na: index(i, te, na))

    grid_spec = pltpu.PrefetchScalarGridSpec(
        num_scalar_prefetch=5,
        grid=(n_tiles,),
        in_specs=[tiled((tm, D_MODEL), lambda i, te, na: (jnp.where(i < na[0], i, 0), 0)),
                  pl.BlockSpec(memory_space=pl.ANY),
                  tiled((1, 1, 1, 2 * D_FF), lambda i, te, na: (layer, te[i], 0, 0)),
                  pl.BlockSpec(memory_space=pl.ANY),
                  tiled((1, 1, 1, D_MODEL), lambda i, te, na: (layer, te[i], 0, 0))],
        out_specs=tiled((tm, D_MODEL), lambda i, te, na: (i, 0)),
        scratch_shapes=[pltpu.VMEM((2, D_MODEL, 2 * D_FF), F32), pltpu.VMEM((2, D_FF, D_MODEL), F32),
                        pltpu.VMEM((D_MODEL, 2 * D_FF), BF16), pltpu.VMEM((D_FF, D_MODEL), BF16),
                        pltpu.SemaphoreType.DMA((2, 2))],
    )
    return pl.pallas_call(
        functools.partial(_expert_kernel, layer=layer),
        grid_spec=grid_spec,
        out_shape=jax.ShapeDtypeStruct((n_rows, D_MODEL), F32),
        compiler_params=_cparams(("arbitrary",), 56),
        name="experts",
    )(tile_expert, first, slot.astype(jnp.int32), next_expert, n_active, xs, w_gu,
      b_gu.reshape(depth, N_EXPERTS, 1, 2 * D_FF), w_down, b_down.reshape(depth, N_EXPERTS, 1, D_MODEL))


def _moe(x, norm_g, modseg, w_router, b_router, w_gu, b_gu, w_down, b_down, layer):
    n_tok = x.shape[0]
    h, slots, wts, table = _router(x, norm_g, modseg, w_router, b_router, layer)
    tm = EXPERT_TILE
    n_tiles = n_tok // TOKEN_TILE
    n_rows = n_tok * TOP_K + n_tiles * N_EXPERTS * ROW_BLOCK + N_EXPERTS * tm
    n_rows = -(-n_rows // tm) * tm
    local_start = table[:, 0, :N_EXPERTS]
    padded = table[:, 1, :N_EXPERTS]
    earlier = table[:, 2, :N_EXPERTS]
    used = earlier[-1] + padded[-1]
    size = ((used + tm - 1) // tm) * tm
    ends = _prefix_sum(size)
    starts = ends - size
    sorted_start = starts[None, :] + earlier
    meta = _copy_lists(local_start, padded, sorted_start)
    tails = jnp.concatenate([starts + used, (size - used) // ROW_BLOCK,
                             ends[-1:], (n_rows - ends[-1:]) // tm,
                             jnp.zeros((LANES - 2 * N_EXPERTS - 2,), jnp.int32)])
    tile_row0 = jnp.arange(n_rows // tm, dtype=jnp.int32) * tm
    tile_expert = jnp.minimum(jnp.sum((ends[None, :] <= tile_row0[:, None]).astype(jnp.int32), axis=1),
                              N_EXPERTS - 1)
    n_active = (ends[-1:] // tm).astype(jnp.int32)
    xs = _dispatch(h, slots, meta, tails, n_rows)
    ys = _experts(xs, tile_expert, ends, n_active, w_gu, b_gu, w_down, b_down, layer)
    return _combine(ys, slots, meta, wts, x, modseg)


def _rope_tables(n_tok):
    rows = n_tok // GRID_W
    row = jnp.repeat(jnp.arange(rows, dtype=F32), GRID_W)
    col = jnp.tile(jnp.arange(GRID_W, dtype=F32), rows)
    quarter = HEAD_DIM // 4
    inv_freq = ROPE_BASE ** (-jnp.arange(quarter, dtype=F32) / quarter)
    ang_r = row[:, None] * inv_freq
    ang_c = col[:, None] * inv_freq
    ang = jnp.concatenate([ang_r, ang_r, ang_c, ang_c], axis=-1)
    cos, sin = jnp.cos(ang), jnp.sin(ang)
    first = (jnp.arange(HEAD_DIM) % (2 * quarter)) < quarter
    sin_a = jnp.where(first, -sin, 0.0)
    sin_b = jnp.where(first, 0.0, sin)
    return tuple(jnp.tile(t, (1, LANES // HEAD_DIM)) for t in (cos, sin_a, sin_b))


def kernel(x_prompt, x_sample, c, cache_k, cache_v, state_rglru, state_hgrn, c_ctx, w_mod, b_mod, norm1_g, norm2_g, w_in, q_norm_g, k_norm_g, conv_w, conv_b, rg_wa, rg_ba, rg_wx, rg_bx, rg_lambda, hgrn_lb_logits, hgrn_norm_g, w_attn_o, w_rnn_o, w_hgrn_o, w_out, w_router, b_router, w_gu, b_gu, w_down, b_down):
    depth = w_mod.shape[0]
    n_p, t_p = x_prompt.shape[0], x_prompt.shape[1]
    n_s, t_s = x_sample.shape[0], x_sample.shape[1]
    tok_p = n_p * t_p
    tok_s = n_s * t_s
    assert tok_p % SEG_ROWS == 0 and t_s == SEG_ROWS and SEG_ROWS % t_p == 0

    x = jnp.concatenate([x_prompt.reshape(tok_p, D_MODEL), x_sample.reshape(tok_s, D_MODEL)], axis=0)

    lb_all = jnp.moveaxis(_prefix_sum(jnp.moveaxis(jax.nn.softmax(hgrn_lb_logits.astype(F32), axis=0), 0, -1)),
                          -1, 0)
    lb_all = lb_all - lb_all[:1]
    rope = _rope_tables(t_s)

    cond8 = jnp.zeros((8, D_MODEL), F32).at[0].set(c_ctx).at[1:1 + n_s].set(c)
    mod = _modulation(cond8, w_mod, b_mod)
    seg_ids = jnp.asarray([0] * (tok_p // SEG_ROWS) + [1 + b for b in range(n_s)], jnp.int32)
    modsegs = mod[:, seg_ids].reshape(depth, seg_ids.shape[0], N_MOD, D_MODEL)

    blocks = jnp.stack([rg_wa, rg_wx], axis=2)
    eye = jnp.eye(RG_BLOCKS, dtype=blocks.dtype)
    w_gate = jnp.einsum("ldgnab,nm->lnadgmb", blocks, eye).reshape(depth, D_RNN, 4 * D_RNN)
    b_gate = jnp.stack([rg_ba, rg_bx], axis=2).reshape(depth, 1, 4 * D_RNN)

    new_k, new_v, new_hr, new_s = [], [], [], []
    for l in range(depth):
        modseg = modsegs[l]
        z = _input_projection(x, norm1_g, modseg, w_in, l)
        attn_p, k_l, v_l = _attention(z, q_norm_g, k_norm_g, l, n_p, t_p, 0)
        (attn_s,) = _attention(z, q_norm_g, k_norm_g, l, n_s, t_s, tok_p, rope=rope,
                               cache=(cache_k, cache_v))
        rnn_p, hr_l = _rglru(z, conv_w, conv_b, w_gate, b_gate, rg_lambda, l, n_p, t_p, 0)
        (rnn_s,) = _rglru(z, conv_w, conv_b, w_gate, b_gate, rg_lambda, l, n_s, t_s, tok_p,
                          state=state_rglru)
        hg_p, s_l = _hgrn(z, lb_all, hgrn_norm_g, l, n_p, t_p, 0)
        (hg_s,) = _hgrn(z, lb_all, hgrn_norm_g, l, n_s, t_s, tok_p, state=state_hgrn)
        x = _merge_project(((attn_p, attn_s), (rnn_p, rnn_s), (hg_p, hg_s)), z, x, modseg,
                           w_attn_o, w_rnn_o, w_hgrn_o, w_out, l)
        x = _moe(x, norm2_g, modseg, w_router, b_router, w_gu, b_gu, w_down, b_down, l)
        new_k.append(k_l.reshape(n_p, t_p, N_KV_HEADS, HEAD_DIM))
        new_v.append(v_l.reshape(n_p, t_p, N_KV_HEADS, HEAD_DIM))
        new_hr.append(hr_l)
        new_s.append(s_l)

    y_prompt = x[:tok_p].reshape(n_p, t_p, D_MODEL)
    y_sample = x[tok_p:].reshape(n_s, t_s, D_MODEL)
    return (y_prompt, y_sample, jnp.stack(new_k, axis=1), jnp.stack(new_v, axis=1),
            jnp.stack(new_hr, axis=1), jnp.stack(new_s, axis=1))
```

```python
import functools

import jax
import jax.numpy as jnp
from jax import lax
from jax.experimental import pallas as pl
from jax.experimental.pallas import tpu as pltpu

F32 = jnp.float32
BF16 = jnp.bfloat16

D_MODEL = 1024
GRID_W = 64
HEAD_DIM = 64
N_HEADS = 8
N_KV_HEADS = 2
KV_GROUP = N_HEADS // N_KV_HEADS
ATTN_W = N_HEADS * HEAD_DIM
KV_W = N_KV_HEADS * HEAD_DIM
ROPE_BASE = 10000.0
D_RNN = 512
RG_BLOCKS = 8
RG_BW = D_RNN // RG_BLOCKS
RG_C = 8.0
HG_HEADS = 8
HG_DK = 64
HG_DV = 64
HG_W = HG_HEADS * HG_DK
N_PAIRS = HG_W // 128
N_EXPERTS = 32
TOP_K = 4
D_FF = D_MODEL
SWIGLU_LIMIT = 7.0
SWIGLU_ALPHA = 1.702
N_MOD = 6
EPS = 1e-6
IN_W = ATTN_W + 2 * KV_W + 2 * D_RNN + 5 * HG_W + 3 * D_MODEL

LANES = 128
SEG_ROWS = 1024
COL_TILE = 256
HG_CHUNK = 128
Q_TILE = 256
EXPERT_TILE = 384
BIG_BLOCK = 64
TOKEN_TILE = 512
ROW_BLOCK = 8
LOCAL_ROWS = TOKEN_TILE * TOP_K + N_EXPERTS * ROW_BLOCK
NEG_BIG = -1e30
LOG2_E = 1.4426950408889634

_SRC_TILES = tuple(range(17, 29)) + tuple(range(7, 17)) + tuple(range(3, 7)) + (0, 1, 2)
Z_GL = 0
Z_HG = 3 * D_MODEL
Z_RG = Z_HG + 5 * HG_W
Z_Q = Z_RG + 2 * D_RNN
Z_K = Z_Q + ATTN_W
Z_V = Z_K + KV_W
Z_W = -(-IN_W // (2 * COL_TILE)) * (2 * COL_TILE)


def _cparams(sem, vmem_mb):
    return pltpu.CompilerParams(dimension_semantics=sem, vmem_limit_bytes=vmem_mb * 1024 * 1024)


def _prefix_sum(x):
    n = x.shape[-1]
    upto = (jnp.arange(n)[None, :] <= jnp.arange(n)[:, None]).astype(x.dtype)
    return jnp.sum(x[..., None, :] * upto, axis=-1)


def _mm(a, b):
    return jnp.dot(a.astype(BF16), b.astype(BF16), preferred_element_type=F32)


def _mm_nt(a, b):
    return lax.dot_general(a.astype(BF16), b.astype(BF16), (((1,), (1,)), ((), ())),
                           preferred_element_type=F32)


def _mm_tn(a, b):
    return lax.dot_general(a.astype(BF16), b.astype(BF16), (((0,), (0,)), ((), ())),
                           preferred_element_type=F32)


def _sigmoid(x):
    return 1.0 / (1.0 + jnp.exp(-x))


def _softplus(x):
    return jnp.maximum(x, 0.0) + jnp.log1p(jnp.exp(-jnp.abs(x)))


def _modulated_norm(x, g, scale, shift):
    y = x * lax.rsqrt(jnp.mean(x * x, axis=-1, keepdims=True) + EPS)
    return (y * g) * (1.0 + scale) + shift


def _mod_kernel(c_ref, w_ref, b_ref, o_ref):
    c = c_ref[...]
    s = c * _sigmoid(c)
    o_ref[0] = _mm(s, w_ref[0]) + b_ref[0]


def _modulation(cond8, w_mod, b_mod):
    depth = w_mod.shape[0]
    tn = 1024
    return pl.pallas_call(
        _mod_kernel,
        grid=(depth, N_MOD * D_MODEL // tn),
        in_specs=[pl.BlockSpec((8, D_MODEL), lambda l, j: (0, 0)),
                  pl.BlockSpec((1, D_MODEL, tn), lambda l, j: (l, 0, j)),
                  pl.BlockSpec((1, 1, tn), lambda l, j: (l, 0, j))],
        out_specs=pl.BlockSpec((1, 8, tn), lambda l, j: (l, 0, j)),
        out_shape=jax.ShapeDtypeStruct((depth, 8, N_MOD * D_MODEL), F32),
        compiler_params=_cparams(("arbitrary", "arbitrary"), 32),
        name="modulation",
    )(cond8, w_mod, b_mod.reshape(depth, 1, N_MOD * D_MODEL))


def _inproj_kernel(src_ref, x_ref, g_ref, mod_ref, wa_ref, wb_ref, o_ref, h_ref, w_s):
    j = pl.program_id(1)

    @pl.when(j == 0)
    def _():
        for s in range(x_ref.shape[0] // SEG_ROWS):
            rows = slice(s * SEG_ROWS, (s + 1) * SEG_ROWS)
            h = _modulated_norm(x_ref[rows, :], g_ref[0], mod_ref[s, 1:2, :], mod_ref[s, 0:1, :])
            h_ref[rows, :] = h.astype(BF16)

    @pl.when(pl.program_id(0) == 0)
    def _():
        w_s[j, :, 0:COL_TILE] = wa_ref[0].astype(BF16)
        w_s[j, :, COL_TILE:2 * COL_TILE] = wb_ref[0].astype(BF16)

    o_ref[...] = jnp.dot(h_ref[...], w_s[j], preferred_element_type=F32)


def _input_projection(x, norm_g, modseg, w_in, layer):
    n_tok = x.shape[0]
    segs = 2
    tm = segs * SEG_ROWS
    n_col = Z_W // (2 * COL_TILE)
    src = jnp.asarray(_SRC_TILES + (_SRC_TILES[-1],) * (Z_W // COL_TILE - len(_SRC_TILES)), jnp.int32)

    def wspec(k):
        return pl.BlockSpec((1, D_MODEL, COL_TILE),
                            lambda i, j, s: (layer, 0, s[2 * jnp.where(i == 0, j, n_col - 1) + k]))

    grid_spec = pltpu.PrefetchScalarGridSpec(
        num_scalar_prefetch=1,
        grid=(n_tok // tm, n_col),
        in_specs=[pl.BlockSpec((tm, D_MODEL), lambda i, j, s: (i, 0)),
                  pl.BlockSpec((1, 1, D_MODEL), lambda i, j, s: (layer, 0, 0)),
                  pl.BlockSpec((segs, N_MOD, D_MODEL), lambda i, j, s: (i, 0, 0)),
                  wspec(0), wspec(1)],
        out_specs=pl.BlockSpec((tm, 2 * COL_TILE), lambda i, j, s: (i, j)),
        scratch_shapes=[pltpu.VMEM((tm, D_MODEL), BF16),
                        pltpu.VMEM((n_col, D_MODEL, 2 * COL_TILE), BF16)],
    )
    return pl.pallas_call(
        _inproj_kernel,
        grid_spec=grid_spec,
        out_shape=jax.ShapeDtypeStruct((n_tok, Z_W), F32),
        compiler_params=_cparams(("arbitrary", "arbitrary"), 56),
        name="input_projection",
    )(src, x, norm_g.reshape(-1, 1, D_MODEL), modseg, w_in, w_in)


def _head_rms(x, g):
    lane = lax.broadcasted_iota(jnp.int32, x.shape, 1)
    lo = lane < HEAD_DIM
    xx = x * x
    s_lo = jnp.sum(jnp.where(lo, xx, 0.0), axis=-1, keepdims=True)
    s_hi = jnp.sum(jnp.where(lo, 0.0, xx), axis=-1, keepdims=True)
    inv = jnp.where(lo, lax.rsqrt(s_lo * (1.0 / HEAD_DIM) + EPS),
                    lax.rsqrt(s_hi * (1.0 / HEAD_DIM) + EPS))
    return x * inv * g


def _rope(x, cos, sin_a, sin_b):
    q = HEAD_DIM // 4
    return x * cos + pltpu.roll(x, LANES - q, 1) * sin_a + pltpu.roll(x, q, 1) * sin_b


def _attn_kernel(*refs, t_len, use_ctx):
    if use_ctx:
        (q_ref, k_ref, v_ref, qg_ref, kg_ref, cos_q_ref, sa_q_ref, sb_q_ref,
         cos_k_ref, sa_k_ref, sb_k_ref, ck_ref, cv_ref, o_ref, kall_ref, vall_ref) = refs
    else:
        (q_ref, k_ref, v_ref, qg_ref, kg_ref, o_ref, ko_ref, vo_ref, kall_ref, vall_ref) = refs

    @pl.when(pl.program_id(1) == 0)
    def _():
        k = _head_rms(k_ref[...], kg_ref[...])
        if use_ctx:
            k = _rope(k, cos_k_ref[...], sa_k_ref[...], sb_k_ref[...])
            kall_ref[0:t_len, :] = k.astype(BF16)
            vall_ref[0:t_len, :] = v_ref[...].astype(BF16)
            kall_ref[t_len:, :] = ck_ref[0, 0].astype(BF16)
            vall_ref[t_len:, :] = cv_ref[0, 0].astype(BF16)
        else:
            ko_ref[0] = k
            vo_ref[0] = v_ref[...]
            kall_ref[...] = k.astype(BF16)
            vall_ref[...] = v_ref[...].astype(BF16)

    heads = []
    for c in range(ATTN_W // LANES):
        qc = _head_rms(q_ref[:, c * LANES:(c + 1) * LANES], qg_ref[...])
        if use_ctx:
            qc = _rope(qc, cos_q_ref[...], sa_q_ref[...], sb_q_ref[...])
        qc = (qc * (HEAD_DIM ** -0.5)).astype(BF16)
        heads.append(qc[:, :HEAD_DIM])
        heads.append(qc[:, HEAD_DIM:])
    tq = q_ref.shape[0]
    for g in range(N_KV_HEADS):
        qs = jnp.concatenate(heads[g * KV_GROUP:(g + 1) * KV_GROUP], axis=0)
        kh = kall_ref[:, g * HEAD_DIM:(g + 1) * HEAD_DIM]
        vh = vall_ref[:, g * HEAD_DIM:(g + 1) * HEAD_DIM]
        s = _mm_nt(qs, kh)
        m = jnp.max(s, axis=-1, keepdims=True)
        p = jnp.exp(s - m)
        den = jnp.sum(p, axis=-1, keepdims=True)
        o = _mm(p, vh) / den
        for j in range(KV_GROUP):
            hh = g * KV_GROUP + j
            o_ref[:, hh * HEAD_DIM:(hh + 1) * HEAD_DIM] = o[j * tq:(j + 1) * tq].astype(BF16)


def _attention(z, q_g, k_g, layer, n_batch, t_len, row0, rope=None, cache=None):
    use_ctx = cache is not None
    nq = t_len // Q_TILE
    rb0 = row0 // t_len
    qb0 = row0 // Q_TILE
    qg = jnp.tile(q_g[layer], 2).reshape(1, LANES)
    kg = jnp.tile(k_g[layer], 2).reshape(1, LANES)
    n_out_tok = n_batch * t_len
    vec = pl.BlockSpec((1, LANES), lambda b, i: (0, 0))
    in_specs = [pl.BlockSpec((Q_TILE, ATTN_W), lambda b, i: (qb0 + b * nq + i, Z_Q // ATTN_W)),
                pl.BlockSpec((t_len, KV_W), lambda b, i: (rb0 + b, Z_K // KV_W)),
                pl.BlockSpec((t_len, KV_W), lambda b, i: (rb0 + b, Z_V // KV_W)),
                vec, vec]
    args = [z, z, z, qg, kg]
    out_specs = [pl.BlockSpec((Q_TILE, ATTN_W), lambda b, i: (b * nq + i, 0))]
    out_shape = [jax.ShapeDtypeStruct((n_out_tok, ATTN_W), BF16)]
    t_keys = t_len
    if use_ctx:
        cos, sin_a, sin_b = rope
        cache_k, cache_v = cache
        past = cache_k.shape[2]
        t_keys = t_len + past
        tq_spec = pl.BlockSpec((Q_TILE, LANES), lambda b, i: (i, 0))
        tk_spec = pl.BlockSpec((t_len, LANES), lambda b, i: (0, 0))
        c_spec = pl.BlockSpec((1, 1, past, KV_W), lambda b, i: (b, layer, 0, 0))
        in_specs += [tq_spec, tq_spec, tq_spec, tk_spec, tk_spec, tk_spec, c_spec, c_spec]
        args += [cos, sin_a, sin_b, cos, sin_a, sin_b,
                 cache_k.reshape(cache_k.shape[0], cache_k.shape[1], past, KV_W),
                 cache_v.reshape(cache_v.shape[0], cache_v.shape[1], past, KV_W)]
    else:
        kv_spec = pl.BlockSpec((1, t_len, KV_W), lambda b, i: (b, 0, 0))
        out_specs += [kv_spec, kv_spec]
        out_shape += [jax.ShapeDtypeStruct((n_batch, t_len, KV_W), F32)] * 2
    return pl.pallas_call(
        functools.partial(_attn_kernel, t_len=t_len, use_ctx=use_ctx),
        grid=(n_batch, nq),
        in_specs=in_specs,
        out_specs=out_specs,
        out_shape=out_shape,
        scratch_shapes=[pltpu.VMEM((t_keys, KV_W), BF16), pltpu.VMEM((t_keys, KV_W), BF16)],
        compiler_params=_cparams(("arbitrary", "arbitrary"), 48),
        name="attention_ctx" if use_ctx else "attention",
    )(*args)


def _shift_rows(x, d):
    n = x.shape[0]
    row = lax.broadcasted_iota(jnp.int32, x.shape, 0)
    y = pltpu.roll(x, d % n, 0)
    if d > 0:
        return jnp.where(row >= d, y, 0.0)
    return jnp.where(row < n + d, y, 0.0)


def _linear_scan(a, u, reverse):
    n, w = a.shape
    row = lax.broadcasted_iota(jnp.int32, a.shape, 0)

    def shifted(x, d, fill):
        if d % ROW_BLOCK == 0:
            pad = jnp.full((d, w), fill, x.dtype)
            return jnp.concatenate([x[d:], pad] if reverse else [pad, x[:n - d]], axis=0)
        if reverse:
            return jnp.where(row < n - d, pltpu.roll(x, n - d, 0), fill)
        return jnp.where(row >= d, pltpu.roll(x, d, 0), fill)

    d = 1
    while d < n:
        u = a * shifted(u, d, 0.0) + u
        a = a * shifted(a, d, 1.0)
        d *= 2
    return a, u


def _rglru_kernel(*refs, use_ctx):
    if use_ctx:
        (x_ref, y_ref, cw_ref, cb_ref, wg_ref, bg_ref, lam_ref, h0_ref, o_ref) = refs
    else:
        (x_ref, y_ref, cw_ref, cb_ref, wg_ref, bg_ref, lam_ref, o_ref, last_ref) = refs
    x = x_ref[...]
    cw = cw_ref[0]
    xr = cb_ref[0] + _shift_rows(x, 1) * cw[0:1] + x * cw[1:2] \
        + _shift_rows(x, -1) * cw[2:3] + _shift_rows(x, -2) * cw[3:4]
    gates = _sigmoid(_mm(xr, wg_ref[0]) + bg_ref[0])
    n = x.shape[0]
    total = None
    lasts = []
    for d in range(2):
        r = gates[:, (2 * d) * D_RNN:(2 * d + 1) * D_RNN]
        i = gates[:, (2 * d + 1) * D_RNN:(2 * d + 2) * D_RNN]
        log_a = (-RG_C * _softplus(-lam_ref[0, d:d + 1, :])) * r
        a = jnp.exp(log_a)
        u = jnp.sqrt(1.0 - a * a) * (i * xr)
        a_cum, h = _linear_scan(a, u, reverse=(d == 1))
        if use_ctx:
            h = h + a_cum * h0_ref[0, 0, d:d + 1, :]
        else:
            lasts.append(h[0:1] if d == 1 else h[n - 1:n])
        total = h if total is None else total + h
    o_ref[...] = (jax.nn.gelu(y_ref[...]) * total).astype(BF16)
    if not use_ctx:
        last_ref[0] = jnp.concatenate(lasts, axis=0)


def _rglru(z, conv_w, conv_b, w_gate, b_gate, lam, layer, n_batch, t_len, row0, state=None):
    use_ctx = state is not None
    rb0 = row0 // t_len
    in_specs = [pl.BlockSpec((t_len, D_RNN), lambda b: (rb0 + b, Z_RG // D_RNN)),
                pl.BlockSpec((t_len, D_RNN), lambda b: (rb0 + b, Z_RG // D_RNN + 1)),
                pl.BlockSpec((1, 4, D_RNN), lambda b: (layer, 0, 0)),
                pl.BlockSpec((1, 1, D_RNN), lambda b: (layer, 0, 0)),
                pl.BlockSpec((1, D_RNN, 4 * D_RNN), lambda b: (layer, 0, 0)),
                pl.BlockSpec((1, 1, 4 * D_RNN), lambda b: (layer, 0, 0)),
                pl.BlockSpec((1, 2, D_RNN), lambda b: (layer, 0, 0))]
    args = [z, z, conv_w, conv_b.reshape(-1, 1, D_RNN), w_gate, b_gate, lam]
    out_specs = [pl.BlockSpec((t_len, D_RNN), lambda b: (b, 0))]
    out_shape = [jax.ShapeDtypeStruct((n_batch * t_len, D_RNN), BF16)]
    if use_ctx:
        in_specs.append(pl.BlockSpec((1, 1, 2, D_RNN), lambda b: (b, layer, 0, 0)))
        args.append(state)
    else:
        out_specs.append(pl.BlockSpec((1, 2, D_RNN), lambda b: (b, 0, 0)))
        out_shape.append(jax.ShapeDtypeStruct((n_batch, 2, D_RNN), F32))
    return pl.pallas_call(
        functools.partial(_rglru_kernel, use_ctx=use_ctx),
        grid=(n_batch,),
        in_specs=in_specs,
        out_specs=out_specs,
        out_shape=out_shape,
        compiler_params=_cparams(("arbitrary",), 56),
        name="rglru_ctx" if use_ctx else "rglru",
    )(*args)


def _cumsum_rows(g, reverse):
    n, w = g.shape
    m = n // ROW_BLOCK

    def scan(x, size, pos):
        d = 1
        while d < size:
            if reverse:
                x = x + jnp.where(pos < size - d, pltpu.roll(x, x.shape[0] - d, 0), 0.0)
            else:
                x = x + jnp.where(pos >= d, pltpu.roll(x, d, 0), 0.0)
            d *= 2
        return x

    g = scan(g, ROW_BLOCK, lax.broadcasted_iota(jnp.int32, g.shape, 0) & (ROW_BLOCK - 1))
    g3 = g.reshape(m, ROW_BLOCK, w)
    total = g3[:, 0, :] if reverse else g3[:, ROW_BLOCK - 1, :]
    before = scan(total, m, lax.broadcasted_iota(jnp.int32, total.shape, 0)) - total
    return (g3 + before[:, None, :]).reshape(n, w)


def _block_row_fn(gc, reverse):
    n, w = gc.shape
    g3 = gc.reshape(n // ROW_BLOCK, ROW_BLOCK, w)
    edge_row = 0 if reverse else ROW_BLOCK - 1
    edge = jnp.broadcast_to(g3[:, edge_row:edge_row + 1, :], g3.shape)
    sub = lax.broadcasted_iota(jnp.int32, g3.shape, 1)

    def at(blk):
        r = blk // 2 if reverse else blk // 2 - 1
        if blk > ROW_BLOCK:
            groups = blk // ROW_BLOCK
            e4 = edge.reshape(n // blk, groups, ROW_BLOCK, w)
            pick = r // ROW_BLOCK
            return jnp.broadcast_to(e4[:, pick:pick + 1], e4.shape).reshape(n, w)
        out = jnp.broadcast_to(g3[:, r:r + 1, :], g3.shape)
        for b0 in range(blk, ROW_BLOCK, blk):
            out = jnp.where(sub >= b0, jnp.broadcast_to(g3[:, b0 + r:b0 + r + 1, :], g3.shape), out)
        return out.reshape(n, w)

    return at


def _first_head(shape):
    lane = lax.broadcasted_iota(jnp.int32, shape, len(shape) - 1)
    return (lane & (LANES - 1)) < HG_DK


def _pair(x, p):
    return x[:, p * LANES:(p + 1) * LANES]


def _hgrn_gates(f_logit, lb):
    y = jnp.exp(-jnp.abs(f_logit))
    log_sig = jnp.minimum(f_logit, 0.0) - jnp.log(1.0 + y)
    a = jnp.log(lb)
    b = jnp.log1p(-lb) + log_sig
    log_f = jnp.maximum(a, b) + jnp.log(1.0 + jnp.exp(-jnp.abs(a - b)))
    k = (1.0 - lb) * (jnp.where(f_logit >= 0.0, y, 1.0) / (1.0 + y))
    return log_f * LOG2_E, k


def _hgrn_chunk_state(gc, k, v, reverse):
    n = gc.shape[0]
    g_tot = gc[0:1] if reverse else gc[n - 1:n]
    k_end = (k * jnp.exp2(jnp.minimum(g_tot - gc, 0.0))).astype(BF16)
    vb = v.astype(BF16)
    same_head = (_first_head((LANES, LANES))
                 == (lax.broadcasted_iota(jnp.int32, (LANES, LANES), 0) < HG_DK))
    ds = [jnp.where(same_head, _mm_tn(_pair(k_end, p), _pair(vb, p)), 0.0) for p in range(N_PAIRS)]
    return ds, jnp.exp2(g_tot)


def _decay_state(s, e_row):
    n = s.shape[0]
    eye = (lax.broadcasted_iota(jnp.int32, (n, n), 0) == lax.broadcasted_iota(jnp.int32, (n, n), 1))
    col = jnp.sum(jnp.where(eye, jnp.broadcast_to(e_row, (n, n)), 0.0), axis=1, keepdims=True)
    return s * col


def _hgrn_intra_scores(q, ks, gcs, txs):
    n = q.shape[0]
    row = lax.broadcasted_iota(jnp.int32, q.shape, 0)
    first = _first_head(q.shape)
    boundary = [_block_row_fn(gcs[0], False), _block_row_fn(gcs[1], True)]
    scores = [None] * HG_HEADS
    w = 1
    while w < n:
        blk = 2 * w
        second = (row & w) != 0
        e_f = jnp.exp2(-jnp.abs(gcs[0] - boundary[0](blk)))
        e_b = jnp.exp2(-jnp.abs(gcs[1] - boundary[1](blk)))
        q_f = jnp.where(second, q * e_f, 0.0).astype(BF16)
        q_b = jnp.where(second, 0.0, q * e_b).astype(BF16)
        k_f = jnp.where(second, 0.0, ks[0] * e_f)
        k_b = jnp.where(second, ks[1] * e_b, 0.0)
        k_heads = [(jnp.where(first, k_f, 0.0).astype(BF16), jnp.where(first, k_b, 0.0).astype(BF16)),
                   (jnp.where(first, 0.0, k_f).astype(BF16), jnp.where(first, 0.0, k_b).astype(BF16))]
        sel = (txs >= w) & (txs < blk)
        for p in range(N_PAIRS):
            qc = jnp.concatenate([_pair(q_f, p), _pair(q_b, p)], axis=1)
            for j in range(2):
                kc = jnp.concatenate([_pair(k_heads[j][0], p), _pair(k_heads[j][1], p)], axis=1)
                s = jnp.where(sel, _mm_nt(qc, kc), 0.0)
                h = 2 * p + j
                scores[h] = s if scores[h] is None else scores[h] + s
        w = blk
    return scores


def _hgrn_kernel(*refs, t_len, use_ctx):
    if use_ctx:
        (q_ref, ff_ref, fb_ref, i_ref, og_ref, lb_ref, ng_ref, s0_ref, o_ref, sin_ref, gk_ref) = refs
    else:
        (q_ref, ff_ref, fb_ref, i_ref, og_ref, lb_ref, ng_ref, o_ref, sl_ref, sin_ref, gk_ref) = refs
    c = HG_CHUNK
    n_chunks = t_len // c
    f_refs = (ff_ref, fb_ref)

    zero = jnp.zeros((HG_DK, HG_DV), F32)
    for d in range(2):
        states = []
        for p in range(N_PAIRS):
            if use_ctx:
                top = jnp.concatenate([s0_ref[0, 0, d, 2 * p], zero], axis=1)
                bot = jnp.concatenate([zero, s0_ref[0, 0, d, 2 * p + 1]], axis=1)
                states.append(jnp.concatenate([top, bot], axis=0))
            else:
                states.append(jnp.zeros((LANES, LANES), F32))
        order = range(n_chunks) if d == 0 else range(n_chunks - 1, -1, -1)
        for ci in order:
            for p in range(N_PAIRS):
                sin_ref[ci, p, d * LANES:(d + 1) * LANES, :] = states[p]
            log_f, k = _hgrn_gates(f_refs[d][pl.ds(ci * c, c), :], lb_ref[0, d:d + 1, :])
            gc = _cumsum_rows(log_f, reverse=(d == 1))
            gk_ref[d, 0, pl.ds(ci * c, c), :] = gc
            gk_ref[d, 1, pl.ds(ci * c, c), :] = k
            ds, e_tot = _hgrn_chunk_state(gc, k, i_ref[pl.ds(ci * c, c), :], reverse=(d == 1))
            states = [_decay_state(states[p], _pair(e_tot, p)) + ds[p] for p in range(N_PAIRS)]
        if not use_ctx:
            for p in range(N_PAIRS):
                sl_ref[0, d, 2 * p] = states[p][:HG_DK, :HG_DV]
                sl_ref[0, d, 2 * p + 1] = states[p][HG_DK:, HG_DV:]

    ti = lax.broadcasted_iota(jnp.int32, (c, c), 0)
    si = lax.broadcasted_iota(jnp.int32, (c, c), 1)
    txs = ti ^ si
    diag = ti == si

    def chunk_body(ci, carry):
        r0 = pl.multiple_of(ci * c, c)
        hq = q_ref[pl.ds(r0, c), :]
        q = hq * _sigmoid(hq)
        v = i_ref[pl.ds(r0, c), :]
        gcs = [gk_ref[d, 0, pl.ds(r0, c), :] for d in range(2)]
        ks = [gk_ref[d, 1, pl.ds(r0, c), :] for d in range(2)]
        scores = _hgrn_intra_scores(q, ks, gcs, txs)
        first = _first_head(q.shape)
        k_sum = ks[0] + ks[1]
        v_heads = (jnp.where(first, v, 0.0).astype(BF16), jnp.where(first, 0.0, v).astype(BF16))
        k_heads = (jnp.where(first, k_sum, 0.0).astype(BF16), jnp.where(first, 0.0, k_sum).astype(BF16))
        q_bf = q.astype(BF16)
        qe = [(q * jnp.exp2(gcs[d])).astype(BF16) for d in range(2)]
        outs = []
        for p in range(N_PAIRS):
            o = _mm(jnp.concatenate([_pair(qe[0], p), _pair(qe[1], p)], axis=1), sin_ref[ci, p])
            for j in range(2):
                s_diag = jnp.where(diag, _mm_nt(_pair(q_bf, p), _pair(k_heads[j], p)), 0.0)
                o = o + _mm(scores[2 * p + j] + s_diag, _pair(v_heads[j], p))
            outs.append(_head_rms(o, ng_ref[0]))
        og = og_ref[pl.ds(r0, c), :]
        o_all = jnp.concatenate(outs, axis=1) * (og * _sigmoid(og))
        o_ref[pl.ds(r0, c), :] = o_all.astype(BF16)
        return carry

    lax.fori_loop(0, n_chunks, chunk_body, 0)


def _hgrn(z, lb, norm_g, layer, n_batch, t_len, row0, state=None):
    use_ctx = state is not None
    rb0 = row0 // t_len
    cb = Z_HG // HG_W

    def zspec(k):
        return pl.BlockSpec((t_len, HG_W), lambda b: (rb0 + b, cb + k))

    in_specs = [zspec(0), zspec(1), zspec(2), zspec(3), zspec(4),
                pl.BlockSpec((1, 2, HG_W), lambda b: (layer, 0, 0)),
                pl.BlockSpec((1, 1, LANES), lambda b: (layer, 0, 0))]
    args = [z, z, z, z, z, lb, jnp.tile(norm_g, (1, LANES // HG_DV)).reshape(-1, 1, LANES)]
    out_specs = [pl.BlockSpec((t_len, HG_W), lambda b: (b, 0))]
    out_shape = [jax.ShapeDtypeStruct((n_batch * t_len, HG_W), BF16)]
    scratch = [pltpu.VMEM((t_len // HG_CHUNK, N_PAIRS, 2 * LANES, LANES), F32),
               pltpu.VMEM((2, 2, t_len, HG_W), F32)]
    if use_ctx:
        in_specs.append(pl.BlockSpec((1, 1, 2, HG_HEADS, HG_DK, HG_DV), lambda b: (b, layer, 0, 0, 0, 0)))
        args.append(state)
    else:
        out_specs.append(pl.BlockSpec((1, 2, HG_HEADS, HG_DK, HG_DV), lambda b: (b, 0, 0, 0, 0)))
        out_shape.append(jax.ShapeDtypeStruct((n_batch, 2, HG_HEADS, HG_DK, HG_DV), F32))
    return pl.pallas_call(
        functools.partial(_hgrn_kernel, t_len=t_len, use_ctx=use_ctx),
        grid=(n_batch,),
        in_specs=in_specs,
        out_specs=out_specs,
        out_shape=out_shape,
        scratch_shapes=scratch,
        compiler_params=_cparams(("arbitrary",), 56),
        name="hgrn_ctx" if use_ctx else "hgrn",
    )(*args)


def _merge_kernel(ap_ref, as_ref, rp_ref, rs_ref, hp_ref, hs_ref, g0_ref, g1_ref, g2_ref, x_ref, mod_ref,
                  wa_ref, wr_ref, wh_ref, wo_ref, o_ref, wa_s, wr_s, wh_s, wo_s, *, prompt_tiles):
    @pl.when(pl.program_id(0) == 0)
    def _():
        wa_s[...] = wa_ref[0].astype(BF16)
        wr_s[...] = wr_ref[0].astype(BF16)
        wh_s[...] = wh_ref[0].astype(BF16)
        wo_s[...] = wo_ref[0].astype(BF16)

    is_prompt = pl.program_id(0) < prompt_tiles

    def proj(p_ref, s_ref, w_s):
        return jnp.dot(jnp.where(is_prompt, p_ref[...], s_ref[...]), w_s[...], preferred_element_type=F32)

    merged = (_sigmoid(g0_ref[...]) * proj(ap_ref, as_ref, wa_s)
              + _sigmoid(g1_ref[...]) * proj(rp_ref, rs_ref, wr_s)
              + _sigmoid(g2_ref[...]) * proj(hp_ref, hs_ref, wh_s))
    out = jnp.dot(merged.astype(BF16), wo_s[...], preferred_element_type=F32)
    o_ref[...] = x_ref[...] + mod_ref[0, 2:3, :] * out


def _merge_project(mixed, z, x, modseg, w_attn_o, w_rnn_o, w_hgrn_o, w_out, layer):
    n_tok = x.shape[0]
    tm = 512
    per_seg = SEG_ROWS // tm
    prompt_tiles = mixed[0][0].shape[0] // tm
    half_p = pl.BlockSpec((tm, ATTN_W), lambda i: (jnp.minimum(i, prompt_tiles - 1), 0))
    half_s = pl.BlockSpec((tm, ATTN_W), lambda i: (jnp.maximum(i - prompt_tiles, 0), 0))
    full = pl.BlockSpec((tm, D_MODEL), lambda i: (i, 0))

    def gspec(k):
        return pl.BlockSpec((tm, D_MODEL), lambda i: (i, Z_GL // D_MODEL + k))

    def wspec(rows):
        return pl.BlockSpec((1, rows, D_MODEL), lambda i: (layer, 0, 0))

    return pl.pallas_call(
        functools.partial(_merge_kernel, prompt_tiles=prompt_tiles),
        grid=(n_tok // tm,),
        in_specs=[half_p, half_s, half_p, half_s, half_p, half_s, gspec(0), gspec(1), gspec(2), full,
                  pl.BlockSpec((1, N_MOD, D_MODEL), lambda i: (i // per_seg, 0, 0)),
                  wspec(ATTN_W), wspec(D_RNN), wspec(HG_W), wspec(D_MODEL)],
        out_specs=full,
        out_shape=jax.ShapeDtypeStruct((n_tok, D_MODEL), F32),
        scratch_shapes=[pltpu.VMEM((ATTN_W, D_MODEL), BF16), pltpu.VMEM((D_RNN, D_MODEL), BF16),
                        pltpu.VMEM((HG_W, D_MODEL), BF16), pltpu.VMEM((D_MODEL, D_MODEL), BF16)],
        compiler_params=_cparams(("arbitrary",), 56),
        name="merge_project",
    )(*[part for pair in mixed for part in pair], z, z, z, x, modseg, w_attn_o, w_rnn_o, w_hgrn_o, w_out)


def _router_kernel(x_ref, g_ref, mod_ref, wr_ref, br_ref, h_ref, slot_ref, wt_ref, meta_ref, run_ref):
    @pl.when(pl.program_id(0) == 0)
    def _():
        run_ref[...] = jnp.zeros_like(run_ref)

    h = _modulated_norm(x_ref[...], g_ref[0], mod_ref[0, 4:5, :], mod_ref[0, 3:4, :])
    h_ref[...] = h
    h_hi = h.astype(BF16)
    h_lo = (h - h_hi.astype(F32)).astype(BF16)
    w = wr_ref[0]
    w_hi = w.astype(BF16)
    w_lo = (w - w_hi.astype(F32)).astype(BF16)
    logits = (jnp.dot(h_hi, w_hi, preferred_element_type=F32) + jnp.dot(h_hi, w_lo, preferred_element_type=F32)
              + jnp.dot(h_lo, w_hi, preferred_element_type=F32)) + br_ref[0]
    tm = logits.shape[0]
    lane = lax.broadcasted_iota(jnp.int32, logits.shape, 1)
    ids, vals = [], []
    chosen = jnp.zeros(logits.shape, jnp.bool_)
    work = logits
    for _ in range(TOP_K):
        m = jnp.max(work, axis=-1, keepdims=True)
        idx = jnp.min(jnp.where(work == m, lane, LANES), axis=-1, keepdims=True)
        hit = lane == idx
        chosen = chosen | hit
        work = jnp.where(hit, -jnp.inf, work)
        ids.append(idx)
        vals.append(m)
    exps = [jnp.exp(v - vals[0]) for v in vals]
    den = exps[0] + exps[1] + exps[2] + exps[3]
    ind = jnp.where(chosen, 1.0, 0.0)
    r_i = lax.broadcasted_iota(jnp.int32, (tm, tm), 0)
    c_i = lax.broadcasted_iota(jnp.int32, (tm, tm), 1)
    count = jnp.sum(ind, axis=0, keepdims=True)
    padded = jnp.floor((count + (ROW_BLOCK - 1)) * (1.0 / ROW_BLOCK)) * ROW_BLOCK
    incl = jnp.broadcast_to(padded, (8, LANES))
    lane8 = lax.broadcasted_iota(jnp.int32, (8, LANES), 1)
    d = 1
    while d < LANES:
        incl = incl + jnp.where(lane8 >= d, pltpu.roll(incl, d, 1), 0.0)
        d *= 2
    local_start = incl[0:1] - padded
    place = _mm(jnp.where(c_i < r_i, 1.0, 0.0), ind) + local_start
    slot_out = jnp.zeros(logits.shape, jnp.int32)
    wt_out = jnp.zeros(logits.shape, F32)
    for k in range(TOP_K):
        slot = jnp.sum(jnp.where(lane == ids[k], place, 0.0), axis=-1, keepdims=True)
        slot_out = jnp.where(lane == k, slot.astype(jnp.int32), slot_out)
        wt_out = jnp.where(lane == k, exps[k] / den, wt_out)
    slot_ref[...] = slot_out
    wt_ref[...] = wt_out
    row8 = lax.broadcasted_iota(jnp.int32, (8, LANES), 0)
    table = jnp.where(row8 == 0, local_start, jnp.where(row8 == 1, padded, run_ref[0:1, :]))
    meta_ref[0] = jnp.where(row8 < 3, table, 0.0).astype(jnp.int32)
    run_ref[...] = run_ref[...] + padded


def _router(x, norm_g, modseg, w_router, b_router, layer):
    n_tok = x.shape[0]
    tm = TOKEN_TILE
    per_seg = SEG_ROWS // tm
    depth = w_router.shape[0]
    w_pad = jnp.zeros((depth, D_MODEL, LANES), F32).at[:, :, :N_EXPERTS].set(w_router)
    b_pad = jnp.full((depth, 1, LANES), NEG_BIG, F32).at[:, 0, :N_EXPERTS].set(b_router)
    full = pl.BlockSpec((tm, D_MODEL), lambda i: (i, 0))
    small = pl.BlockSpec((tm, LANES), lambda i: (i, 0))
    return pl.pallas_call(
        _router_kernel,
        grid=(n_tok // tm,),
        in_specs=[full,
                  pl.BlockSpec((1, 1, D_MODEL), lambda i: (layer, 0, 0)),
                  pl.BlockSpec((1, N_MOD, D_MODEL), lambda i: (i // per_seg, 0, 0)),
                  pl.BlockSpec((1, D_MODEL, LANES), lambda i: (layer, 0, 0)),
                  pl.BlockSpec((1, 1, LANES), lambda i: (layer, 0, 0))],
        out_specs=[full, small, small, pl.BlockSpec((1, 8, LANES), lambda i: (i, 0, 0))],
        out_shape=[jax.ShapeDtypeStruct((n_tok, D_MODEL), F32),
                   jax.ShapeDtypeStruct((n_tok, LANES), jnp.int32),
                   jax.ShapeDtypeStruct((n_tok, LANES), F32),
                   jax.ShapeDtypeStruct((n_tok // tm, 8, LANES), jnp.int32)],
        scratch_shapes=[pltpu.VMEM((8, LANES), F32)],
        compiler_params=_cparams(("arbitrary",), 40),
        name="router",
    )(x, norm_g.reshape(-1, 1, D_MODEL), modseg, w_pad, b_pad)


MAX_BIG = LOCAL_ROWS // BIG_BLOCK
MAX_SMALL = N_EXPERTS * (BIG_BLOCK // ROW_BLOCK - 1)
COPY_LIST_LEN = 1024
_BIG_AT = 2
_SMALL_AT = 2 + 2 * MAX_BIG
assert _SMALL_AT + 2 * MAX_SMALL <= COPY_LIST_LEN


def _copy_lists(local_start, padded, sorted_start):
    n_big = padded // BIG_BLOCK
    n_small = (padded - n_big * BIG_BLOCK) // ROW_BLOCK

    def expand(count, local0, sorted0, step, cap):
        last = _prefix_sum(count)[:, None, :]
        first = last - count[:, None, :]
        f = jnp.arange(cap, dtype=jnp.int32)[None, :, None]
        mine = ((first <= f) & (f < last)).astype(jnp.int32)
        offset = step * (f - first)
        return (last[:, 0, -1:], jnp.sum(mine * (local0[:, None, :] + offset), axis=2),
                jnp.sum(mine * (sorted0[:, None, :] + offset), axis=2))

    cnt_b, loc_b, srt_b = expand(n_big, local_start, sorted_start, BIG_BLOCK, MAX_BIG)
    done = n_big * BIG_BLOCK
    cnt_s, loc_s, srt_s = expand(n_small, local_start + done, sorted_start + done, ROW_BLOCK, MAX_SMALL)
    parts = [cnt_b, cnt_s, loc_b, srt_b, loc_s, srt_s]
    used = sum(p.shape[1] for p in parts)
    parts.append(jnp.zeros((padded.shape[0], COPY_LIST_LEN - used), jnp.int32))
    return jnp.concatenate(parts, axis=1).astype(jnp.int32).reshape(-1)


def _for_each_block(list_ref, fn):
    def big(j, c):
        fn(pl.multiple_of(list_ref[_BIG_AT + j], ROW_BLOCK),
           pl.multiple_of(list_ref[_BIG_AT + MAX_BIG + j], ROW_BLOCK), BIG_BLOCK)
        return c

    def small(j, c):
        fn(pl.multiple_of(list_ref[_SMALL_AT + j], ROW_BLOCK),
           pl.multiple_of(list_ref[_SMALL_AT + MAX_SMALL + j], ROW_BLOCK), ROW_BLOCK)
        return c

    lax.fori_loop(0, list_ref[0], big, 0)
    lax.fori_loop(0, list_ref[1], small, 0)


def _slot_matrix(slot_ref, values):
    col = lax.broadcasted_iota(jnp.int32, (TOKEN_TILE, LOCAL_ROWS), 1)
    out = jnp.zeros((TOKEN_TILE, LOCAL_ROWS), F32)
    for k in range(TOP_K):
        out = jnp.where(col == slot_ref[:, k:k + 1], values[k], out)
    return out.astype(BF16)


def _dispatch_kernel(meta_ref, prev_meta_ref, tail_ref, slot_ref, h_ref, xs_ref, local_ref, zero_ref,
                     sem, block_sem):
    i = pl.program_id(0)
    cur = i % 2

    @pl.when(i == 0)
    def _():
        zero_ref[...] = jnp.zeros_like(zero_ref)

        def tails(fn):
            def group(e, carry):
                def block(j, c):
                    fn(pl.multiple_of(tail_ref[e] + j * ROW_BLOCK, ROW_BLOCK))
                    return c
                return lax.fori_loop(0, tail_ref[N_EXPERTS + e], block, carry)
            lax.fori_loop(0, N_EXPERTS, group, 0)

        def zero_copy(row):
            return pltpu.make_async_copy(zero_ref.at[pl.ds(0, ROW_BLOCK)],
                                         xs_ref.at[pl.ds(row, ROW_BLOCK)], sem)

        tails(lambda row: zero_copy(row).start())
        tails(lambda row: zero_copy(row).wait())

        def unused(fn):
            def tile(j, c):
                fn(pl.multiple_of(tail_ref[2 * N_EXPERTS] + j * EXPERT_TILE, EXPERT_TILE))
                return c
            lax.fori_loop(0, tail_ref[2 * N_EXPERTS + 1], tile, 0)

        def zero_tile(row):
            return pltpu.make_async_copy(zero_ref, xs_ref.at[pl.ds(row, EXPERT_TILE)], sem)

        unused(lambda row: zero_tile(row).start())
        unused(lambda row: zero_tile(row).wait())

    onehot = _slot_matrix(slot_ref, [1.0] * TOP_K)
    local_ref[cur] = _mm_tn(onehot, h_ref[...])

    def block_copy(buf):
        return lambda local_row, sorted_row, rows: pltpu.make_async_copy(
            local_ref.at[buf, pl.ds(local_row, rows)], xs_ref.at[pl.ds(sorted_row, rows)],
            block_sem.at[buf])

    _for_each_block(meta_ref, lambda a, b, n: block_copy(cur)(a, b, n).start())

    @pl.when(i > 0)
    def _():
        _for_each_block(prev_meta_ref, lambda a, b, n: block_copy(1 - cur)(a, b, n).wait())

    @pl.when(i == pl.num_programs(0) - 1)
    def _():
        _for_each_block(meta_ref, lambda a, b, n: block_copy(cur)(a, b, n).wait())


def _dispatch(h, slots, meta, tails, n_rows):
    n_tok = h.shape[0]
    return pl.pallas_call(
        _dispatch_kernel,
        grid=(n_tok // TOKEN_TILE,),
        in_specs=[pl.BlockSpec((COPY_LIST_LEN,), lambda i: (i,), memory_space=pltpu.SMEM),
                  pl.BlockSpec((COPY_LIST_LEN,), lambda i: (jnp.maximum(i - 1, 0),), memory_space=pltpu.SMEM),
                  pl.BlockSpec((LANES,), lambda i: (0,), memory_space=pltpu.SMEM),
                  pl.BlockSpec((TOKEN_TILE, LANES), lambda i: (i, 0)),
                  pl.BlockSpec((TOKEN_TILE, D_MODEL), lambda i: (i, 0))],
        out_specs=pl.BlockSpec(memory_space=pl.ANY),
        out_shape=jax.ShapeDtypeStruct((n_rows, D_MODEL), F32),
        scratch_shapes=[pltpu.VMEM((2, LOCAL_ROWS, D_MODEL), F32), pltpu.VMEM((EXPERT_TILE, D_MODEL), F32),
                        pltpu.SemaphoreType.DMA(()), pltpu.SemaphoreType.DMA((2,))],
        compiler_params=_cparams(("arbitrary",), 48),
        name="dispatch",
    )(meta, meta, tails, slots, h)


def _combine_kernel(meta_ref, next_meta_ref, slot_ref, wt_ref, ys_ref, x_ref, mod_ref, o_ref, local_ref, sem):
    i = pl.program_id(0)
    cur = i % 2

    def block_copy(buf):
        return lambda local_row, sorted_row, rows: pltpu.make_async_copy(
            ys_ref.at[pl.ds(sorted_row, rows)], local_ref.at[buf, pl.ds(local_row, rows)], sem.at[buf])

    def start_gather(table_ref, buf):
        local_ref[buf, TOKEN_TILE * TOP_K:, :] = jnp.zeros((LOCAL_ROWS - TOKEN_TILE * TOP_K, D_MODEL), F32)
        _for_each_block(table_ref, lambda a, b, n: block_copy(buf)(a, b, n).start())

    @pl.when(i == 0)
    def _():
        start_gather(meta_ref, 0)

    @pl.when(i + 1 < pl.num_programs(0))
    def _():
        start_gather(next_meta_ref, 1 - cur)

    _for_each_block(meta_ref, lambda a, b, n: block_copy(cur)(a, b, n).wait())
    weights = _slot_matrix(slot_ref, [wt_ref[:, k:k + 1] for k in range(TOP_K)])
    acc = jnp.dot(weights, local_ref[cur].astype(BF16), preferred_element_type=F32)
    o_ref[...] = x_ref[...] + mod_ref[0, 5:6, :] * acc


def _combine(ys, slots, meta, wts, x, modseg):
    n_tok = x.shape[0]
    per_seg = SEG_ROWS // TOKEN_TILE
    full = pl.BlockSpec((TOKEN_TILE, D_MODEL), lambda i: (i, 0))
    small = pl.BlockSpec((TOKEN_TILE, LANES), lambda i: (i, 0))
    return pl.pallas_call(
        _combine_kernel,
        grid=(n_tok // TOKEN_TILE,),
        in_specs=[pl.BlockSpec((COPY_LIST_LEN,), lambda i: (i,), memory_space=pltpu.SMEM),
                  pl.BlockSpec((COPY_LIST_LEN,), lambda i: (jnp.minimum(i + 1, n_tok // TOKEN_TILE - 1),),
                               memory_space=pltpu.SMEM),
                  small, small,
                  pl.BlockSpec(memory_space=pl.ANY),
                  full,
                  pl.BlockSpec((1, N_MOD, D_MODEL), lambda i: (i // per_seg, 0, 0))],
        out_specs=full,
        out_shape=jax.ShapeDtypeStruct((n_tok, D_MODEL), F32),
        scratch_shapes=[pltpu.VMEM((2, LOCAL_ROWS, D_MODEL), F32), pltpu.SemaphoreType.DMA((2,))],
        compiler_params=_cparams(("arbitrary",), 56),
        name="combine",
    )(meta, meta, slots, wts, ys, x, modseg)


def _expert_kernel(te_ref, first_ref, slot_ref, next_ref, na_ref, x_ref, wgu_hbm, bgu_ref, wd_hbm, bd_ref,
                   o_ref, wgu_f, wd_f, wgu_s, wd_s, sem, *, layer):
    i = pl.program_id(0)
    active = i < na_ref[0]

    def fetch(e, s):
        return (pltpu.make_async_copy(wgu_hbm.at[layer, e], wgu_f.at[s], sem.at[0, s]),
                pltpu.make_async_copy(wd_hbm.at[layer, e], wd_f.at[s], sem.at[1, s]))

    @pl.when(jnp.logical_and(i == 0, first_ref[0] == 1))
    def _():
        for cp in fetch(te_ref[0], 0):
            cp.start()

    @pl.when(first_ref[i] == 1)
    def _():
        s = slot_ref[i]
        for cp in fetch(te_ref[i], s):
            cp.wait()
        wgu_s[...] = wgu_f[s].astype(BF16)
        wd_s[...] = wd_f[s].astype(BF16)

        @pl.when(next_ref[i] >= 0)
        def _():
            for cp in fetch(next_ref[i], 1 - s):
                cp.start()

    @pl.when(active)
    def _():
        gu = jnp.dot(x_ref[...].astype(BF16), wgu_s[...], preferred_element_type=F32) + bgu_ref[0, 0]
        gate = jnp.minimum(gu[:, :D_FF], SWIGLU_LIMIT)
        up = jnp.clip(gu[:, D_FF:], -SWIGLU_LIMIT, SWIGLU_LIMIT)
        act = gate * _sigmoid(SWIGLU_ALPHA * gate) * (up + 1.0)
        o_ref[...] = jnp.dot(act.astype(BF16), wd_s[...], preferred_element_type=F32) + bd_ref[0, 0]

    @pl.when(jnp.logical_not(active))
    def _():
        o_ref[...] = jnp.zeros_like(o_ref)


def _experts(xs, tile_expert, ends, n_active, w_gu, b_gu, w_down, b_down, layer):
    n_rows = xs.shape[0]
    depth = w_gu.shape[0]
    tm = EXPERT_TILE
    n_tiles = n_rows // tm
    tile = jnp.arange(n_tiles, dtype=jnp.int32)
    active = tile < n_active[0]
    first = (active & ((tile == 0) | (tile_expert != jnp.roll(tile_expert, 1)))).astype(jnp.int32)
    slot = (_prefix_sum(first) - 1) % 2
    next_tile = ends[tile_expert] // tm
    next_expert = jnp.where(next_tile < n_active[0],
                            tile_expert[jnp.minimum(next_tile, n_tiles - 1)], -1).astype(jnp.int32)

    def tiled(shape, index):
        return pl.BlockSpec(shape, lambda i, te, fi, sl, nx, na: index(i, te, na))

    grid_spec = pltpu.PrefetchScalarGridSpec(
        num_scalar_prefetch=5,
        grid=(n_tiles,),
        in_specs=[tiled((tm, D_MODEL), lambda i, te, na: (jnp.where(i < na[0], i, 0), 0)),
                  pl.BlockSpec(memory_space=pl.ANY),
                  tiled((1, 1, 1, 2 * D_FF), lambda i, te, na: (layer, te[i], 0, 0)),
                  pl.BlockSpec(memory_space=pl.ANY),
                  tiled((1, 1, 1, D_MODEL), lambda i, te, na: (layer, te[i], 0, 0))],
        out_specs=tiled((tm, D_MODEL), lambda i, te, na: (i, 0)),
        scratch_shapes=[pltpu.VMEM((2, D_MODEL, 2 * D_FF), F32), pltpu.VMEM((2, D_FF, D_MODEL), F32),
                        pltpu.VMEM((D_MODEL, 2 * D_FF), BF16), pltpu.VMEM((D_FF, D_MODEL), BF16),
                        pltpu.SemaphoreType.DMA((2, 2))],
    )
    return pl.pallas_call(
        functools.partial(_expert_kernel, layer=layer),
        grid_spec=grid_spec,
        out_shape=jax.ShapeDtypeStruct((n_rows, D_MODEL), F32),
        compiler_params=_cparams(("arbitrary",), 56),
        name="experts",
    )(tile_expert, first, slot.astype(jnp.int32), next_expert, n_active, xs, w_gu,
      b_gu.reshape(depth, N_EXPERTS, 1, 2 * D_FF), w_down, b_down.reshape(depth, N_EXPERTS, 1, D_MODEL))


def _moe(x, norm_g, modseg, w_router, b_router, w_gu, b_gu, w_down, b_down, layer):
    n_tok = x.shape[0]
    h, slots, wts, table = _router(x, norm_g, modseg, w_router, b_router, layer)
    tm = EXPERT_TILE
    n_tiles = n_tok // TOKEN_TILE
    n_rows = n_tok * TOP_K + n_tiles * N_EXPERTS * ROW_BLOCK + N_EXPERTS * tm
    n_rows = -(-n_rows // tm) * tm
    local_start = table[:, 0, :N_EXPERTS]
    padded = table[:, 1, :N_EXPERTS]
    earlier = table[:, 2, :N_EXPERTS]
    used = earlier[-1] + padded[-1]
    size = ((used + tm - 1) // tm) * tm
    ends = _prefix_sum(size)
    starts = ends - size
    sorted_start = starts[None, :] + earlier
    meta = _copy_lists(local_start, padded, sorted_start)
    tails = jnp.concatenate([starts + used, (size - used) // ROW_BLOCK,
                             ends[-1:], (n_rows - ends[-1:]) // tm,
                             jnp.zeros((LANES - 2 * N_EXPERTS - 2,), jnp.int32)])
    tile_row0 = jnp.arange(n_rows // tm, dtype=jnp.int32) * tm
    tile_expert = jnp.minimum(jnp.sum((ends[None, :] <= tile_row0[:, None]).astype(jnp.int32), axis=1),
                              N_EXPERTS - 1)
    n_active = (ends[-1:] // tm).astype(jnp.int32)
    xs = _dispatch(h, slots, meta, tails, n_rows)
    ys = _experts(xs, tile_expert, ends, n_active, w_gu, b_gu, w_down, b_down, layer)
    return _combine(ys, slots, meta, wts, x, modseg)


def _rope_tables(n_tok):
    rows = n_tok // GRID_W
    row = jnp.repeat(jnp.arange(rows, dtype=F32), GRID_W)
    col = jnp.tile(jnp.arange(GRID_W, dtype=F32), rows)
    quarter = HEAD_DIM // 4
    inv_freq = ROPE_BASE ** (-jnp.arange(quarter, dtype=F32) / quarter)
    ang_r = row[:, None] * inv_freq
    ang_c = col[:, None] * inv_freq
    ang = jnp.concatenate([ang_r, ang_r, ang_c, ang_c], axis=-1)
    cos, sin = jnp.cos(ang), jnp.sin(ang)
    first = (jnp.arange(HEAD_DIM) % (2 * quarter)) < quarter
    sin_a = jnp.where(first, -sin, 0.0)
    sin_b = jnp.where(first, 0.0, sin)
    return tuple(jnp.tile(t, (1, LANES // HEAD_DIM)) for t in (cos, sin_a, sin_b))


def kernel(x_prompt, x_sample, c, cache_k, cache_v, state_rglru, state_hgrn, c_ctx, w_mod, b_mod, norm1_g, norm2_g, w_in, q_norm_g, k_norm_g, conv_w, conv_b, rg_wa, rg_ba, rg_wx, rg_bx, rg_lambda, hgrn_lb_logits, hgrn_norm_g, w_attn_o, w_rnn_o, w_hgrn_o, w_out, w_router, b_router, w_gu, b_gu, w_down, b_down):
    depth = w_mod.shape[0]
    n_p, t_p = x_prompt.shape[0], x_prompt.shape[1]
    n_s, t_s = x_sample.shape[0], x_sample.shape[1]
    tok_p = n_p * t_p
    tok_s = n_s * t_s
    assert tok_p % SEG_ROWS == 0 and t_s == SEG_ROWS and SEG_ROWS % t_p == 0

    x = jnp.concatenate([x_prompt.reshape(tok_p, D_MODEL), x_sample.reshape(tok_s, D_MODEL)], axis=0)

    lb_all = jnp.moveaxis(_prefix_sum(jnp.moveaxis(jax.nn.softmax(hgrn_lb_logits.astype(F32), axis=0), 0, -1)),
                          -1, 0)
    lb_all = lb_all - lb_all[:1]
    rope = _rope_tables(t_s)

    cond8 = jnp.zeros((8, D_MODEL), F32).at[0].set(c_ctx).at[1:1 + n_s].set(c)
    mod = _modulation(cond8, w_mod, b_mod)
    seg_ids = jnp.asarray([0] * (tok_p // SEG_ROWS) + [1 + b for b in range(n_s)], jnp.int32)
    modsegs = mod[:, seg_ids].reshape(depth, seg_ids.shape[0], N_MOD, D_MODEL)

    blocks = jnp.stack([rg_wa, rg_wx], axis=2)
    eye = jnp.eye(RG_BLOCKS, dtype=blocks.dtype)
    w_gate = jnp.einsum("ldgnab,nm->lnadgmb", blocks, eye).reshape(depth, D_RNN, 4 * D_RNN)
    b_gate = jnp.stack([rg_ba, rg_bx], axis=2).reshape(depth, 1, 4 * D_RNN)

    new_k, new_v, new_hr, new_s = [], [], [], []
    for l in range(depth):
        modseg = modsegs[l]
        z = _input_projection(x, norm1_g, modseg, w_in, l)
        attn_p, k_l, v_l = _attention(z, q_norm_g, k_norm_g, l, n_p, t_p, 0)
        (attn_s,) = _attention(z, q_norm_g, k_norm_g, l, n_s, t_s, tok_p, rope=rope,
                               cache=(cache_k, cache_v))
        rnn_p, hr_l = _rglru(z, conv_w, conv_b, w_gate, b_gate, rg_lambda, l, n_p, t_p, 0)
        (rnn_s,) = _rglru(z, conv_w, conv_b, w_gate, b_gate, rg_lambda, l, n_s, t_s, tok_p,
                          state=state_rglru)
        hg_p, s_l = _hgrn(z, lb_all, hgrn_norm_g, l, n_p, t_p, 0)
        (hg_s,) = _hgrn(z, lb_all, hgrn_norm_g, l, n_s, t_s, tok_p, state=state_hgrn)
        x = _merge_project(((attn_p, attn_s), (rnn_p, rnn_s), (hg_p, hg_s)), z, x, modseg,
                           w_attn_o, w_rnn_o, w_hgrn_o, w_out, l)
        x = _moe(x, norm2_g, modseg, w_router, b_router, w_gu, b_gu, w_down, b_down, l)
        new_k.append(k_l.reshape(n_p, t_p, N_KV_HEADS, HEAD_DIM))
        new_v.append(v_l.reshape(n_p, t_p, N_KV_HEADS, HEAD_DIM))
        new_hr.append(hr_l)
        new_s.append(s_l)

    y_prompt = x[:tok_p].reshape(n_p, t_p, D_MODEL)
    y_sample = x[tok_p:].reshape(n_s, t_s, D_MODEL)
    return (y_prompt, y_sample, jnp.stack(new_k, axis=1), jnp.stack(new_v, axis=1),
            jnp.stack(new_hr, axis=1), jnp.stack(new_s, axis=1))
```

```python
import functools

import jax
import jax.numpy as jnp
from jax import lax
from jax.experimental import pallas as pl
from jax.experimental.pallas import tpu as pltpu

F32 = jnp.float32
BF16 = jnp.bfloat16

D_MODEL = 1024
GRID_W = 64
HEAD_DIM = 64
N_HEADS = 8
N_KV_HEADS = 2
KV_GROUP = N_HEADS // N_KV_HEADS
ATTN_W = N_HEADS * HEAD_DIM
KV_W = N_KV_HEADS * HEAD_DIM
ROPE_BASE = 10000.0
D_RNN = 512
RG_BLOCKS = 8
RG_BW = D_RNN // RG_BLOCKS
RG_C = 8.0
HG_HEADS = 8
HG_DK = 64
HG_DV = 64
HG_W = HG_HEADS * HG_DK
N_PAIRS = HG_W // 128
N_EXPERTS = 32
TOP_K = 4
D_FF = D_MODEL
SWIGLU_LIMIT = 7.0
SWIGLU_ALPHA = 1.702
N_MOD = 6
EPS = 1e-6
IN_W = ATTN_W + 2 * KV_W + 2 * D_RNN + 5 * HG_W + 3 * D_MODEL

LANES = 128
SEG_ROWS = 1024
COL_TILE = 256
HG_CHUNK = 256
Q_TILE = 256
EXPERT_TILE = 384
BIG_BLOCK = 64
TOKEN_TILE = 512
ROW_BLOCK = 8
LOCAL_ROWS = TOKEN_TILE * TOP_K + N_EXPERTS * ROW_BLOCK
NEG_BIG = -1e30
LOG2_E = 1.4426950408889634

_SRC_TILES = tuple(range(17, 29)) + tuple(range(7, 17)) + tuple(range(3, 7)) + (0, 1, 2)
Z_GL = 0
Z_HG = 3 * D_MODEL
Z_RG = Z_HG + 5 * HG_W
Z_Q = Z_RG + 2 * D_RNN
Z_K = Z_Q + ATTN_W
Z_V = Z_K + KV_W
Z_W = -(-IN_W // (2 * COL_TILE)) * (2 * COL_TILE)


def _cparams(sem, vmem_mb):
    return pltpu.CompilerParams(dimension_semantics=sem, vmem_limit_bytes=vmem_mb * 1024 * 1024)


def _prefix_sum(x):
    n = x.shape[-1]
    upto = (jnp.arange(n)[None, :] <= jnp.arange(n)[:, None]).astype(x.dtype)
    return jnp.sum(x[..., None, :] * upto, axis=-1)


def _mm(a, b):
    return jnp.dot(a.astype(BF16), b.astype(BF16), preferred_element_type=F32)


def _mm_nt(a, b):
    return lax.dot_general(a.astype(BF16), b.astype(BF16), (((1,), (1,)), ((), ())),
                           preferred_element_type=F32)


def _mm_tn(a, b):
    return lax.dot_general(a.astype(BF16), b.astype(BF16), (((0,), (0,)), ((), ())),
                           preferred_element_type=F32)


def _sigmoid(x):
    return 1.0 / (1.0 + jnp.exp(-x))


def _softplus(x):
    return jnp.maximum(x, 0.0) + jnp.log1p(jnp.exp(-jnp.abs(x)))


def _modulated_norm(x, g, scale, shift):
    y = x * lax.rsqrt(jnp.mean(x * x, axis=-1, keepdims=True) + EPS)
    return (y * g) * (1.0 + scale) + shift


def _mod_kernel(c_ref, w_ref, b_ref, o_ref):
    c = c_ref[...]
    s = c * _sigmoid(c)
    o_ref[0] = _mm(s, w_ref[0]) + b_ref[0]


def _modulation(cond8, w_mod, b_mod):
    depth = w_mod.shape[0]
    tn = 1024
    return pl.pallas_call(
        _mod_kernel,
        grid=(depth, N_MOD * D_MODEL // tn),
        in_specs=[pl.BlockSpec((8, D_MODEL), lambda l, j: (0, 0)),
                  pl.BlockSpec((1, D_MODEL, tn), lambda l, j: (l, 0, j)),
                  pl.BlockSpec((1, 1, tn), lambda l, j: (l, 0, j))],
        out_specs=pl.BlockSpec((1, 8, tn), lambda l, j: (l, 0, j)),
        out_shape=jax.ShapeDtypeStruct((depth, 8, N_MOD * D_MODEL), F32),
        compiler_params=_cparams(("arbitrary", "arbitrary"), 32),
        name="modulation",
    )(cond8, w_mod, b_mod.reshape(depth, 1, N_MOD * D_MODEL))


def _inproj_kernel(src_ref, x_ref, g_ref, mod_ref, wa_ref, wb_ref, o_ref, h_ref, w_s):
    j = pl.program_id(1)

    @pl.when(j == 0)
    def _():
        for s in range(x_ref.shape[0] // SEG_ROWS):
            rows = slice(s * SEG_ROWS, (s + 1) * SEG_ROWS)
            h = _modulated_norm(x_ref[rows, :], g_ref[0], mod_ref[s, 1:2, :], mod_ref[s, 0:1, :])
            h_ref[rows, :] = h.astype(BF16)

    @pl.when(pl.program_id(0) == 0)
    def _():
        w_s[j, :, 0:COL_TILE] = wa_ref[0].astype(BF16)
        w_s[j, :, COL_TILE:2 * COL_TILE] = wb_ref[0].astype(BF16)

    o_ref[...] = jnp.dot(h_ref[...], w_s[j], preferred_element_type=F32)


def _input_projection(x, norm_g, modseg, w_in, layer):
    n_tok = x.shape[0]
    segs = 2
    tm = segs * SEG_ROWS
    n_col = Z_W // (2 * COL_TILE)
    src = jnp.asarray(_SRC_TILES + (_SRC_TILES[-1],) * (Z_W // COL_TILE - len(_SRC_TILES)), jnp.int32)

    def wspec(k):
        return pl.BlockSpec((1, D_MODEL, COL_TILE),
                            lambda i, j, s: (layer, 0, s[2 * jnp.where(i == 0, j, n_col - 1) + k]))

    grid_spec = pltpu.PrefetchScalarGridSpec(
        num_scalar_prefetch=1,
        grid=(n_tok // tm, n_col),
        in_specs=[pl.BlockSpec((tm, D_MODEL), lambda i, j, s: (i, 0)),
                  pl.BlockSpec((1, 1, D_MODEL), lambda i, j, s: (layer, 0, 0)),
                  pl.BlockSpec((segs, N_MOD, D_MODEL), lambda i, j, s: (i, 0, 0)),
                  wspec(0), wspec(1)],
        out_specs=pl.BlockSpec((tm, 2 * COL_TILE), lambda i, j, s: (i, j)),
        scratch_shapes=[pltpu.VMEM((tm, D_MODEL), BF16),
                        pltpu.VMEM((n_col, D_MODEL, 2 * COL_TILE), BF16)],
    )
    return pl.pallas_call(
        _inproj_kernel,
        grid_spec=grid_spec,
        out_shape=jax.ShapeDtypeStruct((n_tok, Z_W), F32),
        compiler_params=_cparams(("arbitrary", "arbitrary"), 56),
        name="input_projection",
    )(src, x, norm_g.reshape(-1, 1, D_MODEL), modseg, w_in, w_in)


def _head_rms(x, g):
    lane = lax.broadcasted_iota(jnp.int32, x.shape, 1)
    lo = lane < HEAD_DIM
    xx = x * x
    s_lo = jnp.sum(jnp.where(lo, xx, 0.0), axis=-1, keepdims=True)
    s_hi = jnp.sum(jnp.where(lo, 0.0, xx), axis=-1, keepdims=True)
    inv = jnp.where(lo, lax.rsqrt(s_lo * (1.0 / HEAD_DIM) + EPS),
                    lax.rsqrt(s_hi * (1.0 / HEAD_DIM) + EPS))
    return x * inv * g


def _rope(x, cos, sin_a, sin_b):
    q = HEAD_DIM // 4
    return x * cos + pltpu.roll(x, LANES - q, 1) * sin_a + pltpu.roll(x, q, 1) * sin_b


def _attn_kernel(*refs, t_len, use_ctx):
    if use_ctx:
        (q_ref, k_ref, v_ref, qg_ref, kg_ref, cos_q_ref, sa_q_ref, sb_q_ref,
         cos_k_ref, sa_k_ref, sb_k_ref, ck_ref, cv_ref, o_ref, kall_ref, vall_ref) = refs
    else:
        (q_ref, k_ref, v_ref, qg_ref, kg_ref, o_ref, ko_ref, vo_ref, kall_ref, vall_ref) = refs

    @pl.when(pl.program_id(1) == 0)
    def _():
        k = _head_rms(k_ref[...], kg_ref[...])
        if use_ctx:
            k = _rope(k, cos_k_ref[...], sa_k_ref[...], sb_k_ref[...])
            kall_ref[0:t_len, :] = k.astype(BF16)
            vall_ref[0:t_len, :] = v_ref[...].astype(BF16)
            kall_ref[t_len:, :] = ck_ref[0, 0].astype(BF16)
            vall_ref[t_len:, :] = cv_ref[0, 0].astype(BF16)
        else:
            ko_ref[0] = k
            vo_ref[0] = v_ref[...]
            kall_ref[...] = k.astype(BF16)
            vall_ref[...] = v_ref[...].astype(BF16)

    heads = []
    for c in range(ATTN_W // LANES):
        qc = _head_rms(q_ref[:, c * LANES:(c + 1) * LANES], qg_ref[...])
        if use_ctx:
            qc = _rope(qc, cos_q_ref[...], sa_q_ref[...], sb_q_ref[...])
        qc = (qc * (HEAD_DIM ** -0.5)).astype(BF16)
        heads.append(qc[:, :HEAD_DIM])
        heads.append(qc[:, HEAD_DIM:])
    tq = q_ref.shape[0]
    for g in range(N_KV_HEADS):
        qs = jnp.concatenate(heads[g * KV_GROUP:(g + 1) * KV_GROUP], axis=0)
        kh = kall_ref[:, g * HEAD_DIM:(g + 1) * HEAD_DIM]
        vh = vall_ref[:, g * HEAD_DIM:(g + 1) * HEAD_DIM]
        s = _mm_nt(qs, kh)
        m = jnp.max(s, axis=-1, keepdims=True)
        p = jnp.exp(s - m)
        den = jnp.sum(p, axis=-1, keepdims=True)
        o = _mm(p, vh) / den
        for j in range(KV_GROUP):
            hh = g * KV_GROUP + j
            o_ref[:, hh * HEAD_DIM:(hh + 1) * HEAD_DIM] = o[j * tq:(j + 1) * tq].astype(BF16)


def _attention(z, q_g, k_g, layer, n_batch, t_len, row0, rope=None, cache=None):
    use_ctx = cache is not None
    nq = t_len // Q_TILE
    rb0 = row0 // t_len
    qb0 = row0 // Q_TILE
    qg = jnp.tile(q_g[layer], 2).reshape(1, LANES)
    kg = jnp.tile(k_g[layer], 2).reshape(1, LANES)
    n_out_tok = n_batch * t_len
    vec = pl.BlockSpec((1, LANES), lambda b, i: (0, 0))
    in_specs = [pl.BlockSpec((Q_TILE, ATTN_W), lambda b, i: (qb0 + b * nq + i, Z_Q // ATTN_W)),
                pl.BlockSpec((t_len, KV_W), lambda b, i: (rb0 + b, Z_K // KV_W)),
                pl.BlockSpec((t_len, KV_W), lambda b, i: (rb0 + b, Z_V // KV_W)),
                vec, vec]
    args = [z, z, z, qg, kg]
    out_specs = [pl.BlockSpec((Q_TILE, ATTN_W), lambda b, i: (b * nq + i, 0))]
    out_shape = [jax.ShapeDtypeStruct((n_out_tok, ATTN_W), BF16)]
    t_keys = t_len
    if use_ctx:
        cos, sin_a, sin_b = rope
        cache_k, cache_v = cache
        past = cache_k.shape[2]
        t_keys = t_len + past
        tq_spec = pl.BlockSpec((Q_TILE, LANES), lambda b, i: (i, 0))
        tk_spec = pl.BlockSpec((t_len, LANES), lambda b, i: (0, 0))
        c_spec = pl.BlockSpec((1, 1, past, KV_W), lambda b, i: (b, layer, 0, 0))
        in_specs += [tq_spec, tq_spec, tq_spec, tk_spec, tk_spec, tk_spec, c_spec, c_spec]
        args += [cos, sin_a, sin_b, cos, sin_a, sin_b,
                 cache_k.reshape(cache_k.shape[0], cache_k.shape[1], past, KV_W),
                 cache_v.reshape(cache_v.shape[0], cache_v.shape[1], past, KV_W)]
    else:
        kv_spec = pl.BlockSpec((1, t_len, KV_W), lambda b, i: (b, 0, 0))
        out_specs += [kv_spec, kv_spec]
        out_shape += [jax.ShapeDtypeStruct((n_batch, t_len, KV_W), F32)] * 2
    return pl.pallas_call(
        functools.partial(_attn_kernel, t_len=t_len, use_ctx=use_ctx),
        grid=(n_batch, nq),
        in_specs=in_specs,
        out_specs=out_specs,
        out_shape=out_shape,
        scratch_shapes=[pltpu.VMEM((t_keys, KV_W), BF16), pltpu.VMEM((t_keys, KV_W), BF16)],
        compiler_params=_cparams(("arbitrary", "arbitrary"), 48),
        name="attention_ctx" if use_ctx else "attention",
    )(*args)


def _shift_rows(x, d):
    n = x.shape[0]
    row = lax.broadcasted_iota(jnp.int32, x.shape, 0)
    y = pltpu.roll(x, d % n, 0)
    if d > 0:
        return jnp.where(row >= d, y, 0.0)
    return jnp.where(row < n + d, y, 0.0)


def _linear_scan(a, u, reverse):
    n, w = a.shape
    row = lax.broadcasted_iota(jnp.int32, a.shape, 0)

    def shifted(x, d, fill):
        if d % ROW_BLOCK == 0:
            pad = jnp.full((d, w), fill, x.dtype)
            return jnp.concatenate([x[d:], pad] if reverse else [pad, x[:n - d]], axis=0)
        if reverse:
            return jnp.where(row < n - d, pltpu.roll(x, n - d, 0), fill)
        return jnp.where(row >= d, pltpu.roll(x, d, 0), fill)

    d = 1
    while d < n:
        u = a * shifted(u, d, 0.0) + u
        a = a * shifted(a, d, 1.0)
        d *= 2
    return a, u


def _rglru_kernel(*refs, use_ctx):
    if use_ctx:
        (x_ref, y_ref, cw_ref, cb_ref, wg_ref, bg_ref, lam_ref, h0_ref, o_ref) = refs
    else:
        (x_ref, y_ref, cw_ref, cb_ref, wg_ref, bg_ref, lam_ref, o_ref, last_ref) = refs
    x = x_ref[...]
    cw = cw_ref[0]
    xr = cb_ref[0] + _shift_rows(x, 1) * cw[0:1] + x * cw[1:2] \
        + _shift_rows(x, -1) * cw[2:3] + _shift_rows(x, -2) * cw[3:4]
    gates = _sigmoid(_mm(xr, wg_ref[0]) + bg_ref[0])
    n = x.shape[0]
    total = None
    lasts = []
    for d in range(2):
        r = gates[:, (2 * d) * D_RNN:(2 * d + 1) * D_RNN]
        i = gates[:, (2 * d + 1) * D_RNN:(2 * d + 2) * D_RNN]
        log_a = (-RG_C * _softplus(-lam_ref[0, d:d + 1, :])) * r
        a = jnp.exp(log_a)
        u = jnp.sqrt(1.0 - a * a) * (i * xr)
        a_cum, h = _linear_scan(a, u, reverse=(d == 1))
        if use_ctx:
            h = h + a_cum * h0_ref[0, 0, d:d + 1, :]
        else:
            lasts.append(h[0:1] if d == 1 else h[n - 1:n])
        total = h if total is None else total + h
    o_ref[...] = (jax.nn.gelu(y_ref[...]) * total).astype(BF16)
    if not use_ctx:
        last_ref[0] = jnp.concatenate(lasts, axis=0)


def _rglru(z, conv_w, conv_b, w_gate, b_gate, lam, layer, n_batch, t_len, row0, state=None):
    use_ctx = state is not None
    rb0 = row0 // t_len
    in_specs = [pl.BlockSpec((t_len, D_RNN), lambda b: (rb0 + b, Z_RG // D_RNN)),
                pl.BlockSpec((t_len, D_RNN), lambda b: (rb0 + b, Z_RG // D_RNN + 1)),
                pl.BlockSpec((1, 4, D_RNN), lambda b: (layer, 0, 0)),
                pl.BlockSpec((1, 1, D_RNN), lambda b: (layer, 0, 0)),
                pl.BlockSpec((1, D_RNN, 4 * D_RNN), lambda b: (layer, 0, 0)),
                pl.BlockSpec((1, 1, 4 * D_RNN), lambda b: (layer, 0, 0)),
                pl.BlockSpec((1, 2, D_RNN), lambda b: (layer, 0, 0))]
    args = [z, z, conv_w, conv_b.reshape(-1, 1, D_RNN), w_gate, b_gate, lam]
    out_specs = [pl.BlockSpec((t_len, D_RNN), lambda b: (b, 0))]
    out_shape = [jax.ShapeDtypeStruct((n_batch * t_len, D_RNN), BF16)]
    if use_ctx:
        in_specs.append(pl.BlockSpec((1, 1, 2, D_RNN), lambda b: (b, layer, 0, 0)))
        args.append(state)
    else:
        out_specs.append(pl.BlockSpec((1, 2, D_RNN), lambda b: (b, 0, 0)))
        out_shape.append(jax.ShapeDtypeStruct((n_batch, 2, D_RNN), F32))
    return pl.pallas_call(
        functools.partial(_rglru_kernel, use_ctx=use_ctx),
        grid=(n_batch,),
        in_specs=in_specs,
        out_specs=out_specs,
        out_shape=out_shape,
        compiler_params=_cparams(("arbitrary",), 56),
        name="rglru_ctx" if use_ctx else "rglru",
    )(*args)


def _cumsum_rows(g, reverse):
    n, w = g.shape
    m = n // ROW_BLOCK

    def scan(x, size, pos):
        d = 1
        while d < size:
            if reverse:
                x = x + jnp.where(pos < size - d, pltpu.roll(x, x.shape[0] - d, 0), 0.0)
            else:
                x = x + jnp.where(pos >= d, pltpu.roll(x, d, 0), 0.0)
            d *= 2
        return x

    g = scan(g, ROW_BLOCK, lax.broadcasted_iota(jnp.int32, g.shape, 0) & (ROW_BLOCK - 1))
    g3 = g.reshape(m, ROW_BLOCK, w)
    total = g3[:, 0, :] if reverse else g3[:, ROW_BLOCK - 1, :]
    before = scan(total, m, lax.broadcasted_iota(jnp.int32, total.shape, 0)) - total
    return (g3 + before[:, None, :]).reshape(n, w)


def _block_row_fn(gc, reverse):
    n, w = gc.shape
    g3 = gc.reshape(n // ROW_BLOCK, ROW_BLOCK, w)
    edge_row = 0 if reverse else ROW_BLOCK - 1
    edge = jnp.broadcast_to(g3[:, edge_row:edge_row + 1, :], g3.shape)
    sub = lax.broadcasted_iota(jnp.int32, g3.shape, 1)

    def at(blk):
        r = blk // 2 if reverse else blk // 2 - 1
        if blk > ROW_BLOCK:
            groups = blk // ROW_BLOCK
            e4 = edge.reshape(n // blk, groups, ROW_BLOCK, w)
            pick = r // ROW_BLOCK
            return jnp.broadcast_to(e4[:, pick:pick + 1], e4.shape).reshape(n, w)
        out = jnp.broadcast_to(g3[:, r:r + 1, :], g3.shape)
        for b0 in range(blk, ROW_BLOCK, blk):
            out = jnp.where(sub >= b0, jnp.broadcast_to(g3[:, b0 + r:b0 + r + 1, :], g3.shape), out)
        return out.reshape(n, w)

    return at


def _first_head(shape):
    lane = lax.broadcasted_iota(jnp.int32, shape, len(shape) - 1)
    return (lane & (LANES - 1)) < HG_DK


def _pair(x, p):
    return x[:, p * LANES:(p + 1) * LANES]


def _hgrn_gates(f_logit, lb):
    y = jnp.exp(-jnp.abs(f_logit))
    log_sig = jnp.minimum(f_logit, 0.0) - jnp.log(1.0 + y)
    a = jnp.log(lb)
    b = jnp.log1p(-lb) + log_sig
    log_f = jnp.maximum(a, b) + jnp.log(1.0 + jnp.exp(-jnp.abs(a - b)))
    k = (1.0 - lb) * (jnp.where(f_logit >= 0.0, y, 1.0) / (1.0 + y))
    return log_f * LOG2_E, k


def _hgrn_chunk_state(gc, k, v, reverse):
    n = gc.shape[0]
    g_tot = gc[0:1] if reverse else gc[n - 1:n]
    k_end = (k * jnp.exp2(jnp.minimum(g_tot - gc, 0.0))).astype(BF16)
    vb = v.astype(BF16)
    same_head = (_first_head((LANES, LANES))
                 == (lax.broadcasted_iota(jnp.int32, (LANES, LANES), 0) < HG_DK))
    ds = [jnp.where(same_head, _mm_tn(_pair(k_end, p), _pair(vb, p)), 0.0) for p in range(N_PAIRS)]
    return ds, jnp.exp2(g_tot)


def _decay_state(s, e_row):
    n = s.shape[0]
    eye = (lax.broadcasted_iota(jnp.int32, (n, n), 0) == lax.broadcasted_iota(jnp.int32, (n, n), 1))
    col = jnp.sum(jnp.where(eye, jnp.broadcast_to(e_row, (n, n)), 0.0), axis=1, keepdims=True)
    return s * col


def _hgrn_intra_scores(q, ks, gcs, txs):
    n = q.shape[0]
    row = lax.broadcasted_iota(jnp.int32, q.shape, 0)
    first = _first_head(q.shape)
    boundary = [_block_row_fn(gcs[0], False), _block_row_fn(gcs[1], True)]
    scores = [None] * HG_HEADS
    w = 1
    while w < n:
        blk = 2 * w
        second = (row & w) != 0
        e_f = jnp.exp2(-jnp.abs(gcs[0] - boundary[0](blk)))
        e_b = jnp.exp2(-jnp.abs(gcs[1] - boundary[1](blk)))
        q_f = jnp.where(second, q * e_f, 0.0).astype(BF16)
        q_b = jnp.where(second, 0.0, q * e_b).astype(BF16)
        k_f = jnp.where(second, 0.0, ks[0] * e_f)
        k_b = jnp.where(second, ks[1] * e_b, 0.0)
        k_heads = [(jnp.where(first, k_f, 0.0).astype(BF16), jnp.where(first, k_b, 0.0).astype(BF16)),
                   (jnp.where(first, 0.0, k_f).astype(BF16), jnp.where(first, 0.0, k_b).astype(BF16))]
        sel = (txs >= w) & (txs < blk)
        for p in range(N_PAIRS):
            qc = jnp.concatenate([_pair(q_f, p), _pair(q_b, p)], axis=1)
            for j in range(2):
                kc = jnp.concatenate([_pair(k_heads[j][0], p), _pair(k_heads[j][1], p)], axis=1)
                s = jnp.where(sel, _mm_nt(qc, kc), 0.0)
                h = 2 * p + j
                scores[h] = s if scores[h] is None else scores[h] + s
        w = blk
    return scores


def _hgrn_kernel(*refs, t_len, use_ctx):
    if use_ctx:
        (q_ref, ff_ref, fb_ref, i_ref, og_ref, lb_ref, ng_ref, s0_ref, o_ref, sin_ref, gk_ref) = refs
    else:
        (q_ref, ff_ref, fb_ref, i_ref, og_ref, lb_ref, ng_ref, o_ref, sl_ref) = refs
    c = HG_CHUNK
    n_chunks = t_len // c
    f_refs = (ff_ref, fb_ref)

    def load_dir(d, r0):
        log_f, k = _hgrn_gates(f_refs[d][pl.ds(r0, c), :], lb_ref[0, d:d + 1, :])
        return _cumsum_rows(log_f, reverse=(d == 1)), k

    if use_ctx:
        zero = jnp.zeros((HG_DK, HG_DV), F32)
        for d in range(2):
            states = []
            for p in range(N_PAIRS):
                top = jnp.concatenate([s0_ref[0, 0, d, 2 * p], zero], axis=1)
                bot = jnp.concatenate([zero, s0_ref[0, 0, d, 2 * p + 1]], axis=1)
                states.append(jnp.concatenate([top, bot], axis=0))
            order = range(n_chunks) if d == 0 else range(n_chunks - 1, -1, -1)
            for ci in order:
                for p in range(N_PAIRS):
                    sin_ref[ci, p, d * LANES:(d + 1) * LANES, :] = states[p]
                gc, k = load_dir(d, ci * c)
                gk_ref[d, 0, pl.ds(ci * c, c), :] = gc
                gk_ref[d, 1, pl.ds(ci * c, c), :] = k
                ds, e_tot = _hgrn_chunk_state(gc, k, i_ref[pl.ds(ci * c, c), :], reverse=(d == 1))
                states = [_decay_state(states[p], _pair(e_tot, p)) + ds[p] for p in range(N_PAIRS)]

    ti = lax.broadcasted_iota(jnp.int32, (c, c), 0)
    si = lax.broadcasted_iota(jnp.int32, (c, c), 1)
    txs = ti ^ si
    diag = ti == si

    def chunk_body(ci, carry):
        r0 = pl.multiple_of(ci * c, c)
        hq = q_ref[pl.ds(r0, c), :]
        q = hq * _sigmoid(hq)
        v = i_ref[pl.ds(r0, c), :]
        gcs, ks = [], []
        for d in range(2):
            if use_ctx:
                gc, k = gk_ref[d, 0, pl.ds(r0, c), :], gk_ref[d, 1, pl.ds(r0, c), :]
            else:
                gc, k = load_dir(d, r0)
            gcs.append(gc)
            ks.append(k)
        scores = _hgrn_intra_scores(q, ks, gcs, txs)
        first = _first_head(q.shape)
        k_sum = ks[0] + ks[1]
        v_heads = (jnp.where(first, v, 0.0).astype(BF16), jnp.where(first, 0.0, v).astype(BF16))
        k_heads = (jnp.where(first, k_sum, 0.0).astype(BF16), jnp.where(first, 0.0, k_sum).astype(BF16))
        q_bf = q.astype(BF16)
        if use_ctx:
            qe = [(q * jnp.exp2(gcs[d])).astype(BF16) for d in range(2)]
        else:
            for d in range(2):
                ds, _ = _hgrn_chunk_state(gcs[d], ks[d], v, reverse=(d == 1))
                for p in range(N_PAIRS):
                    sl_ref[0, d, 2 * p] = ds[p][:HG_DK, :HG_DV]
                    sl_ref[0, d, 2 * p + 1] = ds[p][HG_DK:, HG_DV:]
        outs = []
        for p in range(N_PAIRS):
            o = None
            for j in range(2):
                s_diag = jnp.where(diag, _mm_nt(_pair(q_bf, p), _pair(k_heads[j], p)), 0.0)
                t = _mm(scores[2 * p + j] + s_diag, _pair(v_heads[j], p))
                o = t if o is None else o + t
            if use_ctx:
                o = o + _mm(jnp.concatenate([_pair(qe[0], p), _pair(qe[1], p)], axis=1), sin_ref[ci, p])
            outs.append(_head_rms(o, ng_ref[0]))
        og = og_ref[pl.ds(r0, c), :]
        o_all = jnp.concatenate(outs, axis=1) * (og * _sigmoid(og))
        o_ref[pl.ds(r0, c), :] = o_all.astype(BF16)
        return carry

    if n_chunks == 1:
        chunk_body(0, 0)
    else:
        lax.fori_loop(0, n_chunks, chunk_body, 0)


def _hgrn(z, lb, norm_g, layer, n_batch, t_len, row0, state=None):
    use_ctx = state is not None
    rb0 = row0 // t_len
    cb = Z_HG // HG_W

    def zspec(k):
        return pl.BlockSpec((t_len, HG_W), lambda b: (rb0 + b, cb + k))

    in_specs = [zspec(0), zspec(1), zspec(2), zspec(3), zspec(4),
                pl.BlockSpec((1, 2, HG_W), lambda b: (layer, 0, 0)),
                pl.BlockSpec((1, 1, LANES), lambda b: (layer, 0, 0))]
    args = [z, z, z, z, z, lb, jnp.tile(norm_g, (1, LANES // HG_DV)).reshape(-1, 1, LANES)]
    out_specs = [pl.BlockSpec((t_len, HG_W), lambda b: (b, 0))]
    out_shape = [jax.ShapeDtypeStruct((n_batch * t_len, HG_W), BF16)]
    scratch = []
    if use_ctx:
        in_specs.append(pl.BlockSpec((1, 1, 2, HG_HEADS, HG_DK, HG_DV), lambda b: (b, layer, 0, 0, 0, 0)))
        args.append(state)
        scratch.append(pltpu.VMEM((t_len // HG_CHUNK, N_PAIRS, 2 * LANES, LANES), F32))
        scratch.append(pltpu.VMEM((2, 2, t_len, HG_W), F32))
    else:
        out_specs.append(pl.BlockSpec((1, 2, HG_HEADS, HG_DK, HG_DV), lambda b: (b, 0, 0, 0, 0)))
        out_shape.append(jax.ShapeDtypeStruct((n_batch, 2, HG_HEADS, HG_DK, HG_DV), F32))
    return pl.pallas_call(
        functools.partial(_hgrn_kernel, t_len=t_len, use_ctx=use_ctx),
        grid=(n_batch,),
        in_specs=in_specs,
        out_specs=out_specs,
        out_shape=out_shape,
        scratch_shapes=scratch,
        compiler_params=_cparams(("arbitrary",), 56),
        name="hgrn_ctx" if use_ctx else "hgrn",
    )(*args)


def _merge_kernel(ap_ref, as_ref, rp_ref, rs_ref, hp_ref, hs_ref, g0_ref, g1_ref, g2_ref, x_ref, mod_ref,
                  wa_ref, wr_ref, wh_ref, wo_ref, o_ref, wa_s, wr_s, wh_s, wo_s, *, prompt_tiles):
    @pl.when(pl.program_id(0) == 0)
    def _():
        wa_s[...] = wa_ref[0].astype(BF16)
        wr_s[...] = wr_ref[0].astype(BF16)
        wh_s[...] = wh_ref[0].astype(BF16)
        wo_s[...] = wo_ref[0].astype(BF16)

    is_prompt = pl.program_id(0) < prompt_tiles

    def proj(p_ref, s_ref, w_s):
        return jnp.dot(jnp.where(is_prompt, p_ref[...], s_ref[...]), w_s[...], preferred_element_type=F32)

    merged = (_sigmoid(g0_ref[...]) * proj(ap_ref, as_ref, wa_s)
              + _sigmoid(g1_ref[...]) * proj(rp_ref, rs_ref, wr_s)
              + _sigmoid(g2_ref[...]) * proj(hp_ref, hs_ref, wh_s))
    out = jnp.dot(merged.astype(BF16), wo_s[...], preferred_element_type=F32)
    o_ref[...] = x_ref[...] + mod_ref[0, 2:3, :] * out


def _merge_project(mixed, z, x, modseg, w_attn_o, w_rnn_o, w_hgrn_o, w_out, layer):
    n_tok = x.shape[0]
    tm = 512
    per_seg = SEG_ROWS // tm
    prompt_tiles = mixed[0][0].shape[0] // tm
    half_p = pl.BlockSpec((tm, ATTN_W), lambda i: (jnp.minimum(i, prompt_tiles - 1), 0))
    half_s = pl.BlockSpec((tm, ATTN_W), lambda i: (jnp.maximum(i - prompt_tiles, 0), 0))
    full = pl.BlockSpec((tm, D_MODEL), lambda i: (i, 0))

    def gspec(k):
        return pl.BlockSpec((tm, D_MODEL), lambda i: (i, Z_GL // D_MODEL + k))

    def wspec(rows):
        return pl.BlockSpec((1, rows, D_MODEL), lambda i: (layer, 0, 0))

    return pl.pallas_call(
        functools.partial(_merge_kernel, prompt_tiles=prompt_tiles),
        grid=(n_tok // tm,),
        in_specs=[half_p, half_s, half_p, half_s, half_p, half_s, gspec(0), gspec(1), gspec(2), full,
                  pl.BlockSpec((1, N_MOD, D_MODEL), lambda i: (i // per_seg, 0, 0)),
                  wspec(ATTN_W), wspec(D_RNN), wspec(HG_W), wspec(D_MODEL)],
        out_specs=full,
        out_shape=jax.ShapeDtypeStruct((n_tok, D_MODEL), F32),
        scratch_shapes=[pltpu.VMEM((ATTN_W, D_MODEL), BF16), pltpu.VMEM((D_RNN, D_MODEL), BF16),
                        pltpu.VMEM((HG_W, D_MODEL), BF16), pltpu.VMEM((D_MODEL, D_MODEL), BF16)],
        compiler_params=_cparams(("arbitrary",), 56),
        name="merge_project",
    )(*[part for pair in mixed for part in pair], z, z, z, x, modseg, w_attn_o, w_rnn_o, w_hgrn_o, w_out)


def _router_kernel(x_ref, g_ref, mod_ref, wr_ref, br_ref, h_ref, slot_ref, wt_ref, meta_ref, run_ref):
    @pl.when(pl.program_id(0) == 0)
    def _():
        run_ref[...] = jnp.zeros_like(run_ref)

    h = _modulated_norm(x_ref[...], g_ref[0], mod_ref[0, 4:5, :], mod_ref[0, 3:4, :])
    h_ref[...] = h
    h_hi = h.astype(BF16)
    h_lo = (h - h_hi.astype(F32)).astype(BF16)
    w = wr_ref[0]
    w_hi = w.astype(BF16)
    w_lo = (w - w_hi.astype(F32)).astype(BF16)
    logits = (jnp.dot(h_hi, w_hi, preferred_element_type=F32) + jnp.dot(h_hi, w_lo, preferred_element_type=F32)
              + jnp.dot(h_lo, w_hi, preferred_element_type=F32)) + br_ref[0]
    tm = logits.shape[0]
    lane = lax.broadcasted_iota(jnp.int32, logits.shape, 1)
    ids, vals = [], []
    chosen = jnp.zeros(logits.shape, jnp.bool_)
    work = logits
    for _ in range(TOP_K):
        m = jnp.max(work, axis=-1, keepdims=True)
        idx = jnp.min(jnp.where(work == m, lane, LANES), axis=-1, keepdims=True)
        hit = lane == idx
        chosen = chosen | hit
        work = jnp.where(hit, -jnp.inf, work)
        ids.append(idx)
        vals.append(m)
    exps = [jnp.exp(v - vals[0]) for v in vals]
    den = exps[0] + exps[1] + exps[2] + exps[3]
    ind = jnp.where(chosen, 1.0, 0.0)
    r_i = lax.broadcasted_iota(jnp.int32, (tm, tm), 0)
    c_i = lax.broadcasted_iota(jnp.int32, (tm, tm), 1)
    count = jnp.sum(ind, axis=0, keepdims=True)
    padded = jnp.floor((count + (ROW_BLOCK - 1)) * (1.0 / ROW_BLOCK)) * ROW_BLOCK
    incl = jnp.broadcast_to(padded, (8, LANES))
    lane8 = lax.broadcasted_iota(jnp.int32, (8, LANES), 1)
    d = 1
    while d < LANES:
        incl = incl + jnp.where(lane8 >= d, pltpu.roll(incl, d, 1), 0.0)
        d *= 2
    local_start = incl[0:1] - padded
    place = _mm(jnp.where(c_i < r_i, 1.0, 0.0), ind) + local_start
    slot_out = jnp.zeros(logits.shape, jnp.int32)
    wt_out = jnp.zeros(logits.shape, F32)
    for k in range(TOP_K):
        slot = jnp.sum(jnp.where(lane == ids[k], place, 0.0), axis=-1, keepdims=True)
        slot_out = jnp.where(lane == k, slot.astype(jnp.int32), slot_out)
        wt_out = jnp.where(lane == k, exps[k] / den, wt_out)
    slot_ref[...] = slot_out
    wt_ref[...] = wt_out
    row8 = lax.broadcasted_iota(jnp.int32, (8, LANES), 0)
    table = jnp.where(row8 == 0, local_start, jnp.where(row8 == 1, padded, run_ref[0:1, :]))
    meta_ref[0] = jnp.where(row8 < 3, table, 0.0).astype(jnp.int32)
    run_ref[...] = run_ref[...] + padded


def _router(x, norm_g, modseg, w_router, b_router, layer):
    n_tok = x.shape[0]
    tm = TOKEN_TILE
    per_seg = SEG_ROWS // tm
    depth = w_router.shape[0]
    w_pad = jnp.zeros((depth, D_MODEL, LANES), F32).at[:, :, :N_EXPERTS].set(w_router)
    b_pad = jnp.full((depth, 1, LANES), NEG_BIG, F32).at[:, 0, :N_EXPERTS].set(b_router)
    full = pl.BlockSpec((tm, D_MODEL), lambda i: (i, 0))
    small = pl.BlockSpec((tm, LANES), lambda i: (i, 0))
    return pl.pallas_call(
        _router_kernel,
        grid=(n_tok // tm,),
        in_specs=[full,
                  pl.BlockSpec((1, 1, D_MODEL), lambda i: (layer, 0, 0)),
                  pl.BlockSpec((1, N_MOD, D_MODEL), lambda i: (i // per_seg, 0, 0)),
                  pl.BlockSpec((1, D_MODEL, LANES), lambda i: (layer, 0, 0)),
                  pl.BlockSpec((1, 1, LANES), lambda i: (layer, 0, 0))],
        out_specs=[full, small, small, pl.BlockSpec((1, 8, LANES), lambda i: (i, 0, 0))],
        out_shape=[jax.ShapeDtypeStruct((n_tok, D_MODEL), F32),
                   jax.ShapeDtypeStruct((n_tok, LANES), jnp.int32),
                   jax.ShapeDtypeStruct((n_tok, LANES), F32),
                   jax.ShapeDtypeStruct((n_tok // tm, 8, LANES), jnp.int32)],
        scratch_shapes=[pltpu.VMEM((8, LANES), F32)],
        compiler_params=_cparams(("arbitrary",), 40),
        name="router",
    )(x, norm_g.reshape(-1, 1, D_MODEL), modseg, w_pad, b_pad)


MAX_BIG = LOCAL_ROWS // BIG_BLOCK
MAX_SMALL = N_EXPERTS * (BIG_BLOCK // ROW_BLOCK - 1)
COPY_LIST_LEN = 1024
_BIG_AT = 2
_SMALL_AT = 2 + 2 * MAX_BIG
assert _SMALL_AT + 2 * MAX_SMALL <= COPY_LIST_LEN


def _copy_lists(local_start, padded, sorted_start):
    n_big = padded // BIG_BLOCK
    n_small = (padded - n_big * BIG_BLOCK) // ROW_BLOCK

    def expand(count, local0, sorted0, step, cap):
        last = _prefix_sum(count)[:, None, :]
        first = last - count[:, None, :]
        f = jnp.arange(cap, dtype=jnp.int32)[None, :, None]
        mine = ((first <= f) & (f < last)).astype(jnp.int32)
        offset = step * (f - first)
        return (last[:, 0, -1:], jnp.sum(mine * (local0[:, None, :] + offset), axis=2),
                jnp.sum(mine * (sorted0[:, None, :] + offset), axis=2))

    cnt_b, loc_b, srt_b = expand(n_big, local_start, sorted_start, BIG_BLOCK, MAX_BIG)
    done = n_big * BIG_BLOCK
    cnt_s, loc_s, srt_s = expand(n_small, local_start + done, sorted_start + done, ROW_BLOCK, MAX_SMALL)
    parts = [cnt_b, cnt_s, loc_b, srt_b, loc_s, srt_s]
    used = sum(p.shape[1] for p in parts)
    parts.append(jnp.zeros((padded.shape[0], COPY_LIST_LEN - used), jnp.int32))
    return jnp.concatenate(parts, axis=1).astype(jnp.int32).reshape(-1)


def _for_each_block(list_ref, fn):
    def big(j, c):
        fn(pl.multiple_of(list_ref[_BIG_AT + j], ROW_BLOCK),
           pl.multiple_of(list_ref[_BIG_AT + MAX_BIG + j], ROW_BLOCK), BIG_BLOCK)
        return c

    def small(j, c):
        fn(pl.multiple_of(list_ref[_SMALL_AT + j], ROW_BLOCK),
           pl.multiple_of(list_ref[_SMALL_AT + MAX_SMALL + j], ROW_BLOCK), ROW_BLOCK)
        return c

    lax.fori_loop(0, list_ref[0], big, 0)
    lax.fori_loop(0, list_ref[1], small, 0)


def _slot_matrix(slot_ref, values):
    col = lax.broadcasted_iota(jnp.int32, (TOKEN_TILE, LOCAL_ROWS), 1)
    out = jnp.zeros((TOKEN_TILE, LOCAL_ROWS), F32)
    for k in range(TOP_K):
        out = jnp.where(col == slot_ref[:, k:k + 1], values[k], out)
    return out.astype(BF16)


def _dispatch_kernel(meta_ref, prev_meta_ref, tail_ref, slot_ref, h_ref, xs_ref, local_ref, zero_ref,
                     sem, block_sem):
    i = pl.program_id(0)
    cur = i % 2

    @pl.when(i == 0)
    def _():
        zero_ref[...] = jnp.zeros_like(zero_ref)

        def tails(fn):
            def group(e, carry):
                def block(j, c):
                    fn(pl.multiple_of(tail_ref[e] + j * ROW_BLOCK, ROW_BLOCK))
                    return c
                return lax.fori_loop(0, tail_ref[N_EXPERTS + e], block, carry)
            lax.fori_loop(0, N_EXPERTS, group, 0)

        def zero_copy(row):
            return pltpu.make_async_copy(zero_ref.at[pl.ds(0, ROW_BLOCK)],
                                         xs_ref.at[pl.ds(row, ROW_BLOCK)], sem)

        tails(lambda row: zero_copy(row).start())
        tails(lambda row: zero_copy(row).wait())

        def unused(fn):
            def tile(j, c):
                fn(pl.multiple_of(tail_ref[2 * N_EXPERTS] + j * EXPERT_TILE, EXPERT_TILE))
                return c
            lax.fori_loop(0, tail_ref[2 * N_EXPERTS + 1], tile, 0)

        def zero_tile(row):
            return pltpu.make_async_copy(zero_ref, xs_ref.at[pl.ds(row, EXPERT_TILE)], sem)

        unused(lambda row: zero_tile(row).start())
        unused(lambda row: zero_tile(row).wait())

    onehot = _slot_matrix(slot_ref, [1.0] * TOP_K)
    local_ref[cur] = _mm_tn(onehot, h_ref[...])

    def block_copy(buf):
        return lambda local_row, sorted_row, rows: pltpu.make_async_copy(
            local_ref.at[buf, pl.ds(local_row, rows)], xs_ref.at[pl.ds(sorted_row, rows)],
            block_sem.at[buf])

    _for_each_block(meta_ref, lambda a, b, n: block_copy(cur)(a, b, n).start())

    @pl.when(i > 0)
    def _():
        _for_each_block(prev_meta_ref, lambda a, b, n: block_copy(1 - cur)(a, b, n).wait())

    @pl.when(i == pl.num_programs(0) - 1)
    def _():
        _for_each_block(meta_ref, lambda a, b, n: block_copy(cur)(a, b, n).wait())


def _dispatch(h, slots, meta, tails, n_rows):
    n_tok = h.shape[0]
    return pl.pallas_call(
        _dispatch_kernel,
        grid=(n_tok // TOKEN_TILE,),
        in_specs=[pl.BlockSpec((COPY_LIST_LEN,), lambda i: (i,), memory_space=pltpu.SMEM),
                  pl.BlockSpec((COPY_LIST_LEN,), lambda i: (jnp.maximum(i - 1, 0),), memory_space=pltpu.SMEM),
                  pl.BlockSpec((LANES,), lambda i: (0,), memory_space=pltpu.SMEM),
                  pl.BlockSpec((TOKEN_TILE, LANES), lambda i: (i, 0)),
                  pl.BlockSpec((TOKEN_TILE, D_MODEL), lambda i: (i, 0))],
        out_specs=pl.BlockSpec(memory_space=pl.ANY),
        out_shape=jax.ShapeDtypeStruct((n_rows, D_MODEL), F32),
        scratch_shapes=[pltpu.VMEM((2, LOCAL_ROWS, D_MODEL), F32), pltpu.VMEM((EXPERT_TILE, D_MODEL), F32),
                        pltpu.SemaphoreType.DMA(()), pltpu.SemaphoreType.DMA((2,))],
        compiler_params=_cparams(("arbitrary",), 48),
        name="dispatch",
    )(meta, meta, tails, slots, h)


def _combine_kernel(meta_ref, next_meta_ref, slot_ref, wt_ref, ys_ref, x_ref, mod_ref, o_ref, local_ref, sem):
    i = pl.program_id(0)
    cur = i % 2

    def block_copy(buf):
        return lambda local_row, sorted_row, rows: pltpu.make_async_copy(
            ys_ref.at[pl.ds(sorted_row, rows)], local_ref.at[buf, pl.ds(local_row, rows)], sem.at[buf])

    def start_gather(table_ref, buf):
        local_ref[buf, TOKEN_TILE * TOP_K:, :] = jnp.zeros((LOCAL_ROWS - TOKEN_TILE * TOP_K, D_MODEL), F32)
        _for_each_block(table_ref, lambda a, b, n: block_copy(buf)(a, b, n).start())

    @pl.when(i == 0)
    def _():
        start_gather(meta_ref, 0)

    @pl.when(i + 1 < pl.num_programs(0))
    def _():
        start_gather(next_meta_ref, 1 - cur)

    _for_each_block(meta_ref, lambda a, b, n: block_copy(cur)(a, b, n).wait())
    weights = _slot_matrix(slot_ref, [wt_ref[:, k:k + 1] for k in range(TOP_K)])
    acc = jnp.dot(weights, local_ref[cur].astype(BF16), preferred_element_type=F32)
    o_ref[...] = x_ref[...] + mod_ref[0, 5:6, :] * acc


def _combine(ys, slots, meta, wts, x, modseg):
    n_tok = x.shape[0]
    per_seg = SEG_ROWS // TOKEN_TILE
    full = pl.BlockSpec((TOKEN_TILE, D_MODEL), lambda i: (i, 0))
    small = pl.BlockSpec((TOKEN_TILE, LANES), lambda i: (i, 0))
    return pl.pallas_call(
        _combine_kernel,
        grid=(n_tok // TOKEN_TILE,),
        in_specs=[pl.BlockSpec((COPY_LIST_LEN,), lambda i: (i,), memory_space=pltpu.SMEM),
                  pl.BlockSpec((COPY_LIST_LEN,), lambda i: (jnp.minimum(i + 1, n_tok // TOKEN_TILE - 1),),
                               memory_space=pltpu.SMEM),
                  small, small,
                  pl.BlockSpec(memory_space=pl.ANY),
                  full,
                  pl.BlockSpec((1, N_MOD, D_MODEL), lambda i: (i // per_seg, 0, 0))],
        out_specs=full,
        out_shape=jax.ShapeDtypeStruct((n_tok, D_MODEL), F32),
        scratch_shapes=[pltpu.VMEM((2, LOCAL_ROWS, D_MODEL), F32), pltpu.SemaphoreType.DMA((2,))],
        compiler_params=_cparams(("arbitrary",), 56),
        name="combine",
    )(meta, meta, slots, wts, ys, x, modseg)


def _expert_kernel(te_ref, first_ref, slot_ref, next_ref, na_ref, x_ref, wgu_hbm, bgu_ref, wd_hbm, bd_ref,
                   o_ref, wgu_f, wd_f, wgu_s, wd_s, sem, *, layer):
    i = pl.program_id(0)
    active = i < na_ref[0]

    def fetch(e, s):
        return (pltpu.make_async_copy(wgu_hbm.at[layer, e], wgu_f.at[s], sem.at[0, s]),
                pltpu.make_async_copy(wd_hbm.at[layer, e], wd_f.at[s], sem.at[1, s]))

    @pl.when(jnp.logical_and(i == 0, first_ref[0] == 1))
    def _():
        for cp in fetch(te_ref[0], 0):
            cp.start()

    @pl.when(first_ref[i] == 1)
    def _():
        s = slot_ref[i]
        for cp in fetch(te_ref[i], s):
            cp.wait()
        wgu_s[...] = wgu_f[s].astype(BF16)
        wd_s[...] = wd_f[s].astype(BF16)

        @pl.when(next_ref[i] >= 0)
        def _():
            for cp in fetch(next_ref[i], 1 - s):
                cp.start()

    @pl.when(active)
    def _():
        gu = jnp.dot(x_ref[...].astype(BF16), wgu_s[...], preferred_element_type=F32) + bgu_ref[0, 0]
        gate = jnp.minimum(gu[:, :D_FF], SWIGLU_LIMIT)
        up = jnp.clip(gu[:, D_FF:], -SWIGLU_LIMIT, SWIGLU_LIMIT)
        act = gate * _sigmoid(SWIGLU_ALPHA * gate) * (up + 1.0)
        o_ref[...] = jnp.dot(act.astype(BF16), wd_s[...], preferred_element_type=F32) + bd_ref[0, 0]

    @pl.when(jnp.logical_not(active))
    def _():
        o_ref[...] = jnp.zeros_like(o_ref)


def _experts(xs, tile_expert, ends, n_active, w_gu, b_gu, w_down, b_down, layer):
    n_rows = xs.shape[0]
    depth = w_gu.shape[0]
    tm = EXPERT_TILE
    n_tiles = n_rows // tm
    tile = jnp.arange(n_tiles, dtype=jnp.int32)
    active = tile < n_active[0]
    first = (active & ((tile == 0) | (tile_expert != jnp.roll(tile_expert, 1)))).astype(jnp.int32)
    slot = (_prefix_sum(first) - 1) % 2
    next_tile = ends[tile_expert] // tm
    next_expert = jnp.where(next_tile < n_active[0],
                            tile_expert[jnp.minimum(next_tile, n_tiles - 1)], -1).astype(jnp.int32)

    def tiled(shape, index):
        return pl.BlockSpec(shape, lambda i, te, fi, sl, nx, na: index(i, te, na))

    grid_spec = pltpu.PrefetchScalarGridSpec(
        num_scalar_prefetch=5,
        grid=(n_tiles,),
        in_specs=[tiled((tm, D_MODEL), lambda i, te, na: (jnp.where(i < na[0], i, 0), 0)),
                  pl.BlockSpec(memory_space=pl.ANY),
                  tiled((1, 1, 1, 2 * D_FF), lambda i, te, na: (layer, te[i], 0, 0)),
                  pl.BlockSpec(memory_space=pl.ANY),
                  tiled((1, 1, 1, D_MODEL), lambda i, te, na: (layer, te[i], 0, 0))],
        out_specs=tiled((tm, D_MODEL), lambda i, te, na: (i, 0)),
        scratch_shapes=[pltpu.VMEM((2, D_MODEL, 2 * D_FF), F32), pltpu.VMEM((2, D_FF, D_MODEL), F32),
                        pltpu.VMEM((D_MODEL, 2 * D_FF), BF16), pltpu.VMEM((D_FF, D_MODEL), BF16),
                        pltpu.SemaphoreType.DMA((2, 2))],
    )
    return pl.pallas_call(
        functools.partial(_expert_kernel, layer=layer),
        grid_spec=grid_spec,
        out_shape=jax.ShapeDtypeStruct((n_rows, D_MODEL), F32),
        compiler_params=_cparams(("arbitrary",), 56),
        name="experts",
    )(tile_expert, first, slot.astype(jnp.int32), next_expert, n_active, xs, w_gu,
      b_gu.reshape(depth, N_EXPERTS, 1, 2 * D_FF), w_down, b_down.reshape(depth, N_EXPERTS, 1, D_MODEL))


def _moe(x, norm_g, modseg, w_router, b_router, w_gu, b_gu, w_down, b_down, layer):
    n_tok = x.shape[0]
    h, slots, wts, table = _router(x, norm_g, modseg, w_router, b_router, layer)
    tm = EXPERT_TILE
    n_tiles = n_tok // TOKEN_TILE
    n_rows = n_tok * TOP_K + n_tiles * N_EXPERTS * ROW_BLOCK + N_EXPERTS * tm
    n_rows = -(-n_rows // tm) * tm
    local_start = table[:, 0, :N_EXPERTS]
    padded = table[:, 1, :N_EXPERTS]
    earlier = table[:, 2, :N_EXPERTS]
    used = earlier[-1] + padded[-1]
    size = ((used + tm - 1) // tm) * tm
    ends = _prefix_sum(size)
    starts = ends - size
    sorted_start = starts[None, :] + earlier
    meta = _copy_lists(local_start, padded, sorted_start)
    tails = jnp.concatenate([starts + used, (size - used) // ROW_BLOCK,
                             ends[-1:], (n_rows - ends[-1:]) // tm,
                             jnp.zeros((LANES - 2 * N_EXPERTS - 2,), jnp.int32)])
    tile_row0 = jnp.arange(n_rows // tm, dtype=jnp.int32) * tm
    tile_expert = jnp.minimum(jnp.sum((ends[None, :] <= tile_row0[:, None]).astype(jnp.int32), axis=1),
                              N_EXPERTS - 1)
    n_active = (ends[-1:] // tm).astype(jnp.int32)
    xs = _dispatch(h, slots, meta, tails, n_rows)
    ys = _experts(xs, tile_expert, ends, n_active, w_gu, b_gu, w_down, b_down, layer)
    return _combine(ys, slots, meta, wts, x, modseg)


def _rope_tables(n_tok):
    rows = n_tok // GRID_W
    row = jnp.repeat(jnp.arange(rows, dtype=F32), GRID_W)
    col = jnp.tile(jnp.arange(GRID_W, dtype=F32), rows)
    quarter = HEAD_DIM // 4
    inv_freq = ROPE_BASE ** (-jnp.arange(quarter, dtype=F32) / quarter)
    ang_r = row[:, None] * inv_freq
    ang_c = col[:, None] * inv_freq
    ang = jnp.concatenate([ang_r, ang_r, ang_c, ang_c], axis=-1)
    cos, sin = jnp.cos(ang), jnp.sin(ang)
    first = (jnp.arange(HEAD_DIM) % (2 * quarter)) < quarter
    sin_a = jnp.where(first, -sin, 0.0)
    sin_b = jnp.where(first, 0.0, sin)
    return tuple(jnp.tile(t, (1, LANES // HEAD_DIM)) for t in (cos, sin_a, sin_b))


def kernel(x_prompt, x_sample, c, cache_k, cache_v, state_rglru, state_hgrn, c_ctx, w_mod, b_mod, norm1_g, norm2_g, w_in, q_norm_g, k_norm_g, conv_w, conv_b, rg_wa, rg_ba, rg_wx, rg_bx, rg_lambda, hgrn_lb_logits, hgrn_norm_g, w_attn_o, w_rnn_o, w_hgrn_o, w_out, w_router, b_router, w_gu, b_gu, w_down, b_down):
    depth = w_mod.shape[0]
    n_p, t_p = x_prompt.shape[0], x_prompt.shape[1]
    n_s, t_s = x_sample.shape[0], x_sample.shape[1]
    tok_p = n_p * t_p
    tok_s = n_s * t_s
    assert tok_p % SEG_ROWS == 0 and t_s == SEG_ROWS and SEG_ROWS % t_p == 0

    x = jnp.concatenate([x_prompt.reshape(tok_p, D_MODEL), x_sample.reshape(tok_s, D_MODEL)], axis=0)

    lb_all = jnp.moveaxis(_prefix_sum(jnp.moveaxis(jax.nn.softmax(hgrn_lb_logits.astype(F32), axis=0), 0, -1)),
                          -1, 0)
    lb_all = lb_all - lb_all[:1]
    rope = _rope_tables(t_s)

    cond8 = jnp.zeros((8, D_MODEL), F32).at[0].set(c_ctx).at[1:1 + n_s].set(c)
    mod = _modulation(cond8, w_mod, b_mod)
    seg_ids = jnp.asarray([0] * (tok_p // SEG_ROWS) + [1 + b for b in range(n_s)], jnp.int32)
    modsegs = mod[:, seg_ids].reshape(depth, seg_ids.shape[0], N_MOD, D_MODEL)

    blocks = jnp.stack([rg_wa, rg_wx], axis=2)
    eye = jnp.eye(RG_BLOCKS, dtype=blocks.dtype)
    w_gate = jnp.einsum("ldgnab,nm->lnadgmb", blocks, eye).reshape(depth, D_RNN, 4 * D_RNN)
    b_gate = jnp.stack([rg_ba, rg_bx], axis=2).reshape(depth, 1, 4 * D_RNN)

    new_k, new_v, new_hr, new_s = [], [], [], []
    for l in range(depth):
        modseg = modsegs[l]
        z = _input_projection(x, norm1_g, modseg, w_in, l)
        attn_p, k_l, v_l = _attention(z, q_norm_g, k_norm_g, l, n_p, t_p, 0)
        (attn_s,) = _attention(z, q_norm_g, k_norm_g, l, n_s, t_s, tok_p, rope=rope,
                               cache=(cache_k, cache_v))
        rnn_p, hr_l = _rglru(z, conv_w, conv_b, w_gate, b_gate, rg_lambda, l, n_p, t_p, 0)
        (rnn_s,) = _rglru(z, conv_w, conv_b, w_gate, b_gate, rg_lambda, l, n_s, t_s, tok_p,
                          state=state_rglru)
        hg_p, s_l = _hgrn(z, lb_all, hgrn_norm_g, l, n_p, t_p, 0)
        (hg_s,) = _hgrn(z, lb_all, hgrn_norm_g, l, n_s, t_s, tok_p, state=state_hgrn)
        x = _merge_project(((attn_p, attn_s), (rnn_p, rnn_s), (hg_p, hg_s)), z, x, modseg,
                           w_attn_o, w_rnn_o, w_hgrn_o, w_out, l)
        x = _moe(x, norm2_g, modseg, w_router, b_router, w_gu, b_gu, w_down, b_down, l)
        new_k.append(k_l.reshape(n_p, t_p, N_KV_HEADS, HEAD_DIM))
        new_v.append(v_l.reshape(n_p, t_p, N_KV_HEADS, HEAD_DIM))
        new_hr.append(hr_l)
        new_s.append(s_l)

    y_prompt = x[:tok_p].reshape(n_p, t_p, D_MODEL)
    y_sample = x[tok_p:].reshape(n_s, t_s, D_MODEL)
    return (y_prompt, y_sample, jnp.stack(new_k, axis=1), jnp.stack(new_v, axis=1),
            jnp.stack(new_hr, axis=1), jnp.stack(new_s, axis=1))
```

```python
import functools

import jax
import jax.numpy as jnp
from jax import lax
from jax.experimental import pallas as pl
from jax.experimental.pallas import tpu as pltpu

F32 = jnp.float32
BF16 = jnp.bfloat16

D_MODEL = 1024
GRID_W = 64
HEAD_DIM = 64
N_HEADS = 8
N_KV_HEADS = 2
KV_GROUP = N_HEADS // N_KV_HEADS
ATTN_W = N_HEADS * HEAD_DIM
KV_W = N_KV_HEADS * HEAD_DIM
ROPE_BASE = 10000.0
D_RNN = 512
RG_BLOCKS = 8
RG_BW = D_RNN // RG_BLOCKS
RG_C = 8.0
HG_HEADS = 8
HG_DK = 64
HG_DV = 64
HG_W = HG_HEADS * HG_DK
N_PAIRS = HG_W // 128
N_EXPERTS = 32
TOP_K = 4
D_FF = D_MODEL
SWIGLU_LIMIT = 7.0
SWIGLU_ALPHA = 1.702
N_MOD = 6
EPS = 1e-6
IN_W = ATTN_W + 2 * KV_W + 2 * D_RNN + 5 * HG_W + 3 * D_MODEL

LANES = 128
SEG_ROWS = 1024
COL_TILE = 256
HG_CHUNK = 256
Q_TILE = 256
EXPERT_TILE = 384
BIG_BLOCK = 64
TOKEN_TILE = 512
ROW_BLOCK = 8
LOCAL_ROWS = TOKEN_TILE * TOP_K + N_EXPERTS * ROW_BLOCK
NEG_BIG = -1e30
LOG2_E = 1.4426950408889634

_SRC_TILES = tuple(range(17, 29)) + tuple(range(7, 17)) + tuple(range(3, 7)) + (0, 1, 2)
Z_GL = 0
Z_HG = 3 * D_MODEL
Z_RG = Z_HG + 5 * HG_W
Z_Q = Z_RG + 2 * D_RNN
Z_K = Z_Q + ATTN_W
Z_V = Z_K + KV_W
Z_W = -(-IN_W // (2 * COL_TILE)) * (2 * COL_TILE)


def _cparams(sem, vmem_mb):
    return pltpu.CompilerParams(dimension_semantics=sem, vmem_limit_bytes=vmem_mb * 1024 * 1024)


def _prefix_sum(x):
    n = x.shape[-1]
    upto = (jnp.arange(n)[None, :] <= jnp.arange(n)[:, None]).astype(x.dtype)
    return jnp.sum(x[..., None, :] * upto, axis=-1)


def _mm(a, b):
    return jnp.dot(a.astype(BF16), b.astype(BF16), preferred_element_type=F32)


def _mm_nt(a, b):
    return lax.dot_general(a.astype(BF16), b.astype(BF16), (((1,), (1,)), ((), ())),
                           preferred_element_type=F32)


def _mm_tn(a, b):
    return lax.dot_general(a.astype(BF16), b.astype(BF16), (((0,), (0,)), ((), ())),
                           preferred_element_type=F32)


def _sigmoid(x):
    return 1.0 / (1.0 + jnp.exp(-x))


def _softplus(x):
    return jnp.maximum(x, 0.0) + jnp.log1p(jnp.exp(-jnp.abs(x)))


def _modulated_norm(x, g, scale, shift):
    y = x * lax.rsqrt(jnp.mean(x * x, axis=-1, keepdims=True) + EPS)
    return (y * g) * (1.0 + scale) + shift


def _mod_kernel(c_ref, w_ref, b_ref, o_ref):
    c = c_ref[...]
    s = c * _sigmoid(c)
    o_ref[0] = _mm(s, w_ref[0]) + b_ref[0]


def _modulation(cond8, w_mod, b_mod):
    depth = w_mod.shape[0]
    tn = 1024
    return pl.pallas_call(
        _mod_kernel,
        grid=(depth, N_MOD * D_MODEL // tn),
        in_specs=[pl.BlockSpec((ROW_BLOCK, D_MODEL), lambda l, j: (0, 0)),
                  pl.BlockSpec((1, D_MODEL, tn), lambda l, j: (l, 0, j)),
                  pl.BlockSpec((1, 1, tn), lambda l, j: (l, 0, j))],
        out_specs=pl.BlockSpec((1, ROW_BLOCK, tn), lambda l, j: (l, 0, j)),
        out_shape=jax.ShapeDtypeStruct((depth, ROW_BLOCK, N_MOD * D_MODEL), F32),
        compiler_params=_cparams(("arbitrary", "arbitrary"), 32),
        name="modulation",
    )(cond8, w_mod, b_mod.reshape(depth, 1, N_MOD * D_MODEL))


def _inproj_kernel(src_ref, x_ref, g_ref, mod_ref, wa_ref, wb_ref, o_ref, h_ref, w_s):
    j = pl.program_id(1)

    @pl.when(j == 0)
    def _():
        for s in range(x_ref.shape[0] // SEG_ROWS):
            rows = slice(s * SEG_ROWS, (s + 1) * SEG_ROWS)
            h = _modulated_norm(x_ref[rows, :], g_ref[0], mod_ref[s, 1:2, :], mod_ref[s, 0:1, :])
            h_ref[rows, :] = h.astype(BF16)

    @pl.when(pl.program_id(0) == 0)
    def _():
        w_s[j, :, 0:COL_TILE] = wa_ref[0].astype(BF16)
        w_s[j, :, COL_TILE:2 * COL_TILE] = wb_ref[0].astype(BF16)

    o_ref[...] = jnp.dot(h_ref[...], w_s[j], preferred_element_type=F32)


def _input_projection(x, norm_g, modseg, w_in, layer):
    n_tok = x.shape[0]
    segs = 2
    tm = segs * SEG_ROWS
    n_col = Z_W // (2 * COL_TILE)
    src = jnp.asarray(_SRC_TILES + (_SRC_TILES[-1],) * (Z_W // COL_TILE - len(_SRC_TILES)), jnp.int32)

    def wspec(k):
        return pl.BlockSpec((1, D_MODEL, COL_TILE),
                            lambda i, j, s: (layer, 0, s[2 * jnp.where(i == 0, j, n_col - 1) + k]))

    grid_spec = pltpu.PrefetchScalarGridSpec(
        num_scalar_prefetch=1,
        grid=(n_tok // tm, n_col),
        in_specs=[pl.BlockSpec((tm, D_MODEL), lambda i, j, s: (i, 0)),
                  pl.BlockSpec((1, 1, D_MODEL), lambda i, j, s: (layer, 0, 0)),
                  pl.BlockSpec((segs, N_MOD, D_MODEL), lambda i, j, s: (i, 0, 0)),
                  wspec(0), wspec(1)],
        out_specs=pl.BlockSpec((tm, 2 * COL_TILE), lambda i, j, s: (i, j)),
        scratch_shapes=[pltpu.VMEM((tm, D_MODEL), BF16),
                        pltpu.VMEM((n_col, D_MODEL, 2 * COL_TILE), BF16)],
    )
    return pl.pallas_call(
        _inproj_kernel,
        grid_spec=grid_spec,
        out_shape=jax.ShapeDtypeStruct((n_tok, Z_W), F32),
        compiler_params=_cparams(("arbitrary", "arbitrary"), 56),
        name="input_projection",
    )(src, x, norm_g.reshape(-1, 1, D_MODEL), modseg, w_in, w_in)


def _head_rms(x, g):
    lane = lax.broadcasted_iota(jnp.int32, x.shape, 1)
    lo = lane < HEAD_DIM
    xx = x * x
    s_lo = jnp.sum(jnp.where(lo, xx, 0.0), axis=-1, keepdims=True)
    s_hi = jnp.sum(jnp.where(lo, 0.0, xx), axis=-1, keepdims=True)
    inv = jnp.where(lo, lax.rsqrt(s_lo * (1.0 / HEAD_DIM) + EPS),
                    lax.rsqrt(s_hi * (1.0 / HEAD_DIM) + EPS))
    return x * inv * g


def _rope(x, cos, sin_a, sin_b):
    q = HEAD_DIM // 4
    return x * cos + pltpu.roll(x, LANES - q, 1) * sin_a + pltpu.roll(x, q, 1) * sin_b


def _attn_kernel(*refs, t_len, use_ctx):
    if use_ctx:
        (q_ref, k_ref, v_ref, qg_ref, kg_ref, cos_q_ref, sa_q_ref, sb_q_ref,
         cos_k_ref, sa_k_ref, sb_k_ref, ck_ref, cv_ref, o_ref, kall_ref, vall_ref) = refs
    else:
        (q_ref, k_ref, v_ref, qg_ref, kg_ref, o_ref, ko_ref, vo_ref, kall_ref, vall_ref) = refs

    @pl.when(pl.program_id(1) == 0)
    def _():
        k = _head_rms(k_ref[...], kg_ref[...])
        if use_ctx:
            k = _rope(k, cos_k_ref[...], sa_k_ref[...], sb_k_ref[...])
            kall_ref[0:t_len, :] = k.astype(BF16)
            vall_ref[0:t_len, :] = v_ref[...].astype(BF16)
            kall_ref[t_len:, :] = ck_ref[0, 0].astype(BF16)
            vall_ref[t_len:, :] = cv_ref[0, 0].astype(BF16)
        else:
            ko_ref[0] = k
            vo_ref[0] = v_ref[...]
            kall_ref[...] = k.astype(BF16)
            vall_ref[...] = v_ref[...].astype(BF16)

    heads = []
    for c in range(ATTN_W // LANES):
        qc = _head_rms(q_ref[:, c * LANES:(c + 1) * LANES], qg_ref[...])
        if use_ctx:
            qc = _rope(qc, cos_q_ref[...], sa_q_ref[...], sb_q_ref[...])
        qc = (qc * (HEAD_DIM ** -0.5)).astype(BF16)
        heads.append(qc[:, :HEAD_DIM])
        heads.append(qc[:, HEAD_DIM:])
    tq = q_ref.shape[0]
    for g in range(N_KV_HEADS):
        qs = jnp.concatenate(heads[g * KV_GROUP:(g + 1) * KV_GROUP], axis=0)
        kh = kall_ref[:, g * HEAD_DIM:(g + 1) * HEAD_DIM]
        vh = vall_ref[:, g * HEAD_DIM:(g + 1) * HEAD_DIM]
        s = _mm_nt(qs, kh)
        m = jnp.max(s, axis=-1, keepdims=True)
        p = jnp.exp(s - m)
        den = jnp.sum(p, axis=-1, keepdims=True)
        o = _mm(p, vh) / den
        for j in range(KV_GROUP):
            hh = g * KV_GROUP + j
            o_ref[:, hh * HEAD_DIM:(hh + 1) * HEAD_DIM] = o[j * tq:(j + 1) * tq].astype(BF16)


def _attention(z, q_g, k_g, layer, n_batch, t_len, row0, rope=None, cache=None):
    use_ctx = cache is not None
    nq = t_len // Q_TILE
    rb0 = row0 // t_len
    qb0 = row0 // Q_TILE
    qg = jnp.tile(q_g[layer], 2).reshape(1, LANES)
    kg = jnp.tile(k_g[layer], 2).reshape(1, LANES)
    n_out_tok = n_batch * t_len
    vec = pl.BlockSpec((1, LANES), lambda b, i: (0, 0))
    in_specs = [pl.BlockSpec((Q_TILE, ATTN_W), lambda b, i: (qb0 + b * nq + i, Z_Q // ATTN_W)),
                pl.BlockSpec((t_len, KV_W), lambda b, i: (rb0 + b, Z_K // KV_W)),
                pl.BlockSpec((t_len, KV_W), lambda b, i: (rb0 + b, Z_V // KV_W)),
                vec, vec]
    args = [z, z, z, qg, kg]
    out_specs = [pl.BlockSpec((Q_TILE, ATTN_W), lambda b, i: (b * nq + i, 0))]
    out_shape = [jax.ShapeDtypeStruct((n_out_tok, ATTN_W), BF16)]
    t_keys = t_len
    if use_ctx:
        cos, sin_a, sin_b = rope
        cache_k, cache_v = cache
        past = cache_k.shape[2]
        t_keys = t_len + past
        tq_spec = pl.BlockSpec((Q_TILE, LANES), lambda b, i: (i, 0))
        tk_spec = pl.BlockSpec((t_len, LANES), lambda b, i: (0, 0))
        c_spec = pl.BlockSpec((1, 1, past, KV_W), lambda b, i: (b, layer, 0, 0))
        in_specs += [tq_spec, tq_spec, tq_spec, tk_spec, tk_spec, tk_spec, c_spec, c_spec]
        args += [cos, sin_a, sin_b, cos, sin_a, sin_b,
                 cache_k.reshape(cache_k.shape[0], cache_k.shape[1], past, KV_W),
                 cache_v.reshape(cache_v.shape[0], cache_v.shape[1], past, KV_W)]
    else:
        kv_spec = pl.BlockSpec((1, t_len, KV_W), lambda b, i: (b, 0, 0))
        out_specs += [kv_spec, kv_spec]
        out_shape += [jax.ShapeDtypeStruct((n_batch, t_len, KV_W), F32)] * 2
    return pl.pallas_call(
        functools.partial(_attn_kernel, t_len=t_len, use_ctx=use_ctx),
        grid=(n_batch, nq),
        in_specs=in_specs,
        out_specs=out_specs,
        out_shape=out_shape,
        scratch_shapes=[pltpu.VMEM((t_keys, KV_W), BF16), pltpu.VMEM((t_keys, KV_W), BF16)],
        compiler_params=_cparams(("arbitrary", "arbitrary"), 48),
        name="attention_ctx" if use_ctx else "attention",
    )(*args)


def _shift_rows(x, d):
    n = x.shape[0]
    row = lax.broadcasted_iota(jnp.int32, x.shape, 0)
    y = pltpu.roll(x, d % n, 0)
    if d > 0:
        return jnp.where(row >= d, y, 0.0)
    return jnp.where(row < n + d, y, 0.0)


def _linear_scan(a, u, reverse):
    n, w = a.shape
    row = lax.broadcasted_iota(jnp.int32, a.shape, 0)

    def shifted(x, d, fill):
        if d % ROW_BLOCK == 0:
            pad = jnp.full((d, w), fill, x.dtype)
            return jnp.concatenate([x[d:], pad] if reverse else [pad, x[:n - d]], axis=0)
        if reverse:
            return jnp.where(row < n - d, pltpu.roll(x, n - d, 0), fill)
        return jnp.where(row >= d, pltpu.roll(x, d, 0), fill)

    d = 1
    while d < n:
        u = a * shifted(u, d, 0.0) + u
        a = a * shifted(a, d, 1.0)
        d *= 2
    return a, u


def _rglru_kernel(*refs, use_ctx):
    if use_ctx:
        (x_ref, y_ref, cw_ref, cb_ref, wg_ref, bg_ref, lam_ref, h0_ref, o_ref) = refs
    else:
        (x_ref, y_ref, cw_ref, cb_ref, wg_ref, bg_ref, lam_ref, o_ref, last_ref) = refs
    x = x_ref[...]
    cw = cw_ref[0]
    xr = cb_ref[0] + _shift_rows(x, 1) * cw[0:1] + x * cw[1:2] \
        + _shift_rows(x, -1) * cw[2:3] + _shift_rows(x, -2) * cw[3:4]
    gates = _sigmoid(_mm(xr, wg_ref[0]) + bg_ref[0])
    n = x.shape[0]
    total = None
    lasts = []
    for d in range(2):
        r = gates[:, (2 * d) * D_RNN:(2 * d + 1) * D_RNN]
        i = gates[:, (2 * d + 1) * D_RNN:(2 * d + 2) * D_RNN]
        log_a = (-RG_C * _softplus(-lam_ref[0, d:d + 1, :])) * r
        a = jnp.exp(log_a)
        u = jnp.sqrt(1.0 - a * a) * (i * xr)
        a_cum, h = _linear_scan(a, u, reverse=(d == 1))
        if use_ctx:
            h = h + a_cum * h0_ref[0, 0, d:d + 1, :]
        else:
            lasts.append(h[0:1] if d == 1 else h[n - 1:n])
        total = h if total is None else total + h
    o_ref[...] = (jax.nn.gelu(y_ref[...]) * total).astype(BF16)
    if not use_ctx:
        last_ref[0] = jnp.concatenate(lasts, axis=0)


def _rglru(z, conv_w, conv_b, w_gate, b_gate, lam, layer, n_batch, t_len, row0, state=None):
    use_ctx = state is not None
    rb0 = row0 // t_len
    in_specs = [pl.BlockSpec((t_len, D_RNN), lambda b: (rb0 + b, Z_RG // D_RNN)),
                pl.BlockSpec((t_len, D_RNN), lambda b: (rb0 + b, Z_RG // D_RNN + 1)),
                pl.BlockSpec((1, 4, D_RNN), lambda b: (layer, 0, 0)),
                pl.BlockSpec((1, 1, D_RNN), lambda b: (layer, 0, 0)),
                pl.BlockSpec((1, D_RNN, 4 * D_RNN), lambda b: (layer, 0, 0)),
                pl.BlockSpec((1, 1, 4 * D_RNN), lambda b: (layer, 0, 0)),
                pl.BlockSpec((1, 2, D_RNN), lambda b: (layer, 0, 0))]
    args = [z, z, conv_w, conv_b.reshape(-1, 1, D_RNN), w_gate, b_gate, lam]
    out_specs = [pl.BlockSpec((t_len, D_RNN), lambda b: (b, 0))]
    out_shape = [jax.ShapeDtypeStruct((n_batch * t_len, D_RNN), BF16)]
    if use_ctx:
        in_specs.append(pl.BlockSpec((1, 1, 2, D_RNN), lambda b: (b, layer, 0, 0)))
        args.append(state)
    else:
        out_specs.append(pl.BlockSpec((1, 2, D_RNN), lambda b: (b, 0, 0)))
        out_shape.append(jax.ShapeDtypeStruct((n_batch, 2, D_RNN), F32))
    return pl.pallas_call(
        functools.partial(_rglru_kernel, use_ctx=use_ctx),
        grid=(n_batch,),
        in_specs=in_specs,
        out_specs=out_specs,
        out_shape=out_shape,
        compiler_params=_cparams(("arbitrary",), 56),
        name="rglru_ctx" if use_ctx else "rglru",
    )(*args)


def _cumsum_rows(g, reverse):
    n, w = g.shape
    m = n // ROW_BLOCK

    def scan(x, size, pos):
        d = 1
        while d < size:
            if reverse:
                x = x + jnp.where(pos < size - d, pltpu.roll(x, x.shape[0] - d, 0), 0.0)
            else:
                x = x + jnp.where(pos >= d, pltpu.roll(x, d, 0), 0.0)
            d *= 2
        return x

    g = scan(g, ROW_BLOCK, lax.broadcasted_iota(jnp.int32, g.shape, 0) & (ROW_BLOCK - 1))
    g3 = g.reshape(m, ROW_BLOCK, w)
    total = g3[:, 0, :] if reverse else g3[:, ROW_BLOCK - 1, :]
    before = scan(total, m, lax.broadcasted_iota(jnp.int32, total.shape, 0)) - total
    return (g3 + before[:, None, :]).reshape(n, w)


def _block_row_fn(gc, reverse):
    n, w = gc.shape
    g3 = gc.reshape(n // ROW_BLOCK, ROW_BLOCK, w)
    edge_row = 0 if reverse else ROW_BLOCK - 1
    edge = jnp.broadcast_to(g3[:, edge_row:edge_row + 1, :], g3.shape)
    sub = lax.broadcasted_iota(jnp.int32, g3.shape, 1)

    def at(blk):
        r = blk // 2 if reverse else blk // 2 - 1
        if blk > ROW_BLOCK:
            groups = blk // ROW_BLOCK
            e4 = edge.reshape(n // blk, groups, ROW_BLOCK, w)
            pick = r // ROW_BLOCK
            return jnp.broadcast_to(e4[:, pick:pick + 1], e4.shape).reshape(n, w)
        out = jnp.broadcast_to(g3[:, r:r + 1, :], g3.shape)
        for b0 in range(blk, ROW_BLOCK, blk):
            out = jnp.where(sub >= b0, jnp.broadcast_to(g3[:, b0 + r:b0 + r + 1, :], g3.shape), out)
        return out.reshape(n, w)

    return at


def _first_head(shape):
    lane = lax.broadcasted_iota(jnp.int32, shape, len(shape) - 1)
    return (lane & (LANES - 1)) < HG_DK


def _pair(x, p):
    return x[:, p * LANES:(p + 1) * LANES]


def _hgrn_gates(f_logit, lb):
    y = jnp.exp(-jnp.abs(f_logit))
    log_sig = jnp.minimum(f_logit, 0.0) - jnp.log(1.0 + y)
    a = jnp.log(lb)
    b = jnp.log1p(-lb) + log_sig
    log_f = jnp.maximum(a, b) + jnp.log(1.0 + jnp.exp(-jnp.abs(a - b)))
    k = (1.0 - lb) * (jnp.where(f_logit >= 0.0, y, 1.0) / (1.0 + y))
    return log_f * LOG2_E, k


def _hgrn_chunk_state(gc, k, v, reverse):
    n = gc.shape[0]
    g_tot = gc[0:1] if reverse else gc[n - 1:n]
    k_end = (k * jnp.exp2(jnp.minimum(g_tot - gc, 0.0))).astype(BF16)
    vb = v.astype(BF16)
    same_head = (_first_head((LANES, LANES))
                 == (lax.broadcasted_iota(jnp.int32, (LANES, LANES), 0) < HG_DK))
    ds = [jnp.where(same_head, _mm_tn(_pair(k_end, p), _pair(vb, p)), 0.0) for p in range(N_PAIRS)]
    return ds, jnp.exp2(g_tot)


def _decay_state(s, e_row):
    n = s.shape[0]
    eye = (lax.broadcasted_iota(jnp.int32, (n, n), 0) == lax.broadcasted_iota(jnp.int32, (n, n), 1))
    col = jnp.sum(jnp.where(eye, jnp.broadcast_to(e_row, (n, n)), 0.0), axis=1, keepdims=True)
    return s * col


def _hgrn_intra_scores(q, ks, gcs, txs):
    n = q.shape[0]
    row = lax.broadcasted_iota(jnp.int32, q.shape, 0)
    first = _first_head(q.shape)
    boundary = [_block_row_fn(gcs[0], False), _block_row_fn(gcs[1], True)]
    scores = [None] * HG_HEADS
    w = 1
    while w < n:
        blk = 2 * w
        second = (row & w) != 0
        e_f = jnp.exp2(-jnp.abs(gcs[0] - boundary[0](blk)))
        e_b = jnp.exp2(-jnp.abs(gcs[1] - boundary[1](blk)))
        q_f = jnp.where(second, q * e_f, 0.0).astype(BF16)
        q_b = jnp.where(second, 0.0, q * e_b).astype(BF16)
        k_f = jnp.where(second, 0.0, ks[0] * e_f)
        k_b = jnp.where(second, ks[1] * e_b, 0.0)
        k_heads = [(jnp.where(first, k_f, 0.0).astype(BF16), jnp.where(first, k_b, 0.0).astype(BF16)),
                   (jnp.where(first, 0.0, k_f).astype(BF16), jnp.where(first, 0.0, k_b).astype(BF16))]
        sel = (txs >= w) & (txs < blk)
        for p in range(N_PAIRS):
            qc = jnp.concatenate([_pair(q_f, p), _pair(q_b, p)], axis=1)
            for j in range(2):
                kc = jnp.concatenate([_pair(k_heads[j][0], p), _pair(k_heads[j][1], p)], axis=1)
                s = jnp.where(sel, _mm_nt(qc, kc), 0.0)
                h = 2 * p + j
                scores[h] = s if scores[h] is None else scores[h] + s
        w = blk
    return scores


def _hgrn_kernel(*refs, t_len, use_ctx):
    if use_ctx:
        (q_ref, ff_ref, fb_ref, i_ref, og_ref, lb_ref, ng_ref, s0_ref, o_ref, sin_ref, gk_ref) = refs
    else:
        (q_ref, ff_ref, fb_ref, i_ref, og_ref, lb_ref, ng_ref, o_ref, sl_ref) = refs
    c = HG_CHUNK
    n_chunks = t_len // c
    f_refs = (ff_ref, fb_ref)

    def load_dir(d, r0):
        log_f, k = _hgrn_gates(f_refs[d][pl.ds(r0, c), :], lb_ref[0, d:d + 1, :])
        return _cumsum_rows(log_f, reverse=(d == 1)), k

    if use_ctx:
        zero = jnp.zeros((HG_DK, HG_DV), F32)
        for d in range(2):
            states = []
            for p in range(N_PAIRS):
                top = jnp.concatenate([s0_ref[0, 0, d, 2 * p], zero], axis=1)
                bot = jnp.concatenate([zero, s0_ref[0, 0, d, 2 * p + 1]], axis=1)
                states.append(jnp.concatenate([top, bot], axis=0))
            order = range(n_chunks) if d == 0 else range(n_chunks - 1, -1, -1)
            for ci in order:
                for p in range(N_PAIRS):
                    sin_ref[ci, p, d * LANES:(d + 1) * LANES, :] = states[p]
                gc, k = load_dir(d, ci * c)
                gk_ref[d, 0, pl.ds(ci * c, c), :] = gc
                gk_ref[d, 1, pl.ds(ci * c, c), :] = k
                ds, e_tot = _hgrn_chunk_state(gc, k, i_ref[pl.ds(ci * c, c), :], reverse=(d == 1))
                states = [_decay_state(states[p], _pair(e_tot, p)) + ds[p] for p in range(N_PAIRS)]

    ti = lax.broadcasted_iota(jnp.int32, (c, c), 0)
    si = lax.broadcasted_iota(jnp.int32, (c, c), 1)
    txs = ti ^ si
    diag = ti == si

    def chunk_body(ci, carry):
        r0 = pl.multiple_of(ci * c, c)
        hq = q_ref[pl.ds(r0, c), :]
        q = hq * _sigmoid(hq)
        v = i_ref[pl.ds(r0, c), :]
        gcs, ks = [], []
        for d in range(2):
            if use_ctx:
                gc, k = gk_ref[d, 0, pl.ds(r0, c), :], gk_ref[d, 1, pl.ds(r0, c), :]
            else:
                gc, k = load_dir(d, r0)
            gcs.append(gc)
            ks.append(k)
        scores = _hgrn_intra_scores(q, ks, gcs, txs)
        first = _first_head(q.shape)
        k_sum = ks[0] + ks[1]
        v_heads = (jnp.where(first, v, 0.0).astype(BF16), jnp.where(first, 0.0, v).astype(BF16))
        k_heads = (jnp.where(first, k_sum, 0.0).astype(BF16), jnp.where(first, 0.0, k_sum).astype(BF16))
        q_bf = q.astype(BF16)
        if use_ctx:
            qe = [(q * jnp.exp2(gcs[d])).astype(BF16) for d in range(2)]
        else:
            for d in range(2):
                ds, _ = _hgrn_chunk_state(gcs[d], ks[d], v, reverse=(d == 1))
                for p in range(N_PAIRS):
                    sl_ref[0, d, 2 * p] = ds[p][:HG_DK, :HG_DV]
                    sl_ref[0, d, 2 * p + 1] = ds[p][HG_DK:, HG_DV:]
        outs = []
        for p in range(N_PAIRS):
            o = None
            for j in range(2):
                s_diag = jnp.where(diag, _mm_nt(_pair(q_bf, p), _pair(k_heads[j], p)), 0.0)
                t = _mm(scores[2 * p + j] + s_diag, _pair(v_heads[j], p))
                o = t if o is None else o + t
            if use_ctx:
                o = o + _mm(jnp.concatenate([_pair(qe[0], p), _pair(qe[1], p)], axis=1), sin_ref[ci, p])
            outs.append(_head_rms(o, ng_ref[0]))
        og = og_ref[pl.ds(r0, c), :]
        o_all = jnp.concatenate(outs, axis=1) * (og * _sigmoid(og))
        o_ref[pl.ds(r0, c), :] = o_all.astype(BF16)
        return carry

    if n_chunks == 1:
        chunk_body(0, 0)
    else:
        lax.fori_loop(0, n_chunks, chunk_body, 0)


def _hgrn(z, lb, norm_g, layer, n_batch, t_len, row0, state=None):
    use_ctx = state is not None
    rb0 = row0 // t_len
    cb = Z_HG // HG_W

    def zspec(k):
        return pl.BlockSpec((t_len, HG_W), lambda b: (rb0 + b, cb + k))

    in_specs = [zspec(0), zspec(1), zspec(2), zspec(3), zspec(4),
                pl.BlockSpec((1, 2, HG_W), lambda b: (layer, 0, 0)),
                pl.BlockSpec((1, 1, LANES), lambda b: (layer, 0, 0))]
    args = [z, z, z, z, z, lb, jnp.tile(norm_g, (1, LANES // HG_DV)).reshape(-1, 1, LANES)]
    out_specs = [pl.BlockSpec((t_len, HG_W), lambda b: (b, 0))]
    out_shape = [jax.ShapeDtypeStruct((n_batch * t_len, HG_W), BF16)]
    scratch = []
    if use_ctx:
        in_specs.append(pl.BlockSpec((1, 1, 2, HG_HEADS, HG_DK, HG_DV), lambda b: (b, layer, 0, 0, 0, 0)))
        args.append(state)
        scratch.append(pltpu.VMEM((t_len // HG_CHUNK, N_PAIRS, 2 * LANES, LANES), F32))
        scratch.append(pltpu.VMEM((2, 2, t_len, HG_W), F32))
    else:
        out_specs.append(pl.BlockSpec((1, 2, HG_HEADS, HG_DK, HG_DV), lambda b: (b, 0, 0, 0, 0)))
        out_shape.append(jax.ShapeDtypeStruct((n_batch, 2, HG_HEADS, HG_DK, HG_DV), F32))
    return pl.pallas_call(
        functools.partial(_hgrn_kernel, t_len=t_len, use_ctx=use_ctx),
        grid=(n_batch,),
        in_specs=in_specs,
        out_specs=out_specs,
        out_shape=out_shape,
        scratch_shapes=scratch,
        compiler_params=_cparams(("arbitrary",), 56),
        name="hgrn_ctx" if use_ctx else "hgrn",
    )(*args)


def _merge_kernel(ap_ref, as_ref, rp_ref, rs_ref, hp_ref, hs_ref, g0_ref, g1_ref, g2_ref, x_ref, mod_ref,
                  wa_ref, wr_ref, wh_ref, wo_ref, o_ref, wa_s, wr_s, wh_s, wo_s, *, prompt_tiles):
    @pl.when(pl.program_id(0) == 0)
    def _():
        wa_s[...] = wa_ref[0].astype(BF16)
        wr_s[...] = wr_ref[0].astype(BF16)
        wh_s[...] = wh_ref[0].astype(BF16)
        wo_s[...] = wo_ref[0].astype(BF16)

    is_prompt = pl.program_id(0) < prompt_tiles

    def proj(p_ref, s_ref, w_s):
        return jnp.dot(jnp.where(is_prompt, p_ref[...], s_ref[...]), w_s[...], preferred_element_type=F32)

    merged = (_sigmoid(g0_ref[...]) * proj(ap_ref, as_ref, wa_s)
              + _sigmoid(g1_ref[...]) * proj(rp_ref, rs_ref, wr_s)
              + _sigmoid(g2_ref[...]) * proj(hp_ref, hs_ref, wh_s))
    out = jnp.dot(merged.astype(BF16), wo_s[...], preferred_element_type=F32)
    o_ref[...] = x_ref[...] + mod_ref[0, 2:3, :] * out


def _merge_project(mixed, z, x, modseg, w_attn_o, w_rnn_o, w_hgrn_o, w_out, layer):
    n_tok = x.shape[0]
    tm = 512
    per_seg = SEG_ROWS // tm
    prompt_tiles = mixed[0][0].shape[0] // tm
    half_p = pl.BlockSpec((tm, ATTN_W), lambda i: (jnp.minimum(i, prompt_tiles - 1), 0))
    half_s = pl.BlockSpec((tm, ATTN_W), lambda i: (jnp.maximum(i - prompt_tiles, 0), 0))
    full = pl.BlockSpec((tm, D_MODEL), lambda i: (i, 0))

    def gspec(k):
        return pl.BlockSpec((tm, D_MODEL), lambda i: (i, Z_GL // D_MODEL + k))

    def wspec(rows):
        return pl.BlockSpec((1, rows, D_MODEL), lambda i: (layer, 0, 0))

    return pl.pallas_call(
        functools.partial(_merge_kernel, prompt_tiles=prompt_tiles),
        grid=(n_tok // tm,),
        in_specs=[half_p, half_s, half_p, half_s, half_p, half_s, gspec(0), gspec(1), gspec(2), full,
                  pl.BlockSpec((1, N_MOD, D_MODEL), lambda i: (i // per_seg, 0, 0)),
                  wspec(ATTN_W), wspec(D_RNN), wspec(HG_W), wspec(D_MODEL)],
        out_specs=full,
        out_shape=jax.ShapeDtypeStruct((n_tok, D_MODEL), F32),
        scratch_shapes=[pltpu.VMEM((ATTN_W, D_MODEL), BF16), pltpu.VMEM((D_RNN, D_MODEL), BF16),
                        pltpu.VMEM((HG_W, D_MODEL), BF16), pltpu.VMEM((D_MODEL, D_MODEL), BF16)],
        compiler_params=_cparams(("arbitrary",), 56),
        name="merge_project",
    )(*[part for pair in mixed for part in pair], z, z, z, x, modseg, w_attn_o, w_rnn_o, w_hgrn_o, w_out)


def _router_kernel(x_ref, g_ref, mod_ref, wr_ref, br_ref, h_ref, slot_ref, wt_ref, meta_ref, run_ref):
    @pl.when(pl.program_id(0) == 0)
    def _():
        run_ref[...] = jnp.zeros_like(run_ref)

    h = _modulated_norm(x_ref[...], g_ref[0], mod_ref[0, 4:5, :], mod_ref[0, 3:4, :])
    h_ref[...] = h
    h_hi = h.astype(BF16)
    h_lo = (h - h_hi.astype(F32)).astype(BF16)
    w = wr_ref[0]
    w_hi = w.astype(BF16)
    w_lo = (w - w_hi.astype(F32)).astype(BF16)
    logits = (jnp.dot(h_hi, w_hi, preferred_element_type=F32) + jnp.dot(h_hi, w_lo, preferred_element_type=F32)
              + jnp.dot(h_lo, w_hi, preferred_element_type=F32)) + br_ref[0]
    tm = logits.shape[0]
    lane = lax.broadcasted_iota(jnp.int32, logits.shape, 1)
    ids, vals = [], []
    chosen = jnp.zeros(logits.shape, jnp.bool_)
    work = logits
    for _ in range(TOP_K):
        m = jnp.max(work, axis=-1, keepdims=True)
        idx = jnp.min(jnp.where(work == m, lane, LANES), axis=-1, keepdims=True)
        hit = lane == idx
        chosen = chosen | hit
        work = jnp.where(hit, -jnp.inf, work)
        ids.append(idx)
        vals.append(m)
    exps = [jnp.exp(v - vals[0]) for v in vals]
    den = exps[0] + exps[1] + exps[2] + exps[3]
    ind = jnp.where(chosen, 1.0, 0.0)
    r_i = lax.broadcasted_iota(jnp.int32, (tm, tm), 0)
    c_i = lax.broadcasted_iota(jnp.int32, (tm, tm), 1)
    count = jnp.sum(ind, axis=0, keepdims=True)
    padded = jnp.floor((count + (ROW_BLOCK - 1)) * (1.0 / ROW_BLOCK)) * ROW_BLOCK
    incl = jnp.broadcast_to(padded, (ROW_BLOCK, LANES))
    lane8 = lax.broadcasted_iota(jnp.int32, (ROW_BLOCK, LANES), 1)
    d = 1
    while d < LANES:
        incl = incl + jnp.where(lane8 >= d, pltpu.roll(incl, d, 1), 0.0)
        d *= 2
    local_start = incl[0:1] - padded
    place = _mm(jnp.where(c_i < r_i, 1.0, 0.0), ind) + local_start
    slot_out = jnp.zeros(logits.shape, jnp.int32)
    wt_out = jnp.zeros(logits.shape, F32)
    for k in range(TOP_K):
        slot = jnp.sum(jnp.where(lane == ids[k], place, 0.0), axis=-1, keepdims=True)
        slot_out = jnp.where(lane == k, slot.astype(jnp.int32), slot_out)
        wt_out = jnp.where(lane == k, exps[k] / den, wt_out)
    slot_ref[...] = slot_out
    wt_ref[...] = wt_out
    row8 = lax.broadcasted_iota(jnp.int32, (ROW_BLOCK, LANES), 0)
    table = jnp.where(row8 == 0, local_start, jnp.where(row8 == 1, padded, run_ref[0:1, :]))
    meta_ref[0] = jnp.where(row8 < 3, table, 0.0).astype(jnp.int32)
    run_ref[...] = run_ref[...] + padded


def _router(x, norm_g, modseg, w_router, b_router, layer):
    n_tok = x.shape[0]
    tm = TOKEN_TILE
    per_seg = SEG_ROWS // tm
    depth = w_router.shape[0]
    w_pad = jnp.zeros((depth, D_MODEL, LANES), F32).at[:, :, :N_EXPERTS].set(w_router)
    b_pad = jnp.full((depth, 1, LANES), NEG_BIG, F32).at[:, 0, :N_EXPERTS].set(b_router)
    full = pl.BlockSpec((tm, D_MODEL), lambda i: (i, 0))
    small = pl.BlockSpec((tm, LANES), lambda i: (i, 0))
    return pl.pallas_call(
        _router_kernel,
        grid=(n_tok // tm,),
        in_specs=[full,
                  pl.BlockSpec((1, 1, D_MODEL), lambda i: (layer, 0, 0)),
                  pl.BlockSpec((1, N_MOD, D_MODEL), lambda i: (i // per_seg, 0, 0)),
                  pl.BlockSpec((1, D_MODEL, LANES), lambda i: (layer, 0, 0)),
                  pl.BlockSpec((1, 1, LANES), lambda i: (layer, 0, 0))],
        out_specs=[full, small, small, pl.BlockSpec((1, ROW_BLOCK, LANES), lambda i: (i, 0, 0))],
        out_shape=[jax.ShapeDtypeStruct((n_tok, D_MODEL), F32),
                   jax.ShapeDtypeStruct((n_tok, LANES), jnp.int32),
                   jax.ShapeDtypeStruct((n_tok, LANES), F32),
                   jax.ShapeDtypeStruct((n_tok // tm, ROW_BLOCK, LANES), jnp.int32)],
        scratch_shapes=[pltpu.VMEM((ROW_BLOCK, LANES), F32)],
        compiler_params=_cparams(("arbitrary",), 40),
        name="router",
    )(x, norm_g.reshape(-1, 1, D_MODEL), modseg, w_pad, b_pad)


MAX_BIG = LOCAL_ROWS // BIG_BLOCK
MAX_SMALL = N_EXPERTS * (BIG_BLOCK // ROW_BLOCK - 1)
COPY_LIST_LEN = 1024
_BIG_AT = 2
_SMALL_AT = 2 + 2 * MAX_BIG
assert _SMALL_AT + 2 * MAX_SMALL <= COPY_LIST_LEN


def _copy_lists(local_start, padded, sorted_start):
    n_big = padded // BIG_BLOCK
    n_small = (padded - n_big * BIG_BLOCK) // ROW_BLOCK

    def expand(count, local0, sorted0, step, cap):
        last = _prefix_sum(count)[:, None, :]
        first = last - count[:, None, :]
        f = jnp.arange(cap, dtype=jnp.int32)[None, :, None]
        mine = ((first <= f) & (f < last)).astype(jnp.int32)
        offset = step * (f - first)
        return (last[:, 0, -1:], jnp.sum(mine * (local0[:, None, :] + offset), axis=2),
                jnp.sum(mine * (sorted0[:, None, :] + offset), axis=2))

    cnt_b, loc_b, srt_b = expand(n_big, local_start, sorted_start, BIG_BLOCK, MAX_BIG)
    done = n_big * BIG_BLOCK
    cnt_s, loc_s, srt_s = expand(n_small, local_start + done, sorted_start + done, ROW_BLOCK, MAX_SMALL)
    parts = [cnt_b, cnt_s, loc_b, srt_b, loc_s, srt_s]
    used = sum(p.shape[1] for p in parts)
    parts.append(jnp.zeros((padded.shape[0], COPY_LIST_LEN - used), jnp.int32))
    return jnp.concatenate(parts, axis=1).astype(jnp.int32).reshape(-1)


def _for_each_block(list_ref, fn):
    def big(j, c):
        fn(pl.multiple_of(list_ref[_BIG_AT + j], ROW_BLOCK),
           pl.multiple_of(list_ref[_BIG_AT + MAX_BIG + j], ROW_BLOCK), BIG_BLOCK)
        return c

    def small(j, c):
        fn(pl.multiple_of(list_ref[_SMALL_AT + j], ROW_BLOCK),
           pl.multiple_of(list_ref[_SMALL_AT + MAX_SMALL + j], ROW_BLOCK), ROW_BLOCK)
        return c

    lax.fori_loop(0, list_ref[0], big, 0)
    lax.fori_loop(0, list_ref[1], small, 0)


def _slot_matrix(slot_ref, values):
    col = lax.broadcasted_iota(jnp.int32, (TOKEN_TILE, LOCAL_ROWS), 1)
    out = jnp.zeros((TOKEN_TILE, LOCAL_ROWS), F32)
    for k in range(TOP_K):
        out = jnp.where(col == slot_ref[:, k:k + 1], values[k], out)
    return out.astype(BF16)


def _dispatch_kernel(meta_ref, prev_meta_ref, tail_ref, slot_ref, h_ref, xs_ref, local_ref, zero_ref,
                     sem, block_sem):
    i = pl.program_id(0)
    cur = i % 2

    @pl.when(i == 0)
    def _():
        zero_ref[...] = jnp.zeros_like(zero_ref)

        def tails(fn):
            def group(e, carry):
                def block(j, c):
                    fn(pl.multiple_of(tail_ref[e] + j * ROW_BLOCK, ROW_BLOCK))
                    return c
                return lax.fori_loop(0, tail_ref[N_EXPERTS + e], block, carry)
            lax.fori_loop(0, N_EXPERTS, group, 0)

        def zero_copy(row):
            return pltpu.make_async_copy(zero_ref.at[pl.ds(0, ROW_BLOCK)],
                                         xs_ref.at[pl.ds(row, ROW_BLOCK)], sem)

        tails(lambda row: zero_copy(row).start())
        tails(lambda row: zero_copy(row).wait())

        def unused(fn):
            def tile(j, c):
                fn(pl.multiple_of(tail_ref[2 * N_EXPERTS] + j * EXPERT_TILE, EXPERT_TILE))
                return c
            lax.fori_loop(0, tail_ref[2 * N_EXPERTS + 1], tile, 0)

        def zero_tile(row):
            return pltpu.make_async_copy(zero_ref, xs_ref.at[pl.ds(row, EXPERT_TILE)], sem)

        unused(lambda row: zero_tile(row).start())
        unused(lambda row: zero_tile(row).wait())

    onehot = _slot_matrix(slot_ref, [1.0] * TOP_K)
    local_ref[cur] = _mm_tn(onehot, h_ref[...])

    def block_copy(buf):
        return lambda local_row, sorted_row, rows: pltpu.make_async_copy(
            local_ref.at[buf, pl.ds(local_row, rows)], xs_ref.at[pl.ds(sorted_row, rows)],
            block_sem.at[buf])

    _for_each_block(meta_ref, lambda a, b, n: block_copy(cur)(a, b, n).start())

    @pl.when(i > 0)
    def _():
        _for_each_block(prev_meta_ref, lambda a, b, n: block_copy(1 - cur)(a, b, n).wait())

    @pl.when(i == pl.num_programs(0) - 1)
    def _():
        _for_each_block(meta_ref, lambda a, b, n: block_copy(cur)(a, b, n).wait())


def _dispatch(h, slots, meta, tails, n_rows):
    n_tok = h.shape[0]
    return pl.pallas_call(
        _dispatch_kernel,
        grid=(n_tok // TOKEN_TILE,),
        in_specs=[pl.BlockSpec((COPY_LIST_LEN,), lambda i: (i,), memory_space=pltpu.SMEM),
                  pl.BlockSpec((COPY_LIST_LEN,), lambda i: (jnp.maximum(i - 1, 0),), memory_space=pltpu.SMEM),
                  pl.BlockSpec((LANES,), lambda i: (0,), memory_space=pltpu.SMEM),
                  pl.BlockSpec((TOKEN_TILE, LANES), lambda i: (i, 0)),
                  pl.BlockSpec((TOKEN_TILE, D_MODEL), lambda i: (i, 0))],
        out_specs=pl.BlockSpec(memory_space=pl.ANY),
        out_shape=jax.ShapeDtypeStruct((n_rows, D_MODEL), F32),
        scratch_shapes=[pltpu.VMEM((2, LOCAL_ROWS, D_MODEL), F32), pltpu.VMEM((EXPERT_TILE, D_MODEL), F32),
                        pltpu.SemaphoreType.DMA(()), pltpu.SemaphoreType.DMA((2,))],
        compiler_params=_cparams(("arbitrary",), 48),
        name="dispatch",
    )(meta, meta, tails, slots, h)


def _combine_kernel(meta_ref, next_meta_ref, slot_ref, wt_ref, ys_ref, x_ref, mod_ref, o_ref, local_ref, sem):
    i = pl.program_id(0)
    cur = i % 2

    def block_copy(buf):
        return lambda local_row, sorted_row, rows: pltpu.make_async_copy(
            ys_ref.at[pl.ds(sorted_row, rows)], local_ref.at[buf, pl.ds(local_row, rows)], sem.at[buf])

    def start_gather(table_ref, buf):
        local_ref[buf, TOKEN_TILE * TOP_K:, :] = jnp.zeros((LOCAL_ROWS - TOKEN_TILE * TOP_K, D_MODEL), F32)
        _for_each_block(table_ref, lambda a, b, n: block_copy(buf)(a, b, n).start())

    @pl.when(i == 0)
    def _():
        start_gather(meta_ref, 0)

    @pl.when(i + 1 < pl.num_programs(0))
    def _():
        start_gather(next_meta_ref, 1 - cur)

    _for_each_block(meta_ref, lambda a, b, n: block_copy(cur)(a, b, n).wait())
    weights = _slot_matrix(slot_ref, [wt_ref[:, k:k + 1] for k in range(TOP_K)])
    acc = jnp.dot(weights, local_ref[cur].astype(BF16), preferred_element_type=F32)
    o_ref[...] = x_ref[...] + mod_ref[0, 5:6, :] * acc


def _combine(ys, slots, meta, wts, x, modseg):
    n_tok = x.shape[0]
    per_seg = SEG_ROWS // TOKEN_TILE
    full = pl.BlockSpec((TOKEN_TILE, D_MODEL), lambda i: (i, 0))
    small = pl.BlockSpec((TOKEN_TILE, LANES), lambda i: (i, 0))
    return pl.pallas_call(
        _combine_kernel,
        grid=(n_tok // TOKEN_TILE,),
        in_specs=[pl.BlockSpec((COPY_LIST_LEN,), lambda i: (i,), memory_space=pltpu.SMEM),
                  pl.BlockSpec((COPY_LIST_LEN,), lambda i: (jnp.minimum(i + 1, n_tok // TOKEN_TILE - 1),),
                               memory_space=pltpu.SMEM),
                  small, small,
                  pl.BlockSpec(memory_space=pl.ANY),
                  full,
                  pl.BlockSpec((1, N_MOD, D_MODEL), lambda i: (i // per_seg, 0, 0))],
        out_specs=full,
        out_shape=jax.ShapeDtypeStruct((n_tok, D_MODEL), F32),
        scratch_shapes=[pltpu.VMEM((2, LOCAL_ROWS, D_MODEL), F32), pltpu.SemaphoreType.DMA((2,))],
        compiler_params=_cparams(("arbitrary",), 56),
        name="combine",
    )(meta, meta, slots, wts, ys, x, modseg)


def _expert_kernel(te_ref, first_ref, slot_ref, next_ref, na_ref, x_ref, wgu_hbm, bgu_ref, wd_hbm, bd_ref,
                   o_ref, wgu_f, wd_f, wgu_s, wd_s, sem, *, layer):
    i = pl.program_id(0)
    active = i < na_ref[0]

    def fetch(e, s):
        return (pltpu.make_async_copy(wgu_hbm.at[layer, e], wgu_f.at[s], sem.at[0, s]),
                pltpu.make_async_copy(wd_hbm.at[layer, e], wd_f.at[s], sem.at[1, s]))

    @pl.when(jnp.logical_and(i == 0, first_ref[0] == 1))
    def _():
        for cp in fetch(te_ref[0], 0):
            cp.start()

    @pl.when(first_ref[i] == 1)
    def _():
        s = slot_ref[i]
        for cp in fetch(te_ref[i], s):
            cp.wait()
        wgu_s[...] = wgu_f[s].astype(BF16)
        wd_s[...] = wd_f[s].astype(BF16)

        @pl.when(next_ref[i] >= 0)
        def _():
            for cp in fetch(next_ref[i], 1 - s):
                cp.start()

    @pl.when(active)
    def _():
        gu = jnp.dot(x_ref[...].astype(BF16), wgu_s[...], preferred_element_type=F32) + bgu_ref[0, 0]
        gate = jnp.minimum(gu[:, :D_FF], SWIGLU_LIMIT)
        up = jnp.clip(gu[:, D_FF:], -SWIGLU_LIMIT, SWIGLU_LIMIT)
        act = gate * _sigmoid(SWIGLU_ALPHA * gate) * (up + 1.0)
        o_ref[...] = jnp.dot(act.astype(BF16), wd_s[...], preferred_element_type=F32) + bd_ref[0, 0]

    @pl.when(jnp.logical_not(active))
    def _():
        o_ref[...] = jnp.zeros_like(o_ref)


def _experts(xs, tile_expert, ends, n_active, w_gu, b_gu, w_down, b_down, layer):
    n_rows = xs.shape[0]
    depth = w_gu.shape[0]
    tm = EXPERT_TILE
    n_tiles = n_rows // tm
    tile = jnp.arange(n_tiles, dtype=jnp.int32)
    active = tile < n_active[0]
    first = (active & ((tile == 0) | (tile_expert != jnp.roll(tile_expert, 1)))).astype(jnp.int32)
    slot = (_prefix_sum(first) - 1) % 2
    next_tile = ends[tile_expert] // tm
    next_expert = jnp.where(next_tile < n_active[0],
                            tile_expert[jnp.minimum(next_tile, n_tiles - 1)], -1).astype(jnp.int32)

    def tiled(shape, index):
        return pl.BlockSpec(shape, lambda i, te, fi, sl, nx, na: index(i, te, na))

    grid_spec = pltpu.PrefetchScalarGridSpec(
        num_scalar_prefetch=5,
        grid=(n_tiles,),
        in_specs=[tiled((tm, D_MODEL), lambda i, te, na: (jnp.where(i < na[0], i, 0), 0)),
                  pl.BlockSpec(memory_space=pl.ANY),
                  tiled((1, 1, 1, 2 * D_FF), lambda i, te, na: (layer, te[i], 0, 0)),
                  pl.BlockSpec(memory_space=pl.ANY),
                  tiled((1, 1, 1, D_MODEL), lambda i, te, na: (layer, te[i], 0, 0))],
        out_specs=tiled((tm, D_MODEL), lambda i, te, na: (i, 0)),
        scratch_shapes=[pltpu.VMEM((2, D_MODEL, 2 * D_FF), F32), pltpu.VMEM((2, D_FF, D_MODEL), F32),
                        pltpu.VMEM((D_MODEL, 2 * D_FF), BF16), pltpu.VMEM((D_FF, D_MODEL), BF16),
                        pltpu.SemaphoreType.DMA((2, 2))],
    )
    return pl.pallas_call(
        functools.partial(_expert_kernel, layer=layer),
        grid_spec=grid_spec,
        out_shape=jax.ShapeDtypeStruct((n_rows, D_MODEL), F32),
        compiler_params=_cparams(("arbitrary",), 56),
        name="experts",
    )(tile_expert, first, slot.astype(jnp.int32), next_expert, n_active, xs, w_gu,
      b_gu.reshape(depth, N_EXPERTS, 1, 2 * D_FF), w_down, b_down.reshape(depth, N_EXPERTS, 1, D_MODEL))


def _moe(x, norm_g, modseg, w_router, b_router, w_gu, b_gu, w_down, b_down, layer):
    n_tok = x.shape[0]
    h, slots, wts, table = _router(x, norm_g, modseg, w_router, b_router, layer)
    tm = EXPERT_TILE
    n_tiles = n_tok // TOKEN_TILE
    n_rows = n_tok * TOP_K + n_tiles * N_EXPERTS * ROW_BLOCK + N_EXPERTS * tm
    n_rows = -(-n_rows // tm) * tm
    local_start = table[:, 0, :N_EXPERTS]
    padded = table[:, 1, :N_EXPERTS]
    earlier = table[:, 2, :N_EXPERTS]
    used = earlier[-1] + padded[-1]
    size = ((used + tm - 1) // tm) * tm
    ends = _prefix_sum(size)
    starts = ends - size
    sorted_start = starts[None, :] + earlier
    meta = _copy_lists(local_start, padded, sorted_start)
    tails = jnp.concatenate([starts + used, (size - used) // ROW_BLOCK,
                             ends[-1:], (n_rows - ends[-1:]) // tm,
                             jnp.zeros((LANES - 2 * N_EXPERTS - 2,), jnp.int32)])
    tile_row0 = jnp.arange(n_rows // tm, dtype=jnp.int32) * tm
    tile_expert = jnp.minimum(jnp.sum((ends[None, :] <= tile_row0[:, None]).astype(jnp.int32), axis=1),
                              N_EXPERTS - 1)
    n_active = (ends[-1:] // tm).astype(jnp.int32)
    xs = _dispatch(h, slots, meta, tails, n_rows)
    ys = _experts(xs, tile_expert, ends, n_active, w_gu, b_gu, w_down, b_down, layer)
    return _combine(ys, slots, meta, wts, x, modseg)


def _rope_tables(n_tok):
    rows = n_tok // GRID_W
    row = jnp.repeat(jnp.arange(rows, dtype=F32), GRID_W)
    col = jnp.tile(jnp.arange(GRID_W, dtype=F32), rows)
    quarter = HEAD_DIM // 4
    inv_freq = ROPE_BASE ** (-jnp.arange(quarter, dtype=F32) / quarter)
    ang_r = row[:, None] * inv_freq
    ang_c = col[:, None] * inv_freq
    ang = jnp.concatenate([ang_r, ang_r, ang_c, ang_c], axis=-1)
    cos, sin = jnp.cos(ang), jnp.sin(ang)
    first = (jnp.arange(HEAD_DIM) % (2 * quarter)) < quarter
    sin_a = jnp.where(first, -sin, 0.0)
    sin_b = jnp.where(first, 0.0, sin)
    return tuple(jnp.tile(t, (1, LANES // HEAD_DIM)) for t in (cos, sin_a, sin_b))


def kernel(x_prompt, x_sample, c, cache_k, cache_v, state_rglru, state_hgrn, c_ctx, w_mod, b_mod, norm1_g, norm2_g, w_in, q_norm_g, k_norm_g, conv_w, conv_b, rg_wa, rg_ba, rg_wx, rg_bx, rg_lambda, hgrn_lb_logits, hgrn_norm_g, w_attn_o, w_rnn_o, w_hgrn_o, w_out, w_router, b_router, w_gu, b_gu, w_down, b_down):
    depth = w_mod.shape[0]
    n_p, t_p = x_prompt.shape[0], x_prompt.shape[1]
    n_s, t_s = x_sample.shape[0], x_sample.shape[1]
    tok_p = n_p * t_p
    tok_s = n_s * t_s
    assert tok_p % SEG_ROWS == 0 and t_s == SEG_ROWS and SEG_ROWS % t_p == 0

    x = jnp.concatenate([x_prompt.reshape(tok_p, D_MODEL), x_sample.reshape(tok_s, D_MODEL)], axis=0)

    lb_all = jnp.moveaxis(_prefix_sum(jnp.moveaxis(jax.nn.softmax(hgrn_lb_logits.astype(F32), axis=0), 0, -1)),
                          -1, 0)
    lb_all = lb_all - lb_all[:1]
    rope = _rope_tables(t_s)

    assert 1 + n_s <= ROW_BLOCK
    cond8 = jnp.zeros((ROW_BLOCK, D_MODEL), F32).at[0].set(c_ctx).at[1:1 + n_s].set(c)
    mod = _modulation(cond8, w_mod, b_mod)
    seg_ids = jnp.asarray([0] * (tok_p // SEG_ROWS) + [1 + b for b in range(n_s)], jnp.int32)
    modsegs = mod[:, seg_ids].reshape(depth, seg_ids.shape[0], N_MOD, D_MODEL)

    blocks = jnp.stack([rg_wa, rg_wx], axis=2)
    eye = jnp.eye(RG_BLOCKS, dtype=blocks.dtype)
    w_gate = jnp.einsum("ldgnab,nm->lnadgmb", blocks, eye).reshape(depth, D_RNN, 4 * D_RNN)
    b_gate = jnp.stack([rg_ba, rg_bx], axis=2).reshape(depth, 1, 4 * D_RNN)

    new_k, new_v, new_hr, new_s = [], [], [], []
    for l in range(depth):
        modseg = modsegs[l]
        z = _input_projection(x, norm1_g, modseg, w_in, l)
        attn_p, k_l, v_l = _attention(z, q_norm_g, k_norm_g, l, n_p, t_p, 0)
        (attn_s,) = _attention(z, q_norm_g, k_norm_g, l, n_s, t_s, tok_p, rope=rope,
                               cache=(cache_k, cache_v))
        rnn_p, hr_l = _rglru(z, conv_w, conv_b, w_gate, b_gate, rg_lambda, l, n_p, t_p, 0)
        (rnn_s,) = _rglru(z, conv_w, conv_b, w_gate, b_gate, rg_lambda, l, n_s, t_s, tok_p,
                          state=state_rglru)
        hg_p, s_l = _hgrn(z, lb_all, hgrn_norm_g, l, n_p, t_p, 0)
        (hg_s,) = _hgrn(z, lb_all, hgrn_norm_g, l, n_s, t_s, tok_p, state=state_hgrn)
        x = _merge_project(((attn_p, attn_s), (rnn_p, rnn_s), (hg_p, hg_s)), z, x, modseg,
                           w_attn_o, w_rnn_o, w_hgrn_o, w_out, l)
        x = _moe(x, norm2_g, modseg, w_router, b_router, w_gu, b_gu, w_down, b_down, l)
        new_k.append(k_l.reshape(n_p, t_p, N_KV_HEADS, HEAD_DIM))
        new_v.append(v_l.reshape(n_p, t_p, N_KV_HEADS, HEAD_DIM))
        new_hr.append(hr_l)
        new_s.append(s_l)

    y_prompt = x[:tok_p].reshape(n_p, t_p, D_MODEL)
    y_sample = x[tok_p:].reshape(n_s, t_s, D_MODEL)
    return (y_prompt, y_sample, jnp.stack(new_k, axis=1), jnp.stack(new_v, axis=1),
            jnp.stack(new_hr, axis=1), jnp.stack(new_s, axis=1))
```

```python
import functools

import jax
import jax.numpy as jnp
from jax import lax
from jax.experimental import pallas as pl
from jax.experimental.pallas import tpu as pltpu

F32 = jnp.float32
BF16 = jnp.bfloat16

D_MODEL = 1024
GRID_W = 64
HEAD_DIM = 64
N_HEADS = 8
N_KV_HEADS = 2
KV_GROUP = N_HEADS // N_KV_HEADS
ATTN_W = N_HEADS * HEAD_DIM
KV_W = N_KV_HEADS * HEAD_DIM
ROPE_BASE = 10000.0
D_RNN = 512
RG_BLOCKS = 8
RG_BW = D_RNN // RG_BLOCKS
RG_C = 8.0
HG_HEADS = 8
HG_DK = 64
HG_DV = 64
HG_W = HG_HEADS * HG_DK
N_PAIRS = HG_W // 128
N_EXPERTS = 32
TOP_K = 4
D_FF = D_MODEL
SWIGLU_LIMIT = 7.0
SWIGLU_ALPHA = 1.702
N_MOD = 6
EPS = 1e-6
IN_W = ATTN_W + 2 * KV_W + 2 * D_RNN + 5 * HG_W + 3 * D_MODEL

LANES = 128
SEG_ROWS = 1024
COL_TILE = 256
HG_CHUNK = 256
Q_TILE = 256
EXPERT_TILE = 384
BIG_BLOCK = 64
TOKEN_TILE = 512
ROW_BLOCK = 8
LOCAL_ROWS = TOKEN_TILE * TOP_K + N_EXPERTS * ROW_BLOCK
NEG_BIG = -1e30
LOG2_E = 1.4426950408889634

_SRC_TILES = tuple(range(17, 29)) + tuple(range(7, 17)) + tuple(range(3, 7)) + (0, 1, 2)
Z_GL = 0
Z_HG = 3 * D_MODEL
Z_RG = Z_HG + 5 * HG_W
Z_Q = Z_RG + 2 * D_RNN
Z_K = Z_Q + ATTN_W
Z_V = Z_K + KV_W
Z_W = -(-IN_W // (2 * COL_TILE)) * (2 * COL_TILE)


def _cparams(sem, vmem_mb):
    return pltpu.CompilerParams(dimension_semantics=sem, vmem_limit_bytes=vmem_mb * 1024 * 1024)


def _prefix_sum(x):
    n = x.shape[-1]
    upto = (jnp.arange(n)[None, :] <= jnp.arange(n)[:, None]).astype(x.dtype)
    return jnp.sum(x[..., None, :] * upto, axis=-1)


def _mm(a, b):
    return jnp.dot(a.astype(BF16), b.astype(BF16), preferred_element_type=F32)


def _mm_nt(a, b):
    return lax.dot_general(a.astype(BF16), b.astype(BF16), (((1,), (1,)), ((), ())),
                           preferred_element_type=F32)


def _mm_tn(a, b):
    return lax.dot_general(a.astype(BF16), b.astype(BF16), (((0,), (0,)), ((), ())),
                           preferred_element_type=F32)


def _sigmoid(x):
    return 1.0 / (1.0 + jnp.exp(-x))


def _softplus(x):
    return jnp.maximum(x, 0.0) + jnp.log1p(jnp.exp(-jnp.abs(x)))


def _modulated_norm(x, g, scale, shift):
    y = x * lax.rsqrt(jnp.mean(x * x, axis=-1, keepdims=True) + EPS)
    return (y * g) * (1.0 + scale) + shift


def _mod_kernel(c_ref, w_ref, b_ref, o_ref):
    c = c_ref[...]
    s = c * _sigmoid(c)
    o_ref[0] = _mm(s, w_ref[0]) + b_ref[0]


def _modulation(cond8, w_mod, b_mod):
    depth = w_mod.shape[0]
    tn = 1024
    return pl.pallas_call(
        _mod_kernel,
        grid=(depth, N_MOD * D_MODEL // tn),
        in_specs=[pl.BlockSpec((ROW_BLOCK, D_MODEL), lambda l, j: (0, 0)),
                  pl.BlockSpec((1, D_MODEL, tn), lambda l, j: (l, 0, j)),
                  pl.BlockSpec((1, 1, tn), lambda l, j: (l, 0, j))],
        out_specs=pl.BlockSpec((1, ROW_BLOCK, tn), lambda l, j: (l, 0, j)),
        out_shape=jax.ShapeDtypeStruct((depth, ROW_BLOCK, N_MOD * D_MODEL), F32),
        compiler_params=_cparams(("arbitrary", "arbitrary"), 32),
        name="modulation",
    )(cond8, w_mod, b_mod.reshape(depth, 1, N_MOD * D_MODEL))


def _inproj_kernel(src_ref, x_ref, g_ref, mod_ref, wa_ref, wb_ref, o_ref, h_ref, w_s):
    j = pl.program_id(1)

    @pl.when(j == 0)
    def _():
        for s in range(x_ref.shape[0] // SEG_ROWS):
            rows = slice(s * SEG_ROWS, (s + 1) * SEG_ROWS)
            h = _modulated_norm(x_ref[rows, :], g_ref[0], mod_ref[s, 1:2, :], mod_ref[s, 0:1, :])
            h_ref[rows, :] = h.astype(BF16)

    @pl.when(pl.program_id(0) == 0)
    def _():
        w_s[j, :, 0:COL_TILE] = wa_ref[0].astype(BF16)
        w_s[j, :, COL_TILE:2 * COL_TILE] = wb_ref[0].astype(BF16)

    o_ref[...] = jnp.dot(h_ref[...], w_s[j], preferred_element_type=F32)


def _input_projection(x, norm_g, modseg, w_in, layer):
    n_tok = x.shape[0]
    segs = 2
    tm = segs * SEG_ROWS
    n_col = Z_W // (2 * COL_TILE)
    src = jnp.asarray(_SRC_TILES + (_SRC_TILES[-1],) * (Z_W // COL_TILE - len(_SRC_TILES)), jnp.int32)

    def wspec(k):
        return pl.BlockSpec((1, D_MODEL, COL_TILE),
                            lambda i, j, s: (layer, 0, s[2 * jnp.where(i == 0, j, n_col - 1) + k]))

    grid_spec = pltpu.PrefetchScalarGridSpec(
        num_scalar_prefetch=1,
        grid=(n_tok // tm, n_col),
        in_specs=[pl.BlockSpec((tm, D_MODEL), lambda i, j, s: (i, 0)),
                  pl.BlockSpec((1, 1, D_MODEL), lambda i, j, s: (layer, 0, 0)),
                  pl.BlockSpec((segs, N_MOD, D_MODEL), lambda i, j, s: (i, 0, 0)),
                  wspec(0), wspec(1)],
        out_specs=pl.BlockSpec((tm, 2 * COL_TILE), lambda i, j, s: (i, j)),
        scratch_shapes=[pltpu.VMEM((tm, D_MODEL), BF16),
                        pltpu.VMEM((n_col, D_MODEL, 2 * COL_TILE), BF16)],
    )
    return pl.pallas_call(
        _inproj_kernel,
        grid_spec=grid_spec,
        out_shape=jax.ShapeDtypeStruct((n_tok, Z_W), F32),
        compiler_params=_cparams(("arbitrary", "arbitrary"), 56),
        name="input_projection",
    )(src, x, norm_g.reshape(-1, 1, D_MODEL), modseg, w_in, w_in)


def _head_rms(x, g):
    lane = lax.broadcasted_iota(jnp.int32, x.shape, 1)
    lo = lane < HEAD_DIM
    xx = x * x
    s_lo = jnp.sum(jnp.where(lo, xx, 0.0), axis=-1, keepdims=True)
    s_hi = jnp.sum(jnp.where(lo, 0.0, xx), axis=-1, keepdims=True)
    inv = jnp.where(lo, lax.rsqrt(s_lo * (1.0 / HEAD_DIM) + EPS),
                    lax.rsqrt(s_hi * (1.0 / HEAD_DIM) + EPS))
    return x * inv * g


def _rope(x, cos, sin_a, sin_b):
    q = HEAD_DIM // 4
    return x * cos + pltpu.roll(x, LANES - q, 1) * sin_a + pltpu.roll(x, q, 1) * sin_b


def _attn_kernel(*refs, t_len, use_ctx):
    if use_ctx:
        (q_ref, k_ref, v_ref, qg_ref, kg_ref, cos_q_ref, sa_q_ref, sb_q_ref,
         cos_k_ref, sa_k_ref, sb_k_ref, ck_ref, cv_ref, o_ref, kall_ref, vall_ref) = refs
    else:
        (q_ref, k_ref, v_ref, qg_ref, kg_ref, o_ref, ko_ref, vo_ref, kall_ref, vall_ref) = refs

    @pl.when(pl.program_id(1) == 0)
    def _():
        k = _head_rms(k_ref[...], kg_ref[...])
        if use_ctx:
            k = _rope(k, cos_k_ref[...], sa_k_ref[...], sb_k_ref[...])
            kall_ref[0:t_len, :] = k.astype(BF16)
            vall_ref[0:t_len, :] = v_ref[...].astype(BF16)
            kall_ref[t_len:, :] = ck_ref[0, 0].astype(BF16)
            vall_ref[t_len:, :] = cv_ref[0, 0].astype(BF16)
        else:
            ko_ref[0] = k
            vo_ref[0] = v_ref[...]
            kall_ref[...] = k.astype(BF16)
            vall_ref[...] = v_ref[...].astype(BF16)

    heads = []
    for c in range(ATTN_W // LANES):
        qc = _head_rms(q_ref[:, c * LANES:(c + 1) * LANES], qg_ref[...])
        if use_ctx:
            qc = _rope(qc, cos_q_ref[...], sa_q_ref[...], sb_q_ref[...])
        qc = (qc * (HEAD_DIM ** -0.5)).astype(BF16)
        heads.append(qc[:, :HEAD_DIM])
        heads.append(qc[:, HEAD_DIM:])
    tq = q_ref.shape[0]
    for g in range(N_KV_HEADS):
        qs = jnp.concatenate(heads[g * KV_GROUP:(g + 1) * KV_GROUP], axis=0)
        kh = kall_ref[:, g * HEAD_DIM:(g + 1) * HEAD_DIM]
        vh = vall_ref[:, g * HEAD_DIM:(g + 1) * HEAD_DIM]
        s = _mm_nt(qs, kh)
        m = jnp.max(s, axis=-1, keepdims=True)
        p = jnp.exp(s - m)
        den = jnp.sum(p, axis=-1, keepdims=True)
        o = _mm(p, vh) / den
        for j in range(KV_GROUP):
            hh = g * KV_GROUP + j
            o_ref[:, hh * HEAD_DIM:(hh + 1) * HEAD_DIM] = o[j * tq:(j + 1) * tq].astype(BF16)


def _attention(z, q_g, k_g, layer, n_batch, t_len, row0, rope=None, cache=None):
    use_ctx = cache is not None
    nq = t_len // Q_TILE
    rb0 = row0 // t_len
    qb0 = row0 // Q_TILE
    qg = jnp.tile(q_g[layer], 2).reshape(1, LANES)
    kg = jnp.tile(k_g[layer], 2).reshape(1, LANES)
    n_out_tok = n_batch * t_len
    vec = pl.BlockSpec((1, LANES), lambda b, i: (0, 0))
    in_specs = [pl.BlockSpec((Q_TILE, ATTN_W), lambda b, i: (qb0 + b * nq + i, Z_Q // ATTN_W)),
                pl.BlockSpec((t_len, KV_W), lambda b, i: (rb0 + b, Z_K // KV_W)),
                pl.BlockSpec((t_len, KV_W), lambda b, i: (rb0 + b, Z_V // KV_W)),
                vec, vec]
    args = [z, z, z, qg, kg]
    out_specs = [pl.BlockSpec((Q_TILE, ATTN_W), lambda b, i: (b * nq + i, 0))]
    out_shape = [jax.ShapeDtypeStruct((n_out_tok, ATTN_W), BF16)]
    t_keys = t_len
    if use_ctx:
        cos, sin_a, sin_b = rope
        cache_k, cache_v = cache
        past = cache_k.shape[2]
        t_keys = t_len + past
        tq_spec = pl.BlockSpec((Q_TILE, LANES), lambda b, i: (i, 0))
        tk_spec = pl.BlockSpec((t_len, LANES), lambda b, i: (0, 0))
        c_spec = pl.BlockSpec((1, 1, past, KV_W), lambda b, i: (b, layer, 0, 0))
        in_specs += [tq_spec, tq_spec, tq_spec, tk_spec, tk_spec, tk_spec, c_spec, c_spec]
        args += [cos, sin_a, sin_b, cos, sin_a, sin_b,
                 cache_k.reshape(cache_k.shape[0], cache_k.shape[1], past, KV_W),
                 cache_v.reshape(cache_v.shape[0], cache_v.shape[1], past, KV_W)]
    else:
        kv_spec = pl.BlockSpec((1, t_len, KV_W), lambda b, i: (b, 0, 0))
        out_specs += [kv_spec, kv_spec]
        out_shape += [jax.ShapeDtypeStruct((n_batch, t_len, KV_W), F32)] * 2
    return pl.pallas_call(
        functools.partial(_attn_kernel, t_len=t_len, use_ctx=use_ctx),
        grid=(n_batch, nq),
        in_specs=in_specs,
        out_specs=out_specs,
        out_shape=out_shape,
        scratch_shapes=[pltpu.VMEM((t_keys, KV_W), BF16), pltpu.VMEM((t_keys, KV_W), BF16)],
        compiler_params=_cparams(("arbitrary", "arbitrary"), 48),
        name="attention_ctx" if use_ctx else "attention",
    )(*args)


def _shift_rows(x, d):
    n = x.shape[0]
    row = lax.broadcasted_iota(jnp.int32, x.shape, 0)
    y = pltpu.roll(x, d % n, 0)
    if d > 0:
        return jnp.where(row >= d, y, 0.0)
    return jnp.where(row < n + d, y, 0.0)


def _linear_scan(a, u, reverse):
    n, w = a.shape
    row = lax.broadcasted_iota(jnp.int32, a.shape, 0)

    def shifted(x, d, fill):
        if d % ROW_BLOCK == 0:
            pad = jnp.full((d, w), fill, x.dtype)
            return jnp.concatenate([x[d:], pad] if reverse else [pad, x[:n - d]], axis=0)
        if reverse:
            return jnp.where(row < n - d, pltpu.roll(x, n - d, 0), fill)
        return jnp.where(row >= d, pltpu.roll(x, d, 0), fill)

    d = 1
    while d < n:
        u = a * shifted(u, d, 0.0) + u
        a = a * shifted(a, d, 1.0)
        d *= 2
    return a, u


def _rglru_kernel(*refs, use_ctx):
    if use_ctx:
        (x_ref, y_ref, cw_ref, cb_ref, wg_ref, bg_ref, lam_ref, h0_ref, o_ref) = refs
    else:
        (x_ref, y_ref, cw_ref, cb_ref, wg_ref, bg_ref, lam_ref, o_ref, last_ref) = refs
    x = x_ref[...]
    cw = cw_ref[0]
    xr = cb_ref[0] + _shift_rows(x, 1) * cw[0:1] + x * cw[1:2] \
        + _shift_rows(x, -1) * cw[2:3] + _shift_rows(x, -2) * cw[3:4]
    gates = _sigmoid(_mm(xr, wg_ref[0]) + bg_ref[0])
    n = x.shape[0]
    total = None
    lasts = []
    for d in range(2):
        r = gates[:, (2 * d) * D_RNN:(2 * d + 1) * D_RNN]
        i = gates[:, (2 * d + 1) * D_RNN:(2 * d + 2) * D_RNN]
        log_a = (-RG_C * _softplus(-lam_ref[0, d:d + 1, :])) * r
        a = jnp.exp(log_a)
        u = jnp.sqrt(1.0 - a * a) * (i * xr)
        a_cum, h = _linear_scan(a, u, reverse=(d == 1))
        if use_ctx:
            h = h + a_cum * h0_ref[0, 0, d:d + 1, :]
        else:
            lasts.append(h[0:1] if d == 1 else h[n - 1:n])
        total = h if total is None else total + h
    o_ref[...] = (jax.nn.gelu(y_ref[...]) * total).astype(BF16)
    if not use_ctx:
        last_ref[0] = jnp.concatenate(lasts, axis=0)


def _rglru(z, conv_w, conv_b, w_gate, b_gate, lam, layer, n_batch, t_len, row0, state=None):
    use_ctx = state is not None
    rb0 = row0 // t_len
    in_specs = [pl.BlockSpec((t_len, D_RNN), lambda b: (rb0 + b, Z_RG // D_RNN)),
                pl.BlockSpec((t_len, D_RNN), lambda b: (rb0 + b, Z_RG // D_RNN + 1)),
                pl.BlockSpec((1, 4, D_RNN), lambda b: (layer, 0, 0)),
                pl.BlockSpec((1, 1, D_RNN), lambda b: (layer, 0, 0)),
                pl.BlockSpec((1, D_RNN, 4 * D_RNN), lambda b: (layer, 0, 0)),
                pl.BlockSpec((1, 1, 4 * D_RNN), lambda b: (layer, 0, 0)),
                pl.BlockSpec((1, 2, D_RNN), lambda b: (layer, 0, 0))]
    args = [z, z, conv_w, conv_b.reshape(-1, 1, D_RNN), w_gate, b_gate, lam]
    out_specs = [pl.BlockSpec((t_len, D_RNN), lambda b: (b, 0))]
    out_shape = [jax.ShapeDtypeStruct((n_batch * t_len, D_RNN), BF16)]
    if use_ctx:
        in_specs.append(pl.BlockSpec((1, 1, 2, D_RNN), lambda b: (b, layer, 0, 0)))
        args.append(state)
    else:
        out_specs.append(pl.BlockSpec((1, 2, D_RNN), lambda b: (b, 0, 0)))
        out_shape.append(jax.ShapeDtypeStruct((n_batch, 2, D_RNN), F32))
    return pl.pallas_call(
        functools.partial(_rglru_kernel, use_ctx=use_ctx),
        grid=(n_batch,),
        in_specs=in_specs,
        out_specs=out_specs,
        out_shape=out_shape,
        compiler_params=_cparams(("arbitrary",), 56),
        name="rglru_ctx" if use_ctx else "rglru",
    )(*args)


def _cumsum_rows(g, reverse):
    n, w = g.shape
    m = n // ROW_BLOCK

    def scan(x, size, pos):
        d = 1
        while d < size:
            if reverse:
                x = x + jnp.where(pos < size - d, pltpu.roll(x, x.shape[0] - d, 0), 0.0)
            else:
                x = x + jnp.where(pos >= d, pltpu.roll(x, d, 0), 0.0)
            d *= 2
        return x

    g = scan(g, ROW_BLOCK, lax.broadcasted_iota(jnp.int32, g.shape, 0) & (ROW_BLOCK - 1))
    g3 = g.reshape(m, ROW_BLOCK, w)
    total = g3[:, 0, :] if reverse else g3[:, ROW_BLOCK - 1, :]
    before = scan(total, m, lax.broadcasted_iota(jnp.int32, total.shape, 0)) - total
    return (g3 + before[:, None, :]).reshape(n, w)


def _block_row_fn(gc, reverse):
    n, w = gc.shape
    g3 = gc.reshape(n // ROW_BLOCK, ROW_BLOCK, w)
    edge_row = 0 if reverse else ROW_BLOCK - 1
    edge = jnp.broadcast_to(g3[:, edge_row:edge_row + 1, :], g3.shape)
    sub = lax.broadcasted_iota(jnp.int32, g3.shape, 1)

    def at(blk):
        r = blk // 2 if reverse else blk // 2 - 1
        if blk > ROW_BLOCK:
            groups = blk // ROW_BLOCK
            e4 = edge.reshape(n // blk, groups, ROW_BLOCK, w)
            pick = r // ROW_BLOCK
            return jnp.broadcast_to(e4[:, pick:pick + 1], e4.shape).reshape(n, w)
        out = jnp.broadcast_to(g3[:, r:r + 1, :], g3.shape)
        for b0 in range(blk, ROW_BLOCK, blk):
            out = jnp.where(sub >= b0, jnp.broadcast_to(g3[:, b0 + r:b0 + r + 1, :], g3.shape), out)
        return out.reshape(n, w)

    return at


def _first_head(shape):
    lane = lax.broadcasted_iota(jnp.int32, shape, len(shape) - 1)
    return (lane & (LANES - 1)) < HG_DK


def _pair(x, p):
    return x[:, p * LANES:(p + 1) * LANES]


def _hgrn_gates(f_logit, lb):
    y = jnp.exp(-jnp.abs(f_logit))
    log_sig = jnp.minimum(f_logit, 0.0) - jnp.log(1.0 + y)
    a = jnp.log(lb)
    b = jnp.log1p(-lb) + log_sig
    log_f = jnp.maximum(a, b) + jnp.log(1.0 + jnp.exp(-jnp.abs(a - b)))
    k = (1.0 - lb) * (jnp.where(f_logit >= 0.0, y, 1.0) / (1.0 + y))
    return log_f * LOG2_E, k


def _hgrn_chunk_state(gc, k, v, reverse):
    n = gc.shape[0]
    g_tot = gc[0:1] if reverse else gc[n - 1:n]
    k_end = (k * jnp.exp2(jnp.minimum(g_tot - gc, 0.0))).astype(BF16)
    vb = v.astype(BF16)
    same_head = (_first_head((LANES, LANES))
                 == (lax.broadcasted_iota(jnp.int32, (LANES, LANES), 0) < HG_DK))
    ds = [jnp.where(same_head, _mm_tn(_pair(k_end, p), _pair(vb, p)), 0.0) for p in range(N_PAIRS)]
    return ds, jnp.exp2(g_tot)


def _decay_state(s, e_row):
    n = s.shape[0]
    eye = (lax.broadcasted_iota(jnp.int32, (n, n), 0) == lax.broadcasted_iota(jnp.int32, (n, n), 1))
    col = jnp.sum(jnp.where(eye, jnp.broadcast_to(e_row, (n, n)), 0.0), axis=1, keepdims=True)
    return s * col


def _hgrn_intra_scores(q, ks, gcs, txs):
    n = q.shape[0]
    row = lax.broadcasted_iota(jnp.int32, q.shape, 0)
    first = _first_head(q.shape)
    boundary = [_block_row_fn(gcs[0], False), _block_row_fn(gcs[1], True)]
    scores = [None] * HG_HEADS
    w = 1
    while w < n:
        blk = 2 * w
        second = (row & w) != 0
        e_f = jnp.exp2(-jnp.abs(gcs[0] - boundary[0](blk)))
        e_b = jnp.exp2(-jnp.abs(gcs[1] - boundary[1](blk)))
        q_f = jnp.where(second, q * e_f, 0.0).astype(BF16)
        q_b = jnp.where(second, 0.0, q * e_b).astype(BF16)
        k_f = jnp.where(second, 0.0, ks[0] * e_f)
        k_b = jnp.where(second, ks[1] * e_b, 0.0)
        k_heads = [(jnp.where(first, k_f, 0.0).astype(BF16), jnp.where(first, k_b, 0.0).astype(BF16)),
                   (jnp.where(first, 0.0, k_f).astype(BF16), jnp.where(first, 0.0, k_b).astype(BF16))]
        sel = (txs >= w) & (txs < blk)
        for p in range(N_PAIRS):
            qc = jnp.concatenate([_pair(q_f, p), _pair(q_b, p)], axis=1)
            for j in range(2):
                kc = jnp.concatenate([_pair(k_heads[j][0], p), _pair(k_heads[j][1], p)], axis=1)
                s = jnp.where(sel, _mm_nt(qc, kc), 0.0)
                h = 2 * p + j
                scores[h] = s if scores[h] is None else scores[h] + s
        w = blk
    return scores


def _hgrn_kernel(*refs, t_len, use_ctx):
    if use_ctx:
        (q_ref, ff_ref, fb_ref, i_ref, og_ref, lb_ref, ng_ref, s0_ref, o_ref, sin_ref, gk_ref) = refs
    else:
        (q_ref, ff_ref, fb_ref, i_ref, og_ref, lb_ref, ng_ref, o_ref, sl_ref) = refs
    c = HG_CHUNK
    n_chunks = t_len // c
    f_refs = (ff_ref, fb_ref)

    def load_dir(d, r0):
        log_f, k = _hgrn_gates(f_refs[d][pl.ds(r0, c), :], lb_ref[0, d:d + 1, :])
        return _cumsum_rows(log_f, reverse=(d == 1)), k

    if use_ctx:
        zero = jnp.zeros((HG_DK, HG_DV), F32)
        for d in range(2):
            states = []
            for p in range(N_PAIRS):
                top = jnp.concatenate([s0_ref[0, 0, d, 2 * p], zero], axis=1)
                bot = jnp.concatenate([zero, s0_ref[0, 0, d, 2 * p + 1]], axis=1)
                states.append(jnp.concatenate([top, bot], axis=0))
            order = range(n_chunks) if d == 0 else range(n_chunks - 1, -1, -1)
            for ci in order:
                for p in range(N_PAIRS):
                    sin_ref[ci, p, d * LANES:(d + 1) * LANES, :] = states[p]
                gc, k = load_dir(d, ci * c)
                gk_ref[d, 0, pl.ds(ci * c, c), :] = gc
                gk_ref[d, 1, pl.ds(ci * c, c), :] = k
                ds, e_tot = _hgrn_chunk_state(gc, k, i_ref[pl.ds(ci * c, c), :], reverse=(d == 1))
                states = [_decay_state(states[p], _pair(e_tot, p)) + ds[p] for p in range(N_PAIRS)]

    ti = lax.broadcasted_iota(jnp.int32, (c, c), 0)
    si = lax.broadcasted_iota(jnp.int32, (c, c), 1)
    txs = ti ^ si
    diag = ti == si

    def chunk_body(ci, carry):
        r0 = pl.multiple_of(ci * c, c)
        hq = q_ref[pl.ds(r0, c), :]
        q = hq * _sigmoid(hq)
        v = i_ref[pl.ds(r0, c), :]
        gcs, ks = [], []
        for d in range(2):
            if use_ctx:
                gc, k = gk_ref[d, 0, pl.ds(r0, c), :], gk_ref[d, 1, pl.ds(r0, c), :]
            else:
                gc, k = load_dir(d, r0)
            gcs.append(gc)
            ks.append(k)
        scores = _hgrn_intra_scores(q, ks, gcs, txs)
        first = _first_head(q.shape)
        k_sum = ks[0] + ks[1]
        v_heads = (jnp.where(first, v, 0.0).astype(BF16), jnp.where(first, 0.0, v).astype(BF16))
        k_heads = (jnp.where(first, k_sum, 0.0).astype(BF16), jnp.where(first, 0.0, k_sum).astype(BF16))
        q_bf = q.astype(BF16)
        if use_ctx:
            qe = [(q * jnp.exp2(gcs[d])).astype(BF16) for d in range(2)]
        else:
            for d in range(2):
                ds, _ = _hgrn_chunk_state(gcs[d], ks[d], v, reverse=(d == 1))
                for p in range(N_PAIRS):
                    sl_ref[0, d, 2 * p] = ds[p][:HG_DK, :HG_DV]
                    sl_ref[0, d, 2 * p + 1] = ds[p][HG_DK:, HG_DV:]
        outs = []
        for p in range(N_PAIRS):
            o = None
            for j in range(2):
                s_diag = jnp.where(diag, _mm_nt(_pair(q_bf, p), _pair(k_heads[j], p)), 0.0)
                t = _mm(scores[2 * p + j] + s_diag, _pair(v_heads[j], p))
                o = t if o is None else o + t
            if use_ctx:
                o = o + _mm(jnp.concatenate([_pair(qe[0], p), _pair(qe[1], p)], axis=1), sin_ref[ci, p])
            outs.append(_head_rms(o, ng_ref[0]))
        og = og_ref[pl.ds(r0, c), :]
        o_all = jnp.concatenate(outs, axis=1) * (og * _sigmoid(og))
        o_ref[pl.ds(r0, c), :] = o_all.astype(BF16)
        return carry

    if n_chunks == 1:
        chunk_body(0, 0)
    else:
        lax.fori_loop(0, n_chunks, chunk_body, 0)


def _hgrn(z, lb, norm_g, layer, n_batch, t_len, row0, state=None):
    use_ctx = state is not None
    rb0 = row0 // t_len
    cb = Z_HG // HG_W

    def zspec(k):
        return pl.BlockSpec((t_len, HG_W), lambda b: (rb0 + b, cb + k))

    in_specs = [zspec(0), zspec(1), zspec(2), zspec(3), zspec(4),
                pl.BlockSpec((1, 2, HG_W), lambda b: (layer, 0, 0)),
                pl.BlockSpec((1, 1, LANES), lambda b: (layer, 0, 0))]
    args = [z, z, z, z, z, lb, jnp.tile(norm_g, (1, LANES // HG_DV)).reshape(-1, 1, LANES)]
    out_specs = [pl.BlockSpec((t_len, HG_W), lambda b: (b, 0))]
    out_shape = [jax.ShapeDtypeStruct((n_batch * t_len, HG_W), BF16)]
    scratch = []
    if use_ctx:
        in_specs.append(pl.BlockSpec((1, 1, 2, HG_HEADS, HG_DK, HG_DV), lambda b: (b, layer, 0, 0, 0, 0)))
        args.append(state)
        scratch.append(pltpu.VMEM((t_len // HG_CHUNK, N_PAIRS, 2 * LANES, LANES), F32))
        scratch.append(pltpu.VMEM((2, 2, t_len, HG_W), F32))
    else:
        out_specs.append(pl.BlockSpec((1, 2, HG_HEADS, HG_DK, HG_DV), lambda b: (b, 0, 0, 0, 0)))
        out_shape.append(jax.ShapeDtypeStruct((n_batch, 2, HG_HEADS, HG_DK, HG_DV), F32))
    return pl.pallas_call(
        functools.partial(_hgrn_kernel, t_len=t_len, use_ctx=use_ctx),
        grid=(n_batch,),
        in_specs=in_specs,
        out_specs=out_specs,
        out_shape=out_shape,
        scratch_shapes=scratch,
        compiler_params=_cparams(("arbitrary",), 56),
        name="hgrn_ctx" if use_ctx else "hgrn",
    )(*args)


def _merge_kernel(ap_ref, as_ref, rp_ref, rs_ref, hp_ref, hs_ref, g0_ref, g1_ref, g2_ref, x_ref, mod_ref,
                  wa_ref, wr_ref, wh_ref, wo_ref, o_ref, wa_s, wr_s, wh_s, wo_s, *, prompt_tiles):
    @pl.when(pl.program_id(0) == 0)
    def _():
        wa_s[...] = wa_ref[0].astype(BF16)
        wr_s[...] = wr_ref[0].astype(BF16)
        wh_s[...] = wh_ref[0].astype(BF16)
        wo_s[...] = wo_ref[0].astype(BF16)

    is_prompt = pl.program_id(0) < prompt_tiles

    def proj(p_ref, s_ref, w_s):
        return jnp.dot(jnp.where(is_prompt, p_ref[...], s_ref[...]), w_s[...], preferred_element_type=F32)

    merged = (_sigmoid(g0_ref[...]) * proj(ap_ref, as_ref, wa_s)
              + _sigmoid(g1_ref[...]) * proj(rp_ref, rs_ref, wr_s)
              + _sigmoid(g2_ref[...]) * proj(hp_ref, hs_ref, wh_s))
    out = jnp.dot(merged.astype(BF16), wo_s[...], preferred_element_type=F32)
    o_ref[...] = x_ref[...] + mod_ref[0, 2:3, :] * out


def _merge_project(mixed, z, x, modseg, w_attn_o, w_rnn_o, w_hgrn_o, w_out, layer):
    n_tok = x.shape[0]
    tm = 512
    per_seg = SEG_ROWS // tm
    prompt_tiles = mixed[0][0].shape[0] // tm
    half_p = pl.BlockSpec((tm, ATTN_W), lambda i: (jnp.minimum(i, prompt_tiles - 1), 0))
    half_s = pl.BlockSpec((tm, ATTN_W), lambda i: (jnp.maximum(i - prompt_tiles, 0), 0))
    full = pl.BlockSpec((tm, D_MODEL), lambda i: (i, 0))

    def gspec(k):
        return pl.BlockSpec((tm, D_MODEL), lambda i: (i, Z_GL // D_MODEL + k))

    def wspec(rows):
        return pl.BlockSpec((1, rows, D_MODEL), lambda i: (layer, 0, 0))

    return pl.pallas_call(
        functools.partial(_merge_kernel, prompt_tiles=prompt_tiles),
        grid=(n_tok // tm,),
        in_specs=[half_p, half_s, half_p, half_s, half_p, half_s, gspec(0), gspec(1), gspec(2), full,
                  pl.BlockSpec((1, N_MOD, D_MODEL), lambda i: (i // per_seg, 0, 0)),
                  wspec(ATTN_W), wspec(D_RNN), wspec(HG_W), wspec(D_MODEL)],
        out_specs=full,
        out_shape=jax.ShapeDtypeStruct((n_tok, D_MODEL), F32),
        scratch_shapes=[pltpu.VMEM((ATTN_W, D_MODEL), BF16), pltpu.VMEM((D_RNN, D_MODEL), BF16),
                        pltpu.VMEM((HG_W, D_MODEL), BF16), pltpu.VMEM((D_MODEL, D_MODEL), BF16)],
        compiler_params=_cparams(("arbitrary",), 56),
        name="merge_project",
    )(*[part for pair in mixed for part in pair], z, z, z, x, modseg, w_attn_o, w_rnn_o, w_hgrn_o, w_out)


def _router_kernel(x_ref, g_ref, mod_ref, wr_ref, br_ref, h_ref, slot_ref, wt_ref, meta_ref, run_ref):
    @pl.when(pl.program_id(0) == 0)
    def _():
        run_ref[...] = jnp.zeros_like(run_ref)

    h = _modulated_norm(x_ref[...], g_ref[0], mod_ref[0, 4:5, :], mod_ref[0, 3:4, :])
    h_ref[...] = h
    h_hi = h.astype(BF16)
    h_lo = (h - h_hi.astype(F32)).astype(BF16)
    w = wr_ref[0]
    w_hi = w.astype(BF16)
    w_lo = (w - w_hi.astype(F32)).astype(BF16)
    logits = (jnp.dot(h_hi, w_hi, preferred_element_type=F32) + jnp.dot(h_hi, w_lo, preferred_element_type=F32)
              + jnp.dot(h_lo, w_hi, preferred_element_type=F32)) + br_ref[0]
    tm = logits.shape[0]
    lane = lax.broadcasted_iota(jnp.int32, logits.shape, 1)
    ids, vals = [], []
    chosen = jnp.zeros(logits.shape, jnp.bool_)
    work = logits
    for _ in range(TOP_K):
        m = jnp.max(work, axis=-1, keepdims=True)
        idx = jnp.min(jnp.where(work == m, lane, LANES), axis=-1, keepdims=True)
        hit = lane == idx
        chosen = chosen | hit
        work = jnp.where(hit, -jnp.inf, work)
        ids.append(idx)
        vals.append(m)
    exps = [jnp.exp(v - vals[0]) for v in vals]
    den = exps[0] + exps[1] + exps[2] + exps[3]
    ind = jnp.where(chosen, 1.0, 0.0)
    r_i = lax.broadcasted_iota(jnp.int32, (tm, tm), 0)
    c_i = lax.broadcasted_iota(jnp.int32, (tm, tm), 1)
    count = jnp.sum(ind, axis=0, keepdims=True)
    padded = jnp.floor((count + (ROW_BLOCK - 1)) * (1.0 / ROW_BLOCK)) * ROW_BLOCK
    incl = jnp.broadcast_to(padded, (ROW_BLOCK, LANES))
    lane8 = lax.broadcasted_iota(jnp.int32, (ROW_BLOCK, LANES), 1)
    d = 1
    while d < LANES:
        incl = incl + jnp.where(lane8 >= d, pltpu.roll(incl, d, 1), 0.0)
        d *= 2
    local_start = incl[0:1] - padded
    place = _mm(jnp.where(c_i < r_i, 1.0, 0.0), ind) + local_start
    slot_out = jnp.zeros(logits.shape, jnp.int32)
    wt_out = jnp.zeros(logits.shape, F32)
    for k in range(TOP_K):
        slot = jnp.sum(jnp.where(lane == ids[k], place, 0.0), axis=-1, keepdims=True)
        slot_out = jnp.where(lane == k, slot.astype(jnp.int32), slot_out)
        wt_out = jnp.where(lane == k, exps[k] / den, wt_out)
    slot_ref[...] = slot_out
    wt_ref[...] = wt_out
    row8 = lax.broadcasted_iota(jnp.int32, (ROW_BLOCK, LANES), 0)
    table = jnp.where(row8 == 0, local_start, jnp.where(row8 == 1, padded, run_ref[0:1, :]))
    meta_ref[0] = jnp.where(row8 < 3, table, 0.0).astype(jnp.int32)
    run_ref[...] = run_ref[...] + padded


def _router(x, norm_g, modseg, w_router, b_router, layer):
    n_tok = x.shape[0]
    tm = TOKEN_TILE
    per_seg = SEG_ROWS // tm
    depth = w_router.shape[0]
    w_pad = jnp.zeros((depth, D_MODEL, LANES), F32).at[:, :, :N_EXPERTS].set(w_router)
    b_pad = jnp.full((depth, 1, LANES), NEG_BIG, F32).at[:, 0, :N_EXPERTS].set(b_router)
    full = pl.BlockSpec((tm, D_MODEL), lambda i: (i, 0))
    small = pl.BlockSpec((tm, LANES), lambda i: (i, 0))
    return pl.pallas_call(
        _router_kernel,
        grid=(n_tok // tm,),
        in_specs=[full,
                  pl.BlockSpec((1, 1, D_MODEL), lambda i: (layer, 0, 0)),
                  pl.BlockSpec((1, N_MOD, D_MODEL), lambda i: (i // per_seg, 0, 0)),
                  pl.BlockSpec((1, D_MODEL, LANES), lambda i: (layer, 0, 0)),
                  pl.BlockSpec((1, 1, LANES), lambda i: (layer, 0, 0))],
        out_specs=[full, small, small, pl.BlockSpec((1, ROW_BLOCK, LANES), lambda i: (i, 0, 0))],
        out_shape=[jax.ShapeDtypeStruct((n_tok, D_MODEL), F32),
                   jax.ShapeDtypeStruct((n_tok, LANES), jnp.int32),
                   jax.ShapeDtypeStruct((n_tok, LANES), F32),
                   jax.ShapeDtypeStruct((n_tok // tm, ROW_BLOCK, LANES), jnp.int32)],
        scratch_shapes=[pltpu.VMEM((ROW_BLOCK, LANES), F32)],
        compiler_params=_cparams(("arbitrary",), 40),
        name="router",
    )(x, norm_g.reshape(-1, 1, D_MODEL), modseg, w_pad, b_pad)


COPY_BLOCKS = (BIG_BLOCK, 2 * ROW_BLOCK, ROW_BLOCK)
COPY_CAPS = (LOCAL_ROWS // COPY_BLOCKS[0],) + tuple(
    N_EXPERTS * (COPY_BLOCKS[c - 1] // COPY_BLOCKS[c] - 1) for c in range(1, len(COPY_BLOCKS)))
COPY_LIST_LEN = 1024
_COPY_AT = tuple(len(COPY_BLOCKS) + 2 * sum(COPY_CAPS[:c]) for c in range(len(COPY_BLOCKS)))
assert _COPY_AT[-1] + 2 * COPY_CAPS[-1] <= COPY_LIST_LEN


def _copy_lists(local_start, padded, sorted_start):

    def expand(count, local0, sorted0, step, cap):
        last = _prefix_sum(count)[:, None, :]
        first = last - count[:, None, :]
        f = jnp.arange(cap, dtype=jnp.int32)[None, :, None]
        mine = ((first <= f) & (f < last)).astype(jnp.int32)
        offset = step * (f - first)
        return (last[:, 0, -1:], jnp.sum(mine * (local0[:, None, :] + offset), axis=2),
                jnp.sum(mine * (sorted0[:, None, :] + offset), axis=2))

    counts, lists = [], []
    done = jnp.zeros_like(padded)
    for rows, cap in zip(COPY_BLOCKS, COPY_CAPS):
        n_copies = (padded - done) // rows
        cnt, loc, srt = expand(n_copies, local_start + done, sorted_start + done, rows, cap)
        counts.append(cnt)
        lists += [loc, srt]
        done = done + n_copies * rows
    parts = counts + lists
    used = sum(p.shape[1] for p in parts)
    parts.append(jnp.zeros((padded.shape[0], COPY_LIST_LEN - used), jnp.int32))
    return jnp.concatenate(parts, axis=1).astype(jnp.int32).reshape(-1)


def _for_each_block(list_ref, fn):
    for c, (rows, cap) in enumerate(zip(COPY_BLOCKS, COPY_CAPS)):
        def copy(j, carry, rows=rows, cap=cap, at=_COPY_AT[c]):
            fn(pl.multiple_of(list_ref[at + j], ROW_BLOCK),
               pl.multiple_of(list_ref[at + cap + j], ROW_BLOCK), rows)
            return carry

        lax.fori_loop(0, list_ref[c], copy, 0)


def _slot_matrix(slot_ref, values):
    col = lax.broadcasted_iota(jnp.int32, (TOKEN_TILE, LOCAL_ROWS), 1)
    out = jnp.zeros((TOKEN_TILE, LOCAL_ROWS), F32)
    for k in range(TOP_K):
        out = jnp.where(col == slot_ref[:, k:k + 1], values[k], out)
    return out.astype(BF16)


def _dispatch_kernel(meta_ref, prev_meta_ref, tail_ref, slot_ref, h_ref, xs_ref, local_ref, zero_ref,
                     sem, block_sem):
    i = pl.program_id(0)
    cur = i % 2

    @pl.when(i == 0)
    def _():
        zero_ref[...] = jnp.zeros_like(zero_ref)

        def tails(fn):
            def group(e, carry):
                def block(j, c):
                    fn(pl.multiple_of(tail_ref[e] + j * ROW_BLOCK, ROW_BLOCK))
                    return c
                return lax.fori_loop(0, tail_ref[N_EXPERTS + e], block, carry)
            lax.fori_loop(0, N_EXPERTS, group, 0)

        def zero_copy(row):
            return pltpu.make_async_copy(zero_ref.at[pl.ds(0, ROW_BLOCK)],
                                         xs_ref.at[pl.ds(row, ROW_BLOCK)], sem)

        tails(lambda row: zero_copy(row).start())
        tails(lambda row: zero_copy(row).wait())

        def unused(fn):
            def tile(j, c):
                fn(pl.multiple_of(tail_ref[2 * N_EXPERTS] + j * EXPERT_TILE, EXPERT_TILE))
                return c
            lax.fori_loop(0, tail_ref[2 * N_EXPERTS + 1], tile, 0)

        def zero_tile(row):
            return pltpu.make_async_copy(zero_ref, xs_ref.at[pl.ds(row, EXPERT_TILE)], sem)

        unused(lambda row: zero_tile(row).start())
        unused(lambda row: zero_tile(row).wait())

    onehot = _slot_matrix(slot_ref, [1.0] * TOP_K)
    local_ref[cur] = _mm_tn(onehot, h_ref[...])

    def block_copy(buf):
        return lambda local_row, sorted_row, rows: pltpu.make_async_copy(
            local_ref.at[buf, pl.ds(local_row, rows)], xs_ref.at[pl.ds(sorted_row, rows)],
            block_sem.at[buf])

    _for_each_block(meta_ref, lambda a, b, n: block_copy(cur)(a, b, n).start())

    @pl.when(i > 0)
    def _():
        _for_each_block(prev_meta_ref, lambda a, b, n: block_copy(1 - cur)(a, b, n).wait())

    @pl.when(i == pl.num_programs(0) - 1)
    def _():
        _for_each_block(meta_ref, lambda a, b, n: block_copy(cur)(a, b, n).wait())


def _dispatch(h, slots, meta, tails, n_rows):
    n_tok = h.shape[0]
    return pl.pallas_call(
        _dispatch_kernel,
        grid=(n_tok // TOKEN_TILE,),
        in_specs=[pl.BlockSpec((COPY_LIST_LEN,), lambda i: (i,), memory_space=pltpu.SMEM),
                  pl.BlockSpec((COPY_LIST_LEN,), lambda i: (jnp.maximum(i - 1, 0),), memory_space=pltpu.SMEM),
                  pl.BlockSpec((LANES,), lambda i: (0,), memory_space=pltpu.SMEM),
                  pl.BlockSpec((TOKEN_TILE, LANES), lambda i: (i, 0)),
                  pl.BlockSpec((TOKEN_TILE, D_MODEL), lambda i: (i, 0))],
        out_specs=pl.BlockSpec(memory_space=pl.ANY),
        out_shape=jax.ShapeDtypeStruct((n_rows, D_MODEL), F32),
        scratch_shapes=[pltpu.VMEM((2, LOCAL_ROWS, D_MODEL), F32), pltpu.VMEM((EXPERT_TILE, D_MODEL), F32),
                        pltpu.SemaphoreType.DMA(()), pltpu.SemaphoreType.DMA((2,))],
        compiler_params=_cparams(("arbitrary",), 48),
        name="dispatch",
    )(meta, meta, tails, slots, h)


def _combine_kernel(meta_ref, next_meta_ref, slot_ref, wt_ref, ys_ref, x_ref, mod_ref, o_ref, local_ref, sem):
    i = pl.program_id(0)
    cur = i % 2

    def block_copy(buf):
        return lambda local_row, sorted_row, rows: pltpu.make_async_copy(
            ys_ref.at[pl.ds(sorted_row, rows)], local_ref.at[buf, pl.ds(local_row, rows)], sem.at[buf])

    def start_gather(table_ref, buf):
        local_ref[buf, TOKEN_TILE * TOP_K:, :] = jnp.zeros((LOCAL_ROWS - TOKEN_TILE * TOP_K, D_MODEL), F32)
        _for_each_block(table_ref, lambda a, b, n: block_copy(buf)(a, b, n).start())

    @pl.when(i == 0)
    def _():
        start_gather(meta_ref, 0)

    @pl.when(i + 1 < pl.num_programs(0))
    def _():
        start_gather(next_meta_ref, 1 - cur)

    _for_each_block(meta_ref, lambda a, b, n: block_copy(cur)(a, b, n).wait())
    weights = _slot_matrix(slot_ref, [wt_ref[:, k:k + 1] for k in range(TOP_K)])
    acc = jnp.dot(weights, local_ref[cur].astype(BF16), preferred_element_type=F32)
    o_ref[...] = x_ref[...] + mod_ref[0, 5:6, :] * acc


def _combine(ys, slots, meta, wts, x, modseg):
    n_tok = x.shape[0]
    per_seg = SEG_ROWS // TOKEN_TILE
    full = pl.BlockSpec((TOKEN_TILE, D_MODEL), lambda i: (i, 0))
    small = pl.BlockSpec((TOKEN_TILE, LANES), lambda i: (i, 0))
    return pl.pallas_call(
        _combine_kernel,
        grid=(n_tok // TOKEN_TILE,),
        in_specs=[pl.BlockSpec((COPY_LIST_LEN,), lambda i: (i,), memory_space=pltpu.SMEM),
                  pl.BlockSpec((COPY_LIST_LEN,), lambda i: (jnp.minimum(i + 1, n_tok // TOKEN_TILE - 1),),
                               memory_space=pltpu.SMEM),
                  small, small,
                  pl.BlockSpec(memory_space=pl.ANY),
                  full,
                  pl.BlockSpec((1, N_MOD, D_MODEL), lambda i: (i // per_seg, 0, 0))],
        out_specs=full,
        out_shape=jax.ShapeDtypeStruct((n_tok, D_MODEL), F32),
        scratch_shapes=[pltpu.VMEM((2, LOCAL_ROWS, D_MODEL), F32), pltpu.SemaphoreType.DMA((2,))],
        compiler_params=_cparams(("arbitrary",), 56),
        name="combine",
    )(meta, meta, slots, wts, ys, x, modseg)


def _expert_kernel(te_ref, first_ref, slot_ref, next_ref, na_ref, x_ref, wgu_hbm, bgu_ref, wd_hbm, bd_ref,
                   o_ref, wgu_f, wd_f, wgu_s, wd_s, sem, *, layer):
    i = pl.program_id(0)
    active = i < na_ref[0]

    def fetch(e, s):
        return (pltpu.make_async_copy(wgu_hbm.at[layer, e], wgu_f.at[s], sem.at[0, s]),
                pltpu.make_async_copy(wd_hbm.at[layer, e], wd_f.at[s], sem.at[1, s]))

    @pl.when(jnp.logical_and(i == 0, first_ref[0] == 1))
    def _():
        for cp in fetch(te_ref[0], 0):
            cp.start()

    @pl.when(first_ref[i] == 1)
    def _():
        s = slot_ref[i]
        for cp in fetch(te_ref[i], s):
            cp.wait()
        wgu_s[...] = wgu_f[s].astype(BF16)
        wd_s[...] = wd_f[s].astype(BF16)

        @pl.when(next_ref[i] >= 0)
        def _():
            for cp in fetch(next_ref[i], 1 - s):
                cp.start()

    @pl.when(active)
    def _():
        gu = jnp.dot(x_ref[...].astype(BF16), wgu_s[...], preferred_element_type=F32) + bgu_ref[0, 0]
        gate = jnp.minimum(gu[:, :D_FF], SWIGLU_LIMIT)
        up = jnp.clip(gu[:, D_FF:], -SWIGLU_LIMIT, SWIGLU_LIMIT)
        act = gate * _sigmoid(SWIGLU_ALPHA * gate) * (up + 1.0)
        o_ref[...] = jnp.dot(act.astype(BF16), wd_s[...], preferred_element_type=F32) + bd_ref[0, 0]

    @pl.when(jnp.logical_not(active))
    def _():
        o_ref[...] = jnp.zeros_like(o_ref)


def _experts(xs, tile_expert, ends, n_active, w_gu, b_gu, w_down, b_down, layer):
    n_rows = xs.shape[0]
    depth = w_gu.shape[0]
    tm = EXPERT_TILE
    n_tiles = n_rows // tm
    tile = jnp.arange(n_tiles, dtype=jnp.int32)
    active = tile < n_active[0]
    first = (active & ((tile == 0) | (tile_expert != jnp.roll(tile_expert, 1)))).astype(jnp.int32)
    slot = (_prefix_sum(first) - 1) % 2
    next_tile = ends[tile_expert] // tm
    next_expert = jnp.where(next_tile < n_active[0],
                            tile_expert[jnp.minimum(next_tile, n_tiles - 1)], -1).astype(jnp.int32)

    def tiled(shape, index):
        return pl.BlockSpec(shape, lambda i, te, fi, sl, nx, na: index(i, te, na))

    grid_spec = pltpu.PrefetchScalarGridSpec(
        num_scalar_prefetch=5,
        grid=(n_tiles,),
        in_specs=[tiled((tm, D_MODEL), lambda i, te, na: (jnp.where(i < na[0], i, 0), 0)),
                  pl.BlockSpec(memory_space=pl.ANY),
                  tiled((1, 1, 1, 2 * D_FF), lambda i, te, na: (layer, te[i], 0, 0)),
                  pl.BlockSpec(memory_space=pl.ANY),
                  tiled((1, 1, 1, D_MODEL), lambda i, te, na: (layer, te[i], 0, 0))],
        out_specs=tiled((tm, D_MODEL), lambda i, te, na: (i, 0)),
        scratch_shapes=[pltpu.VMEM((2, D_MODEL, 2 * D_FF), F32), pltpu.VMEM((2, D_FF, D_MODEL), F32),
                        pltpu.VMEM((D_MODEL, 2 * D_FF), BF16), pltpu.VMEM((D_FF, D_MODEL), BF16),
                        pltpu.SemaphoreType.DMA((2, 2))],
    )
    return pl.pallas_call(
        functools.partial(_expert_kernel, layer=layer),
        grid_spec=grid_spec,
        out_shape=jax.ShapeDtypeStruct((n_rows, D_MODEL), F32),
        compiler_params=_cparams(("arbitrary",), 56),
        name="experts",
    )(tile_expert, first, slot.astype(jnp.int32), next_expert, n_active, xs, w_gu,
      b_gu.reshape(depth, N_EXPERTS, 1, 2 * D_FF), w_down, b_down.reshape(depth, N_EXPERTS, 1, D_MODEL))


def _moe(x, norm_g, modseg, w_router, b_router, w_gu, b_gu, w_down, b_down, layer):
    n_tok = x.shape[0]
    h, slots, wts, table = _router(x, norm_g, modseg, w_router, b_router, layer)
    tm = EXPERT_TILE
    n_tiles = n_tok // TOKEN_TILE
    n_rows = n_tok * TOP_K + n_tiles * N_EXPERTS * ROW_BLOCK + N_EXPERTS * tm
    n_rows = -(-n_rows // tm) * tm
    local_start = table[:, 0, :N_EXPERTS]
    padded = table[:, 1, :N_EXPERTS]
    earlier = table[:, 2, :N_EXPERTS]
    used = earlier[-1] + padded[-1]
    size = ((used + tm - 1) // tm) * tm
    ends = _prefix_sum(size)
    starts = ends - size
    sorted_start = starts[None, :] + earlier
    meta = _copy_lists(local_start, padded, sorted_start)
    tails = jnp.concatenate([starts + used, (size - used) // ROW_BLOCK,
                             ends[-1:], (n_rows - ends[-1:]) // tm,
                             jnp.zeros((LANES - 2 * N_EXPERTS - 2,), jnp.int32)])
    tile_row0 = jnp.arange(n_rows // tm, dtype=jnp.int32) * tm
    tile_expert = jnp.minimum(jnp.sum((ends[None, :] <= tile_row0[:, None]).astype(jnp.int32), axis=1),
                              N_EXPERTS - 1)
    n_active = (ends[-1:] // tm).astype(jnp.int32)
    xs = _dispatch(h, slots, meta, tails, n_rows)
    ys = _experts(xs, tile_expert, ends, n_active, w_gu, b_gu, w_down, b_down, layer)
    return _combine(ys, slots, meta, wts, x, modseg)


def _rope_tables(n_tok):
    rows = n_tok // GRID_W
    row = jnp.repeat(jnp.arange(rows, dtype=F32), GRID_W)
    col = jnp.tile(jnp.arange(GRID_W, dtype=F32), rows)
    quarter = HEAD_DIM // 4
    inv_freq = ROPE_BASE ** (-jnp.arange(quarter, dtype=F32) / quarter)
    ang_r = row[:, None] * inv_freq
    ang_c = col[:, None] * inv_freq
    ang = jnp.concatenate([ang_r, ang_r, ang_c, ang_c], axis=-1)
    cos, sin = jnp.cos(ang), jnp.sin(ang)
    first = (jnp.arange(HEAD_DIM) % (2 * quarter)) < quarter
    sin_a = jnp.where(first, -sin, 0.0)
    sin_b = jnp.where(first, 0.0, sin)
    return tuple(jnp.tile(t, (1, LANES // HEAD_DIM)) for t in (cos, sin_a, sin_b))


def kernel(x_prompt, x_sample, c, cache_k, cache_v, state_rglru, state_hgrn, c_ctx, w_mod, b_mod, norm1_g, norm2_g, w_in, q_norm_g, k_norm_g, conv_w, conv_b, rg_wa, rg_ba, rg_wx, rg_bx, rg_lambda, hgrn_lb_logits, hgrn_norm_g, w_attn_o, w_rnn_o, w_hgrn_o, w_out, w_router, b_router, w_gu, b_gu, w_down, b_down):
    depth = w_mod.shape[0]
    n_p, t_p = x_prompt.shape[0], x_prompt.shape[1]
    n_s, t_s = x_sample.shape[0], x_sample.shape[1]
    tok_p = n_p * t_p
    tok_s = n_s * t_s
    assert tok_p % SEG_ROWS == 0 and t_s == SEG_ROWS and SEG_ROWS % t_p == 0

    x = jnp.concatenate([x_prompt.reshape(tok_p, D_MODEL), x_sample.reshape(tok_s, D_MODEL)], axis=0)

    lb_all = jnp.moveaxis(_prefix_sum(jnp.moveaxis(jax.nn.softmax(hgrn_lb_logits.astype(F32), axis=0), 0, -1)),
                          -1, 0)
    lb_all = lb_all - lb_all[:1]
    rope = _rope_tables(t_s)

    assert 1 + n_s <= ROW_BLOCK
    cond8 = jnp.zeros((ROW_BLOCK, D_MODEL), F32).at[0].set(c_ctx).at[1:1 + n_s].set(c)
    mod = _modulation(cond8, w_mod, b_mod)
    seg_ids = jnp.asarray([0] * (tok_p // SEG_ROWS) + [1 + b for b in range(n_s)], jnp.int32)
    modsegs = mod[:, seg_ids].reshape(depth, seg_ids.shape[0], N_MOD, D_MODEL)

    blocks = jnp.stack([rg_wa, rg_wx], axis=2)
    eye = jnp.eye(RG_BLOCKS, dtype=blocks.dtype)
    w_gate = jnp.einsum("ldgnab,nm->lnadgmb", blocks, eye).reshape(depth, D_RNN, 4 * D_RNN)
    b_gate = jnp.stack([rg_ba, rg_bx], axis=2).reshape(depth, 1, 4 * D_RNN)

    new_k, new_v, new_hr, new_s = [], [], [], []
    for l in range(depth):
        modseg = modsegs[l]
        z = _input_projection(x, norm1_g, modseg, w_in, l)
        attn_p, k_l, v_l = _attention(z, q_norm_g, k_norm_g, l, n_p, t_p, 0)
        (attn_s,) = _attention(z, q_norm_g, k_norm_g, l, n_s, t_s, tok_p, rope=rope,
                               cache=(cache_k, cache_v))
        rnn_p, hr_l = _rglru(z, conv_w, conv_b, w_gate, b_gate, rg_lambda, l, n_p, t_p, 0)
        (rnn_s,) = _rglru(z, conv_w, conv_b, w_gate, b_gate, rg_lambda, l, n_s, t_s, tok_p,
                          state=state_rglru)
        hg_p, s_l = _hgrn(z, lb_all, hgrn_norm_g, l, n_p, t_p, 0)
        (hg_s,) = _hgrn(z, lb_all, hgrn_norm_g, l, n_s, t_s, tok_p, state=state_hgrn)
        x = _merge_project(((attn_p, attn_s), (rnn_p, rnn_s), (hg_p, hg_s)), z, x, modseg,
                           w_attn_o, w_rnn_o, w_hgrn_o, w_out, l)
        x = _moe(x, norm2_g, modseg, w_router, b_router, w_gu, b_gu, w_down, b_down, l)
        new_k.append(k_l.reshape(n_p, t_p, N_KV_HEADS, HEAD_DIM))
        new_v.append(v_l.reshape(n_p, t_p, N_KV_HEADS, HEAD_DIM))
        new_hr.append(hr_l)
        new_s.append(s_l)

    y_prompt = x[:tok_p].reshape(n_p, t_p, D_MODEL)
    y_sample = x[tok_p:].reshape(n_s, t_s, D_MODEL)
    return (y_prompt, y_sample, jnp.stack(new_k, axis=1), jnp.stack(new_v, axis=1),
            jnp.stack(new_hr, axis=1), jnp.stack(new_s, axis=1))
```

```python
import functools

import jax
import jax.numpy as jnp
from jax import lax
from jax.experimental import pallas as pl
from jax.experimental.pallas import tpu as pltpu

F32 = jnp.float32
BF16 = jnp.bfloat16

D_MODEL = 1024
GRID_W = 64
HEAD_DIM = 64
N_HEADS = 8
N_KV_HEADS = 2
KV_GROUP = N_HEADS // N_KV_HEADS
ATTN_W = N_HEADS * HEAD_DIM
KV_W = N_KV_HEADS * HEAD_DIM
ROPE_BASE = 10000.0
D_RNN = 512
RG_BLOCKS = 8
RG_BW = D_RNN // RG_BLOCKS
RG_C = 8.0
HG_HEADS = 8
HG_DK = 64
HG_DV = 64
HG_W = HG_HEADS * HG_DK
N_PAIRS = HG_W // 128
N_EXPERTS = 32
TOP_K = 4
D_FF = D_MODEL
SWIGLU_LIMIT = 7.0
SWIGLU_ALPHA = 1.702
N_MOD = 6
EPS = 1e-6
IN_W = ATTN_W + 2 * KV_W + 2 * D_RNN + 5 * HG_W + 3 * D_MODEL

LANES = 128
SEG_ROWS = 1024
COL_TILE = 256
HG_CHUNK = 256
Q_TILE = 256
EXPERT_TILE = 400
BIG_BLOCK = 64
TOKEN_TILE = 512
ROW_BLOCK = 8
MOVE_BLOCK = 16
LOCAL_ROWS = TOKEN_TILE * TOP_K + N_EXPERTS * MOVE_BLOCK
NEG_BIG = -1e30
LOG2_E = 1.4426950408889634

_SRC_TILES = tuple(range(17, 29)) + tuple(range(7, 17)) + tuple(range(3, 7)) + (0, 1, 2)
Z_GL = 0
Z_HG = 3 * D_MODEL
Z_RG = Z_HG + 5 * HG_W
Z_Q = Z_RG + 2 * D_RNN
Z_K = Z_Q + ATTN_W
Z_V = Z_K + KV_W
Z_W = -(-IN_W // (2 * COL_TILE)) * (2 * COL_TILE)


def _cparams(sem, vmem_mb):
    return pltpu.CompilerParams(dimension_semantics=sem, vmem_limit_bytes=vmem_mb * 1024 * 1024)


def _prefix_sum(x):
    n = x.shape[-1]
    upto = (jnp.arange(n)[None, :] <= jnp.arange(n)[:, None]).astype(x.dtype)
    return jnp.sum(x[..., None, :] * upto, axis=-1)


def _mm(a, b):
    return jnp.dot(a.astype(BF16), b.astype(BF16), preferred_element_type=F32)


def _mm_nt(a, b):
    return lax.dot_general(a.astype(BF16), b.astype(BF16), (((1,), (1,)), ((), ())),
                           preferred_element_type=F32)


def _mm_tn(a, b):
    return lax.dot_general(a.astype(BF16), b.astype(BF16), (((0,), (0,)), ((), ())),
                           preferred_element_type=F32)


def _sigmoid(x):
    return 1.0 / (1.0 + jnp.exp(-x))


def _softplus(x):
    return jnp.maximum(x, 0.0) + jnp.log1p(jnp.exp(-jnp.abs(x)))


def _modulated_norm(x, g, scale, shift):
    y = x * lax.rsqrt(jnp.mean(x * x, axis=-1, keepdims=True) + EPS)
    return (y * g) * (1.0 + scale) + shift


def _mod_kernel(c_ref, w_ref, b_ref, o_ref):
    c = c_ref[...]
    s = c * _sigmoid(c)
    o_ref[0] = _mm(s, w_ref[0]) + b_ref[0]


def _modulation(cond8, w_mod, b_mod):
    depth = w_mod.shape[0]
    tn = 1024
    return pl.pallas_call(
        _mod_kernel,
        grid=(depth, N_MOD * D_MODEL // tn),
        in_specs=[pl.BlockSpec((ROW_BLOCK, D_MODEL), lambda l, j: (0, 0)),
                  pl.BlockSpec((1, D_MODEL, tn), lambda l, j: (l, 0, j)),
                  pl.BlockSpec((1, 1, tn), lambda l, j: (l, 0, j))],
        out_specs=pl.BlockSpec((1, ROW_BLOCK, tn), lambda l, j: (l, 0, j)),
        out_shape=jax.ShapeDtypeStruct((depth, ROW_BLOCK, N_MOD * D_MODEL), F32),
        compiler_params=_cparams(("arbitrary", "arbitrary"), 32),
        name="modulation",
    )(cond8, w_mod, b_mod.reshape(depth, 1, N_MOD * D_MODEL))


def _inproj_kernel(src_ref, x_ref, g_ref, mod_ref, wa_ref, wb_ref, o_ref, h_ref, w_s):
    j = pl.program_id(1)

    @pl.when(j == 0)
    def _():
        for s in range(x_ref.shape[0] // SEG_ROWS):
            rows = slice(s * SEG_ROWS, (s + 1) * SEG_ROWS)
            h = _modulated_norm(x_ref[rows, :], g_ref[0], mod_ref[s, 1:2, :], mod_ref[s, 0:1, :])
            h_ref[rows, :] = h.astype(BF16)

    @pl.when(pl.program_id(0) == 0)
    def _():
        w_s[j, :, 0:COL_TILE] = wa_ref[0].astype(BF16)
        w_s[j, :, COL_TILE:2 * COL_TILE] = wb_ref[0].astype(BF16)

    o_ref[...] = jnp.dot(h_ref[...], w_s[j], preferred_element_type=F32)


def _input_projection(x, norm_g, modseg, w_in, layer):
    n_tok = x.shape[0]
    segs = 2
    tm = segs * SEG_ROWS
    n_col = Z_W // (2 * COL_TILE)
    src = jnp.asarray(_SRC_TILES + (_SRC_TILES[-1],) * (Z_W // COL_TILE - len(_SRC_TILES)), jnp.int32)

    def wspec(k):
        return pl.BlockSpec((1, D_MODEL, COL_TILE),
                            lambda i, j, s: (layer, 0, s[2 * jnp.where(i == 0, j, n_col - 1) + k]))

    grid_spec = pltpu.PrefetchScalarGridSpec(
        num_scalar_prefetch=1,
        grid=(n_tok // tm, n_col),
        in_specs=[pl.BlockSpec((tm, D_MODEL), lambda i, j, s: (i, 0)),
                  pl.BlockSpec((1, 1, D_MODEL), lambda i, j, s: (layer, 0, 0)),
                  pl.BlockSpec((segs, N_MOD, D_MODEL), lambda i, j, s: (i, 0, 0)),
                  wspec(0), wspec(1)],
        out_specs=pl.BlockSpec((tm, 2 * COL_TILE), lambda i, j, s: (i, j)),
        scratch_shapes=[pltpu.VMEM((tm, D_MODEL), BF16),
                        pltpu.VMEM((n_col, D_MODEL, 2 * COL_TILE), BF16)],
    )
    return pl.pallas_call(
        _inproj_kernel,
        grid_spec=grid_spec,
        out_shape=jax.ShapeDtypeStruct((n_tok, Z_W), F32),
        compiler_params=_cparams(("arbitrary", "arbitrary"), 56),
        name="input_projection",
    )(src, x, norm_g.reshape(-1, 1, D_MODEL), modseg, w_in, w_in)


def _head_rms(x, g):
    lane = lax.broadcasted_iota(jnp.int32, x.shape, 1)
    lo = lane < HEAD_DIM
    xx = x * x
    s_lo = jnp.sum(jnp.where(lo, xx, 0.0), axis=-1, keepdims=True)
    s_hi = jnp.sum(jnp.where(lo, 0.0, xx), axis=-1, keepdims=True)
    inv = jnp.where(lo, lax.rsqrt(s_lo * (1.0 / HEAD_DIM) + EPS),
                    lax.rsqrt(s_hi * (1.0 / HEAD_DIM) + EPS))
    return x * inv * g


def _rope(x, cos, sin_a, sin_b):
    q = HEAD_DIM // 4
    return x * cos + pltpu.roll(x, LANES - q, 1) * sin_a + pltpu.roll(x, q, 1) * sin_b


def _attn_kernel(*refs, t_len, use_ctx):
    if use_ctx:
        (q_ref, k_ref, v_ref, qg_ref, kg_ref, cos_q_ref, sa_q_ref, sb_q_ref,
         cos_k_ref, sa_k_ref, sb_k_ref, ck_ref, cv_ref, o_ref, kall_ref, vall_ref) = refs
    else:
        (q_ref, k_ref, v_ref, qg_ref, kg_ref, o_ref, ko_ref, vo_ref, kall_ref, vall_ref) = refs

    @pl.when(pl.program_id(1) == 0)
    def _():
        k = _head_rms(k_ref[...], kg_ref[...])
        if use_ctx:
            k = _rope(k, cos_k_ref[...], sa_k_ref[...], sb_k_ref[...])
            kall_ref[0:t_len, :] = k.astype(BF16)
            vall_ref[0:t_len, :] = v_ref[...].astype(BF16)
            kall_ref[t_len:, :] = ck_ref[0, 0].astype(BF16)
            vall_ref[t_len:, :] = cv_ref[0, 0].astype(BF16)
        else:
            ko_ref[0] = k
            vo_ref[0] = v_ref[...]
            kall_ref[...] = k.astype(BF16)
            vall_ref[...] = v_ref[...].astype(BF16)

    heads = []
    for c in range(ATTN_W // LANES):
        qc = _head_rms(q_ref[:, c * LANES:(c + 1) * LANES], qg_ref[...])
        if use_ctx:
            qc = _rope(qc, cos_q_ref[...], sa_q_ref[...], sb_q_ref[...])
        qc = (qc * (HEAD_DIM ** -0.5)).astype(BF16)
        heads.append(qc[:, :HEAD_DIM])
        heads.append(qc[:, HEAD_DIM:])
    tq = q_ref.shape[0]
    for g in range(N_KV_HEADS):
        qs = jnp.concatenate(heads[g * KV_GROUP:(g + 1) * KV_GROUP], axis=0)
        kh = kall_ref[:, g * HEAD_DIM:(g + 1) * HEAD_DIM]
        vh = vall_ref[:, g * HEAD_DIM:(g + 1) * HEAD_DIM]
        s = _mm_nt(qs, kh)
        m = jnp.max(s, axis=-1, keepdims=True)
        p = jnp.exp(s - m)
        den = jnp.sum(p, axis=-1, keepdims=True)
        o = _mm(p, vh) / den
        for j in range(KV_GROUP):
            hh = g * KV_GROUP + j
            o_ref[:, hh * HEAD_DIM:(hh + 1) * HEAD_DIM] = o[j * tq:(j + 1) * tq].astype(BF16)


def _attention(z, q_g, k_g, layer, n_batch, t_len, row0, rope=None, cache=None):
    use_ctx = cache is not None
    nq = t_len // Q_TILE
    rb0 = row0 // t_len
    qb0 = row0 // Q_TILE
    qg = jnp.tile(q_g[layer], 2).reshape(1, LANES)
    kg = jnp.tile(k_g[layer], 2).reshape(1, LANES)
    n_out_tok = n_batch * t_len
    vec = pl.BlockSpec((1, LANES), lambda b, i: (0, 0))
    in_specs = [pl.BlockSpec((Q_TILE, ATTN_W), lambda b, i: (qb0 + b * nq + i, Z_Q // ATTN_W)),
                pl.BlockSpec((t_len, KV_W), lambda b, i: (rb0 + b, Z_K // KV_W)),
                pl.BlockSpec((t_len, KV_W), lambda b, i: (rb0 + b, Z_V // KV_W)),
                vec, vec]
    args = [z, z, z, qg, kg]
    out_specs = [pl.BlockSpec((Q_TILE, ATTN_W), lambda b, i: (b * nq + i, 0))]
    out_shape = [jax.ShapeDtypeStruct((n_out_tok, ATTN_W), BF16)]
    t_keys = t_len
    if use_ctx:
        cos, sin_a, sin_b = rope
        cache_k, cache_v = cache
        past = cache_k.shape[2]
        t_keys = t_len + past
        tq_spec = pl.BlockSpec((Q_TILE, LANES), lambda b, i: (i, 0))
        tk_spec = pl.BlockSpec((t_len, LANES), lambda b, i: (0, 0))
        c_spec = pl.BlockSpec((1, 1, past, KV_W), lambda b, i: (b, layer, 0, 0))
        in_specs += [tq_spec, tq_spec, tq_spec, tk_spec, tk_spec, tk_spec, c_spec, c_spec]
        args += [cos, sin_a, sin_b, cos, sin_a, sin_b,
                 cache_k.reshape(cache_k.shape[0], cache_k.shape[1], past, KV_W),
                 cache_v.reshape(cache_v.shape[0], cache_v.shape[1], past, KV_W)]
    else:
        kv_spec = pl.BlockSpec((1, t_len, KV_W), lambda b, i: (b, 0, 0))
        out_specs += [kv_spec, kv_spec]
        out_shape += [jax.ShapeDtypeStruct((n_batch, t_len, KV_W), F32)] * 2
    return pl.pallas_call(
        functools.partial(_attn_kernel, t_len=t_len, use_ctx=use_ctx),
        grid=(n_batch, nq),
        in_specs=in_specs,
        out_specs=out_specs,
        out_shape=out_shape,
        scratch_shapes=[pltpu.VMEM((t_keys, KV_W), BF16), pltpu.VMEM((t_keys, KV_W), BF16)],
        compiler_params=_cparams(("arbitrary", "arbitrary"), 48),
        name="attention_ctx" if use_ctx else "attention",
    )(*args)


def _shift_rows(x, d):
    n = x.shape[0]
    row = lax.broadcasted_iota(jnp.int32, x.shape, 0)
    y = pltpu.roll(x, d % n, 0)
    if d > 0:
        return jnp.where(row >= d, y, 0.0)
    return jnp.where(row < n + d, y, 0.0)


def _linear_scan(a, u, reverse):
    n, w = a.shape
    row = lax.broadcasted_iota(jnp.int32, a.shape, 0)

    def shifted(x, d, fill):
        if d % ROW_BLOCK == 0:
            pad = jnp.full((d, w), fill, x.dtype)
            return jnp.concatenate([x[d:], pad] if reverse else [pad, x[:n - d]], axis=0)
        if reverse:
            return jnp.where(row < n - d, pltpu.roll(x, n - d, 0), fill)
        return jnp.where(row >= d, pltpu.roll(x, d, 0), fill)

    d = 1
    while d < n:
        u = a * shifted(u, d, 0.0) + u
        a = a * shifted(a, d, 1.0)
        d *= 2
    return a, u


def _rglru_kernel(*refs, use_ctx):
    if use_ctx:
        (x_ref, y_ref, cw_ref, cb_ref, wg_ref, bg_ref, lam_ref, h0_ref, o_ref) = refs
    else:
        (x_ref, y_ref, cw_ref, cb_ref, wg_ref, bg_ref, lam_ref, o_ref, last_ref) = refs
    x = x_ref[...]
    cw = cw_ref[0]
    xr = cb_ref[0] + _shift_rows(x, 1) * cw[0:1] + x * cw[1:2] \
        + _shift_rows(x, -1) * cw[2:3] + _shift_rows(x, -2) * cw[3:4]
    gates = _sigmoid(_mm(xr, wg_ref[0]) + bg_ref[0])
    n = x.shape[0]
    total = None
    lasts = []
    for d in range(2):
        r = gates[:, (2 * d) * D_RNN:(2 * d + 1) * D_RNN]
        i = gates[:, (2 * d + 1) * D_RNN:(2 * d + 2) * D_RNN]
        log_a = (-RG_C * _softplus(-lam_ref[0, d:d + 1, :])) * r
        a = jnp.exp(log_a)
        u = jnp.sqrt(1.0 - a * a) * (i * xr)
        a_cum, h = _linear_scan(a, u, reverse=(d == 1))
        if use_ctx:
            h = h + a_cum * h0_ref[0, 0, d:d + 1, :]
        else:
            lasts.append(h[0:1] if d == 1 else h[n - 1:n])
        total = h if total is None else total + h
    o_ref[...] = (jax.nn.gelu(y_ref[...]) * total).astype(BF16)
    if not use_ctx:
        last_ref[0] = jnp.concatenate(lasts, axis=0)


def _rglru(z, conv_w, conv_b, w_gate, b_gate, lam, layer, n_batch, t_len, row0, state=None):
    use_ctx = state is not None
    rb0 = row0 // t_len
    in_specs = [pl.BlockSpec((t_len, D_RNN), lambda b: (rb0 + b, Z_RG // D_RNN)),
                pl.BlockSpec((t_len, D_RNN), lambda b: (rb0 + b, Z_RG // D_RNN + 1)),
                pl.BlockSpec((1, 4, D_RNN), lambda b: (layer, 0, 0)),
                pl.BlockSpec((1, 1, D_RNN), lambda b: (layer, 0, 0)),
                pl.BlockSpec((1, D_RNN, 4 * D_RNN), lambda b: (layer, 0, 0)),
                pl.BlockSpec((1, 1, 4 * D_RNN), lambda b: (layer, 0, 0)),
                pl.BlockSpec((1, 2, D_RNN), lambda b: (layer, 0, 0))]
    args = [z, z, conv_w, conv_b.reshape(-1, 1, D_RNN), w_gate, b_gate, lam]
    out_specs = [pl.BlockSpec((t_len, D_RNN), lambda b: (b, 0))]
    out_shape = [jax.ShapeDtypeStruct((n_batch * t_len, D_RNN), BF16)]
    if use_ctx:
        in_specs.append(pl.BlockSpec((1, 1, 2, D_RNN), lambda b: (b, layer, 0, 0)))
        args.append(state)
    else:
        out_specs.append(pl.BlockSpec((1, 2, D_RNN), lambda b: (b, 0, 0)))
        out_shape.append(jax.ShapeDtypeStruct((n_batch, 2, D_RNN), F32))
    return pl.pallas_call(
        functools.partial(_rglru_kernel, use_ctx=use_ctx),
        grid=(n_batch,),
        in_specs=in_specs,
        out_specs=out_specs,
        out_shape=out_shape,
        compiler_params=_cparams(("arbitrary",), 56),
        name="rglru_ctx" if use_ctx else "rglru",
    )(*args)


def _cumsum_rows(g, reverse):
    n, w = g.shape
    m = n // ROW_BLOCK

    def scan(x, size, pos):
        d = 1
        while d < size:
            if reverse:
                x = x + jnp.where(pos < size - d, pltpu.roll(x, x.shape[0] - d, 0), 0.0)
            else:
                x = x + jnp.where(pos >= d, pltpu.roll(x, d, 0), 0.0)
            d *= 2
        return x

    g = scan(g, ROW_BLOCK, lax.broadcasted_iota(jnp.int32, g.shape, 0) & (ROW_BLOCK - 1))
    g3 = g.reshape(m, ROW_BLOCK, w)
    total = g3[:, 0, :] if reverse else g3[:, ROW_BLOCK - 1, :]
    before = scan(total, m, lax.broadcasted_iota(jnp.int32, total.shape, 0)) - total
    return (g3 + before[:, None, :]).reshape(n, w)


def _block_row_fn(gc, reverse):
    n, w = gc.shape
    g3 = gc.reshape(n // ROW_BLOCK, ROW_BLOCK, w)
    edge_row = 0 if reverse else ROW_BLOCK - 1
    edge = jnp.broadcast_to(g3[:, edge_row:edge_row + 1, :], g3.shape)
    sub = lax.broadcasted_iota(jnp.int32, g3.shape, 1)

    def at(blk):
        r = blk // 2 if reverse else blk // 2 - 1
        if blk > ROW_BLOCK:
            groups = blk // ROW_BLOCK
            e4 = edge.reshape(n // blk, groups, ROW_BLOCK, w)
            pick = r // ROW_BLOCK
            return jnp.broadcast_to(e4[:, pick:pick + 1], e4.shape).reshape(n, w)
        out = jnp.broadcast_to(g3[:, r:r + 1, :], g3.shape)
        for b0 in range(blk, ROW_BLOCK, blk):
            out = jnp.where(sub >= b0, jnp.broadcast_to(g3[:, b0 + r:b0 + r + 1, :], g3.shape), out)
        return out.reshape(n, w)

    return at


def _first_head(shape):
    lane = lax.broadcasted_iota(jnp.int32, shape, len(shape) - 1)
    return (lane & (LANES - 1)) < HG_DK


def _pair(x, p):
    return x[:, p * LANES:(p + 1) * LANES]


def _hgrn_gates(f_logit, lb):
    y = jnp.exp(-jnp.abs(f_logit))
    log_sig = jnp.minimum(f_logit, 0.0) - jnp.log(1.0 + y)
    a = jnp.log(lb)
    b = jnp.log1p(-lb) + log_sig
    log_f = jnp.maximum(a, b) + jnp.log(1.0 + jnp.exp(-jnp.abs(a - b)))
    k = (1.0 - lb) * (jnp.where(f_logit >= 0.0, y, 1.0) / (1.0 + y))
    return log_f * LOG2_E, k


def _hgrn_chunk_state(gc, k, v, reverse):
    n = gc.shape[0]
    g_tot = gc[0:1] if reverse else gc[n - 1:n]
    k_end = (k * jnp.exp2(jnp.minimum(g_tot - gc, 0.0))).astype(BF16)
    vb = v.astype(BF16)
    same_head = (_first_head((LANES, LANES))
                 == (lax.broadcasted_iota(jnp.int32, (LANES, LANES), 0) < HG_DK))
    ds = [jnp.where(same_head, _mm_tn(_pair(k_end, p), _pair(vb, p)), 0.0) for p in range(N_PAIRS)]
    return ds, jnp.exp2(g_tot)


def _decay_state(s, e_row):
    n = s.shape[0]
    eye = (lax.broadcasted_iota(jnp.int32, (n, n), 0) == lax.broadcasted_iota(jnp.int32, (n, n), 1))
    col = jnp.sum(jnp.where(eye, jnp.broadcast_to(e_row, (n, n)), 0.0), axis=1, keepdims=True)
    return s * col


def _hgrn_intra_scores(q, ks, gcs, txs):
    n = q.shape[0]
    row = lax.broadcasted_iota(jnp.int32, q.shape, 0)
    first = _first_head(q.shape)
    boundary = [_block_row_fn(gcs[0], False), _block_row_fn(gcs[1], True)]
    scores = [None] * HG_HEADS
    w = 1
    while w < n:
        blk = 2 * w
        second = (row & w) != 0
        e_f = jnp.exp2(-jnp.abs(gcs[0] - boundary[0](blk)))
        e_b = jnp.exp2(-jnp.abs(gcs[1] - boundary[1](blk)))
        q_f = jnp.where(second, q * e_f, 0.0).astype(BF16)
        q_b = jnp.where(second, 0.0, q * e_b).astype(BF16)
        k_f = jnp.where(second, 0.0, ks[0] * e_f)
        k_b = jnp.where(second, ks[1] * e_b, 0.0)
        k_heads = [(jnp.where(first, k_f, 0.0).astype(BF16), jnp.where(first, k_b, 0.0).astype(BF16)),
                   (jnp.where(first, 0.0, k_f).astype(BF16), jnp.where(first, 0.0, k_b).astype(BF16))]
        sel = (txs >= w) & (txs < blk)
        for p in range(N_PAIRS):
            qc = jnp.concatenate([_pair(q_f, p), _pair(q_b, p)], axis=1)
            for j in range(2):
                kc = jnp.concatenate([_pair(k_heads[j][0], p), _pair(k_heads[j][1], p)], axis=1)
                s = jnp.where(sel, _mm_nt(qc, kc), 0.0)
                h = 2 * p + j
                scores[h] = s if scores[h] is None else scores[h] + s
        w = blk
    return scores


def _hgrn_kernel(*refs, t_len, use_ctx):
    if use_ctx:
        (q_ref, ff_ref, fb_ref, i_ref, og_ref, lb_ref, ng_ref, s0_ref, o_ref, sin_ref, gk_ref) = refs
    else:
        (q_ref, ff_ref, fb_ref, i_ref, og_ref, lb_ref, ng_ref, o_ref, sl_ref) = refs
    c = HG_CHUNK
    n_chunks = t_len // c
    f_refs = (ff_ref, fb_ref)

    def load_dir(d, r0):
        log_f, k = _hgrn_gates(f_refs[d][pl.ds(r0, c), :], lb_ref[0, d:d + 1, :])
        return _cumsum_rows(log_f, reverse=(d == 1)), k

    if use_ctx:
        zero = jnp.zeros((HG_DK, HG_DV), F32)
        for d in range(2):
            states = []
            for p in range(N_PAIRS):
                top = jnp.concatenate([s0_ref[0, 0, d, 2 * p], zero], axis=1)
                bot = jnp.concatenate([zero, s0_ref[0, 0, d, 2 * p + 1]], axis=1)
                states.append(jnp.concatenate([top, bot], axis=0))
            order = range(n_chunks) if d == 0 else range(n_chunks - 1, -1, -1)
            for ci in order:
                for p in range(N_PAIRS):
                    sin_ref[ci, p, d * LANES:(d + 1) * LANES, :] = states[p]
                gc, k = load_dir(d, ci * c)
                gk_ref[d, 0, pl.ds(ci * c, c), :] = gc
                gk_ref[d, 1, pl.ds(ci * c, c), :] = k
                ds, e_tot = _hgrn_chunk_state(gc, k, i_ref[pl.ds(ci * c, c), :], reverse=(d == 1))
                states = [_decay_state(states[p], _pair(e_tot, p)) + ds[p] for p in range(N_PAIRS)]

    ti = lax.broadcasted_iota(jnp.int32, (c, c), 0)
    si = lax.broadcasted_iota(jnp.int32, (c, c), 1)
    txs = ti ^ si
    diag = ti == si

    def chunk_body(ci, carry):
        r0 = pl.multiple_of(ci * c, c)
        hq = q_ref[pl.ds(r0, c), :]
        q = hq * _sigmoid(hq)
        v = i_ref[pl.ds(r0, c), :]
        gcs, ks = [], []
        for d in range(2):
            if use_ctx:
                gc, k = gk_ref[d, 0, pl.ds(r0, c), :], gk_ref[d, 1, pl.ds(r0, c), :]
            else:
                gc, k = load_dir(d, r0)
            gcs.append(gc)
            ks.append(k)
        scores = _hgrn_intra_scores(q, ks, gcs, txs)
        first = _first_head(q.shape)
        k_sum = ks[0] + ks[1]
        v_heads = (jnp.where(first, v, 0.0).astype(BF16), jnp.where(first, 0.0, v).astype(BF16))
        k_heads = (jnp.where(first, k_sum, 0.0).astype(BF16), jnp.where(first, 0.0, k_sum).astype(BF16))
        q_bf = q.astype(BF16)
        if use_ctx:
            qe = [(q * jnp.exp2(gcs[d])).astype(BF16) for d in range(2)]
        else:
            for d in range(2):
                ds, _ = _hgrn_chunk_state(gcs[d], ks[d], v, reverse=(d == 1))
                for p in range(N_PAIRS):
                    sl_ref[0, d, 2 * p] = ds[p][:HG_DK, :HG_DV]
                    sl_ref[0, d, 2 * p + 1] = ds[p][HG_DK:, HG_DV:]
        outs = []
        for p in range(N_PAIRS):
            o = None
            for j in range(2):
                s_diag = jnp.where(diag, _mm_nt(_pair(q_bf, p), _pair(k_heads[j], p)), 0.0)
                t = _mm(scores[2 * p + j] + s_diag, _pair(v_heads[j], p))
                o = t if o is None else o + t
            if use_ctx:
                o = o + _mm(jnp.concatenate([_pair(qe[0], p), _pair(qe[1], p)], axis=1), sin_ref[ci, p])
            outs.append(_head_rms(o, ng_ref[0]))
        og = og_ref[pl.ds(r0, c), :]
        o_all = jnp.concatenate(outs, axis=1) * (og * _sigmoid(og))
        o_ref[pl.ds(r0, c), :] = o_all.astype(BF16)
        return carry

    if n_chunks == 1:
        chunk_body(0, 0)
    else:
        lax.fori_loop(0, n_chunks, chunk_body, 0)


def _hgrn(z, lb, norm_g, layer, n_batch, t_len, row0, state=None):
    use_ctx = state is not None
    rb0 = row0 // t_len
    cb = Z_HG // HG_W

    def zspec(k):
        return pl.BlockSpec((t_len, HG_W), lambda b: (rb0 + b, cb + k))

    in_specs = [zspec(0), zspec(1), zspec(2), zspec(3), zspec(4),
                pl.BlockSpec((1, 2, HG_W), lambda b: (layer, 0, 0)),
                pl.BlockSpec((1, 1, LANES), lambda b: (layer, 0, 0))]
    args = [z, z, z, z, z, lb, jnp.tile(norm_g, (1, LANES // HG_DV)).reshape(-1, 1, LANES)]
    out_specs = [pl.BlockSpec((t_len, HG_W), lambda b: (b, 0))]
    out_shape = [jax.ShapeDtypeStruct((n_batch * t_len, HG_W), BF16)]
    scratch = []
    if use_ctx:
        in_specs.append(pl.BlockSpec((1, 1, 2, HG_HEADS, HG_DK, HG_DV), lambda b: (b, layer, 0, 0, 0, 0)))
        args.append(state)
        scratch.append(pltpu.VMEM((t_len // HG_CHUNK, N_PAIRS, 2 * LANES, LANES), F32))
        scratch.append(pltpu.VMEM((2, 2, t_len, HG_W), F32))
    else:
        out_specs.append(pl.BlockSpec((1, 2, HG_HEADS, HG_DK, HG_DV), lambda b: (b, 0, 0, 0, 0)))
        out_shape.append(jax.ShapeDtypeStruct((n_batch, 2, HG_HEADS, HG_DK, HG_DV), F32))
    return pl.pallas_call(
        functools.partial(_hgrn_kernel, t_len=t_len, use_ctx=use_ctx),
        grid=(n_batch,),
        in_specs=in_specs,
        out_specs=out_specs,
        out_shape=out_shape,
        scratch_shapes=scratch,
        compiler_params=_cparams(("arbitrary",), 56),
        name="hgrn_ctx" if use_ctx else "hgrn",
    )(*args)


def _merge_kernel(ap_ref, as_ref, rp_ref, rs_ref, hp_ref, hs_ref, g0_ref, g1_ref, g2_ref, x_ref, mod_ref,
                  wa_ref, wr_ref, wh_ref, wo_ref, o_ref, wa_s, wr_s, wh_s, wo_s, *, prompt_tiles):
    @pl.when(pl.program_id(0) == 0)
    def _():
        wa_s[...] = wa_ref[0].astype(BF16)
        wr_s[...] = wr_ref[0].astype(BF16)
        wh_s[...] = wh_ref[0].astype(BF16)
        wo_s[...] = wo_ref[0].astype(BF16)

    is_prompt = pl.program_id(0) < prompt_tiles

    def proj(p_ref, s_ref, w_s):
        return jnp.dot(jnp.where(is_prompt, p_ref[...], s_ref[...]), w_s[...], preferred_element_type=F32)

    merged = (_sigmoid(g0_ref[...]) * proj(ap_ref, as_ref, wa_s)
              + _sigmoid(g1_ref[...]) * proj(rp_ref, rs_ref, wr_s)
              + _sigmoid(g2_ref[...]) * proj(hp_ref, hs_ref, wh_s))
    out = jnp.dot(merged.astype(BF16), wo_s[...], preferred_element_type=F32)
    o_ref[...] = x_ref[...] + mod_ref[0, 2:3, :] * out


def _merge_project(mixed, z, x, modseg, w_attn_o, w_rnn_o, w_hgrn_o, w_out, layer):
    n_tok = x.shape[0]
    tm = 512
    per_seg = SEG_ROWS // tm
    prompt_tiles = mixed[0][0].shape[0] // tm
    half_p = pl.BlockSpec((tm, ATTN_W), lambda i: (jnp.minimum(i, prompt_tiles - 1), 0))
    half_s = pl.BlockSpec((tm, ATTN_W), lambda i: (jnp.maximum(i - prompt_tiles, 0), 0))
    full = pl.BlockSpec((tm, D_MODEL), lambda i: (i, 0))

    def gspec(k):
        return pl.BlockSpec((tm, D_MODEL), lambda i: (i, Z_GL // D_MODEL + k))

    def wspec(rows):
        return pl.BlockSpec((1, rows, D_MODEL), lambda i: (layer, 0, 0))

    return pl.pallas_call(
        functools.partial(_merge_kernel, prompt_tiles=prompt_tiles),
        grid=(n_tok // tm,),
        in_specs=[half_p, half_s, half_p, half_s, half_p, half_s, gspec(0), gspec(1), gspec(2), full,
                  pl.BlockSpec((1, N_MOD, D_MODEL), lambda i: (i // per_seg, 0, 0)),
                  wspec(ATTN_W), wspec(D_RNN), wspec(HG_W), wspec(D_MODEL)],
        out_specs=full,
        out_shape=jax.ShapeDtypeStruct((n_tok, D_MODEL), F32),
        scratch_shapes=[pltpu.VMEM((ATTN_W, D_MODEL), BF16), pltpu.VMEM((D_RNN, D_MODEL), BF16),
                        pltpu.VMEM((HG_W, D_MODEL), BF16), pltpu.VMEM((D_MODEL, D_MODEL), BF16)],
        compiler_params=_cparams(("arbitrary",), 56),
        name="merge_project",
    )(*[part for pair in mixed for part in pair], z, z, z, x, modseg, w_attn_o, w_rnn_o, w_hgrn_o, w_out)


def _router_kernel(x_ref, g_ref, mod_ref, wr_ref, br_ref, h_ref, slot_ref, wt_ref, meta_ref, run_ref):
    @pl.when(pl.program_id(0) == 0)
    def _():
        run_ref[...] = jnp.zeros_like(run_ref)

    h = _modulated_norm(x_ref[...], g_ref[0], mod_ref[0, 4:5, :], mod_ref[0, 3:4, :])
    h_ref[...] = h
    h_hi = h.astype(BF16)
    h_lo = (h - h_hi.astype(F32)).astype(BF16)
    w = wr_ref[0]
    w_hi = w.astype(BF16)
    w_lo = (w - w_hi.astype(F32)).astype(BF16)
    logits = (jnp.dot(h_hi, w_hi, preferred_element_type=F32) + jnp.dot(h_hi, w_lo, preferred_element_type=F32)
              + jnp.dot(h_lo, w_hi, preferred_element_type=F32)) + br_ref[0]
    tm = logits.shape[0]
    lane = lax.broadcasted_iota(jnp.int32, logits.shape, 1)
    ids, vals = [], []
    chosen = jnp.zeros(logits.shape, jnp.bool_)
    work = logits
    for _ in range(TOP_K):
        m = jnp.max(work, axis=-1, keepdims=True)
        idx = jnp.min(jnp.where(work == m, lane, LANES), axis=-1, keepdims=True)
        hit = lane == idx
        chosen = chosen | hit
        work = jnp.where(hit, -jnp.inf, work)
        ids.append(idx)
        vals.append(m)
    exps = [jnp.exp(v - vals[0]) for v in vals]
    den = exps[0] + exps[1] + exps[2] + exps[3]
    ind = jnp.where(chosen, 1.0, 0.0)
    r_i = lax.broadcasted_iota(jnp.int32, (tm, tm), 0)
    c_i = lax.broadcasted_iota(jnp.int32, (tm, tm), 1)
    count = jnp.sum(ind, axis=0, keepdims=True)
    padded = jnp.floor((count + (MOVE_BLOCK - 1)) * (1.0 / MOVE_BLOCK)) * MOVE_BLOCK
    incl = jnp.broadcast_to(padded, (ROW_BLOCK, LANES))
    lane8 = lax.broadcasted_iota(jnp.int32, (ROW_BLOCK, LANES), 1)
    d = 1
    while d < LANES:
        incl = incl + jnp.where(lane8 >= d, pltpu.roll(incl, d, 1), 0.0)
        d *= 2
    local_start = incl[0:1] - padded
    place = _mm(jnp.where(c_i < r_i, 1.0, 0.0), ind) + local_start
    slot_out = jnp.zeros(logits.shape, jnp.int32)
    wt_out = jnp.zeros(logits.shape, F32)
    for k in range(TOP_K):
        slot = jnp.sum(jnp.where(lane == ids[k], place, 0.0), axis=-1, keepdims=True)
        slot_out = jnp.where(lane == k, slot.astype(jnp.int32), slot_out)
        wt_out = jnp.where(lane == k, exps[k] / den, wt_out)
    slot_ref[...] = slot_out
    wt_ref[...] = wt_out
    row8 = lax.broadcasted_iota(jnp.int32, (ROW_BLOCK, LANES), 0)
    table = jnp.where(row8 == 0, local_start, jnp.where(row8 == 1, padded, run_ref[0:1, :]))
    meta_ref[0] = jnp.where(row8 < 3, table, 0.0).astype(jnp.int32)
    run_ref[...] = run_ref[...] + padded


def _router(x, norm_g, modseg, w_router, b_router, layer):
    n_tok = x.shape[0]
    tm = TOKEN_TILE
    per_seg = SEG_ROWS // tm
    depth = w_router.shape[0]
    w_pad = jnp.zeros((depth, D_MODEL, LANES), F32).at[:, :, :N_EXPERTS].set(w_router)
    b_pad = jnp.full((depth, 1, LANES), NEG_BIG, F32).at[:, 0, :N_EXPERTS].set(b_router)
    full = pl.BlockSpec((tm, D_MODEL), lambda i: (i, 0))
    small = pl.BlockSpec((tm, LANES), lambda i: (i, 0))
    return pl.pallas_call(
        _router_kernel,
        grid=(n_tok // tm,),
        in_specs=[full,
                  pl.BlockSpec((1, 1, D_MODEL), lambda i: (layer, 0, 0)),
                  pl.BlockSpec((1, N_MOD, D_MODEL), lambda i: (i // per_seg, 0, 0)),
                  pl.BlockSpec((1, D_MODEL, LANES), lambda i: (layer, 0, 0)),
                  pl.BlockSpec((1, 1, LANES), lambda i: (layer, 0, 0))],
        out_specs=[full, small, small, pl.BlockSpec((1, ROW_BLOCK, LANES), lambda i: (i, 0, 0))],
        out_shape=[jax.ShapeDtypeStruct((n_tok, D_MODEL), F32),
                   jax.ShapeDtypeStruct((n_tok, LANES), jnp.int32),
                   jax.ShapeDtypeStruct((n_tok, LANES), F32),
                   jax.ShapeDtypeStruct((n_tok // tm, ROW_BLOCK, LANES), jnp.int32)],
        scratch_shapes=[pltpu.VMEM((ROW_BLOCK, LANES), F32)],
        compiler_params=_cparams(("arbitrary",), 40),
        name="router",
    )(x, norm_g.reshape(-1, 1, D_MODEL), modseg, w_pad, b_pad)


COPY_BLOCKS = (BIG_BLOCK, MOVE_BLOCK)
COPY_CAPS = (LOCAL_ROWS // COPY_BLOCKS[0],) + tuple(
    N_EXPERTS * (COPY_BLOCKS[c - 1] // COPY_BLOCKS[c] - 1) for c in range(1, len(COPY_BLOCKS)))
COPY_LIST_LEN = 1024
_COPY_AT = tuple(len(COPY_BLOCKS) + 2 * sum(COPY_CAPS[:c]) for c in range(len(COPY_BLOCKS)))
assert _COPY_AT[-1] + 2 * COPY_CAPS[-1] <= COPY_LIST_LEN


def _copy_lists(local_start, padded, sorted_start):

    def expand(count, local0, sorted0, step, cap):
        last = _prefix_sum(count)[:, None, :]
        first = last - count[:, None, :]
        f = jnp.arange(cap, dtype=jnp.int32)[None, :, None]
        mine = ((first <= f) & (f < last)).astype(jnp.int32)
        offset = step * (f - first)
        return (last[:, 0, -1:], jnp.sum(mine * (local0[:, None, :] + offset), axis=2),
                jnp.sum(mine * (sorted0[:, None, :] + offset), axis=2))

    counts, lists = [], []
    done = jnp.zeros_like(padded)
    for rows, cap in zip(COPY_BLOCKS, COPY_CAPS):
        n_copies = (padded - done) // rows
        cnt, loc, srt = expand(n_copies, local_start + done, sorted_start + done, rows, cap)
        counts.append(cnt)
        lists += [loc, srt]
        done = done + n_copies * rows
    parts = counts + lists
    used = sum(p.shape[1] for p in parts)
    parts.append(jnp.zeros((padded.shape[0], COPY_LIST_LEN - used), jnp.int32))
    return jnp.concatenate(parts, axis=1).astype(jnp.int32).reshape(-1)


def _for_each_block(list_ref, fn):
    for c, (rows, cap) in enumerate(zip(COPY_BLOCKS, COPY_CAPS)):
        def copy(j, carry, rows=rows, cap=cap, at=_COPY_AT[c]):
            fn(pl.multiple_of(list_ref[at + j], MOVE_BLOCK),
               pl.multiple_of(list_ref[at + cap + j], MOVE_BLOCK), rows)
            return carry

        lax.fori_loop(0, list_ref[c], copy, 0)


def _slot_matrix(slot_ref, values):
    col = lax.broadcasted_iota(jnp.int32, (TOKEN_TILE, LOCAL_ROWS), 1)
    out = jnp.zeros((TOKEN_TILE, LOCAL_ROWS), F32)
    for k in range(TOP_K):
        out = jnp.where(col == slot_ref[:, k:k + 1], values[k], out)
    return out.astype(BF16)


def _dispatch_kernel(meta_ref, prev_meta_ref, tail_ref, slot_ref, h_ref, xs_ref, local_ref, zero_ref,
                     sem, block_sem):
    i = pl.program_id(0)
    cur = i % 2

    @pl.when(i == 0)
    def _():
        zero_ref[...] = jnp.zeros_like(zero_ref)

        def tails(fn):
            def group(e, carry):
                def block(j, c):
                    fn(pl.multiple_of(tail_ref[e] + j * MOVE_BLOCK, MOVE_BLOCK))
                    return c
                return lax.fori_loop(0, tail_ref[N_EXPERTS + e], block, carry)
            lax.fori_loop(0, N_EXPERTS, group, 0)

        def zero_copy(row):
            return pltpu.make_async_copy(zero_ref.at[pl.ds(0, MOVE_BLOCK)],
                                         xs_ref.at[pl.ds(row, MOVE_BLOCK)], sem)

        tails(lambda row: zero_copy(row).start())
        tails(lambda row: zero_copy(row).wait())

        def unused(fn):
            def tile(j, c):
                fn(pl.multiple_of(tail_ref[2 * N_EXPERTS] + j * EXPERT_TILE, EXPERT_TILE))
                return c
            lax.fori_loop(0, tail_ref[2 * N_EXPERTS + 1], tile, 0)

        def zero_tile(row):
            return pltpu.make_async_copy(zero_ref, xs_ref.at[pl.ds(row, EXPERT_TILE)], sem)

        unused(lambda row: zero_tile(row).start())
        unused(lambda row: zero_tile(row).wait())

    onehot = _slot_matrix(slot_ref, [1.0] * TOP_K)
    local_ref[cur] = _mm_tn(onehot, h_ref[...]).astype(BF16)

    def block_copy(buf):
        return lambda local_row, sorted_row, rows: pltpu.make_async_copy(
            local_ref.at[buf, pl.ds(local_row, rows)], xs_ref.at[pl.ds(sorted_row, rows)],
            block_sem.at[buf])

    _for_each_block(meta_ref, lambda a, b, n: block_copy(cur)(a, b, n).start())

    @pl.when(i > 0)
    def _():
        _for_each_block(prev_meta_ref, lambda a, b, n: block_copy(1 - cur)(a, b, n).wait())

    @pl.when(i == pl.num_programs(0) - 1)
    def _():
        _for_each_block(meta_ref, lambda a, b, n: block_copy(cur)(a, b, n).wait())


def _dispatch(h, slots, meta, tails, n_rows):
    n_tok = h.shape[0]
    return pl.pallas_call(
        _dispatch_kernel,
        grid=(n_tok // TOKEN_TILE,),
        in_specs=[pl.BlockSpec((COPY_LIST_LEN,), lambda i: (i,), memory_space=pltpu.SMEM),
                  pl.BlockSpec((COPY_LIST_LEN,), lambda i: (jnp.maximum(i - 1, 0),), memory_space=pltpu.SMEM),
                  pl.BlockSpec((LANES,), lambda i: (0,), memory_space=pltpu.SMEM),
                  pl.BlockSpec((TOKEN_TILE, LANES), lambda i: (i, 0)),
                  pl.BlockSpec((TOKEN_TILE, D_MODEL), lambda i: (i, 0))],
        out_specs=pl.BlockSpec(memory_space=pl.ANY),
        out_shape=jax.ShapeDtypeStruct((n_rows, D_MODEL), BF16),
        scratch_shapes=[pltpu.VMEM((2, LOCAL_ROWS, D_MODEL), BF16), pltpu.VMEM((EXPERT_TILE, D_MODEL), BF16),
                        pltpu.SemaphoreType.DMA(()), pltpu.SemaphoreType.DMA((2,))],
        compiler_params=_cparams(("arbitrary",), 48),
        name="dispatch",
    )(meta, meta, tails, slots, h)


def _combine_kernel(meta_ref, next_meta_ref, slot_ref, wt_ref, ys_ref, x_ref, mod_ref, o_ref, local_ref, sem):
    i = pl.program_id(0)
    cur = i % 2

    def block_copy(buf):
        return lambda local_row, sorted_row, rows: pltpu.make_async_copy(
            ys_ref.at[pl.ds(sorted_row, rows)], local_ref.at[buf, pl.ds(local_row, rows)], sem.at[buf])

    def start_gather(table_ref, buf):
        local_ref[buf, TOKEN_TILE * TOP_K:, :] = jnp.zeros((LOCAL_ROWS - TOKEN_TILE * TOP_K, D_MODEL), BF16)
        _for_each_block(table_ref, lambda a, b, n: block_copy(buf)(a, b, n).start())

    @pl.when(i == 0)
    def _():
        start_gather(meta_ref, 0)

    @pl.when(i + 1 < pl.num_programs(0))
    def _():
        start_gather(next_meta_ref, 1 - cur)

    _for_each_block(meta_ref, lambda a, b, n: block_copy(cur)(a, b, n).wait())
    weights = _slot_matrix(slot_ref, [wt_ref[:, k:k + 1] for k in range(TOP_K)])
    acc = jnp.dot(weights, local_ref[cur], preferred_element_type=F32)
    o_ref[...] = x_ref[...] + mod_ref[0, 5:6, :] * acc


def _combine(ys, slots, meta, wts, x, modseg):
    n_tok = x.shape[0]
    per_seg = SEG_ROWS // TOKEN_TILE
    full = pl.BlockSpec((TOKEN_TILE, D_MODEL), lambda i: (i, 0))
    small = pl.BlockSpec((TOKEN_TILE, LANES), lambda i: (i, 0))
    return pl.pallas_call(
        _combine_kernel,
        grid=(n_tok // TOKEN_TILE,),
        in_specs=[pl.BlockSpec((COPY_LIST_LEN,), lambda i: (i,), memory_space=pltpu.SMEM),
                  pl.BlockSpec((COPY_LIST_LEN,), lambda i: (jnp.minimum(i + 1, n_tok // TOKEN_TILE - 1),),
                               memory_space=pltpu.SMEM),
                  small, small,
                  pl.BlockSpec(memory_space=pl.ANY),
                  full,
                  pl.BlockSpec((1, N_MOD, D_MODEL), lambda i: (i // per_seg, 0, 0))],
        out_specs=full,
        out_shape=jax.ShapeDtypeStruct((n_tok, D_MODEL), F32),
        scratch_shapes=[pltpu.VMEM((2, LOCAL_ROWS, D_MODEL), BF16), pltpu.SemaphoreType.DMA((2,))],
        compiler_params=_cparams(("arbitrary",), 56),
        name="combine",
    )(meta, meta, slots, wts, ys, x, modseg)


def _expert_kernel(te_ref, first_ref, slot_ref, next_ref, na_ref, x_ref, wgu_hbm, bgu_ref, wd_hbm, bd_ref,
                   o_ref, wgu_f, wd_f, wgu_s, wd_s, sem, *, layer):
    i = pl.program_id(0)
    active = i < na_ref[0]

    def fetch(e, s):
        return (pltpu.make_async_copy(wgu_hbm.at[layer, e], wgu_f.at[s], sem.at[0, s]),
                pltpu.make_async_copy(wd_hbm.at[layer, e], wd_f.at[s], sem.at[1, s]))

    @pl.when(jnp.logical_and(i == 0, first_ref[0] == 1))
    def _():
        for cp in fetch(te_ref[0], 0):
            cp.start()

    @pl.when(first_ref[i] == 1)
    def _():
        s = slot_ref[i]
        for cp in fetch(te_ref[i], s):
            cp.wait()
        wgu_s[...] = wgu_f[s].astype(BF16)
        wd_s[...] = wd_f[s].astype(BF16)

        @pl.when(next_ref[i] >= 0)
        def _():
            for cp in fetch(next_ref[i], 1 - s):
                cp.start()

    @pl.when(active)
    def _():
        gu = jnp.dot(x_ref[...], wgu_s[...], preferred_element_type=F32) + bgu_ref[0, 0]
        gate = jnp.minimum(gu[:, :D_FF], SWIGLU_LIMIT)
        up = jnp.clip(gu[:, D_FF:], -SWIGLU_LIMIT, SWIGLU_LIMIT)
        act = gate * _sigmoid(SWIGLU_ALPHA * gate) * (up + 1.0)
        o_ref[...] = (jnp.dot(act.astype(BF16), wd_s[...], preferred_element_type=F32) + bd_ref[0, 0]).astype(BF16)

    @pl.when(jnp.logical_not(active))
    def _():
        o_ref[...] = jnp.zeros_like(o_ref)


def _experts(xs, tile_expert, ends, n_active, w_gu, b_gu, w_down, b_down, layer):
    n_rows = xs.shape[0]
    depth = w_gu.shape[0]
    tm = EXPERT_TILE
    n_tiles = n_rows // tm
    tile = jnp.arange(n_tiles, dtype=jnp.int32)
    active = tile < n_active[0]
    first = (active & ((tile == 0) | (tile_expert != jnp.roll(tile_expert, 1)))).astype(jnp.int32)
    slot = (_prefix_sum(first) - 1) % 2
    next_tile = ends[tile_expert] // tm
    next_expert = jnp.where(next_tile < n_active[0],
                            tile_expert[jnp.minimum(next_tile, n_tiles - 1)], -1).astype(jnp.int32)

    def tiled(shape, index):
        return pl.BlockSpec(shape, lambda i, te, fi, sl, nx, na: index(i, te, na))

    grid_spec = pltpu.PrefetchScalarGridSpec(
        num_scalar_prefetch=5,
        grid=(n_tiles,),
        in_specs=[tiled((tm, D_MODEL), lambda i, te, na: (jnp.where(i < na[0], i, 0), 0)),
                  pl.BlockSpec(memory_space=pl.ANY),
                  tiled((1, 1, 1, 2 * D_FF), lambda i, te, na: (layer, te[i], 0, 0)),
                  pl.BlockSpec(memory_space=pl.ANY),
                  tiled((1, 1, 1, D_MODEL), lambda i, te, na: (layer, te[i], 0, 0))],
        out_specs=tiled((tm, D_MODEL), lambda i, te, na: (i, 0)),
        scratch_shapes=[pltpu.VMEM((2, D_MODEL, 2 * D_FF), F32), pltpu.VMEM((2, D_FF, D_MODEL), F32),
                        pltpu.VMEM((D_MODEL, 2 * D_FF), BF16), pltpu.VMEM((D_FF, D_MODEL), BF16),
                        pltpu.SemaphoreType.DMA((2, 2))],
    )
    return pl.pallas_call(
        functools.partial(_expert_kernel, layer=layer),
        grid_spec=grid_spec,
        out_shape=jax.ShapeDtypeStruct((n_rows, D_MODEL), BF16),
        compiler_params=_cparams(("arbitrary",), 56),
        name="experts",
    )(tile_expert, first, slot.astype(jnp.int32), next_expert, n_active, xs, w_gu,
      b_gu.reshape(depth, N_EXPERTS, 1, 2 * D_FF), w_down, b_down.reshape(depth, N_EXPERTS, 1, D_MODEL))


def _moe(x, norm_g, modseg, w_router, b_router, w_gu, b_gu, w_down, b_down, layer):
    n_tok = x.shape[0]
    h, slots, wts, table = _router(x, norm_g, modseg, w_router, b_router, layer)
    tm = EXPERT_TILE
    n_tiles = n_tok // TOKEN_TILE
    n_rows = n_tok * TOP_K + n_tiles * N_EXPERTS * MOVE_BLOCK + N_EXPERTS * tm
    n_rows = -(-n_rows // tm) * tm
    local_start = table[:, 0, :N_EXPERTS]
    padded = table[:, 1, :N_EXPERTS]
    earlier = table[:, 2, :N_EXPERTS]
    used = earlier[-1] + padded[-1]
    size = ((used + tm - 1) // tm) * tm
    ends = _prefix_sum(size)
    starts = ends - size
    sorted_start = starts[None, :] + earlier
    meta = _copy_lists(local_start, padded, sorted_start)
    tails = jnp.concatenate([starts + used, (size - used) // MOVE_BLOCK,
                             ends[-1:], (n_rows - ends[-1:]) // tm,
                             jnp.zeros((LANES - 2 * N_EXPERTS - 2,), jnp.int32)])
    tile_row0 = jnp.arange(n_rows // tm, dtype=jnp.int32) * tm
    tile_expert = jnp.minimum(jnp.sum((ends[None, :] <= tile_row0[:, None]).astype(jnp.int32), axis=1),
                              N_EXPERTS - 1)
    n_active = (ends[-1:] // tm).astype(jnp.int32)
    xs = _dispatch(h, slots, meta, tails, n_rows)
    ys = _experts(xs, tile_expert, ends, n_active, w_gu, b_gu, w_down, b_down, layer)
    return _combine(ys, slots, meta, wts, x, modseg)


def _rope_tables(n_tok):
    rows = n_tok // GRID_W
    row = jnp.repeat(jnp.arange(rows, dtype=F32), GRID_W)
    col = jnp.tile(jnp.arange(GRID_W, dtype=F32), rows)
    quarter = HEAD_DIM // 4
    inv_freq = ROPE_BASE ** (-jnp.arange(quarter, dtype=F32) / quarter)
    ang_r = row[:, None] * inv_freq
    ang_c = col[:, None] * inv_freq
    ang = jnp.concatenate([ang_r, ang_r, ang_c, ang_c], axis=-1)
    cos, sin = jnp.cos(ang), jnp.sin(ang)
    first = (jnp.arange(HEAD_DIM) % (2 * quarter)) < quarter
    sin_a = jnp.where(first, -sin, 0.0)
    sin_b = jnp.where(first, 0.0, sin)
    return tuple(jnp.tile(t, (1, LANES // HEAD_DIM)) for t in (cos, sin_a, sin_b))


def kernel(x_prompt, x_sample, c, cache_k, cache_v, state_rglru, state_hgrn, c_ctx, w_mod, b_mod, norm1_g, norm2_g, w_in, q_norm_g, k_norm_g, conv_w, conv_b, rg_wa, rg_ba, rg_wx, rg_bx, rg_lambda, hgrn_lb_logits, hgrn_norm_g, w_attn_o, w_rnn_o, w_hgrn_o, w_out, w_router, b_router, w_gu, b_gu, w_down, b_down):
    depth = w_mod.shape[0]
    n_p, t_p = x_prompt.shape[0], x_prompt.shape[1]
    n_s, t_s = x_sample.shape[0], x_sample.shape[1]
    tok_p = n_p * t_p
    tok_s = n_s * t_s
    assert tok_p % SEG_ROWS == 0 and t_s == SEG_ROWS and SEG_ROWS % t_p == 0

    x = jnp.concatenate([x_prompt.reshape(tok_p, D_MODEL), x_sample.reshape(tok_s, D_MODEL)], axis=0)

    lb_all = jnp.moveaxis(_prefix_sum(jnp.moveaxis(jax.nn.softmax(hgrn_lb_logits.astype(F32), axis=0), 0, -1)),
                          -1, 0)
    lb_all = lb_all - lb_all[:1]
    rope = _rope_tables(t_s)

    assert 1 + n_s <= ROW_BLOCK
    cond8 = jnp.zeros((ROW_BLOCK, D_MODEL), F32).at[0].set(c_ctx).at[1:1 + n_s].set(c)
    mod = _modulation(cond8, w_mod, b_mod)
    seg_ids = jnp.asarray([0] * (tok_p // SEG_ROWS) + [1 + b for b in range(n_s)], jnp.int32)
    modsegs = mod[:, seg_ids].reshape(depth, seg_ids.shape[0], N_MOD, D_MODEL)

    blocks = jnp.stack([rg_wa, rg_wx], axis=2)
    eye = jnp.eye(RG_BLOCKS, dtype=blocks.dtype)
    w_gate = jnp.einsum("ldgnab,nm->lnadgmb", blocks, eye).reshape(depth, D_RNN, 4 * D_RNN)
    b_gate = jnp.stack([rg_ba, rg_bx], axis=2).reshape(depth, 1, 4 * D_RNN)

    new_k, new_v, new_hr, new_s = [], [], [], []
    for l in range(depth):
        modseg = modsegs[l]
        z = _input_projection(x, norm1_g, modseg, w_in, l)
        attn_p, k_l, v_l = _attention(z, q_norm_g, k_norm_g, l, n_p, t_p, 0)
        (attn_s,) = _attention(z, q_norm_g, k_norm_g, l, n_s, t_s, tok_p, rope=rope,
                               cache=(cache_k, cache_v))
        rnn_p, hr_l = _rglru(z, conv_w, conv_b, w_gate, b_gate, rg_lambda, l, n_p, t_p, 0)
        (rnn_s,) = _rglru(z, conv_w, conv_b, w_gate, b_gate, rg_lambda, l, n_s, t_s, tok_p,
                          state=state_rglru)
        hg_p, s_l = _hgrn(z, lb_all, hgrn_norm_g, l, n_p, t_p, 0)
        (hg_s,) = _hgrn(z, lb_all, hgrn_norm_g, l, n_s, t_s, tok_p, state=state_hgrn)
        x = _merge_project(((attn_p, attn_s), (rnn_p, rnn_s), (hg_p, hg_s)), z, x, modseg,
                           w_attn_o, w_rnn_o, w_hgrn_o, w_out, l)
        x = _moe(x, norm2_g, modseg, w_router, b_router, w_gu, b_gu, w_down, b_down, l)
        new_k.append(k_l.reshape(n_p, t_p, N_KV_HEADS, HEAD_DIM))
        new_v.append(v_l.reshape(n_p, t_p, N_KV_HEADS, HEAD_DIM))
        new_hr.append(hr_l)
        new_s.append(s_l)

    y_prompt = x[:tok_p].reshape(n_p, t_p, D_MODEL)
    y_sample = x[tok_p:].reshape(n_s, t_s, D_MODEL)
    return (y_prompt, y_sample, jnp.stack(new_k, axis=1), jnp.stack(new_v, axis=1),
            jnp.stack(new_hr, axis=1), jnp.stack(new_s, axis=1))
```

```python
import functools

import jax
import jax.numpy as jnp
from jax import lax
from jax.experimental import pallas as pl
from jax.experimental.pallas import tpu as pltpu

F32 = jnp.float32
BF16 = jnp.bfloat16

D_MODEL = 1024
GRID_W = 64
HEAD_DIM = 64
N_HEADS = 8
N_KV_HEADS = 2
KV_GROUP = N_HEADS // N_KV_HEADS
ATTN_W = N_HEADS * HEAD_DIM
KV_W = N_KV_HEADS * HEAD_DIM
ROPE_BASE = 10000.0
D_RNN = 512
RG_BLOCKS = 8
RG_BW = D_RNN // RG_BLOCKS
RG_C = 8.0
HG_HEADS = 8
HG_DK = 64
HG_DV = 64
HG_W = HG_HEADS * HG_DK
N_PAIRS = HG_W // 128
N_EXPERTS = 32
TOP_K = 4
D_FF = D_MODEL
SWIGLU_LIMIT = 7.0
SWIGLU_ALPHA = 1.702
N_MOD = 6
EPS = 1e-6
IN_W = ATTN_W + 2 * KV_W + 2 * D_RNN + 5 * HG_W + 3 * D_MODEL

LANES = 128
SEG_ROWS = 1024
COL_TILE = 256
HG_CHUNK = 256
Q_TILE = 256
EXPERT_TILE = 384
BIG_BLOCK = 64
TOKEN_TILE = 512
ROW_BLOCK = 8
LOCAL_ROWS = TOKEN_TILE * TOP_K + N_EXPERTS * ROW_BLOCK
NEG_BIG = -1e30
LOG2_E = 1.4426950408889634

_SRC_TILES = tuple(range(17, 29)) + tuple(range(7, 17)) + tuple(range(3, 7)) + (0, 1, 2)
Z_GL = 0
Z_HG = 3 * D_MODEL
Z_RG = Z_HG + 5 * HG_W
Z_Q = Z_RG + 2 * D_RNN
Z_K = Z_Q + ATTN_W
Z_V = Z_K + KV_W
Z_W = -(-IN_W // (2 * COL_TILE)) * (2 * COL_TILE)


def _cparams(sem, vmem_mb):
    return pltpu.CompilerParams(dimension_semantics=sem, vmem_limit_bytes=vmem_mb * 1024 * 1024)


def _prefix_sum(x):
    n = x.shape[-1]
    upto = (jnp.arange(n)[None, :] <= jnp.arange(n)[:, None]).astype(x.dtype)
    return jnp.sum(x[..., None, :] * upto, axis=-1)


def _mm(a, b):
    return jnp.dot(a.astype(BF16), b.astype(BF16), preferred_element_type=F32)


def _mm_nt(a, b):
    return lax.dot_general(a.astype(BF16), b.astype(BF16), (((1,), (1,)), ((), ())),
                           preferred_element_type=F32)


def _mm_tn(a, b):
    return lax.dot_general(a.astype(BF16), b.astype(BF16), (((0,), (0,)), ((), ())),
                           preferred_element_type=F32)


def _sigmoid(x):
    return 1.0 / (1.0 + jnp.exp(-x))


def _softplus(x):
    return jnp.maximum(x, 0.0) + jnp.log1p(jnp.exp(-jnp.abs(x)))


def _modulated_norm(x, g, scale, shift):
    y = x * lax.rsqrt(jnp.mean(x * x, axis=-1, keepdims=True) + EPS)
    return (y * g) * (1.0 + scale) + shift


def _mod_kernel(c_ref, w_ref, b_ref, o_ref):
    c = c_ref[...]
    s = c * _sigmoid(c)
    o_ref[0] = _mm(s, w_ref[0]) + b_ref[0]


def _modulation(cond8, w_mod, b_mod):
    depth = w_mod.shape[0]
    tn = 1024
    return pl.pallas_call(
        _mod_kernel,
        grid=(depth, N_MOD * D_MODEL // tn),
        in_specs=[pl.BlockSpec((ROW_BLOCK, D_MODEL), lambda l, j: (0, 0)),
                  pl.BlockSpec((1, D_MODEL, tn), lambda l, j: (l, 0, j)),
                  pl.BlockSpec((1, 1, tn), lambda l, j: (l, 0, j))],
        out_specs=pl.BlockSpec((1, ROW_BLOCK, tn), lambda l, j: (l, 0, j)),
        out_shape=jax.ShapeDtypeStruct((depth, ROW_BLOCK, N_MOD * D_MODEL), F32),
        compiler_params=_cparams(("arbitrary", "arbitrary"), 32),
        name="modulation",
    )(cond8, w_mod, b_mod.reshape(depth, 1, N_MOD * D_MODEL))


def _inproj_kernel(src_ref, x_ref, g_ref, mod_ref, wa_ref, wb_ref, o_ref, h_ref, w_s):
    j = pl.program_id(1)

    @pl.when(j == 0)
    def _():
        for s in range(x_ref.shape[0] // SEG_ROWS):
            rows = slice(s * SEG_ROWS, (s + 1) * SEG_ROWS)
            h = _modulated_norm(x_ref[rows, :], g_ref[0], mod_ref[s, 1:2, :], mod_ref[s, 0:1, :])
            h_ref[rows, :] = h.astype(BF16)

    @pl.when(pl.program_id(0) == 0)
    def _():
        w_s[j, :, 0:COL_TILE] = wa_ref[0].astype(BF16)
        w_s[j, :, COL_TILE:2 * COL_TILE] = wb_ref[0].astype(BF16)

    o_ref[...] = jnp.dot(h_ref[...], w_s[j], preferred_element_type=F32)


def _input_projection(x, norm_g, modseg, w_in, layer):
    n_tok = x.shape[0]
    segs = 2
    tm = segs * SEG_ROWS
    n_col = Z_W // (2 * COL_TILE)
    src = jnp.asarray(_SRC_TILES + (_SRC_TILES[-1],) * (Z_W // COL_TILE - len(_SRC_TILES)), jnp.int32)

    def wspec(k):
        return pl.BlockSpec((1, D_MODEL, COL_TILE),
                            lambda i, j, s: (layer, 0, s[2 * jnp.where(i == 0, j, n_col - 1) + k]))

    grid_spec = pltpu.PrefetchScalarGridSpec(
        num_scalar_prefetch=1,
        grid=(n_tok // tm, n_col),
        in_specs=[pl.BlockSpec((tm, D_MODEL), lambda i, j, s: (i, 0)),
                  pl.BlockSpec((1, 1, D_MODEL), lambda i, j, s: (layer, 0, 0)),
                  pl.BlockSpec((segs, N_MOD, D_MODEL), lambda i, j, s: (i, 0, 0)),
                  wspec(0), wspec(1)],
        out_specs=pl.BlockSpec((tm, 2 * COL_TILE), lambda i, j, s: (i, j)),
        scratch_shapes=[pltpu.VMEM((tm, D_MODEL), BF16),
                        pltpu.VMEM((n_col, D_MODEL, 2 * COL_TILE), BF16)],
    )
    return pl.pallas_call(
        _inproj_kernel,
        grid_spec=grid_spec,
        out_shape=jax.ShapeDtypeStruct((n_tok, Z_W), F32),
        compiler_params=_cparams(("arbitrary", "arbitrary"), 56),
        name="input_projection",
    )(src, x, norm_g.reshape(-1, 1, D_MODEL), modseg, w_in, w_in)


def _head_rms(x, g):
    lane = lax.broadcasted_iota(jnp.int32, x.shape, 1)
    lo = lane < HEAD_DIM
    xx = x * x
    s_lo = jnp.sum(jnp.where(lo, xx, 0.0), axis=-1, keepdims=True)
    s_hi = jnp.sum(jnp.where(lo, 0.0, xx), axis=-1, keepdims=True)
    inv = jnp.where(lo, lax.rsqrt(s_lo * (1.0 / HEAD_DIM) + EPS),
                    lax.rsqrt(s_hi * (1.0 / HEAD_DIM) + EPS))
    return x * inv * g


def _rope(x, cos, sin_a, sin_b):
    q = HEAD_DIM // 4
    return x * cos + pltpu.roll(x, LANES - q, 1) * sin_a + pltpu.roll(x, q, 1) * sin_b


def _attn_kernel(*refs, t_len, use_ctx):
    if use_ctx:
        (q_ref, k_ref, v_ref, qg_ref, kg_ref, cos_q_ref, sa_q_ref, sb_q_ref,
         cos_k_ref, sa_k_ref, sb_k_ref, ck_ref, cv_ref, o_ref, kall_ref, vall_ref) = refs
    else:
        (q_ref, k_ref, v_ref, qg_ref, kg_ref, o_ref, ko_ref, vo_ref, kall_ref, vall_ref) = refs

    @pl.when(pl.program_id(1) == 0)
    def _():
        k = _head_rms(k_ref[...], kg_ref[...])
        if use_ctx:
            k = _rope(k, cos_k_ref[...], sa_k_ref[...], sb_k_ref[...])
            kall_ref[0:t_len, :] = k.astype(BF16)
            vall_ref[0:t_len, :] = v_ref[...].astype(BF16)
            kall_ref[t_len:, :] = ck_ref[0, 0].astype(BF16)
            vall_ref[t_len:, :] = cv_ref[0, 0].astype(BF16)
        else:
            ko_ref[0] = k
            vo_ref[0] = v_ref[...]
            kall_ref[...] = k.astype(BF16)
            vall_ref[...] = v_ref[...].astype(BF16)

    heads = []
    for c in range(ATTN_W // LANES):
        qc = _head_rms(q_ref[:, c * LANES:(c + 1) * LANES], qg_ref[...])
        if use_ctx:
            qc = _rope(qc, cos_q_ref[...], sa_q_ref[...], sb_q_ref[...])
        qc = (qc * (HEAD_DIM ** -0.5)).astype(BF16)
        heads.append(qc[:, :HEAD_DIM])
        heads.append(qc[:, HEAD_DIM:])
    tq = q_ref.shape[0]
    for g in range(N_KV_HEADS):
        qs = jnp.concatenate(heads[g * KV_GROUP:(g + 1) * KV_GROUP], axis=0)
        kh = kall_ref[:, g * HEAD_DIM:(g + 1) * HEAD_DIM]
        vh = vall_ref[:, g * HEAD_DIM:(g + 1) * HEAD_DIM]
        s = _mm_nt(qs, kh)
        m = jnp.max(s, axis=-1, keepdims=True)
        p = jnp.exp(s - m)
        den = jnp.sum(p, axis=-1, keepdims=True)
        o = _mm(p, vh) / den
        for j in range(KV_GROUP):
            hh = g * KV_GROUP + j
            o_ref[:, hh * HEAD_DIM:(hh + 1) * HEAD_DIM] = o[j * tq:(j + 1) * tq].astype(BF16)


def _attention(z, q_g, k_g, layer, n_batch, t_len, row0, rope=None, cache=None):
    use_ctx = cache is not None
    nq = t_len // Q_TILE
    rb0 = row0 // t_len
    qb0 = row0 // Q_TILE
    qg = jnp.tile(q_g[layer], 2).reshape(1, LANES)
    kg = jnp.tile(k_g[layer], 2).reshape(1, LANES)
    n_out_tok = n_batch * t_len
    vec = pl.BlockSpec((1, LANES), lambda b, i: (0, 0))
    in_specs = [pl.BlockSpec((Q_TILE, ATTN_W), lambda b, i: (qb0 + b * nq + i, Z_Q // ATTN_W)),
                pl.BlockSpec((t_len, KV_W), lambda b, i: (rb0 + b, Z_K // KV_W)),
                pl.BlockSpec((t_len, KV_W), lambda b, i: (rb0 + b, Z_V // KV_W)),
                vec, vec]
    args = [z, z, z, qg, kg]
    out_specs = [pl.BlockSpec((Q_TILE, ATTN_W), lambda b, i: (b * nq + i, 0))]
    out_shape = [jax.ShapeDtypeStruct((n_out_tok, ATTN_W), BF16)]
    t_keys = t_len
    if use_ctx:
        cos, sin_a, sin_b = rope
        cache_k, cache_v = cache
        past = cache_k.shape[2]
        t_keys = t_len + past
        tq_spec = pl.BlockSpec((Q_TILE, LANES), lambda b, i: (i, 0))
        tk_spec = pl.BlockSpec((t_len, LANES), lambda b, i: (0, 0))
        c_spec = pl.BlockSpec((1, 1, past, KV_W), lambda b, i: (b, layer, 0, 0))
        in_specs += [tq_spec, tq_spec, tq_spec, tk_spec, tk_spec, tk_spec, c_spec, c_spec]
        args += [cos, sin_a, sin_b, cos, sin_a, sin_b,
                 cache_k.reshape(cache_k.shape[0], cache_k.shape[1], past, KV_W),
                 cache_v.reshape(cache_v.shape[0], cache_v.shape[1], past, KV_W)]
    else:
        kv_spec = pl.BlockSpec((1, t_len, KV_W), lambda b, i: (b, 0, 0))
        out_specs += [kv_spec, kv_spec]
        out_shape += [jax.ShapeDtypeStruct((n_batch, t_len, KV_W), F32)] * 2
    return pl.pallas_call(
        functools.partial(_attn_kernel, t_len=t_len, use_ctx=use_ctx),
        grid=(n_batch, nq),
        in_specs=in_specs,
        out_specs=out_specs,
        out_shape=out_shape,
        scratch_shapes=[pltpu.VMEM((t_keys, KV_W), BF16), pltpu.VMEM((t_keys, KV_W), BF16)],
        compiler_params=_cparams(("arbitrary", "arbitrary"), 48),
        name="attention_ctx" if use_ctx else "attention",
    )(*args)


def _shift_rows(x, d):
    n = x.shape[0]
    row = lax.broadcasted_iota(jnp.int32, x.shape, 0)
    y = pltpu.roll(x, d % n, 0)
    if d > 0:
        return jnp.where(row >= d, y, 0.0)
    return jnp.where(row < n + d, y, 0.0)


def _linear_scan(a, u, reverse):
    n, w = a.shape
    row = lax.broadcasted_iota(jnp.int32, a.shape, 0)

    def shifted(x, d, fill):
        if d % ROW_BLOCK == 0:
            pad = jnp.full((d, w), fill, x.dtype)
            return jnp.concatenate([x[d:], pad] if reverse else [pad, x[:n - d]], axis=0)
        if reverse:
            return jnp.where(row < n - d, pltpu.roll(x, n - d, 0), fill)
        return jnp.where(row >= d, pltpu.roll(x, d, 0), fill)

    d = 1
    while d < n:
        u = a * shifted(u, d, 0.0) + u
        a = a * shifted(a, d, 1.0)
        d *= 2
    return a, u


def _rglru_kernel(*refs, use_ctx):
    if use_ctx:
        (x_ref, y_ref, cw_ref, cb_ref, wa_ref, wx_ref, bg_ref, lam_ref, h0_ref, o_ref, wg_s) = refs
    else:
        (x_ref, y_ref, cw_ref, cb_ref, wa_ref, wx_ref, bg_ref, lam_ref, o_ref, last_ref, wg_s) = refs

    @pl.when(pl.program_id(0) == 0)
    def _():
        wg_s[...] = jnp.zeros_like(wg_s)
        for d in range(2):
            for g, w_ref in enumerate((wa_ref, wx_ref)):
                for nb in range(RG_BLOCKS):
                    col0 = (2 * d + g) * D_RNN + nb * RG_BW
                    wg_s[nb * RG_BW:(nb + 1) * RG_BW, col0:col0 + RG_BW] = w_ref[0, d, nb].astype(BF16)

    x = x_ref[...]
    cw = cw_ref[0]
    xr = cb_ref[0] + _shift_rows(x, 1) * cw[0:1] + x * cw[1:2] \
        + _shift_rows(x, -1) * cw[2:3] + _shift_rows(x, -2) * cw[3:4]
    gates = _sigmoid(jnp.dot(xr.astype(BF16), wg_s[...], preferred_element_type=F32) + bg_ref[0])
    n = x.shape[0]
    total = None
    lasts = []
    for d in range(2):
        r = gates[:, (2 * d) * D_RNN:(2 * d + 1) * D_RNN]
        i = gates[:, (2 * d + 1) * D_RNN:(2 * d + 2) * D_RNN]
        log_a = (-RG_C * _softplus(-lam_ref[0, d:d + 1, :])) * r
        a = jnp.exp(log_a)
        u = jnp.sqrt(1.0 - a * a) * (i * xr)
        a_cum, h = _linear_scan(a, u, reverse=(d == 1))
        if use_ctx:
            h = h + a_cum * h0_ref[0, 0, d:d + 1, :]
        else:
            lasts.append(h[0:1] if d == 1 else h[n - 1:n])
        total = h if total is None else total + h
    o_ref[...] = (jax.nn.gelu(y_ref[...]) * total).astype(BF16)
    if not use_ctx:
        last_ref[0] = jnp.concatenate(lasts, axis=0)


def _rglru(z, conv_w, conv_b, rg_wa, rg_wx, b_gate, lam, layer, n_batch, t_len, row0, state=None):
    use_ctx = state is not None
    rb0 = row0 // t_len
    blocks = pl.BlockSpec((1, 2, RG_BLOCKS, RG_BW, RG_BW), lambda b: (layer, 0, 0, 0, 0))
    in_specs = [pl.BlockSpec((t_len, D_RNN), lambda b: (rb0 + b, Z_RG // D_RNN)),
                pl.BlockSpec((t_len, D_RNN), lambda b: (rb0 + b, Z_RG // D_RNN + 1)),
                pl.BlockSpec((1, 4, D_RNN), lambda b: (layer, 0, 0)),
                pl.BlockSpec((1, 1, D_RNN), lambda b: (layer, 0, 0)),
                blocks, blocks,
                pl.BlockSpec((1, 1, 4 * D_RNN), lambda b: (layer, 0, 0)),
                pl.BlockSpec((1, 2, D_RNN), lambda b: (layer, 0, 0))]
    args = [z, z, conv_w, conv_b.reshape(-1, 1, D_RNN), rg_wa, rg_wx, b_gate, lam]
    out_specs = [pl.BlockSpec((t_len, D_RNN), lambda b: (b, 0))]
    out_shape = [jax.ShapeDtypeStruct((n_batch * t_len, D_RNN), BF16)]
    if use_ctx:
        in_specs.append(pl.BlockSpec((1, 1, 2, D_RNN), lambda b: (b, layer, 0, 0)))
        args.append(state)
    else:
        out_specs.append(pl.BlockSpec((1, 2, D_RNN), lambda b: (b, 0, 0)))
        out_shape.append(jax.ShapeDtypeStruct((n_batch, 2, D_RNN), F32))
    return pl.pallas_call(
        functools.partial(_rglru_kernel, use_ctx=use_ctx),
        grid=(n_batch,),
        in_specs=in_specs,
        out_specs=out_specs,
        out_shape=out_shape,
        scratch_shapes=[pltpu.VMEM((D_RNN, 4 * D_RNN), BF16)],
        compiler_params=_cparams(("arbitrary",), 56),
        name="rglru_ctx" if use_ctx else "rglru",
    )(*args)


def _cumsum_rows(g, reverse):
    n, w = g.shape
    m = n // ROW_BLOCK

    def scan(x, size, pos):
        d = 1
        while d < size:
            if reverse:
                x = x + jnp.where(pos < size - d, pltpu.roll(x, x.shape[0] - d, 0), 0.0)
            else:
                x = x + jnp.where(pos >= d, pltpu.roll(x, d, 0), 0.0)
            d *= 2
        return x

    g = scan(g, ROW_BLOCK, lax.broadcasted_iota(jnp.int32, g.shape, 0) & (ROW_BLOCK - 1))
    g3 = g.reshape(m, ROW_BLOCK, w)
    total = g3[:, 0, :] if reverse else g3[:, ROW_BLOCK - 1, :]
    before = scan(total, m, lax.broadcasted_iota(jnp.int32, total.shape, 0)) - total
    return (g3 + before[:, None, :]).reshape(n, w)


def _block_row_fn(gc, reverse):
    n, w = gc.shape
    g3 = gc.reshape(n // ROW_BLOCK, ROW_BLOCK, w)
    edge_row = 0 if reverse else ROW_BLOCK - 1
    edge = jnp.broadcast_to(g3[:, edge_row:edge_row + 1, :], g3.shape)
    sub = lax.broadcasted_iota(jnp.int32, g3.shape, 1)

    def at(blk):
        r = blk // 2 if reverse else blk // 2 - 1
        if blk > ROW_BLOCK:
            groups = blk // ROW_BLOCK
            e4 = edge.reshape(n // blk, groups, ROW_BLOCK, w)
            pick = r // ROW_BLOCK
            return jnp.broadcast_to(e4[:, pick:pick + 1], e4.shape).reshape(n, w)
        out = jnp.broadcast_to(g3[:, r:r + 1, :], g3.shape)
        for b0 in range(blk, ROW_BLOCK, blk):
            out = jnp.where(sub >= b0, jnp.broadcast_to(g3[:, b0 + r:b0 + r + 1, :], g3.shape), out)
        return out.reshape(n, w)

    return at


def _first_head(shape):
    lane = lax.broadcasted_iota(jnp.int32, shape, len(shape) - 1)
    return (lane & (LANES - 1)) < HG_DK


def _pair(x, p):
    return x[:, p * LANES:(p + 1) * LANES]


def _hgrn_gates(f_logit, lb):
    y = jnp.exp(-jnp.abs(f_logit))
    log_sig = jnp.minimum(f_logit, 0.0) - jnp.log(1.0 + y)
    a = jnp.log(lb)
    b = jnp.log1p(-lb) + log_sig
    log_f = jnp.maximum(a, b) + jnp.log(1.0 + jnp.exp(-jnp.abs(a - b)))
    k = (1.0 - lb) * (jnp.where(f_logit >= 0.0, y, 1.0) / (1.0 + y))
    return log_f * LOG2_E, k


def _hgrn_chunk_state(gc, k, v, reverse):
    n = gc.shape[0]
    g_tot = gc[0:1] if reverse else gc[n - 1:n]
    k_end = (k * jnp.exp2(jnp.minimum(g_tot - gc, 0.0))).astype(BF16)
    vb = v.astype(BF16)
    same_head = (_first_head((LANES, LANES))
                 == (lax.broadcasted_iota(jnp.int32, (LANES, LANES), 0) < HG_DK))
    ds = [jnp.where(same_head, _mm_tn(_pair(k_end, p), _pair(vb, p)), 0.0) for p in range(N_PAIRS)]
    return ds, jnp.exp2(g_tot)


def _decay_state(s, e_row):
    n = s.shape[0]
    eye = (lax.broadcasted_iota(jnp.int32, (n, n), 0) == lax.broadcasted_iota(jnp.int32, (n, n), 1))
    col = jnp.sum(jnp.where(eye, jnp.broadcast_to(e_row, (n, n)), 0.0), axis=1, keepdims=True)
    return s * col


def _hgrn_intra_scores(q, ks, gcs, txs):
    n = q.shape[0]
    row = lax.broadcasted_iota(jnp.int32, q.shape, 0)
    first = _first_head(q.shape)
    boundary = [_block_row_fn(gcs[0], False), _block_row_fn(gcs[1], True)]
    scores = [None] * HG_HEADS
    w = 1
    while w < n:
        blk = 2 * w
        second = (row & w) != 0
        e_f = jnp.exp2(-jnp.abs(gcs[0] - boundary[0](blk)))
        e_b = jnp.exp2(-jnp.abs(gcs[1] - boundary[1](blk)))
        q_f = jnp.where(second, q * e_f, 0.0).astype(BF16)
        q_b = jnp.where(second, 0.0, q * e_b).astype(BF16)
        k_f = jnp.where(second, 0.0, ks[0] * e_f)
        k_b = jnp.where(second, ks[1] * e_b, 0.0)
        k_heads = [(jnp.where(first, k_f, 0.0).astype(BF16), jnp.where(first, k_b, 0.0).astype(BF16)),
                   (jnp.where(first, 0.0, k_f).astype(BF16), jnp.where(first, 0.0, k_b).astype(BF16))]
        sel = (txs >= w) & (txs < blk)
        for p in range(N_PAIRS):
            qc = jnp.concatenate([_pair(q_f, p), _pair(q_b, p)], axis=1)
            for j in range(2):
                kc = jnp.concatenate([_pair(k_heads[j][0], p), _pair(k_heads[j][1], p)], axis=1)
                s = jnp.where(sel, _mm_nt(qc, kc), 0.0)
                h = 2 * p + j
                scores[h] = s if scores[h] is None else scores[h] + s
        w = blk
    return scores


def _hgrn_kernel(*refs, t_len, use_ctx):
    if use_ctx:
        (q_ref, ff_ref, fb_ref, i_ref, og_ref, lb_ref, ng_ref, s0_ref, o_ref, sin_ref, gk_ref) = refs
    else:
        (q_ref, ff_ref, fb_ref, i_ref, og_ref, lb_ref, ng_ref, o_ref, sl_ref) = refs
    c = HG_CHUNK
    n_chunks = t_len // c
    f_refs = (ff_ref, fb_ref)

    def load_dir(d, r0):
        log_f, k = _hgrn_gates(f_refs[d][pl.ds(r0, c), :], lb_ref[0, d:d + 1, :])
        return _cumsum_rows(log_f, reverse=(d == 1)), k

    if use_ctx:
        zero = jnp.zeros((HG_DK, HG_DV), F32)
        for d in range(2):
            states = []
            for p in range(N_PAIRS):
                top = jnp.concatenate([s0_ref[0, 0, d, 2 * p], zero], axis=1)
                bot = jnp.concatenate([zero, s0_ref[0, 0, d, 2 * p + 1]], axis=1)
                states.append(jnp.concatenate([top, bot], axis=0))
            order = range(n_chunks) if d == 0 else range(n_chunks - 1, -1, -1)
            for ci in order:
                for p in range(N_PAIRS):
                    sin_ref[ci, p, d * LANES:(d + 1) * LANES, :] = states[p]
                gc, k = load_dir(d, ci * c)
                gk_ref[d, 0, pl.ds(ci * c, c), :] = gc
                gk_ref[d, 1, pl.ds(ci * c, c), :] = k
                ds, e_tot = _hgrn_chunk_state(gc, k, i_ref[pl.ds(ci * c, c), :], reverse=(d == 1))
                states = [_decay_state(states[p], _pair(e_tot, p)) + ds[p] for p in range(N_PAIRS)]

    ti = lax.broadcasted_iota(jnp.int32, (c, c), 0)
    si = lax.broadcasted_iota(jnp.int32, (c, c), 1)
    txs = ti ^ si
    diag = ti == si

    def chunk_body(ci, carry):
        r0 = pl.multiple_of(ci * c, c)
        hq = q_ref[pl.ds(r0, c), :]
        q = hq * _sigmoid(hq)
        v = i_ref[pl.ds(r0, c), :]
        gcs, ks = [], []
        for d in range(2):
            if use_ctx:
                gc, k = gk_ref[d, 0, pl.ds(r0, c), :], gk_ref[d, 1, pl.ds(r0, c), :]
            else:
                gc, k = load_dir(d, r0)
            gcs.append(gc)
            ks.append(k)
        scores = _hgrn_intra_scores(q, ks, gcs, txs)
        first = _first_head(q.shape)
        k_sum = ks[0] + ks[1]
        v_heads = (jnp.where(first, v, 0.0).astype(BF16), jnp.where(first, 0.0, v).astype(BF16))
        k_heads = (jnp.where(first, k_sum, 0.0).astype(BF16), jnp.where(first, 0.0, k_sum).astype(BF16))
        q_bf = q.astype(BF16)
        if use_ctx:
            qe = [(q * jnp.exp2(gcs[d])).astype(BF16) for d in range(2)]
        else:
            for d in range(2):
                ds, _ = _hgrn_chunk_state(gcs[d], ks[d], v, reverse=(d == 1))
                for p in range(N_PAIRS):
                    sl_ref[0, d, 2 * p] = ds[p][:HG_DK, :HG_DV]
                    sl_ref[0, d, 2 * p + 1] = ds[p][HG_DK:, HG_DV:]
        outs = []
        for p in range(N_PAIRS):
            o = None
            for j in range(2):
                s_diag = jnp.where(diag, _mm_nt(_pair(q_bf, p), _pair(k_heads[j], p)), 0.0)
                t = _mm(scores[2 * p + j] + s_diag, _pair(v_heads[j], p))
                o = t if o is None else o + t
            if use_ctx:
                o = o + _mm(jnp.concatenate([_pair(qe[0], p), _pair(qe[1], p)], axis=1), sin_ref[ci, p])
            outs.append(_head_rms(o, ng_ref[0]))
        og = og_ref[pl.ds(r0, c), :]
        o_all = jnp.concatenate(outs, axis=1) * (og * _sigmoid(og))
        o_ref[pl.ds(r0, c), :] = o_all.astype(BF16)
        return carry

    if n_chunks == 1:
        chunk_body(0, 0)
    else:
        lax.fori_loop(0, n_chunks, chunk_body, 0)


def _hgrn(z, lb, norm_g, layer, n_batch, t_len, row0, state=None):
    use_ctx = state is not None
    rb0 = row0 // t_len
    cb = Z_HG // HG_W

    def zspec(k):
        return pl.BlockSpec((t_len, HG_W), lambda b: (rb0 + b, cb + k))

    in_specs = [zspec(0), zspec(1), zspec(2), zspec(3), zspec(4),
                pl.BlockSpec((1, 2, HG_W), lambda b: (layer, 0, 0)),
                pl.BlockSpec((1, 1, LANES), lambda b: (layer, 0, 0))]
    args = [z, z, z, z, z, lb, jnp.tile(norm_g, (1, LANES // HG_DV)).reshape(-1, 1, LANES)]
    out_specs = [pl.BlockSpec((t_len, HG_W), lambda b: (b, 0))]
    out_shape = [jax.ShapeDtypeStruct((n_batch * t_len, HG_W), BF16)]
    scratch = []
    if use_ctx:
        in_specs.append(pl.BlockSpec((1, 1, 2, HG_HEADS, HG_DK, HG_DV), lambda b: (b, layer, 0, 0, 0, 0)))
        args.append(state)
        scratch.append(pltpu.VMEM((t_len // HG_CHUNK, N_PAIRS, 2 * LANES, LANES), F32))
        scratch.append(pltpu.VMEM((2, 2, t_len, HG_W), F32))
    else:
        out_specs.append(pl.BlockSpec((1, 2, HG_HEADS, HG_DK, HG_DV), lambda b: (b, 0, 0, 0, 0)))
        out_shape.append(jax.ShapeDtypeStruct((n_batch, 2, HG_HEADS, HG_DK, HG_DV), F32))
    return pl.pallas_call(
        functools.partial(_hgrn_kernel, t_len=t_len, use_ctx=use_ctx),
        grid=(n_batch,),
        in_specs=in_specs,
        out_specs=out_specs,
        out_shape=out_shape,
        scratch_shapes=scratch,
        compiler_params=_cparams(("arbitrary",), 56),
        name="hgrn_ctx" if use_ctx else "hgrn",
    )(*args)


def _merge_kernel(ap_ref, as_ref, rp_ref, rs_ref, hp_ref, hs_ref, g0_ref, g1_ref, g2_ref, x_ref, mod_ref,
                  wa_ref, wr_ref, wh_ref, wo_ref, o_ref, wa_s, wr_s, wh_s, wo_s, *, prompt_tiles):
    @pl.when(pl.program_id(0) == 0)
    def _():
        wa_s[...] = wa_ref[0].astype(BF16)
        wr_s[...] = wr_ref[0].astype(BF16)
        wh_s[...] = wh_ref[0].astype(BF16)
        wo_s[...] = wo_ref[0].astype(BF16)

    is_prompt = pl.program_id(0) < prompt_tiles

    def proj(p_ref, s_ref, w_s):
        return jnp.dot(jnp.where(is_prompt, p_ref[...], s_ref[...]), w_s[...], preferred_element_type=F32)

    merged = (_sigmoid(g0_ref[...]) * proj(ap_ref, as_ref, wa_s)
              + _sigmoid(g1_ref[...]) * proj(rp_ref, rs_ref, wr_s)
              + _sigmoid(g2_ref[...]) * proj(hp_ref, hs_ref, wh_s))
    out = jnp.dot(merged.astype(BF16), wo_s[...], preferred_element_type=F32)
    o_ref[...] = x_ref[...] + mod_ref[0, 2:3, :] * out


def _merge_project(mixed, z, x, modseg, w_attn_o, w_rnn_o, w_hgrn_o, w_out, layer):
    n_tok = x.shape[0]
    tm = 512
    per_seg = SEG_ROWS // tm
    prompt_tiles = mixed[0][0].shape[0] // tm
    half_p = pl.BlockSpec((tm, ATTN_W), lambda i: (jnp.minimum(i, prompt_tiles - 1), 0))
    half_s = pl.BlockSpec((tm, ATTN_W), lambda i: (jnp.maximum(i - prompt_tiles, 0), 0))
    full = pl.BlockSpec((tm, D_MODEL), lambda i: (i, 0))

    def gspec(k):
        return pl.BlockSpec((tm, D_MODEL), lambda i: (i, Z_GL // D_MODEL + k))

    def wspec(rows):
        return pl.BlockSpec((1, rows, D_MODEL), lambda i: (layer, 0, 0))

    return pl.pallas_call(
        functools.partial(_merge_kernel, prompt_tiles=prompt_tiles),
        grid=(n_tok // tm,),
        in_specs=[half_p, half_s, half_p, half_s, half_p, half_s, gspec(0), gspec(1), gspec(2), full,
                  pl.BlockSpec((1, N_MOD, D_MODEL), lambda i: (i // per_seg, 0, 0)),
                  wspec(ATTN_W), wspec(D_RNN), wspec(HG_W), wspec(D_MODEL)],
        out_specs=full,
        out_shape=jax.ShapeDtypeStruct((n_tok, D_MODEL), F32),
        scratch_shapes=[pltpu.VMEM((ATTN_W, D_MODEL), BF16), pltpu.VMEM((D_RNN, D_MODEL), BF16),
                        pltpu.VMEM((HG_W, D_MODEL), BF16), pltpu.VMEM((D_MODEL, D_MODEL), BF16)],
        compiler_params=_cparams(("arbitrary",), 56),
        name="merge_project",
    )(*[part for pair in mixed for part in pair], z, z, z, x, modseg, w_attn_o, w_rnn_o, w_hgrn_o, w_out)


def _router_kernel(x_ref, g_ref, mod_ref, wr_ref, br_ref, h_ref, slot_ref, wt_ref, meta_ref, run_ref):
    @pl.when(pl.program_id(0) == 0)
    def _():
        run_ref[...] = jnp.zeros_like(run_ref)

    h = _modulated_norm(x_ref[...], g_ref[0], mod_ref[0, 4:5, :], mod_ref[0, 3:4, :])
    h_ref[...] = h
    h_hi = h.astype(BF16)
    h_lo = (h - h_hi.astype(F32)).astype(BF16)
    w = wr_ref[0]
    w_hi = w.astype(BF16)
    w_lo = (w - w_hi.astype(F32)).astype(BF16)
    logits = (jnp.dot(h_hi, w_hi, preferred_element_type=F32) + jnp.dot(h_hi, w_lo, preferred_element_type=F32)
              + jnp.dot(h_lo, w_hi, preferred_element_type=F32)) + br_ref[0]
    tm = logits.shape[0]
    lane = lax.broadcasted_iota(jnp.int32, logits.shape, 1)
    ids, vals = [], []
    chosen = jnp.zeros(logits.shape, jnp.bool_)
    work = logits
    for _ in range(TOP_K):
        m = jnp.max(work, axis=-1, keepdims=True)
        idx = jnp.min(jnp.where(work == m, lane, LANES), axis=-1, keepdims=True)
        hit = lane == idx
        chosen = chosen | hit
        work = jnp.where(hit, -jnp.inf, work)
        ids.append(idx)
        vals.append(m)
    exps = [jnp.exp(v - vals[0]) for v in vals]
    den = exps[0] + exps[1] + exps[2] + exps[3]
    ind = jnp.where(chosen, 1.0, 0.0)
    r_i = lax.broadcasted_iota(jnp.int32, (tm, tm), 0)
    c_i = lax.broadcasted_iota(jnp.int32, (tm, tm), 1)
    count = jnp.sum(ind, axis=0, keepdims=True)
    padded = jnp.floor((count + (ROW_BLOCK - 1)) * (1.0 / ROW_BLOCK)) * ROW_BLOCK
    incl = jnp.broadcast_to(padded, (ROW_BLOCK, LANES))
    lane8 = lax.broadcasted_iota(jnp.int32, (ROW_BLOCK, LANES), 1)
    d = 1
    while d < LANES:
        incl = incl + jnp.where(lane8 >= d, pltpu.roll(incl, d, 1), 0.0)
        d *= 2
    local_start = incl[0:1] - padded
    place = _mm(jnp.where(c_i < r_i, 1.0, 0.0), ind) + local_start
    slot_out = jnp.zeros(logits.shape, jnp.int32)
    wt_out = jnp.zeros(logits.shape, F32)
    for k in range(TOP_K):
        slot = jnp.sum(jnp.where(lane == ids[k], place, 0.0), axis=-1, keepdims=True)
        slot_out = jnp.where(lane == k, slot.astype(jnp.int32), slot_out)
        wt_out = jnp.where(lane == k, exps[k] / den, wt_out)
    slot_ref[...] = slot_out
    wt_ref[...] = wt_out
    row8 = lax.broadcasted_iota(jnp.int32, (ROW_BLOCK, LANES), 0)
    table = jnp.where(row8 == 0, local_start, jnp.where(row8 == 1, padded, run_ref[0:1, :]))
    meta_ref[0] = jnp.where(row8 < 3, table, 0.0).astype(jnp.int32)
    run_ref[...] = run_ref[...] + padded


def _router(x, norm_g, modseg, w_router, b_router, layer):
    n_tok = x.shape[0]
    tm = TOKEN_TILE
    per_seg = SEG_ROWS // tm
    depth = w_router.shape[0]
    w_pad = jnp.zeros((depth, D_MODEL, LANES), F32).at[:, :, :N_EXPERTS].set(w_router)
    b_pad = jnp.full((depth, 1, LANES), NEG_BIG, F32).at[:, 0, :N_EXPERTS].set(b_router)
    full = pl.BlockSpec((tm, D_MODEL), lambda i: (i, 0))
    small = pl.BlockSpec((tm, LANES), lambda i: (i, 0))
    return pl.pallas_call(
        _router_kernel,
        grid=(n_tok // tm,),
        in_specs=[full,
                  pl.BlockSpec((1, 1, D_MODEL), lambda i: (layer, 0, 0)),
                  pl.BlockSpec((1, N_MOD, D_MODEL), lambda i: (i // per_seg, 0, 0)),
                  pl.BlockSpec((1, D_MODEL, LANES), lambda i: (layer, 0, 0)),
                  pl.BlockSpec((1, 1, LANES), lambda i: (layer, 0, 0))],
        out_specs=[full, small, small, pl.BlockSpec((1, ROW_BLOCK, LANES), lambda i: (i, 0, 0))],
        out_shape=[jax.ShapeDtypeStruct((n_tok, D_MODEL), F32),
                   jax.ShapeDtypeStruct((n_tok, LANES), jnp.int32),
                   jax.ShapeDtypeStruct((n_tok, LANES), F32),
                   jax.ShapeDtypeStruct((n_tok // tm, ROW_BLOCK, LANES), jnp.int32)],
        scratch_shapes=[pltpu.VMEM((ROW_BLOCK, LANES), F32)],
        compiler_params=_cparams(("arbitrary",), 40),
        name="router",
    )(x, norm_g.reshape(-1, 1, D_MODEL), modseg, w_pad, b_pad)


COPY_BLOCKS = (BIG_BLOCK, 2 * ROW_BLOCK, ROW_BLOCK)
COPY_CAPS = (LOCAL_ROWS // COPY_BLOCKS[0],) + tuple(
    N_EXPERTS * (COPY_BLOCKS[c - 1] // COPY_BLOCKS[c] - 1) for c in range(1, len(COPY_BLOCKS)))
COPY_LIST_LEN = 1024
_COPY_AT = tuple(len(COPY_BLOCKS) + 2 * sum(COPY_CAPS[:c]) for c in range(len(COPY_BLOCKS)))
assert _COPY_AT[-1] + 2 * COPY_CAPS[-1] <= COPY_LIST_LEN


def _copy_lists(local_start, padded, sorted_start):

    def expand(count, local0, sorted0, step, cap):
        last = _prefix_sum(count)[:, None, :]
        first = last - count[:, None, :]
        f = jnp.arange(cap, dtype=jnp.int32)[None, :, None]
        mine = ((first <= f) & (f < last)).astype(jnp.int32)
        offset = step * (f - first)
        return (last[:, 0, -1:], jnp.sum(mine * (local0[:, None, :] + offset), axis=2),
                jnp.sum(mine * (sorted0[:, None, :] + offset), axis=2))

    counts, lists = [], []
    done = jnp.zeros_like(padded)
    for rows, cap in zip(COPY_BLOCKS, COPY_CAPS):
        n_copies = (padded - done) // rows
        cnt, loc, srt = expand(n_copies, local_start + done, sorted_start + done, rows, cap)
        counts.append(cnt)
        lists += [loc, srt]
        done = done + n_copies * rows
    parts = counts + lists
    used = sum(p.shape[1] for p in parts)
    parts.append(jnp.zeros((padded.shape[0], COPY_LIST_LEN - used), jnp.int32))
    return jnp.concatenate(parts, axis=1).astype(jnp.int32).reshape(-1)


def _for_each_block(list_ref, fn):
    for c, (rows, cap) in enumerate(zip(COPY_BLOCKS, COPY_CAPS)):
        def copy(j, carry, rows=rows, cap=cap, at=_COPY_AT[c]):
            fn(pl.multiple_of(list_ref[at + j], ROW_BLOCK),
               pl.multiple_of(list_ref[at + cap + j], ROW_BLOCK), rows)
            return carry

        lax.fori_loop(0, list_ref[c], copy, 0)


def _slot_matrix(slot_ref, values):
    col = lax.broadcasted_iota(jnp.int32, (TOKEN_TILE, LOCAL_ROWS), 1)
    out = jnp.zeros((TOKEN_TILE, LOCAL_ROWS), F32)
    for k in range(TOP_K):
        out = jnp.where(col == slot_ref[:, k:k + 1], values[k], out)
    return out.astype(BF16)


def _dispatch_kernel(meta_ref, prev_meta_ref, tail_ref, slot_ref, h_ref, xs_ref, local_ref, zero_ref,
                     sem, block_sem):
    i = pl.program_id(0)
    cur = i % 2

    @pl.when(i == 0)
    def _():
        zero_ref[...] = jnp.zeros_like(zero_ref)

        def tails(fn):
            def group(e, carry):
                def block(j, c):
                    fn(pl.multiple_of(tail_ref[e] + j * ROW_BLOCK, ROW_BLOCK))
                    return c
                return lax.fori_loop(0, tail_ref[N_EXPERTS + e], block, carry)
            lax.fori_loop(0, N_EXPERTS, group, 0)

        def zero_copy(row):
            return pltpu.make_async_copy(zero_ref.at[pl.ds(0, ROW_BLOCK)],
                                         xs_ref.at[pl.ds(row, ROW_BLOCK)], sem)

        tails(lambda row: zero_copy(row).start())
        tails(lambda row: zero_copy(row).wait())

        def unused(fn):
            def tile(j, c):
                fn(pl.multiple_of(tail_ref[2 * N_EXPERTS] + j * EXPERT_TILE, EXPERT_TILE))
                return c
            lax.fori_loop(0, tail_ref[2 * N_EXPERTS + 1], tile, 0)

        def zero_tile(row):
            return pltpu.make_async_copy(zero_ref, xs_ref.at[pl.ds(row, EXPERT_TILE)], sem)

        unused(lambda row: zero_tile(row).start())
        unused(lambda row: zero_tile(row).wait())

    onehot = _slot_matrix(slot_ref, [1.0] * TOP_K)
    local_ref[cur] = _mm_tn(onehot, h_ref[...])

    def block_copy(buf):
        return lambda local_row, sorted_row, rows: pltpu.make_async_copy(
            local_ref.at[buf, pl.ds(local_row, rows)], xs_ref.at[pl.ds(sorted_row, rows)],
            block_sem.at[buf])

    _for_each_block(meta_ref, lambda a, b, n: block_copy(cur)(a, b, n).start())

    @pl.when(i > 0)
    def _():
        _for_each_block(prev_meta_ref, lambda a, b, n: block_copy(1 - cur)(a, b, n).wait())

    @pl.when(i == pl.num_programs(0) - 1)
    def _():
        _for_each_block(meta_ref, lambda a, b, n: block_copy(cur)(a, b, n).wait())


def _dispatch(h, slots, meta, tails, n_rows):
    n_tok = h.shape[0]
    return pl.pallas_call(
        _dispatch_kernel,
        grid=(n_tok // TOKEN_TILE,),
        in_specs=[pl.BlockSpec((COPY_LIST_LEN,), lambda i: (i,), memory_space=pltpu.SMEM),
                  pl.BlockSpec((COPY_LIST_LEN,), lambda i: (jnp.maximum(i - 1, 0),), memory_space=pltpu.SMEM),
                  pl.BlockSpec((LANES,), lambda i: (0,), memory_space=pltpu.SMEM),
                  pl.BlockSpec((TOKEN_TILE, LANES), lambda i: (i, 0)),
                  pl.BlockSpec((TOKEN_TILE, D_MODEL), lambda i: (i, 0))],
        out_specs=pl.BlockSpec(memory_space=pl.ANY),
        out_shape=jax.ShapeDtypeStruct((n_rows, D_MODEL), F32),
        scratch_shapes=[pltpu.VMEM((2, LOCAL_ROWS, D_MODEL), F32), pltpu.VMEM((EXPERT_TILE, D_MODEL), F32),
                        pltpu.SemaphoreType.DMA(()), pltpu.SemaphoreType.DMA((2,))],
        compiler_params=_cparams(("arbitrary",), 48),
        name="dispatch",
    )(meta, meta, tails, slots, h)


def _combine_kernel(meta_ref, next_meta_ref, slot_ref, wt_ref, ys_ref, x_ref, mod_ref, o_ref, local_ref, sem):
    i = pl.program_id(0)
    cur = i % 2

    def block_copy(buf):
        return lambda local_row, sorted_row, rows: pltpu.make_async_copy(
            ys_ref.at[pl.ds(sorted_row, rows)], local_ref.at[buf, pl.ds(local_row, rows)], sem.at[buf])

    def start_gather(table_ref, buf):
        local_ref[buf, TOKEN_TILE * TOP_K:, :] = jnp.zeros((LOCAL_ROWS - TOKEN_TILE * TOP_K, D_MODEL), F32)
        _for_each_block(table_ref, lambda a, b, n: block_copy(buf)(a, b, n).start())

    @pl.when(i == 0)
    def _():
        start_gather(meta_ref, 0)

    @pl.when(i + 1 < pl.num_programs(0))
    def _():
        start_gather(next_meta_ref, 1 - cur)

    _for_each_block(meta_ref, lambda a, b, n: block_copy(cur)(a, b, n).wait())
    weights = _slot_matrix(slot_ref, [wt_ref[:, k:k + 1] for k in range(TOP_K)])
    acc = jnp.dot(weights, local_ref[cur].astype(BF16), preferred_element_type=F32)
    o_ref[...] = x_ref[...] + mod_ref[0, 5:6, :] * acc


def _combine(ys, slots, meta, wts, x, modseg):
    n_tok = x.shape[0]
    per_seg = SEG_ROWS // TOKEN_TILE
    full = pl.BlockSpec((TOKEN_TILE, D_MODEL), lambda i: (i, 0))
    small = pl.BlockSpec((TOKEN_TILE, LANES), lambda i: (i, 0))
    return pl.pallas_call(
        _combine_kernel,
        grid=(n_tok // TOKEN_TILE,),
        in_specs=[pl.BlockSpec((COPY_LIST_LEN,), lambda i: (i,), memory_space=pltpu.SMEM),
                  pl.BlockSpec((COPY_LIST_LEN,), lambda i: (jnp.minimum(i + 1, n_tok // TOKEN_TILE - 1),),
                               memory_space=pltpu.SMEM),
                  small, small,
                  pl.BlockSpec(memory_space=pl.ANY),
                  full,
                  pl.BlockSpec((1, N_MOD, D_MODEL), lambda i: (i // per_seg, 0, 0))],
        out_specs=full,
        out_shape=jax.ShapeDtypeStruct((n_tok, D_MODEL), F32),
        scratch_shapes=[pltpu.VMEM((2, LOCAL_ROWS, D_MODEL), F32), pltpu.SemaphoreType.DMA((2,))],
        compiler_params=_cparams(("arbitrary",), 56),
        name="combine",
    )(meta, meta, slots, wts, ys, x, modseg)


def _expert_kernel(te_ref, first_ref, slot_ref, next_ref, na_ref, x_ref, wgu_hbm, bgu_ref, wd_hbm, bd_ref,
                   o_ref, wgu_f, wd_f, wgu_s, wd_s, sem, *, layer):
    i = pl.program_id(0)
    active = i < na_ref[0]

    def fetch(e, s):
        return (pltpu.make_async_copy(wgu_hbm.at[layer, e], wgu_f.at[s], sem.at[0, s]),
                pltpu.make_async_copy(wd_hbm.at[layer, e], wd_f.at[s], sem.at[1, s]))

    @pl.when(jnp.logical_and(i == 0, first_ref[0] == 1))
    def _():
        for cp in fetch(te_ref[0], 0):
            cp.start()

    @pl.when(first_ref[i] == 1)
    def _():
        s = slot_ref[i]
        for cp in fetch(te_ref[i], s):
            cp.wait()
        wgu_s[...] = wgu_f[s].astype(BF16)
        wd_s[...] = wd_f[s].astype(BF16)

        @pl.when(next_ref[i] >= 0)
        def _():
            for cp in fetch(next_ref[i], 1 - s):
                cp.start()

    @pl.when(active)
    def _():
        gu = jnp.dot(x_ref[...].astype(BF16), wgu_s[...], preferred_element_type=F32) + bgu_ref[0, 0]
        gate = jnp.minimum(gu[:, :D_FF], SWIGLU_LIMIT)
        up = jnp.clip(gu[:, D_FF:], -SWIGLU_LIMIT, SWIGLU_LIMIT)
        act = gate * _sigmoid(SWIGLU_ALPHA * gate) * (up + 1.0)
        o_ref[...] = jnp.dot(act.astype(BF16), wd_s[...], preferred_element_type=F32) + bd_ref[0, 0]

    @pl.when(jnp.logical_not(active))
    def _():
        o_ref[...] = jnp.zeros_like(o_ref)


def _experts(xs, tile_expert, ends, n_active, w_gu, b_gu, w_down, b_down, layer):
    n_rows = xs.shape[0]
    depth = w_gu.shape[0]
    tm = EXPERT_TILE
    n_tiles = n_rows // tm
    tile = jnp.arange(n_tiles, dtype=jnp.int32)
    active = tile < n_active[0]
    first = (active & ((tile == 0) | (tile_expert != jnp.roll(tile_expert, 1)))).astype(jnp.int32)
    slot = (_prefix_sum(first) - 1) % 2
    next_tile = ends[tile_expert] // tm
    next_expert = jnp.where(next_tile < n_active[0],
                            tile_expert[jnp.minimum(next_tile, n_tiles - 1)], -1).astype(jnp.int32)

    def tiled(shape, index):
        return pl.BlockSpec(shape, lambda i, te, fi, sl, nx, na: index(i, te, na))

    grid_spec = pltpu.PrefetchScalarGridSpec(
        num_scalar_prefetch=5,
        grid=(n_tiles,),
        in_specs=[tiled((tm, D_MODEL), lambda i, te, na: (jnp.where(i < na[0], i, 0), 0)),
                  pl.BlockSpec(memory_space=pl.ANY),
                  tiled((1, 1, 1, 2 * D_FF), lambda i, te, na: (layer, te[i], 0, 0)),
                  pl.BlockSpec(memory_space=pl.ANY),
                  tiled((1, 1, 1, D_MODEL), lambda i, te, na: (layer, te[i], 0, 0))],
        out_specs=tiled((tm, D_MODEL), lambda i, te, na: (i, 0)),
        scratch_shapes=[pltpu.VMEM((2, D_MODEL, 2 * D_FF), F32), pltpu.VMEM((2, D_FF, D_MODEL), F32),
                        pltpu.VMEM((D_MODEL, 2 * D_FF), BF16), pltpu.VMEM((D_FF, D_MODEL), BF16),
                        pltpu.SemaphoreType.DMA((2, 2))],
    )
    return pl.pallas_call(
        functools.partial(_expert_kernel, layer=layer),
        grid_spec=grid_spec,
        out_shape=jax.ShapeDtypeStruct((n_rows, D_MODEL), F32),
        compiler_params=_cparams(("arbitrary",), 56),
        name="experts",
    )(tile_expert, first, slot.astype(jnp.int32), next_expert, n_active, xs, w_gu,
      b_gu.reshape(depth, N_EXPERTS, 1, 2 * D_FF), w_down, b_down.reshape(depth, N_EXPERTS, 1, D_MODEL))


def _moe(x, norm_g, modseg, w_router, b_router, w_gu, b_gu, w_down, b_down, layer):
    n_tok = x.shape[0]
    h, slots, wts, table = _router(x, norm_g, modseg, w_router, b_router, layer)
    tm = EXPERT_TILE
    n_tiles = n_tok // TOKEN_TILE
    n_rows = n_tok * TOP_K + n_tiles * N_EXPERTS * ROW_BLOCK + N_EXPERTS * tm
    n_rows = -(-n_rows // tm) * tm
    local_start = table[:, 0, :N_EXPERTS]
    padded = table[:, 1, :N_EXPERTS]
    earlier = table[:, 2, :N_EXPERTS]
    used = earlier[-1] + padded[-1]
    size = ((used + tm - 1) // tm) * tm
    ends = _prefix_sum(size)
    starts = ends - size
    sorted_start = starts[None, :] + earlier
    meta = _copy_lists(local_start, padded, sorted_start)
    tails = jnp.concatenate([starts + used, (size - used) // ROW_BLOCK,
                             ends[-1:], (n_rows - ends[-1:]) // tm,
                             jnp.zeros((LANES - 2 * N_EXPERTS - 2,), jnp.int32)])
    tile_row0 = jnp.arange(n_rows // tm, dtype=jnp.int32) * tm
    tile_expert = jnp.minimum(jnp.sum((ends[None, :] <= tile_row0[:, None]).astype(jnp.int32), axis=1),
                              N_EXPERTS - 1)
    n_active = (ends[-1:] // tm).astype(jnp.int32)
    xs = _dispatch(h, slots, meta, tails, n_rows)
    ys = _experts(xs, tile_expert, ends, n_active, w_gu, b_gu, w_down, b_down, layer)
    return _combine(ys, slots, meta, wts, x, modseg)


def _rope_tables(n_tok):
    rows = n_tok // GRID_W
    row = jnp.repeat(jnp.arange(rows, dtype=F32), GRID_W)
    col = jnp.tile(jnp.arange(GRID_W, dtype=F32), rows)
    quarter = HEAD_DIM // 4
    inv_freq = ROPE_BASE ** (-jnp.arange(quarter, dtype=F32) / quarter)
    ang_r = row[:, None] * inv_freq
    ang_c = col[:, None] * inv_freq
    ang = jnp.concatenate([ang_r, ang_r, ang_c, ang_c], axis=-1)
    cos, sin = jnp.cos(ang), jnp.sin(ang)
    first = (jnp.arange(HEAD_DIM) % (2 * quarter)) < quarter
    sin_a = jnp.where(first, -sin, 0.0)
    sin_b = jnp.where(first, 0.0, sin)
    return tuple(jnp.tile(t, (1, LANES // HEAD_DIM)) for t in (cos, sin_a, sin_b))


def kernel(x_prompt, x_sample, c, cache_k, cache_v, state_rglru, state_hgrn, c_ctx, w_mod, b_mod, norm1_g, norm2_g, w_in, q_norm_g, k_norm_g, conv_w, conv_b, rg_wa, rg_ba, rg_wx, rg_bx, rg_lambda, hgrn_lb_logits, hgrn_norm_g, w_attn_o, w_rnn_o, w_hgrn_o, w_out, w_router, b_router, w_gu, b_gu, w_down, b_down):
    depth = w_mod.shape[0]
    n_p, t_p = x_prompt.shape[0], x_prompt.shape[1]
    n_s, t_s = x_sample.shape[0], x_sample.shape[1]
    tok_p = n_p * t_p
    tok_s = n_s * t_s
    assert tok_p % SEG_ROWS == 0 and t_s == SEG_ROWS and SEG_ROWS % t_p == 0

    x = jnp.concatenate([x_prompt.reshape(tok_p, D_MODEL), x_sample.reshape(tok_s, D_MODEL)], axis=0)

    lb_all = jnp.moveaxis(_prefix_sum(jnp.moveaxis(jax.nn.softmax(hgrn_lb_logits.astype(F32), axis=0), 0, -1)),
                          -1, 0)
    lb_all = lb_all - lb_all[:1]
    rope = _rope_tables(t_s)

    assert 1 + n_s <= ROW_BLOCK
    cond8 = jnp.zeros((ROW_BLOCK, D_MODEL), F32).at[0].set(c_ctx).at[1:1 + n_s].set(c)
    mod = _modulation(cond8, w_mod, b_mod)
    seg_ids = jnp.asarray([0] * (tok_p // SEG_ROWS) + [1 + b for b in range(n_s)], jnp.int32)
    modsegs = mod[:, seg_ids].reshape(depth, seg_ids.shape[0], N_MOD, D_MODEL)

    b_gate = jnp.stack([rg_ba, rg_bx], axis=2).reshape(depth, 1, 4 * D_RNN)

    new_k, new_v, new_hr, new_s = [], [], [], []
    for l in range(depth):
        modseg = modsegs[l]
        z = _input_projection(x, norm1_g, modseg, w_in, l)
        attn_p, k_l, v_l = _attention(z, q_norm_g, k_norm_g, l, n_p, t_p, 0)
        (attn_s,) = _attention(z, q_norm_g, k_norm_g, l, n_s, t_s, tok_p, rope=rope,
                               cache=(cache_k, cache_v))
        rnn_p, hr_l = _rglru(z, conv_w, conv_b, rg_wa, rg_wx, b_gate, rg_lambda, l, n_p, t_p, 0)
        (rnn_s,) = _rglru(z, conv_w, conv_b, rg_wa, rg_wx, b_gate, rg_lambda, l, n_s, t_s, tok_p,
                          state=state_rglru)
        hg_p, s_l = _hgrn(z, lb_all, hgrn_norm_g, l, n_p, t_p, 0)
        (hg_s,) = _hgrn(z, lb_all, hgrn_norm_g, l, n_s, t_s, tok_p, state=state_hgrn)
        x = _merge_project(((attn_p, attn_s), (rnn_p, rnn_s), (hg_p, hg_s)), z, x, modseg,
                           w_attn_o, w_rnn_o, w_hgrn_o, w_out, l)
        x = _moe(x, norm2_g, modseg, w_router, b_router, w_gu, b_gu, w_down, b_down, l)
        new_k.append(k_l.reshape(n_p, t_p, N_KV_HEADS, HEAD_DIM))
        new_v.append(v_l.reshape(n_p, t_p, N_KV_HEADS, HEAD_DIM))
        new_hr.append(hr_l)
        new_s.append(s_l)

    y_prompt = x[:tok_p].reshape(n_p, t_p, D_MODEL)
    y_sample = x[tok_p:].reshape(n_s, t_s, D_MODEL)
    return (y_prompt, y_sample, jnp.stack(new_k, axis=1), jnp.stack(new_v, axis=1),
            jnp.stack(new_hr, axis=1), jnp.stack(new_s, axis=1))
```

```python
import functools

import jax
import jax.numpy as jnp
from jax import lax
from jax.experimental import pallas as pl
from jax.experimental.pallas import tpu as pltpu

F32 = jnp.float32
BF16 = jnp.bfloat16

D_MODEL = 1024
GRID_W = 64
HEAD_DIM = 64
N_HEADS = 8
N_KV_HEADS = 2
KV_GROUP = N_HEADS // N_KV_HEADS
ATTN_W = N_HEADS * HEAD_DIM
KV_W = N_KV_HEADS * HEAD_DIM
ROPE_BASE = 10000.0
D_RNN = 512
RG_BLOCKS = 8
RG_BW = D_RNN // RG_BLOCKS
RG_C = 8.0
HG_HEADS = 8
HG_DK = 64
HG_DV = 64
HG_W = HG_HEADS * HG_DK
N_PAIRS = HG_W // 128
N_EXPERTS = 32
TOP_K = 4
D_FF = D_MODEL
SWIGLU_LIMIT = 7.0
SWIGLU_ALPHA = 1.702
N_MOD = 6
EPS = 1e-6
IN_W = ATTN_W + 2 * KV_W + 2 * D_RNN + 5 * HG_W + 3 * D_MODEL

LANES = 128
SEG_ROWS = 1024
COL_TILE = 256
HG_CHUNK = 256
Q_TILE = 256
EXPERT_TILE = 384
BIG_BLOCK = 64
TOKEN_TILE = 512
ROW_BLOCK = 8
LOCAL_ROWS = TOKEN_TILE * TOP_K + N_EXPERTS * ROW_BLOCK
NEG_BIG = -1e30
LOG2_E = 1.4426950408889634

_SRC_TILES = tuple(range(17, 29)) + tuple(range(7, 17)) + tuple(range(3, 7)) + (0, 1, 2)
Z_GL = 0
Z_HG = 3 * D_MODEL
Z_RG = Z_HG + 5 * HG_W
Z_Q = Z_RG + 2 * D_RNN
Z_K = Z_Q + ATTN_W
Z_V = Z_K + KV_W
Z_W = -(-IN_W // (2 * COL_TILE)) * (2 * COL_TILE)


def _cparams(sem, vmem_mb):
    return pltpu.CompilerParams(dimension_semantics=sem, vmem_limit_bytes=vmem_mb * 1024 * 1024)


def _prefix_sum(x):
    n = x.shape[-1]
    upto = (jnp.arange(n)[None, :] <= jnp.arange(n)[:, None]).astype(x.dtype)
    return jnp.sum(x[..., None, :] * upto, axis=-1)


def _mm(a, b):
    return jnp.dot(a.astype(BF16), b.astype(BF16), preferred_element_type=F32)


def _mm_nt(a, b):
    return lax.dot_general(a.astype(BF16), b.astype(BF16), (((1,), (1,)), ((), ())),
                           preferred_element_type=F32)


def _mm_tn(a, b):
    return lax.dot_general(a.astype(BF16), b.astype(BF16), (((0,), (0,)), ((), ())),
                           preferred_element_type=F32)


def _sigmoid(x):
    return 1.0 / (1.0 + jnp.exp(-x))


def _softplus(x):
    return jnp.maximum(x, 0.0) + jnp.log1p(jnp.exp(-jnp.abs(x)))


def _modulated_norm(x, g, scale, shift):
    y = x * lax.rsqrt(jnp.mean(x * x, axis=-1, keepdims=True) + EPS)
    return (y * g) * (1.0 + scale) + shift


def _mod_kernel(c_ref, w_ref, b_ref, o_ref):
    c = c_ref[...]
    s = c * _sigmoid(c)
    o_ref[0] = _mm(s, w_ref[0]) + b_ref[0]


def _modulation(cond8, w_mod, b_mod):
    depth = w_mod.shape[0]
    tn = 1024
    return pl.pallas_call(
        _mod_kernel,
        grid=(depth, N_MOD * D_MODEL // tn),
        in_specs=[pl.BlockSpec((ROW_BLOCK, D_MODEL), lambda l, j: (0, 0)),
                  pl.BlockSpec((1, D_MODEL, tn), lambda l, j: (l, 0, j)),
                  pl.BlockSpec((1, 1, tn), lambda l, j: (l, 0, j))],
        out_specs=pl.BlockSpec((1, ROW_BLOCK, tn), lambda l, j: (l, 0, j)),
        out_shape=jax.ShapeDtypeStruct((depth, ROW_BLOCK, N_MOD * D_MODEL), F32),
        compiler_params=_cparams(("arbitrary", "arbitrary"), 32),
        name="modulation",
    )(cond8, w_mod, b_mod.reshape(depth, 1, N_MOD * D_MODEL))


def _inproj_kernel(src_ref, x_ref, g_ref, mod_ref, wa_ref, wb_ref, o_ref, h_ref, w_s):
    j = pl.program_id(1)

    @pl.when(j == 0)
    def _():
        for s in range(x_ref.shape[0] // SEG_ROWS):
            rows = slice(s * SEG_ROWS, (s + 1) * SEG_ROWS)
            h = _modulated_norm(x_ref[rows, :], g_ref[0], mod_ref[s, 1:2, :], mod_ref[s, 0:1, :])
            h_ref[rows, :] = h.astype(BF16)

    @pl.when(pl.program_id(0) == 0)
    def _():
        w_s[j, :, 0:COL_TILE] = wa_ref[0].astype(BF16)
        w_s[j, :, COL_TILE:2 * COL_TILE] = wb_ref[0].astype(BF16)

    o_ref[...] = jnp.dot(h_ref[...], w_s[j], preferred_element_type=F32)


def _input_projection(x, norm_g, modseg, w_in, layer):
    n_tok = x.shape[0]
    segs = 2
    tm = segs * SEG_ROWS
    n_col = Z_W // (2 * COL_TILE)
    src = jnp.asarray(_SRC_TILES + (_SRC_TILES[-1],) * (Z_W // COL_TILE - len(_SRC_TILES)), jnp.int32)

    def wspec(k):
        return pl.BlockSpec((1, D_MODEL, COL_TILE),
                            lambda i, j, s: (layer, 0, s[2 * jnp.where(i == 0, j, n_col - 1) + k]))

    grid_spec = pltpu.PrefetchScalarGridSpec(
        num_scalar_prefetch=1,
        grid=(n_tok // tm, n_col),
        in_specs=[pl.BlockSpec((tm, D_MODEL), lambda i, j, s: (i, 0)),
                  pl.BlockSpec((1, 1, D_MODEL), lambda i, j, s: (layer, 0, 0)),
                  pl.BlockSpec((segs, N_MOD, D_MODEL), lambda i, j, s: (i, 0, 0)),
                  wspec(0), wspec(1)],
        out_specs=pl.BlockSpec((tm, 2 * COL_TILE), lambda i, j, s: (i, j)),
        scratch_shapes=[pltpu.VMEM((tm, D_MODEL), BF16),
                        pltpu.VMEM((n_col, D_MODEL, 2 * COL_TILE), BF16)],
    )
    return pl.pallas_call(
        _inproj_kernel,
        grid_spec=grid_spec,
        out_shape=jax.ShapeDtypeStruct((n_tok, Z_W), F32),
        compiler_params=_cparams(("arbitrary", "arbitrary"), 56),
        name="input_projection",
    )(src, x, norm_g.reshape(-1, 1, D_MODEL), modseg, w_in, w_in)


def _head_rms(x, g):
    lane = lax.broadcasted_iota(jnp.int32, x.shape, 1)
    lo = lane < HEAD_DIM
    xx = x * x
    s_lo = jnp.sum(jnp.where(lo, xx, 0.0), axis=-1, keepdims=True)
    s_hi = jnp.sum(jnp.where(lo, 0.0, xx), axis=-1, keepdims=True)
    inv = jnp.where(lo, lax.rsqrt(s_lo * (1.0 / HEAD_DIM) + EPS),
                    lax.rsqrt(s_hi * (1.0 / HEAD_DIM) + EPS))
    return x * inv * g


def _rope(x, cos, sin_a, sin_b):
    q = HEAD_DIM // 4
    return x * cos + pltpu.roll(x, LANES - q, 1) * sin_a + pltpu.roll(x, q, 1) * sin_b


def _attn_kernel(*refs, t_len, use_ctx):
    if use_ctx:
        (q_ref, k_ref, v_ref, qg_ref, kg_ref, cos_q_ref, sa_q_ref, sb_q_ref,
         cos_k_ref, sa_k_ref, sb_k_ref, ck_ref, cv_ref, o_ref, kall_ref, vall_ref) = refs
    else:
        (q_ref, k_ref, v_ref, qg_ref, kg_ref, o_ref, ko_ref, vo_ref, kall_ref, vall_ref) = refs

    @pl.when(pl.program_id(1) == 0)
    def _():
        k = _head_rms(k_ref[...], kg_ref[...])
        if use_ctx:
            k = _rope(k, cos_k_ref[...], sa_k_ref[...], sb_k_ref[...])
            kall_ref[0:t_len, :] = k.astype(BF16)
            vall_ref[0:t_len, :] = v_ref[...].astype(BF16)
            kall_ref[t_len:, :] = ck_ref[0, 0].astype(BF16)
            vall_ref[t_len:, :] = cv_ref[0, 0].astype(BF16)
        else:
            ko_ref[0] = k
            vo_ref[0] = v_ref[...]
            kall_ref[...] = k.astype(BF16)
            vall_ref[...] = v_ref[...].astype(BF16)

    heads = []
    for c in range(ATTN_W // LANES):
        qc = _head_rms(q_ref[:, c * LANES:(c + 1) * LANES], qg_ref[...])
        if use_ctx:
            qc = _rope(qc, cos_q_ref[...], sa_q_ref[...], sb_q_ref[...])
        qc = (qc * (HEAD_DIM ** -0.5)).astype(BF16)
        heads.append(qc[:, :HEAD_DIM])
        heads.append(qc[:, HEAD_DIM:])
    tq = q_ref.shape[0]
    for g in range(N_KV_HEADS):
        qs = jnp.concatenate(heads[g * KV_GROUP:(g + 1) * KV_GROUP], axis=0)
        kh = kall_ref[:, g * HEAD_DIM:(g + 1) * HEAD_DIM]
        vh = vall_ref[:, g * HEAD_DIM:(g + 1) * HEAD_DIM]
        s = _mm_nt(qs, kh)
        m = jnp.max(s, axis=-1, keepdims=True)
        p = jnp.exp(s - m)
        den = jnp.sum(p, axis=-1, keepdims=True)
        o = _mm(p, vh) / den
        for j in range(KV_GROUP):
            hh = g * KV_GROUP + j
            o_ref[:, hh * HEAD_DIM:(hh + 1) * HEAD_DIM] = o[j * tq:(j + 1) * tq].astype(BF16)


def _attention(z, q_g, k_g, layer, n_batch, t_len, row0, rope=None, cache=None):
    use_ctx = cache is not None
    nq = t_len // Q_TILE
    rb0 = row0 // t_len
    qb0 = row0 // Q_TILE
    qg = jnp.tile(q_g[layer], 2).reshape(1, LANES)
    kg = jnp.tile(k_g[layer], 2).reshape(1, LANES)
    n_out_tok = n_batch * t_len
    vec = pl.BlockSpec((1, LANES), lambda b, i: (0, 0))
    in_specs = [pl.BlockSpec((Q_TILE, ATTN_W), lambda b, i: (qb0 + b * nq + i, Z_Q // ATTN_W)),
                pl.BlockSpec((t_len, KV_W), lambda b, i: (rb0 + b, Z_K // KV_W)),
                pl.BlockSpec((t_len, KV_W), lambda b, i: (rb0 + b, Z_V // KV_W)),
                vec, vec]
    args = [z, z, z, qg, kg]
    out_specs = [pl.BlockSpec((Q_TILE, ATTN_W), lambda b, i: (b * nq + i, 0))]
    out_shape = [jax.ShapeDtypeStruct((n_out_tok, ATTN_W), BF16)]
    t_keys = t_len
    if use_ctx:
        cos, sin_a, sin_b = rope
        cache_k, cache_v = cache
        past = cache_k.shape[2]
        t_keys = t_len + past
        tq_spec = pl.BlockSpec((Q_TILE, LANES), lambda b, i: (i, 0))
        tk_spec = pl.BlockSpec((t_len, LANES), lambda b, i: (0, 0))
        c_spec = pl.BlockSpec((1, 1, past, KV_W), lambda b, i: (b, layer, 0, 0))
        in_specs += [tq_spec, tq_spec, tq_spec, tk_spec, tk_spec, tk_spec, c_spec, c_spec]
        args += [cos, sin_a, sin_b, cos, sin_a, sin_b,
                 cache_k.reshape(cache_k.shape[0], cache_k.shape[1], past, KV_W),
                 cache_v.reshape(cache_v.shape[0], cache_v.shape[1], past, KV_W)]
    else:
        kv_spec = pl.BlockSpec((1, t_len, KV_W), lambda b, i: (b, 0, 0))
        out_specs += [kv_spec, kv_spec]
        out_shape += [jax.ShapeDtypeStruct((n_batch, t_len, KV_W), F32)] * 2
    return pl.pallas_call(
        functools.partial(_attn_kernel, t_len=t_len, use_ctx=use_ctx),
        grid=(n_batch, nq),
        in_specs=in_specs,
        out_specs=out_specs,
        out_shape=out_shape,
        scratch_shapes=[pltpu.VMEM((t_keys, KV_W), BF16), pltpu.VMEM((t_keys, KV_W), BF16)],
        compiler_params=_cparams(("arbitrary", "arbitrary"), 48),
        name="attention_ctx" if use_ctx else "attention",
    )(*args)


def _shift_rows(x, d):
    n = x.shape[0]
    row = lax.broadcasted_iota(jnp.int32, x.shape, 0)
    y = pltpu.roll(x, d % n, 0)
    if d > 0:
        return jnp.where(row >= d, y, 0.0)
    return jnp.where(row < n + d, y, 0.0)


def _linear_scan(a, u, reverse):
    n, w = a.shape
    row = lax.broadcasted_iota(jnp.int32, a.shape, 0)

    def shifted(x, d, fill):
        if d % ROW_BLOCK == 0:
            pad = jnp.full((d, w), fill, x.dtype)
            return jnp.concatenate([x[d:], pad] if reverse else [pad, x[:n - d]], axis=0)
        if reverse:
            return jnp.where(row < n - d, pltpu.roll(x, n - d, 0), fill)
        return jnp.where(row >= d, pltpu.roll(x, d, 0), fill)

    d = 1
    while d < n:
        u = a * shifted(u, d, 0.0) + u
        a = a * shifted(a, d, 1.0)
        d *= 2
    return a, u


def _rglru_kernel(*refs, use_ctx):
    if use_ctx:
        (x_ref, y_ref, cw_ref, cb_ref, wa_ref, wx_ref, bg_ref, lam_ref, h0_ref, o_ref, wg_s) = refs
    else:
        (x_ref, y_ref, cw_ref, cb_ref, wa_ref, wx_ref, bg_ref, lam_ref, o_ref, last_ref, wg_s) = refs

    @pl.when(pl.program_id(0) == 0)
    def _():
        wg_s[...] = jnp.zeros_like(wg_s)
        for d in range(2):
            for g, w_ref in enumerate((wa_ref, wx_ref)):
                for nb in range(RG_BLOCKS):
                    col0 = (2 * d + g) * D_RNN + nb * RG_BW
                    wg_s[nb * RG_BW:(nb + 1) * RG_BW, col0:col0 + RG_BW] = w_ref[0, d, nb].astype(BF16)

    x = x_ref[...]
    cw = cw_ref[0]
    xr = cb_ref[0] + _shift_rows(x, 1) * cw[0:1] + x * cw[1:2] \
        + _shift_rows(x, -1) * cw[2:3] + _shift_rows(x, -2) * cw[3:4]
    gates = _sigmoid(jnp.dot(xr.astype(BF16), wg_s[...], preferred_element_type=F32) + bg_ref[0])
    n = x.shape[0]
    total = None
    lasts = []
    for d in range(2):
        r = gates[:, (2 * d) * D_RNN:(2 * d + 1) * D_RNN]
        i = gates[:, (2 * d + 1) * D_RNN:(2 * d + 2) * D_RNN]
        log_a = (-RG_C * _softplus(-lam_ref[0, d:d + 1, :])) * r
        a = jnp.exp(log_a)
        u = jnp.sqrt(1.0 - a * a) * (i * xr)
        a_cum, h = _linear_scan(a, u, reverse=(d == 1))
        if use_ctx:
            h = h + a_cum * h0_ref[0, 0, d:d + 1, :]
        else:
            lasts.append(h[0:1] if d == 1 else h[n - 1:n])
        total = h if total is None else total + h
    o_ref[...] = (jax.nn.gelu(y_ref[...]) * total).astype(BF16)
    if not use_ctx:
        last_ref[0] = jnp.concatenate(lasts, axis=0)


def _rglru(z, conv_w, conv_b, rg_wa, rg_wx, b_gate, lam, layer, n_batch, t_len, row0, state=None):
    use_ctx = state is not None
    rb0 = row0 // t_len
    blocks = pl.BlockSpec((1, 2, RG_BLOCKS, RG_BW, RG_BW), lambda b: (layer, 0, 0, 0, 0))
    in_specs = [pl.BlockSpec((t_len, D_RNN), lambda b: (rb0 + b, Z_RG // D_RNN)),
                pl.BlockSpec((t_len, D_RNN), lambda b: (rb0 + b, Z_RG // D_RNN + 1)),
                pl.BlockSpec((1, 4, D_RNN), lambda b: (layer, 0, 0)),
                pl.BlockSpec((1, 1, D_RNN), lambda b: (layer, 0, 0)),
                blocks, blocks,
                pl.BlockSpec((1, 1, 4 * D_RNN), lambda b: (layer, 0, 0)),
                pl.BlockSpec((1, 2, D_RNN), lambda b: (layer, 0, 0))]
    args = [z, z, conv_w, conv_b.reshape(-1, 1, D_RNN), rg_wa, rg_wx, b_gate, lam]
    out_specs = [pl.BlockSpec((t_len, D_RNN), lambda b: (b, 0))]
    out_shape = [jax.ShapeDtypeStruct((n_batch * t_len, D_RNN), BF16)]
    if use_ctx:
        in_specs.append(pl.BlockSpec((1, 1, 2, D_RNN), lambda b: (b, layer, 0, 0)))
        args.append(state)
    else:
        out_specs.append(pl.BlockSpec((1, 2, D_RNN), lambda b: (b, 0, 0)))
        out_shape.append(jax.ShapeDtypeStruct((n_batch, 2, D_RNN), F32))
    return pl.pallas_call(
        functools.partial(_rglru_kernel, use_ctx=use_ctx),
        grid=(n_batch,),
        in_specs=in_specs,
        out_specs=out_specs,
        out_shape=out_shape,
        scratch_shapes=[pltpu.VMEM((D_RNN, 4 * D_RNN), BF16)],
        compiler_params=_cparams(("arbitrary",), 56),
        name="rglru_ctx" if use_ctx else "rglru",
    )(*args)


def _cumsum_rows(g, reverse):
    n, w = g.shape
    m = n // ROW_BLOCK

    def scan(x, size, pos):
        d = 1
        while d < size:
            if reverse:
                x = x + jnp.where(pos < size - d, pltpu.roll(x, x.shape[0] - d, 0), 0.0)
            else:
                x = x + jnp.where(pos >= d, pltpu.roll(x, d, 0), 0.0)
            d *= 2
        return x

    g = scan(g, ROW_BLOCK, lax.broadcasted_iota(jnp.int32, g.shape, 0) & (ROW_BLOCK - 1))
    g3 = g.reshape(m, ROW_BLOCK, w)
    total = g3[:, 0, :] if reverse else g3[:, ROW_BLOCK - 1, :]
    before = scan(total, m, lax.broadcasted_iota(jnp.int32, total.shape, 0)) - total
    return (g3 + before[:, None, :]).reshape(n, w)


def _block_row_fn(gc, reverse):
    n, w = gc.shape
    g3 = gc.reshape(n // ROW_BLOCK, ROW_BLOCK, w)
    edge_row = 0 if reverse else ROW_BLOCK - 1
    edge = jnp.broadcast_to(g3[:, edge_row:edge_row + 1, :], g3.shape)
    sub = lax.broadcasted_iota(jnp.int32, g3.shape, 1)

    def at(blk):
        r = blk // 2 if reverse else blk // 2 - 1
        if blk > ROW_BLOCK:
            groups = blk // ROW_BLOCK
            e4 = edge.reshape(n // blk, groups, ROW_BLOCK, w)
            pick = r // ROW_BLOCK
            return jnp.broadcast_to(e4[:, pick:pick + 1], e4.shape).reshape(n, w)
        out = jnp.broadcast_to(g3[:, r:r + 1, :], g3.shape)
        for b0 in range(blk, ROW_BLOCK, blk):
            out = jnp.where(sub >= b0, jnp.broadcast_to(g3[:, b0 + r:b0 + r + 1, :], g3.shape), out)
        return out.reshape(n, w)

    return at


def _first_head(shape):
    lane = lax.broadcasted_iota(jnp.int32, shape, len(shape) - 1)
    return (lane & (LANES - 1)) < HG_DK


def _pair(x, p):
    return x[:, p * LANES:(p + 1) * LANES]


def _hgrn_gates(f_logit, lb):
    y = jnp.exp(-jnp.abs(f_logit))
    log_sig = jnp.minimum(f_logit, 0.0) - jnp.log(1.0 + y)
    a = jnp.log(lb)
    b = jnp.log1p(-lb) + log_sig
    log_f = jnp.maximum(a, b) + jnp.log(1.0 + jnp.exp(-jnp.abs(a - b)))
    k = (1.0 - lb) * (jnp.where(f_logit >= 0.0, y, 1.0) / (1.0 + y))
    return log_f * LOG2_E, k


def _hgrn_chunk_state(gc, k, v, reverse):
    n = gc.shape[0]
    g_tot = gc[0:1] if reverse else gc[n - 1:n]
    k_end = (k * jnp.exp2(jnp.minimum(g_tot - gc, 0.0))).astype(BF16)
    vb = v.astype(BF16)
    same_head = (_first_head((LANES, LANES))
                 == (lax.broadcasted_iota(jnp.int32, (LANES, LANES), 0) < HG_DK))
    ds = [jnp.where(same_head, _mm_tn(_pair(k_end, p), _pair(vb, p)), 0.0) for p in range(N_PAIRS)]
    return ds, jnp.exp2(g_tot)


def _decay_state(s, e_row):
    n = s.shape[0]
    eye = (lax.broadcasted_iota(jnp.int32, (n, n), 0) == lax.broadcasted_iota(jnp.int32, (n, n), 1))
    col = jnp.sum(jnp.where(eye, jnp.broadcast_to(e_row, (n, n)), 0.0), axis=1, keepdims=True)
    return s * col


def _hgrn_intra_scores(q, ks, gcs, txs):
    n = q.shape[0]
    row = lax.broadcasted_iota(jnp.int32, q.shape, 0)
    first = _first_head(q.shape)
    boundary = [_block_row_fn(gcs[0], False), _block_row_fn(gcs[1], True)]
    scores = [None] * HG_HEADS
    w = 1
    while w < n:
        blk = 2 * w
        second = (row & w) != 0
        e_f = jnp.exp2(-jnp.abs(gcs[0] - boundary[0](blk)))
        e_b = jnp.exp2(-jnp.abs(gcs[1] - boundary[1](blk)))
        q_f = jnp.where(second, q * e_f, 0.0).astype(BF16)
        q_b = jnp.where(second, 0.0, q * e_b).astype(BF16)
        k_f = jnp.where(second, 0.0, ks[0] * e_f)
        k_b = jnp.where(second, ks[1] * e_b, 0.0)
        k_heads = [(jnp.where(first, k_f, 0.0).astype(BF16), jnp.where(first, k_b, 0.0).astype(BF16)),
                   (jnp.where(first, 0.0, k_f).astype(BF16), jnp.where(first, 0.0, k_b).astype(BF16))]
        sel = (txs >= w) & (txs < blk)
        for p in range(N_PAIRS):
            qc = jnp.concatenate([_pair(q_f, p), _pair(q_b, p)], axis=1)
            for j in range(2):
                kc = jnp.concatenate([_pair(k_heads[j][0], p), _pair(k_heads[j][1], p)], axis=1)
                s = jnp.where(sel, _mm_nt(qc, kc), 0.0)
                h = 2 * p + j
                scores[h] = s if scores[h] is None else scores[h] + s
        w = blk
    return scores


def _hgrn_kernel(*refs, t_len, use_ctx):
    if use_ctx:
        (q_ref, ff_ref, fb_ref, i_ref, og_ref, lb_ref, ng_ref, s0_ref, o_ref, sin_ref, gk_ref) = refs
    else:
        (q_ref, ff_ref, fb_ref, i_ref, og_ref, lb_ref, ng_ref, o_ref, sl_ref) = refs
    c = HG_CHUNK
    n_chunks = t_len // c
    f_refs = (ff_ref, fb_ref)

    def load_dir(d, r0):
        log_f, k = _hgrn_gates(f_refs[d][pl.ds(r0, c), :], lb_ref[0, d:d + 1, :])
        return _cumsum_rows(log_f, reverse=(d == 1)), k

    if use_ctx:
        zero = jnp.zeros((HG_DK, HG_DV), F32)
        for d in range(2):
            states = []
            for p in range(N_PAIRS):
                top = jnp.concatenate([s0_ref[0, 0, d, 2 * p], zero], axis=1)
                bot = jnp.concatenate([zero, s0_ref[0, 0, d, 2 * p + 1]], axis=1)
                states.append(jnp.concatenate([top, bot], axis=0))
            order = range(n_chunks) if d == 0 else range(n_chunks - 1, -1, -1)
            for ci in order:
                for p in range(N_PAIRS):
                    sin_ref[ci, p, d * LANES:(d + 1) * LANES, :] = states[p]
                gc, k = load_dir(d, ci * c)
                gk_ref[d, 0, pl.ds(ci * c, c), :] = gc
                gk_ref[d, 1, pl.ds(ci * c, c), :] = k
                ds, e_tot = _hgrn_chunk_state(gc, k, i_ref[pl.ds(ci * c, c), :], reverse=(d == 1))
                states = [_decay_state(states[p], _pair(e_tot, p)) + ds[p] for p in range(N_PAIRS)]

    ti = lax.broadcasted_iota(jnp.int32, (c, c), 0)
    si = lax.broadcasted_iota(jnp.int32, (c, c), 1)
    txs = ti ^ si
    diag = ti == si

    def chunk_body(ci, carry):
        r0 = pl.multiple_of(ci * c, c)
        hq = q_ref[pl.ds(r0, c), :]
        q = hq * _sigmoid(hq)
        v = i_ref[pl.ds(r0, c), :]
        gcs, ks = [], []
        for d in range(2):
            if use_ctx:
                gc, k = gk_ref[d, 0, pl.ds(r0, c), :], gk_ref[d, 1, pl.ds(r0, c), :]
            else:
                gc, k = load_dir(d, r0)
            gcs.append(gc)
            ks.append(k)
        scores = _hgrn_intra_scores(q, ks, gcs, txs)
        first = _first_head(q.shape)
        k_sum = ks[0] + ks[1]
        v_heads = (jnp.where(first, v, 0.0).astype(BF16), jnp.where(first, 0.0, v).astype(BF16))
        k_heads = (jnp.where(first, k_sum, 0.0).astype(BF16), jnp.where(first, 0.0, k_sum).astype(BF16))
        q_bf = q.astype(BF16)
        if use_ctx:
            qe = [(q * jnp.exp2(gcs[d])).astype(BF16) for d in range(2)]
        else:
            for d in range(2):
                ds, _ = _hgrn_chunk_state(gcs[d], ks[d], v, reverse=(d == 1))
                for p in range(N_PAIRS):
                    sl_ref[0, d, 2 * p] = ds[p][:HG_DK, :HG_DV]
                    sl_ref[0, d, 2 * p + 1] = ds[p][HG_DK:, HG_DV:]
        outs = []
        for p in range(N_PAIRS):
            o = None
            for j in range(2):
                s_diag = jnp.where(diag, _mm_nt(_pair(q_bf, p), _pair(k_heads[j], p)), 0.0)
                t = _mm(scores[2 * p + j] + s_diag, _pair(v_heads[j], p))
                o = t if o is None else o + t
            if use_ctx:
                o = o + _mm(jnp.concatenate([_pair(qe[0], p), _pair(qe[1], p)], axis=1), sin_ref[ci, p])
            outs.append(_head_rms(o, ng_ref[0]))
        og = og_ref[pl.ds(r0, c), :]
        o_all = jnp.concatenate(outs, axis=1) * (og * _sigmoid(og))
        o_ref[pl.ds(r0, c), :] = o_all.astype(BF16)
        return carry

    if n_chunks == 1:
        chunk_body(0, 0)
    else:
        lax.fori_loop(0, n_chunks, chunk_body, 0)


def _hgrn(z, lb, norm_g, layer, n_batch, t_len, row0, state=None):
    use_ctx = state is not None
    rb0 = row0 // t_len
    cb = Z_HG // HG_W

    def zspec(k):
        return pl.BlockSpec((t_len, HG_W), lambda b: (rb0 + b, cb + k))

    in_specs = [zspec(0), zspec(1), zspec(2), zspec(3), zspec(4),
                pl.BlockSpec((1, 2, HG_W), lambda b: (layer, 0, 0)),
                pl.BlockSpec((1, 1, LANES), lambda b: (layer, 0, 0))]
    args = [z, z, z, z, z, lb, jnp.tile(norm_g, (1, LANES // HG_DV)).reshape(-1, 1, LANES)]
    out_specs = [pl.BlockSpec((t_len, HG_W), lambda b: (b, 0))]
    out_shape = [jax.ShapeDtypeStruct((n_batch * t_len, HG_W), BF16)]
    scratch = []
    if use_ctx:
        in_specs.append(pl.BlockSpec((1, 1, 2, HG_HEADS, HG_DK, HG_DV), lambda b: (b, layer, 0, 0, 0, 0)))
        args.append(state)
        scratch.append(pltpu.VMEM((t_len // HG_CHUNK, N_PAIRS, 2 * LANES, LANES), F32))
        scratch.append(pltpu.VMEM((2, 2, t_len, HG_W), F32))
    else:
        out_specs.append(pl.BlockSpec((1, 2, HG_HEADS, HG_DK, HG_DV), lambda b: (b, 0, 0, 0, 0)))
        out_shape.append(jax.ShapeDtypeStruct((n_batch, 2, HG_HEADS, HG_DK, HG_DV), F32))
    return pl.pallas_call(
        functools.partial(_hgrn_kernel, t_len=t_len, use_ctx=use_ctx),
        grid=(n_batch,),
        in_specs=in_specs,
        out_specs=out_specs,
        out_shape=out_shape,
        scratch_shapes=scratch,
        compiler_params=_cparams(("arbitrary",), 56),
        name="hgrn_ctx" if use_ctx else "hgrn",
    )(*args)


def _merge_kernel(ap_ref, as_ref, rp_ref, rs_ref, hp_ref, hs_ref, g0_ref, g1_ref, g2_ref, x_ref, mod_ref,
                  wa_ref, wr_ref, wh_ref, wo_ref, o_ref, wa_s, wr_s, wh_s, wo_s, *, prompt_tiles):
    @pl.when(pl.program_id(0) == 0)
    def _():
        wa_s[...] = wa_ref[0].astype(BF16)
        wr_s[...] = wr_ref[0].astype(BF16)
        wh_s[...] = wh_ref[0].astype(BF16)
        wo_s[...] = wo_ref[0].astype(BF16)

    is_prompt = pl.program_id(0) < prompt_tiles

    def proj(p_ref, s_ref, w_s):
        return jnp.dot(jnp.where(is_prompt, p_ref[...], s_ref[...]), w_s[...], preferred_element_type=F32)

    merged = (_sigmoid(g0_ref[...]) * proj(ap_ref, as_ref, wa_s)
              + _sigmoid(g1_ref[...]) * proj(rp_ref, rs_ref, wr_s)
              + _sigmoid(g2_ref[...]) * proj(hp_ref, hs_ref, wh_s))
    out = jnp.dot(merged.astype(BF16), wo_s[...], preferred_element_type=F32)
    o_ref[...] = x_ref[...] + mod_ref[0, 2:3, :] * out


def _merge_project(mixed, z, x, modseg, w_attn_o, w_rnn_o, w_hgrn_o, w_out, layer):
    n_tok = x.shape[0]
    tm = 512
    per_seg = SEG_ROWS // tm
    prompt_tiles = mixed[0][0].shape[0] // tm
    half_p = pl.BlockSpec((tm, ATTN_W), lambda i: (jnp.minimum(i, prompt_tiles - 1), 0))
    half_s = pl.BlockSpec((tm, ATTN_W), lambda i: (jnp.maximum(i - prompt_tiles, 0), 0))
    full = pl.BlockSpec((tm, D_MODEL), lambda i: (i, 0))

    def gspec(k):
        return pl.BlockSpec((tm, D_MODEL), lambda i: (i, Z_GL // D_MODEL + k))

    def wspec(rows):
        return pl.BlockSpec((1, rows, D_MODEL), lambda i: (layer, 0, 0))

    return pl.pallas_call(
        functools.partial(_merge_kernel, prompt_tiles=prompt_tiles),
        grid=(n_tok // tm,),
        in_specs=[half_p, half_s, half_p, half_s, half_p, half_s, gspec(0), gspec(1), gspec(2), full,
                  pl.BlockSpec((1, N_MOD, D_MODEL), lambda i: (i // per_seg, 0, 0)),
                  wspec(ATTN_W), wspec(D_RNN), wspec(HG_W), wspec(D_MODEL)],
        out_specs=full,
        out_shape=jax.ShapeDtypeStruct((n_tok, D_MODEL), F32),
        scratch_shapes=[pltpu.VMEM((ATTN_W, D_MODEL), BF16), pltpu.VMEM((D_RNN, D_MODEL), BF16),
                        pltpu.VMEM((HG_W, D_MODEL), BF16), pltpu.VMEM((D_MODEL, D_MODEL), BF16)],
        compiler_params=_cparams(("arbitrary",), 56),
        name="merge_project",
    )(*[part for pair in mixed for part in pair], z, z, z, x, modseg, w_attn_o, w_rnn_o, w_hgrn_o, w_out)


def _router_kernel(x_ref, g_ref, mod_ref, wr_ref, br_ref, h_ref, slot_ref, wt_ref, meta_ref, run_ref):
    @pl.when(pl.program_id(0) == 0)
    def _():
        run_ref[...] = jnp.zeros_like(run_ref)

    h = _modulated_norm(x_ref[...], g_ref[0], mod_ref[0, 4:5, :], mod_ref[0, 3:4, :])
    h_ref[...] = h
    h_hi = h.astype(BF16)
    h_lo = (h - h_hi.astype(F32)).astype(BF16)
    w = wr_ref[0]
    w_hi = w.astype(BF16)
    w_lo = (w - w_hi.astype(F32)).astype(BF16)
    logits = (jnp.dot(h_hi, w_hi, preferred_element_type=F32) + jnp.dot(h_hi, w_lo, preferred_element_type=F32)
              + jnp.dot(h_lo, w_hi, preferred_element_type=F32)) + br_ref[0]
    tm = logits.shape[0]
    lane = lax.broadcasted_iota(jnp.int32, logits.shape, 1)
    ids, vals = [], []
    chosen = jnp.zeros(logits.shape, jnp.bool_)
    work = logits
    for _ in range(TOP_K):
        m = jnp.max(work, axis=-1, keepdims=True)
        idx = jnp.min(jnp.where(work == m, lane, LANES), axis=-1, keepdims=True)
        hit = lane == idx
        chosen = chosen | hit
        work = jnp.where(hit, -jnp.inf, work)
        ids.append(idx)
        vals.append(m)
    exps = [jnp.exp(v - vals[0]) for v in vals]
    den = exps[0] + exps[1] + exps[2] + exps[3]
    ind = jnp.where(chosen, 1.0, 0.0)
    r_i = lax.broadcasted_iota(jnp.int32, (tm, tm), 0)
    c_i = lax.broadcasted_iota(jnp.int32, (tm, tm), 1)
    count = jnp.sum(ind, axis=0, keepdims=True)
    padded = jnp.floor((count + (ROW_BLOCK - 1)) * (1.0 / ROW_BLOCK)) * ROW_BLOCK
    incl = jnp.broadcast_to(padded, (ROW_BLOCK, LANES))
    lane8 = lax.broadcasted_iota(jnp.int32, (ROW_BLOCK, LANES), 1)
    d = 1
    while d < LANES:
        incl = incl + jnp.where(lane8 >= d, pltpu.roll(incl, d, 1), 0.0)
        d *= 2
    local_start = incl[0:1] - padded
    place = _mm(jnp.where(c_i < r_i, 1.0, 0.0), ind) + local_start
    slot_out = jnp.zeros(logits.shape, jnp.int32)
    wt_out = jnp.zeros(logits.shape, F32)
    for k in range(TOP_K):
        slot = jnp.sum(jnp.where(lane == ids[k], place, 0.0), axis=-1, keepdims=True)
        slot_out = jnp.where(lane == k, slot.astype(jnp.int32), slot_out)
        wt_out = jnp.where(lane == k, exps[k] / den, wt_out)
    slot_ref[...] = slot_out
    wt_ref[...] = wt_out
    row8 = lax.broadcasted_iota(jnp.int32, (ROW_BLOCK, LANES), 0)
    table = jnp.where(row8 == 0, local_start, jnp.where(row8 == 1, padded, run_ref[0:1, :]))
    meta_ref[0] = jnp.where(row8 < 3, table, 0.0).astype(jnp.int32)
    run_ref[...] = run_ref[...] + padded


def _router(x, norm_g, modseg, w_router, b_router, layer):
    n_tok = x.shape[0]
    tm = TOKEN_TILE
    per_seg = SEG_ROWS // tm
    depth = w_router.shape[0]
    w_pad = jnp.zeros((depth, D_MODEL, LANES), F32).at[:, :, :N_EXPERTS].set(w_router)
    b_pad = jnp.full((depth, 1, LANES), NEG_BIG, F32).at[:, 0, :N_EXPERTS].set(b_router)
    full = pl.BlockSpec((tm, D_MODEL), lambda i: (i, 0))
    small = pl.BlockSpec((tm, LANES), lambda i: (i, 0))
    return pl.pallas_call(
        _router_kernel,
        grid=(n_tok // tm,),
        in_specs=[full,
                  pl.BlockSpec((1, 1, D_MODEL), lambda i: (layer, 0, 0)),
                  pl.BlockSpec((1, N_MOD, D_MODEL), lambda i: (i // per_seg, 0, 0)),
                  pl.BlockSpec((1, D_MODEL, LANES), lambda i: (layer, 0, 0)),
                  pl.BlockSpec((1, 1, LANES), lambda i: (layer, 0, 0))],
        out_specs=[full, small, small, pl.BlockSpec((1, ROW_BLOCK, LANES), lambda i: (i, 0, 0))],
        out_shape=[jax.ShapeDtypeStruct((n_tok, D_MODEL), F32),
                   jax.ShapeDtypeStruct((n_tok, LANES), jnp.int32),
                   jax.ShapeDtypeStruct((n_tok, LANES), F32),
                   jax.ShapeDtypeStruct((n_tok // tm, ROW_BLOCK, LANES), jnp.int32)],
        scratch_shapes=[pltpu.VMEM((ROW_BLOCK, LANES), F32)],
        compiler_params=_cparams(("arbitrary",), 40),
        name="router",
    )(x, norm_g.reshape(-1, 1, D_MODEL), modseg, w_pad, b_pad)


COPY_BLOCKS = (BIG_BLOCK, 2 * ROW_BLOCK, ROW_BLOCK)
COPY_CAPS = (LOCAL_ROWS // COPY_BLOCKS[0],) + tuple(
    N_EXPERTS * (COPY_BLOCKS[c - 1] // COPY_BLOCKS[c] - 1) for c in range(1, len(COPY_BLOCKS)))
COPY_LIST_LEN = 1024
_COPY_AT = tuple(len(COPY_BLOCKS) + 2 * sum(COPY_CAPS[:c]) for c in range(len(COPY_BLOCKS)))
assert _COPY_AT[-1] + 2 * COPY_CAPS[-1] <= COPY_LIST_LEN


def _copy_lists(local_start, padded, sorted_start):

    def expand(count, local0, sorted0, step, cap):
        last = _prefix_sum(count)[:, None, :]
        first = last - count[:, None, :]
        f = jnp.arange(cap, dtype=jnp.int32)[None, :, None]
        mine = ((first <= f) & (f < last)).astype(jnp.int32)
        offset = step * (f - first)
        return (last[:, 0, -1:], jnp.sum(mine * (local0[:, None, :] + offset), axis=2),
                jnp.sum(mine * (sorted0[:, None, :] + offset), axis=2))

    counts, lists = [], []
    done = jnp.zeros_like(padded)
    for rows, cap in zip(COPY_BLOCKS, COPY_CAPS):
        n_copies = (padded - done) // rows
        cnt, loc, srt = expand(n_copies, local_start + done, sorted_start + done, rows, cap)
        counts.append(cnt)
        lists += [loc, srt]
        done = done + n_copies * rows
    parts = counts + lists
    used = sum(p.shape[1] for p in parts)
    parts.append(jnp.zeros((padded.shape[0], COPY_LIST_LEN - used), jnp.int32))
    return jnp.concatenate(parts, axis=1).astype(jnp.int32).reshape(-1)


def _for_each_block(list_ref, fn):
    for c, (rows, cap) in enumerate(zip(COPY_BLOCKS, COPY_CAPS)):
        def copy(j, carry, rows=rows, cap=cap, at=_COPY_AT[c]):
            fn(pl.multiple_of(list_ref[at + j], ROW_BLOCK),
               pl.multiple_of(list_ref[at + cap + j], ROW_BLOCK), rows)
            return carry

        lax.fori_loop(0, list_ref[c], copy, 0)


def _slot_matrix(slot_ref, values):
    col = lax.broadcasted_iota(jnp.int32, (TOKEN_TILE, LOCAL_ROWS), 1)
    out = jnp.zeros((TOKEN_TILE, LOCAL_ROWS), F32)
    for k in range(TOP_K):
        out = jnp.where(col == slot_ref[:, k:k + 1], values[k], out)
    return out.astype(BF16)


def _dispatch_kernel(meta_ref, prev_meta_ref, tail_ref, slot_ref, h_ref, xs_ref, local_ref, zero_ref,
                     sem, block_sem):
    i = pl.program_id(0)
    cur = i % 2

    @pl.when(i == 0)
    def _():
        zero_ref[...] = jnp.zeros_like(zero_ref)

        def tails(fn):
            def group(e, carry):
                def block(j, c):
                    fn(pl.multiple_of(tail_ref[e] + j * ROW_BLOCK, ROW_BLOCK))
                    return c
                return lax.fori_loop(0, tail_ref[N_EXPERTS + e], block, carry)
            lax.fori_loop(0, N_EXPERTS, group, 0)

        def zero_copy(row):
            return pltpu.make_async_copy(zero_ref.at[pl.ds(0, ROW_BLOCK)],
                                         xs_ref.at[pl.ds(row, ROW_BLOCK)], sem)

        tails(lambda row: zero_copy(row).start())
        tails(lambda row: zero_copy(row).wait())

        def unused(fn):
            def tile(j, c):
                fn(pl.multiple_of(tail_ref[2 * N_EXPERTS] + j * EXPERT_TILE, EXPERT_TILE))
                return c
            lax.fori_loop(0, tail_ref[2 * N_EXPERTS + 1], tile, 0)

        def zero_tile(row):
            return pltpu.make_async_copy(zero_ref, xs_ref.at[pl.ds(row, EXPERT_TILE)], sem)

        unused(lambda row: zero_tile(row).start())
        unused(lambda row: zero_tile(row).wait())

    onehot = _slot_matrix(slot_ref, [1.0] * TOP_K)
    local_ref[cur] = _mm_tn(onehot, h_ref[...])

    def block_copy(buf):
        return lambda local_row, sorted_row, rows: pltpu.make_async_copy(
            local_ref.at[buf, pl.ds(local_row, rows)], xs_ref.at[pl.ds(sorted_row, rows)],
            block_sem.at[buf])

    _for_each_block(meta_ref, lambda a, b, n: block_copy(cur)(a, b, n).start())

    @pl.when(i > 0)
    def _():
        _for_each_block(prev_meta_ref, lambda a, b, n: block_copy(1 - cur)(a, b, n).wait())

    @pl.when(i == pl.num_programs(0) - 1)
    def _():
        _for_each_block(meta_ref, lambda a, b, n: block_copy(cur)(a, b, n).wait())


def _dispatch(h, slots, meta, tails, n_rows):
    n_tok = h.shape[0]
    return pl.pallas_call(
        _dispatch_kernel,
        grid=(n_tok // TOKEN_TILE,),
        in_specs=[pl.BlockSpec((COPY_LIST_LEN,), lambda i: (i,), memory_space=pltpu.SMEM),
                  pl.BlockSpec((COPY_LIST_LEN,), lambda i: (jnp.maximum(i - 1, 0),), memory_space=pltpu.SMEM),
                  pl.BlockSpec((LANES,), lambda i: (0,), memory_space=pltpu.SMEM),
                  pl.BlockSpec((TOKEN_TILE, LANES), lambda i: (i, 0)),
                  pl.BlockSpec((TOKEN_TILE, D_MODEL), lambda i: (i, 0))],
        out_specs=pl.BlockSpec(memory_space=pl.ANY),
        out_shape=jax.ShapeDtypeStruct((n_rows, D_MODEL), F32),
        scratch_shapes=[pltpu.VMEM((2, LOCAL_ROWS, D_MODEL), F32), pltpu.VMEM((EXPERT_TILE, D_MODEL), F32),
                        pltpu.SemaphoreType.DMA(()), pltpu.SemaphoreType.DMA((2,))],
        compiler_params=_cparams(("arbitrary",), 48),
        name="dispatch",
    )(meta, meta, tails, slots, h)


def _combine_kernel(meta_ref, next_meta_ref, slot_ref, wt_ref, ys_ref, x_ref, mod_ref, *rest, split_tiles):
    local_ref, sem = rest[-2:]
    i = pl.program_id(0)
    cur = i % 2

    def block_copy(buf):
        return lambda local_row, sorted_row, rows: pltpu.make_async_copy(
            ys_ref.at[pl.ds(sorted_row, rows)], local_ref.at[buf, pl.ds(local_row, rows)], sem.at[buf])

    def start_gather(table_ref, buf):
        local_ref[buf, TOKEN_TILE * TOP_K:, :] = jnp.zeros((LOCAL_ROWS - TOKEN_TILE * TOP_K, D_MODEL), F32)
        _for_each_block(table_ref, lambda a, b, n: block_copy(buf)(a, b, n).start())

    @pl.when(i == 0)
    def _():
        start_gather(meta_ref, 0)

    @pl.when(i + 1 < pl.num_programs(0))
    def _():
        start_gather(next_meta_ref, 1 - cur)

    _for_each_block(meta_ref, lambda a, b, n: block_copy(cur)(a, b, n).wait())
    weights = _slot_matrix(slot_ref, [wt_ref[:, k:k + 1] for k in range(TOP_K)])
    acc = jnp.dot(weights, local_ref[cur].astype(BF16), preferred_element_type=F32)
    result = x_ref[...] + mod_ref[0, 5:6, :] * acc
    if split_tiles is None:
        rest[0][...] = result
    else:
        @pl.when(i < split_tiles)
        def _():
            rest[0][...] = result

        @pl.when(i >= split_tiles)
        def _():
            rest[1][...] = result


def _combine(ys, slots, meta, wts, x, modseg, split_tokens=None):
    n_tok = x.shape[0]
    per_seg = SEG_ROWS // TOKEN_TILE
    full = pl.BlockSpec((TOKEN_TILE, D_MODEL), lambda i: (i, 0))
    small = pl.BlockSpec((TOKEN_TILE, LANES), lambda i: (i, 0))
    if split_tokens is None:
        split_tiles = None
        out_specs = full
        out_shape = jax.ShapeDtypeStruct((n_tok, D_MODEL), F32)
    else:
        split_tiles = split_tokens // TOKEN_TILE
        out_specs = [pl.BlockSpec((TOKEN_TILE, D_MODEL), lambda i: (jnp.minimum(i, split_tiles - 1), 0)),
                     pl.BlockSpec((TOKEN_TILE, D_MODEL), lambda i: (jnp.maximum(i - split_tiles, 0), 0))]
        out_shape = [jax.ShapeDtypeStruct((split_tokens, D_MODEL), F32),
                     jax.ShapeDtypeStruct((n_tok - split_tokens, D_MODEL), F32)]
    return pl.pallas_call(
        functools.partial(_combine_kernel, split_tiles=split_tiles),
        grid=(n_tok // TOKEN_TILE,),
        in_specs=[pl.BlockSpec((COPY_LIST_LEN,), lambda i: (i,), memory_space=pltpu.SMEM),
                  pl.BlockSpec((COPY_LIST_LEN,), lambda i: (jnp.minimum(i + 1, n_tok // TOKEN_TILE - 1),),
                               memory_space=pltpu.SMEM),
                  small, small,
                  pl.BlockSpec(memory_space=pl.ANY),
                  full,
                  pl.BlockSpec((1, N_MOD, D_MODEL), lambda i: (i // per_seg, 0, 0))],
        out_specs=out_specs,
        out_shape=out_shape,
        scratch_shapes=[pltpu.VMEM((2, LOCAL_ROWS, D_MODEL), F32), pltpu.SemaphoreType.DMA((2,))],
        compiler_params=_cparams(("arbitrary",), 56),
        name="combine",
    )(meta, meta, slots, wts, ys, x, modseg)


def _expert_kernel(te_ref, first_ref, slot_ref, next_ref, na_ref, x_ref, wgu_hbm, bgu_ref, wd_hbm, bd_ref,
                   o_ref, wgu_f, wd_f, wgu_s, wd_s, sem, *, layer):
    i = pl.program_id(0)
    active = i < na_ref[0]

    def fetch(e, s):
        return (pltpu.make_async_copy(wgu_hbm.at[layer, e], wgu_f.at[s], sem.at[0, s]),
                pltpu.make_async_copy(wd_hbm.at[layer, e], wd_f.at[s], sem.at[1, s]))

    @pl.when(jnp.logical_and(i == 0, first_ref[0] == 1))
    def _():
        for cp in fetch(te_ref[0], 0):
            cp.start()

    @pl.when(first_ref[i] == 1)
    def _():
        s = slot_ref[i]
        for cp in fetch(te_ref[i], s):
            cp.wait()
        wgu_s[...] = wgu_f[s].astype(BF16)
        wd_s[...] = wd_f[s].astype(BF16)

        @pl.when(next_ref[i] >= 0)
        def _():
            for cp in fetch(next_ref[i], 1 - s):
                cp.start()

    @pl.when(active)
    def _():
        gu = jnp.dot(x_ref[...].astype(BF16), wgu_s[...], preferred_element_type=F32) + bgu_ref[0, 0]
        gate = jnp.minimum(gu[:, :D_FF], SWIGLU_LIMIT)
        up = jnp.clip(gu[:, D_FF:], -SWIGLU_LIMIT, SWIGLU_LIMIT)
        act = gate * _sigmoid(SWIGLU_ALPHA * gate) * (up + 1.0)
        o_ref[...] = jnp.dot(act.astype(BF16), wd_s[...], preferred_element_type=F32) + bd_ref[0, 0]

    @pl.when(jnp.logical_not(active))
    def _():
        o_ref[...] = jnp.zeros_like(o_ref)


def _experts(xs, tile_expert, ends, n_active, w_gu, b_gu, w_down, b_down, layer):
    n_rows = xs.shape[0]
    depth = w_gu.shape[0]
    tm = EXPERT_TILE
    n_tiles = n_rows // tm
    tile = jnp.arange(n_tiles, dtype=jnp.int32)
    active = tile < n_active[0]
    first = (active & ((tile == 0) | (tile_expert != jnp.roll(tile_expert, 1)))).astype(jnp.int32)
    slot = (_prefix_sum(first) - 1) % 2
    next_tile = ends[tile_expert] // tm
    next_expert = jnp.where(next_tile < n_active[0],
                            tile_expert[jnp.minimum(next_tile, n_tiles - 1)], -1).astype(jnp.int32)

    def tiled(shape, index):
        return pl.BlockSpec(shape, lambda i, te, fi, sl, nx, na: index(i, te, na))

    grid_spec = pltpu.PrefetchScalarGridSpec(
        num_scalar_prefetch=5,
        grid=(n_tiles,),
        in_specs=[tiled((tm, D_MODEL), lambda i, te, na: (jnp.where(i < na[0], i, 0), 0)),
                  pl.BlockSpec(memory_space=pl.ANY),
                  tiled((1, 1, 1, 2 * D_FF), lambda i, te, na: (layer, te[i], 0, 0)),
                  pl.BlockSpec(memory_space=pl.ANY),
                  tiled((1, 1, 1, D_MODEL), lambda i, te, na: (layer, te[i], 0, 0))],
        out_specs=tiled((tm, D_MODEL), lambda i, te, na: (i, 0)),
        scratch_shapes=[pltpu.VMEM((2, D_MODEL, 2 * D_FF), F32), pltpu.VMEM((2, D_FF, D_MODEL), F32),
                        pltpu.VMEM((D_MODEL, 2 * D_FF), BF16), pltpu.VMEM((D_FF, D_MODEL), BF16),
                        pltpu.SemaphoreType.DMA((2, 2))],
    )
    return pl.pallas_call(
        functools.partial(_expert_kernel, layer=layer),
        grid_spec=grid_spec,
        out_shape=jax.ShapeDtypeStruct((n_rows, D_MODEL), F32),
        compiler_params=_cparams(("arbitrary",), 56),
        name="experts",
    )(tile_expert, first, slot.astype(jnp.int32), next_expert, n_active, xs, w_gu,
      b_gu.reshape(depth, N_EXPERTS, 1, 2 * D_FF), w_down, b_down.reshape(depth, N_EXPERTS, 1, D_MODEL))


def _moe(x, norm_g, modseg, w_router, b_router, w_gu, b_gu, w_down, b_down, layer, split_tokens=None):
    n_tok = x.shape[0]
    h, slots, wts, table = _router(x, norm_g, modseg, w_router, b_router, layer)
    tm = EXPERT_TILE
    n_tiles = n_tok // TOKEN_TILE
    n_rows = n_tok * TOP_K + n_tiles * N_EXPERTS * ROW_BLOCK + N_EXPERTS * tm
    n_rows = -(-n_rows // tm) * tm
    local_start = table[:, 0, :N_EXPERTS]
    padded = table[:, 1, :N_EXPERTS]
    earlier = table[:, 2, :N_EXPERTS]
    used = earlier[-1] + padded[-1]
    size = ((used + tm - 1) // tm) * tm
    ends = _prefix_sum(size)
    starts = ends - size
    sorted_start = starts[None, :] + earlier
    meta = _copy_lists(local_start, padded, sorted_start)
    tails = jnp.concatenate([starts + used, (size - used) // ROW_BLOCK,
                             ends[-1:], (n_rows - ends[-1:]) // tm,
                             jnp.zeros((LANES - 2 * N_EXPERTS - 2,), jnp.int32)])
    tile_row0 = jnp.arange(n_rows // tm, dtype=jnp.int32) * tm
    tile_expert = jnp.minimum(jnp.sum((ends[None, :] <= tile_row0[:, None]).astype(jnp.int32), axis=1),
                              N_EXPERTS - 1)
    n_active = (ends[-1:] // tm).astype(jnp.int32)
    xs = _dispatch(h, slots, meta, tails, n_rows)
    ys = _experts(xs, tile_expert, ends, n_active, w_gu, b_gu, w_down, b_down, layer)
    return _combine(ys, slots, meta, wts, x, modseg, split_tokens)


def _rope_tables(n_tok):
    rows = n_tok // GRID_W
    row = jnp.repeat(jnp.arange(rows, dtype=F32), GRID_W)
    col = jnp.tile(jnp.arange(GRID_W, dtype=F32), rows)
    quarter = HEAD_DIM // 4
    inv_freq = ROPE_BASE ** (-jnp.arange(quarter, dtype=F32) / quarter)
    ang_r = row[:, None] * inv_freq
    ang_c = col[:, None] * inv_freq
    ang = jnp.concatenate([ang_r, ang_r, ang_c, ang_c], axis=-1)
    cos, sin = jnp.cos(ang), jnp.sin(ang)
    first = (jnp.arange(HEAD_DIM) % (2 * quarter)) < quarter
    sin_a = jnp.where(first, -sin, 0.0)
    sin_b = jnp.where(first, 0.0, sin)
    return tuple(jnp.tile(t, (1, LANES // HEAD_DIM)) for t in (cos, sin_a, sin_b))


def kernel(x_prompt, x_sample, c, cache_k, cache_v, state_rglru, state_hgrn, c_ctx, w_mod, b_mod, norm1_g, norm2_g, w_in, q_norm_g, k_norm_g, conv_w, conv_b, rg_wa, rg_ba, rg_wx, rg_bx, rg_lambda, hgrn_lb_logits, hgrn_norm_g, w_attn_o, w_rnn_o, w_hgrn_o, w_out, w_router, b_router, w_gu, b_gu, w_down, b_down):
    depth = w_mod.shape[0]
    n_p, t_p = x_prompt.shape[0], x_prompt.shape[1]
    n_s, t_s = x_sample.shape[0], x_sample.shape[1]
    tok_p = n_p * t_p
    tok_s = n_s * t_s
    assert tok_p % SEG_ROWS == 0 and t_s == SEG_ROWS and SEG_ROWS % t_p == 0

    x = jnp.concatenate([x_prompt.reshape(tok_p, D_MODEL), x_sample.reshape(tok_s, D_MODEL)], axis=0)

    lb_all = jnp.moveaxis(_prefix_sum(jnp.moveaxis(jax.nn.softmax(hgrn_lb_logits.astype(F32), axis=0), 0, -1)),
                          -1, 0)
    lb_all = lb_all - lb_all[:1]
    rope = _rope_tables(t_s)

    assert 1 + n_s <= ROW_BLOCK
    cond8 = jnp.zeros((ROW_BLOCK, D_MODEL), F32).at[0].set(c_ctx).at[1:1 + n_s].set(c)
    mod = _modulation(cond8, w_mod, b_mod)
    seg_ids = jnp.asarray([0] * (tok_p // SEG_ROWS) + [1 + b for b in range(n_s)], jnp.int32)
    modsegs = mod[:, seg_ids].reshape(depth, seg_ids.shape[0], N_MOD, D_MODEL)

    b_gate = jnp.stack([rg_ba, rg_bx], axis=2).reshape(depth, 1, 4 * D_RNN)

    new_k, new_v, new_hr, new_s = [], [], [], []
    for l in range(depth):
        modseg = modsegs[l]
        z = _input_projection(x, norm1_g, modseg, w_in, l)
        attn_p, k_l, v_l = _attention(z, q_norm_g, k_norm_g, l, n_p, t_p, 0)
        (attn_s,) = _attention(z, q_norm_g, k_norm_g, l, n_s, t_s, tok_p, rope=rope,
                               cache=(cache_k, cache_v))
        rnn_p, hr_l = _rglru(z, conv_w, conv_b, rg_wa, rg_wx, b_gate, rg_lambda, l, n_p, t_p, 0)
        (rnn_s,) = _rglru(z, conv_w, conv_b, rg_wa, rg_wx, b_gate, rg_lambda, l, n_s, t_s, tok_p,
                          state=state_rglru)
        hg_p, s_l = _hgrn(z, lb_all, hgrn_norm_g, l, n_p, t_p, 0)
        (hg_s,) = _hgrn(z, lb_all, hgrn_norm_g, l, n_s, t_s, tok_p, state=state_hgrn)
        x = _merge_project(((attn_p, attn_s), (rnn_p, rnn_s), (hg_p, hg_s)), z, x, modseg,
                           w_attn_o, w_rnn_o, w_hgrn_o, w_out, l)
        x = _moe(x, norm2_g, modseg, w_router, b_router, w_gu, b_gu, w_down, b_down, l,
                 split_tokens=tok_p if l == depth - 1 else None)
        new_k.append(k_l.reshape(n_p, t_p, N_KV_HEADS, HEAD_DIM))
        new_v.append(v_l.reshape(n_p, t_p, N_KV_HEADS, HEAD_DIM))
        new_hr.append(hr_l)
        new_s.append(s_l)

    y_prompt = x[0].reshape(n_p, t_p, D_MODEL)
    y_sample = x[1].reshape(n_s, t_s, D_MODEL)
    return (y_prompt, y_sample, jnp.stack(new_k, axis=1), jnp.stack(new_v, axis=1),
            jnp.stack(new_hr, axis=1), jnp.stack(new_s, axis=1))
```

```python
import functools

import jax
import jax.numpy as jnp
from jax import lax
from jax.experimental import pallas as pl
from jax.experimental.pallas import tpu as pltpu

F32 = jnp.float32
BF16 = jnp.bfloat16

D_MODEL = 1024
GRID_W = 64
HEAD_DIM = 64
N_HEADS = 8
N_KV_HEADS = 2
KV_GROUP = N_HEADS // N_KV_HEADS
ATTN_W = N_HEADS * HEAD_DIM
KV_W = N_KV_HEADS * HEAD_DIM
ROPE_BASE = 10000.0
D_RNN = 512
RG_BLOCKS = 8
RG_BW = D_RNN // RG_BLOCKS
RG_C = 8.0
HG_HEADS = 8
HG_DK = 64
HG_DV = 64
HG_W = HG_HEADS * HG_DK
N_PAIRS = HG_W // 128
N_EXPERTS = 32
TOP_K = 4
D_FF = D_MODEL
SWIGLU_LIMIT = 7.0
SWIGLU_ALPHA = 1.702
N_MOD = 6
EPS = 1e-6
IN_W = ATTN_W + 2 * KV_W + 2 * D_RNN + 5 * HG_W + 3 * D_MODEL

LANES = 128
SEG_ROWS = 1024
COL_TILE = 256
HG_CHUNK = 256
Q_TILE = 256
EXPERT_TILE = 384
BIG_BLOCK = 64
TOKEN_TILE = 512
ROW_BLOCK = 8
LOCAL_ROWS = TOKEN_TILE * TOP_K + N_EXPERTS * ROW_BLOCK
NEG_BIG = -1e30
LOG2_E = 1.4426950408889634

_SRC_TILES = tuple(range(17, 29)) + tuple(range(7, 17)) + tuple(range(3, 7)) + (0, 1, 2)
Z_GL = 0
Z_HG = 3 * D_MODEL
Z_RG = Z_HG + 5 * HG_W
Z_Q = Z_RG + 2 * D_RNN
Z_K = Z_Q + ATTN_W
Z_V = Z_K + KV_W
Z_W = -(-IN_W // (2 * COL_TILE)) * (2 * COL_TILE)


def _cparams(sem, vmem_mb):
    return pltpu.CompilerParams(dimension_semantics=sem, vmem_limit_bytes=vmem_mb * 1024 * 1024)


def _prefix_sum(x):
    n = x.shape[-1]
    upto = (jnp.arange(n)[None, :] <= jnp.arange(n)[:, None]).astype(x.dtype)
    return jnp.sum(x[..., None, :] * upto, axis=-1)


def _mm(a, b):
    return jnp.dot(a.astype(BF16), b.astype(BF16), preferred_element_type=F32)


def _mm_nt(a, b):
    return lax.dot_general(a.astype(BF16), b.astype(BF16), (((1,), (1,)), ((), ())),
                           preferred_element_type=F32)


def _mm_tn(a, b):
    return lax.dot_general(a.astype(BF16), b.astype(BF16), (((0,), (0,)), ((), ())),
                           preferred_element_type=F32)


def _sigmoid(x):
    return 1.0 / (1.0 + jnp.exp(-x))


def _softplus(x):
    return jnp.maximum(x, 0.0) + jnp.log1p(jnp.exp(-jnp.abs(x)))


def _modulated_norm(x, g, scale, shift):
    y = x * lax.rsqrt(jnp.mean(x * x, axis=-1, keepdims=True) + EPS)
    return (y * g) * (1.0 + scale) + shift


def _mod_kernel(c_ref, w_ref, b_ref, o_ref):
    c = c_ref[...]
    s = c * _sigmoid(c)
    o_ref[0] = _mm(s, w_ref[0]) + b_ref[0]


def _modulation(cond8, w_mod, b_mod):
    depth = w_mod.shape[0]
    tn = 1024
    return pl.pallas_call(
        _mod_kernel,
        grid=(depth, N_MOD * D_MODEL // tn),
        in_specs=[pl.BlockSpec((ROW_BLOCK, D_MODEL), lambda l, j: (0, 0)),
                  pl.BlockSpec((1, D_MODEL, tn), lambda l, j: (l, 0, j)),
                  pl.BlockSpec((1, 1, tn), lambda l, j: (l, 0, j))],
        out_specs=pl.BlockSpec((1, ROW_BLOCK, tn), lambda l, j: (l, 0, j)),
        out_shape=jax.ShapeDtypeStruct((depth, ROW_BLOCK, N_MOD * D_MODEL), F32),
        compiler_params=_cparams(("arbitrary", "arbitrary"), 32),
        name="modulation",
    )(cond8, w_mod, b_mod.reshape(depth, 1, N_MOD * D_MODEL))


def _inproj_kernel(src_ref, x_ref, g_ref, mod_ref, wa_ref, wb_ref, o_ref, h_ref, w_s):
    j = pl.program_id(1)

    @pl.when(j == 0)
    def _():
        for s in range(x_ref.shape[0] // SEG_ROWS):
            rows = slice(s * SEG_ROWS, (s + 1) * SEG_ROWS)
            h = _modulated_norm(x_ref[rows, :], g_ref[0], mod_ref[s, 1:2, :], mod_ref[s, 0:1, :])
            h_ref[rows, :] = h.astype(BF16)

    @pl.when(pl.program_id(0) == 0)
    def _():
        w_s[j, :, 0:COL_TILE] = wa_ref[0].astype(BF16)
        w_s[j, :, COL_TILE:2 * COL_TILE] = wb_ref[0].astype(BF16)

    o_ref[...] = jnp.dot(h_ref[...], w_s[j], preferred_element_type=F32)


def _input_projection(x, norm_g, modseg, w_in, layer):
    n_tok = x.shape[0]
    segs = 2
    tm = segs * SEG_ROWS
    n_col = Z_W // (2 * COL_TILE)
    src = jnp.asarray(_SRC_TILES + (_SRC_TILES[-1],) * (Z_W // COL_TILE - len(_SRC_TILES)), jnp.int32)

    def wspec(k):
        return pl.BlockSpec((1, D_MODEL, COL_TILE),
                            lambda i, j, s: (layer, 0, s[2 * jnp.where(i == 0, j, n_col - 1) + k]))

    grid_spec = pltpu.PrefetchScalarGridSpec(
        num_scalar_prefetch=1,
        grid=(n_tok // tm, n_col),
        in_specs=[pl.BlockSpec((tm, D_MODEL), lambda i, j, s: (i, 0)),
                  pl.BlockSpec((1, 1, D_MODEL), lambda i, j, s: (layer, 0, 0)),
                  pl.BlockSpec((segs, N_MOD, D_MODEL), lambda i, j, s: (i, 0, 0)),
                  wspec(0), wspec(1)],
        out_specs=pl.BlockSpec((tm, 2 * COL_TILE), lambda i, j, s: (i, j)),
        scratch_shapes=[pltpu.VMEM((tm, D_MODEL), BF16),
                        pltpu.VMEM((n_col, D_MODEL, 2 * COL_TILE), BF16)],
    )
    return pl.pallas_call(
        _inproj_kernel,
        grid_spec=grid_spec,
        out_shape=jax.ShapeDtypeStruct((n_tok, Z_W), F32),
        compiler_params=_cparams(("arbitrary", "arbitrary"), 56),
        name="input_projection",
    )(src, x, norm_g.reshape(-1, 1, D_MODEL), modseg, w_in, w_in)


def _head_rms(x, g):
    lane = lax.broadcasted_iota(jnp.int32, x.shape, 1)
    lo = lane < HEAD_DIM
    xx = x * x
    s_lo = jnp.sum(jnp.where(lo, xx, 0.0), axis=-1, keepdims=True)
    s_hi = jnp.sum(jnp.where(lo, 0.0, xx), axis=-1, keepdims=True)
    inv = jnp.where(lo, lax.rsqrt(s_lo * (1.0 / HEAD_DIM) + EPS),
                    lax.rsqrt(s_hi * (1.0 / HEAD_DIM) + EPS))
    return x * inv * g


def _rope(x, cos, sin_a, sin_b):
    q = HEAD_DIM // 4
    return x * cos + pltpu.roll(x, LANES - q, 1) * sin_a + pltpu.roll(x, q, 1) * sin_b


def _attn_kernel(*refs, t_len, use_ctx):
    if use_ctx:
        (q_ref, k_ref, v_ref, qg_ref, kg_ref, cos_q_ref, sa_q_ref, sb_q_ref,
         cos_k_ref, sa_k_ref, sb_k_ref, ck_ref, cv_ref, o_ref, kall_ref, vall_ref) = refs
    else:
        (q_ref, k_ref, v_ref, qg_ref, kg_ref, o_ref, ko_ref, vo_ref, kall_ref, vall_ref) = refs

    @pl.when(pl.program_id(1) == 0)
    def _():
        k = _head_rms(k_ref[...], kg_ref[...])
        if use_ctx:
            k = _rope(k, cos_k_ref[...], sa_k_ref[...], sb_k_ref[...])
            kall_ref[0:t_len, :] = k.astype(BF16)
            vall_ref[0:t_len, :] = v_ref[...].astype(BF16)
            kall_ref[t_len:, :] = ck_ref[0, 0].astype(BF16)
            vall_ref[t_len:, :] = cv_ref[0, 0].astype(BF16)
        else:
            ko_ref[0] = k
            vo_ref[0] = v_ref[...]
            kall_ref[...] = k.astype(BF16)
            vall_ref[...] = v_ref[...].astype(BF16)

    heads = []
    for c in range(ATTN_W // LANES):
        qc = _head_rms(q_ref[:, c * LANES:(c + 1) * LANES], qg_ref[...])
        if use_ctx:
            qc = _rope(qc, cos_q_ref[...], sa_q_ref[...], sb_q_ref[...])
        qc = (qc * (HEAD_DIM ** -0.5)).astype(BF16)
        heads.append(qc[:, :HEAD_DIM])
        heads.append(qc[:, HEAD_DIM:])
    tq = q_ref.shape[0]
    for g in range(N_KV_HEADS):
        qs = jnp.concatenate(heads[g * KV_GROUP:(g + 1) * KV_GROUP], axis=0)
        kh = kall_ref[:, g * HEAD_DIM:(g + 1) * HEAD_DIM]
        vh = vall_ref[:, g * HEAD_DIM:(g + 1) * HEAD_DIM]
        s = _mm_nt(qs, kh)
        m = jnp.max(s, axis=-1, keepdims=True)
        p = jnp.exp(s - m)
        den = jnp.sum(p, axis=-1, keepdims=True)
        o = _mm(p, vh) / den
        for j in range(KV_GROUP):
            hh = g * KV_GROUP + j
            o_ref[:, hh * HEAD_DIM:(hh + 1) * HEAD_DIM] = o[j * tq:(j + 1) * tq].astype(BF16)


def _attention(z, q_g, k_g, layer, n_batch, t_len, row0, rope=None, cache=None):
    use_ctx = cache is not None
    nq = t_len // Q_TILE
    rb0 = row0 // t_len
    qb0 = row0 // Q_TILE
    qg = jnp.tile(q_g[layer], 2).reshape(1, LANES)
    kg = jnp.tile(k_g[layer], 2).reshape(1, LANES)
    n_out_tok = n_batch * t_len
    vec = pl.BlockSpec((1, LANES), lambda b, i: (0, 0))
    in_specs = [pl.BlockSpec((Q_TILE, ATTN_W), lambda b, i: (qb0 + b * nq + i, Z_Q // ATTN_W)),
                pl.BlockSpec((t_len, KV_W), lambda b, i: (rb0 + b, Z_K // KV_W)),
                pl.BlockSpec((t_len, KV_W), lambda b, i: (rb0 + b, Z_V // KV_W)),
                vec, vec]
    args = [z, z, z, qg, kg]
    out_specs = [pl.BlockSpec((Q_TILE, ATTN_W), lambda b, i: (b * nq + i, 0))]
    out_shape = [jax.ShapeDtypeStruct((n_out_tok, ATTN_W), BF16)]
    t_keys = t_len
    if use_ctx:
        cos, sin_a, sin_b = rope
        cache_k, cache_v = cache
        past = cache_k.shape[2]
        t_keys = t_len + past
        tq_spec = pl.BlockSpec((Q_TILE, LANES), lambda b, i: (i, 0))
        tk_spec = pl.BlockSpec((t_len, LANES), lambda b, i: (0, 0))
        c_spec = pl.BlockSpec((1, 1, past, KV_W), lambda b, i: (b, layer, 0, 0))
        in_specs += [tq_spec, tq_spec, tq_spec, tk_spec, tk_spec, tk_spec, c_spec, c_spec]
        args += [cos, sin_a, sin_b, cos, sin_a, sin_b,
                 cache_k.reshape(cache_k.shape[0], cache_k.shape[1], past, KV_W),
                 cache_v.reshape(cache_v.shape[0], cache_v.shape[1], past, KV_W)]
    else:
        kv_spec = pl.BlockSpec((1, t_len, KV_W), lambda b, i: (b, 0, 0))
        out_specs += [kv_spec, kv_spec]
        out_shape += [jax.ShapeDtypeStruct((n_batch, t_len, KV_W), F32)] * 2
    return pl.pallas_call(
        functools.partial(_attn_kernel, t_len=t_len, use_ctx=use_ctx),
        grid=(n_batch, nq),
        in_specs=in_specs,
        out_specs=out_specs,
        out_shape=out_shape,
        scratch_shapes=[pltpu.VMEM((t_keys, KV_W), BF16), pltpu.VMEM((t_keys, KV_W), BF16)],
        compiler_params=_cparams(("arbitrary", "arbitrary"), 48),
        name="attention_ctx" if use_ctx else "attention",
    )(*args)


def _shift_rows(x, d):
    n = x.shape[0]
    row = lax.broadcasted_iota(jnp.int32, x.shape, 0)
    y = pltpu.roll(x, d % n, 0)
    if d > 0:
        return jnp.where(row >= d, y, 0.0)
    return jnp.where(row < n + d, y, 0.0)


def _linear_scan(a, u, reverse):
    n, w = a.shape
    row = lax.broadcasted_iota(jnp.int32, a.shape, 0)

    def shifted(x, d, fill):
        if d % ROW_BLOCK == 0:
            pad = jnp.full((d, w), fill, x.dtype)
            return jnp.concatenate([x[d:], pad] if reverse else [pad, x[:n - d]], axis=0)
        if reverse:
            return jnp.where(row < n - d, pltpu.roll(x, n - d, 0), fill)
        return jnp.where(row >= d, pltpu.roll(x, d, 0), fill)

    d = 1
    while d < n:
        u = a * shifted(u, d, 0.0) + u
        a = a * shifted(a, d, 1.0)
        d *= 2
    return a, u


def _rglru_kernel(*refs, use_ctx):
    if use_ctx:
        (x_ref, y_ref, cw_ref, cb_ref, wa_ref, wx_ref, bg_ref, lam_ref, h0_ref, o_ref, wg_s) = refs
    else:
        (x_ref, y_ref, cw_ref, cb_ref, wa_ref, wx_ref, bg_ref, lam_ref, o_ref, last_ref, wg_s) = refs

    @pl.when(pl.program_id(0) == 0)
    def _():
        wg_s[...] = jnp.zeros_like(wg_s)
        for d in range(2):
            for g, w_ref in enumerate((wa_ref, wx_ref)):
                for nb in range(RG_BLOCKS):
                    col0 = (2 * d + g) * D_RNN + nb * RG_BW
                    wg_s[nb * RG_BW:(nb + 1) * RG_BW, col0:col0 + RG_BW] = w_ref[0, d, nb].astype(BF16)

    x = x_ref[...]
    cw = cw_ref[0]
    xr = cb_ref[0] + _shift_rows(x, 1) * cw[0:1] + x * cw[1:2] \
        + _shift_rows(x, -1) * cw[2:3] + _shift_rows(x, -2) * cw[3:4]
    gates = _sigmoid(jnp.dot(xr.astype(BF16), wg_s[...], preferred_element_type=F32) + bg_ref[0])
    n = x.shape[0]
    total = None
    lasts = []
    for d in range(2):
        r = gates[:, (2 * d) * D_RNN:(2 * d + 1) * D_RNN]
        i = gates[:, (2 * d + 1) * D_RNN:(2 * d + 2) * D_RNN]
        log_a = (-RG_C * _softplus(-lam_ref[0, d:d + 1, :])) * r
        a = jnp.exp(log_a)
        u = jnp.sqrt(1.0 - a * a) * (i * xr)
        a_cum, h = _linear_scan(a, u, reverse=(d == 1))
        if use_ctx:
            h = h + a_cum * h0_ref[0, 0, d:d + 1, :]
        else:
            lasts.append(h[0:1] if d == 1 else h[n - 1:n])
        total = h if total is None else total + h
    o_ref[...] = (jax.nn.gelu(y_ref[...]) * total).astype(BF16)
    if not use_ctx:
        last_ref[0] = jnp.concatenate(lasts, axis=0)


def _rglru(z, conv_w, conv_b, rg_wa, rg_wx, b_gate, lam, layer, n_batch, t_len, row0, state=None):
    use_ctx = state is not None
    rb0 = row0 // t_len
    blocks = pl.BlockSpec((1, 2, RG_BLOCKS, RG_BW, RG_BW), lambda b: (layer, 0, 0, 0, 0))
    in_specs = [pl.BlockSpec((t_len, D_RNN), lambda b: (rb0 + b, Z_RG // D_RNN)),
                pl.BlockSpec((t_len, D_RNN), lambda b: (rb0 + b, Z_RG // D_RNN + 1)),
                pl.BlockSpec((1, 4, D_RNN), lambda b: (layer, 0, 0)),
                pl.BlockSpec((1, 1, D_RNN), lambda b: (layer, 0, 0)),
                blocks, blocks,
                pl.BlockSpec((1, 1, 4 * D_RNN), lambda b: (layer, 0, 0)),
                pl.BlockSpec((1, 2, D_RNN), lambda b: (layer, 0, 0))]
    args = [z, z, conv_w, conv_b.reshape(-1, 1, D_RNN), rg_wa, rg_wx, b_gate, lam]
    out_specs = [pl.BlockSpec((t_len, D_RNN), lambda b: (b, 0))]
    out_shape = [jax.ShapeDtypeStruct((n_batch * t_len, D_RNN), BF16)]
    if use_ctx:
        in_specs.append(pl.BlockSpec((1, 1, 2, D_RNN), lambda b: (b, layer, 0, 0)))
        args.append(state)
    else:
        out_specs.append(pl.BlockSpec((1, 2, D_RNN), lambda b: (b, 0, 0)))
        out_shape.append(jax.ShapeDtypeStruct((n_batch, 2, D_RNN), F32))
    return pl.pallas_call(
        functools.partial(_rglru_kernel, use_ctx=use_ctx),
        grid=(n_batch,),
        in_specs=in_specs,
        out_specs=out_specs,
        out_shape=out_shape,
        scratch_shapes=[pltpu.VMEM((D_RNN, 4 * D_RNN), BF16)],
        compiler_params=_cparams(("arbitrary",), 56),
        name="rglru_ctx" if use_ctx else "rglru",
    )(*args)


def _cumsum_rows(g, reverse):
    n, w = g.shape
    m = n // ROW_BLOCK

    def scan(x, size, pos):
        d = 1
        while d < size:
            if reverse:
                x = x + jnp.where(pos < size - d, pltpu.roll(x, x.shape[0] - d, 0), 0.0)
            else:
                x = x + jnp.where(pos >= d, pltpu.roll(x, d, 0), 0.0)
            d *= 2
        return x

    g = scan(g, ROW_BLOCK, lax.broadcasted_iota(jnp.int32, g.shape, 0) & (ROW_BLOCK - 1))
    g3 = g.reshape(m, ROW_BLOCK, w)
    total = g3[:, 0, :] if reverse else g3[:, ROW_BLOCK - 1, :]
    before = scan(total, m, lax.broadcasted_iota(jnp.int32, total.shape, 0)) - total
    return (g3 + before[:, None, :]).reshape(n, w)


def _block_row_fn(gc, reverse):
    n, w = gc.shape
    g3 = gc.reshape(n // ROW_BLOCK, ROW_BLOCK, w)
    edge_row = 0 if reverse else ROW_BLOCK - 1
    edge = jnp.broadcast_to(g3[:, edge_row:edge_row + 1, :], g3.shape)
    sub = lax.broadcasted_iota(jnp.int32, g3.shape, 1)

    def at(blk):
        r = blk // 2 if reverse else blk // 2 - 1
        if blk > ROW_BLOCK:
            groups = blk // ROW_BLOCK
            e4 = edge.reshape(n // blk, groups, ROW_BLOCK, w)
            pick = r // ROW_BLOCK
            return jnp.broadcast_to(e4[:, pick:pick + 1], e4.shape).reshape(n, w)
        out = jnp.broadcast_to(g3[:, r:r + 1, :], g3.shape)
        for b0 in range(blk, ROW_BLOCK, blk):
            out = jnp.where(sub >= b0, jnp.broadcast_to(g3[:, b0 + r:b0 + r + 1, :], g3.shape), out)
        return out.reshape(n, w)

    return at


def _first_head(shape):
    lane = lax.broadcasted_iota(jnp.int32, shape, len(shape) - 1)
    return (lane & (LANES - 1)) < HG_DK


def _pair(x, p):
    return x[:, p * LANES:(p + 1) * LANES]


def _hgrn_gates(f_logit, lb):
    y = jnp.exp(-jnp.abs(f_logit))
    log_sig = jnp.minimum(f_logit, 0.0) - jnp.log(1.0 + y)
    a = jnp.log(lb)
    b = jnp.log1p(-lb) + log_sig
    log_f = jnp.maximum(a, b) + jnp.log(1.0 + jnp.exp(-jnp.abs(a - b)))
    k = (1.0 - lb) * (jnp.where(f_logit >= 0.0, y, 1.0) / (1.0 + y))
    return log_f * LOG2_E, k


def _hgrn_chunk_state(gc, k, v, reverse):
    n = gc.shape[0]
    g_tot = gc[0:1] if reverse else gc[n - 1:n]
    k_end = (k * jnp.exp2(jnp.minimum(g_tot - gc, 0.0))).astype(BF16)
    vb = v.astype(BF16)
    same_head = (_first_head((LANES, LANES))
                 == (lax.broadcasted_iota(jnp.int32, (LANES, LANES), 0) < HG_DK))
    ds = [jnp.where(same_head, _mm_tn(_pair(k_end, p), _pair(vb, p)), 0.0) for p in range(N_PAIRS)]
    return ds, jnp.exp2(g_tot)


def _decay_state(s, e_row):
    n = s.shape[0]
    eye = (lax.broadcasted_iota(jnp.int32, (n, n), 0) == lax.broadcasted_iota(jnp.int32, (n, n), 1))
    col = jnp.sum(jnp.where(eye, jnp.broadcast_to(e_row, (n, n)), 0.0), axis=1, keepdims=True)
    return s * col


def _hgrn_intra_scores(q, ks, gcs, txs):
    n = q.shape[0]
    row = lax.broadcasted_iota(jnp.int32, (n, LANES), 0)
    first = _first_head((n, LANES))
    boundary = [_block_row_fn(gcs[0], False), _block_row_fn(gcs[1], True)]
    scores = [None] * HG_HEADS
    w = 1
    while w < n:
        blk = 2 * w
        second = (row & w) != 0
        m_f, m_b = boundary[0](blk), boundary[1](blk)
        sel = (txs >= w) & (txs < blk)
        for p in range(N_PAIRS):
            qp = _pair(q, p)
            e_f = jnp.exp2(-jnp.abs(_pair(gcs[0], p) - _pair(m_f, p)))
            e_b = jnp.exp2(-jnp.abs(_pair(gcs[1], p) - _pair(m_b, p)))
            q_f = jnp.where(second, qp * e_f, 0.0).astype(BF16)
            q_b = jnp.where(second, 0.0, qp * e_b).astype(BF16)
            k_f = jnp.where(second, 0.0, _pair(ks[0], p) * e_f)
            k_b = jnp.where(second, _pair(ks[1], p) * e_b, 0.0)
            qc = jnp.concatenate([q_f, q_b], axis=1)
            for j, keep in enumerate((first, jnp.logical_not(first))):
                kc = jnp.concatenate([jnp.where(keep, k_f, 0.0).astype(BF16),
                                      jnp.where(keep, k_b, 0.0).astype(BF16)], axis=1)
                s = jnp.where(sel, _mm_nt(qc, kc), 0.0)
                h = 2 * p + j
                scores[h] = s if scores[h] is None else scores[h] + s
        w = blk
    return scores


def _hgrn_kernel(*refs, t_len, use_ctx, n_earlier=0):
    if use_ctx:
        (q_ref, ff_ref, fb_ref, i_ref, og_ref, lb_ref, ng_ref, s0_ref, o_ref, sin_ref, gk_ref) = refs
    elif n_earlier:
        (q_ref, ff_ref, fb_ref, i_ref, og_ref, lb_ref, ng_ref, earlier_ref, o_ref, sl_ref) = refs
        sl_ref[0, 0:n_earlier] = earlier_ref[0]
    else:
        (q_ref, ff_ref, fb_ref, i_ref, og_ref, lb_ref, ng_ref, o_ref, sl_ref) = refs
    c = HG_CHUNK
    n_chunks = t_len // c
    f_refs = (ff_ref, fb_ref)

    def load_dir(d, r0):
        log_f, k = _hgrn_gates(f_refs[d][pl.ds(r0, c), :], lb_ref[0, d:d + 1, :])
        return _cumsum_rows(log_f, reverse=(d == 1)), k

    if use_ctx:
        zero = jnp.zeros((HG_DK, HG_DV), F32)
        for d in range(2):
            states = []
            for p in range(N_PAIRS):
                top = jnp.concatenate([s0_ref[0, 0, d, 2 * p], zero], axis=1)
                bot = jnp.concatenate([zero, s0_ref[0, 0, d, 2 * p + 1]], axis=1)
                states.append(jnp.concatenate([top, bot], axis=0))
            order = range(n_chunks) if d == 0 else range(n_chunks - 1, -1, -1)
            for ci in order:
                for p in range(N_PAIRS):
                    sin_ref[ci, p, d * LANES:(d + 1) * LANES, :] = states[p]
                gc, k = load_dir(d, ci * c)
                gk_ref[d, 0, pl.ds(ci * c, c), :] = gc
                gk_ref[d, 1, pl.ds(ci * c, c), :] = k
                ds, e_tot = _hgrn_chunk_state(gc, k, i_ref[pl.ds(ci * c, c), :], reverse=(d == 1))
                states = [_decay_state(states[p], _pair(e_tot, p)) + ds[p] for p in range(N_PAIRS)]

    ti = lax.broadcasted_iota(jnp.int32, (c, c), 0)
    si = lax.broadcasted_iota(jnp.int32, (c, c), 1)
    txs = ti ^ si
    diag = ti == si

    def chunk_body(ci, carry):
        r0 = pl.multiple_of(ci * c, c)
        hq = q_ref[pl.ds(r0, c), :]
        q = hq * _sigmoid(hq)
        v = i_ref[pl.ds(r0, c), :]
        gcs, ks = [], []
        for d in range(2):
            if use_ctx:
                gc, k = gk_ref[d, 0, pl.ds(r0, c), :], gk_ref[d, 1, pl.ds(r0, c), :]
            else:
                gc, k = load_dir(d, r0)
            gcs.append(gc)
            ks.append(k)
        scores = _hgrn_intra_scores(q, ks, gcs, txs)
        first = _first_head(q.shape)
        k_sum = ks[0] + ks[1]
        v_heads = (jnp.where(first, v, 0.0).astype(BF16), jnp.where(first, 0.0, v).astype(BF16))
        k_heads = (jnp.where(first, k_sum, 0.0).astype(BF16), jnp.where(first, 0.0, k_sum).astype(BF16))
        q_bf = q.astype(BF16)
        if use_ctx:
            qe = [(q * jnp.exp2(gcs[d])).astype(BF16) for d in range(2)]
        else:
            for d in range(2):
                ds, _ = _hgrn_chunk_state(gcs[d], ks[d], v, reverse=(d == 1))
                for p in range(N_PAIRS):
                    sl_ref[0, n_earlier, d, 2 * p] = ds[p][:HG_DK, :HG_DV]
                    sl_ref[0, n_earlier, d, 2 * p + 1] = ds[p][HG_DK:, HG_DV:]
        outs = []
        for p in range(N_PAIRS):
            o = None
            for j in range(2):
                s_diag = jnp.where(diag, _mm_nt(_pair(q_bf, p), _pair(k_heads[j], p)), 0.0)
                t = _mm(scores[2 * p + j] + s_diag, _pair(v_heads[j], p))
                o = t if o is None else o + t
            if use_ctx:
                o = o + _mm(jnp.concatenate([_pair(qe[0], p), _pair(qe[1], p)], axis=1), sin_ref[ci, p])
            outs.append(_head_rms(o, ng_ref[0]))
        og = og_ref[pl.ds(r0, c), :]
        o_all = jnp.concatenate(outs, axis=1) * (og * _sigmoid(og))
        o_ref[pl.ds(r0, c), :] = o_all.astype(BF16)
        return carry

    if n_chunks == 1:
        chunk_body(0, 0)
    else:
        lax.fori_loop(0, n_chunks, chunk_body, 0)


def _hgrn(z, lb, norm_g, layer, n_batch, t_len, row0, state=None, earlier=None):
    use_ctx = state is not None
    n_earlier = 0 if earlier is None else earlier.shape[1]
    state_dims = (2, HG_HEADS, HG_DK, HG_DV)
    rb0 = row0 // t_len
    cb = Z_HG // HG_W

    def zspec(k):
        return pl.BlockSpec((t_len, HG_W), lambda b: (rb0 + b, cb + k))

    in_specs = [zspec(0), zspec(1), zspec(2), zspec(3), zspec(4),
                pl.BlockSpec((1, 2, HG_W), lambda b: (layer, 0, 0)),
                pl.BlockSpec((1, 1, LANES), lambda b: (layer, 0, 0))]
    args = [z, z, z, z, z, lb, jnp.tile(norm_g, (1, LANES // HG_DV)).reshape(-1, 1, LANES)]
    out_specs = [pl.BlockSpec((t_len, HG_W), lambda b: (b, 0))]
    out_shape = [jax.ShapeDtypeStruct((n_batch * t_len, HG_W), BF16)]
    scratch = []
    if use_ctx:
        in_specs.append(pl.BlockSpec((1, 1, 2, HG_HEADS, HG_DK, HG_DV), lambda b: (b, layer, 0, 0, 0, 0)))
        args.append(state)
        scratch.append(pltpu.VMEM((t_len // HG_CHUNK, N_PAIRS, 2 * LANES, LANES), F32))
        scratch.append(pltpu.VMEM((2, 2, t_len, HG_W), F32))
    else:
        if n_earlier:
            in_specs.append(pl.BlockSpec((1, n_earlier) + state_dims, lambda b: (b, 0, 0, 0, 0, 0)))
            args.append(earlier)
        out_specs.append(pl.BlockSpec((1, n_earlier + 1) + state_dims, lambda b: (b, 0, 0, 0, 0, 0)))
        out_shape.append(jax.ShapeDtypeStruct((n_batch, n_earlier + 1) + state_dims, F32))
    return pl.pallas_call(
        functools.partial(_hgrn_kernel, t_len=t_len, use_ctx=use_ctx, n_earlier=n_earlier),
        grid=(n_batch,),
        in_specs=in_specs,
        out_specs=out_specs,
        out_shape=out_shape,
        scratch_shapes=scratch,
        compiler_params=_cparams(("arbitrary",), 56),
        name="hgrn_ctx" if use_ctx else "hgrn",
    )(*args)


def _merge_kernel(ap_ref, as_ref, rp_ref, rs_ref, hp_ref, hs_ref, g0_ref, g1_ref, g2_ref, x_ref, mod_ref,
                  wa_ref, wr_ref, wh_ref, wo_ref, o_ref, wa_s, wr_s, wh_s, wo_s, *, prompt_tiles):
    @pl.when(pl.program_id(0) == 0)
    def _():
        wa_s[...] = wa_ref[0].astype(BF16)
        wr_s[...] = wr_ref[0].astype(BF16)
        wh_s[...] = wh_ref[0].astype(BF16)
        wo_s[...] = wo_ref[0].astype(BF16)

    is_prompt = pl.program_id(0) < prompt_tiles

    def proj(p_ref, s_ref, w_s):
        return jnp.dot(jnp.where(is_prompt, p_ref[...], s_ref[...]), w_s[...], preferred_element_type=F32)

    merged = (_sigmoid(g0_ref[...]) * proj(ap_ref, as_ref, wa_s)
              + _sigmoid(g1_ref[...]) * proj(rp_ref, rs_ref, wr_s)
              + _sigmoid(g2_ref[...]) * proj(hp_ref, hs_ref, wh_s))
    out = jnp.dot(merged.astype(BF16), wo_s[...], preferred_element_type=F32)
    o_ref[...] = x_ref[...] + mod_ref[0, 2:3, :] * out


def _merge_project(mixed, z, x, modseg, w_attn_o, w_rnn_o, w_hgrn_o, w_out, layer):
    n_tok = x.shape[0]
    tm = 512
    per_seg = SEG_ROWS // tm
    prompt_tiles = mixed[0][0].shape[0] // tm
    half_p = pl.BlockSpec((tm, ATTN_W), lambda i: (jnp.minimum(i, prompt_tiles - 1), 0))
    half_s = pl.BlockSpec((tm, ATTN_W), lambda i: (jnp.maximum(i - prompt_tiles, 0), 0))
    full = pl.BlockSpec((tm, D_MODEL), lambda i: (i, 0))

    def gspec(k):
        return pl.BlockSpec((tm, D_MODEL), lambda i: (i, Z_GL // D_MODEL + k))

    def wspec(rows):
        return pl.BlockSpec((1, rows, D_MODEL), lambda i: (layer, 0, 0))

    return pl.pallas_call(
        functools.partial(_merge_kernel, prompt_tiles=prompt_tiles),
        grid=(n_tok // tm,),
        in_specs=[half_p, half_s, half_p, half_s, half_p, half_s, gspec(0), gspec(1), gspec(2), full,
                  pl.BlockSpec((1, N_MOD, D_MODEL), lambda i: (i // per_seg, 0, 0)),
                  wspec(ATTN_W), wspec(D_RNN), wspec(HG_W), wspec(D_MODEL)],
        out_specs=full,
        out_shape=jax.ShapeDtypeStruct((n_tok, D_MODEL), F32),
        scratch_shapes=[pltpu.VMEM((ATTN_W, D_MODEL), BF16), pltpu.VMEM((D_RNN, D_MODEL), BF16),
                        pltpu.VMEM((HG_W, D_MODEL), BF16), pltpu.VMEM((D_MODEL, D_MODEL), BF16)],
        compiler_params=_cparams(("arbitrary",), 56),
        name="merge_project",
    )(*[part for pair in mixed for part in pair], z, z, z, x, modseg, w_attn_o, w_rnn_o, w_hgrn_o, w_out)


def _router_kernel(x_ref, g_ref, mod_ref, wr_ref, br_ref, h_ref, slot_ref, wt_ref, meta_ref, run_ref):
    @pl.when(pl.program_id(0) == 0)
    def _():
        run_ref[...] = jnp.zeros_like(run_ref)

    h = _modulated_norm(x_ref[...], g_ref[0], mod_ref[0, 4:5, :], mod_ref[0, 3:4, :])
    h_ref[...] = h
    h_hi = h.astype(BF16)
    h_lo = (h - h_hi.astype(F32)).astype(BF16)
    w = wr_ref[0]
    w_hi = w.astype(BF16)
    w_lo = (w - w_hi.astype(F32)).astype(BF16)
    logits = (jnp.dot(h_hi, w_hi, preferred_element_type=F32) + jnp.dot(h_hi, w_lo, preferred_element_type=F32)
              + jnp.dot(h_lo, w_hi, preferred_element_type=F32)) + br_ref[0]
    tm = logits.shape[0]
    lane = lax.broadcasted_iota(jnp.int32, logits.shape, 1)
    ids, vals = [], []
    chosen = jnp.zeros(logits.shape, jnp.bool_)
    work = logits
    for _ in range(TOP_K):
        m = jnp.max(work, axis=-1, keepdims=True)
        idx = jnp.min(jnp.where(work == m, lane, LANES), axis=-1, keepdims=True)
        hit = lane == idx
        chosen = chosen | hit
        work = jnp.where(hit, -jnp.inf, work)
        ids.append(idx)
        vals.append(m)
    exps = [jnp.exp(v - vals[0]) for v in vals]
    den = exps[0] + exps[1] + exps[2] + exps[3]
    ind = jnp.where(chosen, 1.0, 0.0)
    r_i = lax.broadcasted_iota(jnp.int32, (tm, tm), 0)
    c_i = lax.broadcasted_iota(jnp.int32, (tm, tm), 1)
    count = jnp.sum(ind, axis=0, keepdims=True)
    padded = jnp.floor((count + (ROW_BLOCK - 1)) * (1.0 / ROW_BLOCK)) * ROW_BLOCK
    incl = jnp.broadcast_to(padded, (ROW_BLOCK, LANES))
    lane8 = lax.broadcasted_iota(jnp.int32, (ROW_BLOCK, LANES), 1)
    d = 1
    while d < LANES:
        incl = incl + jnp.where(lane8 >= d, pltpu.roll(incl, d, 1), 0.0)
        d *= 2
    local_start = incl[0:1] - padded
    place = _mm(jnp.where(c_i < r_i, 1.0, 0.0), ind) + local_start
    slot_out = jnp.zeros(logits.shape, jnp.int32)
    wt_out = jnp.zeros(logits.shape, F32)
    for k in range(TOP_K):
        slot = jnp.sum(jnp.where(lane == ids[k], place, 0.0), axis=-1, keepdims=True)
        slot_out = jnp.where(lane == k, slot.astype(jnp.int32), slot_out)
        wt_out = jnp.where(lane == k, exps[k] / den, wt_out)
    slot_ref[...] = slot_out
    wt_ref[...] = wt_out
    row8 = lax.broadcasted_iota(jnp.int32, (ROW_BLOCK, LANES), 0)
    table = jnp.where(row8 == 0, local_start, jnp.where(row8 == 1, padded, run_ref[0:1, :]))
    meta_ref[0] = jnp.where(row8 < 3, table, 0.0).astype(jnp.int32)
    run_ref[...] = run_ref[...] + padded


def _router(x, norm_g, modseg, w_router, b_router, layer):
    n_tok = x.shape[0]
    tm = TOKEN_TILE
    per_seg = SEG_ROWS // tm
    depth = w_router.shape[0]
    w_pad = jnp.zeros((depth, D_MODEL, LANES), F32).at[:, :, :N_EXPERTS].set(w_router)
    b_pad = jnp.full((depth, 1, LANES), NEG_BIG, F32).at[:, 0, :N_EXPERTS].set(b_router)
    full = pl.BlockSpec((tm, D_MODEL), lambda i: (i, 0))
    small = pl.BlockSpec((tm, LANES), lambda i: (i, 0))
    return pl.pallas_call(
        _router_kernel,
        grid=(n_tok // tm,),
        in_specs=[full,
                  pl.BlockSpec((1, 1, D_MODEL), lambda i: (layer, 0, 0)),
                  pl.BlockSpec((1, N_MOD, D_MODEL), lambda i: (i // per_seg, 0, 0)),
                  pl.BlockSpec((1, D_MODEL, LANES), lambda i: (layer, 0, 0)),
                  pl.BlockSpec((1, 1, LANES), lambda i: (layer, 0, 0))],
        out_specs=[full, small, small, pl.BlockSpec((1, ROW_BLOCK, LANES), lambda i: (i, 0, 0))],
        out_shape=[jax.ShapeDtypeStruct((n_tok, D_MODEL), F32),
                   jax.ShapeDtypeStruct((n_tok, LANES), jnp.int32),
                   jax.ShapeDtypeStruct((n_tok, LANES), F32),
                   jax.ShapeDtypeStruct((n_tok // tm, ROW_BLOCK, LANES), jnp.int32)],
        scratch_shapes=[pltpu.VMEM((ROW_BLOCK, LANES), F32)],
        compiler_params=_cparams(("arbitrary",), 40),
        name="router",
    )(x, norm_g.reshape(-1, 1, D_MODEL), modseg, w_pad, b_pad)


COPY_BLOCKS = (BIG_BLOCK, 4 * ROW_BLOCK, 2 * ROW_BLOCK, ROW_BLOCK)
COPY_CAPS = (LOCAL_ROWS // COPY_BLOCKS[0],) + tuple(
    N_EXPERTS * (COPY_BLOCKS[c - 1] // COPY_BLOCKS[c] - 1) for c in range(1, len(COPY_BLOCKS)))
COPY_LIST_LEN = 1024
_COPY_AT = tuple(len(COPY_BLOCKS) + 2 * sum(COPY_CAPS[:c]) for c in range(len(COPY_BLOCKS)))
assert _COPY_AT[-1] + 2 * COPY_CAPS[-1] <= COPY_LIST_LEN


def _copy_lists(local_start, padded, sorted_start):

    def expand(count, local0, sorted0, step, cap):
        last = _prefix_sum(count)[:, None, :]
        first = last - count[:, None, :]
        f = jnp.arange(cap, dtype=jnp.int32)[None, :, None]
        mine = ((first <= f) & (f < last)).astype(jnp.int32)
        offset = step * (f - first)
        return (last[:, 0, -1:], jnp.sum(mine * (local0[:, None, :] + offset), axis=2),
                jnp.sum(mine * (sorted0[:, None, :] + offset), axis=2))

    counts, lists = [], []
    done = jnp.zeros_like(padded)
    for rows, cap in zip(COPY_BLOCKS, COPY_CAPS):
        n_copies = (padded - done) // rows
        cnt, loc, srt = expand(n_copies, local_start + done, sorted_start + done, rows, cap)
        counts.append(cnt)
        lists += [loc, srt]
        done = done + n_copies * rows
    parts = counts + lists
    used = sum(p.shape[1] for p in parts)
    parts.append(jnp.zeros((padded.shape[0], COPY_LIST_LEN - used), jnp.int32))
    return jnp.concatenate(parts, axis=1).astype(jnp.int32).reshape(-1)


def _for_each_block(list_ref, fn):
    for c, (rows, cap) in enumerate(zip(COPY_BLOCKS, COPY_CAPS)):
        def copy(j, carry, rows=rows, cap=cap, at=_COPY_AT[c]):
            fn(pl.multiple_of(list_ref[at + j], ROW_BLOCK),
               pl.multiple_of(list_ref[at + cap + j], ROW_BLOCK), rows)
            return carry

        lax.fori_loop(0, list_ref[c], copy, 0)


def _slot_matrix(slot_ref, values):
    col = lax.broadcasted_iota(jnp.int32, (TOKEN_TILE, LOCAL_ROWS), 1)
    out = jnp.zeros((TOKEN_TILE, LOCAL_ROWS), F32)
    for k in range(TOP_K):
        out = jnp.where(col == slot_ref[:, k:k + 1], values[k], out)
    return out.astype(BF16)


def _dispatch_kernel(meta_ref, prev_meta_ref, tail_ref, slot_ref, h_ref, xs_ref, local_ref, zero_ref,
                     sem, block_sem):
    i = pl.program_id(0)
    cur = i % 2

    @pl.when(i == 0)
    def _():
        zero_ref[...] = jnp.zeros_like(zero_ref)

        def tails(fn):
            def group(e, carry):
                def block(j, c):
                    fn(pl.multiple_of(tail_ref[e] + j * ROW_BLOCK, ROW_BLOCK))
                    return c
                return lax.fori_loop(0, tail_ref[N_EXPERTS + e], block, carry)
            lax.fori_loop(0, N_EXPERTS, group, 0)

        def zero_copy(row):
            return pltpu.make_async_copy(zero_ref.at[pl.ds(0, ROW_BLOCK)],
                                         xs_ref.at[pl.ds(row, ROW_BLOCK)], sem)

        tails(lambda row: zero_copy(row).start())
        tails(lambda row: zero_copy(row).wait())

        def unused(fn):
            def tile(j, c):
                fn(pl.multiple_of(tail_ref[2 * N_EXPERTS] + j * EXPERT_TILE, EXPERT_TILE))
                return c
            lax.fori_loop(0, tail_ref[2 * N_EXPERTS + 1], tile, 0)

        def zero_tile(row):
            return pltpu.make_async_copy(zero_ref, xs_ref.at[pl.ds(row, EXPERT_TILE)], sem)

        unused(lambda row: zero_tile(row).start())
        unused(lambda row: zero_tile(row).wait())

    onehot = _slot_matrix(slot_ref, [1.0] * TOP_K)
    local_ref[cur] = _mm_tn(onehot, h_ref[...])

    def block_copy(buf):
        return lambda local_row, sorted_row, rows: pltpu.make_async_copy(
            local_ref.at[buf, pl.ds(local_row, rows)], xs_ref.at[pl.ds(sorted_row, rows)],
            block_sem.at[buf])

    _for_each_block(meta_ref, lambda a, b, n: block_copy(cur)(a, b, n).start(
        priority=COPY_BLOCKS.index(n) % 2))

    @pl.when(i > 0)
    def _():
        _for_each_block(prev_meta_ref, lambda a, b, n: block_copy(1 - cur)(a, b, n).wait())

    @pl.when(i == pl.num_programs(0) - 1)
    def _():
        _for_each_block(meta_ref, lambda a, b, n: block_copy(cur)(a, b, n).wait())


def _dispatch(h, slots, meta, tails, n_rows):
    n_tok = h.shape[0]
    return pl.pallas_call(
        _dispatch_kernel,
        grid=(n_tok // TOKEN_TILE,),
        in_specs=[pl.BlockSpec((COPY_LIST_LEN,), lambda i: (i,), memory_space=pltpu.SMEM),
                  pl.BlockSpec((COPY_LIST_LEN,), lambda i: (jnp.maximum(i - 1, 0),), memory_space=pltpu.SMEM),
                  pl.BlockSpec((LANES,), lambda i: (0,), memory_space=pltpu.SMEM),
                  pl.BlockSpec((TOKEN_TILE, LANES), lambda i: (i, 0)),
                  pl.BlockSpec((TOKEN_TILE, D_MODEL), lambda i: (i, 0))],
        out_specs=pl.BlockSpec(memory_space=pl.ANY),
        out_shape=jax.ShapeDtypeStruct((n_rows, D_MODEL), F32),
        scratch_shapes=[pltpu.VMEM((2, LOCAL_ROWS, D_MODEL), F32), pltpu.VMEM((EXPERT_TILE, D_MODEL), F32),
                        pltpu.SemaphoreType.DMA(()), pltpu.SemaphoreType.DMA((2,))],
        compiler_params=_cparams(("arbitrary",), 48),
        name="dispatch",
    )(meta, meta, tails, slots, h)


def _combine_kernel(meta_ref, next_meta_ref, slot_ref, wt_ref, ys_ref, x_ref, mod_ref, *rest, split_tiles):
    local_ref, sem = rest[-2:]
    i = pl.program_id(0)
    cur = i % 2

    def block_copy(buf):
        return lambda local_row, sorted_row, rows: pltpu.make_async_copy(
            ys_ref.at[pl.ds(sorted_row, rows)], local_ref.at[buf, pl.ds(local_row, rows)], sem.at[buf])

    def start_gather(table_ref, buf):
        local_ref[buf, TOKEN_TILE * TOP_K:, :] = jnp.zeros((LOCAL_ROWS - TOKEN_TILE * TOP_K, D_MODEL), F32)
        _for_each_block(table_ref, lambda a, b, n: block_copy(buf)(a, b, n).start(
            priority=COPY_BLOCKS.index(n) % 2))

    @pl.when(i == 0)
    def _():
        start_gather(meta_ref, 0)

    @pl.when(i + 1 < pl.num_programs(0))
    def _():
        start_gather(next_meta_ref, 1 - cur)

    _for_each_block(meta_ref, lambda a, b, n: block_copy(cur)(a, b, n).wait())
    weights = _slot_matrix(slot_ref, [wt_ref[:, k:k + 1] for k in range(TOP_K)])
    acc = jnp.dot(weights, local_ref[cur].astype(BF16), preferred_element_type=F32)
    result = x_ref[...] + mod_ref[0, 5:6, :] * acc
    if split_tiles is None:
        rest[0][...] = result
    else:
        @pl.when(i < split_tiles)
        def _():
            rest[0][...] = result

        @pl.when(i >= split_tiles)
        def _():
            rest[1][...] = result


def _combine(ys, slots, meta, wts, x, modseg, split_tokens=None):
    n_tok = x.shape[0]
    per_seg = SEG_ROWS // TOKEN_TILE
    full = pl.BlockSpec((TOKEN_TILE, D_MODEL), lambda i: (i, 0))
    small = pl.BlockSpec((TOKEN_TILE, LANES), lambda i: (i, 0))
    if split_tokens is None:
        split_tiles = None
        out_specs = full
        out_shape = jax.ShapeDtypeStruct((n_tok, D_MODEL), F32)
    else:
        split_tiles = split_tokens // TOKEN_TILE
        out_specs = [pl.BlockSpec((TOKEN_TILE, D_MODEL), lambda i: (jnp.minimum(i, split_tiles - 1), 0)),
                     pl.BlockSpec((TOKEN_TILE, D_MODEL), lambda i: (jnp.maximum(i - split_tiles, 0), 0))]
        out_shape = [jax.ShapeDtypeStruct((split_tokens, D_MODEL), F32),
                     jax.ShapeDtypeStruct((n_tok - split_tokens, D_MODEL), F32)]
    return pl.pallas_call(
        functools.partial(_combine_kernel, split_tiles=split_tiles),
        grid=(n_tok // TOKEN_TILE,),
        in_specs=[pl.BlockSpec((COPY_LIST_LEN,), lambda i: (i,), memory_space=pltpu.SMEM),
                  pl.BlockSpec((COPY_LIST_LEN,), lambda i: (jnp.minimum(i + 1, n_tok // TOKEN_TILE - 1),),
                               memory_space=pltpu.SMEM),
                  small, small,
                  pl.BlockSpec(memory_space=pl.ANY),
                  full,
                  pl.BlockSpec((1, N_MOD, D_MODEL), lambda i: (i // per_seg, 0, 0))],
        out_specs=out_specs,
        out_shape=out_shape,
        scratch_shapes=[pltpu.VMEM((2, LOCAL_ROWS, D_MODEL), F32), pltpu.SemaphoreType.DMA((2,))],
        compiler_params=_cparams(("arbitrary",), 56),
        name="combine",
    )(meta, meta, slots, wts, ys, x, modseg)


def _expert_kernel(te_ref, first_ref, slot_ref, next_ref, na_ref, x_ref, wgu_hbm, bgu_ref, wd_hbm, bd_ref,
                   o_ref, wgu_f, wd_f, wgu_s, wd_s, sem, *, layer):
    i = pl.program_id(0)
    active = i < na_ref[0]

    def fetch(e, s):
        return (pltpu.make_async_copy(wgu_hbm.at[layer, e], wgu_f.at[s], sem.at[0, s]),
                pltpu.make_async_copy(wd_hbm.at[layer, e], wd_f.at[s], sem.at[1, s]))

    @pl.when(jnp.logical_and(i == 0, first_ref[0] == 1))
    def _():
        for cp in fetch(te_ref[0], 0):
            cp.start()

    @pl.when(first_ref[i] == 1)
    def _():
        s = slot_ref[i]
        for cp in fetch(te_ref[i], s):
            cp.wait()
        wgu_s[...] = wgu_f[s].astype(BF16)
        wd_s[...] = wd_f[s].astype(BF16)

        @pl.when(next_ref[i] >= 0)
        def _():
            for cp in fetch(next_ref[i], 1 - s):
                cp.start()

    @pl.when(active)
    def _():
        gu = jnp.dot(x_ref[...].astype(BF16), wgu_s[...], preferred_element_type=F32) + bgu_ref[0, 0]
        gate = jnp.minimum(gu[:, :D_FF], SWIGLU_LIMIT)
        up = jnp.clip(gu[:, D_FF:], -SWIGLU_LIMIT, SWIGLU_LIMIT)
        act = gate * _sigmoid(SWIGLU_ALPHA * gate) * (up + 1.0)
        o_ref[...] = jnp.dot(act.astype(BF16), wd_s[...], preferred_element_type=F32) + bd_ref[0, 0]

    @pl.when(jnp.logical_not(active))
    def _():
        o_ref[...] = jnp.zeros_like(o_ref)


def _experts(xs, tile_expert, ends, n_active, w_gu, b_gu, w_down, b_down, layer):
    n_rows = xs.shape[0]
    depth = w_gu.shape[0]
    tm = EXPERT_TILE
    n_tiles = n_rows // tm
    tile = jnp.arange(n_tiles, dtype=jnp.int32)
    active = tile < n_active[0]
    first = (active & ((tile == 0) | (tile_expert != jnp.roll(tile_expert, 1)))).astype(jnp.int32)
    slot = (_prefix_sum(first) - 1) % 2
    next_tile = ends[tile_expert] // tm
    next_expert = jnp.where(next_tile < n_active[0],
                            tile_expert[jnp.minimum(next_tile, n_tiles - 1)], -1).astype(jnp.int32)

    def tiled(shape, index):
        return pl.BlockSpec(shape, lambda i, te, fi, sl, nx, na: index(i, te, na))

    grid_spec = pltpu.PrefetchScalarGridSpec(
        num_scalar_prefetch=5,
        grid=(n_tiles,),
        in_specs=[tiled((tm, D_MODEL), lambda i, te, na: (jnp.where(i < na[0], i, 0), 0)),
                  pl.BlockSpec(memory_space=pl.ANY),
                  tiled((1, 1, 1, 2 * D_FF), lambda i, te, na: (layer, te[i], 0, 0)),
                  pl.BlockSpec(memory_space=pl.ANY),
                  tiled((1, 1, 1, D_MODEL), lambda i, te, na: (layer, te[i], 0, 0))],
        out_specs=tiled((tm, D_MODEL), lambda i, te, na: (i, 0)),
        scratch_shapes=[pltpu.VMEM((2, D_MODEL, 2 * D_FF), F32), pltpu.VMEM((2, D_FF, D_MODEL), F32),
                        pltpu.VMEM((D_MODEL, 2 * D_FF), BF16), pltpu.VMEM((D_FF, D_MODEL), BF16),
                        pltpu.SemaphoreType.DMA((2, 2))],
    )
    return pl.pallas_call(
        functools.partial(_expert_kernel, layer=layer),
        grid_spec=grid_spec,
        out_shape=jax.ShapeDtypeStruct((n_rows, D_MODEL), F32),
        compiler_params=_cparams(("arbitrary",), 56),
        name="experts",
    )(tile_expert, first, slot.astype(jnp.int32), next_expert, n_active, xs, w_gu,
      b_gu.reshape(depth, N_EXPERTS, 1, 2 * D_FF), w_down, b_down.reshape(depth, N_EXPERTS, 1, D_MODEL))


def _moe(x, norm_g, modseg, w_router, b_router, w_gu, b_gu, w_down, b_down, layer, split_tokens=None):
    n_tok = x.shape[0]
    h, slots, wts, table = _router(x, norm_g, modseg, w_router, b_router, layer)
    tm = EXPERT_TILE
    n_tiles = n_tok // TOKEN_TILE
    n_rows = n_tok * TOP_K + n_tiles * N_EXPERTS * ROW_BLOCK + N_EXPERTS * tm
    n_rows = -(-n_rows // tm) * tm
    local_start = table[:, 0, :N_EXPERTS]
    padded = table[:, 1, :N_EXPERTS]
    earlier = table[:, 2, :N_EXPERTS]
    used = earlier[-1] + padded[-1]
    size = ((used + tm - 1) // tm) * tm
    ends = _prefix_sum(size)
    starts = ends - size
    sorted_start = starts[None, :] + earlier
    meta = _copy_lists(local_start, padded, sorted_start)
    tails = jnp.concatenate([starts + used, (size - used) // ROW_BLOCK,
                             ends[-1:], (n_rows - ends[-1:]) // tm,
                             jnp.zeros((LANES - 2 * N_EXPERTS - 2,), jnp.int32)])
    tile_row0 = jnp.arange(n_rows // tm, dtype=jnp.int32) * tm
    tile_expert = jnp.minimum(jnp.sum((ends[None, :] <= tile_row0[:, None]).astype(jnp.int32), axis=1),
                              N_EXPERTS - 1)
    n_active = (ends[-1:] // tm).astype(jnp.int32)
    xs = _dispatch(h, slots, meta, tails, n_rows)
    ys = _experts(xs, tile_expert, ends, n_active, w_gu, b_gu, w_down, b_down, layer)
    return _combine(ys, slots, meta, wts, x, modseg, split_tokens)


def _rope_tables(n_tok):
    rows = n_tok // GRID_W
    row = jnp.repeat(jnp.arange(rows, dtype=F32), GRID_W)
    col = jnp.tile(jnp.arange(GRID_W, dtype=F32), rows)
    quarter = HEAD_DIM // 4
    inv_freq = ROPE_BASE ** (-jnp.arange(quarter, dtype=F32) / quarter)
    ang_r = row[:, None] * inv_freq
    ang_c = col[:, None] * inv_freq
    ang = jnp.concatenate([ang_r, ang_r, ang_c, ang_c], axis=-1)
    cos, sin = jnp.cos(ang), jnp.sin(ang)
    first = (jnp.arange(HEAD_DIM) % (2 * quarter)) < quarter
    sin_a = jnp.where(first, -sin, 0.0)
    sin_b = jnp.where(first, 0.0, sin)
    return tuple(jnp.tile(t, (1, LANES // HEAD_DIM)) for t in (cos, sin_a, sin_b))


def kernel(x_prompt, x_sample, c, cache_k, cache_v, state_rglru, state_hgrn, c_ctx, w_mod, b_mod, norm1_g, norm2_g, w_in, q_norm_g, k_norm_g, conv_w, conv_b, rg_wa, rg_ba, rg_wx, rg_bx, rg_lambda, hgrn_lb_logits, hgrn_norm_g, w_attn_o, w_rnn_o, w_hgrn_o, w_out, w_router, b_router, w_gu, b_gu, w_down, b_down):
    depth = w_mod.shape[0]
    n_p, t_p = x_prompt.shape[0], x_prompt.shape[1]
    n_s, t_s = x_sample.shape[0], x_sample.shape[1]
    tok_p = n_p * t_p
    tok_s = n_s * t_s
    assert tok_p % SEG_ROWS == 0 and t_s == SEG_ROWS and SEG_ROWS % t_p == 0

    x = jnp.concatenate([x_prompt.reshape(tok_p, D_MODEL), x_sample.reshape(tok_s, D_MODEL)], axis=0)

    lb_all = jnp.moveaxis(_prefix_sum(jnp.moveaxis(jax.nn.softmax(hgrn_lb_logits.astype(F32), axis=0), 0, -1)),
                          -1, 0)
    lb_all = lb_all - lb_all[:1]
    rope = _rope_tables(t_s)

    assert 1 + n_s <= ROW_BLOCK
    cond8 = jnp.zeros((ROW_BLOCK, D_MODEL), F32).at[0].set(c_ctx).at[1:1 + n_s].set(c)
    mod = _modulation(cond8, w_mod, b_mod)
    seg_ids = jnp.asarray([0] * (tok_p // SEG_ROWS) + [1 + b for b in range(n_s)], jnp.int32)
    modsegs = mod[:, seg_ids].reshape(depth, seg_ids.shape[0], N_MOD, D_MODEL)

    b_gate = jnp.stack([rg_ba, rg_bx], axis=2).reshape(depth, 1, 4 * D_RNN)

    new_k, new_v, new_hr = [], [], []
    new_s = None
    for l in range(depth):
        modseg = modsegs[l]
        z = _input_projection(x, norm1_g, modseg, w_in, l)
        attn_p, k_l, v_l = _attention(z, q_norm_g, k_norm_g, l, n_p, t_p, 0)
        (attn_s,) = _attention(z, q_norm_g, k_norm_g, l, n_s, t_s, tok_p, rope=rope,
                               cache=(cache_k, cache_v))
        rnn_p, hr_l = _rglru(z, conv_w, conv_b, rg_wa, rg_wx, b_gate, rg_lambda, l, n_p, t_p, 0)
        (rnn_s,) = _rglru(z, conv_w, conv_b, rg_wa, rg_wx, b_gate, rg_lambda, l, n_s, t_s, tok_p,
                          state=state_rglru)
        hg_p, new_s = _hgrn(z, lb_all, hgrn_norm_g, l, n_p, t_p, 0, earlier=new_s)
        (hg_s,) = _hgrn(z, lb_all, hgrn_norm_g, l, n_s, t_s, tok_p, state=state_hgrn)
        x = _merge_project(((attn_p, attn_s), (rnn_p, rnn_s), (hg_p, hg_s)), z, x, modseg,
                           w_attn_o, w_rnn_o, w_hgrn_o, w_out, l)
        x = _moe(x, norm2_g, modseg, w_router, b_router, w_gu, b_gu, w_down, b_down, l,
                 split_tokens=tok_p if l == depth - 1 else None)
        new_k.append(k_l.reshape(n_p, t_p, N_KV_HEADS, HEAD_DIM))
        new_v.append(v_l.reshape(n_p, t_p, N_KV_HEADS, HEAD_DIM))
        new_hr.append(hr_l)

    y_prompt = x[0].reshape(n_p, t_p, D_MODEL)
    y_sample = x[1].reshape(n_s, t_s, D_MODEL)
    return (y_prompt, y_sample, jnp.stack(new_k, axis=1), jnp.stack(new_v, axis=1),
            jnp.stack(new_hr, axis=1), new_s)
```
